```python
import math
import jax, jax.numpy as jnp
from jax import lax
import numpy as np

D_MODEL = 2048
BATCH = 4
SEQ = 2048
DEPTH = 1

CHUNK = 64
Q_BLOCK = 128
MIX_WIDTH = D_MODEL
ATTN_WIDTH = MIX_WIDTH // 2
RET_WIDTH = MIX_WIDTH - ATTN_WIDTH
ATTN_HEADS = 8
HEAD_DIM = ATTN_WIDTH // ATTN_HEADS
KV_HEADS = 2
KV_WIDTH = KV_HEADS * HEAD_DIM
IDX_HEADS = 16
IDX_DIM = 64
TOPK_MAX = 256
RET_HEADS = 8
RET_DK = RET_WIDTH // RET_HEADS
RET_DV = RET_WIDTH // RET_HEADS
ROT_BASE = 10000.0
N_GROUPS = 4
EXPERTS_PER_GROUP = 8
N_EXPERTS = N_GROUPS * EXPERTS_PER_GROUP
TOP_EXPERTS = 2
EXPERT_FF = D_MODEL // 4
EPS = 1e-6

IN_SPLITS = (ATTN_WIDTH, KV_WIDTH, KV_WIDTH,
             IDX_HEADS * IDX_DIM, IDX_DIM, IDX_HEADS,
             RET_WIDTH, RET_WIDTH, RET_WIDTH, RET_WIDTH)
IN_WIDTH = sum(IN_SPLITS)

kernel_name = "hybrid_dsa_retention_hmoe"


def rms_norm(x, g):
    xf = x.astype(jnp.float32)
    y = xf * lax.rsqrt(jnp.mean(xf * xf, axis=-1, keepdims=True) + EPS)
    return (y * g.astype(jnp.float32)).astype(x.dtype)


def layer_norm(x, w, b):
    xf = x.astype(jnp.float32)
    mu = jnp.mean(xf, axis=-1, keepdims=True)
    var = jnp.mean(jnp.square(xf - mu), axis=-1, keepdims=True)
    y = (xf - mu) * lax.rsqrt(var + EPS)
    return (y * w.astype(jnp.float32) + b.astype(jnp.float32)).astype(x.dtype)


def split_columns(a, sizes):
    out, start = [], 0
    for s in sizes:
        out.append(a[..., start:start + s])
        start += s
    return out


def rotate(x, pos):
    half = x.shape[-1] // 2
    inv = 1.0 / (ROT_BASE ** jnp.linspace(0.0, 1.0, half, dtype=jnp.float32))
    ang = pos[:, None] * inv[None, :]
    c = jnp.cos(ang)[None, :, None, :]
    s = jnp.sin(ang)[None, :, None, :]
    x1, x2 = x[..., :half], x[..., half:]
    return jnp.concatenate([x1 * c - x2 * s, x1 * s + x2 * c], axis=-1)


def dsa_attention(q, k, v, qi, ki, wi):
    B, S = q.shape[0], q.shape[1]
    topk = min(TOPK_MAX, S // 4)
    nb = S // Q_BLOCK
    key_chunk = jnp.arange(S) // CHUNK
    ki32 = ki.astype(jnp.float32)

    def to_blocks(a):
        return a.reshape((B, nb, Q_BLOCK) + a.shape[2:]).swapaxes(0, 1)

    def block(args):
        qb, qib, wib, start = args
        q_chunk = (start + jnp.arange(Q_BLOCK)) // CHUNK
        visible = key_chunk[None, :] <= q_chunk[:, None]
        dots = jnp.einsum('bthd,bsd->bths', qib.astype(jnp.float32), ki32)
        score = jnp.einsum('bths,bth->bts', jax.nn.relu(dots), wib.astype(jnp.float32))
        score = jnp.where(visible[None], score, -jnp.inf)
        _, idx = lax.top_k(score, topk)
        valid = key_chunk[idx] <= q_chunk[None, :, None]
        ks = jax.vmap(lambda kb, ib: kb[ib])(k, idx)
        vs = jax.vmap(lambda vb, ib: vb[ib])(v, idx)
        qg = qb.reshape(B, Q_BLOCK, KV_HEADS, ATTN_HEADS // KV_HEADS, HEAD_DIM)
        s = jnp.einsum('btgrd,btkgd->btgrk', qg, ks).astype(jnp.float32) * (HEAD_DIM ** -0.5)
        s = jnp.where(valid[:, :, None, None, :], s, -jnp.inf)
        p = jax.nn.softmax(s, axis=-1).astype(v.dtype)
        o = jnp.einsum('btgrk,btkgd->btgrd', p, vs)
        return o.reshape(B, Q_BLOCK, ATTN_WIDTH)

    starts = jnp.arange(nb) * Q_BLOCK
    out = lax.map(block, (to_blocks(q), to_blocks(qi), to_blocks(wi), starts))
    return out.swapaxes(0, 1).reshape(B, S, ATTN_WIDTH)


def retention(q, k, v):
    B, S, H, dk = q.shape
    dv = v.shape[-1]
    nc = S // CHUNK
    log_gamma = jnp.log1p(-jnp.exp2(-5.0 - jnp.arange(H, dtype=jnp.float32)))
    n = jnp.arange(CHUNK, dtype=jnp.float32)
    rel = n[:, None] - n[None, :]
    intra = jnp.where(rel >= 0, jnp.exp(log_gamma[:, None, None] * jnp.maximum(rel, 0.0)), 0.0)
    cross_decay = jnp.exp(log_gamma[:, None] * (n + 1.0))[..., None]
    state_decay = jnp.exp(log_gamma[:, None] * (CHUNK - 1.0 - n))[..., None]
    chunk_decay = jnp.exp(log_gamma * CHUNK)[:, None, None]

    def to_chunks(a):
        return a.reshape(B, nc, CHUNK, H, a.shape[-1]).transpose(1, 0, 3, 2, 4)

    def step(R, inp):
        qc, kc, vc = inp
        inner = jnp.einsum('bhnd,bhmd->bhnm', qc, kc) * intra
        o = jnp.einsum('bhnm,bhme->bhne', inner, vc) + jnp.einsum('bhnd,bhde->bhne', qc, R) * cross_decay
        R = R * chunk_decay + jnp.einsum('bhmd,bhme->bhde', kc * state_decay, vc)
        return R, o

    R0 = jnp.zeros((B, H, dk, dv), jnp.float32)
    _, o = lax.scan(step, R0, (to_chunks(q), to_chunks(k), to_chunks(v)))
    return o.transpose(1, 0, 3, 2, 4).reshape(B, S, H, dv)


def hier_moe(h, w_group, w_router, w1, w3, w2):
    B, S, D = h.shape
    hf = h.reshape(B * S, D)
    h32 = hf.astype(jnp.float32)
    g_logits = h32 @ w_group.astype(jnp.float32)
    g_prob = jax.nn.softmax(g_logits, axis=-1)
    g_sel = jnp.argmax(g_logits, axis=-1)
    g_gate = jnp.take_along_axis(g_prob, g_sel[:, None], axis=-1)
    e_logits = jnp.einsum('nd,gde->nge', h32, w_router.astype(jnp.float32))
    e_logits = jnp.take_along_axis(e_logits, g_sel[:, None, None], axis=1)[:, 0]
    top_v, top_i = lax.top_k(e_logits, TOP_EXPERTS)
    top_w = jax.nn.softmax(top_v, axis=-1) * g_gate
    expert_id = g_sel[:, None] * EXPERTS_PER_GROUP + top_i
    comb = jnp.sum(jax.nn.one_hot(expert_id, N_EXPERTS, dtype=jnp.float32) * top_w[..., None], axis=1)
    a = jnp.einsum('nd,edf->nef', hf, w1)
    b = jnp.einsum('nd,edf->nef', hf, w3)
    y = jnp.einsum('nef,efd->nd', jax.nn.silu(a) * b * comb[..., None].astype(h.dtype), w2)
    return y.reshape(B, S, D)


def setup_inputs(seed: int = 0) -> dict:
    key = jax.random.key(seed)
    ks = jax.random.split(key, 16)
    f32 = jnp.float32

    def nrm(k, shape, fan_in):
        return jax.random.normal(k, shape, f32) * (fan_in ** -0.5)

    def gain(k, shape):
        return 1.0 + 0.01 * jax.random.normal(k, shape, f32)

    return {
        "x": jax.random.normal(ks[0], (BATCH, SEQ, D_MODEL), f32),
        "norm1_g": gain(ks[1], (DEPTH, D_MODEL)),
        "w_in": nrm(ks[2], (DEPTH, D_MODEL, IN_WIDTH), D_MODEL),
        "q_norm_g": gain(ks[3], (DEPTH, HEAD_DIM)),
        "k_norm_g": gain(ks[4], (DEPTH, HEAD_DIM)),
        "idx_k_ln_w": gain(ks[5], (DEPTH, IDX_DIM)),
        "idx_k_ln_b": 0.01 * jax.random.normal(ks[6], (DEPTH, IDX_DIM), f32),
        "ret_norm_g": gain(ks[7], (DEPTH, RET_WIDTH)),
        "w_out": nrm(ks[8], (DEPTH, MIX_WIDTH, D_MODEL), MIX_WIDTH),
        "norm2_g": gain(ks[9], (DEPTH, D_MODEL)),
        "w_group": nrm(ks[10], (DEPTH, D_MODEL, N_GROUPS), D_MODEL),
        "w_router": nrm(ks[11], (DEPTH, N_GROUPS, D_MODEL, EXPERTS_PER_GROUP), D_MODEL),
        "w1": nrm(ks[12], (DEPTH, N_EXPERTS, D_MODEL, EXPERT_FF), D_MODEL),
        "w3": nrm(ks[13], (DEPTH, N_EXPERTS, D_MODEL, EXPERT_FF), D_MODEL),
        "w2": nrm(ks[14], (DEPTH, N_EXPERTS, EXPERT_FF, D_MODEL), EXPERT_FF),
    }


def reference(x, norm1_g, w_in, q_norm_g, k_norm_g, idx_k_ln_w, idx_k_ln_b, ret_norm_g,
              w_out, norm2_g, w_group, w_router, w1, w3, w2):
    B, S, _ = x.shape
    pos = jnp.arange(S, dtype=jnp.float32)
    idx_w_scale = (IDX_HEADS ** -0.5) * (IDX_DIM ** -0.5)
    for l in range(DEPTH):
        h = rms_norm(x, norm1_g[l])
        proj = h @ w_in[l]
        aq, ak, av, iq, ik, iw, rq, rk, rv, rg = split_columns(proj, IN_SPLITS)

        aq = rms_norm(aq.reshape(B, S, ATTN_HEADS, HEAD_DIM), q_norm_g[l])
        ak = rms_norm(ak.reshape(B, S, KV_HEADS, HEAD_DIM), k_norm_g[l])
        av = av.reshape(B, S, KV_HEADS, HEAD_DIM)
        iq = iq.reshape(B, S, IDX_HEADS, IDX_DIM)
        ik = layer_norm(ik, idx_k_ln_w[l], idx_k_ln_b[l])
        iw = iw * idx_w_scale
        attn_out = dsa_attention(aq, ak, av, iq, ik, iw)

        rq = rotate(rq.reshape(B, S, RET_HEADS, RET_DK).astype(jnp.float32), pos)
        rk = rotate(rk.reshape(B, S, RET_HEADS, RET_DK).astype(jnp.float32), pos) * (RET_DK ** -0.5)
        rv = rv.reshape(B, S, RET_HEADS, RET_DV).astype(jnp.float32)
        ret = retention(rq, rk, rv)
        ret = rms_norm(ret, ret_norm_g[l].reshape(RET_HEADS, RET_DV)).reshape(B, S, RET_WIDTH)
        ret_out = (jax.nn.silu(rg.astype(jnp.float32)) * ret).astype(x.dtype)

        mixed = jnp.concatenate([attn_out.astype(x.dtype), ret_out], axis=-1) @ w_out[l]
        x = x + mixed

        x = x + hier_moe(rms_norm(x, norm2_g[l]), w_group[l], w_router[l], w1[l], w3[l], w2[l])
    return x
```

```python
import functools
import math

import jax
import jax.numpy as jnp
from jax import lax
from jax.experimental import pallas as pl
from jax.experimental.pallas import tpu as pltpu

CHUNK = 64
ATTN_HEADS = 8
HEAD_DIM = 128
KV_HEADS = 2
HEADS_PER_KV = ATTN_HEADS // KV_HEADS
IDX_HEADS = 16
IDX_DIM = 64
TOPK_MAX = 256
RET_HEADS = 8
RET_DK = 128
RET_DV = 128
ROT_BASE = 10000.0
N_GROUPS = 4
EXPERTS_PER_GROUP = 8
N_EXPERTS = N_GROUPS * EXPERTS_PER_GROUP
EPS = 1e-6

ATTN_WIDTH = ATTN_HEADS * HEAD_DIM
KV_WIDTH = KV_HEADS * HEAD_DIM
IDX_WIDTH = IDX_HEADS * IDX_DIM
RET_WIDTH = RET_HEADS * RET_DK

LANES = 128
VMEM_LIMIT = 56 * 1024 * 1024

AQ_OFF = 0
IQ_OFF = AQ_OFF + ATTN_WIDTH
AK_OFF = IQ_OFF + IDX_WIDTH
AV_OFF = AK_OFF + KV_WIDTH
IK_OFF = AV_OFF + KV_WIDTH
IW_OFF = IK_OFF + LANES
RQ_OFF = 3072
RK_OFF = RQ_OFF + RET_WIDTH
RV_OFF = RK_OFF + RET_WIDTH
RG_OFF = RV_OFF + RET_WIDTH
PROJ_WIDTH = RG_OFF + RET_WIDTH

INT_MIN = -(2 ** 31)
NEG_BIG = -1e30

_NT = (((1,), (1,)), ((), ()))


def _dot(a, b):
    return jnp.dot(a, b, preferred_element_type=jnp.float32)


def _dot_nt(a, b):
    return lax.dot_general(a, b, _NT, preferred_element_type=jnp.float32)


def _params(*sem):
    return pltpu.CompilerParams(dimension_semantics=sem, vmem_limit_bytes=VMEM_LIMIT)


def _in_proj_kernel(x_ref, g_ref, w_ref, o_ref, h_scr, *, row_chunk):
    @pl.when(pl.program_id(1) == 0)
    def _():
        def body(c, carry):
            rows = pl.ds(pl.multiple_of(c * row_chunk, row_chunk), row_chunk)
            x = x_ref[rows, :]
            ms = jnp.mean(x * x, axis=-1, keepdims=True)
            h_scr[rows, :] = ((x * lax.rsqrt(ms + EPS)) * g_ref[...]).astype(jnp.bfloat16)
            return carry
        lax.fori_loop(0, x_ref.shape[0] // row_chunk, body, 0)

    o_ref[...] = _dot(h_scr[...], w_ref[...]).astype(o_ref.dtype)


def _in_proj(x2d, g, wp, *, tm, tn):
    n, d = x2d.shape
    pw = wp.shape[1]
    return pl.pallas_call(
        functools.partial(_in_proj_kernel, row_chunk=min(tm, 128)),
        out_shape=jax.ShapeDtypeStruct((n, pw), jnp.bfloat16),
        grid=(n // tm, pw // tn),
        in_specs=[
            pl.BlockSpec((tm, d), lambda i, j: (i, 0)),
            pl.BlockSpec((1, d), lambda i, j: (0, 0)),
            pl.BlockSpec((d, tn), lambda i, j: (0, j)),
        ],
        out_specs=pl.BlockSpec((tm, tn), lambda i, j: (i, j)),
        scratch_shapes=[pltpu.VMEM((tm, d), jnp.bfloat16)],
        compiler_params=_params("arbitrary", "arbitrary"),
        name="in_proj",
    )(x2d, g, wp)


def _sortable_key(score):
    bits = pltpu.bitcast(score, jnp.int32)
    return bits ^ ((bits >> 31) & jnp.int32(0x7FFFFFFF))


def _attn_kernel(aq_ref, iq_ref, iw_ref, ak_ref, av_ref, ik_ref, qg_ref, kg_ref, lnw_ref, lnb_ref,
                 o_ref,
                 kn_scr, ikn_scr, vt_scr, key_scr, iqh_scr, wt_scr, qn_scr, acc_scr, m_scr, l_scr,
                 *, tk, topk, idx_w_scale):
    i = pl.program_id(1)
    seq = ak_ref.shape[1]
    tq = aq_ref.shape[1]
    chunk_shift = CHUNK.bit_length() - 1

    @pl.when(i == 0)
    def _():
        def body(c, carry):
            rows = pl.ds(pl.multiple_of(c * tk, tk), tk)
            for g in range(KV_HEADS):
                cols = slice(g * HEAD_DIM, (g + 1) * HEAD_DIM)
                k = ak_ref[0, rows, cols].astype(jnp.float32)
                ms = jnp.mean(k * k, axis=-1, keepdims=True)
                kn_scr[rows, cols] = ((k * lax.rsqrt(ms + EPS)) * kg_ref[...]).astype(jnp.bfloat16)
                v = av_ref[0, rows, cols].astype(jnp.float32)
                vt_scr[g, c] = v.T.astype(jnp.bfloat16)
            ki = ik_ref[0, rows, :IDX_DIM].astype(jnp.float32)
            mu = jnp.mean(ki, axis=-1, keepdims=True)
            var = jnp.mean(jnp.square(ki - mu), axis=-1, keepdims=True)
            y = (ki - mu) * lax.rsqrt(var + EPS)
            ikn_scr[rows, :] = (y * lnw_ref[...] + lnb_ref[...]).astype(jnp.bfloat16)
            return carry
        lax.fori_loop(0, seq // tk, body, 0)

    t0 = i * tq
    n_kt = (t0 + tq) // tk
    scale = HEAD_DIM ** -0.5
    for h in range(ATTN_HEADS):
        q = aq_ref[0, :, h * HEAD_DIM:(h + 1) * HEAD_DIM].astype(jnp.float32)
        ms = jnp.mean(q * q, axis=-1, keepdims=True)
        qn_scr[h] = ((q * lax.rsqrt(ms + EPS)) * qg_ref[...] * scale).astype(jnp.bfloat16)
    for h in range(IDX_HEADS):
        iqh_scr[h] = iq_ref[0, :, h * IDX_DIM:(h + 1) * IDX_DIM]
    wt_scr[...] = iw_ref[0].astype(jnp.float32).T * idx_w_scale

    q_chunk = (t0 + lax.broadcasted_iota(jnp.int32, (tk, tq), 1)) >> chunk_shift

    def score_body(kt, carry):
        rows = pl.ds(pl.multiple_of(kt * tk, tk), tk)
        ik_t = ikn_scr[rows, :]
        acc = jnp.zeros((tk, tq), jnp.float32)
        for h in range(IDX_HEADS):
            d = _dot_nt(ik_t, iqh_scr[h])
            acc = acc + jnp.maximum(d, 0.0) * wt_scr[h:h + 1, :]
        k_chunk = (kt * tk + lax.broadcasted_iota(jnp.int32, (tk, tq), 0)) >> chunk_shift
        key_scr[rows, :] = jnp.where(k_chunk <= q_chunk, _sortable_key(acc), jnp.int32(INT_MIN))
        return carry
    lax.fori_loop(0, n_kt, score_body, 0)

    def bit_body(it, lo):
        cand = lo + lax.shift_left(jnp.int32(1), 31 - it)

        def count_body(kt, cnt):
            rows = pl.ds(pl.multiple_of(kt * tk, tk), tk)
            hit = jnp.where(key_scr[rows, :] >= cand, 1.0, 0.0)
            return cnt + jnp.sum(hit, axis=0, keepdims=True)
        cnt = lax.fori_loop(0, n_kt, count_body, jnp.zeros((1, tq), jnp.float32))
        return jnp.where(cnt >= float(topk), cand, lo)
    lo = lax.fori_loop(0, 32, bit_body, jnp.full((1, tq), INT_MIN, jnp.int32))
    thr = jnp.maximum(lo, jnp.int32(INT_MIN + 1))

    m_scr[...] = jnp.full(m_scr.shape, NEG_BIG, jnp.float32)
    l_scr[...] = jnp.zeros(l_scr.shape, jnp.float32)
    acc_scr[...] = jnp.zeros(acc_scr.shape, jnp.float32)

    def attn_body(kt, carry):
        rows = pl.ds(pl.multiple_of(kt * tk, tk), tk)
        bias = jnp.where(key_scr[rows, :] >= thr, 0.0, NEG_BIG)
        for g in range(KV_HEADS):
            k_t = kn_scr[rows, g * HEAD_DIM:(g + 1) * HEAD_DIM]
            vt_t = vt_scr[g, kt]
            for r in range(HEADS_PER_KV):
                h = g * HEADS_PER_KV + r
                s = _dot_nt(k_t, qn_scr[h]) + bias
                m_old = m_scr[h:h + 1, :]
                m_new = jnp.maximum(m_old, jnp.max(s, axis=0, keepdims=True))
                p = jnp.exp(s - m_new)
                alpha = jnp.exp(m_old - m_new)
                l_scr[h:h + 1, :] = alpha * l_scr[h:h + 1, :] + jnp.sum(p, axis=0, keepdims=True)
                acc_scr[h] = acc_scr[h] * alpha + _dot(vt_t, p.astype(jnp.bfloat16))
                m_scr[h:h + 1, :] = m_new
        return carry
    lax.fori_loop(0, n_kt, attn_body, 0)

    for h in range(ATTN_HEADS):
        o = acc_scr[h] / l_scr[h:h + 1, :]
        o_ref[0, :, h * HEAD_DIM:(h + 1) * HEAD_DIM] = o.T.astype(o_ref.dtype)


def _dsa_attention(p3, q_g, k_g, ln_w, ln_b, *, tq, tk):
    b, seq, _ = p3.shape
    topk = min(TOPK_MAX, seq // 4)
    idx_w_scale = (IDX_HEADS ** -0.5) * (IDX_DIM ** -0.5)
    assert seq % tq == 0 and tq % tk == 0 and tk % CHUNK == 0

    def col(off, width):
        return off // width

    return pl.pallas_call(
        functools.partial(_attn_kernel, tk=tk, topk=topk, idx_w_scale=idx_w_scale),
        out_shape=jax.ShapeDtypeStruct((b, seq, ATTN_WIDTH), jnp.bfloat16),
        grid=(b, seq // tq),
        in_specs=[
            pl.BlockSpec((1, tq, ATTN_WIDTH), lambda bi, i: (bi, i, col(AQ_OFF, ATTN_WIDTH))),
            pl.BlockSpec((1, tq, IDX_WIDTH), lambda bi, i: (bi, i, col(IQ_OFF, IDX_WIDTH))),
            pl.BlockSpec((1, tq, LANES), lambda bi, i: (bi, i, col(IW_OFF, LANES))),
            pl.BlockSpec((1, seq, KV_WIDTH), lambda bi, i: (bi, 0, col(AK_OFF, KV_WIDTH))),
            pl.BlockSpec((1, seq, KV_WIDTH), lambda bi, i: (bi, 0, col(AV_OFF, KV_WIDTH))),
            pl.BlockSpec((1, seq, LANES), lambda bi, i: (bi, 0, col(IK_OFF, LANES))),
            pl.BlockSpec((1, HEAD_DIM), lambda bi, i: (0, 0)),
            pl.BlockSpec((1, HEAD_DIM), lambda bi, i: (0, 0)),
            pl.BlockSpec((1, IDX_DIM), lambda bi, i: (0, 0)),
            pl.BlockSpec((1, IDX_DIM), lambda bi, i: (0, 0)),
        ],
        out_specs=pl.BlockSpec((1, tq, ATTN_WIDTH), lambda bi, i: (bi, i, 0)),
        scratch_shapes=[
            pltpu.VMEM((seq, KV_WIDTH), jnp.bfloat16),
            pltpu.VMEM((seq, IDX_DIM), jnp.bfloat16),
            pltpu.VMEM((KV_HEADS, seq // tk, HEAD_DIM, tk), jnp.bfloat16),
            pltpu.VMEM((seq, tq), jnp.int32),
            pltpu.VMEM((IDX_HEADS, tq, IDX_DIM), jnp.bfloat16),
            pltpu.VMEM((LANES, tq), jnp.float32),
            pltpu.VMEM((ATTN_HEADS, tq, HEAD_DIM), jnp.bfloat16),
            pltpu.VMEM((ATTN_HEADS, HEAD_DIM, tq), jnp.float32),
            pltpu.VMEM((ATTN_HEADS, tq), jnp.float32),
            pltpu.VMEM((ATTN_HEADS, tq), jnp.float32),
        ],
        compiler_params=_params("arbitrary", "arbitrary"),
        name="dsa_attn",
    )(p3, p3, p3, p3, p3, p3, q_g, k_g, ln_w, ln_b)


def _ret_kernel(lg_ref, rq_ref, rk_ref, rv_ref, rg_ref, cos_ref, sin_ref, g_ref, o_ref, *, rc):
    h = pl.program_id(1)
    lg = lg_ref[h]
    seq = rq_ref.shape[1]
    n = lax.broadcasted_iota(jnp.int32, (rc, RET_DV), 0).astype(jnp.float32)
    cross_decay = jnp.exp(lg * (n + 1.0))
    state_decay = jnp.exp(lg * (rc - 1.0 - n))
    chunk_decay = jnp.exp(lg * jnp.full((RET_DK, RET_DV), float(rc), jnp.float32))
    rel = (lax.broadcasted_iota(jnp.int32, (rc, rc), 0)
           - lax.broadcasted_iota(jnp.int32, (rc, rc), 1)).astype(jnp.float32)
    intra = jnp.where(rel >= 0, jnp.exp(lg * jnp.maximum(rel, 0.0)), 0.0)

    def rot(x, rows):
        return x * cos_ref[rows, :] + pltpu.roll(x, RET_DK // 2, 1) * sin_ref[rows, :]

    state = jnp.zeros((RET_DK, RET_DV), jnp.float32)
    for c in range(seq // rc):
        rows = slice(c * rc, (c + 1) * rc)
        q = rot(rq_ref[0, rows, :].astype(jnp.float32), rows)
        k = rot(rk_ref[0, rows, :].astype(jnp.float32), rows) * (RET_DK ** -0.5)
        v = rv_ref[0, rows, :]
        qb = q.astype(jnp.bfloat16)
        inner = _dot_nt(qb, k.astype(jnp.bfloat16)) * intra
        o = _dot(inner.astype(jnp.bfloat16), v) + _dot(qb, state.astype(jnp.bfloat16)) * cross_decay
        kd_t = (k * state_decay).T.astype(jnp.bfloat16)
        state = state * chunk_decay + _dot(kd_t, v)
        ms = jnp.mean(o * o, axis=-1, keepdims=True)
        y = (o * lax.rsqrt(ms + EPS)) * g_ref[0]
        gate = rg_ref[0, rows, :].astype(jnp.float32)
        o_ref[0, rows, :] = ((gate * (1.0 / (1.0 + jnp.exp(-gate)))) * y).astype(o_ref.dtype)


def _retention(p3, log_gamma, cos2, sin2, ret_g, *, rc):
    b, seq, _ = p3.shape
    assert seq % rc == 0

    def head_spec(off):
        return pl.BlockSpec((1, seq, RET_DK), lambda bi, h: (bi, 0, off // RET_DK + h))

    return pl.pallas_call(
        functools.partial(_ret_kernel, rc=rc),
        out_shape=jax.ShapeDtypeStruct((b, seq, RET_WIDTH), jnp.bfloat16),
        grid=(b, RET_HEADS),
        in_specs=[
            pl.BlockSpec(memory_space=pltpu.SMEM),
            head_spec(RQ_OFF), head_spec(RK_OFF), head_spec(RV_OFF), head_spec(RG_OFF),
            pl.BlockSpec((seq, RET_DK), lambda bi, h: (0, 0)),
            pl.BlockSpec((seq, RET_DK), lambda bi, h: (0, 0)),
            pl.BlockSpec((1, 1, RET_DV), lambda bi, h: (h, 0, 0)),
        ],
        out_specs=pl.BlockSpec((1, seq, RET_DV), lambda bi, h: (bi, 0, h)),
        compiler_params=_params("arbitrary", "arbitrary"),
        name="retention",
    )(log_gamma, p3, p3, p3, p3, cos2, sin2, ret_g)


def _routing(logits):
    lane = lax.broadcasted_iota(jnp.int32, logits.shape, 1).astype(jnp.float32)
    big = float(LANES)
    neg = -jnp.inf

    def first_argmax(v, vmax):
        return jnp.min(jnp.where(v == vmax, lane, big), axis=-1, keepdims=True)

    g_mask = (lane >= N_EXPERTS) & (lane < N_EXPERTS + N_GROUPS)
    gl = jnp.where(g_mask, logits, neg)
    g_max = jnp.max(gl, axis=-1, keepdims=True)
    g_sel = first_argmax(gl, g_max) - N_EXPERTS
    g_gate = 1.0 / jnp.sum(jnp.where(g_mask, jnp.exp(gl - g_max), 0.0), axis=-1, keepdims=True)

    e_lo = g_sel * EXPERTS_PER_GROUP
    el = jnp.where((lane >= e_lo) & (lane < e_lo + EXPERTS_PER_GROUP), logits, neg)
    v1 = jnp.max(el, axis=-1, keepdims=True)
    i1 = first_argmax(el, v1)
    el2 = jnp.where(lane == i1, neg, el)
    v2 = jnp.max(el2, axis=-1, keepdims=True)
    i2 = first_argmax(el2, v2)
    e2 = jnp.exp(v2 - v1)
    denom = 1.0 + e2
    w1 = (1.0 / denom) * g_gate
    w2 = (e2 / denom) * g_gate
    return jnp.where(lane == i1, w1, 0.0) + jnp.where(lane == i2, w2, 0.0)


def _out_proj_kernel(a_ref, r_ref, x_ref, wa_ref, wr_ref, g_ref, rhi_ref, rlo_ref,
                     x1_ref, h2_ref, comb_ref):
    mixed = _dot(a_ref[...], wa_ref[...]) + _dot(r_ref[...], wr_ref[...])
    x1 = x_ref[...] + mixed
    x1_ref[...] = x1
    ms = jnp.mean(x1 * x1, axis=-1, keepdims=True)
    h2 = (x1 * lax.rsqrt(ms + EPS)) * g_ref[...]
    hi = h2.astype(jnp.bfloat16)
    h2_ref[...] = hi
    lo = (h2 - hi.astype(jnp.float32)).astype(jnp.bfloat16)
    logits = _dot(hi, rhi_ref[...]) + (_dot(hi, rlo_ref[...]) + _dot(lo, rhi_ref[...]))
    comb_ref[...] = _routing(logits)


def _out_proj(attn2d, ret2d, x2d, w_out_bf, g2, r_hi, r_lo, *, tm):
    n, d = x2d.shape
    return pl.pallas_call(
        _out_proj_kernel,
        out_shape=(
            jax.ShapeDtypeStruct((n, d), jnp.float32),
            jax.ShapeDtypeStruct((n, d), jnp.bfloat16),
            jax.ShapeDtypeStruct((n, LANES), jnp.float32),
        ),
        grid=(n // tm,),
        in_specs=[
            pl.BlockSpec((tm, ATTN_WIDTH), lambda i: (i, 0)),
            pl.BlockSpec((tm, RET_WIDTH), lambda i: (i, 0)),
            pl.BlockSpec((tm, d), lambda i: (i, 0)),
            pl.BlockSpec((ATTN_WIDTH, d), lambda i: (0, 0)),
            pl.BlockSpec((RET_WIDTH, d), lambda i: (ATTN_WIDTH // RET_WIDTH, 0)),
            pl.BlockSpec((1, d), lambda i: (0, 0)),
            pl.BlockSpec((d, LANES), lambda i: (0, 0)),
            pl.BlockSpec((d, LANES), lambda i: (0, 0)),
        ],
        out_specs=(
            pl.BlockSpec((tm, d), lambda i: (i, 0)),
            pl.BlockSpec((tm, d), lambda i: (i, 0)),
            pl.BlockSpec((tm, LANES), lambda i: (i, 0)),
        ),
        compiler_params=_params("arbitrary"),
        name="out_proj",
    )(attn2d, ret2d, x2d, w_out_bf, w_out_bf, g2, r_hi, r_lo)


def _moe_kernel(h_ref, comb_ref, x1_ref, w1_ref, w3_ref, w2_ref, o_ref):
    e = pl.program_id(1)

    @pl.when(e == 0)
    def _():
        o_ref[...] = x1_ref[...]

    h = h_ref[...]
    a = _dot(h, w1_ref[0].astype(jnp.bfloat16))
    b = _dot(h, w3_ref[0].astype(jnp.bfloat16))
    lane = lax.broadcasted_iota(jnp.int32, comb_ref.shape, 1)
    c = jnp.sum(jnp.where(lane == e, comb_ref[...], 0.0), axis=-1, keepdims=True)
    act = (a * (1.0 / (1.0 + jnp.exp(-a)))) * b * c
    o_ref[...] += _dot(act.astype(jnp.bfloat16), w2_ref[0].astype(jnp.bfloat16))


def _moe(h2, comb, x1, w1, w3, w2, *, tm):
    n, d = x1.shape
    ne, _, ff = w1.shape
    return pl.pallas_call(
        _moe_kernel,
        out_shape=jax.ShapeDtypeStruct((n, d), jnp.float32),
        grid=(n // tm, ne),
        in_specs=[
            pl.BlockSpec((tm, d), lambda i, e: (i, 0)),
            pl.BlockSpec((tm, LANES), lambda i, e: (i, 0)),
            pl.BlockSpec((tm, d), lambda i, e: (i, 0)),
            pl.BlockSpec((1, d, ff), lambda i, e: (e, 0, 0)),
            pl.BlockSpec((1, d, ff), lambda i, e: (e, 0, 0)),
            pl.BlockSpec((1, ff, d), lambda i, e: (e, 0, 0)),
        ],
        out_specs=pl.BlockSpec((tm, d), lambda i, e: (i, 0)),
        compiler_params=_params("arbitrary", "arbitrary"),
        name="moe",
    )(h2, comb, x1, w1, w3, w2)


def _pack_w_in(w_in):
    d = w_in.shape[0]
    sizes = (ATTN_WIDTH, KV_WIDTH, KV_WIDTH, IDX_WIDTH, IDX_DIM, IDX_HEADS,
             RET_WIDTH, RET_WIDTH, RET_WIDTH, RET_WIDTH)
    parts, start = [], 0
    for s in sizes:
        parts.append(w_in[:, start:start + s])
        start += s
    aq, ak, av, iq, ik, iw, rq, rk, rv, rg = parts

    def z(width):
        return jnp.zeros((d, width), w_in.dtype)

    packed = jnp.concatenate(
        [aq, iq, ak, av, ik, z(LANES - IDX_DIM), iw, z(LANES - IDX_HEADS), z(RQ_OFF - IW_OFF - LANES),
         rq, rk, rv, rg], axis=1)
    assert packed.shape[1] == PROJ_WIDTH
    return packed.astype(jnp.bfloat16)


def _pack_router(w_group, w_router):
    d = w_group.shape[0]
    experts = jnp.transpose(w_router, (1, 0, 2)).reshape(d, N_EXPERTS)
    wr = jnp.concatenate(
        [experts, w_group, jnp.zeros((d, LANES - N_EXPERTS - N_GROUPS), w_group.dtype)], axis=1)
    hi = wr.astype(jnp.bfloat16)
    lo = (wr - hi.astype(jnp.float32)).astype(jnp.bfloat16)
    return hi, lo


def _rotation_tables(seq):
    half = RET_DK // 2
    pos = jnp.arange(seq, dtype=jnp.float32)
    inv = 1.0 / (ROT_BASE ** jnp.linspace(0.0, 1.0, half, dtype=jnp.float32))
    ang = pos[:, None] * inv[None, :]
    c, s = jnp.cos(ang), jnp.sin(ang)
    return jnp.concatenate([c, c], axis=-1), jnp.concatenate([-s, s], axis=-1)


def _tiles(n, seq):
    def fit(total, want):
        t = min(total, want)
        while total % t:
            t //= 2
        return t
    return dict(
        proj_tm=fit(n, 1024), proj_tn=1024,
        attn_tq=fit(seq, 256), attn_tk=fit(seq, 256),
        ret_rc=fit(seq, 256),
        out_tm=fit(n, 256),
        moe_tm=fit(n, 512),
    )


def kernel(x, norm1_g, w_in, q_norm_g, k_norm_g, idx_k_ln_w, idx_k_ln_b, ret_norm_g,
           w_out, norm2_g, w_group, w_router, w1, w3, w2):
    b, seq, d = x.shape
    n = b * seq
    depth = w_in.shape[0]
    t = _tiles(n, seq)
    cos2, sin2 = _rotation_tables(seq)
    log_gamma = jnp.log1p(-jnp.exp2(-5.0 - jnp.arange(RET_HEADS, dtype=jnp.float32)))

    x2d = x.reshape(n, d)
    for l in range(depth):
        proj = _in_proj(x2d, norm1_g[l][None, :], _pack_w_in(w_in[l]),
                        tm=t["proj_tm"], tn=t["proj_tn"])
        p3 = proj.reshape(b, seq, PROJ_WIDTH)
        attn = _dsa_attention(p3, q_norm_g[l][None, :], k_norm_g[l][None, :],
                              idx_k_ln_w[l][None, :], idx_k_ln_b[l][None, :],
                              tq=t["attn_tq"], tk=t["attn_tk"])
        ret = _retention(p3, log_gamma, cos2, sin2, ret_norm_g[l].reshape(RET_HEADS, 1, RET_DV),
                         rc=t["ret_rc"])
        r_hi, r_lo = _pack_router(w_group[l], w_router[l])
        x1, h2, comb = _out_proj(attn.reshape(n, ATTN_WIDTH), ret.reshape(n, RET_WIDTH), x2d,
                                 w_out[l].astype(jnp.bfloat16), norm2_g[l][None, :], r_hi, r_lo,
                                 tm=t["out_tm"])
        x2d = _moe(h2, comb, x1, w1[l], w3[l], w2[l], tm=t["moe_tm"])
    return x2d.reshape(b, seq, d)
```

```python
import functools
import math

import jax
import jax.numpy as jnp
from jax import lax
from jax.experimental import pallas as pl
from jax.experimental.pallas import tpu as pltpu

CHUNK = 64
ATTN_HEADS = 8
HEAD_DIM = 128
KV_HEADS = 2
HEADS_PER_KV = ATTN_HEADS // KV_HEADS
IDX_HEADS = 16
IDX_DIM = 64
TOPK_MAX = 256
RET_HEADS = 8
RET_DK = 128
RET_DV = 128
ROT_BASE = 10000.0
N_GROUPS = 4
EXPERTS_PER_GROUP = 8
N_EXPERTS = N_GROUPS * EXPERTS_PER_GROUP
EPS = 1e-6

ATTN_WIDTH = ATTN_HEADS * HEAD_DIM
KV_WIDTH = KV_HEADS * HEAD_DIM
IDX_WIDTH = IDX_HEADS * IDX_DIM
RET_WIDTH = RET_HEADS * RET_DK

LANES = 128
VMEM_LIMIT = 56 * 1024 * 1024

AQ_OFF = 0
IQ_OFF = AQ_OFF + ATTN_WIDTH
AK_OFF = IQ_OFF + IDX_WIDTH
AV_OFF = AK_OFF + KV_WIDTH
IK_OFF = AV_OFF + KV_WIDTH
IW_OFF = IK_OFF + LANES
RQ_OFF = 3072
RK_OFF = RQ_OFF + RET_WIDTH
RV_OFF = RK_OFF + RET_WIDTH
RG_OFF = RV_OFF + RET_WIDTH
PROJ_WIDTH = RG_OFF + RET_WIDTH

ROUTE_E1, ROUTE_E2, ROUTE_W1, ROUTE_W2 = 0, 1, 2, 3

INT_MIN = -(2 ** 31)
NEG_BIG = -1e30

_NT = (((1,), (1,)), ((), ()))


def _dot(a, b):
    return jnp.dot(a, b, preferred_element_type=jnp.float32)


def _dot_nt(a, b):
    return lax.dot_general(a, b, _NT, preferred_element_type=jnp.float32)


def _params(*sem):
    return pltpu.CompilerParams(dimension_semantics=sem, vmem_limit_bytes=VMEM_LIMIT)


def _in_proj_kernel(x_ref, g_ref, w_ref, o_ref, h_scr, *, row_chunk):
    @pl.when(pl.program_id(1) == 0)
    def _():
        def body(c, carry):
            rows = pl.ds(pl.multiple_of(c * row_chunk, row_chunk), row_chunk)
            x = x_ref[rows, :]
            ms = jnp.mean(x * x, axis=-1, keepdims=True)
            h_scr[rows, :] = ((x * lax.rsqrt(ms + EPS)) * g_ref[...]).astype(jnp.bfloat16)
            return carry
        lax.fori_loop(0, x_ref.shape[0] // row_chunk, body, 0)

    o_ref[...] = _dot(h_scr[...], w_ref[...]).astype(o_ref.dtype)


def _in_proj(x2d, g, wp, *, tm, tn):
    n, d = x2d.shape
    pw = wp.shape[1]
    return pl.pallas_call(
        functools.partial(_in_proj_kernel, row_chunk=min(tm, 128)),
        out_shape=jax.ShapeDtypeStruct((n, pw), jnp.bfloat16),
        grid=(n // tm, pw // tn),
        in_specs=[
            pl.BlockSpec((tm, d), lambda i, j: (i, 0)),
            pl.BlockSpec((1, d), lambda i, j: (0, 0)),
            pl.BlockSpec((d, tn), lambda i, j: (0, j)),
        ],
        out_specs=pl.BlockSpec((tm, tn), lambda i, j: (i, j)),
        scratch_shapes=[pltpu.VMEM((tm, d), jnp.bfloat16)],
        compiler_params=_params("arbitrary", "arbitrary"),
        name="in_proj",
    )(x2d, g, wp)


def _ordered_float(v):
    bits = v ^ ((v >> 31) & jnp.int32(0x7FFFFFFF))
    return pltpu.bitcast(bits, jnp.float32)


def _attn_kernel(aq_ref, iq_ref, iw_ref, ak_ref, av_ref, ik_ref, qg_ref, kg_ref, lnw_ref, lnb_ref,
                 o_ref,
                 kn_scr, ikn_scr, vt_scr, key_scr, iqh_scr, wt_scr, qn_scr, acc_scr, m_scr, l_scr,
                 *, tk, topk, idx_w_scale):
    i = pl.program_id(1)
    seq = ak_ref.shape[1]
    tq = aq_ref.shape[1]
    chunk_shift = CHUNK.bit_length() - 1

    @pl.when(i == 0)
    def _():
        def body(c, carry):
            rows = pl.ds(pl.multiple_of(c * tk, tk), tk)
            for g in range(KV_HEADS):
                cols = slice(g * HEAD_DIM, (g + 1) * HEAD_DIM)
                k = ak_ref[0, rows, cols].astype(jnp.float32)
                ms = jnp.mean(k * k, axis=-1, keepdims=True)
                kn_scr[rows, cols] = ((k * lax.rsqrt(ms + EPS)) * kg_ref[...]).astype(jnp.bfloat16)
                v = av_ref[0, rows, cols].astype(jnp.float32)
                vt_scr[g, c] = v.T.astype(jnp.bfloat16)
            ki = ik_ref[0, rows, :IDX_DIM].astype(jnp.float32)
            mu = jnp.mean(ki, axis=-1, keepdims=True)
            var = jnp.mean(jnp.square(ki - mu), axis=-1, keepdims=True)
            y = (ki - mu) * lax.rsqrt(var + EPS)
            ikn_scr[rows, :] = (y * lnw_ref[...] + lnb_ref[...]).astype(jnp.bfloat16)
            return carry
        lax.fori_loop(0, seq // tk, body, 0)

    t0 = i * tq
    n_kt = (t0 + tq) // tk
    scale = HEAD_DIM ** -0.5
    for h in range(ATTN_HEADS):
        q = aq_ref[0, :, h * HEAD_DIM:(h + 1) * HEAD_DIM].astype(jnp.float32)
        ms = jnp.mean(q * q, axis=-1, keepdims=True)
        qn_scr[h] = ((q * lax.rsqrt(ms + EPS)) * qg_ref[...] * scale).astype(jnp.bfloat16)
    for h in range(IDX_HEADS):
        iqh_scr[h] = iq_ref[0, :, h * IDX_DIM:(h + 1) * IDX_DIM]
    wt_scr[...] = iw_ref[0].astype(jnp.float32).T * idx_w_scale

    q_chunk = (t0 + lax.broadcasted_iota(jnp.int32, (tk, tq), 1)) >> chunk_shift

    def score_body(kt, carry):
        rows = pl.ds(pl.multiple_of(kt * tk, tk), tk)
        ik_t = ikn_scr[rows, :]
        acc = jnp.zeros((tk, tq), jnp.float32)
        for h in range(IDX_HEADS):
            d = _dot_nt(ik_t, iqh_scr[h])
            acc = acc + jnp.maximum(d, 0.0) * wt_scr[h:h + 1, :]
        k_chunk = (kt * tk + lax.broadcasted_iota(jnp.int32, (tk, tq), 0)) >> chunk_shift
        key_scr[rows, :] = jnp.where(k_chunk <= q_chunk, acc, -jnp.inf)
        return carry
    lax.fori_loop(0, n_kt, score_body, 0)

    def bit_body(it, lo):
        cand = lo + lax.shift_left(jnp.int32(1), 31 - it)
        cand_f = _ordered_float(cand)

        def count_body(kt, cnt):
            rows = pl.ds(pl.multiple_of(kt * tk, tk), tk)
            hit = jnp.where(key_scr[rows, :] >= cand_f, 1.0, 0.0)
            return cnt + jnp.sum(hit, axis=0, keepdims=True)
        cnt = lax.fori_loop(0, n_kt, count_body, jnp.zeros((1, tq), jnp.float32))
        return jnp.where(cnt >= float(topk), cand, lo)
    lo = lax.fori_loop(0, 32, bit_body, jnp.full((1, tq), INT_MIN, jnp.int32))
    thr = jnp.where(lo == INT_MIN, jnp.finfo(jnp.float32).min, _ordered_float(lo))

    m_scr[...] = jnp.full(m_scr.shape, NEG_BIG, jnp.float32)
    l_scr[...] = jnp.zeros(l_scr.shape, jnp.float32)
    acc_scr[...] = jnp.zeros(acc_scr.shape, jnp.float32)

    def attn_body(kt, carry):
        rows = pl.ds(pl.multiple_of(kt * tk, tk), tk)
        bias = jnp.where(key_scr[rows, :] >= thr, 0.0, NEG_BIG)
        for g in range(KV_HEADS):
            k_t = kn_scr[rows, g * HEAD_DIM:(g + 1) * HEAD_DIM]
            vt_t = vt_scr[g, kt]
            for r in range(HEADS_PER_KV):
                h = g * HEADS_PER_KV + r
                s = _dot_nt(k_t, qn_scr[h]) + bias
                m_old = m_scr[h:h + 1, :]
                m_new = jnp.maximum(m_old, jnp.max(s, axis=0, keepdims=True))
                p = jnp.exp(s - m_new)
                alpha = jnp.exp(m_old - m_new)
                l_scr[h:h + 1, :] = alpha * l_scr[h:h + 1, :] + jnp.sum(p, axis=0, keepdims=True)
                acc_scr[h] = acc_scr[h] * alpha + _dot(vt_t, p.astype(jnp.bfloat16))
                m_scr[h:h + 1, :] = m_new
        return carry
    lax.fori_loop(0, n_kt, attn_body, 0)

    for h in range(ATTN_HEADS):
        o = acc_scr[h] / l_scr[h:h + 1, :]
        o_ref[0, :, h * HEAD_DIM:(h + 1) * HEAD_DIM] = o.T.astype(o_ref.dtype)


def _dsa_attention(p3, q_g, k_g, ln_w, ln_b, *, tq, tk):
    b, seq, _ = p3.shape
    topk = min(TOPK_MAX, seq // 4)
    idx_w_scale = (IDX_HEADS ** -0.5) * (IDX_DIM ** -0.5)
    assert seq % tq == 0 and tq % tk == 0 and tk % CHUNK == 0

    def col(off, width):
        return off // width

    return pl.pallas_call(
        functools.partial(_attn_kernel, tk=tk, topk=topk, idx_w_scale=idx_w_scale),
        out_shape=jax.ShapeDtypeStruct((b, seq, ATTN_WIDTH), jnp.bfloat16),
        grid=(b, seq // tq),
        in_specs=[
            pl.BlockSpec((1, tq, ATTN_WIDTH), lambda bi, i: (bi, i, col(AQ_OFF, ATTN_WIDTH))),
            pl.BlockSpec((1, tq, IDX_WIDTH), lambda bi, i: (bi, i, col(IQ_OFF, IDX_WIDTH))),
            pl.BlockSpec((1, tq, LANES), lambda bi, i: (bi, i, col(IW_OFF, LANES))),
            pl.BlockSpec((1, seq, KV_WIDTH), lambda bi, i: (bi, 0, col(AK_OFF, KV_WIDTH))),
            pl.BlockSpec((1, seq, KV_WIDTH), lambda bi, i: (bi, 0, col(AV_OFF, KV_WIDTH))),
            pl.BlockSpec((1, seq, LANES), lambda bi, i: (bi, 0, col(IK_OFF, LANES))),
            pl.BlockSpec((1, HEAD_DIM), lambda bi, i: (0, 0)),
            pl.BlockSpec((1, HEAD_DIM), lambda bi, i: (0, 0)),
            pl.BlockSpec((1, IDX_DIM), lambda bi, i: (0, 0)),
            pl.BlockSpec((1, IDX_DIM), lambda bi, i: (0, 0)),
        ],
        out_specs=pl.BlockSpec((1, tq, ATTN_WIDTH), lambda bi, i: (bi, i, 0)),
        scratch_shapes=[
            pltpu.VMEM((seq, KV_WIDTH), jnp.bfloat16),
            pltpu.VMEM((seq, IDX_DIM), jnp.bfloat16),
            pltpu.VMEM((KV_HEADS, seq // tk, HEAD_DIM, tk), jnp.bfloat16),
            pltpu.VMEM((seq, tq), jnp.float32),
            pltpu.VMEM((IDX_HEADS, tq, IDX_DIM), jnp.bfloat16),
            pltpu.VMEM((LANES, tq), jnp.float32),
            pltpu.VMEM((ATTN_HEADS, tq, HEAD_DIM), jnp.bfloat16),
            pltpu.VMEM((ATTN_HEADS, HEAD_DIM, tq), jnp.float32),
            pltpu.VMEM((ATTN_HEADS, tq), jnp.float32),
            pltpu.VMEM((ATTN_HEADS, tq), jnp.float32),
        ],
        compiler_params=_params("arbitrary", "arbitrary"),
        name="dsa_attn",
    )(p3, p3, p3, p3, p3, p3, q_g, k_g, ln_w, ln_b)


def _ret_kernel(lg_ref, rq_ref, rk_ref, rv_ref, rg_ref, cos_ref, sin_ref, g_ref, o_ref, *, rc):
    h = pl.program_id(1)
    lg = lg_ref[h]
    seq = rq_ref.shape[1]
    n = lax.broadcasted_iota(jnp.int32, (rc, RET_DV), 0).astype(jnp.float32)
    cross_decay = jnp.exp(lg * (n + 1.0))
    state_decay = jnp.exp(lg * (rc - 1.0 - n))
    chunk_decay = jnp.exp(lg * jnp.full((RET_DK, RET_DV), float(rc), jnp.float32))
    rel = (lax.broadcasted_iota(jnp.int32, (rc, rc), 0)
           - lax.broadcasted_iota(jnp.int32, (rc, rc), 1)).astype(jnp.float32)
    intra = jnp.where(rel >= 0, jnp.exp(lg * jnp.maximum(rel, 0.0)), 0.0)

    def rot(x, rows):
        return x * cos_ref[rows, :] + pltpu.roll(x, RET_DK // 2, 1) * sin_ref[rows, :]

    state = jnp.zeros((RET_DK, RET_DV), jnp.float32)
    for c in range(seq // rc):
        rows = slice(c * rc, (c + 1) * rc)
        q = rot(rq_ref[0, rows, :].astype(jnp.float32), rows)
        k = rot(rk_ref[0, rows, :].astype(jnp.float32), rows) * (RET_DK ** -0.5)
        v = rv_ref[0, rows, :]
        qb = q.astype(jnp.bfloat16)
        inner = _dot_nt(qb, k.astype(jnp.bfloat16)) * intra
        o = _dot(inner.astype(jnp.bfloat16), v) + _dot(qb, state.astype(jnp.bfloat16)) * cross_decay
        kd_t = (k * state_decay).T.astype(jnp.bfloat16)
        state = state * chunk_decay + _dot(kd_t, v)
        ms = jnp.mean(o * o, axis=-1, keepdims=True)
        y = (o * lax.rsqrt(ms + EPS)) * g_ref[0]
        gate = rg_ref[0, rows, :].astype(jnp.float32)
        o_ref[0, rows, :] = ((gate * (1.0 / (1.0 + jnp.exp(-gate)))) * y).astype(o_ref.dtype)


def _retention(p3, log_gamma, cos2, sin2, ret_g, *, rc):
    b, seq, _ = p3.shape
    assert seq % rc == 0

    def head_spec(off):
        return pl.BlockSpec((1, seq, RET_DK), lambda bi, h: (bi, 0, off // RET_DK + h))

    return pl.pallas_call(
        functools.partial(_ret_kernel, rc=rc),
        out_shape=jax.ShapeDtypeStruct((b, seq, RET_WIDTH), jnp.bfloat16),
        grid=(b, RET_HEADS),
        in_specs=[
            pl.BlockSpec(memory_space=pltpu.SMEM),
            head_spec(RQ_OFF), head_spec(RK_OFF), head_spec(RV_OFF), head_spec(RG_OFF),
            pl.BlockSpec((seq, RET_DK), lambda bi, h: (0, 0)),
            pl.BlockSpec((seq, RET_DK), lambda bi, h: (0, 0)),
            pl.BlockSpec((1, 1, RET_DV), lambda bi, h: (h, 0, 0)),
        ],
        out_specs=pl.BlockSpec((1, seq, RET_DV), lambda bi, h: (bi, 0, h)),
        compiler_params=_params("arbitrary", "arbitrary"),
        name="retention",
    )(log_gamma, p3, p3, p3, p3, cos2, sin2, ret_g)


def _routing(logits):
    lane = lax.broadcasted_iota(jnp.int32, logits.shape, 1).astype(jnp.float32)
    big = float(LANES)
    neg = -jnp.inf

    def first_argmax(v, vmax):
        return jnp.min(jnp.where(v == vmax, lane, big), axis=-1, keepdims=True)

    g_mask = (lane >= N_EXPERTS) & (lane < N_EXPERTS + N_GROUPS)
    gl = jnp.where(g_mask, logits, neg)
    g_max = jnp.max(gl, axis=-1, keepdims=True)
    g_sel = first_argmax(gl, g_max) - N_EXPERTS
    g_gate = 1.0 / jnp.sum(jnp.where(g_mask, jnp.exp(gl - g_max), 0.0), axis=-1, keepdims=True)

    e_lo = g_sel * EXPERTS_PER_GROUP
    el = jnp.where((lane >= e_lo) & (lane < e_lo + EXPERTS_PER_GROUP), logits, neg)
    v1 = jnp.max(el, axis=-1, keepdims=True)
    i1 = first_argmax(el, v1)
    el2 = jnp.where(lane == i1, neg, el)
    v2 = jnp.max(el2, axis=-1, keepdims=True)
    i2 = first_argmax(el2, v2)
    e2 = jnp.exp(v2 - v1)
    denom = 1.0 + e2
    w1 = (1.0 / denom) * g_gate
    w2 = (e2 / denom) * g_gate
    route = jnp.where(lane == ROUTE_E1, i1, 0.0) + jnp.where(lane == ROUTE_E2, i2, 0.0)
    return route + jnp.where(lane == ROUTE_W1, w1, 0.0) + jnp.where(lane == ROUTE_W2, w2, 0.0)


def _norm2(x1, g):
    ms = jnp.mean(x1 * x1, axis=-1, keepdims=True)
    return (x1 * lax.rsqrt(ms + EPS)) * g


def _out_proj_kernel(a_ref, r_ref, x_ref, wa_ref, wr_ref, g_ref, rhi_ref, rlo_ref,
                     x1_ref, route_ref):
    mixed = _dot(a_ref[...], wa_ref[...]) + _dot(r_ref[...], wr_ref[...])
    x1 = x_ref[...] + mixed
    x1_ref[...] = x1
    h2 = _norm2(x1, g_ref[...])
    hi = h2.astype(jnp.bfloat16)
    lo = (h2 - hi.astype(jnp.float32)).astype(jnp.bfloat16)
    logits = _dot(hi, rhi_ref[...]) + (_dot(hi, rlo_ref[...]) + _dot(lo, rhi_ref[...]))
    route_ref[...] = _routing(logits)


def _out_proj(attn2d, ret2d, x2d, w_out_bf, g2, r_hi, r_lo, *, tm):
    n, d = x2d.shape
    return pl.pallas_call(
        _out_proj_kernel,
        out_shape=(
            jax.ShapeDtypeStruct((n, d), jnp.float32),
            jax.ShapeDtypeStruct((n, LANES), jnp.float32),
        ),
        grid=(n // tm,),
        in_specs=[
            pl.BlockSpec((tm, ATTN_WIDTH), lambda i: (i, 0)),
            pl.BlockSpec((tm, RET_WIDTH), lambda i: (i, 0)),
            pl.BlockSpec((tm, d), lambda i: (i, 0)),
            pl.BlockSpec((ATTN_WIDTH, d), lambda i: (0, 0)),
            pl.BlockSpec((RET_WIDTH, d), lambda i: (ATTN_WIDTH // RET_WIDTH, 0)),
            pl.BlockSpec((1, d), lambda i: (0, 0)),
            pl.BlockSpec((d, LANES), lambda i: (0, 0)),
            pl.BlockSpec((d, LANES), lambda i: (0, 0)),
        ],
        out_specs=(
            pl.BlockSpec((tm, d), lambda i: (i, 0)),
            pl.BlockSpec((tm, LANES), lambda i: (i, 0)),
        ),
        compiler_params=_params("arbitrary"),
        name="out_proj",
    )(attn2d, ret2d, x2d, w_out_bf, w_out_bf, g2, r_hi, r_lo)


def _plan_kernel(route_ref, pos_ref, cnt_ref, rank_scr, *, tm, blk):
    n = route_ref.shape[0]
    lane = lax.broadcasted_iota(jnp.int32, (blk, LANES), 1).astype(jnp.float32)
    before = (lax.broadcasted_iota(jnp.int32, (blk, blk), 1)
              < lax.broadcasted_iota(jnp.int32, (blk, blk), 0)).astype(jnp.bfloat16)

    def one_hot(rows):
        r = route_ref[rows, :]
        e1 = r[:, ROUTE_E1:ROUTE_E1 + 1]
        e2 = r[:, ROUTE_E2:ROUTE_E2 + 1]
        return lane == e1, lane == e2

    def rank_body(b, run):
        rows = pl.ds(pl.multiple_of(b * blk, blk), blk)
        m1, m2 = one_hot(rows)
        sel = jnp.where(m1 | m2, 1.0, 0.0)
        rank_scr[rows, :] = _dot(before, sel.astype(jnp.bfloat16)) + run
        return run + jnp.sum(sel, axis=0, keepdims=True)
    cnt = lax.fori_loop(0, n // blk, rank_body, jnp.zeros((1, LANES), jnp.float32))
    cnt_ref[...] = jnp.broadcast_to(cnt, cnt_ref.shape)

    tiles = jnp.floor((cnt + (tm - 1.0)) * (1.0 / tm))
    below = (lax.broadcasted_iota(jnp.int32, (LANES, LANES), 0)
             < lax.broadcasted_iota(jnp.int32, (LANES, LANES), 1)).astype(jnp.bfloat16)
    start = _dot(jnp.broadcast_to(tiles, (8, LANES)).astype(jnp.bfloat16), below)[0:1, :] * float(tm)

    def pos_body(b, carry):
        rows = pl.ds(pl.multiple_of(b * blk, blk), blk)
        m1, m2 = one_hot(rows)
        dest = rank_scr[rows, :] + start
        p1 = jnp.sum(jnp.where(m1, dest, 0.0), axis=-1, keepdims=True)
        p2 = jnp.sum(jnp.where(m2, dest, 0.0), axis=-1, keepdims=True)
        pos_ref[rows, :] = (jnp.where(lane == 0.0, p1, 0.0) + jnp.where(lane == 1.0, p2, 0.0)).astype(jnp.int32)
        return carry
    lax.fori_loop(0, n // blk, pos_body, 0)


def _moe_plan(route, *, tm, blk):
    n = route.shape[0]
    return pl.pallas_call(
        functools.partial(_plan_kernel, tm=tm, blk=blk),
        out_shape=(jax.ShapeDtypeStruct((n, LANES), jnp.int32),
                   jax.ShapeDtypeStruct((8, LANES), jnp.float32)),
        scratch_shapes=[pltpu.VMEM((n, LANES), jnp.float32)],
        compiler_params=pltpu.CompilerParams(vmem_limit_bytes=VMEM_LIMIT),
        name="moe_plan",
    )(route)


def _row_copy(src, src_row, dst, dst_row, sem):
    return pltpu.make_async_copy(src.at[pl.ds(src_row, 1), :], dst.at[pl.ds(dst_row, 1), :], sem)


def _scatter_kernel(pos_ref, x1_ref, g_ref, xs_ref, hbuf, sem):
    tm = x1_ref.shape[0]
    base = pl.program_id(0) * tm
    hbuf[...] = _norm2(x1_ref[...], g_ref[...])
    for r in range(tm):
        for s in range(2):
            _row_copy(hbuf, r, xs_ref, pos_ref[(base + r) * 2 + s], sem).start()

    def wait_body(j, carry):
        _row_copy(hbuf, 0, xs_ref, 0, sem).wait()
        return carry
    lax.fori_loop(0, 2 * tm, wait_body, 0)


def _moe_scatter(pos, x1, g2, *, tm, rows):
    n, d = x1.shape
    return pl.pallas_call(
        _scatter_kernel,
        out_shape=jax.ShapeDtypeStruct((rows, d), jnp.float32),
        grid_spec=pltpu.PrefetchScalarGridSpec(
            num_scalar_prefetch=1,
            grid=(n // tm,),
            in_specs=[pl.BlockSpec((tm, d), lambda i, pos: (i, 0)),
                      pl.BlockSpec((1, d), lambda i, pos: (0, 0))],
            out_specs=pl.BlockSpec(memory_space=pl.ANY),
            scratch_shapes=[pltpu.VMEM((tm, d), jnp.float32), pltpu.SemaphoreType.DMA(())],
        ),
        compiler_params=_params("arbitrary"),
        name="moe_scatter",
    )(pos, x1, g2)


def _ffn_kernel(te_ref, tv_ref, nu_ref, xs_ref, w1_ref, w3_ref, w2_ref, y_ref):
    t = pl.program_id(0)

    @pl.when(t < nu_ref[0])
    def _():
        row = lax.broadcasted_iota(jnp.int32, xs_ref.shape, 0)
        x = jnp.where(row < tv_ref[t], xs_ref[...], 0.0).astype(jnp.bfloat16)
        a = _dot(x, w1_ref[0].astype(jnp.bfloat16))
        b = _dot(x, w3_ref[0].astype(jnp.bfloat16))
        act = (a * (1.0 / (1.0 + jnp.exp(-a)))) * b
        y_ref[...] = _dot(act.astype(jnp.bfloat16), w2_ref[0].astype(jnp.bfloat16))


def _moe_ffn(tile_expert, tile_valid, n_used, xs, w1, w3, w2, *, tm):
    rows, d = xs.shape
    _, _, ff = w1.shape

    def tile(t, te, tv, nu):
        return jnp.minimum(t, nu[0] - 1)

    return pl.pallas_call(
        _ffn_kernel,
        out_shape=jax.ShapeDtypeStruct((rows, d), jnp.float32),
        grid_spec=pltpu.PrefetchScalarGridSpec(
            num_scalar_prefetch=3,
            grid=(rows // tm,),
            in_specs=[
                pl.BlockSpec((tm, d), lambda t, te, tv, nu: (tile(t, te, tv, nu), 0)),
                pl.BlockSpec((1, d, ff), lambda t, te, tv, nu: (te[tile(t, te, tv, nu)], 0, 0)),
                pl.BlockSpec((1, d, ff), lambda t, te, tv, nu: (te[tile(t, te, tv, nu)], 0, 0)),
                pl.BlockSpec((1, ff, d), lambda t, te, tv, nu: (te[tile(t, te, tv, nu)], 0, 0)),
            ],
            out_specs=pl.BlockSpec((tm, d), lambda t, te, tv, nu: (tile(t, te, tv, nu), 0)),
        ),
        compiler_params=_params("arbitrary"),
        name="moe_ffn",
    )(tile_expert, tile_valid, n_used, xs, w1, w3, w2)


def _combine_kernel(pos_ref, x1_ref, route_ref, y_ref, o_ref, ybuf, sem):
    tm = x1_ref.shape[0]
    base = pl.program_id(0) * tm
    for r in range(tm):
        for s in range(2):
            _row_copy(y_ref, pos_ref[(base + r) * 2 + s], ybuf.at[s], r, sem).start()

    def wait_body(j, carry):
        _row_copy(y_ref, 0, ybuf.at[0], 0, sem).wait()
        return carry
    lax.fori_loop(0, 2 * tm, wait_body, 0)
    route = route_ref[...]
    w1 = route[:, ROUTE_W1:ROUTE_W1 + 1]
    w2 = route[:, ROUTE_W2:ROUTE_W2 + 1]
    o_ref[...] = x1_ref[...] + (w1 * ybuf[0] + w2 * ybuf[1])


def _moe_combine(pos, x1, route, y, *, tm):
    n, d = x1.shape
    return pl.pallas_call(
        _combine_kernel,
        out_shape=jax.ShapeDtypeStruct((n, d), jnp.float32),
        grid_spec=pltpu.PrefetchScalarGridSpec(
            num_scalar_prefetch=1,
            grid=(n // tm,),
            in_specs=[pl.BlockSpec((tm, d), lambda i, pos: (i, 0)),
                      pl.BlockSpec((tm, LANES), lambda i, pos: (i, 0)),
                      pl.BlockSpec(memory_space=pl.ANY)],
            out_specs=pl.BlockSpec((tm, d), lambda i, pos: (i, 0)),
            scratch_shapes=[pltpu.VMEM((2, tm, d), jnp.float32), pltpu.SemaphoreType.DMA(())],
        ),
        compiler_params=_params("arbitrary"),
        name="moe_combine",
    )(pos, x1, route, y)


def _moe(x1, route, g2, w1, w3, w2, *, tm, gather_tm):
    n, d = x1.shape
    max_tiles = (2 * n) // tm + N_EXPERTS
    pos2d, cnt = _moe_plan(route, tm=tm, blk=gather_tm)
    pos = pos2d[:, :2].reshape(2 * n)
    counts = cnt[0, :N_EXPERTS].astype(jnp.int32)
    tiles = (counts + (tm - 1)) // tm
    ends = jnp.cumsum(tiles)
    t_idx = jnp.arange(max_tiles, dtype=jnp.int32)
    tile_expert = jnp.sum((ends[None, :] <= t_idx[:, None]).astype(jnp.int32), axis=1)
    tile_expert = jnp.minimum(tile_expert, N_EXPERTS - 1)
    first_tile = (ends - tiles)[tile_expert]
    tile_valid = jnp.clip(counts[tile_expert] - (t_idx - first_tile) * tm, 0, tm).astype(jnp.int32)
    n_used = ends[-1:].astype(jnp.int32)

    xs = _moe_scatter(pos, x1, g2, tm=gather_tm, rows=max_tiles * tm)
    y = _moe_ffn(tile_expert, tile_valid, n_used, xs, w1, w3, w2, tm=tm)
    return _moe_combine(pos, x1, route, y, tm=gather_tm)


def _pack_w_in(w_in):
    d = w_in.shape[0]
    sizes = (ATTN_WIDTH, KV_WIDTH, KV_WIDTH, IDX_WIDTH, IDX_DIM, IDX_HEADS,
             RET_WIDTH, RET_WIDTH, RET_WIDTH, RET_WIDTH)
    parts, start = [], 0
    for s in sizes:
        parts.append(w_in[:, start:start + s])
        start += s
    aq, ak, av, iq, ik, iw, rq, rk, rv, rg = parts

    def z(width):
        return jnp.zeros((d, width), w_in.dtype)

    packed = jnp.concatenate(
        [aq, iq, ak, av, ik, z(LANES - IDX_DIM), iw, z(LANES - IDX_HEADS), z(RQ_OFF - IW_OFF - LANES),
         rq, rk, rv, rg], axis=1)
    assert packed.shape[1] == PROJ_WIDTH
    return packed.astype(jnp.bfloat16)


def _pack_router(w_group, w_router):
    d = w_group.shape[0]
    experts = jnp.transpose(w_router, (1, 0, 2)).reshape(d, N_EXPERTS)
    wr = jnp.concatenate(
        [experts, w_group, jnp.zeros((d, LANES - N_EXPERTS - N_GROUPS), w_group.dtype)], axis=1)
    hi = wr.astype(jnp.bfloat16)
    lo = (wr - hi.astype(jnp.float32)).astype(jnp.bfloat16)
    return hi, lo


def _rotation_tables(seq):
    half = RET_DK // 2
    pos = jnp.arange(seq, dtype=jnp.float32)
    inv = 1.0 / (ROT_BASE ** jnp.linspace(0.0, 1.0, half, dtype=jnp.float32))
    ang = pos[:, None] * inv[None, :]
    c, s = jnp.cos(ang), jnp.sin(ang)
    return jnp.concatenate([c, c], axis=-1), jnp.concatenate([-s, s], axis=-1)


def _tiles(n, seq):
    def fit(total, want):
        t = min(total, want)
        while total % t:
            t //= 2
        return t
    return dict(
        proj_tm=fit(n, 1024), proj_tn=1024,
        attn_tq=fit(seq, 256), attn_tk=fit(seq, 256),
        ret_rc=fit(seq, 256),
        out_tm=fit(n, 256),
        moe_tm=fit(n, 256), moe_gather_tm=fit(n, 256),
    )


def kernel(x, norm1_g, w_in, q_norm_g, k_norm_g, idx_k_ln_w, idx_k_ln_b, ret_norm_g,
           w_out, norm2_g, w_group, w_router, w1, w3, w2):
    b, seq, d = x.shape
    n = b * seq
    depth = w_in.shape[0]
    t = _tiles(n, seq)
    cos2, sin2 = _rotation_tables(seq)
    log_gamma = jnp.log1p(-jnp.exp2(-5.0 - jnp.arange(RET_HEADS, dtype=jnp.float32)))

    x2d = x.reshape(n, d)
    for l in range(depth):
        proj = _in_proj(x2d, norm1_g[l][None, :], _pack_w_in(w_in[l]),
                        tm=t["proj_tm"], tn=t["proj_tn"])
        p3 = proj.reshape(b, seq, PROJ_WIDTH)
        attn = _dsa_attention(p3, q_norm_g[l][None, :], k_norm_g[l][None, :],
                              idx_k_ln_w[l][None, :], idx_k_ln_b[l][None, :],
                              tq=t["attn_tq"], tk=t["attn_tk"])
        ret = _retention(p3, log_gamma, cos2, sin2, ret_norm_g[l].reshape(RET_HEADS, 1, RET_DV),
                         rc=t["ret_rc"])
        r_hi, r_lo = _pack_router(w_group[l], w_router[l])
        g2 = norm2_g[l][None, :]
        x1, route = _out_proj(attn.reshape(n, ATTN_WIDTH), ret.reshape(n, RET_WIDTH), x2d,
                              w_out[l].astype(jnp.bfloat16), g2, r_hi, r_lo, tm=t["out_tm"])
        x2d = _moe(x1, route, g2, w1[l], w3[l], w2[l], tm=t["moe_tm"], gather_tm=t["moe_gather_tm"])
    return x2d.reshape(b, seq, d)
```

```python
import functools
import math

import jax
import jax.numpy as jnp
from jax import lax
from jax.experimental import pallas as pl
from jax.experimental.pallas import tpu as pltpu

CHUNK = 64
ATTN_HEADS = 8
HEAD_DIM = 128
KV_HEADS = 2
HEADS_PER_KV = ATTN_HEADS // KV_HEADS
IDX_HEADS = 16
IDX_DIM = 64
TOPK_MAX = 256
RET_HEADS = 8
RET_DK = 128
RET_DV = 128
ROT_BASE = 10000.0
N_GROUPS = 4
EXPERTS_PER_GROUP = 8
N_EXPERTS = N_GROUPS * EXPERTS_PER_GROUP
EPS = 1e-6

ATTN_WIDTH = ATTN_HEADS * HEAD_DIM
KV_WIDTH = KV_HEADS * HEAD_DIM
IDX_WIDTH = IDX_HEADS * IDX_DIM
RET_WIDTH = RET_HEADS * RET_DK

LANES = 128
VMEM_LIMIT = 56 * 1024 * 1024

AQ_OFF = 0
IQ_OFF = AQ_OFF + ATTN_WIDTH
AK_OFF = IQ_OFF + IDX_WIDTH
AV_OFF = AK_OFF + KV_WIDTH
IK_OFF = AV_OFF + KV_WIDTH
IW_OFF = IK_OFF + LANES
RQ_OFF = 3072
RK_OFF = RQ_OFF + RET_WIDTH
RV_OFF = RK_OFF + RET_WIDTH
RG_OFF = RV_OFF + RET_WIDTH
PROJ_WIDTH = RG_OFF + RET_WIDTH

ROUTE_E1, ROUTE_E2, ROUTE_W1, ROUTE_W2 = 0, 1, 2, 3

INT_MIN = -(2 ** 31)
NEG_BIG = -1e30

_NT = (((1,), (1,)), ((), ()))


def _dot(a, b):
    return jnp.dot(a, b, preferred_element_type=jnp.float32)


def _dot_nt(a, b):
    return lax.dot_general(a, b, _NT, preferred_element_type=jnp.float32)


def _params(*sem):
    return pltpu.CompilerParams(dimension_semantics=sem, vmem_limit_bytes=VMEM_LIMIT)


def _in_proj_kernel(x_ref, g_ref, w_ref, o_ref, h_scr, *, row_chunk):
    @pl.when(pl.program_id(1) == 0)
    def _():
        def body(c, carry):
            rows = pl.ds(pl.multiple_of(c * row_chunk, row_chunk), row_chunk)
            x = x_ref[rows, :]
            ms = jnp.mean(x * x, axis=-1, keepdims=True)
            h_scr[rows, :] = ((x * lax.rsqrt(ms + EPS)) * g_ref[...]).astype(jnp.bfloat16)
            return carry
        lax.fori_loop(0, x_ref.shape[0] // row_chunk, body, 0)

    o_ref[...] = _dot(h_scr[...], w_ref[...]).astype(o_ref.dtype)


def _in_proj(x2d, g, wp, *, tm, tn):
    n, d = x2d.shape
    pw = wp.shape[1]
    return pl.pallas_call(
        functools.partial(_in_proj_kernel, row_chunk=min(tm, 128)),
        out_shape=jax.ShapeDtypeStruct((n, pw), jnp.bfloat16),
        grid=(n // tm, pw // tn),
        in_specs=[
            pl.BlockSpec((tm, d), lambda i, j: (i, 0)),
            pl.BlockSpec((1, d), lambda i, j: (0, 0)),
            pl.BlockSpec((d, tn), lambda i, j: (0, j)),
        ],
        out_specs=pl.BlockSpec((tm, tn), lambda i, j: (i, j)),
        scratch_shapes=[pltpu.VMEM((tm, d), jnp.bfloat16)],
        compiler_params=_params("arbitrary", "arbitrary"),
        name="in_proj",
    )(x2d, g, wp)


def _ordered_float(v):
    bits = v ^ ((v >> 31) & jnp.int32(0x7FFFFFFF))
    return pltpu.bitcast(bits, jnp.float32)


def _attn_kernel(aq_ref, iq_ref, iw_ref, ak_ref, av_ref, ik_ref, qg_ref, kg_ref, lnw_ref, lnb_ref,
                 o_ref,
                 kn_scr, ikn_scr, vt_scr, key_scr, iqh_scr, wt_scr, qn_scr, acc_scr, m_scr, l_scr,
                 *, tk, topk, idx_w_scale):
    i = pl.program_id(1)
    seq = ak_ref.shape[1]
    tq = aq_ref.shape[1]
    chunk_shift = CHUNK.bit_length() - 1

    @pl.when(i == 0)
    def _():
        def body(c, carry):
            rows = pl.ds(pl.multiple_of(c * tk, tk), tk)
            for g in range(KV_HEADS):
                cols = slice(g * HEAD_DIM, (g + 1) * HEAD_DIM)
                k = ak_ref[0, rows, cols].astype(jnp.float32)
                ms = jnp.mean(k * k, axis=-1, keepdims=True)
                kn_scr[rows, cols] = ((k * lax.rsqrt(ms + EPS)) * kg_ref[...]).astype(jnp.bfloat16)
                v = av_ref[0, rows, cols].astype(jnp.float32)
                vt_scr[g, c] = v.T.astype(jnp.bfloat16)
            ki = ik_ref[0, rows, :IDX_DIM].astype(jnp.float32)
            mu = jnp.mean(ki, axis=-1, keepdims=True)
            var = jnp.mean(jnp.square(ki - mu), axis=-1, keepdims=True)
            y = (ki - mu) * lax.rsqrt(var + EPS)
            ikn_scr[rows, :] = (y * lnw_ref[...] + lnb_ref[...]).astype(jnp.bfloat16)
            return carry
        lax.fori_loop(0, seq // tk, body, 0)

    t0 = i * tq
    n_kt = (t0 + tq) // tk
    scale = (HEAD_DIM ** -0.5) * math.log2(math.e)
    for h in range(ATTN_HEADS):
        g, r = divmod(h, HEADS_PER_KV)
        q = aq_ref[0, :, h * HEAD_DIM:(h + 1) * HEAD_DIM].astype(jnp.float32)
        ms = jnp.mean(q * q, axis=-1, keepdims=True)
        qn_scr[g, r * tq:(r + 1) * tq, :] = (
            (q * lax.rsqrt(ms + EPS)) * qg_ref[...] * scale).astype(jnp.bfloat16)
    for h in range(IDX_HEADS):
        iqh_scr[h] = iq_ref[0, :, h * IDX_DIM:(h + 1) * IDX_DIM]
    wt_scr[...] = iw_ref[0].astype(jnp.float32).T * idx_w_scale

    q_chunk = (t0 + lax.broadcasted_iota(jnp.int32, (tk, tq), 1)) >> chunk_shift

    def score_body(kt, carry):
        rows = pl.ds(pl.multiple_of(kt * tk, tk), tk)
        ik_t = ikn_scr[rows, :]
        acc = jnp.zeros((tk, tq), jnp.float32)
        for h in range(IDX_HEADS):
            d = _dot_nt(ik_t, iqh_scr[h])
            acc = acc + jnp.maximum(d, 0.0) * wt_scr[h:h + 1, :]
        k_chunk = (kt * tk + lax.broadcasted_iota(jnp.int32, (tk, tq), 0)) >> chunk_shift
        key_scr[rows, :] = jnp.where(k_chunk <= q_chunk, acc, -jnp.inf)
        return carry
    lax.fori_loop(0, n_kt, score_body, 0)

    def bit_body(it, lo):
        cand = lo + lax.shift_left(jnp.int32(1), 31 - it)
        cand_f = _ordered_float(cand)

        def count_body(kt, part):
            rows = pl.ds(pl.multiple_of(kt * tk, tk), tk)
            hit = jnp.where(key_scr[rows, :] >= cand_f, 1.0, 0.0).reshape(tk // 8, 8, tq)
            while hit.shape[0] > 1:
                half = hit.shape[0] // 2
                hit = hit[:half] + hit[half:]
            return part + hit[0]
        part = lax.fori_loop(0, n_kt, count_body, jnp.zeros((8, tq), jnp.float32))
        cnt = jnp.sum(part, axis=0, keepdims=True)
        return jnp.where(cnt >= float(topk), cand, lo)
    lo = lax.fori_loop(0, 32, bit_body, jnp.full((1, tq), INT_MIN, jnp.int32))
    thr = jnp.where(lo == INT_MIN, jnp.finfo(jnp.float32).min, _ordered_float(lo))

    m_scr[...] = jnp.full(m_scr.shape, NEG_BIG, jnp.float32)
    l_scr[...] = jnp.zeros(l_scr.shape, jnp.float32)
    acc_scr[...] = jnp.zeros(acc_scr.shape, jnp.float32)

    def attn_body(kt, carry):
        rows = pl.ds(pl.multiple_of(kt * tk, tk), tk)
        bias = jnp.where(key_scr[rows, :] >= thr, 0.0, NEG_BIG)
        bias = jnp.concatenate([bias] * HEADS_PER_KV, axis=1)
        for g in range(KV_HEADS):
            k_t = kn_scr[rows, g * HEAD_DIM:(g + 1) * HEAD_DIM]
            s = _dot_nt(k_t, qn_scr[g]) + bias
            m_old = m_scr[g]
            m_new = jnp.maximum(m_old, jnp.max(s, axis=0, keepdims=True))
            p = jnp.exp2(s - m_new)
            alpha = jnp.exp2(m_old - m_new)
            l_scr[g] = alpha * l_scr[g] + jnp.sum(p, axis=0, keepdims=True)
            acc_scr[g] = acc_scr[g] * alpha + _dot(vt_scr[g, kt], p.astype(jnp.bfloat16))
            m_scr[g] = m_new
        return carry
    lax.fori_loop(0, n_kt, attn_body, 0)

    for h in range(ATTN_HEADS):
        g, r = divmod(h, HEADS_PER_KV)
        o = acc_scr[g, :, r * tq:(r + 1) * tq] / l_scr[g, :, r * tq:(r + 1) * tq]
        o_ref[0, :, h * HEAD_DIM:(h + 1) * HEAD_DIM] = o.T.astype(o_ref.dtype)


def _dsa_attention(p3, q_g, k_g, ln_w, ln_b, *, tq, tk):
    b, seq, _ = p3.shape
    topk = min(TOPK_MAX, seq // 4)
    idx_w_scale = (IDX_HEADS ** -0.5) * (IDX_DIM ** -0.5)
    assert seq % tq == 0 and tq % tk == 0 and tk % CHUNK == 0

    def col(off, width):
        return off // width

    return pl.pallas_call(
        functools.partial(_attn_kernel, tk=tk, topk=topk, idx_w_scale=idx_w_scale),
        out_shape=jax.ShapeDtypeStruct((b, seq, ATTN_WIDTH), jnp.bfloat16),
        grid=(b, seq // tq),
        in_specs=[
            pl.BlockSpec((1, tq, ATTN_WIDTH), lambda bi, i: (bi, i, col(AQ_OFF, ATTN_WIDTH))),
            pl.BlockSpec((1, tq, IDX_WIDTH), lambda bi, i: (bi, i, col(IQ_OFF, IDX_WIDTH))),
            pl.BlockSpec((1, tq, LANES), lambda bi, i: (bi, i, col(IW_OFF, LANES))),
            pl.BlockSpec((1, seq, KV_WIDTH), lambda bi, i: (bi, 0, col(AK_OFF, KV_WIDTH))),
            pl.BlockSpec((1, seq, KV_WIDTH), lambda bi, i: (bi, 0, col(AV_OFF, KV_WIDTH))),
            pl.BlockSpec((1, seq, LANES), lambda bi, i: (bi, 0, col(IK_OFF, LANES))),
            pl.BlockSpec((1, HEAD_DIM), lambda bi, i: (0, 0)),
            pl.BlockSpec((1, HEAD_DIM), lambda bi, i: (0, 0)),
            pl.BlockSpec((1, IDX_DIM), lambda bi, i: (0, 0)),
            pl.BlockSpec((1, IDX_DIM), lambda bi, i: (0, 0)),
        ],
        out_specs=pl.BlockSpec((1, tq, ATTN_WIDTH), lambda bi, i: (bi, i, 0)),
        scratch_shapes=[
            pltpu.VMEM((seq, KV_WIDTH), jnp.bfloat16),
            pltpu.VMEM((seq, IDX_DIM), jnp.bfloat16),
            pltpu.VMEM((KV_HEADS, seq // tk, HEAD_DIM, tk), jnp.bfloat16),
            pltpu.VMEM((seq, tq), jnp.float32),
            pltpu.VMEM((IDX_HEADS, tq, IDX_DIM), jnp.bfloat16),
            pltpu.VMEM((LANES, tq), jnp.float32),
            pltpu.VMEM((KV_HEADS, HEADS_PER_KV * tq, HEAD_DIM), jnp.bfloat16),
            pltpu.VMEM((KV_HEADS, HEAD_DIM, HEADS_PER_KV * tq), jnp.float32),
            pltpu.VMEM((KV_HEADS, 1, HEADS_PER_KV * tq), jnp.float32),
            pltpu.VMEM((KV_HEADS, 1, HEADS_PER_KV * tq), jnp.float32),
        ],
        compiler_params=_params("arbitrary", "arbitrary"),
        name="dsa_attn",
    )(p3, p3, p3, p3, p3, p3, q_g, k_g, ln_w, ln_b)


def _ret_kernel(lg_ref, rq_ref, rk_ref, rv_ref, rg_ref, cos_ref, sin_ref, g_ref, o_ref, *, rc):
    h = pl.program_id(1)
    lg = lg_ref[h]
    seq = rq_ref.shape[1]
    n = lax.broadcasted_iota(jnp.int32, (rc, RET_DV), 0).astype(jnp.float32)
    cross_decay = jnp.exp(lg * (n + 1.0))
    state_decay = jnp.exp(lg * (rc - 1.0 - n))
    chunk_decay = jnp.exp(lg * jnp.full((RET_DK, RET_DV), float(rc), jnp.float32))
    rel = (lax.broadcasted_iota(jnp.int32, (rc, rc), 0)
           - lax.broadcasted_iota(jnp.int32, (rc, rc), 1)).astype(jnp.float32)
    intra = jnp.where(rel >= 0, jnp.exp(lg * jnp.maximum(rel, 0.0)), 0.0)

    def rot(x, rows):
        return x * cos_ref[rows, :] + pltpu.roll(x, RET_DK // 2, 1) * sin_ref[rows, :]

    state = jnp.zeros((RET_DK, RET_DV), jnp.float32)
    for c in range(seq // rc):
        rows = slice(c * rc, (c + 1) * rc)
        q = rot(rq_ref[0, rows, :].astype(jnp.float32), rows)
        k = rot(rk_ref[0, rows, :].astype(jnp.float32), rows) * (RET_DK ** -0.5)
        v = rv_ref[0, rows, :]
        qb = q.astype(jnp.bfloat16)
        inner = _dot_nt(qb, k.astype(jnp.bfloat16)) * intra
        o = _dot(inner.astype(jnp.bfloat16), v) + _dot(qb, state.astype(jnp.bfloat16)) * cross_decay
        kd_t = (k * state_decay).T.astype(jnp.bfloat16)
        state = state * chunk_decay + _dot(kd_t, v)
        ms = jnp.mean(o * o, axis=-1, keepdims=True)
        y = (o * lax.rsqrt(ms + EPS)) * g_ref[0]
        gate = rg_ref[0, rows, :].astype(jnp.float32)
        o_ref[0, rows, :] = ((gate * (1.0 / (1.0 + jnp.exp(-gate)))) * y).astype(o_ref.dtype)


def _retention(p3, log_gamma, cos2, sin2, ret_g, *, rc):
    b, seq, _ = p3.shape
    assert seq % rc == 0

    def head_spec(off):
        return pl.BlockSpec((1, seq, RET_DK), lambda bi, h: (bi, 0, off // RET_DK + h))

    return pl.pallas_call(
        functools.partial(_ret_kernel, rc=rc),
        out_shape=jax.ShapeDtypeStruct((b, seq, RET_WIDTH), jnp.bfloat16),
        grid=(b, RET_HEADS),
        in_specs=[
            pl.BlockSpec(memory_space=pltpu.SMEM),
            head_spec(RQ_OFF), head_spec(RK_OFF), head_spec(RV_OFF), head_spec(RG_OFF),
            pl.BlockSpec((seq, RET_DK), lambda bi, h: (0, 0)),
            pl.BlockSpec((seq, RET_DK), lambda bi, h: (0, 0)),
            pl.BlockSpec((1, 1, RET_DV), lambda bi, h: (h, 0, 0)),
        ],
        out_specs=pl.BlockSpec((1, seq, RET_DV), lambda bi, h: (bi, 0, h)),
        compiler_params=_params("arbitrary", "arbitrary"),
        name="retention",
    )(log_gamma, p3, p3, p3, p3, cos2, sin2, ret_g)


def _routing(logits):
    lane = lax.broadcasted_iota(jnp.int32, logits.shape, 1).astype(jnp.float32)
    big = float(LANES)
    neg = -jnp.inf

    def first_argmax(v, vmax):
        return jnp.min(jnp.where(v == vmax, lane, big), axis=-1, keepdims=True)

    g_mask = (lane >= N_EXPERTS) & (lane < N_EXPERTS + N_GROUPS)
    gl = jnp.where(g_mask, logits, neg)
    g_max = jnp.max(gl, axis=-1, keepdims=True)
    g_sel = first_argmax(gl, g_max) - N_EXPERTS
    g_gate = 1.0 / jnp.sum(jnp.where(g_mask, jnp.exp(gl - g_max), 0.0), axis=-1, keepdims=True)

    e_lo = g_sel * EXPERTS_PER_GROUP
    el = jnp.where((lane >= e_lo) & (lane < e_lo + EXPERTS_PER_GROUP), logits, neg)
    v1 = jnp.max(el, axis=-1, keepdims=True)
    i1 = first_argmax(el, v1)
    el2 = jnp.where(lane == i1, neg, el)
    v2 = jnp.max(el2, axis=-1, keepdims=True)
    i2 = first_argmax(el2, v2)
    e2 = jnp.exp(v2 - v1)
    denom = 1.0 + e2
    w1 = (1.0 / denom) * g_gate
    w2 = (e2 / denom) * g_gate
    route = jnp.where(lane == ROUTE_E1, i1, 0.0) + jnp.where(lane == ROUTE_E2, i2, 0.0)
    return route + jnp.where(lane == ROUTE_W1, w1, 0.0) + jnp.where(lane == ROUTE_W2, w2, 0.0)


def _norm2(x1, g):
    ms = jnp.mean(x1 * x1, axis=-1, keepdims=True)
    return (x1 * lax.rsqrt(ms + EPS)) * g


def _out_proj_kernel(a_ref, r_ref, x_ref, wa_ref, wr_ref, g_ref, rhi_ref, rlo_ref,
                     x1_ref, route_ref):
    mixed = _dot(a_ref[...], wa_ref[...]) + _dot(r_ref[...], wr_ref[...])
    x1 = x_ref[...] + mixed
    x1_ref[...] = x1
    h2 = _norm2(x1, g_ref[...])
    hi = h2.astype(jnp.bfloat16)
    lo = (h2 - hi.astype(jnp.float32)).astype(jnp.bfloat16)
    logits = _dot(hi, rhi_ref[...]) + (_dot(hi, rlo_ref[...]) + _dot(lo, rhi_ref[...]))
    route_ref[...] = _routing(logits)


def _out_proj(attn2d, ret2d, x2d, w_out_bf, g2, r_hi, r_lo, *, tm):
    n, d = x2d.shape
    return pl.pallas_call(
        _out_proj_kernel,
        out_shape=(
            jax.ShapeDtypeStruct((n, d), jnp.float32),
            jax.ShapeDtypeStruct((n, LANES), jnp.float32),
        ),
        grid=(n // tm,),
        in_specs=[
            pl.BlockSpec((tm, ATTN_WIDTH), lambda i: (i, 0)),
            pl.BlockSpec((tm, RET_WIDTH), lambda i: (i, 0)),
            pl.BlockSpec((tm, d), lambda i: (i, 0)),
            pl.BlockSpec((ATTN_WIDTH, d), lambda i: (0, 0)),
            pl.BlockSpec((RET_WIDTH, d), lambda i: (ATTN_WIDTH // RET_WIDTH, 0)),
            pl.BlockSpec((1, d), lambda i: (0, 0)),
            pl.BlockSpec((d, LANES), lambda i: (0, 0)),
            pl.BlockSpec((d, LANES), lambda i: (0, 0)),
        ],
        out_specs=(
            pl.BlockSpec((tm, d), lambda i: (i, 0)),
            pl.BlockSpec((tm, LANES), lambda i: (i, 0)),
        ),
        compiler_params=_params("arbitrary"),
        name="out_proj",
    )(attn2d, ret2d, x2d, w_out_bf, w_out_bf, g2, r_hi, r_lo)


def _plan_kernel(route_ref, pos_ref, cnt_ref, rank_scr, *, tm, blk):
    n = route_ref.shape[0]
    lane = lax.broadcasted_iota(jnp.int32, (blk, LANES), 1).astype(jnp.float32)
    before = (lax.broadcasted_iota(jnp.int32, (blk, blk), 1)
              < lax.broadcasted_iota(jnp.int32, (blk, blk), 0)).astype(jnp.bfloat16)

    def one_hot(rows):
        r = route_ref[rows, :]
        e1 = r[:, ROUTE_E1:ROUTE_E1 + 1]
        e2 = r[:, ROUTE_E2:ROUTE_E2 + 1]
        return lane == e1, lane == e2

    def rank_body(b, run):
        rows = pl.ds(pl.multiple_of(b * blk, blk), blk)
        m1, m2 = one_hot(rows)
        sel = jnp.where(m1 | m2, 1.0, 0.0)
        rank_scr[rows, :] = _dot(before, sel.astype(jnp.bfloat16)) + run
        return run + jnp.sum(sel, axis=0, keepdims=True)
    cnt = lax.fori_loop(0, n // blk, rank_body, jnp.zeros((1, LANES), jnp.float32))
    cnt_ref[...] = jnp.broadcast_to(cnt, cnt_ref.shape)

    tiles = jnp.floor((cnt + (tm - 1.0)) * (1.0 / tm))
    below = (lax.broadcasted_iota(jnp.int32, (LANES, LANES), 0)
             < lax.broadcasted_iota(jnp.int32, (LANES, LANES), 1)).astype(jnp.bfloat16)
    start = _dot(jnp.broadcast_to(tiles, (8, LANES)).astype(jnp.bfloat16), below)[0:1, :] * float(tm)

    def pos_body(b, carry):
        rows = pl.ds(pl.multiple_of(b * blk, blk), blk)
        m1, m2 = one_hot(rows)
        dest = rank_scr[rows, :] + start
        p1 = jnp.sum(jnp.where(m1, dest, 0.0), axis=-1, keepdims=True)
        p2 = jnp.sum(jnp.where(m2, dest, 0.0), axis=-1, keepdims=True)
        pos_ref[rows, :] = (jnp.where(lane == 0.0, p1, 0.0) + jnp.where(lane == 1.0, p2, 0.0)).astype(jnp.int32)
        return carry
    lax.fori_loop(0, n // blk, pos_body, 0)


def _moe_plan(route, *, tm, blk):
    n = route.shape[0]
    return pl.pallas_call(
        functools.partial(_plan_kernel, tm=tm, blk=blk),
        out_shape=(jax.ShapeDtypeStruct((n, LANES), jnp.int32),
                   jax.ShapeDtypeStruct((8, LANES), jnp.float32)),
        scratch_shapes=[pltpu.VMEM((n, LANES), jnp.float32)],
        compiler_params=pltpu.CompilerParams(vmem_limit_bytes=VMEM_LIMIT),
        name="moe_plan",
    )(route)


def _row_copy(src, src_row, dst, dst_row, sem):
    return pltpu.make_async_copy(src.at[pl.ds(src_row, 1), :], dst.at[pl.ds(dst_row, 1), :], sem)


def _scatter_kernel(pos_ref, x1_ref, g_ref, xs_ref, hbuf, sem):
    tm = x1_ref.shape[0]
    base = pl.program_id(0) * tm
    hbuf[...] = _norm2(x1_ref[...], g_ref[...])
    for r in range(tm):
        for s in range(2):
            _row_copy(hbuf, r, xs_ref, pos_ref[(base + r) * 2 + s], sem).start()

    def wait_body(j, carry):
        _row_copy(hbuf, 0, xs_ref, 0, sem).wait()
        return carry
    lax.fori_loop(0, 2 * tm, wait_body, 0)


def _moe_scatter(pos, x1, g2, *, tm, rows):
    n, d = x1.shape
    return pl.pallas_call(
        _scatter_kernel,
        out_shape=jax.ShapeDtypeStruct((rows, d), jnp.float32),
        grid_spec=pltpu.PrefetchScalarGridSpec(
            num_scalar_prefetch=1,
            grid=(n // tm,),
            in_specs=[pl.BlockSpec((tm, d), lambda i, pos: (i, 0)),
                      pl.BlockSpec((1, d), lambda i, pos: (0, 0))],
            out_specs=pl.BlockSpec(memory_space=pl.ANY),
            scratch_shapes=[pltpu.VMEM((tm, d), jnp.float32), pltpu.SemaphoreType.DMA(())],
        ),
        compiler_params=_params("arbitrary"),
        name="moe_scatter",
    )(pos, x1, g2)


def _ffn_kernel(te_ref, tv_ref, nu_ref, xs_ref, w1_ref, w3_ref, w2_ref, y_ref):
    t = pl.program_id(0)

    @pl.when(t < nu_ref[0])
    def _():
        row = lax.broadcasted_iota(jnp.int32, xs_ref.shape, 0)
        x = jnp.where(row < tv_ref[t], xs_ref[...], 0.0).astype(jnp.bfloat16)
        a = _dot(x, w1_ref[0].astype(jnp.bfloat16))
        b = _dot(x, w3_ref[0].astype(jnp.bfloat16))
        act = (a * (1.0 / (1.0 + jnp.exp(-a)))) * b
        y_ref[...] = _dot(act.astype(jnp.bfloat16), w2_ref[0].astype(jnp.bfloat16))


def _moe_ffn(tile_expert, tile_valid, n_used, xs, w1, w3, w2, *, tm):
    rows, d = xs.shape
    _, _, ff = w1.shape

    def tile(t, te, tv, nu):
        return jnp.minimum(t, nu[0] - 1)

    return pl.pallas_call(
        _ffn_kernel,
        out_shape=jax.ShapeDtypeStruct((rows, d), jnp.float32),
        grid_spec=pltpu.PrefetchScalarGridSpec(
            num_scalar_prefetch=3,
            grid=(rows // tm,),
            in_specs=[
                pl.BlockSpec((tm, d), lambda t, te, tv, nu: (tile(t, te, tv, nu), 0)),
                pl.BlockSpec((1, d, ff), lambda t, te, tv, nu: (te[tile(t, te, tv, nu)], 0, 0)),
                pl.BlockSpec((1, d, ff), lambda t, te, tv, nu: (te[tile(t, te, tv, nu)], 0, 0)),
                pl.BlockSpec((1, ff, d), lambda t, te, tv, nu: (te[tile(t, te, tv, nu)], 0, 0)),
            ],
            out_specs=pl.BlockSpec((tm, d), lambda t, te, tv, nu: (tile(t, te, tv, nu), 0)),
        ),
        compiler_params=_params("arbitrary"),
        name="moe_ffn",
    )(tile_expert, tile_valid, n_used, xs, w1, w3, w2)


def _combine_kernel(pos_ref, x1_ref, route_ref, y_ref, o_ref, ybuf, sem):
    tm = x1_ref.shape[0]
    base = pl.program_id(0) * tm
    for r in range(tm):
        for s in range(2):
            _row_copy(y_ref, pos_ref[(base + r) * 2 + s], ybuf.at[s], r, sem).start()

    def wait_body(j, carry):
        _row_copy(y_ref, 0, ybuf.at[0], 0, sem).wait()
        return carry
    lax.fori_loop(0, 2 * tm, wait_body, 0)
    route = route_ref[...]
    w1 = route[:, ROUTE_W1:ROUTE_W1 + 1]
    w2 = route[:, ROUTE_W2:ROUTE_W2 + 1]
    o_ref[...] = x1_ref[...] + (w1 * ybuf[0] + w2 * ybuf[1])


def _moe_combine(pos, x1, route, y, *, tm):
    n, d = x1.shape
    return pl.pallas_call(
        _combine_kernel,
        out_shape=jax.ShapeDtypeStruct((n, d), jnp.float32),
        grid_spec=pltpu.PrefetchScalarGridSpec(
            num_scalar_prefetch=1,
            grid=(n // tm,),
            in_specs=[pl.BlockSpec((tm, d), lambda i, pos: (i, 0)),
                      pl.BlockSpec((tm, LANES), lambda i, pos: (i, 0)),
                      pl.BlockSpec(memory_space=pl.ANY)],
            out_specs=pl.BlockSpec((tm, d), lambda i, pos: (i, 0)),
            scratch_shapes=[pltpu.VMEM((2, tm, d), jnp.float32), pltpu.SemaphoreType.DMA(())],
        ),
        compiler_params=_params("arbitrary"),
        name="moe_combine",
    )(pos, x1, route, y)


def _moe(x1, route, g2, w1, w3, w2, *, tm, gather_tm):
    n, d = x1.shape
    max_tiles = (2 * n) // tm + N_EXPERTS
    pos2d, cnt = _moe_plan(route, tm=tm, blk=gather_tm)
    pos = pos2d[:, :2].reshape(2 * n)
    counts = cnt[0, :N_EXPERTS].astype(jnp.int32)
    tiles = (counts + (tm - 1)) // tm
    ends = jnp.cumsum(tiles)
    t_idx = jnp.arange(max_tiles, dtype=jnp.int32)
    tile_expert = jnp.sum((ends[None, :] <= t_idx[:, None]).astype(jnp.int32), axis=1)
    tile_expert = jnp.minimum(tile_expert, N_EXPERTS - 1)
    first_tile = (ends - tiles)[tile_expert]
    tile_valid = jnp.clip(counts[tile_expert] - (t_idx - first_tile) * tm, 0, tm).astype(jnp.int32)
    n_used = ends[-1:].astype(jnp.int32)

    xs = _moe_scatter(pos, x1, g2, tm=gather_tm, rows=max_tiles * tm)
    y = _moe_ffn(tile_expert, tile_valid, n_used, xs, w1, w3, w2, tm=tm)
    return _moe_combine(pos, x1, route, y, tm=gather_tm)


def _pack_w_in(w_in):
    d = w_in.shape[0]
    sizes = (ATTN_WIDTH, KV_WIDTH, KV_WIDTH, IDX_WIDTH, IDX_DIM, IDX_HEADS,
             RET_WIDTH, RET_WIDTH, RET_WIDTH, RET_WIDTH)
    parts, start = [], 0
    for s in sizes:
        parts.append(w_in[:, start:start + s])
        start += s
    aq, ak, av, iq, ik, iw, rq, rk, rv, rg = parts

    def z(width):
        return jnp.zeros((d, width), w_in.dtype)

    packed = jnp.concatenate(
        [aq, iq, ak, av, ik, z(LANES - IDX_DIM), iw, z(LANES - IDX_HEADS), z(RQ_OFF - IW_OFF - LANES),
         rq, rk, rv, rg], axis=1)
    assert packed.shape[1] == PROJ_WIDTH
    return packed.astype(jnp.bfloat16)


def _pack_router(w_group, w_router):
    d = w_group.shape[0]
    experts = jnp.transpose(w_router, (1, 0, 2)).reshape(d, N_EXPERTS)
    wr = jnp.concatenate(
        [experts, w_group, jnp.zeros((d, LANES - N_EXPERTS - N_GROUPS), w_group.dtype)], axis=1)
    hi = wr.astype(jnp.bfloat16)
    lo = (wr - hi.astype(jnp.float32)).astype(jnp.bfloat16)
    return hi, lo


def _rotation_tables(seq):
    half = RET_DK // 2
    pos = jnp.arange(seq, dtype=jnp.float32)
    inv = 1.0 / (ROT_BASE ** jnp.linspace(0.0, 1.0, half, dtype=jnp.float32))
    ang = pos[:, None] * inv[None, :]
    c, s = jnp.cos(ang), jnp.sin(ang)
    return jnp.concatenate([c, c], axis=-1), jnp.concatenate([-s, s], axis=-1)


def _tiles(n, seq):
    def fit(total, want):
        t = min(total, want)
        while total % t:
            t //= 2
        return t
    return dict(
        proj_tm=fit(n, 1024), proj_tn=1024,
        attn_tq=fit(seq, 256), attn_tk=fit(seq, 256),
        ret_rc=fit(seq, 256),
        out_tm=fit(n, 256),
        moe_tm=fit(n, 256), moe_gather_tm=fit(n, 256),
    )


def kernel(x, norm1_g, w_in, q_norm_g, k_norm_g, idx_k_ln_w, idx_k_ln_b, ret_norm_g,
           w_out, norm2_g, w_group, w_router, w1, w3, w2):
    b, seq, d = x.shape
    n = b * seq
    depth = w_in.shape[0]
    t = _tiles(n, seq)
    cos2, sin2 = _rotation_tables(seq)
    log_gamma = jnp.log1p(-jnp.exp2(-5.0 - jnp.arange(RET_HEADS, dtype=jnp.float32)))

    x2d = x.reshape(n, d)
    for l in range(depth):
        proj = _in_proj(x2d, norm1_g[l][None, :], _pack_w_in(w_in[l]),
                        tm=t["proj_tm"], tn=t["proj_tn"])
        p3 = proj.reshape(b, seq, PROJ_WIDTH)
        attn = _dsa_attention(p3, q_norm_g[l][None, :], k_norm_g[l][None, :],
                              idx_k_ln_w[l][None, :], idx_k_ln_b[l][None, :],
                              tq=t["attn_tq"], tk=t["attn_tk"])
        ret = _retention(p3, log_gamma, cos2, sin2, ret_norm_g[l].reshape(RET_HEADS, 1, RET_DV),
                         rc=t["ret_rc"])
        r_hi, r_lo = _pack_router(w_group[l], w_router[l])
        g2 = norm2_g[l][None, :]
        x1, route = _out_proj(attn.reshape(n, ATTN_WIDTH), ret.reshape(n, RET_WIDTH), x2d,
                              w_out[l].astype(jnp.bfloat16), g2, r_hi, r_lo, tm=t["out_tm"])
        x2d = _moe(x1, route, g2, w1[l], w3[l], w2[l], tm=t["moe_tm"], gather_tm=t["moe_gather_tm"])
    return x2d.reshape(b, seq, d)
```

```python
import functools
import math

import jax
import jax.numpy as jnp
from jax import lax
from jax.experimental import pallas as pl
from jax.experimental.pallas import tpu as pltpu

CHUNK = 64
ATTN_HEADS = 8
HEAD_DIM = 128
KV_HEADS = 2
HEADS_PER_KV = ATTN_HEADS // KV_HEADS
IDX_HEADS = 16
IDX_DIM = 64
TOPK_MAX = 256
RET_HEADS = 8
RET_DK = 128
RET_DV = 128
ROT_BASE = 10000.0
N_GROUPS = 4
EXPERTS_PER_GROUP = 8
N_EXPERTS = N_GROUPS * EXPERTS_PER_GROUP
EPS = 1e-6

ATTN_WIDTH = ATTN_HEADS * HEAD_DIM
KV_WIDTH = KV_HEADS * HEAD_DIM
IDX_WIDTH = IDX_HEADS * IDX_DIM
RET_WIDTH = RET_HEADS * RET_DK

LANES = 128
VMEM_LIMIT = 56 * 1024 * 1024

AQ_OFF = 0
IQ_OFF = AQ_OFF + ATTN_WIDTH
AK_OFF = IQ_OFF + IDX_WIDTH
AV_OFF = AK_OFF + KV_WIDTH
IK_OFF = AV_OFF + KV_WIDTH
IW_OFF = IK_OFF + LANES
RQ_OFF = 3072
RK_OFF = RQ_OFF + RET_WIDTH
RV_OFF = RK_OFF + RET_WIDTH
RG_OFF = RV_OFF + RET_WIDTH
PROJ_WIDTH = RG_OFF + RET_WIDTH

ROUTE_E1, ROUTE_E2, ROUTE_W1, ROUTE_W2 = 0, 1, 2, 3

INT_MIN = -(2 ** 31)
NEG_BIG = -1e30

_NT = (((1,), (1,)), ((), ()))


def _dot(a, b):
    return jnp.dot(a, b, preferred_element_type=jnp.float32)


def _dot_nt(a, b):
    return lax.dot_general(a, b, _NT, preferred_element_type=jnp.float32)


def _params(*sem):
    return pltpu.CompilerParams(dimension_semantics=sem, vmem_limit_bytes=VMEM_LIMIT)


def _in_proj_kernel(x_ref, g_ref, w_ref, o_ref, h_scr, *, row_chunk):
    @pl.when(pl.program_id(1) == 0)
    def _():
        def body(c, carry):
            rows = pl.ds(pl.multiple_of(c * row_chunk, row_chunk), row_chunk)
            x = x_ref[rows, :]
            ms = jnp.mean(x * x, axis=-1, keepdims=True)
            h_scr[rows, :] = ((x * lax.rsqrt(ms + EPS)) * g_ref[...]).astype(jnp.bfloat16)
            return carry
        lax.fori_loop(0, x_ref.shape[0] // row_chunk, body, 0)

    o_ref[...] = _dot(h_scr[...], w_ref[...]).astype(o_ref.dtype)


def _in_proj(x2d, g, wp, *, tm, tn):
    n, d = x2d.shape
    pw = wp.shape[1]
    return pl.pallas_call(
        functools.partial(_in_proj_kernel, row_chunk=min(tm, 128)),
        out_shape=jax.ShapeDtypeStruct((n, pw), jnp.bfloat16),
        grid=(n // tm, pw // tn),
        in_specs=[
            pl.BlockSpec((tm, d), lambda i, j: (i, 0)),
            pl.BlockSpec((1, d), lambda i, j: (0, 0)),
            pl.BlockSpec((d, tn), lambda i, j: (0, j)),
        ],
        out_specs=pl.BlockSpec((tm, tn), lambda i, j: (i, j)),
        scratch_shapes=[pltpu.VMEM((tm, d), jnp.bfloat16)],
        compiler_params=_params("arbitrary", "arbitrary"),
        name="in_proj",
    )(x2d, g, wp)


def _ordered_float(v):
    bits = v ^ ((v >> 31) & jnp.int32(0x7FFFFFFF))
    return pltpu.bitcast(bits, jnp.float32)


def _attn_kernel(aq_ref, iq_ref, iw_ref, ak_ref, av_ref, ik_ref, qg_ref, kg_ref, lnw_ref, lnb_ref,
                 o_ref,
                 kn_scr, ikn_scr, vt_scr, key_scr, iqh_scr, wt_scr, qn_scr, acc_scr, m_scr, l_scr,
                 *, tk, topk, idx_w_scale):
    i = pl.program_id(1)
    seq = ak_ref.shape[1]
    tq = aq_ref.shape[1]
    chunk_shift = CHUNK.bit_length() - 1

    @pl.when(i == 0)
    def _():
        def body(c, carry):
            rows = pl.ds(pl.multiple_of(c * tk, tk), tk)
            for g in range(KV_HEADS):
                cols = slice(g * HEAD_DIM, (g + 1) * HEAD_DIM)
                k = ak_ref[0, rows, cols].astype(jnp.float32)
                ms = jnp.mean(k * k, axis=-1, keepdims=True)
                kn_scr[rows, cols] = ((k * lax.rsqrt(ms + EPS)) * kg_ref[...]).astype(jnp.bfloat16)
                v = av_ref[0, rows, cols].astype(jnp.float32)
                vt_scr[g, c] = v.T.astype(jnp.bfloat16)
            ki = ik_ref[0, rows, :IDX_DIM].astype(jnp.float32)
            mu = jnp.mean(ki, axis=-1, keepdims=True)
            var = jnp.mean(jnp.square(ki - mu), axis=-1, keepdims=True)
            y = (ki - mu) * lax.rsqrt(var + EPS)
            ikn_scr[rows, :] = (y * lnw_ref[...] + lnb_ref[...]).astype(jnp.bfloat16)
            return carry
        lax.fori_loop(0, seq // tk, body, 0)

    t0 = i * tq
    n_kt = (t0 + tq) // tk
    scale = (HEAD_DIM ** -0.5) * math.log2(math.e)
    for h in range(ATTN_HEADS):
        g, r = divmod(h, HEADS_PER_KV)
        q = aq_ref[0, :, h * HEAD_DIM:(h + 1) * HEAD_DIM].astype(jnp.float32)
        ms = jnp.mean(q * q, axis=-1, keepdims=True)
        qn_scr[g, r * tq:(r + 1) * tq, :] = (
            (q * lax.rsqrt(ms + EPS)) * qg_ref[...] * scale).astype(jnp.bfloat16)
    for h in range(IDX_HEADS):
        iqh_scr[h] = iq_ref[0, :, h * IDX_DIM:(h + 1) * IDX_DIM]
    wt_scr[...] = iw_ref[0].astype(jnp.float32).T * idx_w_scale

    q_chunk = (t0 + lax.broadcasted_iota(jnp.int32, (tk, tq), 1)) >> chunk_shift

    def score_body(kt, carry):
        rows = pl.ds(pl.multiple_of(kt * tk, tk), tk)
        ik_t = ikn_scr[rows, :]
        acc = jnp.zeros((tk, tq), jnp.float32)
        for h in range(IDX_HEADS):
            d = _dot_nt(ik_t, iqh_scr[h])
            acc = acc + jnp.maximum(d, 0.0) * wt_scr[h:h + 1, :]
        k_chunk = (kt * tk + lax.broadcasted_iota(jnp.int32, (tk, tq), 0)) >> chunk_shift
        key_scr[rows, :] = jnp.where(k_chunk <= q_chunk, acc, -jnp.inf)
        return carry
    lax.fori_loop(0, n_kt, score_body, 0)

    def bit_body(it, lo):
        cand = lo + lax.shift_left(jnp.int32(1), 31 - it)
        cand_f = _ordered_float(cand)

        def count_body(kt, part):
            rows = pl.ds(pl.multiple_of(kt * tk, tk), tk)
            hit = jnp.where(key_scr[rows, :] >= cand_f, 1.0, 0.0).reshape(tk // 8, 8, tq)
            while hit.shape[0] > 1:
                half = hit.shape[0] // 2
                hit = hit[:half] + hit[half:]
            return part + hit[0]
        part = lax.fori_loop(0, n_kt, count_body, jnp.zeros((8, tq), jnp.float32))
        cnt = jnp.sum(part, axis=0, keepdims=True)
        return jnp.where(cnt >= float(topk), cand, lo)
    lo = lax.fori_loop(0, 32, bit_body, jnp.full((1, tq), INT_MIN, jnp.int32))
    thr = jnp.where(lo == INT_MIN, jnp.finfo(jnp.float32).min, _ordered_float(lo))

    m_scr[...] = jnp.full(m_scr.shape, NEG_BIG, jnp.float32)
    l_scr[...] = jnp.zeros(l_scr.shape, jnp.float32)
    acc_scr[...] = jnp.zeros(acc_scr.shape, jnp.float32)

    def attn_body(kt, carry):
        rows = pl.ds(pl.multiple_of(kt * tk, tk), tk)
        bias = jnp.where(key_scr[rows, :] >= thr, 0.0, NEG_BIG)
        bias = jnp.concatenate([bias] * HEADS_PER_KV, axis=1)
        for g in range(KV_HEADS):
            k_t = kn_scr[rows, g * HEAD_DIM:(g + 1) * HEAD_DIM]
            s = _dot_nt(k_t, qn_scr[g]) + bias
            m_old = m_scr[g]
            m_new = jnp.maximum(m_old, jnp.max(s, axis=0, keepdims=True))
            p = jnp.exp2(s - m_new)
            alpha = jnp.exp2(m_old - m_new)
            l_scr[g] = alpha * l_scr[g] + jnp.sum(p, axis=0, keepdims=True)
            acc_scr[g] = acc_scr[g] * alpha + _dot(vt_scr[g, kt], p.astype(jnp.bfloat16))
            m_scr[g] = m_new
        return carry
    lax.fori_loop(0, n_kt, attn_body, 0)

    for h in range(ATTN_HEADS):
        g, r = divmod(h, HEADS_PER_KV)
        o = acc_scr[g, :, r * tq:(r + 1) * tq] / l_scr[g, :, r * tq:(r + 1) * tq]
        o_ref[0, :, h * HEAD_DIM:(h + 1) * HEAD_DIM] = o.T.astype(o_ref.dtype)


def _dsa_attention(p3, q_g, k_g, ln_w, ln_b, *, tq, tk):
    b, seq, _ = p3.shape
    topk = min(TOPK_MAX, seq // 4)
    idx_w_scale = (IDX_HEADS ** -0.5) * (IDX_DIM ** -0.5)
    assert seq % tq == 0 and tq % tk == 0 and tk % CHUNK == 0

    def col(off, width):
        return off // width

    return pl.pallas_call(
        functools.partial(_attn_kernel, tk=tk, topk=topk, idx_w_scale=idx_w_scale),
        out_shape=jax.ShapeDtypeStruct((b, seq, ATTN_WIDTH), jnp.bfloat16),
        grid=(b, seq // tq),
        in_specs=[
            pl.BlockSpec((1, tq, ATTN_WIDTH), lambda bi, i: (bi, i, col(AQ_OFF, ATTN_WIDTH))),
            pl.BlockSpec((1, tq, IDX_WIDTH), lambda bi, i: (bi, i, col(IQ_OFF, IDX_WIDTH))),
            pl.BlockSpec((1, tq, LANES), lambda bi, i: (bi, i, col(IW_OFF, LANES))),
            pl.BlockSpec((1, seq, KV_WIDTH), lambda bi, i: (bi, 0, col(AK_OFF, KV_WIDTH))),
            pl.BlockSpec((1, seq, KV_WIDTH), lambda bi, i: (bi, 0, col(AV_OFF, KV_WIDTH))),
            pl.BlockSpec((1, seq, LANES), lambda bi, i: (bi, 0, col(IK_OFF, LANES))),
            pl.BlockSpec((1, HEAD_DIM), lambda bi, i: (0, 0)),
            pl.BlockSpec((1, HEAD_DIM), lambda bi, i: (0, 0)),
            pl.BlockSpec((1, IDX_DIM), lambda bi, i: (0, 0)),
            pl.BlockSpec((1, IDX_DIM), lambda bi, i: (0, 0)),
        ],
        out_specs=pl.BlockSpec((1, tq, ATTN_WIDTH), lambda bi, i: (bi, i, 0)),
        scratch_shapes=[
            pltpu.VMEM((seq, KV_WIDTH), jnp.bfloat16),
            pltpu.VMEM((seq, IDX_DIM), jnp.bfloat16),
            pltpu.VMEM((KV_HEADS, seq // tk, HEAD_DIM, tk), jnp.bfloat16),
            pltpu.VMEM((seq, tq), jnp.float32),
            pltpu.VMEM((IDX_HEADS, tq, IDX_DIM), jnp.bfloat16),
            pltpu.VMEM((LANES, tq), jnp.float32),
            pltpu.VMEM((KV_HEADS, HEADS_PER_KV * tq, HEAD_DIM), jnp.bfloat16),
            pltpu.VMEM((KV_HEADS, HEAD_DIM, HEADS_PER_KV * tq), jnp.float32),
            pltpu.VMEM((KV_HEADS, 1, HEADS_PER_KV * tq), jnp.float32),
            pltpu.VMEM((KV_HEADS, 1, HEADS_PER_KV * tq), jnp.float32),
        ],
        compiler_params=_params("arbitrary", "arbitrary"),
        name="dsa_attn",
    )(p3, p3, p3, p3, p3, p3, q_g, k_g, ln_w, ln_b)


def _ret_kernel(lg_ref, rq_ref, rk_ref, rv_ref, rg_ref, cos_ref, sin_ref, g_ref, o_ref, *, rc):
    h = pl.program_id(1)
    lg = lg_ref[h]
    seq = rq_ref.shape[1]
    n = lax.broadcasted_iota(jnp.int32, (rc, RET_DV), 0).astype(jnp.float32)
    cross_decay = jnp.exp(lg * (n + 1.0))
    state_decay = jnp.exp(lg * (rc - 1.0 - n))
    chunk_decay = jnp.exp(lg * jnp.full((RET_DK, RET_DV), float(rc), jnp.float32))
    rel = (lax.broadcasted_iota(jnp.int32, (rc, rc), 0)
           - lax.broadcasted_iota(jnp.int32, (rc, rc), 1)).astype(jnp.float32)
    intra = jnp.where(rel >= 0, jnp.exp(lg * jnp.maximum(rel, 0.0)), 0.0)

    def rot(x, rows):
        return x * cos_ref[rows, :] + pltpu.roll(x, RET_DK // 2, 1) * sin_ref[rows, :]

    state = jnp.zeros((RET_DK, RET_DV), jnp.float32)
    for c in range(seq // rc):
        rows = slice(c * rc, (c + 1) * rc)
        q = rot(rq_ref[0, rows, :].astype(jnp.float32), rows)
        k = rot(rk_ref[0, rows, :].astype(jnp.float32), rows) * (RET_DK ** -0.5)
        v = rv_ref[0, rows, :]
        qb = q.astype(jnp.bfloat16)
        inner = _dot_nt(qb, k.astype(jnp.bfloat16)) * intra
        o = _dot(inner.astype(jnp.bfloat16), v) + _dot(qb, state.astype(jnp.bfloat16)) * cross_decay
        kd_t = (k * state_decay).T.astype(jnp.bfloat16)
        state = state * chunk_decay + _dot(kd_t, v)
        ms = jnp.mean(o * o, axis=-1, keepdims=True)
        y = (o * lax.rsqrt(ms + EPS)) * g_ref[0]
        gate = rg_ref[0, rows, :].astype(jnp.float32)
        o_ref[0, rows, :] = ((gate * (1.0 / (1.0 + jnp.exp(-gate)))) * y).astype(o_ref.dtype)


def _retention(p3, log_gamma, cos2, sin2, ret_g, *, rc):
    b, seq, _ = p3.shape
    assert seq % rc == 0

    def head_spec(off):
        return pl.BlockSpec((1, seq, RET_DK), lambda bi, h: (bi, 0, off // RET_DK + h))

    return pl.pallas_call(
        functools.partial(_ret_kernel, rc=rc),
        out_shape=jax.ShapeDtypeStruct((b, seq, RET_WIDTH), jnp.bfloat16),
        grid=(b, RET_HEADS),
        in_specs=[
            pl.BlockSpec(memory_space=pltpu.SMEM),
            head_spec(RQ_OFF), head_spec(RK_OFF), head_spec(RV_OFF), head_spec(RG_OFF),
            pl.BlockSpec((seq, RET_DK), lambda bi, h: (0, 0)),
            pl.BlockSpec((seq, RET_DK), lambda bi, h: (0, 0)),
            pl.BlockSpec((1, 1, RET_DV), lambda bi, h: (h, 0, 0)),
        ],
        out_specs=pl.BlockSpec((1, seq, RET_DV), lambda bi, h: (bi, 0, h)),
        compiler_params=_params("arbitrary", "arbitrary"),
        name="retention",
    )(log_gamma, p3, p3, p3, p3, cos2, sin2, ret_g)


def _routing(logits):
    lane = lax.broadcasted_iota(jnp.int32, logits.shape, 1).astype(jnp.float32)
    big = float(LANES)
    neg = -jnp.inf

    def first_argmax(v, vmax):
        return jnp.min(jnp.where(v == vmax, lane, big), axis=-1, keepdims=True)

    g_mask = (lane >= N_EXPERTS) & (lane < N_EXPERTS + N_GROUPS)
    gl = jnp.where(g_mask, logits, neg)
    g_max = jnp.max(gl, axis=-1, keepdims=True)
    g_sel = first_argmax(gl, g_max) - N_EXPERTS
    g_gate = 1.0 / jnp.sum(jnp.where(g_mask, jnp.exp(gl - g_max), 0.0), axis=-1, keepdims=True)

    e_lo = g_sel * EXPERTS_PER_GROUP
    el = jnp.where((lane >= e_lo) & (lane < e_lo + EXPERTS_PER_GROUP), logits, neg)
    v1 = jnp.max(el, axis=-1, keepdims=True)
    i1 = first_argmax(el, v1)
    el2 = jnp.where(lane == i1, neg, el)
    v2 = jnp.max(el2, axis=-1, keepdims=True)
    i2 = first_argmax(el2, v2)
    e2 = jnp.exp(v2 - v1)
    denom = 1.0 + e2
    w1 = (1.0 / denom) * g_gate
    w2 = (e2 / denom) * g_gate
    route = jnp.where(lane == ROUTE_E1, i1, 0.0) + jnp.where(lane == ROUTE_E2, i2, 0.0)
    return route + jnp.where(lane == ROUTE_W1, w1, 0.0) + jnp.where(lane == ROUTE_W2, w2, 0.0)


def _norm2(x1, g):
    ms = jnp.mean(x1 * x1, axis=-1, keepdims=True)
    return (x1 * lax.rsqrt(ms + EPS)) * g


def _out_proj_kernel(a_ref, r_ref, x_ref, wa_ref, wr_ref, g_ref, rhi_ref, rlo_ref,
                     x1_ref, h2_ref, route_ref):
    mixed = _dot(a_ref[...], wa_ref[...]) + _dot(r_ref[...], wr_ref[...])
    x1 = x_ref[...] + mixed
    x1_ref[...] = x1
    h2 = _norm2(x1, g_ref[...])
    h2_ref[...] = h2
    hi = h2.astype(jnp.bfloat16)
    lo = (h2 - hi.astype(jnp.float32)).astype(jnp.bfloat16)
    logits = _dot(hi, rhi_ref[...]) + (_dot(hi, rlo_ref[...]) + _dot(lo, rhi_ref[...]))
    route_ref[...] = _routing(logits)


def _out_proj(attn2d, ret2d, x2d, w_out_bf, g2, r_hi, r_lo, *, tm):
    n, d = x2d.shape
    return pl.pallas_call(
        _out_proj_kernel,
        out_shape=(
            jax.ShapeDtypeStruct((n, d), jnp.float32),
            jax.ShapeDtypeStruct((n, d), jnp.float32),
            jax.ShapeDtypeStruct((n, LANES), jnp.float32),
        ),
        grid=(n // tm,),
        in_specs=[
            pl.BlockSpec((tm, ATTN_WIDTH), lambda i: (i, 0)),
            pl.BlockSpec((tm, RET_WIDTH), lambda i: (i, 0)),
            pl.BlockSpec((tm, d), lambda i: (i, 0)),
            pl.BlockSpec((ATTN_WIDTH, d), lambda i: (0, 0)),
            pl.BlockSpec((RET_WIDTH, d), lambda i: (ATTN_WIDTH // RET_WIDTH, 0)),
            pl.BlockSpec((1, d), lambda i: (0, 0)),
            pl.BlockSpec((d, LANES), lambda i: (0, 0)),
            pl.BlockSpec((d, LANES), lambda i: (0, 0)),
        ],
        out_specs=(
            pl.BlockSpec((tm, d), lambda i: (i, 0)),
            pl.BlockSpec((tm, d), lambda i: (i, 0)),
            pl.BlockSpec((tm, LANES), lambda i: (i, 0)),
        ),
        compiler_params=_params("arbitrary"),
        name="out_proj",
    )(attn2d, ret2d, x2d, w_out_bf, w_out_bf, g2, r_hi, r_lo)


def _plan_kernel(route_ref, pos_ref, cnt_ref, rank_scr, *, tm, blk):
    n = route_ref.shape[0]
    lane = lax.broadcasted_iota(jnp.int32, (blk, LANES), 1).astype(jnp.float32)
    before = (lax.broadcasted_iota(jnp.int32, (blk, blk), 1)
              < lax.broadcasted_iota(jnp.int32, (blk, blk), 0)).astype(jnp.bfloat16)

    def one_hot(rows):
        r = route_ref[rows, :]
        e1 = r[:, ROUTE_E1:ROUTE_E1 + 1]
        e2 = r[:, ROUTE_E2:ROUTE_E2 + 1]
        return lane == e1, lane == e2

    def rank_body(b, run):
        rows = pl.ds(pl.multiple_of(b * blk, blk), blk)
        m1, m2 = one_hot(rows)
        sel = jnp.where(m1 | m2, 1.0, 0.0)
        rank_scr[rows, :] = _dot(before, sel.astype(jnp.bfloat16)) + run
        return run + jnp.sum(sel, axis=0, keepdims=True)
    cnt = lax.fori_loop(0, n // blk, rank_body, jnp.zeros((1, LANES), jnp.float32))
    cnt_ref[...] = jnp.broadcast_to(cnt, cnt_ref.shape)

    tiles = jnp.floor((cnt + (tm - 1.0)) * (1.0 / tm))
    below = (lax.broadcasted_iota(jnp.int32, (LANES, LANES), 0)
             < lax.broadcasted_iota(jnp.int32, (LANES, LANES), 1)).astype(jnp.bfloat16)
    start = _dot(jnp.broadcast_to(tiles, (8, LANES)).astype(jnp.bfloat16), below)[0:1, :] * float(tm)

    def pos_body(b, carry):
        rows = pl.ds(pl.multiple_of(b * blk, blk), blk)
        m1, m2 = one_hot(rows)
        dest = rank_scr[rows, :] + start
        p1 = jnp.sum(jnp.where(m1, dest, 0.0), axis=-1, keepdims=True)
        p2 = jnp.sum(jnp.where(m2, dest, 0.0), axis=-1, keepdims=True)
        pos_ref[rows, :] = (jnp.where(lane == 0.0, p1, 0.0) + jnp.where(lane == 1.0, p2, 0.0)).astype(jnp.int32)
        return carry
    lax.fori_loop(0, n // blk, pos_body, 0)


def _moe_plan(route, *, tm, blk):
    n = route.shape[0]
    return pl.pallas_call(
        functools.partial(_plan_kernel, tm=tm, blk=blk),
        out_shape=(jax.ShapeDtypeStruct((n, LANES), jnp.int32),
                   jax.ShapeDtypeStruct((8, LANES), jnp.float32)),
        scratch_shapes=[pltpu.VMEM((n, LANES), jnp.float32)],
        compiler_params=pltpu.CompilerParams(vmem_limit_bytes=VMEM_LIMIT),
        name="moe_plan",
    )(route)


def _row_copy(src, src_row, dst, dst_row, sem):
    return pltpu.make_async_copy(src.at[pl.ds(src_row, 1), :], dst.at[pl.ds(dst_row, 1), :], sem)


def _invert_kernel(pos_ref, sid_ref):
    def body(j, carry):
        sid_ref[pos_ref[j]] = j
        return carry
    lax.fori_loop(0, pos_ref.shape[0], body, 0, unroll=8)


def _moe_invert(pos, *, rows):
    return pl.pallas_call(
        _invert_kernel,
        out_shape=jax.ShapeDtypeStruct((rows,), jnp.int32),
        in_specs=[pl.BlockSpec(memory_space=pltpu.SMEM)],
        out_specs=pl.BlockSpec(memory_space=pltpu.SMEM),
        name="moe_invert",
    )(pos)


ROW_GROUP = 8


def _ffn_kernel(te_ref, tv_ref, nu_ref, sid_ref, h2_ref, w1_ref, w3_ref, w2_ref, y2_ref,
                xbuf, ybuf, gsem, ssem):
    t = pl.program_id(0)
    n_used = nu_ref[0]
    tm = xbuf.shape[1]

    def for_rows(tile, fn):
        valid = tv_ref[tile]

        def body(c, carry):
            for u in range(ROW_GROUP):
                r = c * ROW_GROUP + u

                @pl.when(r < valid)
                def _():
                    fn(r, sid_ref[tile * tm + r])
            return carry
        lax.fori_loop(0, (valid + (ROW_GROUP - 1)) // ROW_GROUP, body, 0)

    def gather(tile, slot):
        return lambda r, sid: _row_copy(h2_ref, sid >> 1, xbuf.at[slot], r, gsem.at[slot])

    def scatter(tile, slot):
        return lambda r, sid: _row_copy(ybuf.at[slot], r, y2_ref, sid, ssem.at[slot])

    def start_all(tile, copy):
        for_rows(tile, lambda r, sid: copy(r, sid).start())

    def wait_all(tile, copy):
        for_rows(tile, lambda r, sid: copy(r, sid).wait())

    @pl.when(t == 0)
    def _():
        start_all(0, gather(0, 0))

    @pl.when(t < n_used)
    def _():
        slot = t % 2

        @pl.when(t + 1 < n_used)
        def _():
            start_all(t + 1, gather(t + 1, 1 - slot))

        wait_all(t, gather(t, slot))

        @pl.when(t >= 2)
        def _():
            wait_all(t - 2, scatter(t - 2, slot))

        row = lax.broadcasted_iota(jnp.int32, xbuf.shape[1:], 0)
        x = jnp.where(row < tv_ref[t], xbuf[slot], 0.0).astype(jnp.bfloat16)
        a = _dot(x, w1_ref[0].astype(jnp.bfloat16))
        b = _dot(x, w3_ref[0].astype(jnp.bfloat16))
        act = (a * (1.0 / (1.0 + jnp.exp(-a)))) * b
        ybuf[slot] = _dot(act.astype(jnp.bfloat16), w2_ref[0].astype(jnp.bfloat16))
        start_all(t, scatter(t, slot))

    @pl.when(t == pl.num_programs(0) - 1)
    def _():
        @pl.when(n_used >= 2)
        def _():
            wait_all(n_used - 2, scatter(n_used - 2, n_used % 2))
        wait_all(n_used - 1, scatter(n_used - 1, (n_used - 1) % 2))


def _moe_ffn(tile_expert, tile_valid, n_used, row_sid, h2, w1, w3, w2, *, tm):
    n, d = h2.shape
    _, _, ff = w1.shape
    max_tiles = tile_expert.shape[0]

    def expert(t, te, tv, nu, sid):
        return te[jnp.minimum(t, nu[0] - 1)]

    return pl.pallas_call(
        _ffn_kernel,
        out_shape=jax.ShapeDtypeStruct((2 * n, d), jnp.float32),
        grid_spec=pltpu.PrefetchScalarGridSpec(
            num_scalar_prefetch=4,
            grid=(max_tiles,),
            in_specs=[
                pl.BlockSpec(memory_space=pl.ANY),
                pl.BlockSpec((1, d, ff), lambda *a: (expert(*a), 0, 0)),
                pl.BlockSpec((1, d, ff), lambda *a: (expert(*a), 0, 0)),
                pl.BlockSpec((1, ff, d), lambda *a: (expert(*a), 0, 0)),
            ],
            out_specs=pl.BlockSpec(memory_space=pl.ANY),
            scratch_shapes=[
                pltpu.VMEM((2, tm, d), jnp.float32),
                pltpu.VMEM((2, tm, d), jnp.float32),
                pltpu.SemaphoreType.DMA((2,)),
                pltpu.SemaphoreType.DMA((2,)),
            ],
        ),
        compiler_params=_params("arbitrary"),
        name="moe_ffn",
    )(tile_expert, tile_valid, n_used, row_sid, h2, w1, w3, w2)


def _combine_kernel(x1_ref, route_ref, y_ref, o_ref):
    d = x1_ref.shape[1]
    route = route_ref[...]
    w1 = route[:, ROUTE_W1:ROUTE_W1 + 1]
    w2 = route[:, ROUTE_W2:ROUTE_W2 + 1]
    o_ref[...] = x1_ref[...] + (w1 * y_ref[:, :d] + w2 * y_ref[:, d:])


def _moe_combine(x1, route, y2, *, tm):
    n, d = x1.shape
    return pl.pallas_call(
        _combine_kernel,
        out_shape=jax.ShapeDtypeStruct((n, d), jnp.float32),
        grid=(n // tm,),
        in_specs=[pl.BlockSpec((tm, d), lambda i: (i, 0)),
                  pl.BlockSpec((tm, LANES), lambda i: (i, 0)),
                  pl.BlockSpec((tm, 2 * d), lambda i: (i, 0))],
        out_specs=pl.BlockSpec((tm, d), lambda i: (i, 0)),
        compiler_params=_params("arbitrary"),
        name="moe_combine",
    )(x1, route, y2.reshape(n, 2 * d))


def _moe(x1, h2, route, w1, w3, w2, *, tm, gather_tm):
    n, d = x1.shape
    max_tiles = (2 * n) // tm + N_EXPERTS
    pos2d, cnt = _moe_plan(route, tm=tm, blk=gather_tm)
    pos = pos2d[:, :2].reshape(2 * n)
    counts = cnt[0, :N_EXPERTS].astype(jnp.int32)
    tiles = (counts + (tm - 1)) // tm
    ends = jnp.cumsum(tiles)
    t_idx = jnp.arange(max_tiles, dtype=jnp.int32)
    tile_expert = jnp.sum((ends[None, :] <= t_idx[:, None]).astype(jnp.int32), axis=1)
    tile_expert = jnp.minimum(tile_expert, N_EXPERTS - 1)
    first_tile = (ends - tiles)[tile_expert]
    tile_valid = jnp.clip(counts[tile_expert] - (t_idx - first_tile) * tm, 0, tm).astype(jnp.int32)
    n_used = ends[-1:].astype(jnp.int32)

    row_sid = _moe_invert(pos, rows=max_tiles * tm)
    y2 = _moe_ffn(tile_expert, tile_valid, n_used, row_sid, h2, w1, w3, w2, tm=tm)
    return _moe_combine(x1, route, y2, tm=gather_tm)


def _pack_w_in(w_in):
    d = w_in.shape[0]
    sizes = (ATTN_WIDTH, KV_WIDTH, KV_WIDTH, IDX_WIDTH, IDX_DIM, IDX_HEADS,
             RET_WIDTH, RET_WIDTH, RET_WIDTH, RET_WIDTH)
    parts, start = [], 0
    for s in sizes:
        parts.append(w_in[:, start:start + s])
        start += s
    aq, ak, av, iq, ik, iw, rq, rk, rv, rg = parts

    def z(width):
        return jnp.zeros((d, width), w_in.dtype)

    packed = jnp.concatenate(
        [aq, iq, ak, av, ik, z(LANES - IDX_DIM), iw, z(LANES - IDX_HEADS), z(RQ_OFF - IW_OFF - LANES),
         rq, rk, rv, rg], axis=1)
    assert packed.shape[1] == PROJ_WIDTH
    return packed.astype(jnp.bfloat16)


def _pack_router(w_group, w_router):
    d = w_group.shape[0]
    experts = jnp.transpose(w_router, (1, 0, 2)).reshape(d, N_EXPERTS)
    wr = jnp.concatenate(
        [experts, w_group, jnp.zeros((d, LANES - N_EXPERTS - N_GROUPS), w_group.dtype)], axis=1)
    hi = wr.astype(jnp.bfloat16)
    lo = (wr - hi.astype(jnp.float32)).astype(jnp.bfloat16)
    return hi, lo


def _rotation_tables(seq):
    half = RET_DK // 2
    pos = jnp.arange(seq, dtype=jnp.float32)
    inv = 1.0 / (ROT_BASE ** jnp.linspace(0.0, 1.0, half, dtype=jnp.float32))
    ang = pos[:, None] * inv[None, :]
    c, s = jnp.cos(ang), jnp.sin(ang)
    return jnp.concatenate([c, c], axis=-1), jnp.concatenate([-s, s], axis=-1)


def _tiles(n, seq):
    def fit(total, want):
        t = min(total, want)
        while total % t:
            t //= 2
        return t
    return dict(
        proj_tm=fit(n, 1024), proj_tn=1024,
        attn_tq=fit(seq, 256), attn_tk=fit(seq, 256),
        ret_rc=fit(seq, 256),
        out_tm=fit(n, 256),
        moe_tm=fit(n, 256), moe_gather_tm=fit(n, 256),
    )


def kernel(x, norm1_g, w_in, q_norm_g, k_norm_g, idx_k_ln_w, idx_k_ln_b, ret_norm_g,
           w_out, norm2_g, w_group, w_router, w1, w3, w2):
    b, seq, d = x.shape
    n = b * seq
    depth = w_in.shape[0]
    t = _tiles(n, seq)
    cos2, sin2 = _rotation_tables(seq)
    log_gamma = jnp.log1p(-jnp.exp2(-5.0 - jnp.arange(RET_HEADS, dtype=jnp.float32)))

    x2d = x.reshape(n, d)
    for l in range(depth):
        proj = _in_proj(x2d, norm1_g[l][None, :], _pack_w_in(w_in[l]),
                        tm=t["proj_tm"], tn=t["proj_tn"])
        p3 = proj.reshape(b, seq, PROJ_WIDTH)
        attn = _dsa_attention(p3, q_norm_g[l][None, :], k_norm_g[l][None, :],
                              idx_k_ln_w[l][None, :], idx_k_ln_b[l][None, :],
                              tq=t["attn_tq"], tk=t["attn_tk"])
        ret = _retention(p3, log_gamma, cos2, sin2, ret_norm_g[l].reshape(RET_HEADS, 1, RET_DV),
                         rc=t["ret_rc"])
        r_hi, r_lo = _pack_router(w_group[l], w_router[l])
        g2 = norm2_g[l][None, :]
        x1, h2, route = _out_proj(attn.reshape(n, ATTN_WIDTH), ret.reshape(n, RET_WIDTH), x2d,
                                  w_out[l].astype(jnp.bfloat16), g2, r_hi, r_lo, tm=t["out_tm"])
        x2d = _moe(x1, h2, route, w1[l], w3[l], w2[l], tm=t["moe_tm"], gather_tm=t["moe_gather_tm"])
    return x2d.reshape(b, seq, d)
```

```python
import functools
import math

import jax
import jax.numpy as jnp
from jax import lax
from jax.experimental import pallas as pl
from jax.experimental.pallas import tpu as pltpu

CHUNK = 64
ATTN_HEADS = 8
HEAD_DIM = 128
KV_HEADS = 2
HEADS_PER_KV = ATTN_HEADS // KV_HEADS
IDX_HEADS = 16
IDX_DIM = 64
TOPK_MAX = 256
RET_HEADS = 8
RET_DK = 128
RET_DV = 128
ROT_BASE = 10000.0
N_GROUPS = 4
EXPERTS_PER_GROUP = 8
N_EXPERTS = N_GROUPS * EXPERTS_PER_GROUP
EPS = 1e-6

ATTN_WIDTH = ATTN_HEADS * HEAD_DIM
KV_WIDTH = KV_HEADS * HEAD_DIM
IDX_WIDTH = IDX_HEADS * IDX_DIM
RET_WIDTH = RET_HEADS * RET_DK

LANES = 128
VMEM_LIMIT = 56 * 1024 * 1024

AQ_OFF = 0
IQ_OFF = AQ_OFF + ATTN_WIDTH
AK_OFF = IQ_OFF + IDX_WIDTH
AV_OFF = AK_OFF + KV_WIDTH
IK_OFF = AV_OFF + KV_WIDTH
IW_OFF = IK_OFF + LANES
RQ_OFF = 3072
RK_OFF = RQ_OFF + RET_WIDTH
RV_OFF = RK_OFF + RET_WIDTH
RG_OFF = RV_OFF + RET_WIDTH
PROJ_WIDTH = RG_OFF + RET_WIDTH

ROUTE_E1, ROUTE_E2, ROUTE_W1, ROUTE_W2 = 0, 1, 2, 3

INT_MIN = -(2 ** 31)
NEG_BIG = -1e30

_NT = (((1,), (1,)), ((), ()))


def _dot(a, b):
    return jnp.dot(a, b, preferred_element_type=jnp.float32)


def _dot_nt(a, b):
    return lax.dot_general(a, b, _NT, preferred_element_type=jnp.float32)


def _params(*sem):
    return pltpu.CompilerParams(dimension_semantics=sem, vmem_limit_bytes=VMEM_LIMIT)


def _in_proj_kernel(x_ref, g_ref, w_ref, o_ref, h_scr, *, row_chunk):
    @pl.when(pl.program_id(1) == 0)
    def _():
        def body(c, carry):
            rows = pl.ds(pl.multiple_of(c * row_chunk, row_chunk), row_chunk)
            x = x_ref[rows, :]
            ms = jnp.mean(x * x, axis=-1, keepdims=True)
            h_scr[rows, :] = ((x * lax.rsqrt(ms + EPS)) * g_ref[...]).astype(jnp.bfloat16)
            return carry
        lax.fori_loop(0, x_ref.shape[0] // row_chunk, body, 0)

    o_ref[...] = _dot(h_scr[...], w_ref[...]).astype(o_ref.dtype)


def _in_proj(x2d, g, wp, *, tm, tn):
    n, d = x2d.shape
    pw = wp.shape[1]
    return pl.pallas_call(
        functools.partial(_in_proj_kernel, row_chunk=min(tm, 128)),
        out_shape=jax.ShapeDtypeStruct((n, pw), jnp.bfloat16),
        grid=(n // tm, pw // tn),
        in_specs=[
            pl.BlockSpec((tm, d), lambda i, j: (i, 0)),
            pl.BlockSpec((1, d), lambda i, j: (0, 0)),
            pl.BlockSpec((d, tn), lambda i, j: (0, j)),
        ],
        out_specs=pl.BlockSpec((tm, tn), lambda i, j: (i, j)),
        scratch_shapes=[pltpu.VMEM((tm, d), jnp.bfloat16)],
        compiler_params=_params("arbitrary", "arbitrary"),
        name="in_proj",
    )(x2d, g, wp)


def _ordered_float(v):
    bits = v ^ ((v >> 31) & jnp.int32(0x7FFFFFFF))
    return pltpu.bitcast(bits, jnp.float32)


def _attn_kernel(aq_ref, iq_ref, iw_ref, ak_ref, av_ref, ik_ref, qg_ref, kg_ref, lnw_ref, lnb_ref,
                 o_ref,
                 kn_scr, ikn_scr, vt_scr, key_scr, iqh_scr, wt_scr, qn_scr, acc_scr, m_scr, l_scr,
                 *, tk, topk, idx_w_scale):
    i = pl.program_id(1)
    seq = ak_ref.shape[1]
    tq = aq_ref.shape[1]
    chunk_shift = CHUNK.bit_length() - 1

    @pl.when(i == 0)
    def _():
        def body(c, carry):
            rows = pl.ds(pl.multiple_of(c * tk, tk), tk)
            for g in range(KV_HEADS):
                cols = slice(g * HEAD_DIM, (g + 1) * HEAD_DIM)
                k = ak_ref[0, rows, cols].astype(jnp.float32)
                ms = jnp.mean(k * k, axis=-1, keepdims=True)
                kn_scr[rows, cols] = ((k * lax.rsqrt(ms + EPS)) * kg_ref[...]).astype(jnp.bfloat16)
                v = av_ref[0, rows, cols].astype(jnp.float32)
                vt_scr[g, c] = v.T.astype(jnp.bfloat16)
            ki = ik_ref[0, rows, :IDX_DIM].astype(jnp.float32)
            mu = jnp.mean(ki, axis=-1, keepdims=True)
            var = jnp.mean(jnp.square(ki - mu), axis=-1, keepdims=True)
            y = (ki - mu) * lax.rsqrt(var + EPS)
            ikn_scr[rows, :] = (y * lnw_ref[...] + lnb_ref[...]).astype(jnp.bfloat16)
            return carry
        lax.fori_loop(0, seq // tk, body, 0)

    t0 = i * tq
    n_kt = (t0 + tq) // tk
    scale = (HEAD_DIM ** -0.5) * math.log2(math.e)
    for h in range(ATTN_HEADS):
        g, r = divmod(h, HEADS_PER_KV)
        q = aq_ref[0, :, h * HEAD_DIM:(h + 1) * HEAD_DIM].astype(jnp.float32)
        ms = jnp.mean(q * q, axis=-1, keepdims=True)
        qn_scr[g, r * tq:(r + 1) * tq, :] = (
            (q * lax.rsqrt(ms + EPS)) * qg_ref[...] * scale).astype(jnp.bfloat16)
    for h in range(IDX_HEADS):
        iqh_scr[h] = iq_ref[0, :, h * IDX_DIM:(h + 1) * IDX_DIM]
    wt_scr[...] = iw_ref[0].astype(jnp.float32).T * idx_w_scale

    q_chunk = (t0 + lax.broadcasted_iota(jnp.int32, (tk, tq), 1)) >> chunk_shift

    def score_body(kt, carry):
        rows = pl.ds(pl.multiple_of(kt * tk, tk), tk)
        ik_t = ikn_scr[rows, :]
        acc = jnp.zeros((tk, tq), jnp.float32)
        for h in range(IDX_HEADS):
            d = _dot_nt(ik_t, iqh_scr[h])
            acc = acc + jnp.maximum(d, 0.0) * wt_scr[h:h + 1, :]
        k_chunk = (kt * tk + lax.broadcasted_iota(jnp.int32, (tk, tq), 0)) >> chunk_shift
        key_scr[rows, :] = jnp.where(k_chunk <= q_chunk, acc, -jnp.inf)
        return carry
    lax.fori_loop(0, n_kt, score_body, 0)

    def bit_body(it, lo):
        cand = lo + lax.shift_left(jnp.int32(1), 31 - it)
        cand_f = _ordered_float(cand)

        def count_body(kt, part):
            rows = pl.ds(pl.multiple_of(kt * tk, tk), tk)
            hit = jnp.where(key_scr[rows, :] >= cand_f, 1.0, 0.0).reshape(tk // 8, 8, tq)
            while hit.shape[0] > 1:
                half = hit.shape[0] // 2
                hit = hit[:half] + hit[half:]
            return part + hit[0]
        part = lax.fori_loop(0, n_kt, count_body, jnp.zeros((8, tq), jnp.float32))
        cnt = jnp.sum(part, axis=0, keepdims=True)
        return jnp.where(cnt >= float(topk), cand, lo)
    lo = lax.fori_loop(0, 32, bit_body, jnp.full((1, tq), INT_MIN, jnp.int32))
    thr = jnp.where(lo == INT_MIN, jnp.finfo(jnp.float32).min, _ordered_float(lo))

    m_scr[...] = jnp.full(m_scr.shape, NEG_BIG, jnp.float32)
    l_scr[...] = jnp.zeros(l_scr.shape, jnp.float32)
    acc_scr[...] = jnp.zeros(acc_scr.shape, jnp.float32)

    def attn_body(kt, carry):
        rows = pl.ds(pl.multiple_of(kt * tk, tk), tk)
        bias = jnp.where(key_scr[rows, :] >= thr, 0.0, NEG_BIG)
        bias = jnp.concatenate([bias] * HEADS_PER_KV, axis=1)
        for g in range(KV_HEADS):
            k_t = kn_scr[rows, g * HEAD_DIM:(g + 1) * HEAD_DIM]
            s = _dot_nt(k_t, qn_scr[g]) + bias
            m_old = m_scr[g]
            m_new = jnp.maximum(m_old, jnp.max(s, axis=0, keepdims=True))
            p = jnp.exp2(s - m_new)
            alpha = jnp.exp2(m_old - m_new)
            l_scr[g] = alpha * l_scr[g] + jnp.sum(p, axis=0, keepdims=True)
            acc_scr[g] = acc_scr[g] * alpha + _dot(vt_scr[g, kt], p.astype(jnp.bfloat16))
            m_scr[g] = m_new
        return carry
    lax.fori_loop(0, n_kt, attn_body, 0)

    for h in range(ATTN_HEADS):
        g, r = divmod(h, HEADS_PER_KV)
        o = acc_scr[g, :, r * tq:(r + 1) * tq] / l_scr[g, :, r * tq:(r + 1) * tq]
        o_ref[0, :, h * HEAD_DIM:(h + 1) * HEAD_DIM] = o.T.astype(o_ref.dtype)


def _dsa_attention(p3, q_g, k_g, ln_w, ln_b, *, tq, tk):
    b, seq, _ = p3.shape
    topk = min(TOPK_MAX, seq // 4)
    idx_w_scale = (IDX_HEADS ** -0.5) * (IDX_DIM ** -0.5)
    assert seq % tq == 0 and tq % tk == 0 and tk % CHUNK == 0

    def col(off, width):
        return off // width

    return pl.pallas_call(
        functools.partial(_attn_kernel, tk=tk, topk=topk, idx_w_scale=idx_w_scale),
        out_shape=jax.ShapeDtypeStruct((b, seq, ATTN_WIDTH), jnp.bfloat16),
        grid=(b, seq // tq),
        in_specs=[
            pl.BlockSpec((1, tq, ATTN_WIDTH), lambda bi, i: (bi, i, col(AQ_OFF, ATTN_WIDTH))),
            pl.BlockSpec((1, tq, IDX_WIDTH), lambda bi, i: (bi, i, col(IQ_OFF, IDX_WIDTH))),
            pl.BlockSpec((1, tq, LANES), lambda bi, i: (bi, i, col(IW_OFF, LANES))),
            pl.BlockSpec((1, seq, KV_WIDTH), lambda bi, i: (bi, 0, col(AK_OFF, KV_WIDTH))),
            pl.BlockSpec((1, seq, KV_WIDTH), lambda bi, i: (bi, 0, col(AV_OFF, KV_WIDTH))),
            pl.BlockSpec((1, seq, LANES), lambda bi, i: (bi, 0, col(IK_OFF, LANES))),
            pl.BlockSpec((1, HEAD_DIM), lambda bi, i: (0, 0)),
            pl.BlockSpec((1, HEAD_DIM), lambda bi, i: (0, 0)),
            pl.BlockSpec((1, IDX_DIM), lambda bi, i: (0, 0)),
            pl.BlockSpec((1, IDX_DIM), lambda bi, i: (0, 0)),
        ],
        out_specs=pl.BlockSpec((1, tq, ATTN_WIDTH), lambda bi, i: (bi, i, 0)),
        scratch_shapes=[
            pltpu.VMEM((seq, KV_WIDTH), jnp.bfloat16),
            pltpu.VMEM((seq, IDX_DIM), jnp.bfloat16),
            pltpu.VMEM((KV_HEADS, seq // tk, HEAD_DIM, tk), jnp.bfloat16),
            pltpu.VMEM((seq, tq), jnp.float32),
            pltpu.VMEM((IDX_HEADS, tq, IDX_DIM), jnp.bfloat16),
            pltpu.VMEM((LANES, tq), jnp.float32),
            pltpu.VMEM((KV_HEADS, HEADS_PER_KV * tq, HEAD_DIM), jnp.bfloat16),
            pltpu.VMEM((KV_HEADS, HEAD_DIM, HEADS_PER_KV * tq), jnp.float32),
            pltpu.VMEM((KV_HEADS, 1, HEADS_PER_KV * tq), jnp.float32),
            pltpu.VMEM((KV_HEADS, 1, HEADS_PER_KV * tq), jnp.float32),
        ],
        compiler_params=_params("arbitrary", "arbitrary"),
        name="dsa_attn",
    )(p3, p3, p3, p3, p3, p3, q_g, k_g, ln_w, ln_b)


def _ret_kernel(lg_ref, rq_ref, rk_ref, rv_ref, rg_ref, cos_ref, sin_ref, g_ref, o_ref, *, rc):
    h = pl.program_id(1)
    lg = lg_ref[h]
    seq = rq_ref.shape[1]
    n = lax.broadcasted_iota(jnp.int32, (rc, RET_DV), 0).astype(jnp.float32)
    cross_decay = jnp.exp(lg * (n + 1.0))
    state_decay = jnp.exp(lg * (rc - 1.0 - n))
    chunk_decay = jnp.exp(lg * jnp.full((RET_DK, RET_DV), float(rc), jnp.float32))
    rel = (lax.broadcasted_iota(jnp.int32, (rc, rc), 0)
           - lax.broadcasted_iota(jnp.int32, (rc, rc), 1)).astype(jnp.float32)
    intra = jnp.where(rel >= 0, jnp.exp(lg * jnp.maximum(rel, 0.0)), 0.0)

    def rot(x, rows):
        return x * cos_ref[rows, :] + pltpu.roll(x, RET_DK // 2, 1) * sin_ref[rows, :]

    state = jnp.zeros((RET_DK, RET_DV), jnp.float32)
    for c in range(seq // rc):
        rows = slice(c * rc, (c + 1) * rc)
        q = rot(rq_ref[0, rows, :].astype(jnp.float32), rows)
        k = rot(rk_ref[0, rows, :].astype(jnp.float32), rows) * (RET_DK ** -0.5)
        v = rv_ref[0, rows, :]
        qb = q.astype(jnp.bfloat16)
        inner = _dot_nt(qb, k.astype(jnp.bfloat16)) * intra
        o = _dot(inner.astype(jnp.bfloat16), v) + _dot(qb, state.astype(jnp.bfloat16)) * cross_decay
        kd_t = (k * state_decay).T.astype(jnp.bfloat16)
        state = state * chunk_decay + _dot(kd_t, v)
        ms = jnp.mean(o * o, axis=-1, keepdims=True)
        y = (o * lax.rsqrt(ms + EPS)) * g_ref[0]
        gate = rg_ref[0, rows, :].astype(jnp.float32)
        o_ref[0, rows, :] = ((gate * (1.0 / (1.0 + jnp.exp(-gate)))) * y).astype(o_ref.dtype)


def _retention(p3, log_gamma, cos2, sin2, ret_g, *, rc):
    b, seq, _ = p3.shape
    assert seq % rc == 0

    def head_spec(off):
        return pl.BlockSpec((1, seq, RET_DK), lambda bi, h: (bi, 0, off // RET_DK + h))

    return pl.pallas_call(
        functools.partial(_ret_kernel, rc=rc),
        out_shape=jax.ShapeDtypeStruct((b, seq, RET_WIDTH), jnp.bfloat16),
        grid=(b, RET_HEADS),
        in_specs=[
            pl.BlockSpec(memory_space=pltpu.SMEM),
            head_spec(RQ_OFF), head_spec(RK_OFF), head_spec(RV_OFF), head_spec(RG_OFF),
            pl.BlockSpec((seq, RET_DK), lambda bi, h: (0, 0)),
            pl.BlockSpec((seq, RET_DK), lambda bi, h: (0, 0)),
            pl.BlockSpec((1, 1, RET_DV), lambda bi, h: (h, 0, 0)),
        ],
        out_specs=pl.BlockSpec((1, seq, RET_DV), lambda bi, h: (bi, 0, h)),
        compiler_params=_params("arbitrary", "arbitrary"),
        name="retention",
    )(log_gamma, p3, p3, p3, p3, cos2, sin2, ret_g)


def _routing(logits):
    lane = lax.broadcasted_iota(jnp.int32, logits.shape, 1).astype(jnp.float32)
    big = float(LANES)
    neg = -jnp.inf

    def first_argmax(v, vmax):
        return jnp.min(jnp.where(v == vmax, lane, big), axis=-1, keepdims=True)

    g_mask = (lane >= N_EXPERTS) & (lane < N_EXPERTS + N_GROUPS)
    gl = jnp.where(g_mask, logits, neg)
    g_max = jnp.max(gl, axis=-1, keepdims=True)
    g_sel = first_argmax(gl, g_max) - N_EXPERTS
    g_gate = 1.0 / jnp.sum(jnp.where(g_mask, jnp.exp(gl - g_max), 0.0), axis=-1, keepdims=True)

    e_lo = g_sel * EXPERTS_PER_GROUP
    el = jnp.where((lane >= e_lo) & (lane < e_lo + EXPERTS_PER_GROUP), logits, neg)
    v1 = jnp.max(el, axis=-1, keepdims=True)
    i1 = first_argmax(el, v1)
    el2 = jnp.where(lane == i1, neg, el)
    v2 = jnp.max(el2, axis=-1, keepdims=True)
    i2 = first_argmax(el2, v2)
    e2 = jnp.exp(v2 - v1)
    denom = 1.0 + e2
    w1 = (1.0 / denom) * g_gate
    w2 = (e2 / denom) * g_gate
    route = jnp.where(lane == ROUTE_E1, i1, 0.0) + jnp.where(lane == ROUTE_E2, i2, 0.0)
    return route + jnp.where(lane == ROUTE_W1, w1, 0.0) + jnp.where(lane == ROUTE_W2, w2, 0.0)


def _norm2(x1, g):
    ms = jnp.mean(x1 * x1, axis=-1, keepdims=True)
    return (x1 * lax.rsqrt(ms + EPS)) * g


def _out_proj_kernel(a_ref, r_ref, x_ref, wa_ref, wr_ref, g_ref, rhi_ref, rlo_ref,
                     x1_ref, h2_ref, route_ref):
    mixed = _dot(a_ref[...], wa_ref[...]) + _dot(r_ref[...], wr_ref[...])
    x1 = x_ref[...] + mixed
    x1_ref[...] = x1
    h2 = _norm2(x1, g_ref[...])
    h2_ref[...] = h2
    hi = h2.astype(jnp.bfloat16)
    lo = (h2 - hi.astype(jnp.float32)).astype(jnp.bfloat16)
    logits = _dot(hi, rhi_ref[...]) + (_dot(hi, rlo_ref[...]) + _dot(lo, rhi_ref[...]))
    route_ref[...] = _routing(logits)


def _out_proj(attn2d, ret2d, x2d, w_out_bf, g2, r_hi, r_lo, *, tm):
    n, d = x2d.shape
    return pl.pallas_call(
        _out_proj_kernel,
        out_shape=(
            jax.ShapeDtypeStruct((n, d), jnp.float32),
            jax.ShapeDtypeStruct((n, d), jnp.float32),
            jax.ShapeDtypeStruct((n, LANES), jnp.float32),
        ),
        grid=(n // tm,),
        in_specs=[
            pl.BlockSpec((tm, ATTN_WIDTH), lambda i: (i, 0)),
            pl.BlockSpec((tm, RET_WIDTH), lambda i: (i, 0)),
            pl.BlockSpec((tm, d), lambda i: (i, 0)),
            pl.BlockSpec((ATTN_WIDTH, d), lambda i: (0, 0)),
            pl.BlockSpec((RET_WIDTH, d), lambda i: (ATTN_WIDTH // RET_WIDTH, 0)),
            pl.BlockSpec((1, d), lambda i: (0, 0)),
            pl.BlockSpec((d, LANES), lambda i: (0, 0)),
            pl.BlockSpec((d, LANES), lambda i: (0, 0)),
        ],
        out_specs=(
            pl.BlockSpec((tm, d), lambda i: (i, 0)),
            pl.BlockSpec((tm, d), lambda i: (i, 0)),
            pl.BlockSpec((tm, LANES), lambda i: (i, 0)),
        ),
        compiler_params=_params("arbitrary"),
        name="out_proj",
    )(attn2d, ret2d, x2d, w_out_bf, w_out_bf, g2, r_hi, r_lo)


def _plan_kernel(route_ref, pos_ref, cnt_ref, rank_scr, *, tm, blk):
    n = route_ref.shape[0]
    lane = lax.broadcasted_iota(jnp.int32, (blk, LANES), 1).astype(jnp.float32)
    before = (lax.broadcasted_iota(jnp.int32, (blk, blk), 1)
              < lax.broadcasted_iota(jnp.int32, (blk, blk), 0)).astype(jnp.bfloat16)

    def one_hot(rows):
        r = route_ref[rows, :]
        e1 = r[:, ROUTE_E1:ROUTE_E1 + 1]
        e2 = r[:, ROUTE_E2:ROUTE_E2 + 1]
        return lane == e1, lane == e2

    def rank_body(b, run):
        rows = pl.ds(pl.multiple_of(b * blk, blk), blk)
        m1, m2 = one_hot(rows)
        sel = jnp.where(m1 | m2, 1.0, 0.0)
        rank_scr[rows, :] = _dot(before, sel.astype(jnp.bfloat16)) + run
        return run + jnp.sum(sel, axis=0, keepdims=True)
    cnt = lax.fori_loop(0, n // blk, rank_body, jnp.zeros((1, LANES), jnp.float32))
    cnt_ref[...] = jnp.broadcast_to(cnt, cnt_ref.shape)

    tiles = jnp.floor((cnt + (tm - 1.0)) * (1.0 / tm))
    below = (lax.broadcasted_iota(jnp.int32, (LANES, LANES), 0)
             < lax.broadcasted_iota(jnp.int32, (LANES, LANES), 1)).astype(jnp.bfloat16)
    start = _dot(jnp.broadcast_to(tiles, (8, LANES)).astype(jnp.bfloat16), below)[0:1, :] * float(tm)

    def pos_body(b, carry):
        rows = pl.ds(pl.multiple_of(b * blk, blk), blk)
        m1, m2 = one_hot(rows)
        dest = rank_scr[rows, :] + start
        p1 = jnp.sum(jnp.where(m1, dest, 0.0), axis=-1, keepdims=True)
        p2 = jnp.sum(jnp.where(m2, dest, 0.0), axis=-1, keepdims=True)
        pos_ref[rows, :] = (jnp.where(lane == 0.0, p1, 0.0) + jnp.where(lane == 1.0, p2, 0.0)).astype(jnp.int32)
        return carry
    lax.fori_loop(0, n // blk, pos_body, 0)


def _moe_plan(route, *, tm, blk):
    n = route.shape[0]
    return pl.pallas_call(
        functools.partial(_plan_kernel, tm=tm, blk=blk),
        out_shape=(jax.ShapeDtypeStruct((n, LANES), jnp.int32),
                   jax.ShapeDtypeStruct((8, LANES), jnp.float32)),
        scratch_shapes=[pltpu.VMEM((n, LANES), jnp.float32)],
        compiler_params=pltpu.CompilerParams(vmem_limit_bytes=VMEM_LIMIT),
        name="moe_plan",
    )(route)


def _row_copy(src, src_row, dst, dst_row, sem):
    return pltpu.make_async_copy(src.at[pl.ds(src_row, 1), :], dst.at[pl.ds(dst_row, 1), :], sem)


def _invert_kernel(pos_ref, sid_ref):
    def body(j, carry):
        sid_ref[pos_ref[j]] = j
        return carry
    lax.fori_loop(0, pos_ref.shape[0], body, 0, unroll=8)


def _moe_invert(pos, *, rows):
    return pl.pallas_call(
        _invert_kernel,
        out_shape=jax.ShapeDtypeStruct((rows,), jnp.int32),
        in_specs=[pl.BlockSpec(memory_space=pltpu.SMEM)],
        out_specs=pl.BlockSpec(memory_space=pltpu.SMEM),
        name="moe_invert",
    )(pos)


ROW_GROUP = 8


def _ffn_kernel(te_ref, tv_ref, nu_ref, sid_ref, h2_ref, w1_ref, w3_ref, w2_ref, y2_ref,
                xbuf, ybuf, gsem, ssem):
    t = pl.program_id(0)
    n_used = nu_ref[0]
    tm = xbuf.shape[1]

    n_tok = h2_ref.shape[0]

    def gather(slot):
        return lambda r, sid: _row_copy(h2_ref, jnp.where(sid >= n_tok, sid - n_tok, sid),
                                        xbuf.at[slot], r, gsem.at[slot])

    def scatter(slot):
        return lambda r, sid: _row_copy(ybuf.at[slot], r, y2_ref, sid, ssem.at[slot])

    def start_partial(tile, copy):
        valid = tv_ref[tile]

        def body(c, carry):
            for u in range(ROW_GROUP):
                r = c * ROW_GROUP + u

                @pl.when(r < valid)
                def _():
                    copy(r, sid_ref[tile * tm + r]).start()
            return carry
        lax.fori_loop(0, (valid + (ROW_GROUP - 1)) // ROW_GROUP, body, 0)

    def start_all(tile, copy):
        @pl.when(tv_ref[tile] == tm)
        def _():
            for r in range(tm):
                copy(r, sid_ref[tile * tm + r]).start()

        @pl.when(tv_ref[tile] < tm)
        def _():
            start_partial(tile, copy)

    def wait_all(tile, copy, tile_copy):
        @pl.when(tv_ref[tile] == tm)
        def _():
            tile_copy.wait()

        @pl.when(tv_ref[tile] < tm)
        def _():
            def body(r, carry):
                copy(0, 0).wait()
                return carry
            lax.fori_loop(0, tv_ref[tile], body, 0)

    def wait_gather(tile, slot):
        wait_all(tile, gather(slot),
                 pltpu.make_async_copy(h2_ref.at[pl.ds(0, tm), :], xbuf.at[slot], gsem.at[slot]))

    def wait_scatter(tile, slot):
        wait_all(tile, scatter(slot),
                 pltpu.make_async_copy(ybuf.at[slot], y2_ref.at[pl.ds(0, tm), :], ssem.at[slot]))

    @pl.when(t == 0)
    def _():
        start_partial(0, gather(0))

    @pl.when(t < n_used)
    def _():
        slot = t % 2

        @pl.when(t + 1 < n_used)
        def _():
            start_all(t + 1, gather(1 - slot))

        wait_gather(t, slot)

        @pl.when(t >= 2)
        def _():
            wait_scatter(t - 2, slot)

        row = lax.broadcasted_iota(jnp.int32, xbuf.shape[1:], 0)
        x = jnp.where(row < tv_ref[t], xbuf[slot], 0.0).astype(jnp.bfloat16)
        a = _dot(x, w1_ref[0].astype(jnp.bfloat16))
        b = _dot(x, w3_ref[0].astype(jnp.bfloat16))
        act = (a * (1.0 / (1.0 + jnp.exp(-a)))) * b
        ybuf[slot] = _dot(act.astype(jnp.bfloat16), w2_ref[0].astype(jnp.bfloat16))
        start_all(t, scatter(slot))

    @pl.when(t == pl.num_programs(0) - 1)
    def _():
        @pl.when(n_used >= 2)
        def _():
            wait_scatter(n_used - 2, n_used % 2)
        wait_scatter(n_used - 1, (n_used - 1) % 2)


def _moe_ffn(tile_expert, tile_valid, n_used, row_sid, h2, w1, w3, w2, *, tm):
    n, d = h2.shape
    _, _, ff = w1.shape
    max_tiles = tile_expert.shape[0]

    def expert(t, te, tv, nu, sid):
        return te[jnp.minimum(t, nu[0] - 1)]

    return pl.pallas_call(
        _ffn_kernel,
        out_shape=jax.ShapeDtypeStruct((2 * n, d), jnp.float32),
        grid_spec=pltpu.PrefetchScalarGridSpec(
            num_scalar_prefetch=4,
            grid=(max_tiles,),
            in_specs=[
                pl.BlockSpec(memory_space=pl.ANY),
                pl.BlockSpec((1, d, ff), lambda *a: (expert(*a), 0, 0)),
                pl.BlockSpec((1, d, ff), lambda *a: (expert(*a), 0, 0)),
                pl.BlockSpec((1, ff, d), lambda *a: (expert(*a), 0, 0)),
            ],
            out_specs=pl.BlockSpec(memory_space=pl.ANY),
            scratch_shapes=[
                pltpu.VMEM((2, tm, d), jnp.float32),
                pltpu.VMEM((2, tm, d), jnp.float32),
                pltpu.SemaphoreType.DMA((2,)),
                pltpu.SemaphoreType.DMA((2,)),
            ],
        ),
        compiler_params=_params("arbitrary"),
        name="moe_ffn",
    )(tile_expert, tile_valid, n_used, row_sid, h2, w1, w3, w2)


def _combine_kernel(x1_ref, route_ref, ya_ref, yb_ref, o_ref):
    route = route_ref[...]
    w1 = route[:, ROUTE_W1:ROUTE_W1 + 1]
    w2 = route[:, ROUTE_W2:ROUTE_W2 + 1]
    o_ref[...] = x1_ref[...] + (w1 * ya_ref[...] + w2 * yb_ref[...])


def _moe_combine(x1, route, y2, *, tm):
    n, d = x1.shape
    return pl.pallas_call(
        _combine_kernel,
        out_shape=jax.ShapeDtypeStruct((n, d), jnp.float32),
        grid=(n // tm,),
        in_specs=[pl.BlockSpec((tm, d), lambda i: (i, 0)),
                  pl.BlockSpec((tm, LANES), lambda i: (i, 0)),
                  pl.BlockSpec((tm, d), lambda i: (i, 0)),
                  pl.BlockSpec((tm, d), lambda i: (i + n // tm, 0))],
        out_specs=pl.BlockSpec((tm, d), lambda i: (i, 0)),
        compiler_params=_params("arbitrary"),
        name="moe_combine",
    )(x1, route, y2, y2)


def _moe(x1, h2, route, w1, w3, w2, *, tm, gather_tm):
    n, d = x1.shape
    max_tiles = (2 * n) // tm + N_EXPERTS
    pos2d, cnt = _moe_plan(route, tm=tm, blk=gather_tm)
    pos = pos2d[:, :2].T.reshape(2 * n)
    counts = cnt[0, :N_EXPERTS].astype(jnp.int32)
    tiles = (counts + (tm - 1)) // tm
    ends = jnp.cumsum(tiles)
    t_idx = jnp.arange(max_tiles, dtype=jnp.int32)
    tile_expert = jnp.sum((ends[None, :] <= t_idx[:, None]).astype(jnp.int32), axis=1)
    tile_expert = jnp.minimum(tile_expert, N_EXPERTS - 1)
    first_tile = (ends - tiles)[tile_expert]
    tile_valid = jnp.clip(counts[tile_expert] - (t_idx - first_tile) * tm, 0, tm).astype(jnp.int32)
    n_used = ends[-1:].astype(jnp.int32)

    row_sid = _moe_invert(pos, rows=max_tiles * tm)
    y2 = _moe_ffn(tile_expert, tile_valid, n_used, row_sid, h2, w1, w3, w2, tm=tm)
    return _moe_combine(x1, route, y2, tm=gather_tm)


def _pack_w_in(w_in):
    d = w_in.shape[0]
    sizes = (ATTN_WIDTH, KV_WIDTH, KV_WIDTH, IDX_WIDTH, IDX_DIM, IDX_HEADS,
             RET_WIDTH, RET_WIDTH, RET_WIDTH, RET_WIDTH)
    parts, start = [], 0
    for s in sizes:
        parts.append(w_in[:, start:start + s])
        start += s
    aq, ak, av, iq, ik, iw, rq, rk, rv, rg = parts

    def z(width):
        return jnp.zeros((d, width), w_in.dtype)

    packed = jnp.concatenate(
        [aq, iq, ak, av, ik, z(LANES - IDX_DIM), iw, z(LANES - IDX_HEADS), z(RQ_OFF - IW_OFF - LANES),
         rq, rk, rv, rg], axis=1)
    assert packed.shape[1] == PROJ_WIDTH
    return packed.astype(jnp.bfloat16)


def _pack_router(w_group, w_router):
    d = w_group.shape[0]
    experts = jnp.transpose(w_router, (1, 0, 2)).reshape(d, N_EXPERTS)
    wr = jnp.concatenate(
        [experts, w_group, jnp.zeros((d, LANES - N_EXPERTS - N_GROUPS), w_group.dtype)], axis=1)
    hi = wr.astype(jnp.bfloat16)
    lo = (wr - hi.astype(jnp.float32)).astype(jnp.bfloat16)
    return hi, lo


def _rotation_tables(seq):
    half = RET_DK // 2
    pos = jnp.arange(seq, dtype=jnp.float32)
    inv = 1.0 / (ROT_BASE ** jnp.linspace(0.0, 1.0, half, dtype=jnp.float32))
    ang = pos[:, None] * inv[None, :]
    c, s = jnp.cos(ang), jnp.sin(ang)
    return jnp.concatenate([c, c], axis=-1), jnp.concatenate([-s, s], axis=-1)


def _tiles(n, seq):
    def fit(total, want):
        t = min(total, want)
        while total % t:
            t //= 2
        return t
    return dict(
        proj_tm=fit(n, 1024), proj_tn=1024,
        attn_tq=fit(seq, 256), attn_tk=fit(seq, 256),
        ret_rc=fit(seq, 256),
        out_tm=fit(n, 256),
        moe_tm=fit(n, 256), moe_gather_tm=fit(n, 256),
    )


def kernel(x, norm1_g, w_in, q_norm_g, k_norm_g, idx_k_ln_w, idx_k_ln_b, ret_norm_g,
           w_out, norm2_g, w_group, w_router, w1, w3, w2):
    b, seq, d = x.shape
    n = b * seq
    depth = w_in.shape[0]
    t = _tiles(n, seq)
    cos2, sin2 = _rotation_tables(seq)
    log_gamma = jnp.log1p(-jnp.exp2(-5.0 - jnp.arange(RET_HEADS, dtype=jnp.float32)))

    x2d = x.reshape(n, d)
    for l in range(depth):
        proj = _in_proj(x2d, norm1_g[l][None, :], _pack_w_in(w_in[l]),
                        tm=t["proj_tm"], tn=t["proj_tn"])
        p3 = proj.reshape(b, seq, PROJ_WIDTH)
        attn = _dsa_attention(p3, q_norm_g[l][None, :], k_norm_g[l][None, :],
                              idx_k_ln_w[l][None, :], idx_k_ln_b[l][None, :],
                              tq=t["attn_tq"], tk=t["attn_tk"])
        ret = _retention(p3, log_gamma, cos2, sin2, ret_norm_g[l].reshape(RET_HEADS, 1, RET_DV),
                         rc=t["ret_rc"])
        r_hi, r_lo = _pack_router(w_group[l], w_router[l])
        g2 = norm2_g[l][None, :]
        x1, h2, route = _out_proj(attn.reshape(n, ATTN_WIDTH), ret.reshape(n, RET_WIDTH), x2d,
                                  w_out[l].astype(jnp.bfloat16), g2, r_hi, r_lo, tm=t["out_tm"])
        x2d = _moe(x1, h2, route, w1[l], w3[l], w2[l], tm=t["moe_tm"], gather_tm=t["moe_gather_tm"])
    return x2d.reshape(b, seq, d)
```

```python
import functools
import math

import jax
import jax.numpy as jnp
from jax import lax
from jax.experimental import pallas as pl
from jax.experimental.pallas import tpu as pltpu

CHUNK = 64
ATTN_HEADS = 8
HEAD_DIM = 128
KV_HEADS = 2
HEADS_PER_KV = ATTN_HEADS // KV_HEADS
IDX_HEADS = 16
IDX_DIM = 64
TOPK_MAX = 256
RET_HEADS = 8
RET_DK = 128
RET_DV = 128
ROT_BASE = 10000.0
N_GROUPS = 4
EXPERTS_PER_GROUP = 8
N_EXPERTS = N_GROUPS * EXPERTS_PER_GROUP
EPS = 1e-6

ATTN_WIDTH = ATTN_HEADS * HEAD_DIM
KV_WIDTH = KV_HEADS * HEAD_DIM
IDX_WIDTH = IDX_HEADS * IDX_DIM
RET_WIDTH = RET_HEADS * RET_DK

LANES = 128
VMEM_LIMIT = 56 * 1024 * 1024

AQ_OFF = 0
AK_OFF = AQ_OFF + ATTN_WIDTH
AV_OFF = AK_OFF + KV_WIDTH
IQ_OFF = AV_OFF + KV_WIDTH
IK_OFF = IQ_OFF + IDX_WIDTH
IW_OFF = IK_OFF + IDX_DIM
RQ_OFF = IW_OFF + IDX_HEADS
RK_OFF = RQ_OFF + RET_WIDTH
RV_OFF = RK_OFF + RET_WIDTH
RG_OFF = RV_OFF + RET_WIDTH
IN_WIDTH = RG_OFF + RET_WIDTH
RET_LANE = RQ_OFF % LANES
assert RK_OFF % LANES == RET_LANE and RV_OFF % LANES == RET_LANE and RG_OFF % LANES == RET_LANE
assert IW_OFF // LANES == IK_OFF // LANES

ROUTE_E1, ROUTE_E2, ROUTE_W1, ROUTE_W2 = 0, 1, 2, 3

INT_MIN = -(2 ** 31)
NEG_BIG = -1e30

_NT = (((1,), (1,)), ((), ()))


def _dot(a, b):
    return jnp.dot(a, b, preferred_element_type=jnp.float32)


def _dot_nt(a, b):
    return lax.dot_general(a, b, _NT, preferred_element_type=jnp.float32)


def _params(*sem):
    return pltpu.CompilerParams(dimension_semantics=sem, vmem_limit_bytes=VMEM_LIMIT)


def _in_proj_kernel(x_ref, g_ref, w_ref, o_ref, h_scr, *, row_chunk, in_width):
    j = pl.program_id(1)

    @pl.when(j == 0)
    def _():
        def body(c, carry):
            rows = pl.ds(pl.multiple_of(c * row_chunk, row_chunk), row_chunk)
            x = x_ref[rows, :]
            ms = jnp.mean(x * x, axis=-1, keepdims=True)
            h_scr[rows, :] = ((x * lax.rsqrt(ms + EPS)) * g_ref[...]).astype(jnp.bfloat16)
            return carry
        lax.fori_loop(0, x_ref.shape[0] // row_chunk, body, 0)

    tn = w_ref.shape[1]
    col = j * tn + lax.broadcasted_iota(jnp.int32, (1, tn), 1)
    w = jnp.where(col < in_width, w_ref[...], 0.0).astype(jnp.bfloat16)
    o_ref[...] = _dot(h_scr[...], w).astype(o_ref.dtype)


def _in_proj(x2d, g, w_in, *, tm, tn):
    n, d = x2d.shape
    in_width = w_in.shape[1]
    pw = pl.cdiv(in_width, tn) * tn
    return pl.pallas_call(
        functools.partial(_in_proj_kernel, row_chunk=min(tm, 128), in_width=in_width),
        out_shape=jax.ShapeDtypeStruct((n, pw), jnp.bfloat16),
        grid=(n // tm, pw // tn),
        in_specs=[
            pl.BlockSpec((tm, d), lambda i, j: (i, 0)),
            pl.BlockSpec((1, d), lambda i, j: (0, 0)),
            pl.BlockSpec((d, tn), lambda i, j: (0, j)),
        ],
        out_specs=pl.BlockSpec((tm, tn), lambda i, j: (i, j)),
        scratch_shapes=[pltpu.VMEM((tm, d), jnp.bfloat16)],
        compiler_params=_params("arbitrary", "arbitrary"),
        name="in_proj",
    )(x2d, g, w_in)


def _ordered_float(v):
    bits = v ^ ((v >> 31) & jnp.int32(0x7FFFFFFF))
    return pltpu.bitcast(bits, jnp.float32)


def _attn_kernel(aq_ref, iqa_ref, iqb_ref, iw_ref, ak_ref, av_ref, ik_ref, qg_ref, kg_ref, lnw_ref,
                 lnb_ref, o_ref,
                 kn_scr, ikn_scr, vt_scr, key_scr, iqh_scr, wt_scr, qn_scr, acc_scr, m_scr, l_scr,
                 *, tk, topk, idx_w_scale):
    i = pl.program_id(1)
    seq = ak_ref.shape[1]
    tq = aq_ref.shape[1]
    chunk_shift = CHUNK.bit_length() - 1

    @pl.when(i == 0)
    def _():
        def body(c, carry):
            rows = pl.ds(pl.multiple_of(c * tk, tk), tk)
            for g in range(KV_HEADS):
                cols = slice(g * HEAD_DIM, (g + 1) * HEAD_DIM)
                k = ak_ref[0, rows, cols].astype(jnp.float32)
                ms = jnp.mean(k * k, axis=-1, keepdims=True)
                kn_scr[rows, cols] = ((k * lax.rsqrt(ms + EPS)) * kg_ref[...]).astype(jnp.bfloat16)
                v = av_ref[0, rows, cols].astype(jnp.float32)
                vt_scr[g, c] = v.T.astype(jnp.bfloat16)
            ki = ik_ref[0, rows, :IDX_DIM].astype(jnp.float32)
            mu = jnp.mean(ki, axis=-1, keepdims=True)
            var = jnp.mean(jnp.square(ki - mu), axis=-1, keepdims=True)
            y = (ki - mu) * lax.rsqrt(var + EPS)
            ikn_scr[rows, :] = (y * lnw_ref[...] + lnb_ref[...]).astype(jnp.bfloat16)
            return carry
        lax.fori_loop(0, seq // tk, body, 0)

    t0 = i * tq
    n_kt = (t0 + tq) // tk
    scale = (HEAD_DIM ** -0.5) * math.log2(math.e)
    for h in range(ATTN_HEADS):
        g, r = divmod(h, HEADS_PER_KV)
        q = aq_ref[0, :, h * HEAD_DIM:(h + 1) * HEAD_DIM].astype(jnp.float32)
        ms = jnp.mean(q * q, axis=-1, keepdims=True)
        qn_scr[g, r * tq:(r + 1) * tq, :] = (
            (q * lax.rsqrt(ms + EPS)) * qg_ref[...] * scale).astype(jnp.bfloat16)
    half = IDX_HEADS // 2
    for h in range(IDX_HEADS):
        src = iqa_ref if h < half else iqb_ref
        iqh_scr[h] = src[0, :, (h % half) * IDX_DIM:(h % half + 1) * IDX_DIM]
    wt_scr[...] = iw_ref[0].astype(jnp.float32).T * idx_w_scale
    w_row = IW_OFF % LANES

    q_chunk = (t0 + lax.broadcasted_iota(jnp.int32, (tk, tq), 1)) >> chunk_shift

    def score_body(kt, carry):
        rows = pl.ds(pl.multiple_of(kt * tk, tk), tk)
        ik_t = ikn_scr[rows, :]
        acc = jnp.zeros((tk, tq), jnp.float32)
        for h in range(IDX_HEADS):
            d = _dot_nt(ik_t, iqh_scr[h])
            acc = acc + jnp.maximum(d, 0.0) * wt_scr[w_row + h:w_row + h + 1, :]
        k_chunk = (kt * tk + lax.broadcasted_iota(jnp.int32, (tk, tq), 0)) >> chunk_shift
        key_scr[rows, :] = jnp.where(k_chunk <= q_chunk, acc, -jnp.inf)
        return carry
    lax.fori_loop(0, n_kt, score_body, 0)

    def bit_body(it, lo):
        cand = lo + lax.shift_left(jnp.int32(1), 31 - it)
        cand_f = _ordered_float(cand)

        def count_body(kt, part):
            rows = pl.ds(pl.multiple_of(kt * tk, tk), tk)
            hit = jnp.where(key_scr[rows, :] >= cand_f, 1.0, 0.0).reshape(tk // 8, 8, tq)
            while hit.shape[0] > 1:
                half = hit.shape[0] // 2
                hit = hit[:half] + hit[half:]
            return part + hit[0]
        part = lax.fori_loop(0, n_kt, count_body, jnp.zeros((8, tq), jnp.float32))
        cnt = jnp.sum(part, axis=0, keepdims=True)
        return jnp.where(cnt >= float(topk), cand, lo)
    lo = lax.fori_loop(0, 32, bit_body, jnp.full((1, tq), INT_MIN, jnp.int32))
    thr = jnp.where(lo == INT_MIN, jnp.finfo(jnp.float32).min, _ordered_float(lo))

    m_scr[...] = jnp.full(m_scr.shape, NEG_BIG, jnp.float32)
    l_scr[...] = jnp.zeros(l_scr.shape, jnp.float32)
    acc_scr[...] = jnp.zeros(acc_scr.shape, jnp.float32)

    def attn_body(kt, carry):
        rows = pl.ds(pl.multiple_of(kt * tk, tk), tk)
        bias = jnp.where(key_scr[rows, :] >= thr, 0.0, NEG_BIG)
        bias = jnp.concatenate([bias] * HEADS_PER_KV, axis=1)
        for g in range(KV_HEADS):
            k_t = kn_scr[rows, g * HEAD_DIM:(g + 1) * HEAD_DIM]
            s = _dot_nt(k_t, qn_scr[g]) + bias
            m_old = m_scr[g]
            m_new = jnp.maximum(m_old, jnp.max(s, axis=0, keepdims=True))
            p = jnp.exp2(s - m_new)
            alpha = jnp.exp2(m_old - m_new)
            l_scr[g] = alpha * l_scr[g] + jnp.sum(p, axis=0, keepdims=True)
            acc_scr[g] = acc_scr[g] * alpha + _dot(vt_scr[g, kt], p.astype(jnp.bfloat16))
            m_scr[g] = m_new
        return carry
    lax.fori_loop(0, n_kt, attn_body, 0)

    for h in range(ATTN_HEADS):
        g, r = divmod(h, HEADS_PER_KV)
        o = acc_scr[g, :, r * tq:(r + 1) * tq] / l_scr[g, :, r * tq:(r + 1) * tq]
        o_ref[0, :, h * HEAD_DIM:(h + 1) * HEAD_DIM] = o.T.astype(o_ref.dtype)


def _dsa_attention(p3, q_g, k_g, ln_w, ln_b, *, tq, tk):
    b, seq, _ = p3.shape
    topk = min(TOPK_MAX, seq // 4)
    idx_w_scale = (IDX_HEADS ** -0.5) * (IDX_DIM ** -0.5)
    assert seq % tq == 0 and tq % tk == 0 and tk % CHUNK == 0

    def col(off, width):
        assert off % width == 0 or width == LANES
        return off // width

    half_iq = IDX_WIDTH // 2
    return pl.pallas_call(
        functools.partial(_attn_kernel, tk=tk, topk=topk, idx_w_scale=idx_w_scale),
        out_shape=jax.ShapeDtypeStruct((b, seq, ATTN_WIDTH), jnp.bfloat16),
        grid=(b, seq // tq),
        in_specs=[
            pl.BlockSpec((1, tq, ATTN_WIDTH), lambda bi, i: (bi, i, col(AQ_OFF, ATTN_WIDTH))),
            pl.BlockSpec((1, tq, half_iq), lambda bi, i: (bi, i, col(IQ_OFF, half_iq))),
            pl.BlockSpec((1, tq, half_iq), lambda bi, i: (bi, i, col(IQ_OFF, half_iq) + 1)),
            pl.BlockSpec((1, tq, LANES), lambda bi, i: (bi, i, col(IW_OFF, LANES))),
            pl.BlockSpec((1, seq, KV_WIDTH), lambda bi, i: (bi, 0, col(AK_OFF, KV_WIDTH))),
            pl.BlockSpec((1, seq, KV_WIDTH), lambda bi, i: (bi, 0, col(AV_OFF, KV_WIDTH))),
            pl.BlockSpec((1, seq, LANES), lambda bi, i: (bi, 0, col(IK_OFF, LANES))),
            pl.BlockSpec((1, HEAD_DIM), lambda bi, i: (0, 0)),
            pl.BlockSpec((1, HEAD_DIM), lambda bi, i: (0, 0)),
            pl.BlockSpec((1, IDX_DIM), lambda bi, i: (0, 0)),
            pl.BlockSpec((1, IDX_DIM), lambda bi, i: (0, 0)),
        ],
        out_specs=pl.BlockSpec((1, tq, ATTN_WIDTH), lambda bi, i: (bi, i, 0)),
        scratch_shapes=[
            pltpu.VMEM((seq, KV_WIDTH), jnp.bfloat16),
            pltpu.VMEM((seq, IDX_DIM), jnp.bfloat16),
            pltpu.VMEM((KV_HEADS, seq // tk, HEAD_DIM, tk), jnp.bfloat16),
            pltpu.VMEM((seq, tq), jnp.float32),
            pltpu.VMEM((IDX_HEADS, tq, IDX_DIM), jnp.bfloat16),
            pltpu.VMEM((LANES, tq), jnp.float32),
            pltpu.VMEM((KV_HEADS, HEADS_PER_KV * tq, HEAD_DIM), jnp.bfloat16),
            pltpu.VMEM((KV_HEADS, HEAD_DIM, HEADS_PER_KV * tq), jnp.float32),
            pltpu.VMEM((KV_HEADS, 1, HEADS_PER_KV * tq), jnp.float32),
            pltpu.VMEM((KV_HEADS, 1, HEADS_PER_KV * tq), jnp.float32),
        ],
        compiler_params=_params("arbitrary", "arbitrary"),
        name="dsa_attn",
    )(p3, p3, p3, p3, p3, p3, p3, q_g, k_g, ln_w, ln_b)


def _ret_kernel(lg_ref, qa_ref, qb_ref, ka_ref, kb_ref, va_ref, vb_ref, ga_ref, gb_ref,
                cos_ref, sin_ref, g_ref, o_ref, *, rc):
    h = pl.program_id(1)
    lg = lg_ref[h]
    seq = qa_ref.shape[1]
    first = lax.broadcasted_iota(jnp.int32, (rc, LANES), 1) >= RET_LANE

    def section(a_ref, b_ref, rows):
        return jnp.where(first, a_ref[0, rows, :], b_ref[0, rows, :])

    n = lax.broadcasted_iota(jnp.int32, (rc, RET_DV), 0).astype(jnp.float32)
    cross_decay = jnp.exp(lg * (n + 1.0))
    state_decay = jnp.exp(lg * (rc - 1.0 - n))
    chunk_decay = jnp.exp(lg * jnp.full((RET_DK, RET_DV), float(rc), jnp.float32))
    rel = (lax.broadcasted_iota(jnp.int32, (rc, rc), 0)
           - lax.broadcasted_iota(jnp.int32, (rc, rc), 1)).astype(jnp.float32)
    intra = jnp.where(rel >= 0, jnp.exp(lg * jnp.maximum(rel, 0.0)), 0.0)

    def rot(x, rows):
        return x * cos_ref[rows, :] + pltpu.roll(x, RET_DK // 2, 1) * sin_ref[rows, :]

    state = jnp.zeros((RET_DK, RET_DV), jnp.float32)
    for c in range(seq // rc):
        rows = slice(c * rc, (c + 1) * rc)
        q = rot(section(qa_ref, qb_ref, rows).astype(jnp.float32), rows)
        k = rot(section(ka_ref, kb_ref, rows).astype(jnp.float32), rows) * (RET_DK ** -0.5)
        v = section(va_ref, vb_ref, rows)
        qb = q.astype(jnp.bfloat16)
        inner = _dot_nt(qb, k.astype(jnp.bfloat16)) * intra
        o = _dot(inner.astype(jnp.bfloat16), v) + _dot(qb, state.astype(jnp.bfloat16)) * cross_decay
        kd_t = (k * state_decay).T.astype(jnp.bfloat16)
        state = state * chunk_decay + _dot(kd_t, v)
        ms = jnp.mean(o * o, axis=-1, keepdims=True)
        y = (o * lax.rsqrt(ms + EPS)) * g_ref[0]
        gate = section(ga_ref, gb_ref, rows).astype(jnp.float32)
        out = (gate * (1.0 / (1.0 + jnp.exp(-gate)))) * y
        o_ref[0, rows, :] = pltpu.roll(out, LANES - RET_LANE, 1).astype(o_ref.dtype)


def _retention(p3, log_gamma, cos2, sin2, ret_g, *, rc):
    b, seq, _ = p3.shape
    assert seq % rc == 0 and RET_DK == LANES and RET_DV == LANES

    def head_specs(off):
        blk = off // LANES
        return [pl.BlockSpec((1, seq, LANES), lambda bi, h: (bi, 0, blk + h)),
                pl.BlockSpec((1, seq, LANES), lambda bi, h: (bi, 0, blk + h + 1))]

    def rotated(a):
        return jnp.roll(a, RET_LANE, axis=-1)

    return pl.pallas_call(
        functools.partial(_ret_kernel, rc=rc),
        out_shape=jax.ShapeDtypeStruct((b, seq, RET_WIDTH), jnp.bfloat16),
        grid=(b, RET_HEADS),
        in_specs=[
            pl.BlockSpec(memory_space=pltpu.SMEM),
            *head_specs(RQ_OFF), *head_specs(RK_OFF), *head_specs(RV_OFF), *head_specs(RG_OFF),
            pl.BlockSpec((seq, RET_DK), lambda bi, h: (0, 0)),
            pl.BlockSpec((seq, RET_DK), lambda bi, h: (0, 0)),
            pl.BlockSpec((1, 1, RET_DV), lambda bi, h: (h, 0, 0)),
        ],
        out_specs=pl.BlockSpec((1, seq, RET_DV), lambda bi, h: (bi, 0, h)),
        compiler_params=_params("arbitrary", "arbitrary"),
        name="retention",
    )(log_gamma, *([p3] * 8), rotated(cos2), rotated(sin2), rotated(ret_g))


def _routing(logits):
    lane = lax.broadcasted_iota(jnp.int32, logits.shape, 1).astype(jnp.float32)
    big = float(LANES)
    neg = -jnp.inf

    def first_argmax(v, vmax):
        return jnp.min(jnp.where(v == vmax, lane, big), axis=-1, keepdims=True)

    g_mask = (lane >= N_EXPERTS) & (lane < N_EXPERTS + N_GROUPS)
    gl = jnp.where(g_mask, logits, neg)
    g_max = jnp.max(gl, axis=-1, keepdims=True)
    g_sel = first_argmax(gl, g_max) - N_EXPERTS
    g_gate = 1.0 / jnp.sum(jnp.where(g_mask, jnp.exp(gl - g_max), 0.0), axis=-1, keepdims=True)

    e_lo = g_sel * EXPERTS_PER_GROUP
    el = jnp.where((lane >= e_lo) & (lane < e_lo + EXPERTS_PER_GROUP), logits, neg)
    v1 = jnp.max(el, axis=-1, keepdims=True)
    i1 = first_argmax(el, v1)
    el2 = jnp.where(lane == i1, neg, el)
    v2 = jnp.max(el2, axis=-1, keepdims=True)
    i2 = first_argmax(el2, v2)
    e2 = jnp.exp(v2 - v1)
    denom = 1.0 + e2
    w1 = (1.0 / denom) * g_gate
    w2 = (e2 / denom) * g_gate
    route = jnp.where(lane == ROUTE_E1, i1, 0.0) + jnp.where(lane == ROUTE_E2, i2, 0.0)
    return route + jnp.where(lane == ROUTE_W1, w1, 0.0) + jnp.where(lane == ROUTE_W2, w2, 0.0)


def _norm2(x1, g):
    ms = jnp.mean(x1 * x1, axis=-1, keepdims=True)
    return (x1 * lax.rsqrt(ms + EPS)) * g


def _out_proj_kernel(a_ref, r_ref, x_ref, wa_ref, wr_ref, g_ref, rhi_ref, rlo_ref,
                     x1_ref, h2_ref, route_ref):
    mixed = _dot(a_ref[...], wa_ref[...]) + _dot(r_ref[...], wr_ref[...])
    x1 = x_ref[...] + mixed
    x1_ref[...] = x1
    h2 = _norm2(x1, g_ref[...])
    h2_ref[...] = h2
    hi = h2.astype(jnp.bfloat16)
    lo = (h2 - hi.astype(jnp.float32)).astype(jnp.bfloat16)
    logits = _dot(hi, rhi_ref[...]) + (_dot(hi, rlo_ref[...]) + _dot(lo, rhi_ref[...]))
    route_ref[...] = _routing(logits)


def _out_proj(attn2d, ret2d, x2d, w_out_bf, g2, r_hi, r_lo, *, tm):
    n, d = x2d.shape
    return pl.pallas_call(
        _out_proj_kernel,
        out_shape=(
            jax.ShapeDtypeStruct((n, d), jnp.float32),
            jax.ShapeDtypeStruct((n, d), jnp.float32),
            jax.ShapeDtypeStruct((n, LANES), jnp.float32),
        ),
        grid=(n // tm,),
        in_specs=[
            pl.BlockSpec((tm, ATTN_WIDTH), lambda i: (i, 0)),
            pl.BlockSpec((tm, RET_WIDTH), lambda i: (i, 0)),
            pl.BlockSpec((tm, d), lambda i: (i, 0)),
            pl.BlockSpec((ATTN_WIDTH, d), lambda i: (0, 0)),
            pl.BlockSpec((RET_WIDTH, d), lambda i: (ATTN_WIDTH // RET_WIDTH, 0)),
            pl.BlockSpec((1, d), lambda i: (0, 0)),
            pl.BlockSpec((d, LANES), lambda i: (0, 0)),
            pl.BlockSpec((d, LANES), lambda i: (0, 0)),
        ],
        out_specs=(
            pl.BlockSpec((tm, d), lambda i: (i, 0)),
            pl.BlockSpec((tm, d), lambda i: (i, 0)),
            pl.BlockSpec((tm, LANES), lambda i: (i, 0)),
        ),
        compiler_params=_params("arbitrary"),
        name="out_proj",
    )(attn2d, ret2d, x2d, w_out_bf, w_out_bf, g2, r_hi, r_lo)


def _plan_kernel(route_ref, pos_ref, cnt_ref, rank_scr, *, tm, blk):
    n = route_ref.shape[0]
    lane = lax.broadcasted_iota(jnp.int32, (blk, LANES), 1).astype(jnp.float32)
    before = (lax.broadcasted_iota(jnp.int32, (blk, blk), 1)
              < lax.broadcasted_iota(jnp.int32, (blk, blk), 0)).astype(jnp.bfloat16)

    def one_hot(rows):
        r = route_ref[rows, :]
        e1 = r[:, ROUTE_E1:ROUTE_E1 + 1]
        e2 = r[:, ROUTE_E2:ROUTE_E2 + 1]
        return lane == e1, lane == e2

    def rank_body(b, run):
        rows = pl.ds(pl.multiple_of(b * blk, blk), blk)
        m1, m2 = one_hot(rows)
        sel = jnp.where(m1 | m2, 1.0, 0.0)
        rank_scr[rows, :] = _dot(before, sel.astype(jnp.bfloat16)) + run
        return run + jnp.sum(sel, axis=0, keepdims=True)
    cnt = lax.fori_loop(0, n // blk, rank_body, jnp.zeros((1, LANES), jnp.float32))
    cnt_ref[...] = jnp.broadcast_to(cnt, cnt_ref.shape)

    tiles = jnp.floor((cnt + (tm - 1.0)) * (1.0 / tm))
    below = (lax.broadcasted_iota(jnp.int32, (LANES, LANES), 0)
             < lax.broadcasted_iota(jnp.int32, (LANES, LANES), 1)).astype(jnp.bfloat16)
    start = _dot(jnp.broadcast_to(tiles, (8, LANES)).astype(jnp.bfloat16), below)[0:1, :] * float(tm)

    def pos_body(b, carry):
        rows = pl.ds(pl.multiple_of(b * blk, blk), blk)
        m1, m2 = one_hot(rows)
        dest = rank_scr[rows, :] + start
        p1 = jnp.sum(jnp.where(m1, dest, 0.0), axis=-1, keepdims=True)
        p2 = jnp.sum(jnp.where(m2, dest, 0.0), axis=-1, keepdims=True)
        pos_ref[rows, :] = (jnp.where(lane == 0.0, p1, 0.0) + jnp.where(lane == 1.0, p2, 0.0)).astype(jnp.int32)
        return carry
    lax.fori_loop(0, n // blk, pos_body, 0)


def _moe_plan(route, *, tm, blk):
    n = route.shape[0]
    return pl.pallas_call(
        functools.partial(_plan_kernel, tm=tm, blk=blk),
        out_shape=(jax.ShapeDtypeStruct((n, LANES), jnp.int32),
                   jax.ShapeDtypeStruct((8, LANES), jnp.float32)),
        scratch_shapes=[pltpu.VMEM((n, LANES), jnp.float32)],
        compiler_params=pltpu.CompilerParams(vmem_limit_bytes=VMEM_LIMIT),
        name="moe_plan",
    )(route)


def _row_copy(src, src_row, dst, dst_row, sem):
    return pltpu.make_async_copy(src.at[pl.ds(src_row, 1), :], dst.at[pl.ds(dst_row, 1), :], sem)


def _invert_kernel(pos_ref, sid_ref):
    def body(j, carry):
        sid_ref[pos_ref[j]] = j
        return carry
    lax.fori_loop(0, pos_ref.shape[0], body, 0, unroll=8)


def _moe_invert(pos, *, rows):
    return pl.pallas_call(
        _invert_kernel,
        out_shape=jax.ShapeDtypeStruct((rows,), jnp.int32),
        in_specs=[pl.BlockSpec(memory_space=pltpu.SMEM)],
        out_specs=pl.BlockSpec(memory_space=pltpu.SMEM),
        name="moe_invert",
    )(pos)


ROW_GROUP = 8


def _ffn_kernel(te_ref, tv_ref, nu_ref, sid_ref, h2_ref, w1_ref, w3_ref, w2_ref, y2_ref,
                xbuf, ybuf, gsem, ssem):
    t = pl.program_id(0)
    n_used = nu_ref[0]
    tm = xbuf.shape[1]

    n_tok = h2_ref.shape[0]

    def gather(slot):
        return lambda r, sid: _row_copy(h2_ref, jnp.where(sid >= n_tok, sid - n_tok, sid),
                                        xbuf.at[slot], r, gsem.at[slot])

    def scatter(slot):
        return lambda r, sid: _row_copy(ybuf.at[slot], r, y2_ref, sid, ssem.at[slot])

    def start_partial(tile, copy):
        valid = tv_ref[tile]

        def body(c, carry):
            for u in range(ROW_GROUP):
                r = c * ROW_GROUP + u

                @pl.when(r < valid)
                def _():
                    copy(r, sid_ref[tile * tm + r]).start()
            return carry
        lax.fori_loop(0, (valid + (ROW_GROUP - 1)) // ROW_GROUP, body, 0)

    def start_all(tile, copy):
        @pl.when(tv_ref[tile] == tm)
        def _():
            for r in range(tm):
                copy(r, sid_ref[tile * tm + r]).start()

        @pl.when(tv_ref[tile] < tm)
        def _():
            start_partial(tile, copy)

    def wait_all(tile, copy, tile_copy):
        @pl.when(tv_ref[tile] == tm)
        def _():
            tile_copy.wait()

        @pl.when(tv_ref[tile] < tm)
        def _():
            def body(r, carry):
                copy(0, 0).wait()
                return carry
            lax.fori_loop(0, tv_ref[tile], body, 0)

    def wait_gather(tile, slot):
        wait_all(tile, gather(slot),
                 pltpu.make_async_copy(h2_ref.at[pl.ds(0, tm), :], xbuf.at[slot], gsem.at[slot]))

    def wait_scatter(tile, slot):
        wait_all(tile, scatter(slot),
                 pltpu.make_async_copy(ybuf.at[slot], y2_ref.at[pl.ds(0, tm), :], ssem.at[slot]))

    @pl.when(t == 0)
    def _():
        start_partial(0, gather(0))

    @pl.when(t < n_used)
    def _():
        slot = t % 2

        @pl.when(t + 1 < n_used)
        def _():
            start_all(t + 1, gather(1 - slot))

        wait_gather(t, slot)

        @pl.when(t >= 2)
        def _():
            wait_scatter(t - 2, slot)

        row = lax.broadcasted_iota(jnp.int32, xbuf.shape[1:], 0)
        x = jnp.where(row < tv_ref[t], xbuf[slot], 0.0).astype(jnp.bfloat16)
        a = _dot(x, w1_ref[0].astype(jnp.bfloat16))
        b = _dot(x, w3_ref[0].astype(jnp.bfloat16))
        act = (a * (1.0 / (1.0 + jnp.exp(-a)))) * b
        ybuf[slot] = _dot(act.astype(jnp.bfloat16), w2_ref[0].astype(jnp.bfloat16))
        start_all(t, scatter(slot))

    @pl.when(t == pl.num_programs(0) - 1)
    def _():
        @pl.when(n_used >= 2)
        def _():
            wait_scatter(n_used - 2, n_used % 2)
        wait_scatter(n_used - 1, (n_used - 1) % 2)


def _moe_ffn(tile_expert, tile_valid, n_used, row_sid, h2, w1, w3, w2, *, tm):
    n, d = h2.shape
    _, _, ff = w1.shape
    max_tiles = tile_expert.shape[0]

    def expert(t, te, tv, nu, sid):
        return te[jnp.minimum(t, nu[0] - 1)]

    return pl.pallas_call(
        _ffn_kernel,
        out_shape=jax.ShapeDtypeStruct((2 * n, d), jnp.float32),
        grid_spec=pltpu.PrefetchScalarGridSpec(
            num_scalar_prefetch=4,
            grid=(max_tiles,),
            in_specs=[
                pl.BlockSpec(memory_space=pl.ANY),
                pl.BlockSpec((1, d, ff), lambda *a: (expert(*a), 0, 0)),
                pl.BlockSpec((1, d, ff), lambda *a: (expert(*a), 0, 0)),
                pl.BlockSpec((1, ff, d), lambda *a: (expert(*a), 0, 0)),
            ],
            out_specs=pl.BlockSpec(memory_space=pl.ANY),
            scratch_shapes=[
                pltpu.VMEM((2, tm, d), jnp.float32),
                pltpu.VMEM((2, tm, d), jnp.float32),
                pltpu.SemaphoreType.DMA((2,)),
                pltpu.SemaphoreType.DMA((2,)),
            ],
        ),
        compiler_params=_params("arbitrary"),
        name="moe_ffn",
    )(tile_expert, tile_valid, n_used, row_sid, h2, w1, w3, w2)


def _combine_kernel(x1_ref, route_ref, ya_ref, yb_ref, o_ref):
    route = route_ref[...]
    w1 = route[:, ROUTE_W1:ROUTE_W1 + 1]
    w2 = route[:, ROUTE_W2:ROUTE_W2 + 1]
    o_ref[...] = x1_ref[...] + (w1 * ya_ref[...] + w2 * yb_ref[...])


def _moe_combine(x1, route, y2, *, tm):
    n, d = x1.shape
    return pl.pallas_call(
        _combine_kernel,
        out_shape=jax.ShapeDtypeStruct((n, d), jnp.float32),
        grid=(n // tm,),
        in_specs=[pl.BlockSpec((tm, d), lambda i: (i, 0)),
                  pl.BlockSpec((tm, LANES), lambda i: (i, 0)),
                  pl.BlockSpec((tm, d), lambda i: (i, 0)),
                  pl.BlockSpec((tm, d), lambda i: (i + n // tm, 0))],
        out_specs=pl.BlockSpec((tm, d), lambda i: (i, 0)),
        compiler_params=_params("arbitrary"),
        name="moe_combine",
    )(x1, route, y2, y2)


def _moe(x1, h2, route, w1, w3, w2, *, tm, gather_tm):
    n, d = x1.shape
    max_tiles = (2 * n) // tm + N_EXPERTS
    pos2d, cnt = _moe_plan(route, tm=tm, blk=gather_tm)
    pos = pos2d[:, :2].T.reshape(2 * n)
    counts = cnt[0, :N_EXPERTS].astype(jnp.int32)
    tiles = (counts + (tm - 1)) // tm
    ends = jnp.cumsum(tiles)
    t_idx = jnp.arange(max_tiles, dtype=jnp.int32)
    tile_expert = jnp.sum((ends[None, :] <= t_idx[:, None]).astype(jnp.int32), axis=1)
    tile_expert = jnp.minimum(tile_expert, N_EXPERTS - 1)
    first_tile = (ends - tiles)[tile_expert]
    tile_valid = jnp.clip(counts[tile_expert] - (t_idx - first_tile) * tm, 0, tm).astype(jnp.int32)
    n_used = ends[-1:].astype(jnp.int32)

    row_sid = _moe_invert(pos, rows=max_tiles * tm)
    y2 = _moe_ffn(tile_expert, tile_valid, n_used, row_sid, h2, w1, w3, w2, tm=tm)
    return _moe_combine(x1, route, y2, tm=gather_tm)


def _pack_router(w_group, w_router):
    d = w_group.shape[0]
    experts = jnp.transpose(w_router, (1, 0, 2)).reshape(d, N_EXPERTS)
    wr = jnp.concatenate(
        [experts, w_group, jnp.zeros((d, LANES - N_EXPERTS - N_GROUPS), w_group.dtype)], axis=1)
    hi = wr.astype(jnp.bfloat16)
    lo = (wr - hi.astype(jnp.float32)).astype(jnp.bfloat16)
    return hi, lo


def _rotation_tables(seq):
    half = RET_DK // 2
    pos = jnp.arange(seq, dtype=jnp.float32)
    inv = 1.0 / (ROT_BASE ** jnp.linspace(0.0, 1.0, half, dtype=jnp.float32))
    ang = pos[:, None] * inv[None, :]
    c, s = jnp.cos(ang), jnp.sin(ang)
    return jnp.concatenate([c, c], axis=-1), jnp.concatenate([-s, s], axis=-1)


def _tiles(n, seq):
    def fit(total, want):
        t = min(total, want)
        while total % t:
            t //= 2
        return t
    return dict(
        proj_tm=fit(n, 1024), proj_tn=1024,
        attn_tq=fit(seq, 256), attn_tk=fit(seq, 256),
        ret_rc=fit(seq, 256),
        out_tm=fit(n, 512),
        moe_tm=fit(n, 256), moe_gather_tm=fit(n, 256),
    )


def kernel(x, norm1_g, w_in, q_norm_g, k_norm_g, idx_k_ln_w, idx_k_ln_b, ret_norm_g,
           w_out, norm2_g, w_group, w_router, w1, w3, w2):
    b, seq, d = x.shape
    n = b * seq
    depth = w_in.shape[0]
    t = _tiles(n, seq)
    cos2, sin2 = _rotation_tables(seq)
    log_gamma = jnp.log1p(-jnp.exp2(-5.0 - jnp.arange(RET_HEADS, dtype=jnp.float32)))

    x2d = x.reshape(n, d)
    for l in range(depth):
        assert w_in.shape[2] == IN_WIDTH
        proj = _in_proj(x2d, norm1_g[l][None, :], w_in[l], tm=t["proj_tm"], tn=t["proj_tn"])
        p3 = proj.reshape(b, seq, proj.shape[1])
        attn = _dsa_attention(p3, q_norm_g[l][None, :], k_norm_g[l][None, :],
                              idx_k_ln_w[l][None, :], idx_k_ln_b[l][None, :],
                              tq=t["attn_tq"], tk=t["attn_tk"])
        ret = _retention(p3, log_gamma, cos2, sin2, ret_norm_g[l].reshape(RET_HEADS, 1, RET_DV),
                         rc=t["ret_rc"])
        r_hi, r_lo = _pack_router(w_group[l], w_router[l])
        g2 = norm2_g[l][None, :]
        x1, h2, route = _out_proj(attn.reshape(n, ATTN_WIDTH), ret.reshape(n, RET_WIDTH), x2d,
                                  w_out[l].astype(jnp.bfloat16), g2, r_hi, r_lo, tm=t["out_tm"])
        x2d = _moe(x1, h2, route, w1[l], w3[l], w2[l], tm=t["moe_tm"], gather_tm=t["moe_gather_tm"])
    return x2d.reshape(b, seq, d)
```

```python
import functools
import math

import jax
import jax.numpy as jnp
from jax import lax
from jax.experimental import pallas as pl
from jax.experimental.pallas import tpu as pltpu

CHUNK = 64
ATTN_HEADS = 8
HEAD_DIM = 128
KV_HEADS = 2
HEADS_PER_KV = ATTN_HEADS // KV_HEADS
IDX_HEADS = 16
IDX_DIM = 64
TOPK_MAX = 256
RET_HEADS = 8
RET_DK = 128
RET_DV = 128
ROT_BASE = 10000.0
N_GROUPS = 4
EXPERTS_PER_GROUP = 8
N_EXPERTS = N_GROUPS * EXPERTS_PER_GROUP
EPS = 1e-6

ATTN_WIDTH = ATTN_HEADS * HEAD_DIM
KV_WIDTH = KV_HEADS * HEAD_DIM
IDX_WIDTH = IDX_HEADS * IDX_DIM
RET_WIDTH = RET_HEADS * RET_DK

LANES = 128
VMEM_LIMIT = 56 * 1024 * 1024

AQ_OFF = 0
AK_OFF = AQ_OFF + ATTN_WIDTH
AV_OFF = AK_OFF + KV_WIDTH
IQ_OFF = AV_OFF + KV_WIDTH
IK_OFF = IQ_OFF + IDX_WIDTH
IW_OFF = IK_OFF + IDX_DIM
W_RET = IW_OFF + IDX_HEADS
IN_WIDTH = W_RET + 4 * RET_WIDTH
assert IW_OFF // LANES == IK_OFF // LANES
PROJ_TN = 1024
RQ_OFF = -(-W_RET // PROJ_TN) * PROJ_TN
RK_OFF = RQ_OFF + RET_WIDTH
RV_OFF = RK_OFF + RET_WIDTH
RG_OFF = RV_OFF + RET_WIDTH
PROJ_WIDTH = RG_OFF + RET_WIDTH

ROUTE_E1, ROUTE_E2, ROUTE_W1, ROUTE_W2 = 0, 1, 2, 3

INT_MIN = -(2 ** 31)
NEG_BIG = -1e30

_NT = (((1,), (1,)), ((), ()))


def _dot(a, b):
    return jnp.dot(a, b, preferred_element_type=jnp.float32)


def _dot_nt(a, b):
    return lax.dot_general(a, b, _NT, preferred_element_type=jnp.float32)


def _params(*sem):
    return pltpu.CompilerParams(dimension_semantics=sem, vmem_limit_bytes=VMEM_LIMIT)


def _in_proj_kernel(x_ref, g_ref, wt_ref, o_ref, h_scr, *, row_chunk):
    @pl.when(pl.program_id(1) == 0)
    def _():
        def body(c, carry):
            rows = pl.ds(pl.multiple_of(c * row_chunk, row_chunk), row_chunk)
            x = x_ref[rows, :]
            ms = jnp.mean(x * x, axis=-1, keepdims=True)
            h_scr[rows, :] = ((x * lax.rsqrt(ms + EPS)) * g_ref[...]).astype(jnp.bfloat16)
            return carry
        lax.fori_loop(0, x_ref.shape[0] // row_chunk, body, 0)

    o_ref[...] = _dot_nt(h_scr[...], wt_ref[...].astype(jnp.bfloat16)).astype(o_ref.dtype)


def _in_proj(x2d, g, w_in_t, *, tm):
    n, d = x2d.shape
    tn = PROJ_TN
    assert w_in_t.shape == (IN_WIDTH, d) and W_RET % 8 == 0
    attn_tiles = RQ_OFF // tn

    def window(i, j):
        start8 = jnp.where(j < attn_tiles, j * (tn // 8), W_RET // 8 + (j - attn_tiles) * (tn // 8))
        return 8 * start8, 0

    return pl.pallas_call(
        functools.partial(_in_proj_kernel, row_chunk=min(tm, 128)),
        out_shape=jax.ShapeDtypeStruct((n, PROJ_WIDTH), jnp.bfloat16),
        grid=(n // tm, PROJ_WIDTH // tn),
        in_specs=[
            pl.BlockSpec((tm, d), lambda i, j: (i, 0)),
            pl.BlockSpec((1, d), lambda i, j: (0, 0)),
            pl.BlockSpec((pl.Element(tn), pl.Element(d)), window),
        ],
        out_specs=pl.BlockSpec((tm, tn), lambda i, j: (i, j)),
        scratch_shapes=[pltpu.VMEM((tm, d), jnp.bfloat16)],
        compiler_params=_params("arbitrary", "arbitrary"),
        name="in_proj",
    )(x2d, g, w_in_t)


def _ordered_float(v):
    bits = v ^ ((v >> 31) & jnp.int32(0x7FFFFFFF))
    return pltpu.bitcast(bits, jnp.float32)


def _attn_kernel(aq_ref, iqa_ref, iqb_ref, iw_ref, ak_ref, av_ref, ik_ref, qg_ref, kg_ref, lnw_ref,
                 lnb_ref, o_ref,
                 kn_scr, ikn_scr, vt_scr, key_scr, iqh_scr, wt_scr, qn_scr, acc_scr, m_scr, l_scr,
                 *, tk, topk, idx_w_scale):
    i = pl.program_id(1)
    seq = ak_ref.shape[1]
    tq = aq_ref.shape[1]
    chunk_shift = CHUNK.bit_length() - 1

    @pl.when(i == 0)
    def _():
        def body(c, carry):
            rows = pl.ds(pl.multiple_of(c * tk, tk), tk)
            for g in range(KV_HEADS):
                cols = slice(g * HEAD_DIM, (g + 1) * HEAD_DIM)
                k = ak_ref[0, rows, cols].astype(jnp.float32)
                ms = jnp.mean(k * k, axis=-1, keepdims=True)
                kn_scr[rows, cols] = ((k * lax.rsqrt(ms + EPS)) * kg_ref[...]).astype(jnp.bfloat16)
                v = av_ref[0, rows, cols].astype(jnp.float32)
                vt_scr[g, c] = v.T.astype(jnp.bfloat16)
            ki = ik_ref[0, rows, :IDX_DIM].astype(jnp.float32)
            mu = jnp.mean(ki, axis=-1, keepdims=True)
            var = jnp.mean(jnp.square(ki - mu), axis=-1, keepdims=True)
            y = (ki - mu) * lax.rsqrt(var + EPS)
            ikn_scr[rows, :] = (y * lnw_ref[...] + lnb_ref[...]).astype(jnp.bfloat16)
            return carry
        lax.fori_loop(0, seq // tk, body, 0)

    t0 = i * tq
    n_kt = (t0 + tq) // tk
    scale = (HEAD_DIM ** -0.5) * math.log2(math.e)
    for h in range(ATTN_HEADS):
        g, r = divmod(h, HEADS_PER_KV)
        q = aq_ref[0, :, h * HEAD_DIM:(h + 1) * HEAD_DIM].astype(jnp.float32)
        ms = jnp.mean(q * q, axis=-1, keepdims=True)
        qn_scr[g, r * tq:(r + 1) * tq, :] = (
            (q * lax.rsqrt(ms + EPS)) * qg_ref[...] * scale).astype(jnp.bfloat16)
    half = IDX_HEADS // 2
    for h in range(IDX_HEADS):
        src = iqa_ref if h < half else iqb_ref
        iqh_scr[h] = src[0, :, (h % half) * IDX_DIM:(h % half + 1) * IDX_DIM]
    wt_scr[...] = iw_ref[0].astype(jnp.float32).T * idx_w_scale
    w_row = IW_OFF % LANES

    q_chunk = (t0 + lax.broadcasted_iota(jnp.int32, (tk, tq), 1)) >> chunk_shift

    def score_body(kt, carry):
        rows = pl.ds(pl.multiple_of(kt * tk, tk), tk)
        ik_t = ikn_scr[rows, :]
        acc = jnp.zeros((tk, tq), jnp.float32)
        for h in range(IDX_HEADS):
            d = _dot_nt(ik_t, iqh_scr[h])
            acc = acc + jnp.maximum(d, 0.0) * wt_scr[w_row + h:w_row + h + 1, :]
        k_chunk = (kt * tk + lax.broadcasted_iota(jnp.int32, (tk, tq), 0)) >> chunk_shift
        key_scr[rows, :] = jnp.where(k_chunk <= q_chunk, acc, -jnp.inf)
        return carry
    lax.fori_loop(0, n_kt, score_body, 0)

    def bit_body(it, lo):
        cand = lo + lax.shift_left(jnp.int32(1), 31 - it)
        cand_f = _ordered_float(cand)

        def count_body(kt, part):
            rows = pl.ds(pl.multiple_of(kt * tk, tk), tk)
            hit = jnp.where(key_scr[rows, :] >= cand_f, 1.0, 0.0).reshape(tk // 8, 8, tq)
            while hit.shape[0] > 1:
                half = hit.shape[0] // 2
                hit = hit[:half] + hit[half:]
            return part + hit[0]
        part = lax.fori_loop(0, n_kt, count_body, jnp.zeros((8, tq), jnp.float32))
        cnt = jnp.sum(part, axis=0, keepdims=True)
        return jnp.where(cnt >= float(topk), cand, lo)
    lo = lax.fori_loop(0, 32, bit_body, jnp.full((1, tq), INT_MIN, jnp.int32))
    thr = jnp.where(lo == INT_MIN, jnp.finfo(jnp.float32).min, _ordered_float(lo))

    m_scr[...] = jnp.full(m_scr.shape, NEG_BIG, jnp.float32)
    l_scr[...] = jnp.zeros(l_scr.shape, jnp.float32)
    acc_scr[...] = jnp.zeros(acc_scr.shape, jnp.float32)

    def attn_body(kt, carry):
        rows = pl.ds(pl.multiple_of(kt * tk, tk), tk)
        bias = jnp.where(key_scr[rows, :] >= thr, 0.0, NEG_BIG)
        bias = jnp.concatenate([bias] * HEADS_PER_KV, axis=1)
        for g in range(KV_HEADS):
            k_t = kn_scr[rows, g * HEAD_DIM:(g + 1) * HEAD_DIM]
            s = _dot_nt(k_t, qn_scr[g]) + bias
            m_old = m_scr[g]
            m_new = jnp.maximum(m_old, jnp.max(s, axis=0, keepdims=True))
            p = jnp.exp2(s - m_new)
            alpha = jnp.exp2(m_old - m_new)
            l_scr[g] = alpha * l_scr[g] + jnp.sum(p, axis=0, keepdims=True)
            acc_scr[g] = acc_scr[g] * alpha + _dot(vt_scr[g, kt], p.astype(jnp.bfloat16))
            m_scr[g] = m_new
        return carry
    lax.fori_loop(0, n_kt, attn_body, 0)

    for h in range(ATTN_HEADS):
        g, r = divmod(h, HEADS_PER_KV)
        o = acc_scr[g, :, r * tq:(r + 1) * tq] / l_scr[g, :, r * tq:(r + 1) * tq]
        o_ref[0, :, h * HEAD_DIM:(h + 1) * HEAD_DIM] = o.T.astype(o_ref.dtype)


def _dsa_attention(p3, q_g, k_g, ln_w, ln_b, *, tq, tk):
    b, seq, _ = p3.shape
    topk = min(TOPK_MAX, seq // 4)
    idx_w_scale = (IDX_HEADS ** -0.5) * (IDX_DIM ** -0.5)
    assert seq % tq == 0 and tq % tk == 0 and tk % CHUNK == 0

    def col(off, width):
        assert off % width == 0 or width == LANES
        return off // width

    half_iq = IDX_WIDTH // 2
    return pl.pallas_call(
        functools.partial(_attn_kernel, tk=tk, topk=topk, idx_w_scale=idx_w_scale),
        out_shape=jax.ShapeDtypeStruct((b, seq, ATTN_WIDTH), jnp.bfloat16),
        grid=(b, seq // tq),
        in_specs=[
            pl.BlockSpec((1, tq, ATTN_WIDTH), lambda bi, i: (bi, i, col(AQ_OFF, ATTN_WIDTH))),
            pl.BlockSpec((1, tq, half_iq), lambda bi, i: (bi, i, col(IQ_OFF, half_iq))),
            pl.BlockSpec((1, tq, half_iq), lambda bi, i: (bi, i, col(IQ_OFF, half_iq) + 1)),
            pl.BlockSpec((1, tq, LANES), lambda bi, i: (bi, i, col(IW_OFF, LANES))),
            pl.BlockSpec((1, seq, KV_WIDTH), lambda bi, i: (bi, 0, col(AK_OFF, KV_WIDTH))),
            pl.BlockSpec((1, seq, KV_WIDTH), lambda bi, i: (bi, 0, col(AV_OFF, KV_WIDTH))),
            pl.BlockSpec((1, seq, LANES), lambda bi, i: (bi, 0, col(IK_OFF, LANES))),
            pl.BlockSpec((1, HEAD_DIM), lambda bi, i: (0, 0)),
            pl.BlockSpec((1, HEAD_DIM), lambda bi, i: (0, 0)),
            pl.BlockSpec((1, IDX_DIM), lambda bi, i: (0, 0)),
            pl.BlockSpec((1, IDX_DIM), lambda bi, i: (0, 0)),
        ],
        out_specs=pl.BlockSpec((1, tq, ATTN_WIDTH), lambda bi, i: (bi, i, 0)),
        scratch_shapes=[
            pltpu.VMEM((seq, KV_WIDTH), jnp.bfloat16),
            pltpu.VMEM((seq, IDX_DIM), jnp.bfloat16),
            pltpu.VMEM((KV_HEADS, seq // tk, HEAD_DIM, tk), jnp.bfloat16),
            pltpu.VMEM((seq, tq), jnp.float32),
            pltpu.VMEM((IDX_HEADS, tq, IDX_DIM), jnp.bfloat16),
            pltpu.VMEM((LANES, tq), jnp.float32),
            pltpu.VMEM((KV_HEADS, HEADS_PER_KV * tq, HEAD_DIM), jnp.bfloat16),
            pltpu.VMEM((KV_HEADS, HEAD_DIM, HEADS_PER_KV * tq), jnp.float32),
            pltpu.VMEM((KV_HEADS, 1, HEADS_PER_KV * tq), jnp.float32),
            pltpu.VMEM((KV_HEADS, 1, HEADS_PER_KV * tq), jnp.float32),
        ],
        compiler_params=_params("arbitrary", "arbitrary"),
        name="dsa_attn",
    )(p3, p3, p3, p3, p3, p3, p3, q_g, k_g, ln_w, ln_b)


def _ret_kernel(lg_ref, rq_ref, rk_ref, rv_ref, rg_ref, cos_ref, sin_ref, g_ref, o_ref, *, rc):
    h = pl.program_id(1)
    lg = lg_ref[h]
    seq = rq_ref.shape[1]
    n = lax.broadcasted_iota(jnp.int32, (rc, RET_DV), 0).astype(jnp.float32)
    cross_decay = jnp.exp(lg * (n + 1.0))
    state_decay = jnp.exp(lg * (rc - 1.0 - n))
    chunk_decay = jnp.exp(lg * jnp.full((RET_DK, RET_DV), float(rc), jnp.float32))
    rel = (lax.broadcasted_iota(jnp.int32, (rc, rc), 0)
           - lax.broadcasted_iota(jnp.int32, (rc, rc), 1)).astype(jnp.float32)
    intra = jnp.where(rel >= 0, jnp.exp(lg * jnp.maximum(rel, 0.0)), 0.0)

    def rot(x, rows):
        return x * cos_ref[rows, :] + pltpu.roll(x, RET_DK // 2, 1) * sin_ref[rows, :]

    state = jnp.zeros((RET_DK, RET_DV), jnp.float32)
    for c in range(seq // rc):
        rows = slice(c * rc, (c + 1) * rc)
        q = rot(rq_ref[0, rows, :].astype(jnp.float32), rows)
        k = rot(rk_ref[0, rows, :].astype(jnp.float32), rows) * (RET_DK ** -0.5)
        v = rv_ref[0, rows, :]
        qb = q.astype(jnp.bfloat16)
        inner = _dot_nt(qb, k.astype(jnp.bfloat16)) * intra
        o = _dot(inner.astype(jnp.bfloat16), v) + _dot(qb, state.astype(jnp.bfloat16)) * cross_decay
        kd_t = (k * state_decay).T.astype(jnp.bfloat16)
        state = state * chunk_decay + _dot(kd_t, v)
        ms = jnp.mean(o * o, axis=-1, keepdims=True)
        y = (o * lax.rsqrt(ms + EPS)) * g_ref[0]
        gate = rg_ref[0, rows, :].astype(jnp.float32)
        o_ref[0, rows, :] = ((gate * (1.0 / (1.0 + jnp.exp(-gate)))) * y).astype(o_ref.dtype)


def _retention(p3, log_gamma, cos2, sin2, ret_g, *, rc):
    b, seq, _ = p3.shape
    assert seq % rc == 0

    def head_spec(off):
        return pl.BlockSpec((1, seq, RET_DK), lambda bi, h: (bi, 0, off // RET_DK + h))

    return pl.pallas_call(
        functools.partial(_ret_kernel, rc=rc),
        out_shape=jax.ShapeDtypeStruct((b, seq, RET_WIDTH), jnp.bfloat16),
        grid=(b, RET_HEADS),
        in_specs=[
            pl.BlockSpec(memory_space=pltpu.SMEM),
            head_spec(RQ_OFF), head_spec(RK_OFF), head_spec(RV_OFF), head_spec(RG_OFF),
            pl.BlockSpec((seq, RET_DK), lambda bi, h: (0, 0)),
            pl.BlockSpec((seq, RET_DK), lambda bi, h: (0, 0)),
            pl.BlockSpec((1, 1, RET_DV), lambda bi, h: (h, 0, 0)),
        ],
        out_specs=pl.BlockSpec((1, seq, RET_DV), lambda bi, h: (bi, 0, h)),
        compiler_params=_params("arbitrary", "arbitrary"),
        name="retention",
    )(log_gamma, p3, p3, p3, p3, cos2, sin2, ret_g)


def _routing(logits):
    lane = lax.broadcasted_iota(jnp.int32, logits.shape, 1).astype(jnp.float32)
    big = float(LANES)
    neg = -jnp.inf

    def first_argmax(v, vmax):
        return jnp.min(jnp.where(v == vmax, lane, big), axis=-1, keepdims=True)

    g_mask = (lane >= N_EXPERTS) & (lane < N_EXPERTS + N_GROUPS)
    gl = jnp.where(g_mask, logits, neg)
    g_max = jnp.max(gl, axis=-1, keepdims=True)
    g_sel = first_argmax(gl, g_max) - N_EXPERTS
    g_gate = 1.0 / jnp.sum(jnp.where(g_mask, jnp.exp(gl - g_max), 0.0), axis=-1, keepdims=True)

    e_lo = g_sel * EXPERTS_PER_GROUP
    el = jnp.where((lane >= e_lo) & (lane < e_lo + EXPERTS_PER_GROUP), logits, neg)
    v1 = jnp.max(el, axis=-1, keepdims=True)
    i1 = first_argmax(el, v1)
    el2 = jnp.where(lane == i1, neg, el)
    v2 = jnp.max(el2, axis=-1, keepdims=True)
    i2 = first_argmax(el2, v2)
    e2 = jnp.exp(v2 - v1)
    denom = 1.0 + e2
    w1 = (1.0 / denom) * g_gate
    w2 = (e2 / denom) * g_gate
    route = jnp.where(lane == ROUTE_E1, i1, 0.0) + jnp.where(lane == ROUTE_E2, i2, 0.0)
    return route + jnp.where(lane == ROUTE_W1, w1, 0.0) + jnp.where(lane == ROUTE_W2, w2, 0.0)


def _norm2(x1, g):
    ms = jnp.mean(x1 * x1, axis=-1, keepdims=True)
    return (x1 * lax.rsqrt(ms + EPS)) * g


def _out_proj_kernel(a_ref, r_ref, x_ref, wa_ref, wr_ref, g_ref, rhi_ref, rlo_ref,
                     x1_ref, h2_ref, route_ref):
    mixed = _dot(a_ref[...], wa_ref[...]) + _dot(r_ref[...], wr_ref[...])
    x1 = x_ref[...] + mixed
    x1_ref[...] = x1
    h2 = _norm2(x1, g_ref[...])
    h2_ref[...] = h2
    hi = h2.astype(jnp.bfloat16)
    lo = (h2 - hi.astype(jnp.float32)).astype(jnp.bfloat16)
    logits = _dot(hi, rhi_ref[...]) + (_dot(hi, rlo_ref[...]) + _dot(lo, rhi_ref[...]))
    route_ref[...] = _routing(logits)


def _out_proj(attn2d, ret2d, x2d, w_out_bf, g2, r_hi, r_lo, *, tm):
    n, d = x2d.shape
    return pl.pallas_call(
        _out_proj_kernel,
        out_shape=(
            jax.ShapeDtypeStruct((n, d), jnp.float32),
            jax.ShapeDtypeStruct((n, d), jnp.float32),
            jax.ShapeDtypeStruct((n, LANES), jnp.float32),
        ),
        grid=(n // tm,),
        in_specs=[
            pl.BlockSpec((tm, ATTN_WIDTH), lambda i: (i, 0)),
            pl.BlockSpec((tm, RET_WIDTH), lambda i: (i, 0)),
            pl.BlockSpec((tm, d), lambda i: (i, 0)),
            pl.BlockSpec((ATTN_WIDTH, d), lambda i: (0, 0)),
            pl.BlockSpec((RET_WIDTH, d), lambda i: (ATTN_WIDTH // RET_WIDTH, 0)),
            pl.BlockSpec((1, d), lambda i: (0, 0)),
            pl.BlockSpec((d, LANES), lambda i: (0, 0)),
            pl.BlockSpec((d, LANES), lambda i: (0, 0)),
        ],
        out_specs=(
            pl.BlockSpec((tm, d), lambda i: (i, 0)),
            pl.BlockSpec((tm, d), lambda i: (i, 0)),
            pl.BlockSpec((tm, LANES), lambda i: (i, 0)),
        ),
        compiler_params=_params("arbitrary"),
        name="out_proj",
    )(attn2d, ret2d, x2d, w_out_bf, w_out_bf, g2, r_hi, r_lo)


def _plan_kernel(route_ref, pos_ref, cnt_ref, rank_scr, *, tm, blk):
    n = route_ref.shape[0]
    lane = lax.broadcasted_iota(jnp.int32, (blk, LANES), 1).astype(jnp.float32)
    before = (lax.broadcasted_iota(jnp.int32, (blk, blk), 1)
              < lax.broadcasted_iota(jnp.int32, (blk, blk), 0)).astype(jnp.bfloat16)

    def one_hot(rows):
        r = route_ref[rows, :]
        e1 = r[:, ROUTE_E1:ROUTE_E1 + 1]
        e2 = r[:, ROUTE_E2:ROUTE_E2 + 1]
        return lane == e1, lane == e2

    def rank_body(b, run):
        rows = pl.ds(pl.multiple_of(b * blk, blk), blk)
        m1, m2 = one_hot(rows)
        sel = jnp.where(m1 | m2, 1.0, 0.0)
        rank_scr[rows, :] = _dot(before, sel.astype(jnp.bfloat16)) + run
        return run + jnp.sum(sel, axis=0, keepdims=True)
    cnt = lax.fori_loop(0, n // blk, rank_body, jnp.zeros((1, LANES), jnp.float32))
    cnt_ref[...] = jnp.broadcast_to(cnt, cnt_ref.shape)

    tiles = jnp.floor((cnt + (tm - 1.0)) * (1.0 / tm))
    below = (lax.broadcasted_iota(jnp.int32, (LANES, LANES), 0)
             < lax.broadcasted_iota(jnp.int32, (LANES, LANES), 1)).astype(jnp.bfloat16)
    start = _dot(jnp.broadcast_to(tiles, (8, LANES)).astype(jnp.bfloat16), below)[0:1, :] * float(tm)

    def pos_body(b, carry):
        rows = pl.ds(pl.multiple_of(b * blk, blk), blk)
        m1, m2 = one_hot(rows)
        dest = rank_scr[rows, :] + start
        p1 = jnp.sum(jnp.where(m1, dest, 0.0), axis=-1, keepdims=True)
        p2 = jnp.sum(jnp.where(m2, dest, 0.0), axis=-1, keepdims=True)
        pos_ref[rows, :] = (jnp.where(lane == 0.0, p1, 0.0) + jnp.where(lane == 1.0, p2, 0.0)).astype(jnp.int32)
        return carry
    lax.fori_loop(0, n // blk, pos_body, 0)


def _moe_plan(route, *, tm, blk):
    n = route.shape[0]
    return pl.pallas_call(
        functools.partial(_plan_kernel, tm=tm, blk=blk),
        out_shape=(jax.ShapeDtypeStruct((n, LANES), jnp.int32),
                   jax.ShapeDtypeStruct((8, LANES), jnp.float32)),
        scratch_shapes=[pltpu.VMEM((n, LANES), jnp.float32)],
        compiler_params=pltpu.CompilerParams(vmem_limit_bytes=VMEM_LIMIT),
        name="moe_plan",
    )(route)


def _row_copy(src, src_row, dst, dst_row, sem):
    return pltpu.make_async_copy(src.at[pl.ds(src_row, 1), :], dst.at[pl.ds(dst_row, 1), :], sem)


def _invert_kernel(pos_ref, sid_ref):
    def body(j, carry):
        sid_ref[pos_ref[j]] = j
        return carry
    lax.fori_loop(0, pos_ref.shape[0], body, 0, unroll=8)


def _moe_invert(pos, *, rows):
    return pl.pallas_call(
        _invert_kernel,
        out_shape=jax.ShapeDtypeStruct((rows,), jnp.int32),
        in_specs=[pl.BlockSpec(memory_space=pltpu.SMEM)],
        out_specs=pl.BlockSpec(memory_space=pltpu.SMEM),
        name="moe_invert",
    )(pos)


ROW_GROUP = 8


def _ffn_kernel(te_ref, tv_ref, nu_ref, sid_ref, h2_ref, w1_ref, w3_ref, w2_ref, y2_ref,
                xbuf, ybuf, gsem, ssem):
    t = pl.program_id(0)
    n_used = nu_ref[0]
    tm = xbuf.shape[1]

    n_tok = h2_ref.shape[0]

    def gather(slot):
        return lambda r, sid: _row_copy(h2_ref, jnp.where(sid >= n_tok, sid - n_tok, sid),
                                        xbuf.at[slot], r, gsem.at[slot])

    def scatter(slot):
        return lambda r, sid: _row_copy(ybuf.at[slot], r, y2_ref, sid, ssem.at[slot])

    def start_partial(tile, copy):
        valid = tv_ref[tile]

        def body(c, carry):
            for u in range(ROW_GROUP):
                r = c * ROW_GROUP + u

                @pl.when(r < valid)
                def _():
                    copy(r, sid_ref[tile * tm + r]).start()
            return carry
        lax.fori_loop(0, (valid + (ROW_GROUP - 1)) // ROW_GROUP, body, 0)

    def start_all(tile, copy):
        @pl.when(tv_ref[tile] == tm)
        def _():
            for r in range(tm):
                copy(r, sid_ref[tile * tm + r]).start()

        @pl.when(tv_ref[tile] < tm)
        def _():
            start_partial(tile, copy)

    def wait_all(tile, copy, tile_copy):
        @pl.when(tv_ref[tile] == tm)
        def _():
            tile_copy.wait()

        @pl.when(tv_ref[tile] < tm)
        def _():
            def body(r, carry):
                copy(0, 0).wait()
                return carry
            lax.fori_loop(0, tv_ref[tile], body, 0)

    def wait_gather(tile, slot):
        wait_all(tile, gather(slot),
                 pltpu.make_async_copy(h2_ref.at[pl.ds(0, tm), :], xbuf.at[slot], gsem.at[slot]))

    def wait_scatter(tile, slot):
        wait_all(tile, scatter(slot),
                 pltpu.make_async_copy(ybuf.at[slot], y2_ref.at[pl.ds(0, tm), :], ssem.at[slot]))

    @pl.when(t == 0)
    def _():
        start_partial(0, gather(0))

    @pl.when(t < n_used)
    def _():
        slot = t % 2

        @pl.when(t + 1 < n_used)
        def _():
            start_all(t + 1, gather(1 - slot))

        wait_gather(t, slot)

        @pl.when(t >= 2)
        def _():
            wait_scatter(t - 2, slot)

        row = lax.broadcasted_iota(jnp.int32, xbuf.shape[1:], 0)
        x = jnp.where(row < tv_ref[t], xbuf[slot], 0.0).astype(jnp.bfloat16)
        a = _dot(x, w1_ref[0].astype(jnp.bfloat16))
        b = _dot(x, w3_ref[0].astype(jnp.bfloat16))
        act = (a * (1.0 / (1.0 + jnp.exp(-a)))) * b
        ybuf[slot] = _dot(act.astype(jnp.bfloat16), w2_ref[0].astype(jnp.bfloat16))
        start_all(t, scatter(slot))

    @pl.when(t == pl.num_programs(0) - 1)
    def _():
        @pl.when(n_used >= 2)
        def _():
            wait_scatter(n_used - 2, n_used % 2)
        wait_scatter(n_used - 1, (n_used - 1) % 2)


def _moe_ffn(tile_expert, tile_valid, n_used, row_sid, h2, w1, w3, w2, *, tm):
    n, d = h2.shape
    _, _, ff = w1.shape
    max_tiles = tile_expert.shape[0]

    def expert(t, te, tv, nu, sid):
        return te[jnp.minimum(t, nu[0] - 1)]

    return pl.pallas_call(
        _ffn_kernel,
        out_shape=jax.ShapeDtypeStruct((2 * n, d), jnp.float32),
        grid_spec=pltpu.PrefetchScalarGridSpec(
            num_scalar_prefetch=4,
            grid=(max_tiles,),
            in_specs=[
                pl.BlockSpec(memory_space=pl.ANY),
                pl.BlockSpec((1, d, ff), lambda *a: (expert(*a), 0, 0)),
                pl.BlockSpec((1, d, ff), lambda *a: (expert(*a), 0, 0)),
                pl.BlockSpec((1, ff, d), lambda *a: (expert(*a), 0, 0)),
            ],
            out_specs=pl.BlockSpec(memory_space=pl.ANY),
            scratch_shapes=[
                pltpu.VMEM((2, tm, d), jnp.float32),
                pltpu.VMEM((2, tm, d), jnp.float32),
                pltpu.SemaphoreType.DMA((2,)),
                pltpu.SemaphoreType.DMA((2,)),
            ],
        ),
        compiler_params=_params("arbitrary"),
        name="moe_ffn",
    )(tile_expert, tile_valid, n_used, row_sid, h2, w1, w3, w2)


def _combine_kernel(x1_ref, route_ref, ya_ref, yb_ref, o_ref):
    route = route_ref[...]
    w1 = route[:, ROUTE_W1:ROUTE_W1 + 1]
    w2 = route[:, ROUTE_W2:ROUTE_W2 + 1]
    o_ref[...] = x1_ref[...] + (w1 * ya_ref[...] + w2 * yb_ref[...])


def _moe_combine(x1, route, y2, *, tm):
    n, d = x1.shape
    return pl.pallas_call(
        _combine_kernel,
        out_shape=jax.ShapeDtypeStruct((n, d), jnp.float32),
        grid=(n // tm,),
        in_specs=[pl.BlockSpec((tm, d), lambda i: (i, 0)),
                  pl.BlockSpec((tm, LANES), lambda i: (i, 0)),
                  pl.BlockSpec((tm, d), lambda i: (i, 0)),
                  pl.BlockSpec((tm, d), lambda i: (i + n // tm, 0))],
        out_specs=pl.BlockSpec((tm, d), lambda i: (i, 0)),
        compiler_params=_params("arbitrary"),
        name="moe_combine",
    )(x1, route, y2, y2)


def _moe(x1, h2, route, w1, w3, w2, *, tm, gather_tm):
    n, d = x1.shape
    max_tiles = (2 * n) // tm + N_EXPERTS
    pos2d, cnt = _moe_plan(route, tm=tm, blk=gather_tm)
    pos = pos2d[:, :2].T.reshape(2 * n)
    counts = cnt[0, :N_EXPERTS].astype(jnp.int32)
    tiles = (counts + (tm - 1)) // tm
    ends = jnp.cumsum(tiles)
    t_idx = jnp.arange(max_tiles, dtype=jnp.int32)
    tile_expert = jnp.sum((ends[None, :] <= t_idx[:, None]).astype(jnp.int32), axis=1)
    tile_expert = jnp.minimum(tile_expert, N_EXPERTS - 1)
    first_tile = (ends - tiles)[tile_expert]
    tile_valid = jnp.clip(counts[tile_expert] - (t_idx - first_tile) * tm, 0, tm).astype(jnp.int32)
    n_used = ends[-1:].astype(jnp.int32)

    row_sid = _moe_invert(pos, rows=max_tiles * tm)
    y2 = _moe_ffn(tile_expert, tile_valid, n_used, row_sid, h2, w1, w3, w2, tm=tm)
    return _moe_combine(x1, route, y2, tm=gather_tm)


def _pack_router(w_group, w_router):
    d = w_group.shape[0]
    experts = jnp.transpose(w_router, (1, 0, 2)).reshape(d, N_EXPERTS)
    wr = jnp.concatenate(
        [experts, w_group, jnp.zeros((d, LANES - N_EXPERTS - N_GROUPS), w_group.dtype)], axis=1)
    hi = wr.astype(jnp.bfloat16)
    lo = (wr - hi.astype(jnp.float32)).astype(jnp.bfloat16)
    return hi, lo


def _rotation_tables(seq):
    half = RET_DK // 2
    pos = jnp.arange(seq, dtype=jnp.float32)
    inv = 1.0 / (ROT_BASE ** jnp.linspace(0.0, 1.0, half, dtype=jnp.float32))
    ang = pos[:, None] * inv[None, :]
    c, s = jnp.cos(ang), jnp.sin(ang)
    return jnp.concatenate([c, c], axis=-1), jnp.concatenate([-s, s], axis=-1)


def _tiles(n, seq):
    def fit(total, want):
        t = min(total, want)
        while total % t:
            t //= 2
        return t
    return dict(
        proj_tm=fit(n, 1024),
        attn_tq=fit(seq, 256), attn_tk=fit(seq, 256),
        ret_rc=fit(seq, 256),
        out_tm=fit(n, 256),
        moe_tm=fit(n, 256), moe_gather_tm=fit(n, 256),
    )


def kernel(x, norm1_g, w_in, q_norm_g, k_norm_g, idx_k_ln_w, idx_k_ln_b, ret_norm_g,
           w_out, norm2_g, w_group, w_router, w1, w3, w2):
    b, seq, d = x.shape
    n = b * seq
    depth = w_in.shape[0]
    t = _tiles(n, seq)
    cos2, sin2 = _rotation_tables(seq)
    log_gamma = jnp.log1p(-jnp.exp2(-5.0 - jnp.arange(RET_HEADS, dtype=jnp.float32)))

    x2d = x.reshape(n, d)
    for l in range(depth):
        proj = _in_proj(x2d, norm1_g[l][None, :], w_in[l].T, tm=t["proj_tm"])
        p3 = proj.reshape(b, seq, proj.shape[1])
        attn = _dsa_attention(p3, q_norm_g[l][None, :], k_norm_g[l][None, :],
                              idx_k_ln_w[l][None, :], idx_k_ln_b[l][None, :],
                              tq=t["attn_tq"], tk=t["attn_tk"])
        ret = _retention(p3, log_gamma, cos2, sin2, ret_norm_g[l].reshape(RET_HEADS, 1, RET_DV),
                         rc=t["ret_rc"])
        r_hi, r_lo = _pack_router(w_group[l], w_router[l])
        g2 = norm2_g[l][None, :]
        x1, h2, route = _out_proj(attn.reshape(n, ATTN_WIDTH), ret.reshape(n, RET_WIDTH), x2d,
                                  w_out[l].astype(jnp.bfloat16), g2, r_hi, r_lo, tm=t["out_tm"])
        x2d = _moe(x1, h2, route, w1[l], w3[l], w2[l], tm=t["moe_tm"], gather_tm=t["moe_gather_tm"])
    return x2d.reshape(b, seq, d)
```

```python
import functools
import math

import jax
import jax.numpy as jnp
from jax import lax
from jax.experimental import pallas as pl
from jax.experimental.pallas import tpu as pltpu

CHUNK = 64
ATTN_HEADS = 8
HEAD_DIM = 128
KV_HEADS = 2
HEADS_PER_KV = ATTN_HEADS // KV_HEADS
IDX_HEADS = 16
IDX_DIM = 64
TOPK_MAX = 256
RET_HEADS = 8
RET_DK = 128
RET_DV = 128
ROT_BASE = 10000.0
N_GROUPS = 4
EXPERTS_PER_GROUP = 8
N_EXPERTS = N_GROUPS * EXPERTS_PER_GROUP
EPS = 1e-6

ATTN_WIDTH = ATTN_HEADS * HEAD_DIM
KV_WIDTH = KV_HEADS * HEAD_DIM
IDX_WIDTH = IDX_HEADS * IDX_DIM
RET_WIDTH = RET_HEADS * RET_DK

LANES = 128
VMEM_LIMIT = 56 * 1024 * 1024

AQ_OFF = 0
AK_OFF = AQ_OFF + ATTN_WIDTH
AV_OFF = AK_OFF + KV_WIDTH
IQ_OFF = AV_OFF + KV_WIDTH
IK_OFF = IQ_OFF + IDX_WIDTH
IW_OFF = IK_OFF + IDX_DIM
W_RET = IW_OFF + IDX_HEADS
IN_WIDTH = W_RET + 4 * RET_WIDTH
assert IW_OFF // LANES == IK_OFF // LANES
PROJ_TN = 1024
RQ_OFF = -(-W_RET // PROJ_TN) * PROJ_TN
RK_OFF = RQ_OFF + RET_WIDTH
RV_OFF = RK_OFF + RET_WIDTH
RG_OFF = RV_OFF + RET_WIDTH
PROJ_WIDTH = RG_OFF + RET_WIDTH

ROUTE_E1, ROUTE_E2, ROUTE_W1, ROUTE_W2 = 0, 1, 2, 3

INT_MIN = -(2 ** 31)
NEG_BIG = -1e30

_NT = (((1,), (1,)), ((), ()))


def _dot(a, b):
    return jnp.dot(a, b, preferred_element_type=jnp.float32)


def _dot_nt(a, b):
    return lax.dot_general(a, b, _NT, preferred_element_type=jnp.float32)


def _params(*sem):
    return pltpu.CompilerParams(dimension_semantics=sem, vmem_limit_bytes=VMEM_LIMIT)


def _in_proj_kernel(x_ref, g_ref, wt_ref, o_ref, h_scr, *, row_chunk):
    @pl.when(pl.program_id(1) == 0)
    def _():
        def body(c, carry):
            rows = pl.ds(pl.multiple_of(c * row_chunk, row_chunk), row_chunk)
            x = x_ref[rows, :]
            ms = jnp.mean(x * x, axis=-1, keepdims=True)
            h_scr[rows, :] = ((x * lax.rsqrt(ms + EPS)) * g_ref[...]).astype(jnp.bfloat16)
            return carry
        lax.fori_loop(0, x_ref.shape[0] // row_chunk, body, 0)

    o_ref[...] = _dot_nt(h_scr[...], wt_ref[...].astype(jnp.bfloat16)).astype(o_ref.dtype)


def _in_proj(x2d, g, w_in_t, *, tm):
    n, d = x2d.shape
    tn = PROJ_TN
    assert w_in_t.shape == (IN_WIDTH, d) and W_RET % 8 == 0
    attn_tiles = RQ_OFF // tn

    def window(i, j):
        start8 = jnp.where(j < attn_tiles, j * (tn // 8), W_RET // 8 + (j - attn_tiles) * (tn // 8))
        return 8 * start8, 0

    return pl.pallas_call(
        functools.partial(_in_proj_kernel, row_chunk=min(tm, 128)),
        out_shape=jax.ShapeDtypeStruct((n, PROJ_WIDTH), jnp.bfloat16),
        grid=(n // tm, PROJ_WIDTH // tn),
        in_specs=[
            pl.BlockSpec((tm, d), lambda i, j: (i, 0)),
            pl.BlockSpec((1, d), lambda i, j: (0, 0)),
            pl.BlockSpec((pl.Element(tn), pl.Element(d)), window),
        ],
        out_specs=pl.BlockSpec((tm, tn), lambda i, j: (i, j)),
        scratch_shapes=[pltpu.VMEM((tm, d), jnp.bfloat16)],
        compiler_params=_params("arbitrary", "arbitrary"),
        name="in_proj",
    )(x2d, g, w_in_t)


def _ordered_float(v):
    bits = v ^ ((v >> 31) & jnp.int32(0x7FFFFFFF))
    return pltpu.bitcast(bits, jnp.float32)


def _attn_kernel(aq_ref, iqa_ref, iqb_ref, iw_ref, ak_ref, av_ref, ik_ref, qg_ref, kg_ref, lnw_ref,
                 lnb_ref, o_ref,
                 kn_scr, ikn_scr, vt_scr, key_scr, iqh_scr, wt_scr, qn_scr, acc_scr, m_scr, l_scr,
                 sa_scr, sb_scr, *, tk, topk, idx_w_scale):
    i = pl.program_id(1)
    seq = ak_ref.shape[1]
    tq = aq_ref.shape[1]
    chunk_shift = CHUNK.bit_length() - 1

    @pl.when(i == 0)
    def _():
        def body(c, carry):
            rows = pl.ds(pl.multiple_of(c * tk, tk), tk)
            for g in range(KV_HEADS):
                cols = slice(g * HEAD_DIM, (g + 1) * HEAD_DIM)
                k = ak_ref[0, rows, cols].astype(jnp.float32)
                ms = jnp.mean(k * k, axis=-1, keepdims=True)
                kn_scr[rows, cols] = ((k * lax.rsqrt(ms + EPS)) * kg_ref[...]).astype(jnp.bfloat16)
                v = av_ref[0, rows, cols].astype(jnp.float32)
                vt_scr[g, c] = v.T.astype(jnp.bfloat16)
            ki = ik_ref[0, rows, :IDX_DIM].astype(jnp.float32)
            mu = jnp.mean(ki, axis=-1, keepdims=True)
            var = jnp.mean(jnp.square(ki - mu), axis=-1, keepdims=True)
            y = (ki - mu) * lax.rsqrt(var + EPS)
            ikn_scr[rows, :] = (y * lnw_ref[...] + lnb_ref[...]).astype(jnp.bfloat16)
            return carry
        lax.fori_loop(0, seq // tk, body, 0)

    t0 = i * tq
    n_kt = (t0 + tq) // tk
    scale = (HEAD_DIM ** -0.5) * math.log2(math.e)
    for h in range(ATTN_HEADS):
        g, r = divmod(h, HEADS_PER_KV)
        q = aq_ref[0, :, h * HEAD_DIM:(h + 1) * HEAD_DIM].astype(jnp.float32)
        ms = jnp.mean(q * q, axis=-1, keepdims=True)
        qn_scr[g, r * tq:(r + 1) * tq, :] = (
            (q * lax.rsqrt(ms + EPS)) * qg_ref[...] * scale).astype(jnp.bfloat16)
    half = IDX_HEADS // 2
    for h in range(IDX_HEADS):
        src = iqa_ref if h < half else iqb_ref
        iqh_scr[h] = src[0, :, (h % half) * IDX_DIM:(h % half + 1) * IDX_DIM]
    wt_scr[...] = iw_ref[0].astype(jnp.float32).T * idx_w_scale
    w_row = IW_OFF % LANES

    q_chunk = (t0 + lax.broadcasted_iota(jnp.int32, (tk, tq), 1)) >> chunk_shift

    def score_body(kt, carry):
        rows = pl.ds(pl.multiple_of(kt * tk, tk), tk)
        ik_t = ikn_scr[rows, :]
        acc = jnp.zeros((tk, tq), jnp.float32)
        for h in range(IDX_HEADS):
            d = _dot_nt(ik_t, iqh_scr[h])
            acc = acc + jnp.maximum(d, 0.0) * wt_scr[w_row + h:w_row + h + 1, :]
        k_chunk = (kt * tk + lax.broadcasted_iota(jnp.int32, (tk, tq), 0)) >> chunk_shift
        key_scr[rows, :] = jnp.where(k_chunk <= q_chunk, acc, -jnp.inf)
        return carry
    lax.fori_loop(0, n_kt, score_body, 0)

    def bit_body(it, lo):
        cand = lo + lax.shift_left(jnp.int32(1), 31 - it)
        cand_f = _ordered_float(cand)

        def count_body(kt, part):
            rows = pl.ds(pl.multiple_of(kt * tk, tk), tk)
            hit = jnp.where(key_scr[rows, :] >= cand_f, 1.0, 0.0).reshape(tk // 8, 8, tq)
            while hit.shape[0] > 1:
                half = hit.shape[0] // 2
                hit = hit[:half] + hit[half:]
            return part + hit[0]
        part = lax.fori_loop(0, n_kt, count_body, jnp.zeros((8, tq), jnp.float32))
        cnt = jnp.sum(part, axis=0, keepdims=True)
        return jnp.where(cnt >= float(topk), cand, lo)
    lo = lax.fori_loop(0, 32, bit_body, jnp.full((1, tq), INT_MIN, jnp.int32))
    thr = jnp.where(lo == INT_MIN, jnp.finfo(jnp.float32).min, _ordered_float(lo))

    m_scr[...] = jnp.full(m_scr.shape, NEG_BIG, jnp.float32)
    l_scr[...] = jnp.zeros(l_scr.shape, jnp.float32)
    acc_scr[...] = jnp.zeros(acc_scr.shape, jnp.float32)

    def logits(kt, s_ref):
        rows = pl.ds(pl.multiple_of(kt * tk, tk), tk)
        for g in range(KV_HEADS):
            s_ref[g] = _dot_nt(kn_scr[rows, g * HEAD_DIM:(g + 1) * HEAD_DIM], qn_scr[g])

    def consume(kt, s_ref):
        rows = pl.ds(pl.multiple_of(kt * tk, tk), tk)
        bias = jnp.where(key_scr[rows, :] >= thr, 0.0, NEG_BIG)
        bias = jnp.concatenate([bias] * HEADS_PER_KV, axis=1)
        for g in range(KV_HEADS):
            s = s_ref[g] + bias
            m_old = m_scr[g]
            m_new = jnp.maximum(m_old, jnp.max(s, axis=0, keepdims=True))
            p = jnp.exp2(s - m_new)
            alpha = jnp.exp2(m_old - m_new)
            l_scr[g] = alpha * l_scr[g] + jnp.sum(p, axis=0, keepdims=True)
            acc_scr[g] = acc_scr[g] * alpha + _dot(vt_scr[g, kt], p.astype(jnp.bfloat16))
            m_scr[g] = m_new

    logits(0, sa_scr)

    def pair_body(j, carry):
        consume(2 * j, sa_scr)
        logits(2 * j + 1, sb_scr)
        consume(2 * j + 1, sb_scr)
        logits(jnp.minimum(2 * j + 2, n_kt - 1), sa_scr)
        return carry
    lax.fori_loop(0, n_kt // 2, pair_body, 0)

    @pl.when(n_kt % 2 == 1)
    def _():
        consume(n_kt - 1, sa_scr)

    for h in range(ATTN_HEADS):
        g, r = divmod(h, HEADS_PER_KV)
        o = acc_scr[g, :, r * tq:(r + 1) * tq] / l_scr[g, :, r * tq:(r + 1) * tq]
        o_ref[0, :, h * HEAD_DIM:(h + 1) * HEAD_DIM] = o.T.astype(o_ref.dtype)


def _dsa_attention(p3, q_g, k_g, ln_w, ln_b, *, tq, tk):
    b, seq, _ = p3.shape
    topk = min(TOPK_MAX, seq // 4)
    idx_w_scale = (IDX_HEADS ** -0.5) * (IDX_DIM ** -0.5)
    assert seq % tq == 0 and tq % tk == 0 and tk % CHUNK == 0

    def col(off, width):
        assert off % width == 0 or width == LANES
        return off // width

    half_iq = IDX_WIDTH // 2
    return pl.pallas_call(
        functools.partial(_attn_kernel, tk=tk, topk=topk, idx_w_scale=idx_w_scale),
        out_shape=jax.ShapeDtypeStruct((b, seq, ATTN_WIDTH), jnp.bfloat16),
        grid=(b, seq // tq),
        in_specs=[
            pl.BlockSpec((1, tq, ATTN_WIDTH), lambda bi, i: (bi, i, col(AQ_OFF, ATTN_WIDTH))),
            pl.BlockSpec((1, tq, half_iq), lambda bi, i: (bi, i, col(IQ_OFF, half_iq))),
            pl.BlockSpec((1, tq, half_iq), lambda bi, i: (bi, i, col(IQ_OFF, half_iq) + 1)),
            pl.BlockSpec((1, tq, LANES), lambda bi, i: (bi, i, col(IW_OFF, LANES))),
            pl.BlockSpec((1, seq, KV_WIDTH), lambda bi, i: (bi, 0, col(AK_OFF, KV_WIDTH))),
            pl.BlockSpec((1, seq, KV_WIDTH), lambda bi, i: (bi, 0, col(AV_OFF, KV_WIDTH))),
            pl.BlockSpec((1, seq, LANES), lambda bi, i: (bi, 0, col(IK_OFF, LANES))),
            pl.BlockSpec((1, HEAD_DIM), lambda bi, i: (0, 0)),
            pl.BlockSpec((1, HEAD_DIM), lambda bi, i: (0, 0)),
            pl.BlockSpec((1, IDX_DIM), lambda bi, i: (0, 0)),
            pl.BlockSpec((1, IDX_DIM), lambda bi, i: (0, 0)),
        ],
        out_specs=pl.BlockSpec((1, tq, ATTN_WIDTH), lambda bi, i: (bi, i, 0)),
        scratch_shapes=[
            pltpu.VMEM((seq, KV_WIDTH), jnp.bfloat16),
            pltpu.VMEM((seq, IDX_DIM), jnp.bfloat16),
            pltpu.VMEM((KV_HEADS, seq // tk, HEAD_DIM, tk), jnp.bfloat16),
            pltpu.VMEM((seq, tq), jnp.float32),
            pltpu.VMEM((IDX_HEADS, tq, IDX_DIM), jnp.bfloat16),
            pltpu.VMEM((LANES, tq), jnp.float32),
            pltpu.VMEM((KV_HEADS, HEADS_PER_KV * tq, HEAD_DIM), jnp.bfloat16),
            pltpu.VMEM((KV_HEADS, HEAD_DIM, HEADS_PER_KV * tq), jnp.float32),
            pltpu.VMEM((KV_HEADS, 1, HEADS_PER_KV * tq), jnp.float32),
            pltpu.VMEM((KV_HEADS, 1, HEADS_PER_KV * tq), jnp.float32),
            pltpu.VMEM((KV_HEADS, tk, HEADS_PER_KV * tq), jnp.float32),
            pltpu.VMEM((KV_HEADS, tk, HEADS_PER_KV * tq), jnp.float32),
        ],
        compiler_params=_params("arbitrary", "arbitrary"),
        name="dsa_attn",
    )(p3, p3, p3, p3, p3, p3, p3, q_g, k_g, ln_w, ln_b)


def _ret_kernel(lg_ref, rq_ref, rk_ref, rv_ref, rg_ref, cos_ref, sin_ref, g_ref, o_ref, *, rc):
    h = pl.program_id(1)
    lg = lg_ref[h]
    seq = rq_ref.shape[1]
    n = lax.broadcasted_iota(jnp.int32, (rc, RET_DV), 0).astype(jnp.float32)
    cross_decay = jnp.exp(lg * (n + 1.0))
    state_decay = jnp.exp(lg * (rc - 1.0 - n))
    chunk_decay = jnp.exp(lg * jnp.full((RET_DK, RET_DV), float(rc), jnp.float32))
    rel = (lax.broadcasted_iota(jnp.int32, (rc, rc), 0)
           - lax.broadcasted_iota(jnp.int32, (rc, rc), 1)).astype(jnp.float32)
    intra = jnp.where(rel >= 0, jnp.exp(lg * jnp.maximum(rel, 0.0)), 0.0)

    def rot(x, rows):
        return x * cos_ref[rows, :] + pltpu.roll(x, RET_DK // 2, 1) * sin_ref[rows, :]

    state = jnp.zeros((RET_DK, RET_DV), jnp.float32)
    for c in range(seq // rc):
        rows = slice(c * rc, (c + 1) * rc)
        q = rot(rq_ref[0, rows, :].astype(jnp.float32), rows)
        k = rot(rk_ref[0, rows, :].astype(jnp.float32), rows) * (RET_DK ** -0.5)
        v = rv_ref[0, rows, :]
        qb = q.astype(jnp.bfloat16)
        inner = _dot_nt(qb, k.astype(jnp.bfloat16)) * intra
        o = _dot(inner.astype(jnp.bfloat16), v) + _dot(qb, state.astype(jnp.bfloat16)) * cross_decay
        kd_t = (k * state_decay).T.astype(jnp.bfloat16)
        state = state * chunk_decay + _dot(kd_t, v)
        ms = jnp.mean(o * o, axis=-1, keepdims=True)
        y = (o * lax.rsqrt(ms + EPS)) * g_ref[0]
        gate = rg_ref[0, rows, :].astype(jnp.float32)
        o_ref[0, rows, :] = ((gate * (1.0 / (1.0 + jnp.exp(-gate)))) * y).astype(o_ref.dtype)


def _retention(p3, log_gamma, cos2, sin2, ret_g, *, rc):
    b, seq, _ = p3.shape
    assert seq % rc == 0

    def head_spec(off):
        return pl.BlockSpec((1, seq, RET_DK), lambda bi, h: (bi, 0, off // RET_DK + h))

    return pl.pallas_call(
        functools.partial(_ret_kernel, rc=rc),
        out_shape=jax.ShapeDtypeStruct((b, seq, RET_WIDTH), jnp.bfloat16),
        grid=(b, RET_HEADS),
        in_specs=[
            pl.BlockSpec(memory_space=pltpu.SMEM),
            head_spec(RQ_OFF), head_spec(RK_OFF), head_spec(RV_OFF), head_spec(RG_OFF),
            pl.BlockSpec((seq, RET_DK), lambda bi, h: (0, 0)),
            pl.BlockSpec((seq, RET_DK), lambda bi, h: (0, 0)),
            pl.BlockSpec((1, 1, RET_DV), lambda bi, h: (h, 0, 0)),
        ],
        out_specs=pl.BlockSpec((1, seq, RET_DV), lambda bi, h: (bi, 0, h)),
        compiler_params=_params("arbitrary", "arbitrary"),
        name="retention",
    )(log_gamma, p3, p3, p3, p3, cos2, sin2, ret_g)


def _routing(logits):
    lane = lax.broadcasted_iota(jnp.int32, logits.shape, 1).astype(jnp.float32)
    big = float(LANES)
    neg = -jnp.inf

    def first_argmax(v, vmax):
        return jnp.min(jnp.where(v == vmax, lane, big), axis=-1, keepdims=True)

    g_mask = (lane >= N_EXPERTS) & (lane < N_EXPERTS + N_GROUPS)
    gl = jnp.where(g_mask, logits, neg)
    g_max = jnp.max(gl, axis=-1, keepdims=True)
    g_sel = first_argmax(gl, g_max) - N_EXPERTS
    g_gate = 1.0 / jnp.sum(jnp.where(g_mask, jnp.exp(gl - g_max), 0.0), axis=-1, keepdims=True)

    e_lo = g_sel * EXPERTS_PER_GROUP
    el = jnp.where((lane >= e_lo) & (lane < e_lo + EXPERTS_PER_GROUP), logits, neg)
    v1 = jnp.max(el, axis=-1, keepdims=True)
    i1 = first_argmax(el, v1)
    el2 = jnp.where(lane == i1, neg, el)
    v2 = jnp.max(el2, axis=-1, keepdims=True)
    i2 = first_argmax(el2, v2)
    e2 = jnp.exp(v2 - v1)
    denom = 1.0 + e2
    w1 = (1.0 / denom) * g_gate
    w2 = (e2 / denom) * g_gate
    route = jnp.where(lane == ROUTE_E1, i1, 0.0) + jnp.where(lane == ROUTE_E2, i2, 0.0)
    return route + jnp.where(lane == ROUTE_W1, w1, 0.0) + jnp.where(lane == ROUTE_W2, w2, 0.0)


def _norm2(x1, g):
    ms = jnp.mean(x1 * x1, axis=-1, keepdims=True)
    return (x1 * lax.rsqrt(ms + EPS)) * g


def _out_proj_kernel(a_ref, r_ref, x_ref, wa_ref, wr_ref, g_ref, rhi_ref, rlo_ref,
                     x1_ref, h2_ref, route_ref):
    mixed = _dot(a_ref[...], wa_ref[...]) + _dot(r_ref[...], wr_ref[...])
    x1 = x_ref[...] + mixed
    x1_ref[...] = x1
    h2 = _norm2(x1, g_ref[...])
    _store_token_major(h2_ref, h2)
    hi = h2.astype(jnp.bfloat16)
    lo = (h2 - hi.astype(jnp.float32)).astype(jnp.bfloat16)
    logits = _dot(hi, rhi_ref[...]) + (_dot(hi, rlo_ref[...]) + _dot(lo, rhi_ref[...]))
    route_ref[...] = _routing(logits)


def _out_proj(attn2d, ret2d, x2d, w_out_bf, g2, r_hi, r_lo, *, tm):
    n, d = x2d.shape
    return pl.pallas_call(
        _out_proj_kernel,
        out_shape=(
            jax.ShapeDtypeStruct((n, d), jnp.float32),
            jax.ShapeDtypeStruct((n * d // LANES, LANES), jnp.float32),
            jax.ShapeDtypeStruct((n, LANES), jnp.float32),
        ),
        grid=(n // tm,),
        in_specs=[
            pl.BlockSpec((tm, ATTN_WIDTH), lambda i: (i, 0)),
            pl.BlockSpec((tm, RET_WIDTH), lambda i: (i, 0)),
            pl.BlockSpec((tm, d), lambda i: (i, 0)),
            pl.BlockSpec((ATTN_WIDTH, d), lambda i: (0, 0)),
            pl.BlockSpec((RET_WIDTH, d), lambda i: (ATTN_WIDTH // RET_WIDTH, 0)),
            pl.BlockSpec((1, d), lambda i: (0, 0)),
            pl.BlockSpec((d, LANES), lambda i: (0, 0)),
            pl.BlockSpec((d, LANES), lambda i: (0, 0)),
        ],
        out_specs=(
            pl.BlockSpec((tm, d), lambda i: (i, 0)),
            pl.BlockSpec((tm * d // LANES, LANES), lambda i: (i, 0)),
            pl.BlockSpec((tm, LANES), lambda i: (i, 0)),
        ),
        compiler_params=_params("arbitrary"),
        name="out_proj",
    )(attn2d, ret2d, x2d, w_out_bf, w_out_bf, g2, r_hi, r_lo)


def _plan_kernel(route_ref, pos_ref, cnt_ref, rank_scr, *, tm, blk):
    n = route_ref.shape[0]
    lane = lax.broadcasted_iota(jnp.int32, (blk, LANES), 1).astype(jnp.float32)
    before = (lax.broadcasted_iota(jnp.int32, (blk, blk), 1)
              < lax.broadcasted_iota(jnp.int32, (blk, blk), 0)).astype(jnp.bfloat16)

    def one_hot(rows):
        r = route_ref[rows, :]
        e1 = r[:, ROUTE_E1:ROUTE_E1 + 1]
        e2 = r[:, ROUTE_E2:ROUTE_E2 + 1]
        return lane == e1, lane == e2

    def rank_body(b, run):
        rows = pl.ds(pl.multiple_of(b * blk, blk), blk)
        m1, m2 = one_hot(rows)
        sel = jnp.where(m1 | m2, 1.0, 0.0)
        rank_scr[rows, :] = _dot(before, sel.astype(jnp.bfloat16)) + run
        return run + jnp.sum(sel, axis=0, keepdims=True)
    cnt = lax.fori_loop(0, n // blk, rank_body, jnp.zeros((1, LANES), jnp.float32))
    cnt_ref[...] = jnp.broadcast_to(cnt, cnt_ref.shape)

    tiles = jnp.floor((cnt + (tm - 1.0)) * (1.0 / tm))
    below = (lax.broadcasted_iota(jnp.int32, (LANES, LANES), 0)
             < lax.broadcasted_iota(jnp.int32, (LANES, LANES), 1)).astype(jnp.bfloat16)
    start = _dot(jnp.broadcast_to(tiles, (8, LANES)).astype(jnp.bfloat16), below)[0:1, :] * float(tm)

    def pos_body(b, carry):
        rows = pl.ds(pl.multiple_of(b * blk, blk), blk)
        m1, m2 = one_hot(rows)
        dest = rank_scr[rows, :] + start
        p1 = jnp.sum(jnp.where(m1, dest, 0.0), axis=-1, keepdims=True)
        p2 = jnp.sum(jnp.where(m2, dest, 0.0), axis=-1, keepdims=True)
        pos_ref[rows, :] = (jnp.where(lane == 0.0, p1, 0.0) + jnp.where(lane == 1.0, p2, 0.0)).astype(jnp.int32)
        return carry
    lax.fori_loop(0, n // blk, pos_body, 0)


def _moe_plan(route, *, tm, blk):
    n = route.shape[0]
    return pl.pallas_call(
        functools.partial(_plan_kernel, tm=tm, blk=blk),
        out_shape=(jax.ShapeDtypeStruct((n, LANES), jnp.int32),
                   jax.ShapeDtypeStruct((8, LANES), jnp.float32)),
        scratch_shapes=[pltpu.VMEM((n, LANES), jnp.float32)],
        compiler_params=pltpu.CompilerParams(vmem_limit_bytes=VMEM_LIMIT),
        name="moe_plan",
    )(route)


def _store_token_major(ref, x):
    rows, d = x.shape
    ch = d // LANES
    for c in range(ch):
        ref[pl.ds(c, rows, stride=ch), :] = x[:, c * LANES:(c + 1) * LANES]


def _load_token_major(ref, rows, d):
    ch = d // LANES
    return jnp.concatenate([ref[pl.ds(c, rows, stride=ch), :] for c in range(ch)], axis=1)


def _row_copy(src, src_row, dst, dst_row, sem, ch):
    return pltpu.make_async_copy(src.at[pl.ds(src_row * ch, ch), :], dst.at[pl.ds(dst_row * ch, ch), :], sem)


def _invert_kernel(pos_ref, sid_ref):
    def body(j, carry):
        sid_ref[pos_ref[j]] = j
        return carry
    lax.fori_loop(0, pos_ref.shape[0], body, 0, unroll=8)


def _moe_invert(pos, *, rows):
    return pl.pallas_call(
        _invert_kernel,
        out_shape=jax.ShapeDtypeStruct((rows,), jnp.int32),
        in_specs=[pl.BlockSpec(memory_space=pltpu.SMEM)],
        out_specs=pl.BlockSpec(memory_space=pltpu.SMEM),
        name="moe_invert",
    )(pos)


ROW_GROUP = 8


def _ffn_kernel(te_ref, tv_ref, nu_ref, sid_ref, h2_ref, w1_ref, w3_ref, w2_ref, y2_ref,
                xbuf, ybuf, gsem, ssem):
    t = pl.program_id(0)
    n_used = nu_ref[0]
    d = w1_ref.shape[1]
    ch = d // LANES
    tm = xbuf.shape[1] // ch
    n_tok = h2_ref.shape[0] // ch

    def gather(slot):
        return lambda r, sid: _row_copy(h2_ref, jnp.where(sid >= n_tok, sid - n_tok, sid),
                                        xbuf.at[slot], r, gsem.at[slot], ch)

    def scatter(slot):
        return lambda r, sid: _row_copy(ybuf.at[slot], r, y2_ref, sid, ssem.at[slot], ch)

    def start_partial(tile, copy):
        valid = tv_ref[tile]

        def body(c, carry):
            for u in range(ROW_GROUP):
                r = c * ROW_GROUP + u

                @pl.when(r < valid)
                def _():
                    copy(r, sid_ref[tile * tm + r]).start()
            return carry
        lax.fori_loop(0, (valid + (ROW_GROUP - 1)) // ROW_GROUP, body, 0)

    def start_all(tile, copy):
        @pl.when(tv_ref[tile] == tm)
        def _():
            for r in range(tm):
                copy(r, sid_ref[tile * tm + r]).start()

        @pl.when(tv_ref[tile] < tm)
        def _():
            start_partial(tile, copy)

    def wait_all(tile, copy, tile_copy):
        @pl.when(tv_ref[tile] == tm)
        def _():
            tile_copy.wait()

        @pl.when(tv_ref[tile] < tm)
        def _():
            def body(r, carry):
                copy(0, 0).wait()
                return carry
            lax.fori_loop(0, tv_ref[tile], body, 0)

    def wait_gather(tile, slot):
        wait_all(tile, gather(slot),
                 pltpu.make_async_copy(h2_ref.at[pl.ds(0, tm * ch), :], xbuf.at[slot], gsem.at[slot]))

    def wait_scatter(tile, slot):
        wait_all(tile, scatter(slot),
                 pltpu.make_async_copy(ybuf.at[slot], y2_ref.at[pl.ds(0, tm * ch), :], ssem.at[slot]))

    @pl.when(t == 0)
    def _():
        start_partial(0, gather(0))

    @pl.when(t < n_used)
    def _():
        slot = t % 2

        @pl.when(t + 1 < n_used)
        def _():
            start_all(t + 1, gather(1 - slot))

        wait_gather(t, slot)

        @pl.when(t >= 2)
        def _():
            wait_scatter(t - 2, slot)

        row = lax.broadcasted_iota(jnp.int32, (tm, d), 0)
        x = jnp.where(row < tv_ref[t], _load_token_major(xbuf.at[slot], tm, d), 0.0).astype(jnp.bfloat16)
        a = _dot(x, w1_ref[0].astype(jnp.bfloat16))
        b = _dot(x, w3_ref[0].astype(jnp.bfloat16))
        act = (a * (1.0 / (1.0 + jnp.exp(-a)))) * b
        _store_token_major(ybuf.at[slot], _dot(act.astype(jnp.bfloat16), w2_ref[0].astype(jnp.bfloat16)))
        start_all(t, scatter(slot))

    @pl.when(t == pl.num_programs(0) - 1)
    def _():
        @pl.when(n_used >= 2)
        def _():
            wait_scatter(n_used - 2, n_used % 2)
        wait_scatter(n_used - 1, (n_used - 1) % 2)


def _moe_ffn(tile_expert, tile_valid, n_used, row_sid, h2, w1, w3, w2, *, tm):
    _, d, ff = w1.shape
    ch = d // LANES
    n = h2.shape[0] // ch
    max_tiles = tile_expert.shape[0]

    def expert(t, te, tv, nu, sid):
        return te[jnp.minimum(t, nu[0] - 1)]

    return pl.pallas_call(
        _ffn_kernel,
        out_shape=jax.ShapeDtypeStruct((2 * n * ch, LANES), jnp.float32),
        grid_spec=pltpu.PrefetchScalarGridSpec(
            num_scalar_prefetch=4,
            grid=(max_tiles,),
            in_specs=[
                pl.BlockSpec(memory_space=pl.ANY),
                pl.BlockSpec((1, d, ff), lambda *a: (expert(*a), 0, 0)),
                pl.BlockSpec((1, d, ff), lambda *a: (expert(*a), 0, 0)),
                pl.BlockSpec((1, ff, d), lambda *a: (expert(*a), 0, 0)),
            ],
            out_specs=pl.BlockSpec(memory_space=pl.ANY),
            scratch_shapes=[
                pltpu.VMEM((2, tm * ch, LANES), jnp.float32),
                pltpu.VMEM((2, tm * ch, LANES), jnp.float32),
                pltpu.SemaphoreType.DMA((2,)),
                pltpu.SemaphoreType.DMA((2,)),
            ],
        ),
        compiler_params=_params("arbitrary"),
        name="moe_ffn",
    )(tile_expert, tile_valid, n_used, row_sid, h2, w1, w3, w2)


def _combine_kernel(x1_ref, route_ref, ya_ref, yb_ref, o_ref):
    route = route_ref[...]
    w1 = route[:, ROUTE_W1:ROUTE_W1 + 1]
    w2 = route[:, ROUTE_W2:ROUTE_W2 + 1]
    tm, d = x1_ref.shape
    o_ref[...] = x1_ref[...] + (w1 * _load_token_major(ya_ref, tm, d) + w2 * _load_token_major(yb_ref, tm, d))


def _moe_combine(x1, route, y2, *, tm):
    n, d = x1.shape
    return pl.pallas_call(
        _combine_kernel,
        out_shape=jax.ShapeDtypeStruct((n, d), jnp.float32),
        grid=(n // tm,),
        in_specs=[pl.BlockSpec((tm, d), lambda i: (i, 0)),
                  pl.BlockSpec((tm, LANES), lambda i: (i, 0)),
                  pl.BlockSpec((tm * d // LANES, LANES), lambda i: (i, 0)),
                  pl.BlockSpec((tm * d // LANES, LANES), lambda i: (i + n // tm, 0))],
        out_specs=pl.BlockSpec((tm, d), lambda i: (i, 0)),
        compiler_params=_params("arbitrary"),
        name="moe_combine",
    )(x1, route, y2, y2)


def _moe(x1, h2, route, w1, w3, w2, *, tm, gather_tm):
    n, d = x1.shape
    max_tiles = (2 * n) // tm + N_EXPERTS
    pos2d, cnt = _moe_plan(route, tm=tm, blk=gather_tm)
    pos = pos2d[:, :2].T.reshape(2 * n)
    counts = cnt[0, :N_EXPERTS].astype(jnp.int32)
    tiles = (counts + (tm - 1)) // tm
    ends = jnp.cumsum(tiles)
    t_idx = jnp.arange(max_tiles, dtype=jnp.int32)
    tile_expert = jnp.sum((ends[None, :] <= t_idx[:, None]).astype(jnp.int32), axis=1)
    tile_expert = jnp.minimum(tile_expert, N_EXPERTS - 1)
    first_tile = (ends - tiles)[tile_expert]
    tile_valid = jnp.clip(counts[tile_expert] - (t_idx - first_tile) * tm, 0, tm).astype(jnp.int32)
    n_used = ends[-1:].astype(jnp.int32)

    row_sid = _moe_invert(pos, rows=max_tiles * tm)
    y2 = _moe_ffn(tile_expert, tile_valid, n_used, row_sid, h2, w1, w3, w2, tm=tm)
    return _moe_combine(x1, route, y2, tm=gather_tm)


def _pack_router(w_group, w_router):
    d = w_group.shape[0]
    experts = jnp.transpose(w_router, (1, 0, 2)).reshape(d, N_EXPERTS)
    wr = jnp.concatenate(
        [experts, w_group, jnp.zeros((d, LANES - N_EXPERTS - N_GROUPS), w_group.dtype)], axis=1)
    hi = wr.astype(jnp.bfloat16)
    lo = (wr - hi.astype(jnp.float32)).astype(jnp.bfloat16)
    return hi, lo


def _rotation_tables(seq):
    half = RET_DK // 2
    pos = jnp.arange(seq, dtype=jnp.float32)
    inv = 1.0 / (ROT_BASE ** jnp.linspace(0.0, 1.0, half, dtype=jnp.float32))
    ang = pos[:, None] * inv[None, :]
    c, s = jnp.cos(ang), jnp.sin(ang)
    return jnp.concatenate([c, c], axis=-1), jnp.concatenate([-s, s], axis=-1)


def _tiles(n, seq):
    def fit(total, want):
        t = min(total, want)
        while total % t:
            t //= 2
        return t
    return dict(
        proj_tm=fit(n, 1024),
        attn_tq=fit(seq, 256), attn_tk=fit(seq, 256),
        ret_rc=fit(seq, 256),
        out_tm=fit(n, 256),
        moe_tm=fit(n, 256), moe_gather_tm=fit(n, 256),
    )


def kernel(x, norm1_g, w_in, q_norm_g, k_norm_g, idx_k_ln_w, idx_k_ln_b, ret_norm_g,
           w_out, norm2_g, w_group, w_router, w1, w3, w2):
    b, seq, d = x.shape
    n = b * seq
    depth = w_in.shape[0]
    t = _tiles(n, seq)
    cos2, sin2 = _rotation_tables(seq)
    log_gamma = jnp.log1p(-jnp.exp2(-5.0 - jnp.arange(RET_HEADS, dtype=jnp.float32)))

    x2d = x.reshape(n, d)
    for l in range(depth):
        proj = _in_proj(x2d, norm1_g[l][None, :], w_in[l].T, tm=t["proj_tm"])
        p3 = proj.reshape(b, seq, proj.shape[1])
        attn = _dsa_attention(p3, q_norm_g[l][None, :], k_norm_g[l][None, :],
                              idx_k_ln_w[l][None, :], idx_k_ln_b[l][None, :],
                              tq=t["attn_tq"], tk=t["attn_tk"])
        ret = _retention(p3, log_gamma, cos2, sin2, ret_norm_g[l].reshape(RET_HEADS, 1, RET_DV),
                         rc=t["ret_rc"])
        r_hi, r_lo = _pack_router(w_group[l], w_router[l])
        g2 = norm2_g[l][None, :]
        x1, h2, route = _out_proj(attn.reshape(n, ATTN_WIDTH), ret.reshape(n, RET_WIDTH), x2d,
                                  w_out[l].astype(jnp.bfloat16), g2, r_hi, r_lo, tm=t["out_tm"])
        x2d = _moe(x1, h2, route, w1[l], w3[l], w2[l], tm=t["moe_tm"], gather_tm=t["moe_gather_tm"])
    return x2d.reshape(b, seq, d)
```

```python
import functools
import math

import jax
import jax.numpy as jnp
from jax import lax
from jax.experimental import pallas as pl
from jax.experimental.pallas import tpu as pltpu

CHUNK = 64
ATTN_HEADS = 8
HEAD_DIM = 128
KV_HEADS = 2
HEADS_PER_KV = ATTN_HEADS // KV_HEADS
IDX_HEADS = 16
IDX_DIM = 64
TOPK_MAX = 256
RET_HEADS = 8
RET_DK = 128
RET_DV = 128
ROT_BASE = 10000.0
N_GROUPS = 4
EXPERTS_PER_GROUP = 8
N_EXPERTS = N_GROUPS * EXPERTS_PER_GROUP
EPS = 1e-6

ATTN_WIDTH = ATTN_HEADS * HEAD_DIM
KV_WIDTH = KV_HEADS * HEAD_DIM
IDX_WIDTH = IDX_HEADS * IDX_DIM
RET_WIDTH = RET_HEADS * RET_DK

LANES = 128
VMEM_LIMIT = 56 * 1024 * 1024

AQ_OFF = 0
AK_OFF = AQ_OFF + ATTN_WIDTH
AV_OFF = AK_OFF + KV_WIDTH
IQ_OFF = AV_OFF + KV_WIDTH
IK_OFF = IQ_OFF + IDX_WIDTH
IW_OFF = IK_OFF + IDX_DIM
W_RET = IW_OFF + IDX_HEADS
IN_WIDTH = W_RET + 4 * RET_WIDTH
assert IW_OFF // LANES == IK_OFF // LANES
PROJ_TN = 1024
RQ_OFF = -(-W_RET // PROJ_TN) * PROJ_TN
RK_OFF = RQ_OFF + RET_WIDTH
RV_OFF = RK_OFF + RET_WIDTH
RG_OFF = RV_OFF + RET_WIDTH
PROJ_WIDTH = RG_OFF + RET_WIDTH

ROUTE_E1, ROUTE_E2, ROUTE_W1, ROUTE_W2 = 0, 1, 2, 3

INT_MIN = -(2 ** 31)
NEG_BIG = -1e30

_NT = (((1,), (1,)), ((), ()))


def _dot(a, b):
    return jnp.dot(a, b, preferred_element_type=jnp.float32)


def _dot_nt(a, b):
    return lax.dot_general(a, b, _NT, preferred_element_type=jnp.float32)


def _params(*sem):
    return pltpu.CompilerParams(dimension_semantics=sem, vmem_limit_bytes=VMEM_LIMIT)


def _in_proj_kernel(x_ref, g_ref, wt_ref, o_ref, h_scr, *, row_chunk):
    @pl.when(pl.program_id(1) == 0)
    def _():
        def body(c, carry):
            rows = pl.ds(pl.multiple_of(c * row_chunk, row_chunk), row_chunk)
            x = x_ref[rows, :]
            ms = jnp.mean(x * x, axis=-1, keepdims=True)
            h_scr[rows, :] = ((x * lax.rsqrt(ms + EPS)) * g_ref[...]).astype(jnp.bfloat16)
            return carry
        lax.fori_loop(0, x_ref.shape[0] // row_chunk, body, 0)

    o_ref[...] = _dot_nt(h_scr[...], wt_ref[...].astype(jnp.bfloat16)).astype(o_ref.dtype)


def _in_proj(x2d, g, w_in_t, *, tm):
    n, d = x2d.shape
    tn = PROJ_TN
    assert w_in_t.shape == (IN_WIDTH, d) and W_RET % 8 == 0
    attn_tiles = RQ_OFF // tn

    def window(i, j):
        start8 = jnp.where(j < attn_tiles, j * (tn // 8), W_RET // 8 + (j - attn_tiles) * (tn // 8))
        return 8 * start8, 0

    return pl.pallas_call(
        functools.partial(_in_proj_kernel, row_chunk=min(tm, 128)),
        out_shape=jax.ShapeDtypeStruct((n, PROJ_WIDTH), jnp.bfloat16),
        grid=(n // tm, PROJ_WIDTH // tn),
        in_specs=[
            pl.BlockSpec((tm, d), lambda i, j: (i, 0)),
            pl.BlockSpec((1, d), lambda i, j: (0, 0)),
            pl.BlockSpec((pl.Element(tn), pl.Element(d)), window),
        ],
        out_specs=pl.BlockSpec((tm, tn), lambda i, j: (i, j)),
        scratch_shapes=[pltpu.VMEM((tm, d), jnp.bfloat16)],
        compiler_params=_params("arbitrary", "arbitrary"),
        name="in_proj",
    )(x2d, g, w_in_t)


def _ordered_float(v):
    bits = v ^ ((v >> 31) & jnp.int32(0x7FFFFFFF))
    return pltpu.bitcast(bits, jnp.float32)


def _attn_kernel(aq_ref, iqa_ref, iqb_ref, iw_ref, ak_ref, av_ref, ik_ref, qg_ref, kg_ref, lnw_ref,
                 lnb_ref, o_ref,
                 kn_scr, ikn_scr, vt_scr, key_scr, iqh_scr, wt_scr, qn_scr, acc_scr, m_scr, l_scr,
                 sa_scr, sb_scr, *, tk, topk, idx_w_scale):
    i = pl.program_id(1)
    seq = ak_ref.shape[1]
    tq = aq_ref.shape[1]
    chunk_shift = CHUNK.bit_length() - 1

    @pl.when(i == 0)
    def _():
        def body(c, carry):
            rows = pl.ds(pl.multiple_of(c * tk, tk), tk)
            for g in range(KV_HEADS):
                cols = slice(g * HEAD_DIM, (g + 1) * HEAD_DIM)
                k = ak_ref[0, rows, cols].astype(jnp.float32)
                ms = jnp.mean(k * k, axis=-1, keepdims=True)
                kn_scr[rows, cols] = ((k * lax.rsqrt(ms + EPS)) * kg_ref[...]).astype(jnp.bfloat16)
                v = av_ref[0, rows, cols].astype(jnp.float32)
                vt_scr[g, c] = v.T.astype(jnp.bfloat16)
            ki = ik_ref[0, rows, :IDX_DIM].astype(jnp.float32)
            mu = jnp.mean(ki, axis=-1, keepdims=True)
            var = jnp.mean(jnp.square(ki - mu), axis=-1, keepdims=True)
            y = (ki - mu) * lax.rsqrt(var + EPS)
            ikn_scr[rows, :] = (y * lnw_ref[...] + lnb_ref[...]).astype(jnp.bfloat16)
            return carry
        lax.fori_loop(0, seq // tk, body, 0)

    t0 = i * tq
    n_kt = (t0 + tq) // tk
    scale = (HEAD_DIM ** -0.5) * math.log2(math.e)
    for h in range(ATTN_HEADS):
        g, r = divmod(h, HEADS_PER_KV)
        q = aq_ref[0, :, h * HEAD_DIM:(h + 1) * HEAD_DIM].astype(jnp.float32)
        ms = jnp.mean(q * q, axis=-1, keepdims=True)
        qn_scr[g, r * tq:(r + 1) * tq, :] = (
            (q * lax.rsqrt(ms + EPS)) * qg_ref[...] * scale).astype(jnp.bfloat16)
    half = IDX_HEADS // 2
    for h in range(IDX_HEADS):
        src = iqa_ref if h < half else iqb_ref
        iqh_scr[h] = src[0, :, (h % half) * IDX_DIM:(h % half + 1) * IDX_DIM]
    wt_scr[...] = iw_ref[0].astype(jnp.float32).T * idx_w_scale
    w_row = IW_OFF % LANES

    q_chunk = (t0 + lax.broadcasted_iota(jnp.int32, (tk, tq), 1)) >> chunk_shift

    def score_body(kt, carry):
        rows = pl.ds(pl.multiple_of(kt * tk, tk), tk)
        ik_t = ikn_scr[rows, :]
        acc = jnp.zeros((tk, tq), jnp.float32)
        for h in range(IDX_HEADS):
            d = _dot_nt(ik_t, iqh_scr[h])
            acc = acc + jnp.maximum(d, 0.0) * wt_scr[w_row + h:w_row + h + 1, :]
        k_chunk = (kt * tk + lax.broadcasted_iota(jnp.int32, (tk, tq), 0)) >> chunk_shift
        key_scr[rows, :] = jnp.where(k_chunk <= q_chunk, acc, -jnp.inf)
        return carry
    lax.fori_loop(0, n_kt, score_body, 0)

    def bit_body(it, lo):
        cand = lo + lax.shift_left(jnp.int32(1), 31 - it)
        cand_f = _ordered_float(cand)

        def count_body(kt, part):
            rows = pl.ds(pl.multiple_of(kt * tk, tk), tk)
            hit = jnp.where(key_scr[rows, :] >= cand_f, 1.0, 0.0).reshape(tk // 8, 8, tq)
            while hit.shape[0] > 1:
                half = hit.shape[0] // 2
                hit = hit[:half] + hit[half:]
            return part + hit[0]
        part = lax.fori_loop(0, n_kt, count_body, jnp.zeros((8, tq), jnp.float32))
        cnt = jnp.sum(part, axis=0, keepdims=True)
        return jnp.where(cnt >= float(topk), cand, lo)
    lo = lax.fori_loop(0, 32, bit_body, jnp.full((1, tq), INT_MIN, jnp.int32))
    thr = jnp.where(lo == INT_MIN, jnp.finfo(jnp.float32).min, _ordered_float(lo))

    m_scr[...] = jnp.full(m_scr.shape, NEG_BIG, jnp.float32)
    l_scr[...] = jnp.zeros(l_scr.shape, jnp.float32)
    acc_scr[...] = jnp.zeros(acc_scr.shape, jnp.float32)

    def logits(kt, s_ref):
        rows = pl.ds(pl.multiple_of(kt * tk, tk), tk)
        for g in range(KV_HEADS):
            s_ref[g] = _dot_nt(kn_scr[rows, g * HEAD_DIM:(g + 1) * HEAD_DIM], qn_scr[g])

    def consume(kt, s_ref):
        rows = pl.ds(pl.multiple_of(kt * tk, tk), tk)
        bias = jnp.where(key_scr[rows, :] >= thr, 0.0, NEG_BIG)
        bias = jnp.concatenate([bias] * HEADS_PER_KV, axis=1)
        for g in range(KV_HEADS):
            s = s_ref[g] + bias
            m_old = m_scr[g]
            m_new = jnp.maximum(m_old, jnp.max(s, axis=0, keepdims=True))
            p = jnp.exp2(s - m_new)
            alpha = jnp.exp2(m_old - m_new)
            l_scr[g] = alpha * l_scr[g] + jnp.sum(p, axis=0, keepdims=True)
            acc_scr[g] = acc_scr[g] * alpha + _dot(vt_scr[g, kt], p.astype(jnp.bfloat16))
            m_scr[g] = m_new

    logits(0, sa_scr)

    def pair_body(j, carry):
        consume(2 * j, sa_scr)
        logits(2 * j + 1, sb_scr)
        consume(2 * j + 1, sb_scr)
        logits(jnp.minimum(2 * j + 2, n_kt - 1), sa_scr)
        return carry
    lax.fori_loop(0, n_kt // 2, pair_body, 0)

    @pl.when(n_kt % 2 == 1)
    def _():
        consume(n_kt - 1, sa_scr)

    for h in range(ATTN_HEADS):
        g, r = divmod(h, HEADS_PER_KV)
        o = acc_scr[g, :, r * tq:(r + 1) * tq] / l_scr[g, :, r * tq:(r + 1) * tq]
        o_ref[0, :, h * HEAD_DIM:(h + 1) * HEAD_DIM] = o.T.astype(o_ref.dtype)


def _dsa_attention(p3, q_g, k_g, ln_w, ln_b, *, tq, tk):
    b, seq, _ = p3.shape
    topk = min(TOPK_MAX, seq // 4)
    idx_w_scale = (IDX_HEADS ** -0.5) * (IDX_DIM ** -0.5)
    assert seq % tq == 0 and tq % tk == 0 and tk % CHUNK == 0

    def col(off, width):
        assert off % width == 0 or width == LANES
        return off // width

    half_iq = IDX_WIDTH // 2
    return pl.pallas_call(
        functools.partial(_attn_kernel, tk=tk, topk=topk, idx_w_scale=idx_w_scale),
        out_shape=jax.ShapeDtypeStruct((b, seq, ATTN_WIDTH), jnp.bfloat16),
        grid=(b, seq // tq),
        in_specs=[
            pl.BlockSpec((1, tq, ATTN_WIDTH), lambda bi, i: (bi, i, col(AQ_OFF, ATTN_WIDTH))),
            pl.BlockSpec((1, tq, half_iq), lambda bi, i: (bi, i, col(IQ_OFF, half_iq))),
            pl.BlockSpec((1, tq, half_iq), lambda bi, i: (bi, i, col(IQ_OFF, half_iq) + 1)),
            pl.BlockSpec((1, tq, LANES), lambda bi, i: (bi, i, col(IW_OFF, LANES))),
            pl.BlockSpec((1, seq, KV_WIDTH), lambda bi, i: (bi, 0, col(AK_OFF, KV_WIDTH))),
            pl.BlockSpec((1, seq, KV_WIDTH), lambda bi, i: (bi, 0, col(AV_OFF, KV_WIDTH))),
            pl.BlockSpec((1, seq, LANES), lambda bi, i: (bi, 0, col(IK_OFF, LANES))),
            pl.BlockSpec((1, HEAD_DIM), lambda bi, i: (0, 0)),
            pl.BlockSpec((1, HEAD_DIM), lambda bi, i: (0, 0)),
            pl.BlockSpec((1, IDX_DIM), lambda bi, i: (0, 0)),
            pl.BlockSpec((1, IDX_DIM), lambda bi, i: (0, 0)),
        ],
        out_specs=pl.BlockSpec((1, tq, ATTN_WIDTH), lambda bi, i: (bi, i, 0)),
        scratch_shapes=[
            pltpu.VMEM((seq, KV_WIDTH), jnp.bfloat16),
            pltpu.VMEM((seq, IDX_DIM), jnp.bfloat16),
            pltpu.VMEM((KV_HEADS, seq // tk, HEAD_DIM, tk), jnp.bfloat16),
            pltpu.VMEM((seq, tq), jnp.float32),
            pltpu.VMEM((IDX_HEADS, tq, IDX_DIM), jnp.bfloat16),
            pltpu.VMEM((LANES, tq), jnp.float32),
            pltpu.VMEM((KV_HEADS, HEADS_PER_KV * tq, HEAD_DIM), jnp.bfloat16),
            pltpu.VMEM((KV_HEADS, HEAD_DIM, HEADS_PER_KV * tq), jnp.float32),
            pltpu.VMEM((KV_HEADS, 1, HEADS_PER_KV * tq), jnp.float32),
            pltpu.VMEM((KV_HEADS, 1, HEADS_PER_KV * tq), jnp.float32),
            pltpu.VMEM((KV_HEADS, tk, HEADS_PER_KV * tq), jnp.float32),
            pltpu.VMEM((KV_HEADS, tk, HEADS_PER_KV * tq), jnp.float32),
        ],
        compiler_params=_params("arbitrary", "arbitrary"),
        name="dsa_attn",
    )(p3, p3, p3, p3, p3, p3, p3, q_g, k_g, ln_w, ln_b)


def _ret_kernel(lg_ref, rq_ref, rk_ref, rv_ref, rg_ref, cos_ref, sin_ref, g_ref, o_ref, *, rc):
    h = pl.program_id(1)
    lg = lg_ref[h]
    seq = rq_ref.shape[1]
    n = lax.broadcasted_iota(jnp.int32, (rc, RET_DV), 0).astype(jnp.float32)
    cross_decay = jnp.exp(lg * (n + 1.0))
    state_decay = jnp.exp(lg * (rc - 1.0 - n))
    chunk_decay = jnp.exp(lg * jnp.full((RET_DK, RET_DV), float(rc), jnp.float32))
    rel = (lax.broadcasted_iota(jnp.int32, (rc, rc), 0)
           - lax.broadcasted_iota(jnp.int32, (rc, rc), 1)).astype(jnp.float32)
    intra = jnp.where(rel >= 0, jnp.exp(lg * jnp.maximum(rel, 0.0)), 0.0)

    def rot(x, rows):
        return x * cos_ref[rows, :] + pltpu.roll(x, RET_DK // 2, 1) * sin_ref[rows, :]

    state = jnp.zeros((RET_DK, RET_DV), jnp.float32)
    for c in range(seq // rc):
        rows = slice(c * rc, (c + 1) * rc)
        q = rot(rq_ref[0, rows, :].astype(jnp.float32), rows)
        k = rot(rk_ref[0, rows, :].astype(jnp.float32), rows) * (RET_DK ** -0.5)
        v = rv_ref[0, rows, :]
        qb = q.astype(jnp.bfloat16)
        inner = _dot_nt(qb, k.astype(jnp.bfloat16)) * intra
        o = _dot(inner.astype(jnp.bfloat16), v) + _dot(qb, state.astype(jnp.bfloat16)) * cross_decay
        kd_t = (k * state_decay).T.astype(jnp.bfloat16)
        state = state * chunk_decay + _dot(kd_t, v)
        ms = jnp.mean(o * o, axis=-1, keepdims=True)
        y = (o * lax.rsqrt(ms + EPS)) * g_ref[0]
        gate = rg_ref[0, rows, :].astype(jnp.float32)
        o_ref[0, rows, :] = ((gate * (1.0 / (1.0 + jnp.exp(-gate)))) * y).astype(o_ref.dtype)


def _retention(p3, log_gamma, cos2, sin2, ret_g, *, rc):
    b, seq, _ = p3.shape
    assert seq % rc == 0

    def head_spec(off):
        return pl.BlockSpec((1, seq, RET_DK), lambda bi, h: (bi, 0, off // RET_DK + h))

    return pl.pallas_call(
        functools.partial(_ret_kernel, rc=rc),
        out_shape=jax.ShapeDtypeStruct((b, seq, RET_WIDTH), jnp.bfloat16),
        grid=(b, RET_HEADS),
        in_specs=[
            pl.BlockSpec(memory_space=pltpu.SMEM),
            head_spec(RQ_OFF), head_spec(RK_OFF), head_spec(RV_OFF), head_spec(RG_OFF),
            pl.BlockSpec((seq, RET_DK), lambda bi, h: (0, 0)),
            pl.BlockSpec((seq, RET_DK), lambda bi, h: (0, 0)),
            pl.BlockSpec((1, 1, RET_DV), lambda bi, h: (h, 0, 0)),
        ],
        out_specs=pl.BlockSpec((1, seq, RET_DV), lambda bi, h: (bi, 0, h)),
        compiler_params=_params("arbitrary", "arbitrary"),
        name="retention",
    )(log_gamma, p3, p3, p3, p3, cos2, sin2, ret_g)


def _routing(logits):
    lane = lax.broadcasted_iota(jnp.int32, logits.shape, 1).astype(jnp.float32)
    big = float(LANES)
    neg = -jnp.inf

    def first_argmax(v, vmax):
        return jnp.min(jnp.where(v == vmax, lane, big), axis=-1, keepdims=True)

    g_mask = (lane >= N_EXPERTS) & (lane < N_EXPERTS + N_GROUPS)
    gl = jnp.where(g_mask, logits, neg)
    g_max = jnp.max(gl, axis=-1, keepdims=True)
    g_sel = first_argmax(gl, g_max) - N_EXPERTS
    g_gate = 1.0 / jnp.sum(jnp.where(g_mask, jnp.exp(gl - g_max), 0.0), axis=-1, keepdims=True)

    e_lo = g_sel * EXPERTS_PER_GROUP
    el = jnp.where((lane >= e_lo) & (lane < e_lo + EXPERTS_PER_GROUP), logits, neg)
    v1 = jnp.max(el, axis=-1, keepdims=True)
    i1 = first_argmax(el, v1)
    el2 = jnp.where(lane == i1, neg, el)
    v2 = jnp.max(el2, axis=-1, keepdims=True)
    i2 = first_argmax(el2, v2)
    e2 = jnp.exp(v2 - v1)
    denom = 1.0 + e2
    w1 = (1.0 / denom) * g_gate
    w2 = (e2 / denom) * g_gate
    route = jnp.where(lane == ROUTE_E1, i1, 0.0) + jnp.where(lane == ROUTE_E2, i2, 0.0)
    return route + jnp.where(lane == ROUTE_W1, w1, 0.0) + jnp.where(lane == ROUTE_W2, w2, 0.0)


def _norm2(x1, g):
    ms = jnp.mean(x1 * x1, axis=-1, keepdims=True)
    return (x1 * lax.rsqrt(ms + EPS)) * g


def _out_proj_kernel(a_ref, r_ref, x_ref, wa_ref, wr_ref, g_ref, rhi_ref, rlo_ref,
                     x1_ref, h2_ref, route_ref):
    mixed = _dot(a_ref[...], wa_ref[...]) + _dot(r_ref[...], wr_ref[...])
    x1 = x_ref[...] + mixed
    x1_ref[...] = x1
    h2 = _norm2(x1, g_ref[...])
    h2_ref[...] = h2
    hi = h2.astype(jnp.bfloat16)
    lo = (h2 - hi.astype(jnp.float32)).astype(jnp.bfloat16)
    logits = _dot(hi, rhi_ref[...]) + (_dot(hi, rlo_ref[...]) + _dot(lo, rhi_ref[...]))
    route_ref[...] = _routing(logits)


def _out_proj(attn2d, ret2d, x2d, w_out_bf, g2, r_hi, r_lo, *, tm):
    n, d = x2d.shape
    return pl.pallas_call(
        _out_proj_kernel,
        out_shape=(
            jax.ShapeDtypeStruct((n, d), jnp.float32),
            jax.ShapeDtypeStruct((n, d), jnp.float32),
            jax.ShapeDtypeStruct((n, LANES), jnp.float32),
        ),
        grid=(n // tm,),
        in_specs=[
            pl.BlockSpec((tm, ATTN_WIDTH), lambda i: (i, 0)),
            pl.BlockSpec((tm, RET_WIDTH), lambda i: (i, 0)),
            pl.BlockSpec((tm, d), lambda i: (i, 0)),
            pl.BlockSpec((ATTN_WIDTH, d), lambda i: (0, 0)),
            pl.BlockSpec((RET_WIDTH, d), lambda i: (ATTN_WIDTH // RET_WIDTH, 0)),
            pl.BlockSpec((1, d), lambda i: (0, 0)),
            pl.BlockSpec((d, LANES), lambda i: (0, 0)),
            pl.BlockSpec((d, LANES), lambda i: (0, 0)),
        ],
        out_specs=(
            pl.BlockSpec((tm, d), lambda i: (i, 0)),
            pl.BlockSpec((tm, d), lambda i: (i, 0)),
            pl.BlockSpec((tm, LANES), lambda i: (i, 0)),
        ),
        compiler_params=_params("arbitrary"),
        name="out_proj",
    )(attn2d, ret2d, x2d, w_out_bf, w_out_bf, g2, r_hi, r_lo)


def _plan_kernel(route_ref, pos_ref, cnt_ref, rank_scr, *, tm, blk):
    n = route_ref.shape[0]
    lane = lax.broadcasted_iota(jnp.int32, (blk, LANES), 1).astype(jnp.float32)
    before = (lax.broadcasted_iota(jnp.int32, (blk, blk), 1)
              < lax.broadcasted_iota(jnp.int32, (blk, blk), 0)).astype(jnp.bfloat16)

    def one_hot(rows):
        r = route_ref[rows, :]
        e1 = r[:, ROUTE_E1:ROUTE_E1 + 1]
        e2 = r[:, ROUTE_E2:ROUTE_E2 + 1]
        return lane == e1, lane == e2

    def rank_body(b, run):
        rows = pl.ds(pl.multiple_of(b * blk, blk), blk)
        m1, m2 = one_hot(rows)
        sel = jnp.where(m1 | m2, 1.0, 0.0)
        rank_scr[rows, :] = _dot(before, sel.astype(jnp.bfloat16)) + run
        return run + jnp.sum(sel, axis=0, keepdims=True)
    cnt = lax.fori_loop(0, n // blk, rank_body, jnp.zeros((1, LANES), jnp.float32))
    cnt_ref[...] = jnp.broadcast_to(cnt, cnt_ref.shape)

    tiles = jnp.floor((cnt + (tm - 1.0)) * (1.0 / tm))
    below = (lax.broadcasted_iota(jnp.int32, (LANES, LANES), 0)
             < lax.broadcasted_iota(jnp.int32, (LANES, LANES), 1)).astype(jnp.bfloat16)
    start = _dot(jnp.broadcast_to(tiles, (8, LANES)).astype(jnp.bfloat16), below)[0:1, :] * float(tm)

    def pos_body(b, carry):
        rows = pl.ds(pl.multiple_of(b * blk, blk), blk)
        m1, m2 = one_hot(rows)
        dest = rank_scr[rows, :] + start
        p1 = jnp.sum(jnp.where(m1, dest, 0.0), axis=-1, keepdims=True)
        p2 = jnp.sum(jnp.where(m2, dest, 0.0), axis=-1, keepdims=True)
        pos_ref[rows, :] = (jnp.where(lane == 0.0, p1, 0.0) + jnp.where(lane == 1.0, p2, 0.0)).astype(jnp.int32)
        return carry
    lax.fori_loop(0, n // blk, pos_body, 0)


def _moe_plan(route, *, tm, blk):
    n = route.shape[0]
    return pl.pallas_call(
        functools.partial(_plan_kernel, tm=tm, blk=blk),
        out_shape=(jax.ShapeDtypeStruct((n, LANES), jnp.int32),
                   jax.ShapeDtypeStruct((8, LANES), jnp.float32)),
        scratch_shapes=[pltpu.VMEM((n, LANES), jnp.float32)],
        compiler_params=pltpu.CompilerParams(vmem_limit_bytes=VMEM_LIMIT),
        name="moe_plan",
    )(route)


def _row_copy(src, src_row, dst, dst_row, sem):
    return pltpu.make_async_copy(src.at[pl.ds(src_row, 1), :], dst.at[pl.ds(dst_row, 1), :], sem)


def _invert_kernel(pos_ref, sid_ref):
    def body(j, carry):
        sid_ref[pos_ref[j]] = j
        return carry
    lax.fori_loop(0, pos_ref.shape[0], body, 0, unroll=8)


def _moe_invert(pos, *, rows):
    return pl.pallas_call(
        _invert_kernel,
        out_shape=jax.ShapeDtypeStruct((rows,), jnp.int32),
        in_specs=[pl.BlockSpec(memory_space=pltpu.SMEM)],
        out_specs=pl.BlockSpec(memory_space=pltpu.SMEM),
        name="moe_invert",
    )(pos)


ROW_GROUP = 8
NO_NEXT, NOT_FIRST = -1, -2


def _ffn_kernel(te_ref, tv_ref, nu_ref, seg_ref, nexte_ref, sid_ref,
                h2_ref, w1_ref, w3_ref, w2_ref, y2_ref,
                xbuf, ybuf, w1buf, w3buf, w2buf, gsem, ssem, wsem):
    t = pl.program_id(0)
    n_used = nu_ref[0]
    tm = xbuf.shape[1]
    n_tok = h2_ref.shape[0]

    def weight_copies(e, wslot):
        return [pltpu.make_async_copy(src.at[e], dst.at[wslot], wsem.at[wslot])
                for src, dst in ((w1_ref, w1buf), (w3_ref, w3buf), (w2_ref, w2buf))]

    def gather(slot):
        return lambda r, sid: _row_copy(h2_ref, jnp.where(sid >= n_tok, sid - n_tok, sid),
                                        xbuf.at[slot], r, gsem.at[slot])

    def scatter(slot):
        return lambda r, sid: _row_copy(ybuf.at[slot], r, y2_ref, sid, ssem.at[slot])

    def start_partial(tile, copy):
        valid = tv_ref[tile]

        def body(c, carry):
            for u in range(ROW_GROUP):
                r = c * ROW_GROUP + u

                @pl.when(r < valid)
                def _():
                    copy(r, sid_ref[tile * tm + r]).start()
            return carry
        lax.fori_loop(0, (valid + (ROW_GROUP - 1)) // ROW_GROUP, body, 0)

    def start_all(tile, copy):
        @pl.when(tv_ref[tile] == tm)
        def _():
            for r in range(tm):
                copy(r, sid_ref[tile * tm + r]).start()

        @pl.when(tv_ref[tile] < tm)
        def _():
            start_partial(tile, copy)

    def wait_all(tile, copy, tile_copy):
        @pl.when(tv_ref[tile] == tm)
        def _():
            tile_copy.wait()

        @pl.when(tv_ref[tile] < tm)
        def _():
            def body(r, carry):
                copy(0, 0).wait()
                return carry
            lax.fori_loop(0, tv_ref[tile], body, 0)

    def wait_gather(tile, slot):
        wait_all(tile, gather(slot),
                 pltpu.make_async_copy(h2_ref.at[pl.ds(0, tm), :], xbuf.at[slot], gsem.at[slot]))

    def wait_scatter(tile, slot):
        wait_all(tile, scatter(slot),
                 pltpu.make_async_copy(ybuf.at[slot], y2_ref.at[pl.ds(0, tm), :], ssem.at[slot]))

    @pl.when(t == 0)
    def _():
        for cp in weight_copies(te_ref[0], 0):
            cp.start()
        start_partial(0, gather(0))

    @pl.when(t < n_used)
    def _():
        slot = t % 2
        wslot = seg_ref[t] % 2

        @pl.when(t + 1 < n_used)
        def _():
            start_all(t + 1, gather(1 - slot))

        @pl.when(nexte_ref[t] != NOT_FIRST)
        def _():
            for cp in weight_copies(te_ref[t], wslot):
                cp.wait()

            @pl.when(nexte_ref[t] >= 0)
            def _():
                for cp in weight_copies(nexte_ref[t], 1 - wslot):
                    cp.start()

        wait_gather(t, slot)

        @pl.when(t >= 2)
        def _():
            wait_scatter(t - 2, slot)

        row = lax.broadcasted_iota(jnp.int32, xbuf.shape[1:], 0)
        x = jnp.where(row < tv_ref[t], xbuf[slot], 0.0).astype(jnp.bfloat16)
        a = _dot(x, w1buf[wslot].astype(jnp.bfloat16))
        b = _dot(x, w3buf[wslot].astype(jnp.bfloat16))
        act = (a * (1.0 / (1.0 + jnp.exp(-a)))) * b
        ybuf[slot] = _dot(act.astype(jnp.bfloat16), w2buf[wslot].astype(jnp.bfloat16))
        start_all(t, scatter(slot))

    @pl.when(t == pl.num_programs(0) - 1)
    def _():
        @pl.when(n_used >= 2)
        def _():
            wait_scatter(n_used - 2, n_used % 2)
        wait_scatter(n_used - 1, (n_used - 1) % 2)


def _moe_ffn(tile_expert, tile_valid, n_used, tile_seg, tile_next, row_sid, h2, w1, w3, w2, *, tm):
    n, d = h2.shape
    _, _, ff = w1.shape
    max_tiles = tile_expert.shape[0]
    any_spec = pl.BlockSpec(memory_space=pl.ANY)
    return pl.pallas_call(
        _ffn_kernel,
        out_shape=jax.ShapeDtypeStruct((2 * n, d), jnp.float32),
        grid_spec=pltpu.PrefetchScalarGridSpec(
            num_scalar_prefetch=6,
            grid=(max_tiles,),
            in_specs=[any_spec, any_spec, any_spec, any_spec],
            out_specs=any_spec,
            scratch_shapes=[
                pltpu.VMEM((2, tm, d), jnp.float32),
                pltpu.VMEM((2, tm, d), jnp.float32),
                pltpu.VMEM((2, d, ff), jnp.float32),
                pltpu.VMEM((2, d, ff), jnp.float32),
                pltpu.VMEM((2, ff, d), jnp.float32),
                pltpu.SemaphoreType.DMA((2,)),
                pltpu.SemaphoreType.DMA((2,)),
                pltpu.SemaphoreType.DMA((2,)),
            ],
        ),
        compiler_params=_params("arbitrary"),
        name="moe_ffn",
    )(tile_expert, tile_valid, n_used, tile_seg, tile_next, row_sid, h2, w1, w3, w2)


def _combine_kernel(x1_ref, route_ref, ya_ref, yb_ref, o_ref):
    route = route_ref[...]
    w1 = route[:, ROUTE_W1:ROUTE_W1 + 1]
    w2 = route[:, ROUTE_W2:ROUTE_W2 + 1]
    o_ref[...] = x1_ref[...] + (w1 * ya_ref[...] + w2 * yb_ref[...])


def _moe_combine(x1, route, y2, *, tm):
    n, d = x1.shape
    return pl.pallas_call(
        _combine_kernel,
        out_shape=jax.ShapeDtypeStruct((n, d), jnp.float32),
        grid=(n // tm,),
        in_specs=[pl.BlockSpec((tm, d), lambda i: (i, 0)),
                  pl.BlockSpec((tm, LANES), lambda i: (i, 0)),
                  pl.BlockSpec((tm, d), lambda i: (i, 0)),
                  pl.BlockSpec((tm, d), lambda i: (i + n // tm, 0))],
        out_specs=pl.BlockSpec((tm, d), lambda i: (i, 0)),
        compiler_params=_params("arbitrary"),
        name="moe_combine",
    )(x1, route, y2, y2)


def _moe(x1, h2, route, w1, w3, w2, *, tm, gather_tm):
    n, d = x1.shape
    max_tiles = (2 * n) // tm + N_EXPERTS
    pos2d, cnt = _moe_plan(route, tm=tm, blk=gather_tm)
    pos = pos2d[:, :2].T.reshape(2 * n)
    counts = cnt[0, :N_EXPERTS].astype(jnp.int32)
    tiles = (counts + (tm - 1)) // tm
    ends = jnp.cumsum(tiles)
    t_idx = jnp.arange(max_tiles, dtype=jnp.int32)
    tile_expert = jnp.sum((ends[None, :] <= t_idx[:, None]).astype(jnp.int32), axis=1)
    tile_expert = jnp.minimum(tile_expert, N_EXPERTS - 1)
    first_tile = (ends - tiles)[tile_expert]
    tile_valid = jnp.clip(counts[tile_expert] - (t_idx - first_tile) * tm, 0, tm).astype(jnp.int32)
    n_used = ends[-1:].astype(jnp.int32)
    used = t_idx < n_used[0]
    is_first = used & (t_idx == first_tile)
    tile_seg = (jnp.cumsum(is_first.astype(jnp.int32)) - 1).astype(jnp.int32)
    next_start = first_tile + tiles[tile_expert]
    next_e = jnp.where(next_start < n_used[0], tile_expert[jnp.minimum(next_start, max_tiles - 1)], NO_NEXT)
    tile_next = jnp.where(is_first, next_e, NOT_FIRST).astype(jnp.int32)

    row_sid = _moe_invert(pos, rows=max_tiles * tm)
    y2 = _moe_ffn(tile_expert, tile_valid, n_used, tile_seg, tile_next, row_sid, h2, w1, w3, w2, tm=tm)
    return _moe_combine(x1, route, y2, tm=gather_tm)


def _pack_router(w_group, w_router):
    d = w_group.shape[0]
    experts = jnp.transpose(w_router, (1, 0, 2)).reshape(d, N_EXPERTS)
    wr = jnp.concatenate(
        [experts, w_group, jnp.zeros((d, LANES - N_EXPERTS - N_GROUPS), w_group.dtype)], axis=1)
    hi = wr.astype(jnp.bfloat16)
    lo = (wr - hi.astype(jnp.float32)).astype(jnp.bfloat16)
    return hi, lo


def _rotation_tables(seq):
    half = RET_DK // 2
    pos = jnp.arange(seq, dtype=jnp.float32)
    inv = 1.0 / (ROT_BASE ** jnp.linspace(0.0, 1.0, half, dtype=jnp.float32))
    ang = pos[:, None] * inv[None, :]
    c, s = jnp.cos(ang), jnp.sin(ang)
    return jnp.concatenate([c, c], axis=-1), jnp.concatenate([-s, s], axis=-1)


def _tiles(n, seq):
    def fit(total, want):
        t = min(total, want)
        while total % t:
            t //= 2
        return t
    return dict(
        proj_tm=fit(n, 1024),
        attn_tq=fit(seq, 256), attn_tk=fit(seq, 256),
        ret_rc=fit(seq, 256),
        out_tm=fit(n, 256),
        moe_tm=fit(n, 256), moe_gather_tm=fit(n, 256),
    )


def kernel(x, norm1_g, w_in, q_norm_g, k_norm_g, idx_k_ln_w, idx_k_ln_b, ret_norm_g,
           w_out, norm2_g, w_group, w_router, w1, w3, w2):
    b, seq, d = x.shape
    n = b * seq
    depth = w_in.shape[0]
    t = _tiles(n, seq)
    cos2, sin2 = _rotation_tables(seq)
    log_gamma = jnp.log1p(-jnp.exp2(-5.0 - jnp.arange(RET_HEADS, dtype=jnp.float32)))

    x2d = x.reshape(n, d)
    for l in range(depth):
        proj = _in_proj(x2d, norm1_g[l][None, :], w_in[l].T, tm=t["proj_tm"])
        p3 = proj.reshape(b, seq, proj.shape[1])
        attn = _dsa_attention(p3, q_norm_g[l][None, :], k_norm_g[l][None, :],
                              idx_k_ln_w[l][None, :], idx_k_ln_b[l][None, :],
                              tq=t["attn_tq"], tk=t["attn_tk"])
        ret = _retention(p3, log_gamma, cos2, sin2, ret_norm_g[l].reshape(RET_HEADS, 1, RET_DV),
                         rc=t["ret_rc"])
        r_hi, r_lo = _pack_router(w_group[l], w_router[l])
        g2 = norm2_g[l][None, :]
        x1, h2, route = _out_proj(attn.reshape(n, ATTN_WIDTH), ret.reshape(n, RET_WIDTH), x2d,
                                  w_out[l].astype(jnp.bfloat16), g2, r_hi, r_lo, tm=t["out_tm"])
        x2d = _moe(x1, h2, route, w1[l], w3[l], w2[l], tm=t["moe_tm"], gather_tm=t["moe_gather_tm"])
    return x2d.reshape(b, seq, d)
```

```python
import functools
import math

import jax
import jax.numpy as jnp
from jax import lax
from jax.experimental import pallas as pl
from jax.experimental.pallas import tpu as pltpu

CHUNK = 64
ATTN_HEADS = 8
HEAD_DIM = 128
KV_HEADS = 2
HEADS_PER_KV = ATTN_HEADS // KV_HEADS
IDX_HEADS = 16
IDX_DIM = 64
TOPK_MAX = 256
RET_HEADS = 8
RET_DK = 128
RET_DV = 128
ROT_BASE = 10000.0
N_GROUPS = 4
EXPERTS_PER_GROUP = 8
N_EXPERTS = N_GROUPS * EXPERTS_PER_GROUP
EPS = 1e-6

ATTN_WIDTH = ATTN_HEADS * HEAD_DIM
KV_WIDTH = KV_HEADS * HEAD_DIM
IDX_WIDTH = IDX_HEADS * IDX_DIM
RET_WIDTH = RET_HEADS * RET_DK

LANES = 128
VMEM_LIMIT = 56 * 1024 * 1024

AQ_OFF = 0
AK_OFF = AQ_OFF + ATTN_WIDTH
AV_OFF = AK_OFF + KV_WIDTH
IQ_OFF = AV_OFF + KV_WIDTH
IK_OFF = IQ_OFF + IDX_WIDTH
IW_OFF = IK_OFF + IDX_DIM
W_RET = IW_OFF + IDX_HEADS
IN_WIDTH = W_RET + 4 * RET_WIDTH
assert IW_OFF // LANES == IK_OFF // LANES
PROJ_TN = 1024
RQ_OFF = -(-W_RET // PROJ_TN) * PROJ_TN
RK_OFF = RQ_OFF + RET_WIDTH
RV_OFF = RK_OFF + RET_WIDTH
RG_OFF = RV_OFF + RET_WIDTH
PROJ_WIDTH = RG_OFF + RET_WIDTH

ROUTE_E1, ROUTE_E2, ROUTE_W1, ROUTE_W2 = 0, 1, 2, 3

SUM_ROWS = 16

INT_MIN = -(2 ** 31)
NEG_BIG = -1e30

_NT = (((1,), (1,)), ((), ()))


def _dot(a, b):
    return jnp.dot(a, b, preferred_element_type=jnp.float32)


def _dot_nt(a, b):
    return lax.dot_general(a, b, _NT, preferred_element_type=jnp.float32)


def _params(*sem):
    return pltpu.CompilerParams(dimension_semantics=sem, vmem_limit_bytes=VMEM_LIMIT)


def _in_proj_kernel(x_ref, g_ref, wt_ref, o_ref, h_scr, *, row_chunk):
    @pl.when(pl.program_id(1) == 0)
    def _():
        def body(c, carry):
            rows = pl.ds(pl.multiple_of(c * row_chunk, row_chunk), row_chunk)
            x = x_ref[rows, :]
            ms = jnp.mean(x * x, axis=-1, keepdims=True)
            h_scr[rows, :] = ((x * lax.rsqrt(ms + EPS)) * g_ref[...]).astype(jnp.bfloat16)
            return carry
        lax.fori_loop(0, x_ref.shape[0] // row_chunk, body, 0)

    o_ref[...] = _dot_nt(h_scr[...], wt_ref[...].astype(jnp.bfloat16)).astype(o_ref.dtype)


def _in_proj(x2d, g, w_in_t, *, tm):
    n, d = x2d.shape
    tn = PROJ_TN
    assert w_in_t.shape == (IN_WIDTH, d) and W_RET % 8 == 0
    attn_tiles = RQ_OFF // tn

    def window(i, j):
        start8 = jnp.where(j < attn_tiles, j * (tn // 8), W_RET // 8 + (j - attn_tiles) * (tn // 8))
        return 8 * start8, 0

    return pl.pallas_call(
        functools.partial(_in_proj_kernel, row_chunk=min(tm, 128)),
        out_shape=jax.ShapeDtypeStruct((n, PROJ_WIDTH), jnp.bfloat16),
        grid=(n // tm, PROJ_WIDTH // tn),
        in_specs=[
            pl.BlockSpec((tm, d), lambda i, j: (i, 0)),
            pl.BlockSpec((1, d), lambda i, j: (0, 0)),
            pl.BlockSpec((pl.Element(tn), pl.Element(d)), window),
        ],
        out_specs=pl.BlockSpec((tm, tn), lambda i, j: (i, j)),
        scratch_shapes=[pltpu.VMEM((tm, d), jnp.bfloat16)],
        compiler_params=_params("arbitrary", "arbitrary"),
        name="in_proj",
    )(x2d, g, w_in_t)


def _ordered_float(v):
    bits = v ^ ((v >> 31) & jnp.int32(0x7FFFFFFF))
    return pltpu.bitcast(bits, jnp.float32)


def _attn_kernel(aq_ref, iqa_ref, iqb_ref, iw_ref, ak_ref, av_ref, ik_ref, qg_ref, kg_ref, lnw_ref,
                 lnb_ref, o_ref,
                 kn_scr, ikn_scr, vt_scr, key_scr, iqh_scr, wt_scr, qn_scr, acc_scr, s_scr,
                 *, tk, topk, idx_w_scale):
    i = pl.program_id(1)
    seq = ak_ref.shape[1]
    tq = aq_ref.shape[1]
    chunk_shift = CHUNK.bit_length() - 1

    @pl.when(i == 0)
    def _():
        def body(c, carry):
            rows = pl.ds(pl.multiple_of(c * tk, tk), tk)
            for g in range(KV_HEADS):
                cols = slice(g * HEAD_DIM, (g + 1) * HEAD_DIM)
                k = ak_ref[0, rows, cols].astype(jnp.float32)
                ms = jnp.mean(k * k, axis=-1, keepdims=True)
                kn_scr[rows, cols] = ((k * lax.rsqrt(ms + EPS)) * kg_ref[...]).astype(jnp.bfloat16)
                v = av_ref[0, rows, cols].astype(jnp.float32)
                vt_scr[g, c, :HEAD_DIM, :] = v.T.astype(jnp.bfloat16)
                vt_scr[g, c, HEAD_DIM:, :] = jnp.ones((SUM_ROWS, tk), jnp.bfloat16)
            ki = ik_ref[0, rows, :IDX_DIM].astype(jnp.float32)
            mu = jnp.mean(ki, axis=-1, keepdims=True)
            var = jnp.mean(jnp.square(ki - mu), axis=-1, keepdims=True)
            y = (ki - mu) * lax.rsqrt(var + EPS)
            ikn_scr[rows, :] = (y * lnw_ref[...] + lnb_ref[...]).astype(jnp.bfloat16)
            return carry
        lax.fori_loop(0, seq // tk, body, 0)

    t0 = i * tq
    n_kt = (t0 + tq) // tk
    scale = (HEAD_DIM ** -0.5) * math.log2(math.e)
    for h in range(ATTN_HEADS):
        g, r = divmod(h, HEADS_PER_KV)
        q = aq_ref[0, :, h * HEAD_DIM:(h + 1) * HEAD_DIM].astype(jnp.float32)
        ms = jnp.mean(q * q, axis=-1, keepdims=True)
        qn_scr[g, r * tq:(r + 1) * tq, :] = (
            (q * lax.rsqrt(ms + EPS)) * qg_ref[...] * scale).astype(jnp.bfloat16)
    half = IDX_HEADS // 2
    for h in range(IDX_HEADS):
        src = iqa_ref if h < half else iqb_ref
        iqh_scr[h] = src[0, :, (h % half) * IDX_DIM:(h % half + 1) * IDX_DIM]
    wt_scr[...] = iw_ref[0].astype(jnp.float32).T * idx_w_scale
    w_row = IW_OFF % LANES

    q_chunk = (t0 + lax.broadcasted_iota(jnp.int32, (tk, tq), 1)) >> chunk_shift

    def score_body(kt, carry):
        rows = pl.ds(pl.multiple_of(kt * tk, tk), tk)
        ik_t = ikn_scr[rows, :]
        acc = jnp.zeros((tk, tq), jnp.float32)
        for h in range(IDX_HEADS):
            d = _dot_nt(ik_t, iqh_scr[h])
            acc = acc + jnp.maximum(d, 0.0) * wt_scr[w_row + h:w_row + h + 1, :]
        k_chunk = (kt * tk + lax.broadcasted_iota(jnp.int32, (tk, tq), 0)) >> chunk_shift
        key_scr[rows, :] = jnp.where(k_chunk <= q_chunk, acc, -jnp.inf)
        return carry
    lax.fori_loop(0, n_kt, score_body, 0)

    def bit_body(it, lo):
        cand = lo + lax.shift_left(jnp.int32(1), 31 - it)
        cand_f = _ordered_float(cand)

        def count_body(kt, part):
            rows = pl.ds(pl.multiple_of(kt * tk, tk), tk)
            hit = jnp.where(key_scr[rows, :] >= cand_f, 1.0, 0.0).reshape(tk // 8, 8, tq)
            while hit.shape[0] > 1:
                half = hit.shape[0] // 2
                hit = hit[:half] + hit[half:]
            return part + hit[0]
        part = lax.fori_loop(0, n_kt, count_body, jnp.zeros((8, tq), jnp.float32))
        cnt = jnp.sum(part, axis=0, keepdims=True)
        return jnp.where(cnt >= float(topk), cand, lo)
    lo = lax.fori_loop(0, 32, bit_body, jnp.full((1, tq), INT_MIN, jnp.int32))
    thr = jnp.where(lo == INT_MIN, jnp.finfo(jnp.float32).min, _ordered_float(lo))

    def logit_body(kt, m):
        rows = pl.ds(pl.multiple_of(kt * tk, tk), tk)
        bias = jnp.where(key_scr[rows, :] >= thr, 0.0, NEG_BIG)
        bias = jnp.concatenate([bias] * HEADS_PER_KV, axis=1)
        new_m = []
        for g in range(KV_HEADS):
            s = _dot_nt(kn_scr[rows, g * HEAD_DIM:(g + 1) * HEAD_DIM], qn_scr[g]) + bias
            s_scr[g, rows, :] = s
            new_m.append(jnp.maximum(m[g], jnp.max(s, axis=0, keepdims=True)))
        return tuple(new_m)
    m0 = jnp.full((1, HEADS_PER_KV * tq), NEG_BIG, jnp.float32)
    m = lax.fori_loop(0, n_kt, logit_body, (m0,) * KV_HEADS)

    acc_scr[...] = jnp.zeros(acc_scr.shape, jnp.float32)

    def pv_body(kt, carry):
        rows = pl.ds(pl.multiple_of(kt * tk, tk), tk)
        for g in range(KV_HEADS):
            p = jnp.exp2(s_scr[g, rows, :] - m[g]).astype(jnp.bfloat16)
            acc_scr[g] += _dot(vt_scr[g, kt], p)
        return carry
    lax.fori_loop(0, n_kt, pv_body, 0)

    for h in range(ATTN_HEADS):
        g, r = divmod(h, HEADS_PER_KV)
        cols = slice(r * tq, (r + 1) * tq)
        o = acc_scr[g, :HEAD_DIM, cols] / acc_scr[g, HEAD_DIM:HEAD_DIM + 1, cols]
        o_ref[0, :, h * HEAD_DIM:(h + 1) * HEAD_DIM] = o.T.astype(o_ref.dtype)


def _dsa_attention(p3, q_g, k_g, ln_w, ln_b, *, tq, tk):
    b, seq, _ = p3.shape
    topk = min(TOPK_MAX, seq // 4)
    idx_w_scale = (IDX_HEADS ** -0.5) * (IDX_DIM ** -0.5)
    assert seq % tq == 0 and tq % tk == 0 and tk % CHUNK == 0

    def col(off, width):
        assert off % width == 0 or width == LANES
        return off // width

    half_iq = IDX_WIDTH // 2
    return pl.pallas_call(
        functools.partial(_attn_kernel, tk=tk, topk=topk, idx_w_scale=idx_w_scale),
        out_shape=jax.ShapeDtypeStruct((b, seq, ATTN_WIDTH), jnp.bfloat16),
        grid=(b, seq // tq),
        in_specs=[
            pl.BlockSpec((1, tq, ATTN_WIDTH), lambda bi, i: (bi, i, col(AQ_OFF, ATTN_WIDTH))),
            pl.BlockSpec((1, tq, half_iq), lambda bi, i: (bi, i, col(IQ_OFF, half_iq))),
            pl.BlockSpec((1, tq, half_iq), lambda bi, i: (bi, i, col(IQ_OFF, half_iq) + 1)),
            pl.BlockSpec((1, tq, LANES), lambda bi, i: (bi, i, col(IW_OFF, LANES))),
            pl.BlockSpec((1, seq, KV_WIDTH), lambda bi, i: (bi, 0, col(AK_OFF, KV_WIDTH))),
            pl.BlockSpec((1, seq, KV_WIDTH), lambda bi, i: (bi, 0, col(AV_OFF, KV_WIDTH))),
            pl.BlockSpec((1, seq, LANES), lambda bi, i: (bi, 0, col(IK_OFF, LANES))),
            pl.BlockSpec((1, HEAD_DIM), lambda bi, i: (0, 0)),
            pl.BlockSpec((1, HEAD_DIM), lambda bi, i: (0, 0)),
            pl.BlockSpec((1, IDX_DIM), lambda bi, i: (0, 0)),
            pl.BlockSpec((1, IDX_DIM), lambda bi, i: (0, 0)),
        ],
        out_specs=pl.BlockSpec((1, tq, ATTN_WIDTH), lambda bi, i: (bi, i, 0)),
        scratch_shapes=[
            pltpu.VMEM((seq, KV_WIDTH), jnp.bfloat16),
            pltpu.VMEM((seq, IDX_DIM), jnp.bfloat16),
            pltpu.VMEM((KV_HEADS, seq // tk, HEAD_DIM + SUM_ROWS, tk), jnp.bfloat16),
            pltpu.VMEM((seq, tq), jnp.float32),
            pltpu.VMEM((IDX_HEADS, tq, IDX_DIM), jnp.bfloat16),
            pltpu.VMEM((LANES, tq), jnp.float32),
            pltpu.VMEM((KV_HEADS, HEADS_PER_KV * tq, HEAD_DIM), jnp.bfloat16),
            pltpu.VMEM((KV_HEADS, HEAD_DIM + SUM_ROWS, HEADS_PER_KV * tq), jnp.float32),
            pltpu.VMEM((KV_HEADS, seq, HEADS_PER_KV * tq), jnp.float32),
        ],
        compiler_params=_params("arbitrary", "arbitrary"),
        name="dsa_attn",
    )(p3, p3, p3, p3, p3, p3, p3, q_g, k_g, ln_w, ln_b)


def _ret_kernel(lg_ref, rq_ref, rk_ref, rv_ref, rg_ref, cos_ref, sin_ref, g_ref, o_ref, *, rc):
    h = pl.program_id(1)
    lg = lg_ref[h]
    seq = rq_ref.shape[1]
    n = lax.broadcasted_iota(jnp.int32, (rc, RET_DV), 0).astype(jnp.float32)
    cross_decay = jnp.exp(lg * (n + 1.0))
    state_decay = jnp.exp(lg * (rc - 1.0 - n))
    chunk_decay = jnp.exp(lg * jnp.full((RET_DK, RET_DV), float(rc), jnp.float32))
    rel = (lax.broadcasted_iota(jnp.int32, (rc, rc), 0)
           - lax.broadcasted_iota(jnp.int32, (rc, rc), 1)).astype(jnp.float32)
    intra = jnp.where(rel >= 0, jnp.exp(lg * jnp.maximum(rel, 0.0)), 0.0)

    def rot(x, rows):
        return x * cos_ref[rows, :] + pltpu.roll(x, RET_DK // 2, 1) * sin_ref[rows, :]

    state = jnp.zeros((RET_DK, RET_DV), jnp.float32)
    for c in range(seq // rc):
        rows = slice(c * rc, (c + 1) * rc)
        q = rot(rq_ref[0, rows, :].astype(jnp.float32), rows)
        k = rot(rk_ref[0, rows, :].astype(jnp.float32), rows) * (RET_DK ** -0.5)
        v = rv_ref[0, rows, :]
        qb = q.astype(jnp.bfloat16)
        inner = _dot_nt(qb, k.astype(jnp.bfloat16)) * intra
        o = _dot(inner.astype(jnp.bfloat16), v) + _dot(qb, state.astype(jnp.bfloat16)) * cross_decay
        kd_t = (k * state_decay).T.astype(jnp.bfloat16)
        state = state * chunk_decay + _dot(kd_t, v)
        ms = jnp.mean(o * o, axis=-1, keepdims=True)
        y = (o * lax.rsqrt(ms + EPS)) * g_ref[0]
        gate = rg_ref[0, rows, :].astype(jnp.float32)
        o_ref[0, rows, :] = ((gate * (1.0 / (1.0 + jnp.exp(-gate)))) * y).astype(o_ref.dtype)


def _retention(p3, log_gamma, cos2, sin2, ret_g, *, rc):
    b, seq, _ = p3.shape
    assert seq % rc == 0

    def head_spec(off):
        return pl.BlockSpec((1, seq, RET_DK), lambda bi, h: (bi, 0, off // RET_DK + h))

    return pl.pallas_call(
        functools.partial(_ret_kernel, rc=rc),
        out_shape=jax.ShapeDtypeStruct((b, seq, RET_WIDTH), jnp.bfloat16),
        grid=(b, RET_HEADS),
        in_specs=[
            pl.BlockSpec(memory_space=pltpu.SMEM),
            head_spec(RQ_OFF), head_spec(RK_OFF), head_spec(RV_OFF), head_spec(RG_OFF),
            pl.BlockSpec((seq, RET_DK), lambda bi, h: (0, 0)),
            pl.BlockSpec((seq, RET_DK), lambda bi, h: (0, 0)),
            pl.BlockSpec((1, 1, RET_DV), lambda bi, h: (h, 0, 0)),
        ],
        out_specs=pl.BlockSpec((1, seq, RET_DV), lambda bi, h: (bi, 0, h)),
        compiler_params=_params("arbitrary", "arbitrary"),
        name="retention",
    )(log_gamma, p3, p3, p3, p3, cos2, sin2, ret_g)


def _routing(logits):
    lane = lax.broadcasted_iota(jnp.int32, logits.shape, 1).astype(jnp.float32)
    big = float(LANES)
    neg = -jnp.inf

    def first_argmax(v, vmax):
        return jnp.min(jnp.where(v == vmax, lane, big), axis=-1, keepdims=True)

    g_mask = (lane >= N_EXPERTS) & (lane < N_EXPERTS + N_GROUPS)
    gl = jnp.where(g_mask, logits, neg)
    g_max = jnp.max(gl, axis=-1, keepdims=True)
    g_sel = first_argmax(gl, g_max) - N_EXPERTS
    g_gate = 1.0 / jnp.sum(jnp.where(g_mask, jnp.exp(gl - g_max), 0.0), axis=-1, keepdims=True)

    e_lo = g_sel * EXPERTS_PER_GROUP
    el = jnp.where((lane >= e_lo) & (lane < e_lo + EXPERTS_PER_GROUP), logits, neg)
    v1 = jnp.max(el, axis=-1, keepdims=True)
    i1 = first_argmax(el, v1)
    el2 = jnp.where(lane == i1, neg, el)
    v2 = jnp.max(el2, axis=-1, keepdims=True)
    i2 = first_argmax(el2, v2)
    e2 = jnp.exp(v2 - v1)
    denom = 1.0 + e2
    w1 = (1.0 / denom) * g_gate
    w2 = (e2 / denom) * g_gate
    route = jnp.where(lane == ROUTE_E1, i1, 0.0) + jnp.where(lane == ROUTE_E2, i2, 0.0)
    return route + jnp.where(lane == ROUTE_W1, w1, 0.0) + jnp.where(lane == ROUTE_W2, w2, 0.0)


def _norm2(x1, g):
    ms = jnp.mean(x1 * x1, axis=-1, keepdims=True)
    return (x1 * lax.rsqrt(ms + EPS)) * g


def _out_proj_kernel(a_ref, r_ref, x_ref, wa_ref, wr_ref, g_ref, rhi_ref, rlo_ref,
                     x1_ref, h2_ref, route_ref):
    mixed = _dot(a_ref[...], wa_ref[...]) + _dot(r_ref[...], wr_ref[...])
    x1 = x_ref[...] + mixed
    x1_ref[...] = x1
    h2 = _norm2(x1, g_ref[...])
    h2_ref[...] = h2
    hi = h2.astype(jnp.bfloat16)
    lo = (h2 - hi.astype(jnp.float32)).astype(jnp.bfloat16)
    logits = _dot(hi, rhi_ref[...]) + (_dot(hi, rlo_ref[...]) + _dot(lo, rhi_ref[...]))
    route_ref[...] = _routing(logits)


def _out_proj(attn2d, ret2d, x2d, w_out_bf, g2, r_hi, r_lo, *, tm):
    n, d = x2d.shape
    return pl.pallas_call(
        _out_proj_kernel,
        out_shape=(
            jax.ShapeDtypeStruct((n, d), jnp.float32),
            jax.ShapeDtypeStruct((n, d), jnp.float32),
            jax.ShapeDtypeStruct((n, LANES), jnp.float32),
        ),
        grid=(n // tm,),
        in_specs=[
            pl.BlockSpec((tm, ATTN_WIDTH), lambda i: (i, 0)),
            pl.BlockSpec((tm, RET_WIDTH), lambda i: (i, 0)),
            pl.BlockSpec((tm, d), lambda i: (i, 0)),
            pl.BlockSpec((ATTN_WIDTH, d), lambda i: (0, 0)),
            pl.BlockSpec((RET_WIDTH, d), lambda i: (ATTN_WIDTH // RET_WIDTH, 0)),
            pl.BlockSpec((1, d), lambda i: (0, 0)),
            pl.BlockSpec((d, LANES), lambda i: (0, 0)),
            pl.BlockSpec((d, LANES), lambda i: (0, 0)),
        ],
        out_specs=(
            pl.BlockSpec((tm, d), lambda i: (i, 0)),
            pl.BlockSpec((tm, d), lambda i: (i, 0)),
            pl.BlockSpec((tm, LANES), lambda i: (i, 0)),
        ),
        compiler_params=_params("arbitrary"),
        name="out_proj",
    )(attn2d, ret2d, x2d, w_out_bf, w_out_bf, g2, r_hi, r_lo)


def _plan_kernel(route_ref, pos_ref, cnt_ref, rank_scr, *, tm, blk):
    n = route_ref.shape[0]
    lane = lax.broadcasted_iota(jnp.int32, (blk, LANES), 1).astype(jnp.float32)
    before = (lax.broadcasted_iota(jnp.int32, (blk, blk), 1)
              < lax.broadcasted_iota(jnp.int32, (blk, blk), 0)).astype(jnp.bfloat16)

    def one_hot(rows):
        r = route_ref[rows, :]
        e1 = r[:, ROUTE_E1:ROUTE_E1 + 1]
        e2 = r[:, ROUTE_E2:ROUTE_E2 + 1]
        return lane == e1, lane == e2

    def rank_body(b, run):
        rows = pl.ds(pl.multiple_of(b * blk, blk), blk)
        m1, m2 = one_hot(rows)
        sel = jnp.where(m1 | m2, 1.0, 0.0)
        rank_scr[rows, :] = _dot(before, sel.astype(jnp.bfloat16)) + run
        return run + jnp.sum(sel, axis=0, keepdims=True)
    cnt = lax.fori_loop(0, n // blk, rank_body, jnp.zeros((1, LANES), jnp.float32))
    cnt_ref[...] = jnp.broadcast_to(cnt, cnt_ref.shape)

    tiles = jnp.floor((cnt + (tm - 1.0)) * (1.0 / tm))
    below = (lax.broadcasted_iota(jnp.int32, (LANES, LANES), 0)
             < lax.broadcasted_iota(jnp.int32, (LANES, LANES), 1)).astype(jnp.bfloat16)
    start = _dot(jnp.broadcast_to(tiles, (8, LANES)).astype(jnp.bfloat16), below)[0:1, :] * float(tm)

    def pos_body(b, carry):
        rows = pl.ds(pl.multiple_of(b * blk, blk), blk)
        m1, m2 = one_hot(rows)
        dest = rank_scr[rows, :] + start
        p1 = jnp.sum(jnp.where(m1, dest, 0.0), axis=-1, keepdims=True)
        p2 = jnp.sum(jnp.where(m2, dest, 0.0), axis=-1, keepdims=True)
        pos_ref[rows, :] = (jnp.where(lane == 0.0, p1, 0.0) + jnp.where(lane == 1.0, p2, 0.0)).astype(jnp.int32)
        return carry
    lax.fori_loop(0, n // blk, pos_body, 0)


def _moe_plan(route, *, tm, blk):
    n = route.shape[0]
    return pl.pallas_call(
        functools.partial(_plan_kernel, tm=tm, blk=blk),
        out_shape=(jax.ShapeDtypeStruct((n, LANES), jnp.int32),
                   jax.ShapeDtypeStruct((8, LANES), jnp.float32)),
        scratch_shapes=[pltpu.VMEM((n, LANES), jnp.float32)],
        compiler_params=pltpu.CompilerParams(vmem_limit_bytes=VMEM_LIMIT),
        name="moe_plan",
    )(route)


def _row_copy(src, src_row, dst, dst_row, sem):
    return pltpu.make_async_copy(src.at[pl.ds(src_row, 1), :], dst.at[pl.ds(dst_row, 1), :], sem)


def _invert_kernel(pos_ref, sid_ref):
    def body(j, carry):
        sid_ref[pos_ref[j]] = j
        return carry
    lax.fori_loop(0, pos_ref.shape[0], body, 0, unroll=8)


def _moe_invert(pos, *, rows):
    return pl.pallas_call(
        _invert_kernel,
        out_shape=jax.ShapeDtypeStruct((rows,), jnp.int32),
        in_specs=[pl.BlockSpec(memory_space=pltpu.SMEM)],
        out_specs=pl.BlockSpec(memory_space=pltpu.SMEM),
        name="moe_invert",
    )(pos)


ROW_GROUP = 8
NO_NEXT, NOT_FIRST = -1, -2


def _ffn_kernel(te_ref, tv_ref, nu_ref, seg_ref, nexte_ref, sid_ref,
                h2_ref, w1_ref, w3_ref, w2_ref, y2_ref,
                xbuf, ybuf, w1buf, w3buf, w2buf, gsem, ssem, wsem):
    t = pl.program_id(0)
    n_used = nu_ref[0]
    tm = xbuf.shape[1]
    n_tok = h2_ref.shape[0]

    def weight_copies(e, wslot):
        return [pltpu.make_async_copy(src.at[e], dst.at[wslot], wsem.at[wslot])
                for src, dst in ((w1_ref, w1buf), (w3_ref, w3buf), (w2_ref, w2buf))]

    def gather(slot):
        return lambda r, sid: _row_copy(h2_ref, jnp.where(sid >= n_tok, sid - n_tok, sid),
                                        xbuf.at[slot], r, gsem.at[slot])

    def scatter(slot):
        return lambda r, sid: _row_copy(ybuf.at[slot], r, y2_ref, sid, ssem.at[slot])

    def start_partial(tile, copy):
        valid = tv_ref[tile]

        def body(c, carry):
            for u in range(ROW_GROUP):
                r = c * ROW_GROUP + u

                @pl.when(r < valid)
                def _():
                    copy(r, sid_ref[tile * tm + r]).start()
            return carry
        lax.fori_loop(0, (valid + (ROW_GROUP - 1)) // ROW_GROUP, body, 0)

    def start_all(tile, copy):
        @pl.when(tv_ref[tile] == tm)
        def _():
            for r in range(tm):
                copy(r, sid_ref[tile * tm + r]).start()

        @pl.when(tv_ref[tile] < tm)
        def _():
            start_partial(tile, copy)

    def wait_all(tile, copy, tile_copy):
        @pl.when(tv_ref[tile] == tm)
        def _():
            tile_copy.wait()

        @pl.when(tv_ref[tile] < tm)
        def _():
            def body(r, carry):
                copy(0, 0).wait()
                return carry
            lax.fori_loop(0, tv_ref[tile], body, 0)

    def wait_gather(tile, slot):
        wait_all(tile, gather(slot),
                 pltpu.make_async_copy(h2_ref.at[pl.ds(0, tm), :], xbuf.at[slot], gsem.at[slot]))

    def wait_scatter(tile, slot):
        wait_all(tile, scatter(slot),
                 pltpu.make_async_copy(ybuf.at[slot], y2_ref.at[pl.ds(0, tm), :], ssem.at[slot]))

    @pl.when(t == 0)
    def _():
        for cp in weight_copies(te_ref[0], 0):
            cp.start()
        start_partial(0, gather(0))

    @pl.when(t < n_used)
    def _():
        slot = t % 2
        wslot = seg_ref[t] % 2

        @pl.when(t + 1 < n_used)
        def _():
            start_all(t + 1, gather(1 - slot))

        @pl.when(nexte_ref[t] != NOT_FIRST)
        def _():
            for cp in weight_copies(te_ref[t], wslot):
                cp.wait()

            @pl.when(nexte_ref[t] >= 0)
            def _():
                for cp in weight_copies(nexte_ref[t], 1 - wslot):
                    cp.start()

        wait_gather(t, slot)

        @pl.when(t >= 2)
        def _():
            wait_scatter(t - 2, slot)

        row = lax.broadcasted_iota(jnp.int32, xbuf.shape[1:], 0)
        x = jnp.where(row < tv_ref[t], xbuf[slot], 0.0).astype(jnp.bfloat16)
        a = _dot(x, w1buf[wslot].astype(jnp.bfloat16))
        b = _dot(x, w3buf[wslot].astype(jnp.bfloat16))
        act = (a * (1.0 / (1.0 + jnp.exp(-a)))) * b
        ybuf[slot] = _dot(act.astype(jnp.bfloat16), w2buf[wslot].astype(jnp.bfloat16))
        start_all(t, scatter(slot))

    @pl.when(t == pl.num_programs(0) - 1)
    def _():
        @pl.when(n_used >= 2)
        def _():
            wait_scatter(n_used - 2, n_used % 2)
        wait_scatter(n_used - 1, (n_used - 1) % 2)


def _moe_ffn(tile_expert, tile_valid, n_used, tile_seg, tile_next, row_sid, h2, w1, w3, w2, *, tm):
    n, d = h2.shape
    _, _, ff = w1.shape
    max_tiles = tile_expert.shape[0]
    any_spec = pl.BlockSpec(memory_space=pl.ANY)
    return pl.pallas_call(
        _ffn_kernel,
        out_shape=jax.ShapeDtypeStruct((2 * n, d), jnp.float32),
        grid_spec=pltpu.PrefetchScalarGridSpec(
            num_scalar_prefetch=6,
            grid=(max_tiles,),
            in_specs=[any_spec, any_spec, any_spec, any_spec],
            out_specs=any_spec,
            scratch_shapes=[
                pltpu.VMEM((2, tm, d), jnp.float32),
                pltpu.VMEM((2, tm, d), jnp.float32),
                pltpu.VMEM((2, d, ff), jnp.float32),
                pltpu.VMEM((2, d, ff), jnp.float32),
                pltpu.VMEM((2, ff, d), jnp.float32),
                pltpu.SemaphoreType.DMA((2,)),
                pltpu.SemaphoreType.DMA((2,)),
                pltpu.SemaphoreType.DMA((2,)),
            ],
        ),
        compiler_params=_params("arbitrary"),
        name="moe_ffn",
    )(tile_expert, tile_valid, n_used, tile_seg, tile_next, row_sid, h2, w1, w3, w2)


def _combine_kernel(x1_ref, route_ref, ya_ref, yb_ref, o_ref):
    route = route_ref[...]
    w1 = route[:, ROUTE_W1:ROUTE_W1 + 1]
    w2 = route[:, ROUTE_W2:ROUTE_W2 + 1]
    o_ref[...] = x1_ref[...] + (w1 * ya_ref[...] + w2 * yb_ref[...])


def _moe_combine(x1, route, y2, *, tm):
    n, d = x1.shape
    return pl.pallas_call(
        _combine_kernel,
        out_shape=jax.ShapeDtypeStruct((n, d), jnp.float32),
        grid=(n // tm,),
        in_specs=[pl.BlockSpec((tm, d), lambda i: (i, 0)),
                  pl.BlockSpec((tm, LANES), lambda i: (i, 0)),
                  pl.BlockSpec((tm, d), lambda i: (i, 0)),
                  pl.BlockSpec((tm, d), lambda i: (i + n // tm, 0))],
        out_specs=pl.BlockSpec((tm, d), lambda i: (i, 0)),
        compiler_params=_params("arbitrary"),
        name="moe_combine",
    )(x1, route, y2, y2)


def _moe(x1, h2, route, w1, w3, w2, *, tm, gather_tm):
    n, d = x1.shape
    max_tiles = (2 * n) // tm + N_EXPERTS
    pos2d, cnt = _moe_plan(route, tm=tm, blk=gather_tm)
    pos = pos2d[:, :2].T.reshape(2 * n)
    counts = cnt[0, :N_EXPERTS].astype(jnp.int32)
    tiles = (counts + (tm - 1)) // tm
    ends = jnp.cumsum(tiles)
    t_idx = jnp.arange(max_tiles, dtype=jnp.int32)
    tile_expert = jnp.sum((ends[None, :] <= t_idx[:, None]).astype(jnp.int32), axis=1)
    tile_expert = jnp.minimum(tile_expert, N_EXPERTS - 1)
    first_tile = (ends - tiles)[tile_expert]
    tile_valid = jnp.clip(counts[tile_expert] - (t_idx - first_tile) * tm, 0, tm).astype(jnp.int32)
    n_used = ends[-1:].astype(jnp.int32)
    used = t_idx < n_used[0]
    is_first = used & (t_idx == first_tile)
    tile_seg = (jnp.cumsum(is_first.astype(jnp.int32)) - 1).astype(jnp.int32)
    next_start = first_tile + tiles[tile_expert]
    next_e = jnp.where(next_start < n_used[0], tile_expert[jnp.minimum(next_start, max_tiles - 1)], NO_NEXT)
    tile_next = jnp.where(is_first, next_e, NOT_FIRST).astype(jnp.int32)

    row_sid = _moe_invert(pos, rows=max_tiles * tm)
    y2 = _moe_ffn(tile_expert, tile_valid, n_used, tile_seg, tile_next, row_sid, h2, w1, w3, w2, tm=tm)
    return _moe_combine(x1, route, y2, tm=gather_tm)


def _pack_router(w_group, w_router):
    d = w_group.shape[0]
    experts = jnp.transpose(w_router, (1, 0, 2)).reshape(d, N_EXPERTS)
    wr = jnp.concatenate(
        [experts, w_group, jnp.zeros((d, LANES - N_EXPERTS - N_GROUPS), w_group.dtype)], axis=1)
    hi = wr.astype(jnp.bfloat16)
    lo = (wr - hi.astype(jnp.float32)).astype(jnp.bfloat16)
    return hi, lo


def _rotation_tables(seq):
    half = RET_DK // 2
    pos = jnp.arange(seq, dtype=jnp.float32)
    inv = 1.0 / (ROT_BASE ** jnp.linspace(0.0, 1.0, half, dtype=jnp.float32))
    ang = pos[:, None] * inv[None, :]
    c, s = jnp.cos(ang), jnp.sin(ang)
    return jnp.concatenate([c, c], axis=-1), jnp.concatenate([-s, s], axis=-1)


def _tiles(n, seq):
    def fit(total, want):
        t = min(total, want)
        while total % t:
            t //= 2
        return t
    return dict(
        proj_tm=fit(n, 1024),
        attn_tq=fit(seq, 256), attn_tk=fit(seq, 256),
        ret_rc=fit(seq, 256),
        out_tm=fit(n, 256),
        moe_tm=fit(n, 256), moe_gather_tm=fit(n, 256),
    )


def kernel(x, norm1_g, w_in, q_norm_g, k_norm_g, idx_k_ln_w, idx_k_ln_b, ret_norm_g,
           w_out, norm2_g, w_group, w_router, w1, w3, w2):
    b, seq, d = x.shape
    n = b * seq
    depth = w_in.shape[0]
    t = _tiles(n, seq)
    cos2, sin2 = _rotation_tables(seq)
    log_gamma = jnp.log1p(-jnp.exp2(-5.0 - jnp.arange(RET_HEADS, dtype=jnp.float32)))

    x2d = x.reshape(n, d)
    for l in range(depth):
        proj = _in_proj(x2d, norm1_g[l][None, :], w_in[l].T, tm=t["proj_tm"])
        p3 = proj.reshape(b, seq, proj.shape[1])
        attn = _dsa_attention(p3, q_norm_g[l][None, :], k_norm_g[l][None, :],
                              idx_k_ln_w[l][None, :], idx_k_ln_b[l][None, :],
                              tq=t["attn_tq"], tk=t["attn_tk"])
        ret = _retention(p3, log_gamma, cos2, sin2, ret_norm_g[l].reshape(RET_HEADS, 1, RET_DV),
                         rc=t["ret_rc"])
        r_hi, r_lo = _pack_router(w_group[l], w_router[l])
        g2 = norm2_g[l][None, :]
        x1, h2, route = _out_proj(attn.reshape(n, ATTN_WIDTH), ret.reshape(n, RET_WIDTH), x2d,
                                  w_out[l].astype(jnp.bfloat16), g2, r_hi, r_lo, tm=t["out_tm"])
        x2d = _moe(x1, h2, route, w1[l], w3[l], w2[l], tm=t["moe_tm"], gather_tm=t["moe_gather_tm"])
    return x2d.reshape(b, seq, d)
```

```python
import functools
import math

import jax
import jax.numpy as jnp
from jax import lax
from jax.experimental import pallas as pl
from jax.experimental.pallas import tpu as pltpu

CHUNK = 64
ATTN_HEADS = 8
HEAD_DIM = 128
KV_HEADS = 2
HEADS_PER_KV = ATTN_HEADS // KV_HEADS
IDX_HEADS = 16
IDX_DIM = 64
TOPK_MAX = 256
RET_HEADS = 8
RET_DK = 128
RET_DV = 128
ROT_BASE = 10000.0
N_GROUPS = 4
EXPERTS_PER_GROUP = 8
N_EXPERTS = N_GROUPS * EXPERTS_PER_GROUP
EPS = 1e-6

ATTN_WIDTH = ATTN_HEADS * HEAD_DIM
KV_WIDTH = KV_HEADS * HEAD_DIM
IDX_WIDTH = IDX_HEADS * IDX_DIM
RET_WIDTH = RET_HEADS * RET_DK

LANES = 128
VMEM_LIMIT = 56 * 1024 * 1024

AQ_OFF = 0
AK_OFF = AQ_OFF + ATTN_WIDTH
AV_OFF = AK_OFF + KV_WIDTH
IQ_OFF = AV_OFF + KV_WIDTH
IK_OFF = IQ_OFF + IDX_WIDTH
IW_OFF = IK_OFF + IDX_DIM
W_RET = IW_OFF + IDX_HEADS
IN_WIDTH = W_RET + 4 * RET_WIDTH
assert IW_OFF // LANES == IK_OFF // LANES
PROJ_TN = 1024
RQ_OFF = -(-W_RET // PROJ_TN) * PROJ_TN
RK_OFF = RQ_OFF + RET_WIDTH
RV_OFF = RK_OFF + RET_WIDTH
RG_OFF = RV_OFF + RET_WIDTH
PROJ_WIDTH = RG_OFF + RET_WIDTH

ROUTE_E1, ROUTE_E2, ROUTE_W1, ROUTE_W2 = 0, 1, 2, 3

SUM_ROWS = 16

INT_MIN = -(2 ** 31)
NEG_BIG = -1e30

_NT = (((1,), (1,)), ((), ()))


def _dot(a, b):
    return jnp.dot(a, b, preferred_element_type=jnp.float32)


def _dot_nt(a, b):
    return lax.dot_general(a, b, _NT, preferred_element_type=jnp.float32)


def _params(*sem):
    return pltpu.CompilerParams(dimension_semantics=sem, vmem_limit_bytes=VMEM_LIMIT)


def _in_proj_kernel(x_ref, g_ref, wt_ref, o_ref, h_scr, *, row_chunk):
    @pl.when(pl.program_id(1) == 0)
    def _():
        def body(c, carry):
            rows = pl.ds(pl.multiple_of(c * row_chunk, row_chunk), row_chunk)
            x = x_ref[rows, :]
            ms = jnp.mean(x * x, axis=-1, keepdims=True)
            h_scr[rows, :] = ((x * lax.rsqrt(ms + EPS)) * g_ref[...]).astype(jnp.bfloat16)
            return carry
        lax.fori_loop(0, x_ref.shape[0] // row_chunk, body, 0)

    o_ref[...] = _dot_nt(h_scr[...], wt_ref[...].astype(jnp.bfloat16)).astype(o_ref.dtype)


def _in_proj(x2d, g, w_in_t, *, tm):
    n, d = x2d.shape
    tn = PROJ_TN
    assert w_in_t.shape == (IN_WIDTH, d) and W_RET % 8 == 0
    attn_tiles = RQ_OFF // tn

    def window(i, j):
        start8 = jnp.where(j < attn_tiles, j * (tn // 8), W_RET // 8 + (j - attn_tiles) * (tn // 8))
        return 8 * start8, 0

    return pl.pallas_call(
        functools.partial(_in_proj_kernel, row_chunk=min(tm, 128)),
        out_shape=jax.ShapeDtypeStruct((n, PROJ_WIDTH), jnp.bfloat16),
        grid=(n // tm, PROJ_WIDTH // tn),
        in_specs=[
            pl.BlockSpec((tm, d), lambda i, j: (i, 0)),
            pl.BlockSpec((1, d), lambda i, j: (0, 0)),
            pl.BlockSpec((pl.Element(tn), pl.Element(d)), window),
        ],
        out_specs=pl.BlockSpec((tm, tn), lambda i, j: (i, j)),
        scratch_shapes=[pltpu.VMEM((tm, d), jnp.bfloat16)],
        compiler_params=_params("arbitrary", "arbitrary"),
        name="in_proj",
    )(x2d, g, w_in_t)


def _ordered_float(v):
    bits = v ^ ((v >> 31) & jnp.int32(0x7FFFFFFF))
    return pltpu.bitcast(bits, jnp.float32)


def _attn_kernel(aq_ref, iqa_ref, iqb_ref, iw_ref, ak_ref, av_ref, ik_ref, qg_ref, kg_ref, lnw_ref,
                 lnb_ref, o_ref,
                 kn_scr, ikn_scr, vt_scr, key_scr, wt_scr, qn_scr, acc_scr, s_scr,
                 *, tk, topk, idx_w_scale):
    i = pl.program_id(1)
    seq = ak_ref.shape[1]
    tq = aq_ref.shape[1]
    chunk_shift = CHUNK.bit_length() - 1

    @pl.when(i == 0)
    def _():
        def body(c, carry):
            rows = pl.ds(pl.multiple_of(c * tk, tk), tk)
            for g in range(KV_HEADS):
                cols = slice(g * HEAD_DIM, (g + 1) * HEAD_DIM)
                k = ak_ref[0, rows, cols].astype(jnp.float32)
                ms = jnp.mean(k * k, axis=-1, keepdims=True)
                kn_scr[rows, cols] = ((k * lax.rsqrt(ms + EPS)) * kg_ref[...]).astype(jnp.bfloat16)
                v = av_ref[0, rows, cols].astype(jnp.float32)
                vt_scr[g, c, :HEAD_DIM, :] = v.T.astype(jnp.bfloat16)
                vt_scr[g, c, HEAD_DIM:, :] = jnp.ones((SUM_ROWS, tk), jnp.bfloat16)
            ki = ik_ref[0, rows, :IDX_DIM].astype(jnp.float32)
            mu = jnp.mean(ki, axis=-1, keepdims=True)
            var = jnp.mean(jnp.square(ki - mu), axis=-1, keepdims=True)
            y = ((ki - mu) * lax.rsqrt(var + EPS) * lnw_ref[...] + lnb_ref[...]).astype(jnp.bfloat16)
            zeros = jnp.zeros_like(y)
            ikn_scr[0, rows, :] = jnp.concatenate([y, zeros], axis=1)
            ikn_scr[1, rows, :] = jnp.concatenate([zeros, y], axis=1)
            return carry
        lax.fori_loop(0, seq // tk, body, 0)

    t0 = i * tq
    n_kt = (t0 + tq) // tk
    scale = (HEAD_DIM ** -0.5) * math.log2(math.e)
    for h in range(ATTN_HEADS):
        g, r = divmod(h, HEADS_PER_KV)
        q = aq_ref[0, :, h * HEAD_DIM:(h + 1) * HEAD_DIM].astype(jnp.float32)
        ms = jnp.mean(q * q, axis=-1, keepdims=True)
        qn_scr[g, r * tq:(r + 1) * tq, :] = (
            (q * lax.rsqrt(ms + EPS)) * qg_ref[...] * scale).astype(jnp.bfloat16)
    wt_scr[...] = iw_ref[0].astype(jnp.float32).T * idx_w_scale
    w_row = IW_OFF % LANES

    q_chunk = (t0 + lax.broadcasted_iota(jnp.int32, (tk, tq), 1)) >> chunk_shift

    def score_body(kt, carry):
        rows = pl.ds(pl.multiple_of(kt * tk, tk), tk)
        ik_first, ik_second = ikn_scr[0, rows, :], ikn_scr[1, rows, :]
        acc = jnp.zeros((tk, tq), jnp.float32)
        pairs_per_ref = iqa_ref.shape[2] // LANES
        for pair in range(IDX_HEADS // 2):
            src = iqa_ref if pair < pairs_per_ref else iqb_ref
            lane0 = (pair % pairs_per_ref) * LANES
            q_pair = src[0, :, lane0:lane0 + LANES]
            for sub, ik_t in enumerate((ik_first, ik_second)):
                h = 2 * pair + sub
                d = _dot_nt(ik_t, q_pair)
                acc = acc + jnp.maximum(d, 0.0) * wt_scr[w_row + h:w_row + h + 1, :]
        k_chunk = (kt * tk + lax.broadcasted_iota(jnp.int32, (tk, tq), 0)) >> chunk_shift
        key_scr[rows, :] = jnp.where(k_chunk <= q_chunk, acc, -jnp.inf)
        return carry
    lax.fori_loop(0, n_kt, score_body, 0)

    def bit_body(it, lo):
        cand = lo + lax.shift_left(jnp.int32(1), 31 - it)
        cand_f = _ordered_float(cand)

        def count_body(kt, part):
            rows = pl.ds(pl.multiple_of(kt * tk, tk), tk)
            hit = jnp.where(key_scr[rows, :] >= cand_f, 1.0, 0.0).reshape(tk // 8, 8, tq)
            while hit.shape[0] > 1:
                half = hit.shape[0] // 2
                hit = hit[:half] + hit[half:]
            return part + hit[0]
        part = lax.fori_loop(0, n_kt, count_body, jnp.zeros((8, tq), jnp.float32))
        cnt = jnp.sum(part, axis=0, keepdims=True)
        return jnp.where(cnt >= float(topk), cand, lo)
    lo = lax.fori_loop(0, 32, bit_body, jnp.full((1, tq), INT_MIN, jnp.int32))
    thr = jnp.where(lo == INT_MIN, jnp.finfo(jnp.float32).min, _ordered_float(lo))

    def logit_body(kt, m):
        rows = pl.ds(pl.multiple_of(kt * tk, tk), tk)
        bias = jnp.where(key_scr[rows, :] >= thr, 0.0, NEG_BIG)
        bias = jnp.concatenate([bias] * HEADS_PER_KV, axis=1)
        new_m = []
        for g in range(KV_HEADS):
            s = _dot_nt(kn_scr[rows, g * HEAD_DIM:(g + 1) * HEAD_DIM], qn_scr[g]) + bias
            s_scr[g, rows, :] = s
            new_m.append(jnp.maximum(m[g], jnp.max(s, axis=0, keepdims=True)))
        return tuple(new_m)
    m0 = jnp.full((1, HEADS_PER_KV * tq), NEG_BIG, jnp.float32)
    m = lax.fori_loop(0, n_kt, logit_body, (m0,) * KV_HEADS)

    acc_scr[...] = jnp.zeros(acc_scr.shape, jnp.float32)

    def pv_body(kt, carry):
        rows = pl.ds(pl.multiple_of(kt * tk, tk), tk)
        for g in range(KV_HEADS):
            p = jnp.exp2(s_scr[g, rows, :] - m[g]).astype(jnp.bfloat16)
            acc_scr[g] += _dot(vt_scr[g, kt], p)
        return carry
    lax.fori_loop(0, n_kt, pv_body, 0)

    for h in range(ATTN_HEADS):
        g, r = divmod(h, HEADS_PER_KV)
        cols = slice(r * tq, (r + 1) * tq)
        o = acc_scr[g, :HEAD_DIM, cols] / acc_scr[g, HEAD_DIM:HEAD_DIM + 1, cols]
        o_ref[0, :, h * HEAD_DIM:(h + 1) * HEAD_DIM] = o.T.astype(o_ref.dtype)


def _dsa_attention(p3, q_g, k_g, ln_w, ln_b, *, tq, tk):
    b, seq, _ = p3.shape
    topk = min(TOPK_MAX, seq // 4)
    idx_w_scale = (IDX_HEADS ** -0.5) * (IDX_DIM ** -0.5)
    assert seq % tq == 0 and tq % tk == 0 and tk % CHUNK == 0

    def col(off, width):
        assert off % width == 0 or width == LANES
        return off // width

    half_iq = IDX_WIDTH // 2
    return pl.pallas_call(
        functools.partial(_attn_kernel, tk=tk, topk=topk, idx_w_scale=idx_w_scale),
        out_shape=jax.ShapeDtypeStruct((b, seq, ATTN_WIDTH), jnp.bfloat16),
        grid=(b, seq // tq),
        in_specs=[
            pl.BlockSpec((1, tq, ATTN_WIDTH), lambda bi, i: (bi, i, col(AQ_OFF, ATTN_WIDTH))),
            pl.BlockSpec((1, tq, half_iq), lambda bi, i: (bi, i, col(IQ_OFF, half_iq))),
            pl.BlockSpec((1, tq, half_iq), lambda bi, i: (bi, i, col(IQ_OFF, half_iq) + 1)),
            pl.BlockSpec((1, tq, LANES), lambda bi, i: (bi, i, col(IW_OFF, LANES))),
            pl.BlockSpec((1, seq, KV_WIDTH), lambda bi, i: (bi, 0, col(AK_OFF, KV_WIDTH))),
            pl.BlockSpec((1, seq, KV_WIDTH), lambda bi, i: (bi, 0, col(AV_OFF, KV_WIDTH))),
            pl.BlockSpec((1, seq, LANES), lambda bi, i: (bi, 0, col(IK_OFF, LANES))),
            pl.BlockSpec((1, HEAD_DIM), lambda bi, i: (0, 0)),
            pl.BlockSpec((1, HEAD_DIM), lambda bi, i: (0, 0)),
            pl.BlockSpec((1, IDX_DIM), lambda bi, i: (0, 0)),
            pl.BlockSpec((1, IDX_DIM), lambda bi, i: (0, 0)),
        ],
        out_specs=pl.BlockSpec((1, tq, ATTN_WIDTH), lambda bi, i: (bi, i, 0)),
        scratch_shapes=[
            pltpu.VMEM((seq, KV_WIDTH), jnp.bfloat16),
            pltpu.VMEM((2, seq, 2 * IDX_DIM), jnp.bfloat16),
            pltpu.VMEM((KV_HEADS, seq // tk, HEAD_DIM + SUM_ROWS, tk), jnp.bfloat16),
            pltpu.VMEM((seq, tq), jnp.float32),
            pltpu.VMEM((LANES, tq), jnp.float32),
            pltpu.VMEM((KV_HEADS, HEADS_PER_KV * tq, HEAD_DIM), jnp.bfloat16),
            pltpu.VMEM((KV_HEADS, HEAD_DIM + SUM_ROWS, HEADS_PER_KV * tq), jnp.float32),
            pltpu.VMEM((KV_HEADS, seq, HEADS_PER_KV * tq), jnp.float32),
        ],
        compiler_params=_params("arbitrary", "arbitrary"),
        name="dsa_attn",
    )(p3, p3, p3, p3, p3, p3, p3, q_g, k_g, ln_w, ln_b)


def _ret_kernel(lg_ref, rq_ref, rk_ref, rv_ref, rg_ref, cos_ref, sin_ref, g_ref, o_ref, *, rc):
    h = pl.program_id(1)
    lg = lg_ref[h]
    seq = rq_ref.shape[1]
    n = lax.broadcasted_iota(jnp.int32, (rc, RET_DV), 0).astype(jnp.float32)
    cross_decay = jnp.exp(lg * (n + 1.0))
    state_decay = jnp.exp(lg * (rc - 1.0 - n))
    chunk_decay = jnp.exp(lg * jnp.full((RET_DK, RET_DV), float(rc), jnp.float32))
    rel = (lax.broadcasted_iota(jnp.int32, (rc, rc), 0)
           - lax.broadcasted_iota(jnp.int32, (rc, rc), 1)).astype(jnp.float32)
    intra = jnp.where(rel >= 0, jnp.exp(lg * jnp.maximum(rel, 0.0)), 0.0)

    def rot(x, rows):
        return x * cos_ref[rows, :] + pltpu.roll(x, RET_DK // 2, 1) * sin_ref[rows, :]

    state = jnp.zeros((RET_DK, RET_DV), jnp.float32)
    for c in range(seq // rc):
        rows = slice(c * rc, (c + 1) * rc)
        q = rot(rq_ref[0, rows, :].astype(jnp.float32), rows)
        k = rot(rk_ref[0, rows, :].astype(jnp.float32), rows) * (RET_DK ** -0.5)
        v = rv_ref[0, rows, :]
        qb = q.astype(jnp.bfloat16)
        inner = _dot_nt(qb, k.astype(jnp.bfloat16)) * intra
        o = _dot(inner.astype(jnp.bfloat16), v) + _dot(qb, state.astype(jnp.bfloat16)) * cross_decay
        kd_t = (k * state_decay).T.astype(jnp.bfloat16)
        state = state * chunk_decay + _dot(kd_t, v)
        ms = jnp.mean(o * o, axis=-1, keepdims=True)
        y = (o * lax.rsqrt(ms + EPS)) * g_ref[0]
        gate = rg_ref[0, rows, :].astype(jnp.float32)
        o_ref[0, rows, :] = ((gate * (1.0 / (1.0 + jnp.exp(-gate)))) * y).astype(o_ref.dtype)


def _retention(p3, log_gamma, cos2, sin2, ret_g, *, rc):
    b, seq, _ = p3.shape
    assert seq % rc == 0

    def head_spec(off):
        return pl.BlockSpec((1, seq, RET_DK), lambda bi, h: (bi, 0, off // RET_DK + h))

    return pl.pallas_call(
        functools.partial(_ret_kernel, rc=rc),
        out_shape=jax.ShapeDtypeStruct((b, seq, RET_WIDTH), jnp.bfloat16),
        grid=(b, RET_HEADS),
        in_specs=[
            pl.BlockSpec(memory_space=pltpu.SMEM),
            head_spec(RQ_OFF), head_spec(RK_OFF), head_spec(RV_OFF), head_spec(RG_OFF),
            pl.BlockSpec((seq, RET_DK), lambda bi, h: (0, 0)),
            pl.BlockSpec((seq, RET_DK), lambda bi, h: (0, 0)),
            pl.BlockSpec((1, 1, RET_DV), lambda bi, h: (h, 0, 0)),
        ],
        out_specs=pl.BlockSpec((1, seq, RET_DV), lambda bi, h: (bi, 0, h)),
        compiler_params=_params("arbitrary", "arbitrary"),
        name="retention",
    )(log_gamma, p3, p3, p3, p3, cos2, sin2, ret_g)


def _routing(logits):
    lane = lax.broadcasted_iota(jnp.int32, logits.shape, 1).astype(jnp.float32)
    big = float(LANES)
    neg = -jnp.inf

    def first_argmax(v, vmax):
        return jnp.min(jnp.where(v == vmax, lane, big), axis=-1, keepdims=True)

    g_mask = (lane >= N_EXPERTS) & (lane < N_EXPERTS + N_GROUPS)
    gl = jnp.where(g_mask, logits, neg)
    g_max = jnp.max(gl, axis=-1, keepdims=True)
    g_sel = first_argmax(gl, g_max) - N_EXPERTS
    g_gate = 1.0 / jnp.sum(jnp.where(g_mask, jnp.exp(gl - g_max), 0.0), axis=-1, keepdims=True)

    e_lo = g_sel * EXPERTS_PER_GROUP
    el = jnp.where((lane >= e_lo) & (lane < e_lo + EXPERTS_PER_GROUP), logits, neg)
    v1 = jnp.max(el, axis=-1, keepdims=True)
    i1 = first_argmax(el, v1)
    el2 = jnp.where(lane == i1, neg, el)
    v2 = jnp.max(el2, axis=-1, keepdims=True)
    i2 = first_argmax(el2, v2)
    e2 = jnp.exp(v2 - v1)
    denom = 1.0 + e2
    w1 = (1.0 / denom) * g_gate
    w2 = (e2 / denom) * g_gate
    route = jnp.where(lane == ROUTE_E1, i1, 0.0) + jnp.where(lane == ROUTE_E2, i2, 0.0)
    return route + jnp.where(lane == ROUTE_W1, w1, 0.0) + jnp.where(lane == ROUTE_W2, w2, 0.0)


def _norm2(x1, g):
    ms = jnp.mean(x1 * x1, axis=-1, keepdims=True)
    return (x1 * lax.rsqrt(ms + EPS)) * g


def _out_proj_kernel(a_ref, r_ref, x_ref, wa_ref, wr_ref, g_ref, rcat_ref,
                     x1_ref, h2_ref, route_ref):
    mixed = _dot(a_ref[...], wa_ref[...]) + _dot(r_ref[...], wr_ref[...])
    x1 = x_ref[...] + mixed
    x1_ref[...] = x1
    h2 = _norm2(x1, g_ref[...])
    h2_ref[...] = h2
    hi = h2.astype(jnp.bfloat16)
    lo = (h2 - hi.astype(jnp.float32)).astype(jnp.bfloat16)
    both = _dot(hi, rcat_ref[...])
    logits = both[:, :LANES] + (both[:, LANES:] + _dot(lo, rcat_ref[:, :LANES]))
    route_ref[...] = _routing(logits)


def _out_proj(attn2d, ret2d, x2d, w_out_bf, g2, r_cat, *, tm):
    n, d = x2d.shape
    return pl.pallas_call(
        _out_proj_kernel,
        out_shape=(
            jax.ShapeDtypeStruct((n, d), jnp.float32),
            jax.ShapeDtypeStruct((n, d), jnp.float32),
            jax.ShapeDtypeStruct((n, LANES), jnp.float32),
        ),
        grid=(n // tm,),
        in_specs=[
            pl.BlockSpec((tm, ATTN_WIDTH), lambda i: (i, 0)),
            pl.BlockSpec((tm, RET_WIDTH), lambda i: (i, 0)),
            pl.BlockSpec((tm, d), lambda i: (i, 0)),
            pl.BlockSpec((ATTN_WIDTH, d), lambda i: (0, 0)),
            pl.BlockSpec((RET_WIDTH, d), lambda i: (ATTN_WIDTH // RET_WIDTH, 0)),
            pl.BlockSpec((1, d), lambda i: (0, 0)),
            pl.BlockSpec((d, 2 * LANES), lambda i: (0, 0)),
        ],
        out_specs=(
            pl.BlockSpec((tm, d), lambda i: (i, 0)),
            pl.BlockSpec((tm, d), lambda i: (i, 0)),
            pl.BlockSpec((tm, LANES), lambda i: (i, 0)),
        ),
        compiler_params=_params("arbitrary"),
        name="out_proj",
    )(attn2d, ret2d, x2d, w_out_bf, w_out_bf, g2, r_cat)


def _plan_kernel(route_ref, pos_ref, cnt_ref, rank_scr, *, tm, blk):
    n = route_ref.shape[0]
    lane = lax.broadcasted_iota(jnp.int32, (blk, LANES), 1).astype(jnp.float32)
    before = (lax.broadcasted_iota(jnp.int32, (blk, blk), 1)
              < lax.broadcasted_iota(jnp.int32, (blk, blk), 0)).astype(jnp.bfloat16)

    def one_hot(rows):
        r = route_ref[rows, :]
        e1 = r[:, ROUTE_E1:ROUTE_E1 + 1]
        e2 = r[:, ROUTE_E2:ROUTE_E2 + 1]
        return lane == e1, lane == e2

    def rank_body(b, run):
        rows = pl.ds(pl.multiple_of(b * blk, blk), blk)
        m1, m2 = one_hot(rows)
        sel = jnp.where(m1 | m2, 1.0, 0.0)
        rank_scr[rows, :] = _dot(before, sel.astype(jnp.bfloat16)) + run
        return run + jnp.sum(sel, axis=0, keepdims=True)
    cnt = lax.fori_loop(0, n // blk, rank_body, jnp.zeros((1, LANES), jnp.float32))
    cnt_ref[...] = jnp.broadcast_to(cnt, cnt_ref.shape)

    tiles = jnp.floor((cnt + (tm - 1.0)) * (1.0 / tm))
    below = (lax.broadcasted_iota(jnp.int32, (LANES, LANES), 0)
             < lax.broadcasted_iota(jnp.int32, (LANES, LANES), 1)).astype(jnp.bfloat16)
    start = _dot(jnp.broadcast_to(tiles, (8, LANES)).astype(jnp.bfloat16), below)[0:1, :] * float(tm)

    def pos_body(b, carry):
        rows = pl.ds(pl.multiple_of(b * blk, blk), blk)
        m1, m2 = one_hot(rows)
        dest = rank_scr[rows, :] + start
        p1 = jnp.sum(jnp.where(m1, dest, 0.0), axis=-1, keepdims=True)
        p2 = jnp.sum(jnp.where(m2, dest, 0.0), axis=-1, keepdims=True)
        pos_ref[rows, :] = (jnp.where(lane == 0.0, p1, 0.0) + jnp.where(lane == 1.0, p2, 0.0)).astype(jnp.int32)
        return carry
    lax.fori_loop(0, n // blk, pos_body, 0)


def _moe_plan(route, *, tm, blk):
    n = route.shape[0]
    return pl.pallas_call(
        functools.partial(_plan_kernel, tm=tm, blk=blk),
        out_shape=(jax.ShapeDtypeStruct((n, LANES), jnp.int32),
                   jax.ShapeDtypeStruct((8, LANES), jnp.float32)),
        scratch_shapes=[pltpu.VMEM((n, LANES), jnp.float32)],
        compiler_params=pltpu.CompilerParams(vmem_limit_bytes=VMEM_LIMIT),
        name="moe_plan",
    )(route)


def _row_copy(src, src_row, dst, dst_row, sem):
    return pltpu.make_async_copy(src.at[pl.ds(src_row, 1), :], dst.at[pl.ds(dst_row, 1), :], sem)


def _invert_kernel(pos_ref, sid_ref):
    def body(j, carry):
        sid_ref[pos_ref[j]] = j
        return carry
    lax.fori_loop(0, pos_ref.shape[0], body, 0, unroll=8)


def _moe_invert(pos, *, rows):
    return pl.pallas_call(
        _invert_kernel,
        out_shape=jax.ShapeDtypeStruct((rows,), jnp.int32),
        in_specs=[pl.BlockSpec(memory_space=pltpu.SMEM)],
        out_specs=pl.BlockSpec(memory_space=pltpu.SMEM),
        name="moe_invert",
    )(pos)


ROW_GROUP = 8
NO_NEXT, NOT_FIRST = -1, -2


def _ffn_kernel(te_ref, tv_ref, nu_ref, seg_ref, nexte_ref, sid_ref, tok_ref,
                h2_ref, w1_ref, w3_ref, w2_ref, y2_ref,
                xbuf, ybuf, w1buf, w3buf, w2buf, gsem, ssem, wsem):
    t = pl.program_id(0)
    n_used = nu_ref[0]
    tm = xbuf.shape[1]

    def weight_copies(e, wslot):
        return [pltpu.make_async_copy(src.at[e], dst.at[wslot], wsem.at[wslot])
                for src, dst in ((w1_ref, w1buf), (w3_ref, w3buf), (w2_ref, w2buf))]

    def gather(slot):
        return tok_ref, lambda r, tok: _row_copy(h2_ref, tok, xbuf.at[slot], r, gsem.at[slot])

    def scatter(slot):
        return sid_ref, lambda r, sid: _row_copy(ybuf.at[slot], r, y2_ref, sid, ssem.at[slot])

    def start_partial(tile, table, copy):
        valid = tv_ref[tile]

        def body(c, carry):
            for u in range(ROW_GROUP):
                r = c * ROW_GROUP + u

                @pl.when(r < valid)
                def _():
                    copy(r, table[tile * tm + r]).start()
            return carry
        lax.fori_loop(0, (valid + (ROW_GROUP - 1)) // ROW_GROUP, body, 0)

    def start_all(tile, table, copy):
        @pl.when(tv_ref[tile] == tm)
        def _():
            for r in range(tm):
                copy(r, table[tile * tm + r]).start()

        @pl.when(tv_ref[tile] < tm)
        def _():
            start_partial(tile, table, copy)

    def wait_all(tile, copy, tile_copy):
        @pl.when(tv_ref[tile] == tm)
        def _():
            tile_copy.wait()

        @pl.when(tv_ref[tile] < tm)
        def _():
            def body(r, carry):
                copy(0, 0).wait()
                return carry
            lax.fori_loop(0, tv_ref[tile], body, 0)

    def wait_gather(tile, slot):
        wait_all(tile, gather(slot)[1],
                 pltpu.make_async_copy(h2_ref.at[pl.ds(0, tm), :], xbuf.at[slot], gsem.at[slot]))

    def wait_scatter(tile, slot):
        wait_all(tile, scatter(slot)[1],
                 pltpu.make_async_copy(ybuf.at[slot], y2_ref.at[pl.ds(0, tm), :], ssem.at[slot]))

    @pl.when(t == 0)
    def _():
        for cp in weight_copies(te_ref[0], 0):
            cp.start()
        start_partial(0, *gather(0))

    @pl.when(t < n_used)
    def _():
        slot = t % 2
        wslot = seg_ref[t] % 2

        @pl.when(t + 1 < n_used)
        def _():
            start_all(t + 1, *gather(1 - slot))

        @pl.when(nexte_ref[t] != NOT_FIRST)
        def _():
            for cp in weight_copies(te_ref[t], wslot):
                cp.wait()

            @pl.when(nexte_ref[t] >= 0)
            def _():
                for cp in weight_copies(nexte_ref[t], 1 - wslot):
                    cp.start()

        wait_gather(t, slot)

        @pl.when(t >= 2)
        def _():
            wait_scatter(t - 2, slot)

        row = lax.broadcasted_iota(jnp.int32, xbuf.shape[1:], 0)
        x = jnp.where(row < tv_ref[t], xbuf[slot], 0.0).astype(jnp.bfloat16)
        a = _dot(x, w1buf[wslot].astype(jnp.bfloat16))
        b = _dot(x, w3buf[wslot].astype(jnp.bfloat16))
        act = (a * (1.0 / (1.0 + jnp.exp(-a)))) * b
        ybuf[slot] = _dot(act.astype(jnp.bfloat16), w2buf[wslot].astype(jnp.bfloat16))
        start_all(t, *scatter(slot))

    @pl.when(t == pl.num_programs(0) - 1)
    def _():
        @pl.when(n_used >= 2)
        def _():
            wait_scatter(n_used - 2, n_used % 2)
        wait_scatter(n_used - 1, (n_used - 1) % 2)


def _moe_ffn(tile_expert, tile_valid, n_used, tile_seg, tile_next, row_sid, row_tok, h2, w1, w3, w2,
             *, tm):
    n, d = h2.shape
    _, _, ff = w1.shape
    max_tiles = tile_expert.shape[0]
    any_spec = pl.BlockSpec(memory_space=pl.ANY)
    return pl.pallas_call(
        _ffn_kernel,
        out_shape=jax.ShapeDtypeStruct((2 * n, d), jnp.float32),
        grid_spec=pltpu.PrefetchScalarGridSpec(
            num_scalar_prefetch=7,
            grid=(max_tiles,),
            in_specs=[any_spec, any_spec, any_spec, any_spec],
            out_specs=any_spec,
            scratch_shapes=[
                pltpu.VMEM((2, tm, d), jnp.float32),
                pltpu.VMEM((2, tm, d), jnp.float32),
                pltpu.VMEM((2, d, ff), jnp.float32),
                pltpu.VMEM((2, d, ff), jnp.float32),
                pltpu.VMEM((2, ff, d), jnp.float32),
                pltpu.SemaphoreType.DMA((2,)),
                pltpu.SemaphoreType.DMA((2,)),
                pltpu.SemaphoreType.DMA((2,)),
            ],
        ),
        compiler_params=_params("arbitrary"),
        name="moe_ffn",
    )(tile_expert, tile_valid, n_used, tile_seg, tile_next, row_sid, row_tok, h2, w1, w3, w2)


def _combine_kernel(x1_ref, route_ref, ya_ref, yb_ref, o_ref):
    route = route_ref[...]
    w1 = route[:, ROUTE_W1:ROUTE_W1 + 1]
    w2 = route[:, ROUTE_W2:ROUTE_W2 + 1]
    o_ref[...] = x1_ref[...] + (w1 * ya_ref[...] + w2 * yb_ref[...])


def _moe_combine(x1, route, y2, *, tm):
    n, d = x1.shape
    return pl.pallas_call(
        _combine_kernel,
        out_shape=jax.ShapeDtypeStruct((n, d), jnp.float32),
        grid=(n // tm,),
        in_specs=[pl.BlockSpec((tm, d), lambda i: (i, 0)),
                  pl.BlockSpec((tm, LANES), lambda i: (i, 0)),
                  pl.BlockSpec((tm, d), lambda i: (i, 0)),
                  pl.BlockSpec((tm, d), lambda i: (i + n // tm, 0))],
        out_specs=pl.BlockSpec((tm, d), lambda i: (i, 0)),
        compiler_params=_params("arbitrary"),
        name="moe_combine",
    )(x1, route, y2, y2)


def _moe(x1, h2, route, w1, w3, w2, *, tm, gather_tm):
    n, d = x1.shape
    max_tiles = (2 * n) // tm + N_EXPERTS
    pos2d, cnt = _moe_plan(route, tm=tm, blk=gather_tm)
    pos = pos2d[:, :2].T.reshape(2 * n)
    counts = cnt[0, :N_EXPERTS].astype(jnp.int32)
    tiles = (counts + (tm - 1)) // tm
    ends = jnp.cumsum(tiles)
    t_idx = jnp.arange(max_tiles, dtype=jnp.int32)
    tile_expert = jnp.sum((ends[None, :] <= t_idx[:, None]).astype(jnp.int32), axis=1)
    tile_expert = jnp.minimum(tile_expert, N_EXPERTS - 1)
    first_tile = (ends - tiles)[tile_expert]
    tile_valid = jnp.clip(counts[tile_expert] - (t_idx - first_tile) * tm, 0, tm).astype(jnp.int32)
    n_used = ends[-1:].astype(jnp.int32)
    used = t_idx < n_used[0]
    is_first = used & (t_idx == first_tile)
    tile_seg = (jnp.cumsum(is_first.astype(jnp.int32)) - 1).astype(jnp.int32)
    next_start = first_tile + tiles[tile_expert]
    next_e = jnp.where(next_start < n_used[0], tile_expert[jnp.minimum(next_start, max_tiles - 1)], NO_NEXT)
    tile_next = jnp.where(is_first, next_e, NOT_FIRST).astype(jnp.int32)

    row_sid = _moe_invert(pos, rows=max_tiles * tm)
    row_tok = jnp.where(row_sid >= n, row_sid - n, row_sid)
    y2 = _moe_ffn(tile_expert, tile_valid, n_used, tile_seg, tile_next, row_sid, row_tok, h2, w1, w3, w2,
                  tm=tm)
    return _moe_combine(x1, route, y2, tm=gather_tm)


def _pack_router(w_group, w_router):
    d = w_group.shape[0]
    experts = jnp.transpose(w_router, (1, 0, 2)).reshape(d, N_EXPERTS)
    wr = jnp.concatenate(
        [experts, w_group, jnp.zeros((d, LANES - N_EXPERTS - N_GROUPS), w_group.dtype)], axis=1)
    hi = wr.astype(jnp.bfloat16)
    lo = (wr - hi.astype(jnp.float32)).astype(jnp.bfloat16)
    return jnp.concatenate([hi, lo], axis=1)


def _rotation_tables(seq):
    half = RET_DK // 2
    pos = jnp.arange(seq, dtype=jnp.float32)
    inv = 1.0 / (ROT_BASE ** jnp.linspace(0.0, 1.0, half, dtype=jnp.float32))
    ang = pos[:, None] * inv[None, :]
    c, s = jnp.cos(ang), jnp.sin(ang)
    return jnp.concatenate([c, c], axis=-1), jnp.concatenate([-s, s], axis=-1)


def _tiles(n, seq):
    def fit(total, want):
        t = min(total, want)
        while total % t:
            t //= 2
        return t
    return dict(
        proj_tm=fit(n, 1024),
        attn_tq=fit(seq, 256), attn_tk=fit(seq, 256),
        ret_rc=fit(seq, 256),
        out_tm=fit(n, 256),
        moe_tm=fit(n, 256), moe_gather_tm=fit(n, 256),
    )


def kernel(x, norm1_g, w_in, q_norm_g, k_norm_g, idx_k_ln_w, idx_k_ln_b, ret_norm_g,
           w_out, norm2_g, w_group, w_router, w1, w3, w2):
    b, seq, d = x.shape
    n = b * seq
    depth = w_in.shape[0]
    t = _tiles(n, seq)
    cos2, sin2 = _rotation_tables(seq)
    log_gamma = jnp.log1p(-jnp.exp2(-5.0 - jnp.arange(RET_HEADS, dtype=jnp.float32)))

    x2d = x.reshape(n, d)
    for l in range(depth):
        proj = _in_proj(x2d, norm1_g[l][None, :], w_in[l].T, tm=t["proj_tm"])
        p3 = proj.reshape(b, seq, proj.shape[1])
        attn = _dsa_attention(p3, q_norm_g[l][None, :], k_norm_g[l][None, :],
                              idx_k_ln_w[l][None, :], idx_k_ln_b[l][None, :],
                              tq=t["attn_tq"], tk=t["attn_tk"])
        ret = _retention(p3, log_gamma, cos2, sin2, ret_norm_g[l].reshape(RET_HEADS, 1, RET_DV),
                         rc=t["ret_rc"])
        r_cat = _pack_router(w_group[l], w_router[l])
        g2 = norm2_g[l][None, :]
        x1, h2, route = _out_proj(attn.reshape(n, ATTN_WIDTH), ret.reshape(n, RET_WIDTH), x2d,
                                  w_out[l].astype(jnp.bfloat16), g2, r_cat, tm=t["out_tm"])
        x2d = _moe(x1, h2, route, w1[l], w3[l], w2[l], tm=t["moe_tm"], gather_tm=t["moe_gather_tm"])
    return x2d.reshape(b, seq, d)
```

```python
import functools
import math

import jax
import jax.numpy as jnp
from jax import lax
from jax.experimental import pallas as pl
from jax.experimental.pallas import tpu as pltpu

CHUNK = 64
ATTN_HEADS = 8
HEAD_DIM = 128
KV_HEADS = 2
HEADS_PER_KV = ATTN_HEADS // KV_HEADS
IDX_HEADS = 16
IDX_DIM = 64
TOPK_MAX = 256
RET_HEADS = 8
RET_DK = 128
RET_DV = 128
ROT_BASE = 10000.0
N_GROUPS = 4
EXPERTS_PER_GROUP = 8
N_EXPERTS = N_GROUPS * EXPERTS_PER_GROUP
EPS = 1e-6

ATTN_WIDTH = ATTN_HEADS * HEAD_DIM
KV_WIDTH = KV_HEADS * HEAD_DIM
IDX_WIDTH = IDX_HEADS * IDX_DIM
RET_WIDTH = RET_HEADS * RET_DK

LANES = 128
VMEM_LIMIT = 56 * 1024 * 1024

AQ_OFF = 0
AK_OFF = AQ_OFF + ATTN_WIDTH
AV_OFF = AK_OFF + KV_WIDTH
IQ_OFF = AV_OFF + KV_WIDTH
IK_OFF = IQ_OFF + IDX_WIDTH
IW_OFF = IK_OFF + IDX_DIM
W_RET = IW_OFF + IDX_HEADS
IN_WIDTH = W_RET + 4 * RET_WIDTH
assert IW_OFF // LANES == IK_OFF // LANES
PROJ_TN = 1024
RQ_OFF = -(-W_RET // PROJ_TN) * PROJ_TN
RK_OFF = RQ_OFF + RET_WIDTH
RV_OFF = RK_OFF + RET_WIDTH
RG_OFF = RV_OFF + RET_WIDTH
PROJ_WIDTH = RG_OFF + RET_WIDTH

ROUTE_E1, ROUTE_E2, ROUTE_W1, ROUTE_W2 = 0, 1, 2, 3

SUM_ROWS = 16

INT_MIN = -(2 ** 31)
NEG_BIG = -1e30

_NT = (((1,), (1,)), ((), ()))


def _dot(a, b):
    return jnp.dot(a, b, preferred_element_type=jnp.float32)


def _dot_nt(a, b):
    return lax.dot_general(a, b, _NT, preferred_element_type=jnp.float32)


def _params(*sem):
    return pltpu.CompilerParams(dimension_semantics=sem, vmem_limit_bytes=VMEM_LIMIT)


def _in_proj_kernel(x_ref, g_ref, wt_ref, o_ref, h_scr, *, row_chunk):
    @pl.when(pl.program_id(1) == 0)
    def _():
        def body(c, carry):
            rows = pl.ds(pl.multiple_of(c * row_chunk, row_chunk), row_chunk)
            x = x_ref[rows, :]
            ms = jnp.mean(x * x, axis=-1, keepdims=True)
            h_scr[rows, :] = ((x * lax.rsqrt(ms + EPS)) * g_ref[...]).astype(jnp.bfloat16)
            return carry
        lax.fori_loop(0, x_ref.shape[0] // row_chunk, body, 0)

    o_ref[...] = _dot_nt(h_scr[...], wt_ref[...].astype(jnp.bfloat16)).astype(o_ref.dtype)


def _in_proj(x2d, g, w_in_t, *, tm):
    n, d = x2d.shape
    tn = PROJ_TN
    assert w_in_t.shape == (IN_WIDTH, d) and W_RET % 8 == 0
    attn_tiles = RQ_OFF // tn

    def window(i, j):
        start8 = jnp.where(j < attn_tiles, j * (tn // 8), W_RET // 8 + (j - attn_tiles) * (tn // 8))
        return 8 * start8, 0

    return pl.pallas_call(
        functools.partial(_in_proj_kernel, row_chunk=min(tm, 128)),
        out_shape=jax.ShapeDtypeStruct((n, PROJ_WIDTH), jnp.bfloat16),
        grid=(n // tm, PROJ_WIDTH // tn),
        in_specs=[
            pl.BlockSpec((tm, d), lambda i, j: (i, 0)),
            pl.BlockSpec((1, d), lambda i, j: (0, 0)),
            pl.BlockSpec((pl.Element(tn), pl.Element(d)), window),
        ],
        out_specs=pl.BlockSpec((tm, tn), lambda i, j: (i, j)),
        scratch_shapes=[pltpu.VMEM((tm, d), jnp.bfloat16)],
        compiler_params=_params("arbitrary", "arbitrary"),
        name="in_proj",
    )(x2d, g, w_in_t)


def _ordered_float(v):
    bits = v ^ ((v >> 31) & jnp.int32(0x7FFFFFFF))
    return pltpu.bitcast(bits, jnp.float32)


def _attn_kernel(aq_ref, iqa_ref, iqb_ref, iw_ref, ak_ref, av_ref, ik_ref, qg_ref, kg_ref, lnw_ref,
                 lnb_ref, o_ref,
                 kn_scr, ikn_scr, vt_scr, key_scr, wt_scr, qn_scr, acc_scr, s_scr,
                 *, tk, topk, idx_w_scale):
    i = pl.program_id(1)
    seq = ak_ref.shape[1]
    tq = aq_ref.shape[1]
    chunk_shift = CHUNK.bit_length() - 1

    @pl.when(i == 0)
    def _():
        def body(c, carry):
            rows = pl.ds(pl.multiple_of(c * tk, tk), tk)
            for g in range(KV_HEADS):
                cols = slice(g * HEAD_DIM, (g + 1) * HEAD_DIM)
                k = ak_ref[0, rows, cols].astype(jnp.float32)
                ms = jnp.mean(k * k, axis=-1, keepdims=True)
                kn_scr[rows, cols] = ((k * lax.rsqrt(ms + EPS)) * kg_ref[...]).astype(jnp.bfloat16)
                v = av_ref[0, rows, cols].astype(jnp.float32)
                vt_scr[g, c, :HEAD_DIM, :] = v.T.astype(jnp.bfloat16)
                vt_scr[g, c, HEAD_DIM:, :] = jnp.ones((SUM_ROWS, tk), jnp.bfloat16)
            ki = ik_ref[0, rows, :IDX_DIM].astype(jnp.float32)
            mu = jnp.mean(ki, axis=-1, keepdims=True)
            var = jnp.mean(jnp.square(ki - mu), axis=-1, keepdims=True)
            y = ((ki - mu) * lax.rsqrt(var + EPS) * lnw_ref[...] + lnb_ref[...]).astype(jnp.bfloat16)
            zeros = jnp.zeros_like(y)
            ikn_scr[0, rows, :] = jnp.concatenate([y, zeros], axis=1)
            ikn_scr[1, rows, :] = jnp.concatenate([zeros, y], axis=1)
            return carry
        lax.fori_loop(0, seq // tk, body, 0)

    t0 = i * tq
    n_kt = (t0 + tq) // tk
    scale = (HEAD_DIM ** -0.5) * math.log2(math.e)
    for h in range(ATTN_HEADS):
        g, r = divmod(h, HEADS_PER_KV)
        q = aq_ref[0, :, h * HEAD_DIM:(h + 1) * HEAD_DIM].astype(jnp.float32)
        ms = jnp.mean(q * q, axis=-1, keepdims=True)
        qn_scr[g, r * tq:(r + 1) * tq, :] = (
            (q * lax.rsqrt(ms + EPS)) * qg_ref[...] * scale).astype(jnp.bfloat16)
    wt_scr[...] = iw_ref[0].astype(jnp.float32).T * idx_w_scale
    w_row = IW_OFF % LANES

    q_chunk = (t0 + lax.broadcasted_iota(jnp.int32, (tk, tq), 1)) >> chunk_shift

    def score_body(kt, carry):
        rows = pl.ds(pl.multiple_of(kt * tk, tk), tk)
        ik_first, ik_second = ikn_scr[0, rows, :], ikn_scr[1, rows, :]
        acc = jnp.zeros((tk, tq), jnp.float32)
        pairs_per_ref = iqa_ref.shape[2] // LANES
        for pair in range(IDX_HEADS // 2):
            src = iqa_ref if pair < pairs_per_ref else iqb_ref
            lane0 = (pair % pairs_per_ref) * LANES
            q_pair = src[0, :, lane0:lane0 + LANES]
            for sub, ik_t in enumerate((ik_first, ik_second)):
                h = 2 * pair + sub
                d = _dot_nt(ik_t, q_pair)
                acc = acc + jnp.maximum(d, 0.0) * wt_scr[w_row + h:w_row + h + 1, :]
        k_chunk = (kt * tk + lax.broadcasted_iota(jnp.int32, (tk, tq), 0)) >> chunk_shift
        key_scr[rows, :] = jnp.where(k_chunk <= q_chunk, acc, -jnp.inf)
        return carry
    lax.fori_loop(0, n_kt, score_body, 0)

    def bit_body(it, lo):
        cand = lo + lax.shift_left(jnp.int32(1), 31 - it)
        cand_f = _ordered_float(cand)

        def count_body(kt, part):
            rows = pl.ds(pl.multiple_of(kt * tk, tk), tk)
            hit = jnp.where(key_scr[rows, :] >= cand_f, 1.0, 0.0).reshape(tk // 8, 8, tq)
            while hit.shape[0] > 1:
                half = hit.shape[0] // 2
                hit = hit[:half] + hit[half:]
            return part + hit[0]
        part = lax.fori_loop(0, n_kt, count_body, jnp.zeros((8, tq), jnp.float32))
        cnt = jnp.sum(part, axis=0, keepdims=True)
        return jnp.where(cnt >= float(topk), cand, lo)
    lo = lax.fori_loop(0, 32, bit_body, jnp.full((1, tq), INT_MIN, jnp.int32))
    thr = jnp.where(lo == INT_MIN, jnp.finfo(jnp.float32).min, _ordered_float(lo))

    def logit_body(kt, m):
        rows = pl.ds(pl.multiple_of(kt * tk, tk), tk)
        bias = jnp.where(key_scr[rows, :] >= thr, 0.0, NEG_BIG)
        bias = jnp.concatenate([bias] * HEADS_PER_KV, axis=1)
        new_m = []
        for g in range(KV_HEADS):
            s = _dot_nt(kn_scr[rows, g * HEAD_DIM:(g + 1) * HEAD_DIM], qn_scr[g]) + bias
            s_scr[g, rows, :] = s
            new_m.append(jnp.maximum(m[g], jnp.max(s, axis=0, keepdims=True)))
        return tuple(new_m)
    m0 = jnp.full((1, HEADS_PER_KV * tq), NEG_BIG, jnp.float32)
    m = lax.fori_loop(0, n_kt, logit_body, (m0,) * KV_HEADS)

    acc_scr[...] = jnp.zeros(acc_scr.shape, jnp.float32)

    def pv_body(kt, carry):
        rows = pl.ds(pl.multiple_of(kt * tk, tk), tk)
        for g in range(KV_HEADS):
            p = jnp.exp2(s_scr[g, rows, :] - m[g]).astype(jnp.bfloat16)
            acc_scr[g] += _dot(vt_scr[g, kt], p)
        return carry
    lax.fori_loop(0, n_kt, pv_body, 0)

    for h in range(ATTN_HEADS):
        g, r = divmod(h, HEADS_PER_KV)
        cols = slice(r * tq, (r + 1) * tq)
        o = acc_scr[g, :HEAD_DIM, cols] / acc_scr[g, HEAD_DIM:HEAD_DIM + 1, cols]
        o_ref[0, :, h * HEAD_DIM:(h + 1) * HEAD_DIM] = o.T.astype(o_ref.dtype)


def _dsa_attention(p3, q_g, k_g, ln_w, ln_b, *, tq, tk):
    b, seq, _ = p3.shape
    topk = min(TOPK_MAX, seq // 4)
    idx_w_scale = (IDX_HEADS ** -0.5) * (IDX_DIM ** -0.5)
    assert seq % tq == 0 and tq % tk == 0 and tk % CHUNK == 0

    def col(off, width):
        assert off % width == 0 or width == LANES
        return off // width

    half_iq = IDX_WIDTH // 2
    return pl.pallas_call(
        functools.partial(_attn_kernel, tk=tk, topk=topk, idx_w_scale=idx_w_scale),
        out_shape=jax.ShapeDtypeStruct((b, seq, ATTN_WIDTH), jnp.bfloat16),
        grid=(b, seq // tq),
        in_specs=[
            pl.BlockSpec((1, tq, ATTN_WIDTH), lambda bi, i: (bi, i, col(AQ_OFF, ATTN_WIDTH))),
            pl.BlockSpec((1, tq, half_iq), lambda bi, i: (bi, i, col(IQ_OFF, half_iq))),
            pl.BlockSpec((1, tq, half_iq), lambda bi, i: (bi, i, col(IQ_OFF, half_iq) + 1)),
            pl.BlockSpec((1, tq, LANES), lambda bi, i: (bi, i, col(IW_OFF, LANES))),
            pl.BlockSpec((1, seq, KV_WIDTH), lambda bi, i: (bi, 0, col(AK_OFF, KV_WIDTH))),
            pl.BlockSpec((1, seq, KV_WIDTH), lambda bi, i: (bi, 0, col(AV_OFF, KV_WIDTH))),
            pl.BlockSpec((1, seq, LANES), lambda bi, i: (bi, 0, col(IK_OFF, LANES))),
            pl.BlockSpec((1, HEAD_DIM), lambda bi, i: (0, 0)),
            pl.BlockSpec((1, HEAD_DIM), lambda bi, i: (0, 0)),
            pl.BlockSpec((1, IDX_DIM), lambda bi, i: (0, 0)),
            pl.BlockSpec((1, IDX_DIM), lambda bi, i: (0, 0)),
        ],
        out_specs=pl.BlockSpec((1, tq, ATTN_WIDTH), lambda bi, i: (bi, i, 0)),
        scratch_shapes=[
            pltpu.VMEM((seq, KV_WIDTH), jnp.bfloat16),
            pltpu.VMEM((2, seq, 2 * IDX_DIM), jnp.bfloat16),
            pltpu.VMEM((KV_HEADS, seq // tk, HEAD_DIM + SUM_ROWS, tk), jnp.bfloat16),
            pltpu.VMEM((seq, tq), jnp.float32),
            pltpu.VMEM((LANES, tq), jnp.float32),
            pltpu.VMEM((KV_HEADS, HEADS_PER_KV * tq, HEAD_DIM), jnp.bfloat16),
            pltpu.VMEM((KV_HEADS, HEAD_DIM + SUM_ROWS, HEADS_PER_KV * tq), jnp.float32),
            pltpu.VMEM((KV_HEADS, seq, HEADS_PER_KV * tq), jnp.float32),
        ],
        compiler_params=_params("arbitrary", "arbitrary"),
        name="dsa_attn",
    )(p3, p3, p3, p3, p3, p3, p3, q_g, k_g, ln_w, ln_b)


def _ret_kernel(lg_ref, rq_ref, rk_ref, rv_ref, rg_ref, cos_ref, sin_ref, g_ref, o_ref, *, rc):
    h = pl.program_id(1)
    lg = lg_ref[h]
    seq = rq_ref.shape[1]
    n = lax.broadcasted_iota(jnp.int32, (rc, RET_DV), 0).astype(jnp.float32)
    cross_decay = jnp.exp(lg * (n + 1.0))
    state_decay = jnp.exp(lg * (rc - 1.0 - n))
    chunk_decay = jnp.exp(lg * jnp.full((RET_DK, RET_DV), float(rc), jnp.float32))
    rel = (lax.broadcasted_iota(jnp.int32, (rc, rc), 0)
           - lax.broadcasted_iota(jnp.int32, (rc, rc), 1)).astype(jnp.float32)
    intra = jnp.where(rel >= 0, jnp.exp(lg * jnp.maximum(rel, 0.0)), 0.0)

    def rot(x, rows):
        return x * cos_ref[rows, :] + pltpu.roll(x, RET_DK // 2, 1) * sin_ref[rows, :]

    state = jnp.zeros((RET_DK, RET_DV), jnp.float32)
    for c in range(seq // rc):
        rows = slice(c * rc, (c + 1) * rc)
        q = rot(rq_ref[0, rows, :].astype(jnp.float32), rows)
        k = rot(rk_ref[0, rows, :].astype(jnp.float32), rows) * (RET_DK ** -0.5)
        v = rv_ref[0, rows, :]
        qb = q.astype(jnp.bfloat16)
        inner = _dot_nt(qb, k.astype(jnp.bfloat16)) * intra
        o = _dot(inner.astype(jnp.bfloat16), v) + _dot(qb, state.astype(jnp.bfloat16)) * cross_decay
        kd_t = (k * state_decay).T.astype(jnp.bfloat16)
        state = state * chunk_decay + _dot(kd_t, v)
        ms = jnp.mean(o * o, axis=-1, keepdims=True)
        y = (o * lax.rsqrt(ms + EPS)) * g_ref[0]
        gate = rg_ref[0, rows, :].astype(jnp.float32)
        o_ref[0, rows, :] = ((gate * (1.0 / (1.0 + jnp.exp(-gate)))) * y).astype(o_ref.dtype)


def _retention(p3, log_gamma, cos2, sin2, ret_g, *, rc):
    b, seq, _ = p3.shape
    assert seq % rc == 0

    def head_spec(off):
        return pl.BlockSpec((1, seq, RET_DK), lambda bi, h: (bi, 0, off // RET_DK + h))

    return pl.pallas_call(
        functools.partial(_ret_kernel, rc=rc),
        out_shape=jax.ShapeDtypeStruct((b, seq, RET_WIDTH), jnp.bfloat16),
        grid=(b, RET_HEADS),
        in_specs=[
            pl.BlockSpec(memory_space=pltpu.SMEM),
            head_spec(RQ_OFF), head_spec(RK_OFF), head_spec(RV_OFF), head_spec(RG_OFF),
            pl.BlockSpec((seq, RET_DK), lambda bi, h: (0, 0)),
            pl.BlockSpec((seq, RET_DK), lambda bi, h: (0, 0)),
            pl.BlockSpec((1, 1, RET_DV), lambda bi, h: (h, 0, 0)),
        ],
        out_specs=pl.BlockSpec((1, seq, RET_DV), lambda bi, h: (bi, 0, h)),
        compiler_params=_params("arbitrary", "arbitrary"),
        name="retention",
    )(log_gamma, p3, p3, p3, p3, cos2, sin2, ret_g)


def _routing(logits):
    lane = lax.broadcasted_iota(jnp.int32, logits.shape, 1).astype(jnp.float32)
    big = float(LANES)
    neg = -jnp.inf

    def first_argmax(v, vmax):
        return jnp.min(jnp.where(v == vmax, lane, big), axis=-1, keepdims=True)

    g_mask = (lane >= N_EXPERTS) & (lane < N_EXPERTS + N_GROUPS)
    gl = jnp.where(g_mask, logits, neg)
    g_max = jnp.max(gl, axis=-1, keepdims=True)
    g_sel = first_argmax(gl, g_max) - N_EXPERTS
    g_gate = 1.0 / jnp.sum(jnp.where(g_mask, jnp.exp(gl - g_max), 0.0), axis=-1, keepdims=True)

    e_lo = g_sel * EXPERTS_PER_GROUP
    el = jnp.where((lane >= e_lo) & (lane < e_lo + EXPERTS_PER_GROUP), logits, neg)
    v1 = jnp.max(el, axis=-1, keepdims=True)
    i1 = first_argmax(el, v1)
    el2 = jnp.where(lane == i1, neg, el)
    v2 = jnp.max(el2, axis=-1, keepdims=True)
    i2 = first_argmax(el2, v2)
    e2 = jnp.exp(v2 - v1)
    denom = 1.0 + e2
    w1 = (1.0 / denom) * g_gate
    w2 = (e2 / denom) * g_gate
    route = jnp.where(lane == ROUTE_E1, i1, 0.0) + jnp.where(lane == ROUTE_E2, i2, 0.0)
    return route + jnp.where(lane == ROUTE_W1, w1, 0.0) + jnp.where(lane == ROUTE_W2, w2, 0.0)


def _norm2(x1, g):
    ms = jnp.mean(x1 * x1, axis=-1, keepdims=True)
    return (x1 * lax.rsqrt(ms + EPS)) * g


OUT_SUBTILES = 2


def _out_proj_kernel(a_ref, r_ref, x_ref, wa_ref, wr_ref, g_ref, rcat_ref,
                     x1_ref, h2_ref, route_ref):
    sub = x_ref.shape[0] // OUT_SUBTILES
    for s in range(OUT_SUBTILES):
        rows = slice(s * sub, (s + 1) * sub)
        mixed = _dot(a_ref[rows, :], wa_ref[...]) + _dot(r_ref[rows, :], wr_ref[...])
        x1 = x_ref[rows, :] + mixed
        x1_ref[rows, :] = x1
        h2 = _norm2(x1, g_ref[...])
        h2_ref[rows, :] = h2
        hi = h2.astype(jnp.bfloat16)
        lo = (h2 - hi.astype(jnp.float32)).astype(jnp.bfloat16)
        both = _dot(hi, rcat_ref[...])
        logits = both[:, :LANES] + (both[:, LANES:] + _dot(lo, rcat_ref[:, :LANES]))
        route_ref[rows, :] = _routing(logits)


def _out_proj(attn2d, ret2d, x2d, w_out_bf, g2, r_cat, *, tm):
    n, d = x2d.shape
    return pl.pallas_call(
        _out_proj_kernel,
        out_shape=(
            jax.ShapeDtypeStruct((n, d), jnp.float32),
            jax.ShapeDtypeStruct((n, d), jnp.float32),
            jax.ShapeDtypeStruct((n, LANES), jnp.float32),
        ),
        grid=(n // tm,),
        in_specs=[
            pl.BlockSpec((tm, ATTN_WIDTH), lambda i: (i, 0)),
            pl.BlockSpec((tm, RET_WIDTH), lambda i: (i, 0)),
            pl.BlockSpec((tm, d), lambda i: (i, 0)),
            pl.BlockSpec((ATTN_WIDTH, d), lambda i: (0, 0)),
            pl.BlockSpec((RET_WIDTH, d), lambda i: (ATTN_WIDTH // RET_WIDTH, 0)),
            pl.BlockSpec((1, d), lambda i: (0, 0)),
            pl.BlockSpec((d, 2 * LANES), lambda i: (0, 0)),
        ],
        out_specs=(
            pl.BlockSpec((tm, d), lambda i: (i, 0)),
            pl.BlockSpec((tm, d), lambda i: (i, 0)),
            pl.BlockSpec((tm, LANES), lambda i: (i, 0)),
        ),
        compiler_params=_params("arbitrary"),
        name="out_proj",
    )(attn2d, ret2d, x2d, w_out_bf, w_out_bf, g2, r_cat)


def _plan_kernel(route_ref, pos_ref, cnt_ref, rank_scr, *, tm, blk):
    n = route_ref.shape[0]
    lane = lax.broadcasted_iota(jnp.int32, (blk, LANES), 1).astype(jnp.float32)
    before = (lax.broadcasted_iota(jnp.int32, (blk, blk), 1)
              < lax.broadcasted_iota(jnp.int32, (blk, blk), 0)).astype(jnp.bfloat16)

    def one_hot(rows):
        r = route_ref[rows, :]
        e1 = r[:, ROUTE_E1:ROUTE_E1 + 1]
        e2 = r[:, ROUTE_E2:ROUTE_E2 + 1]
        return lane == e1, lane == e2

    def rank_body(b, run):
        rows = pl.ds(pl.multiple_of(b * blk, blk), blk)
        m1, m2 = one_hot(rows)
        sel = jnp.where(m1 | m2, 1.0, 0.0)
        rank_scr[rows, :] = _dot(before, sel.astype(jnp.bfloat16)) + run
        return run + jnp.sum(sel, axis=0, keepdims=True)
    cnt = lax.fori_loop(0, n // blk, rank_body, jnp.zeros((1, LANES), jnp.float32), unroll=2)
    cnt_ref[...] = jnp.broadcast_to(cnt, cnt_ref.shape)

    tiles = jnp.floor((cnt + (tm - 1.0)) * (1.0 / tm))
    below = (lax.broadcasted_iota(jnp.int32, (LANES, LANES), 0)
             < lax.broadcasted_iota(jnp.int32, (LANES, LANES), 1)).astype(jnp.bfloat16)
    start = _dot(jnp.broadcast_to(tiles, (8, LANES)).astype(jnp.bfloat16), below)[0:1, :] * float(tm)

    def pos_body(b, carry):
        rows = pl.ds(pl.multiple_of(b * blk, blk), blk)
        m1, m2 = one_hot(rows)
        dest = rank_scr[rows, :] + start
        p1 = jnp.sum(jnp.where(m1, dest, 0.0), axis=-1, keepdims=True)
        p2 = jnp.sum(jnp.where(m2, dest, 0.0), axis=-1, keepdims=True)
        pos_ref[rows, :] = (jnp.where(lane == 0.0, p1, 0.0) + jnp.where(lane == 1.0, p2, 0.0)).astype(jnp.int32)
        return carry
    lax.fori_loop(0, n // blk, pos_body, 0, unroll=2)


def _moe_plan(route, *, tm, blk):
    n = route.shape[0]
    return pl.pallas_call(
        functools.partial(_plan_kernel, tm=tm, blk=blk),
        out_shape=(jax.ShapeDtypeStruct((n, LANES), jnp.int32),
                   jax.ShapeDtypeStruct((8, LANES), jnp.float32)),
        scratch_shapes=[pltpu.VMEM((n, LANES), jnp.float32)],
        compiler_params=pltpu.CompilerParams(vmem_limit_bytes=VMEM_LIMIT),
        name="moe_plan",
    )(route)


def _row_copy(src, src_row, dst, dst_row, sem):
    return pltpu.make_async_copy(src.at[pl.ds(src_row, 1), :], dst.at[pl.ds(dst_row, 1), :], sem)


def _invert_kernel(pos_ref, sid_ref):
    def body(j, carry):
        sid_ref[pos_ref[j]] = j
        return carry
    lax.fori_loop(0, pos_ref.shape[0], body, 0, unroll=8)


def _moe_invert(pos, *, rows):
    return pl.pallas_call(
        _invert_kernel,
        out_shape=jax.ShapeDtypeStruct((rows,), jnp.int32),
        in_specs=[pl.BlockSpec(memory_space=pltpu.SMEM)],
        out_specs=pl.BlockSpec(memory_space=pltpu.SMEM),
        name="moe_invert",
    )(pos)


ROW_GROUP = 8
NO_NEXT, NOT_FIRST = -1, -2


def _ffn_kernel(te_ref, tv_ref, nu_ref, seg_ref, nexte_ref, sid_ref, tok_ref,
                h2_ref, w1_ref, w3_ref, w2_ref, y2_ref,
                xbuf, ybuf, w1buf, w3buf, w2buf, gsem, ssem, wsem):
    t = pl.program_id(0)
    n_used = nu_ref[0]
    tm = xbuf.shape[1]

    def weight_copies(e, wslot):
        return [pltpu.make_async_copy(src.at[e], dst.at[wslot], wsem.at[wslot])
                for src, dst in ((w1_ref, w1buf), (w3_ref, w3buf), (w2_ref, w2buf))]

    def gather(slot):
        return tok_ref, lambda r, tok: _row_copy(h2_ref, tok, xbuf.at[slot], r, gsem.at[slot])

    def scatter(slot):
        return sid_ref, lambda r, sid: _row_copy(ybuf.at[slot], r, y2_ref, sid, ssem.at[slot])

    def start_partial(tile, table, copy):
        valid = tv_ref[tile]

        def body(c, carry):
            for u in range(ROW_GROUP):
                r = c * ROW_GROUP + u

                @pl.when(r < valid)
                def _():
                    copy(r, table[tile * tm + r]).start()
            return carry
        lax.fori_loop(0, (valid + (ROW_GROUP - 1)) // ROW_GROUP, body, 0)

    def start_all(tile, table, copy):
        @pl.when(tv_ref[tile] == tm)
        def _():
            for r in range(tm):
                copy(r, table[tile * tm + r]).start()

        @pl.when(tv_ref[tile] < tm)
        def _():
            start_partial(tile, table, copy)

    def wait_all(tile, copy, tile_copy):
        @pl.when(tv_ref[tile] == tm)
        def _():
            tile_copy.wait()

        @pl.when(tv_ref[tile] < tm)
        def _():
            def body(r, carry):
                copy(0, 0).wait()
                return carry
            lax.fori_loop(0, tv_ref[tile], body, 0)

    def wait_gather(tile, slot):
        wait_all(tile, gather(slot)[1],
                 pltpu.make_async_copy(h2_ref.at[pl.ds(0, tm), :], xbuf.at[slot], gsem.at[slot]))

    def wait_scatter(tile, slot):
        wait_all(tile, scatter(slot)[1],
                 pltpu.make_async_copy(ybuf.at[slot], y2_ref.at[pl.ds(0, tm), :], ssem.at[slot]))

    @pl.when(t == 0)
    def _():
        for cp in weight_copies(te_ref[0], 0):
            cp.start()
        start_partial(0, *gather(0))

    @pl.when(t < n_used)
    def _():
        slot = t % 2
        wslot = seg_ref[t] % 2

        @pl.when(t + 1 < n_used)
        def _():
            start_all(t + 1, *gather(1 - slot))

        @pl.when(nexte_ref[t] != NOT_FIRST)
        def _():
            for cp in weight_copies(te_ref[t], wslot):
                cp.wait()

            @pl.when(nexte_ref[t] >= 0)
            def _():
                for cp in weight_copies(nexte_ref[t], 1 - wslot):
                    cp.start()

        wait_gather(t, slot)

        @pl.when(t >= 2)
        def _():
            wait_scatter(t - 2, slot)

        row = lax.broadcasted_iota(jnp.int32, xbuf.shape[1:], 0)
        x = jnp.where(row < tv_ref[t], xbuf[slot], 0.0).astype(jnp.bfloat16)
        a = _dot(x, w1buf[wslot].astype(jnp.bfloat16))
        b = _dot(x, w3buf[wslot].astype(jnp.bfloat16))
        act = (a * (1.0 / (1.0 + jnp.exp(-a)))) * b
        ybuf[slot] = _dot(act.astype(jnp.bfloat16), w2buf[wslot].astype(jnp.bfloat16))
        start_all(t, *scatter(slot))

    @pl.when(t == pl.num_programs(0) - 1)
    def _():
        @pl.when(n_used >= 2)
        def _():
            wait_scatter(n_used - 2, n_used % 2)
        wait_scatter(n_used - 1, (n_used - 1) % 2)


def _moe_ffn(tile_expert, tile_valid, n_used, tile_seg, tile_next, row_sid, row_tok, h2, w1, w3, w2,
             *, tm):
    n, d = h2.shape
    _, _, ff = w1.shape
    max_tiles = tile_expert.shape[0]
    any_spec = pl.BlockSpec(memory_space=pl.ANY)
    return pl.pallas_call(
        _ffn_kernel,
        out_shape=jax.ShapeDtypeStruct((2 * n, d), jnp.float32),
        grid_spec=pltpu.PrefetchScalarGridSpec(
            num_scalar_prefetch=7,
            grid=(max_tiles,),
            in_specs=[any_spec, any_spec, any_spec, any_spec],
            out_specs=any_spec,
            scratch_shapes=[
                pltpu.VMEM((2, tm, d), jnp.float32),
                pltpu.VMEM((2, tm, d), jnp.float32),
                pltpu.VMEM((2, d, ff), jnp.float32),
                pltpu.VMEM((2, d, ff), jnp.float32),
                pltpu.VMEM((2, ff, d), jnp.float32),
                pltpu.SemaphoreType.DMA((2,)),
                pltpu.SemaphoreType.DMA((2,)),
                pltpu.SemaphoreType.DMA((2,)),
            ],
        ),
        compiler_params=_params("arbitrary"),
        name="moe_ffn",
    )(tile_expert, tile_valid, n_used, tile_seg, tile_next, row_sid, row_tok, h2, w1, w3, w2)


def _combine_kernel(x1_ref, route_ref, ya_ref, yb_ref, o_ref):
    route = route_ref[...]
    w1 = route[:, ROUTE_W1:ROUTE_W1 + 1]
    w2 = route[:, ROUTE_W2:ROUTE_W2 + 1]
    o_ref[...] = x1_ref[...] + (w1 * ya_ref[...] + w2 * yb_ref[...])


def _moe_combine(x1, route, y2, *, tm):
    n, d = x1.shape
    return pl.pallas_call(
        _combine_kernel,
        out_shape=jax.ShapeDtypeStruct((n, d), jnp.float32),
        grid=(n // tm,),
        in_specs=[pl.BlockSpec((tm, d), lambda i: (i, 0)),
                  pl.BlockSpec((tm, LANES), lambda i: (i, 0)),
                  pl.BlockSpec((tm, d), lambda i: (i, 0)),
                  pl.BlockSpec((tm, d), lambda i: (i + n // tm, 0))],
        out_specs=pl.BlockSpec((tm, d), lambda i: (i, 0)),
        compiler_params=_params("arbitrary"),
        name="moe_combine",
    )(x1, route, y2, y2)


def _moe(x1, h2, route, w1, w3, w2, *, tm, gather_tm):
    n, d = x1.shape
    max_tiles = (2 * n) // tm + N_EXPERTS
    pos2d, cnt = _moe_plan(route, tm=tm, blk=gather_tm)
    pos = pos2d[:, :2].T.reshape(2 * n)
    counts = cnt[0, :N_EXPERTS].astype(jnp.int32)
    tiles = (counts + (tm - 1)) // tm
    ends = jnp.cumsum(tiles)
    t_idx = jnp.arange(max_tiles, dtype=jnp.int32)
    tile_expert = jnp.sum((ends[None, :] <= t_idx[:, None]).astype(jnp.int32), axis=1)
    tile_expert = jnp.minimum(tile_expert, N_EXPERTS - 1)
    first_tile = (ends - tiles)[tile_expert]
    tile_valid = jnp.clip(counts[tile_expert] - (t_idx - first_tile) * tm, 0, tm).astype(jnp.int32)
    n_used = ends[-1:].astype(jnp.int32)
    used = t_idx < n_used[0]
    is_first = used & (t_idx == first_tile)
    tile_seg = (jnp.cumsum(is_first.astype(jnp.int32)) - 1).astype(jnp.int32)
    next_start = first_tile + tiles[tile_expert]
    next_e = jnp.where(next_start < n_used[0], tile_expert[jnp.minimum(next_start, max_tiles - 1)], NO_NEXT)
    tile_next = jnp.where(is_first, next_e, NOT_FIRST).astype(jnp.int32)

    row_sid = _moe_invert(pos, rows=max_tiles * tm)
    row_tok = jnp.where(row_sid >= n, row_sid - n, row_sid)
    y2 = _moe_ffn(tile_expert, tile_valid, n_used, tile_seg, tile_next, row_sid, row_tok, h2, w1, w3, w2,
                  tm=tm)
    return _moe_combine(x1, route, y2, tm=gather_tm)


def _pack_router(w_group, w_router):
    d = w_group.shape[0]
    experts = jnp.transpose(w_router, (1, 0, 2)).reshape(d, N_EXPERTS)
    wr = jnp.concatenate(
        [experts, w_group, jnp.zeros((d, LANES - N_EXPERTS - N_GROUPS), w_group.dtype)], axis=1)
    hi = wr.astype(jnp.bfloat16)
    lo = (wr - hi.astype(jnp.float32)).astype(jnp.bfloat16)
    return jnp.concatenate([hi, lo], axis=1)


def _rotation_tables(seq):
    half = RET_DK // 2
    pos = jnp.arange(seq, dtype=jnp.float32)
    inv = 1.0 / (ROT_BASE ** jnp.linspace(0.0, 1.0, half, dtype=jnp.float32))
    ang = pos[:, None] * inv[None, :]
    c, s = jnp.cos(ang), jnp.sin(ang)
    return jnp.concatenate([c, c], axis=-1), jnp.concatenate([-s, s], axis=-1)


def _tiles(n, seq):
    def fit(total, want):
        t = min(total, want)
        while total % t:
            t //= 2
        return t
    return dict(
        proj_tm=fit(n, 1024),
        attn_tq=fit(seq, 256), attn_tk=fit(seq, 256),
        ret_rc=fit(seq, 256),
        out_tm=fit(n, 512),
        moe_tm=fit(n, 256), moe_gather_tm=fit(n, 256),
    )


def kernel(x, norm1_g, w_in, q_norm_g, k_norm_g, idx_k_ln_w, idx_k_ln_b, ret_norm_g,
           w_out, norm2_g, w_group, w_router, w1, w3, w2):
    b, seq, d = x.shape
    n = b * seq
    depth = w_in.shape[0]
    t = _tiles(n, seq)
    cos2, sin2 = _rotation_tables(seq)
    log_gamma = jnp.log1p(-jnp.exp2(-5.0 - jnp.arange(RET_HEADS, dtype=jnp.float32)))

    x2d = x.reshape(n, d)
    for l in range(depth):
        proj = _in_proj(x2d, norm1_g[l][None, :], w_in[l].T, tm=t["proj_tm"])
        p3 = proj.reshape(b, seq, proj.shape[1])
        attn = _dsa_attention(p3, q_norm_g[l][None, :], k_norm_g[l][None, :],
                              idx_k_ln_w[l][None, :], idx_k_ln_b[l][None, :],
                              tq=t["attn_tq"], tk=t["attn_tk"])
        ret = _retention(p3, log_gamma, cos2, sin2, ret_norm_g[l].reshape(RET_HEADS, 1, RET_DV),
                         rc=t["ret_rc"])
        r_cat = _pack_router(w_group[l], w_router[l])
        g2 = norm2_g[l][None, :]
        x1, h2, route = _out_proj(attn.reshape(n, ATTN_WIDTH), ret.reshape(n, RET_WIDTH), x2d,
                                  w_out[l].astype(jnp.bfloat16), g2, r_cat, tm=t["out_tm"])
        x2d = _moe(x1, h2, route, w1[l], w3[l], w2[l], tm=t["moe_tm"], gather_tm=t["moe_gather_tm"])
    return x2d.reshape(b, seq, d)
```

```python
import functools
import math

import jax
import jax.numpy as jnp
from jax import lax
from jax.experimental import pallas as pl
from jax.experimental.pallas import tpu as pltpu

CHUNK = 64
ATTN_HEADS = 8
HEAD_DIM = 128
KV_HEADS = 2
HEADS_PER_KV = ATTN_HEADS // KV_HEADS
IDX_HEADS = 16
IDX_DIM = 64
TOPK_MAX = 256
RET_HEADS = 8
RET_DK = 128
RET_DV = 128
ROT_BASE = 10000.0
N_GROUPS = 4
EXPERTS_PER_GROUP = 8
N_EXPERTS = N_GROUPS * EXPERTS_PER_GROUP
EPS = 1e-6

ATTN_WIDTH = ATTN_HEADS * HEAD_DIM
KV_WIDTH = KV_HEADS * HEAD_DIM
IDX_WIDTH = IDX_HEADS * IDX_DIM
RET_WIDTH = RET_HEADS * RET_DK

LANES = 128
VMEM_LIMIT = 56 * 1024 * 1024

AQ_OFF = 0
AK_OFF = AQ_OFF + ATTN_WIDTH
AV_OFF = AK_OFF + KV_WIDTH
IQ_OFF = AV_OFF + KV_WIDTH
IK_OFF = IQ_OFF + IDX_WIDTH
IW_OFF = IK_OFF + IDX_DIM
W_RET = IW_OFF + IDX_HEADS
IN_WIDTH = W_RET + 4 * RET_WIDTH
assert IW_OFF // LANES == IK_OFF // LANES
PROJ_TN = 1024
RQ_OFF = -(-W_RET // PROJ_TN) * PROJ_TN
RK_OFF = RQ_OFF + RET_WIDTH
RV_OFF = RK_OFF + RET_WIDTH
RG_OFF = RV_OFF + RET_WIDTH
PROJ_WIDTH = RG_OFF + RET_WIDTH

ROUTE_E1, ROUTE_E2, ROUTE_W1, ROUTE_W2 = 0, 1, 2, 3

SUM_ROWS = 16

INT_MIN = -(2 ** 31)
NEG_BIG = -1e30

_NT = (((1,), (1,)), ((), ()))


def _dot(a, b):
    return jnp.dot(a, b, preferred_element_type=jnp.float32)


def _dot_nt(a, b):
    return lax.dot_general(a, b, _NT, preferred_element_type=jnp.float32)


def _params(*sem):
    return pltpu.CompilerParams(dimension_semantics=sem, vmem_limit_bytes=VMEM_LIMIT)


def _in_proj_kernel(x_ref, g_ref, wt_ref, o_ref, h_scr, *, row_chunk):
    @pl.when(pl.program_id(1) == 0)
    def _():
        def body(c, carry):
            rows = pl.ds(pl.multiple_of(c * row_chunk, row_chunk), row_chunk)
            x = x_ref[rows, :]
            ms = jnp.mean(x * x, axis=-1, keepdims=True)
            h_scr[rows, :] = ((x * lax.rsqrt(ms + EPS)) * g_ref[...]).astype(jnp.bfloat16)
            return carry
        lax.fori_loop(0, x_ref.shape[0] // row_chunk, body, 0)

    o_ref[...] = _dot_nt(h_scr[...], wt_ref[...].astype(jnp.bfloat16)).astype(o_ref.dtype)


def _in_proj(x2d, g, w_in_t, *, tm):
    n, d = x2d.shape
    tn = PROJ_TN
    assert w_in_t.shape == (IN_WIDTH, d) and W_RET % 8 == 0
    attn_tiles = RQ_OFF // tn

    def window(i, j):
        start8 = jnp.where(j < attn_tiles, j * (tn // 8), W_RET // 8 + (j - attn_tiles) * (tn // 8))
        return 8 * start8, 0

    return pl.pallas_call(
        functools.partial(_in_proj_kernel, row_chunk=min(tm, 128)),
        out_shape=jax.ShapeDtypeStruct((n, PROJ_WIDTH), jnp.bfloat16),
        grid=(n // tm, PROJ_WIDTH // tn),
        in_specs=[
            pl.BlockSpec((tm, d), lambda i, j: (i, 0)),
            pl.BlockSpec((1, d), lambda i, j: (0, 0)),
            pl.BlockSpec((pl.Element(tn), pl.Element(d)), window),
        ],
        out_specs=pl.BlockSpec((tm, tn), lambda i, j: (i, j)),
        scratch_shapes=[pltpu.VMEM((tm, d), jnp.bfloat16)],
        compiler_params=_params("arbitrary", "arbitrary"),
        name="in_proj",
    )(x2d, g, w_in_t)


def _ordered_float(v):
    bits = v ^ ((v >> 31) & jnp.int32(0x7FFFFFFF))
    return pltpu.bitcast(bits, jnp.float32)


def _attn_kernel(aq_ref, iqa_ref, iqb_ref, iw_ref, ak_ref, av_ref, ik_ref, qg_ref, kg_ref, lnw_ref,
                 lnb_ref, o_ref,
                 kn_scr, ikn_scr, vt_scr, key_scr, wt_scr, qn_scr, acc_scr, s_scr,
                 *, tk, topk, idx_w_scale):
    i = pl.program_id(1)
    seq = ak_ref.shape[1]
    tq = aq_ref.shape[1]
    chunk_shift = CHUNK.bit_length() - 1

    @pl.when(i == 0)
    def _():
        def body(c, carry):
            rows = pl.ds(pl.multiple_of(c * tk, tk), tk)
            for g in range(KV_HEADS):
                cols = slice(g * HEAD_DIM, (g + 1) * HEAD_DIM)
                k = ak_ref[0, rows, cols].astype(jnp.float32)
                ms = jnp.mean(k * k, axis=-1, keepdims=True)
                kn_scr[rows, cols] = ((k * lax.rsqrt(ms + EPS)) * kg_ref[...]).astype(jnp.bfloat16)
                v = av_ref[0, rows, cols].astype(jnp.float32)
                vt_scr[g, c, :HEAD_DIM, :] = v.T.astype(jnp.bfloat16)
                vt_scr[g, c, HEAD_DIM:, :] = jnp.ones((SUM_ROWS, tk), jnp.bfloat16)
            ki = ik_ref[0, rows, :IDX_DIM].astype(jnp.float32)
            mu = jnp.mean(ki, axis=-1, keepdims=True)
            var = jnp.mean(jnp.square(ki - mu), axis=-1, keepdims=True)
            y = ((ki - mu) * lax.rsqrt(var + EPS) * lnw_ref[...] + lnb_ref[...]).astype(jnp.bfloat16)
            zeros = jnp.zeros_like(y)
            ikn_scr[0, rows, :] = jnp.concatenate([y, zeros], axis=1)
            ikn_scr[1, rows, :] = jnp.concatenate([zeros, y], axis=1)
            return carry
        lax.fori_loop(0, seq // tk, body, 0)

    t0 = i * tq
    n_kt = (t0 + tq) // tk
    scale = (HEAD_DIM ** -0.5) * math.log2(math.e)
    for h in range(ATTN_HEADS):
        g, r = divmod(h, HEADS_PER_KV)
        q = aq_ref[0, :, h * HEAD_DIM:(h + 1) * HEAD_DIM].astype(jnp.float32)
        ms = jnp.mean(q * q, axis=-1, keepdims=True)
        qn_scr[g, r * tq:(r + 1) * tq, :] = (
            (q * lax.rsqrt(ms + EPS)) * qg_ref[...] * scale).astype(jnp.bfloat16)
    wt_scr[...] = iw_ref[0].astype(jnp.float32).T * idx_w_scale
    w_row = IW_OFF % LANES

    q_chunk = (t0 + lax.broadcasted_iota(jnp.int32, (tk, tq), 1)) >> chunk_shift

    def score_body(kt, carry):
        rows = pl.ds(pl.multiple_of(kt * tk, tk), tk)
        ik_first, ik_second = ikn_scr[0, rows, :], ikn_scr[1, rows, :]
        acc = jnp.zeros((tk, tq), jnp.float32)
        pairs_per_ref = iqa_ref.shape[2] // LANES
        for pair in range(IDX_HEADS // 2):
            src = iqa_ref if pair < pairs_per_ref else iqb_ref
            lane0 = (pair % pairs_per_ref) * LANES
            q_pair = src[0, :, lane0:lane0 + LANES]
            for sub, ik_t in enumerate((ik_first, ik_second)):
                h = 2 * pair + sub
                d = _dot_nt(ik_t, q_pair)
                acc = acc + jnp.maximum(d, 0.0) * wt_scr[w_row + h:w_row + h + 1, :]
        k_chunk = (kt * tk + lax.broadcasted_iota(jnp.int32, (tk, tq), 0)) >> chunk_shift
        key_scr[rows, :] = jnp.where(k_chunk <= q_chunk, acc, -jnp.inf)
        return carry
    lax.fori_loop(0, n_kt, score_body, 0)

    def bit_body(it, lo):
        cand = lo + lax.shift_left(jnp.int32(1), 31 - it)
        cand_f = _ordered_float(cand)

        def count_body(kt, part):
            rows = pl.ds(pl.multiple_of(kt * tk, tk), tk)
            hit = jnp.where(key_scr[rows, :] >= cand_f, 1.0, 0.0).reshape(tk // 8, 8, tq)
            while hit.shape[0] > 1:
                half = hit.shape[0] // 2
                hit = hit[:half] + hit[half:]
            return part + hit[0]
        part = lax.fori_loop(0, n_kt, count_body, jnp.zeros((8, tq), jnp.float32))
        cnt = jnp.sum(part, axis=0, keepdims=True)
        return jnp.where(cnt >= float(topk), cand, lo)
    lo = lax.fori_loop(0, 32, bit_body, jnp.full((1, tq), INT_MIN, jnp.int32))
    thr = jnp.where(lo == INT_MIN, jnp.finfo(jnp.float32).min, _ordered_float(lo))

    def logit_body(kt, m):
        rows = pl.ds(pl.multiple_of(kt * tk, tk), tk)
        bias = jnp.where(key_scr[rows, :] >= thr, 0.0, NEG_BIG)
        bias = jnp.concatenate([bias] * HEADS_PER_KV, axis=1)
        new_m = []
        for g in range(KV_HEADS):
            s = _dot_nt(kn_scr[rows, g * HEAD_DIM:(g + 1) * HEAD_DIM], qn_scr[g]) + bias
            s_scr[g, rows, :] = s
            new_m.append(jnp.maximum(m[g], jnp.max(s, axis=0, keepdims=True)))
        return tuple(new_m)
    m0 = jnp.full((1, HEADS_PER_KV * tq), NEG_BIG, jnp.float32)
    m = lax.fori_loop(0, n_kt, logit_body, (m0,) * KV_HEADS)

    acc_scr[...] = jnp.zeros(acc_scr.shape, jnp.float32)

    def pv_body(kt, carry):
        rows = pl.ds(pl.multiple_of(kt * tk, tk), tk)
        for g in range(KV_HEADS):
            p = jnp.exp2(s_scr[g, rows, :] - m[g]).astype(jnp.bfloat16)
            acc_scr[g] += _dot(vt_scr[g, kt], p)
        return carry
    lax.fori_loop(0, n_kt, pv_body, 0)

    for h in range(ATTN_HEADS):
        g, r = divmod(h, HEADS_PER_KV)
        cols = slice(r * tq, (r + 1) * tq)
        o = acc_scr[g, :HEAD_DIM, cols] / acc_scr[g, HEAD_DIM:HEAD_DIM + 1, cols]
        o_ref[0, :, h * HEAD_DIM:(h + 1) * HEAD_DIM] = o.T.astype(o_ref.dtype)


def _dsa_attention(p3, q_g, k_g, ln_w, ln_b, *, tq, tk):
    b, seq, _ = p3.shape
    topk = min(TOPK_MAX, seq // 4)
    idx_w_scale = (IDX_HEADS ** -0.5) * (IDX_DIM ** -0.5)
    assert seq % tq == 0 and tq % tk == 0 and tk % CHUNK == 0

    def col(off, width):
        assert off % width == 0 or width == LANES
        return off // width

    half_iq = IDX_WIDTH // 2
    return pl.pallas_call(
        functools.partial(_attn_kernel, tk=tk, topk=topk, idx_w_scale=idx_w_scale),
        out_shape=jax.ShapeDtypeStruct((b, seq, ATTN_WIDTH), jnp.bfloat16),
        grid=(b, seq // tq),
        in_specs=[
            pl.BlockSpec((1, tq, ATTN_WIDTH), lambda bi, i: (bi, i, col(AQ_OFF, ATTN_WIDTH))),
            pl.BlockSpec((1, tq, half_iq), lambda bi, i: (bi, i, col(IQ_OFF, half_iq))),
            pl.BlockSpec((1, tq, half_iq), lambda bi, i: (bi, i, col(IQ_OFF, half_iq) + 1)),
            pl.BlockSpec((1, tq, LANES), lambda bi, i: (bi, i, col(IW_OFF, LANES))),
            pl.BlockSpec((1, seq, KV_WIDTH), lambda bi, i: (bi, 0, col(AK_OFF, KV_WIDTH))),
            pl.BlockSpec((1, seq, KV_WIDTH), lambda bi, i: (bi, 0, col(AV_OFF, KV_WIDTH))),
            pl.BlockSpec((1, seq, LANES), lambda bi, i: (bi, 0, col(IK_OFF, LANES))),
            pl.BlockSpec((1, HEAD_DIM), lambda bi, i: (0, 0)),
            pl.BlockSpec((1, HEAD_DIM), lambda bi, i: (0, 0)),
            pl.BlockSpec((1, IDX_DIM), lambda bi, i: (0, 0)),
            pl.BlockSpec((1, IDX_DIM), lambda bi, i: (0, 0)),
        ],
        out_specs=pl.BlockSpec((1, tq, ATTN_WIDTH), lambda bi, i: (bi, i, 0)),
        scratch_shapes=[
            pltpu.VMEM((seq, KV_WIDTH), jnp.bfloat16),
            pltpu.VMEM((2, seq, 2 * IDX_DIM), jnp.bfloat16),
            pltpu.VMEM((KV_HEADS, seq // tk, HEAD_DIM + SUM_ROWS, tk), jnp.bfloat16),
            pltpu.VMEM((seq, tq), jnp.float32),
            pltpu.VMEM((LANES, tq), jnp.float32),
            pltpu.VMEM((KV_HEADS, HEADS_PER_KV * tq, HEAD_DIM), jnp.bfloat16),
            pltpu.VMEM((KV_HEADS, HEAD_DIM + SUM_ROWS, HEADS_PER_KV * tq), jnp.float32),
            pltpu.VMEM((KV_HEADS, seq, HEADS_PER_KV * tq), jnp.float32),
        ],
        compiler_params=_params("arbitrary", "arbitrary"),
        name="dsa_attn",
    )(p3, p3, p3, p3, p3, p3, p3, q_g, k_g, ln_w, ln_b)


RET_HEADS_PER_STEP = 2


def _ret_kernel(lg_ref, rq_ref, rk_ref, rv_ref, rg_ref, cos_ref, sin_ref, g_ref, o_ref, *, rc):
    seq = rq_ref.shape[1]
    n = lax.broadcasted_iota(jnp.int32, (rc, RET_DV), 0).astype(jnp.float32)
    rel = (lax.broadcasted_iota(jnp.int32, (rc, rc), 0)
           - lax.broadcasted_iota(jnp.int32, (rc, rc), 1)).astype(jnp.float32)

    def rot(x, rows):
        return x * cos_ref[rows, :] + pltpu.roll(x, RET_DK // 2, 1) * sin_ref[rows, :]

    heads = []
    for hh in range(RET_HEADS_PER_STEP):
        lg = lg_ref[pl.program_id(1) * RET_HEADS_PER_STEP + hh]
        heads.append(dict(
            cols=slice(hh * RET_DK, (hh + 1) * RET_DK),
            cross_decay=jnp.exp(lg * (n + 1.0)),
            state_decay=jnp.exp(lg * (rc - 1.0 - n)),
            chunk_decay=jnp.exp(lg * jnp.full((RET_DK, RET_DV), float(rc), jnp.float32)),
            intra=jnp.where(rel >= 0, jnp.exp(lg * jnp.maximum(rel, 0.0)), 0.0),
            state=jnp.zeros((RET_DK, RET_DV), jnp.float32),
            gain=g_ref[hh],
        ))

    for c in range(seq // rc):
        rows = slice(c * rc, (c + 1) * rc)
        for hd in heads:
            cols = hd["cols"]
            q = rot(rq_ref[0, rows, cols].astype(jnp.float32), rows)
            k = rot(rk_ref[0, rows, cols].astype(jnp.float32), rows) * (RET_DK ** -0.5)
            v = rv_ref[0, rows, cols]
            qb = q.astype(jnp.bfloat16)
            inner = _dot_nt(qb, k.astype(jnp.bfloat16)) * hd["intra"]
            o = (_dot(inner.astype(jnp.bfloat16), v)
                 + _dot(qb, hd["state"].astype(jnp.bfloat16)) * hd["cross_decay"])
            kd_t = (k * hd["state_decay"]).T.astype(jnp.bfloat16)
            hd["state"] = hd["state"] * hd["chunk_decay"] + _dot(kd_t, v)
            ms = jnp.mean(o * o, axis=-1, keepdims=True)
            y = (o * lax.rsqrt(ms + EPS)) * hd["gain"]
            gate = rg_ref[0, rows, cols].astype(jnp.float32)
            o_ref[0, rows, cols] = ((gate * (1.0 / (1.0 + jnp.exp(-gate)))) * y).astype(o_ref.dtype)


def _retention(p3, log_gamma, cos2, sin2, ret_g, *, rc):
    b, seq, _ = p3.shape
    hps = RET_HEADS_PER_STEP
    width = hps * RET_DK
    assert seq % rc == 0 and RET_HEADS % hps == 0

    def head_spec(off):
        assert off % width == 0
        return pl.BlockSpec((1, seq, width), lambda bi, h: (bi, 0, off // width + h))

    return pl.pallas_call(
        functools.partial(_ret_kernel, rc=rc),
        out_shape=jax.ShapeDtypeStruct((b, seq, RET_WIDTH), jnp.bfloat16),
        grid=(b, RET_HEADS // hps),
        in_specs=[
            pl.BlockSpec(memory_space=pltpu.SMEM),
            head_spec(RQ_OFF), head_spec(RK_OFF), head_spec(RV_OFF), head_spec(RG_OFF),
            pl.BlockSpec((seq, RET_DK), lambda bi, h: (0, 0)),
            pl.BlockSpec((seq, RET_DK), lambda bi, h: (0, 0)),
            pl.BlockSpec((hps, 1, RET_DV), lambda bi, h: (h, 0, 0)),
        ],
        out_specs=pl.BlockSpec((1, seq, width), lambda bi, h: (bi, 0, h)),
        compiler_params=_params("arbitrary", "arbitrary"),
        name="retention",
    )(log_gamma, p3, p3, p3, p3, cos2, sin2, ret_g)


def _routing(logits):
    lane = lax.broadcasted_iota(jnp.int32, logits.shape, 1).astype(jnp.float32)
    big = float(LANES)
    neg = -jnp.inf

    def first_argmax(v, vmax):
        return jnp.min(jnp.where(v == vmax, lane, big), axis=-1, keepdims=True)

    g_mask = (lane >= N_EXPERTS) & (lane < N_EXPERTS + N_GROUPS)
    gl = jnp.where(g_mask, logits, neg)
    g_max = jnp.max(gl, axis=-1, keepdims=True)
    g_sel = first_argmax(gl, g_max) - N_EXPERTS
    g_gate = 1.0 / jnp.sum(jnp.where(g_mask, jnp.exp(gl - g_max), 0.0), axis=-1, keepdims=True)

    e_lo = g_sel * EXPERTS_PER_GROUP
    el = jnp.where((lane >= e_lo) & (lane < e_lo + EXPERTS_PER_GROUP), logits, neg)
    v1 = jnp.max(el, axis=-1, keepdims=True)
    i1 = first_argmax(el, v1)
    el2 = jnp.where(lane == i1, neg, el)
    v2 = jnp.max(el2, axis=-1, keepdims=True)
    i2 = first_argmax(el2, v2)
    e2 = jnp.exp(v2 - v1)
    denom = 1.0 + e2
    w1 = (1.0 / denom) * g_gate
    w2 = (e2 / denom) * g_gate
    route = jnp.where(lane == ROUTE_E1, i1, 0.0) + jnp.where(lane == ROUTE_E2, i2, 0.0)
    return route + jnp.where(lane == ROUTE_W1, w1, 0.0) + jnp.where(lane == ROUTE_W2, w2, 0.0)


def _norm2(x1, g):
    ms = jnp.mean(x1 * x1, axis=-1, keepdims=True)
    return (x1 * lax.rsqrt(ms + EPS)) * g


OUT_SUBTILES = 2


def _out_proj_kernel(a_ref, r_ref, x_ref, wa_ref, wr_ref, g_ref, rcat_ref,
                     x1_ref, h2_ref, route_ref):
    sub = x_ref.shape[0] // OUT_SUBTILES
    for s in range(OUT_SUBTILES):
        rows = slice(s * sub, (s + 1) * sub)
        mixed = _dot(a_ref[rows, :], wa_ref[...]) + _dot(r_ref[rows, :], wr_ref[...])
        x1 = x_ref[rows, :] + mixed
        x1_ref[rows, :] = x1
        h2 = _norm2(x1, g_ref[...])
        h2_ref[rows, :] = h2
        hi = h2.astype(jnp.bfloat16)
        lo = (h2 - hi.astype(jnp.float32)).astype(jnp.bfloat16)
        both = _dot(hi, rcat_ref[...])
        logits = both[:, :LANES] + (both[:, LANES:] + _dot(lo, rcat_ref[:, :LANES]))
        route_ref[rows, :] = _routing(logits)


def _out_proj(attn2d, ret2d, x2d, w_out_bf, g2, r_cat, *, tm):
    n, d = x2d.shape
    return pl.pallas_call(
        _out_proj_kernel,
        out_shape=(
            jax.ShapeDtypeStruct((n, d), jnp.float32),
            jax.ShapeDtypeStruct((n, d), jnp.float32),
            jax.ShapeDtypeStruct((n, LANES), jnp.float32),
        ),
        grid=(n // tm,),
        in_specs=[
            pl.BlockSpec((tm, ATTN_WIDTH), lambda i: (i, 0)),
            pl.BlockSpec((tm, RET_WIDTH), lambda i: (i, 0)),
            pl.BlockSpec((tm, d), lambda i: (i, 0)),
            pl.BlockSpec((ATTN_WIDTH, d), lambda i: (0, 0)),
            pl.BlockSpec((RET_WIDTH, d), lambda i: (ATTN_WIDTH // RET_WIDTH, 0)),
            pl.BlockSpec((1, d), lambda i: (0, 0)),
            pl.BlockSpec((d, 2 * LANES), lambda i: (0, 0)),
        ],
        out_specs=(
            pl.BlockSpec((tm, d), lambda i: (i, 0)),
            pl.BlockSpec((tm, d), lambda i: (i, 0)),
            pl.BlockSpec((tm, LANES), lambda i: (i, 0)),
        ),
        compiler_params=_params("arbitrary"),
        name="out_proj",
    )(attn2d, ret2d, x2d, w_out_bf, w_out_bf, g2, r_cat)


def _plan_kernel(route_ref, pos_ref, cnt_ref, rank_scr, *, tm, blk):
    n = route_ref.shape[0]
    lane = lax.broadcasted_iota(jnp.int32, (blk, LANES), 1).astype(jnp.float32)
    before = (lax.broadcasted_iota(jnp.int32, (blk, blk), 1)
              < lax.broadcasted_iota(jnp.int32, (blk, blk), 0)).astype(jnp.bfloat16)

    def one_hot(rows):
        r = route_ref[rows, :]
        e1 = r[:, ROUTE_E1:ROUTE_E1 + 1]
        e2 = r[:, ROUTE_E2:ROUTE_E2 + 1]
        return lane == e1, lane == e2

    def rank_body(b, run):
        rows = pl.ds(pl.multiple_of(b * blk, blk), blk)
        m1, m2 = one_hot(rows)
        sel = jnp.where(m1 | m2, 1.0, 0.0)
        rank_scr[rows, :] = _dot(before, sel.astype(jnp.bfloat16)) + run
        return run + jnp.sum(sel, axis=0, keepdims=True)
    cnt = lax.fori_loop(0, n // blk, rank_body, jnp.zeros((1, LANES), jnp.float32), unroll=2)
    cnt_ref[...] = jnp.broadcast_to(cnt, cnt_ref.shape)

    tiles = jnp.floor((cnt + (tm - 1.0)) * (1.0 / tm))
    below = (lax.broadcasted_iota(jnp.int32, (LANES, LANES), 0)
             < lax.broadcasted_iota(jnp.int32, (LANES, LANES), 1)).astype(jnp.bfloat16)
    start = _dot(jnp.broadcast_to(tiles, (8, LANES)).astype(jnp.bfloat16), below)[0:1, :] * float(tm)

    def pos_body(b, carry):
        rows = pl.ds(pl.multiple_of(b * blk, blk), blk)
        m1, m2 = one_hot(rows)
        dest = rank_scr[rows, :] + start
        p1 = jnp.sum(jnp.where(m1, dest, 0.0), axis=-1, keepdims=True)
        p2 = jnp.sum(jnp.where(m2, dest, 0.0), axis=-1, keepdims=True)
        pos_ref[rows, :] = (jnp.where(lane == 0.0, p1, 0.0) + jnp.where(lane == 1.0, p2, 0.0)).astype(jnp.int32)
        return carry
    lax.fori_loop(0, n // blk, pos_body, 0, unroll=2)


def _moe_plan(route, *, tm, blk):
    n = route.shape[0]
    return pl.pallas_call(
        functools.partial(_plan_kernel, tm=tm, blk=blk),
        out_shape=(jax.ShapeDtypeStruct((n, LANES), jnp.int32),
                   jax.ShapeDtypeStruct((8, LANES), jnp.float32)),
        scratch_shapes=[pltpu.VMEM((n, LANES), jnp.float32)],
        compiler_params=pltpu.CompilerParams(vmem_limit_bytes=VMEM_LIMIT),
        name="moe_plan",
    )(route)


def _row_copy(src, src_row, dst, dst_row, sem):
    return pltpu.make_async_copy(src.at[pl.ds(src_row, 1), :], dst.at[pl.ds(dst_row, 1), :], sem)


def _invert_kernel(pos_ref, sid_ref):
    def body(j, carry):
        sid_ref[pos_ref[j]] = j
        return carry
    lax.fori_loop(0, pos_ref.shape[0], body, 0, unroll=8)


def _moe_invert(pos, *, rows):
    return pl.pallas_call(
        _invert_kernel,
        out_shape=jax.ShapeDtypeStruct((rows,), jnp.int32),
        in_specs=[pl.BlockSpec(memory_space=pltpu.SMEM)],
        out_specs=pl.BlockSpec(memory_space=pltpu.SMEM),
        name="moe_invert",
    )(pos)


ROW_GROUP = 8
ROW_BLOCK = 64
NO_NEXT, NOT_FIRST = -1, -2


def _ffn_kernel(te_ref, tv_ref, nu_ref, seg_ref, nexte_ref, sid_ref, tok_ref,
                h2_ref, w1_ref, w3_ref, w2_ref, y2_ref,
                xbuf, ybuf, w1buf, w3buf, w2buf, gsem, ssem, wsem):
    t = pl.program_id(0)
    n_used = nu_ref[0]
    tm = xbuf.shape[1]

    def weight_copies(e, wslot):
        return [pltpu.make_async_copy(src.at[e], dst.at[wslot], wsem.at[wslot])
                for src, dst in ((w1_ref, w1buf), (w3_ref, w3buf), (w2_ref, w2buf))]

    def gather(slot):
        return tok_ref, lambda r, tok: _row_copy(h2_ref, tok, xbuf.at[slot], r, gsem.at[slot])

    def scatter(slot):
        return sid_ref, lambda r, sid: _row_copy(ybuf.at[slot], r, y2_ref, sid, ssem.at[slot])

    def start_all(tile, table, copy):
        valid = tv_ref[tile]
        for blk in range(tm // ROW_BLOCK):
            @pl.when(valid >= (blk + 1) * ROW_BLOCK)
            def _():
                for r in range(blk * ROW_BLOCK, (blk + 1) * ROW_BLOCK):
                    copy(r, table[tile * tm + r]).start()
        done = (valid // ROW_BLOCK) * ROW_BLOCK

        def body(c, carry):
            for u in range(ROW_GROUP):
                r = done + c * ROW_GROUP + u

                @pl.when(r < valid)
                def _():
                    copy(r, table[tile * tm + r]).start()
            return carry
        lax.fori_loop(0, (valid - done + (ROW_GROUP - 1)) // ROW_GROUP, body, 0)

    def wait_all(tile, copy, block_copy):
        valid = tv_ref[tile]
        for blk in range(tm // ROW_BLOCK):
            @pl.when(valid >= (blk + 1) * ROW_BLOCK)
            def _():
                block_copy.wait()

        def body(r, carry):
            copy(0, 0).wait()
            return carry
        lax.fori_loop(0, valid % ROW_BLOCK, body, 0)

    def wait_gather(tile, slot):
        wait_all(tile, gather(slot)[1],
                 pltpu.make_async_copy(h2_ref.at[pl.ds(0, ROW_BLOCK), :],
                                       xbuf.at[slot, pl.ds(0, ROW_BLOCK), :], gsem.at[slot]))

    def wait_scatter(tile, slot):
        wait_all(tile, scatter(slot)[1],
                 pltpu.make_async_copy(ybuf.at[slot, pl.ds(0, ROW_BLOCK), :],
                                       y2_ref.at[pl.ds(0, ROW_BLOCK), :], ssem.at[slot]))

    @pl.when(t == 0)
    def _():
        for cp in weight_copies(te_ref[0], 0):
            cp.start()
        start_all(0, *gather(0))

    @pl.when(t < n_used)
    def _():
        slot = t % 2
        wslot = seg_ref[t] % 2

        @pl.when(t + 1 < n_used)
        def _():
            start_all(t + 1, *gather(1 - slot))

        @pl.when(nexte_ref[t] != NOT_FIRST)
        def _():
            for cp in weight_copies(te_ref[t], wslot):
                cp.wait()

            @pl.when(nexte_ref[t] >= 0)
            def _():
                for cp in weight_copies(nexte_ref[t], 1 - wslot):
                    cp.start()

        wait_gather(t, slot)

        @pl.when(t >= 2)
        def _():
            wait_scatter(t - 2, slot)

        row = lax.broadcasted_iota(jnp.int32, xbuf.shape[1:], 0)
        x = jnp.where(row < tv_ref[t], xbuf[slot], 0.0).astype(jnp.bfloat16)
        a = _dot(x, w1buf[wslot].astype(jnp.bfloat16))
        b = _dot(x, w3buf[wslot].astype(jnp.bfloat16))
        act = (a * (1.0 / (1.0 + jnp.exp(-a)))) * b
        ybuf[slot] = _dot(act.astype(jnp.bfloat16), w2buf[wslot].astype(jnp.bfloat16))
        start_all(t, *scatter(slot))

    @pl.when(t == pl.num_programs(0) - 1)
    def _():
        @pl.when(n_used >= 2)
        def _():
            wait_scatter(n_used - 2, n_used % 2)
        wait_scatter(n_used - 1, (n_used - 1) % 2)


def _moe_ffn(tile_expert, tile_valid, n_used, tile_seg, tile_next, row_sid, row_tok, h2, w1, w3, w2,
             *, tm):
    n, d = h2.shape
    _, _, ff = w1.shape
    max_tiles = tile_expert.shape[0]
    any_spec = pl.BlockSpec(memory_space=pl.ANY)
    return pl.pallas_call(
        _ffn_kernel,
        out_shape=jax.ShapeDtypeStruct((2 * n, d), jnp.float32),
        grid_spec=pltpu.PrefetchScalarGridSpec(
            num_scalar_prefetch=7,
            grid=(max_tiles,),
            in_specs=[any_spec, any_spec, any_spec, any_spec],
            out_specs=any_spec,
            scratch_shapes=[
                pltpu.VMEM((2, tm, d), jnp.float32),
                pltpu.VMEM((2, tm, d), jnp.float32),
                pltpu.VMEM((2, d, ff), jnp.float32),
                pltpu.VMEM((2, d, ff), jnp.float32),
                pltpu.VMEM((2, ff, d), jnp.float32),
                pltpu.SemaphoreType.DMA((2,)),
                pltpu.SemaphoreType.DMA((2,)),
                pltpu.SemaphoreType.DMA((2,)),
            ],
        ),
        compiler_params=_params("arbitrary"),
        name="moe_ffn",
    )(tile_expert, tile_valid, n_used, tile_seg, tile_next, row_sid, row_tok, h2, w1, w3, w2)


def _combine_kernel(x1_ref, route_ref, ya_ref, yb_ref, o_ref):
    route = route_ref[...]
    w1 = route[:, ROUTE_W1:ROUTE_W1 + 1]
    w2 = route[:, ROUTE_W2:ROUTE_W2 + 1]
    o_ref[...] = x1_ref[...] + (w1 * ya_ref[...] + w2 * yb_ref[...])


def _moe_combine(x1, route, y2, *, tm):
    n, d = x1.shape
    return pl.pallas_call(
        _combine_kernel,
        out_shape=jax.ShapeDtypeStruct((n, d), jnp.float32),
        grid=(n // tm,),
        in_specs=[pl.BlockSpec((tm, d), lambda i: (i, 0)),
                  pl.BlockSpec((tm, LANES), lambda i: (i, 0)),
                  pl.BlockSpec((tm, d), lambda i: (i, 0)),
                  pl.BlockSpec((tm, d), lambda i: (i + n // tm, 0))],
        out_specs=pl.BlockSpec((tm, d), lambda i: (i, 0)),
        compiler_params=_params("arbitrary"),
        name="moe_combine",
    )(x1, route, y2, y2)


def _moe(x1, h2, route, w1, w3, w2, *, tm, gather_tm):
    n, d = x1.shape
    max_tiles = (2 * n) // tm + N_EXPERTS
    pos2d, cnt = _moe_plan(route, tm=tm, blk=gather_tm)
    pos = pos2d[:, :2].T.reshape(2 * n)
    counts = cnt[0, :N_EXPERTS].astype(jnp.int32)
    tiles = (counts + (tm - 1)) // tm
    ends = jnp.cumsum(tiles)
    t_idx = jnp.arange(max_tiles, dtype=jnp.int32)
    tile_expert = jnp.sum((ends[None, :] <= t_idx[:, None]).astype(jnp.int32), axis=1)
    tile_expert = jnp.minimum(tile_expert, N_EXPERTS - 1)
    first_tile = (ends - tiles)[tile_expert]
    tile_valid = jnp.clip(counts[tile_expert] - (t_idx - first_tile) * tm, 0, tm).astype(jnp.int32)
    n_used = ends[-1:].astype(jnp.int32)
    used = t_idx < n_used[0]
    is_first = used & (t_idx == first_tile)
    tile_seg = (jnp.cumsum(is_first.astype(jnp.int32)) - 1).astype(jnp.int32)
    next_start = first_tile + tiles[tile_expert]
    next_e = jnp.where(next_start < n_used[0], tile_expert[jnp.minimum(next_start, max_tiles - 1)], NO_NEXT)
    tile_next = jnp.where(is_first, next_e, NOT_FIRST).astype(jnp.int32)

    row_sid = _moe_invert(pos, rows=max_tiles * tm)
    row_tok = jnp.where(row_sid >= n, row_sid - n, row_sid)
    y2 = _moe_ffn(tile_expert, tile_valid, n_used, tile_seg, tile_next, row_sid, row_tok, h2, w1, w3, w2,
                  tm=tm)
    return _moe_combine(x1, route, y2, tm=gather_tm)


def _pack_router(w_group, w_router):
    d = w_group.shape[0]
    experts = jnp.transpose(w_router, (1, 0, 2)).reshape(d, N_EXPERTS)
    wr = jnp.concatenate(
        [experts, w_group, jnp.zeros((d, LANES - N_EXPERTS - N_GROUPS), w_group.dtype)], axis=1)
    hi = wr.astype(jnp.bfloat16)
    lo = (wr - hi.astype(jnp.float32)).astype(jnp.bfloat16)
    return jnp.concatenate([hi, lo], axis=1)


def _rotation_tables(seq):
    half = RET_DK // 2
    pos = jnp.arange(seq, dtype=jnp.float32)
    inv = 1.0 / (ROT_BASE ** jnp.linspace(0.0, 1.0, half, dtype=jnp.float32))
    ang = pos[:, None] * inv[None, :]
    c, s = jnp.cos(ang), jnp.sin(ang)
    return jnp.concatenate([c, c], axis=-1), jnp.concatenate([-s, s], axis=-1)


def _tiles(n, seq):
    def fit(total, want):
        t = min(total, want)
        while total % t:
            t //= 2
        return t
    return dict(
        proj_tm=fit(n, 1024),
        attn_tq=fit(seq, 256), attn_tk=fit(seq, 256),
        ret_rc=fit(seq, 256),
        out_tm=fit(n, 512),
        moe_tm=fit(n, 256), moe_gather_tm=fit(n, 256),
    )


def kernel(x, norm1_g, w_in, q_norm_g, k_norm_g, idx_k_ln_w, idx_k_ln_b, ret_norm_g,
           w_out, norm2_g, w_group, w_router, w1, w3, w2):
    b, seq, d = x.shape
    n = b * seq
    depth = w_in.shape[0]
    t = _tiles(n, seq)
    cos2, sin2 = _rotation_tables(seq)
    log_gamma = jnp.log1p(-jnp.exp2(-5.0 - jnp.arange(RET_HEADS, dtype=jnp.float32)))

    x2d = x.reshape(n, d)
    for l in range(depth):
        proj = _in_proj(x2d, norm1_g[l][None, :], w_in[l].T, tm=t["proj_tm"])
        p3 = proj.reshape(b, seq, proj.shape[1])
        attn = _dsa_attention(p3, q_norm_g[l][None, :], k_norm_g[l][None, :],
                              idx_k_ln_w[l][None, :], idx_k_ln_b[l][None, :],
                              tq=t["attn_tq"], tk=t["attn_tk"])
        ret = _retention(p3, log_gamma, cos2, sin2, ret_norm_g[l].reshape(RET_HEADS, 1, RET_DV),
                         rc=t["ret_rc"])
        r_cat = _pack_router(w_group[l], w_router[l])
        g2 = norm2_g[l][None, :]
        x1, h2, route = _out_proj(attn.reshape(n, ATTN_WIDTH), ret.reshape(n, RET_WIDTH), x2d,
                                  w_out[l].astype(jnp.bfloat16), g2, r_cat, tm=t["out_tm"])
        x2d = _moe(x1, h2, route, w1[l], w3[l], w2[l], tm=t["moe_tm"], gather_tm=t["moe_gather_tm"])
    return x2d.reshape(b, seq, d)
```

```python
import functools
import math

import jax
import jax.numpy as jnp
from jax import lax
from jax.experimental import pallas as pl
from jax.experimental.pallas import tpu as pltpu

CHUNK = 64
ATTN_HEADS = 8
HEAD_DIM = 128
KV_HEADS = 2
HEADS_PER_KV = ATTN_HEADS // KV_HEADS
IDX_HEADS = 16
IDX_DIM = 64
TOPK_MAX = 256
RET_HEADS = 8
RET_DK = 128
RET_DV = 128
ROT_BASE = 10000.0
N_GROUPS = 4
EXPERTS_PER_GROUP = 8
N_EXPERTS = N_GROUPS * EXPERTS_PER_GROUP
EPS = 1e-6

ATTN_WIDTH = ATTN_HEADS * HEAD_DIM
KV_WIDTH = KV_HEADS * HEAD_DIM
IDX_WIDTH = IDX_HEADS * IDX_DIM
RET_WIDTH = RET_HEADS * RET_DK

LANES = 128
VMEM_LIMIT = 56 * 1024 * 1024

AQ_OFF = 0
AK_OFF = AQ_OFF + ATTN_WIDTH
AV_OFF = AK_OFF + KV_WIDTH
IQ_OFF = AV_OFF + KV_WIDTH
IK_OFF = IQ_OFF + IDX_WIDTH
IW_OFF = IK_OFF + IDX_DIM
W_RET = IW_OFF + IDX_HEADS
IN_WIDTH = W_RET + 4 * RET_WIDTH
assert IW_OFF // LANES == IK_OFF // LANES
PROJ_TN = 1024
RQ_OFF = -(-W_RET // PROJ_TN) * PROJ_TN
RK_OFF = RQ_OFF + RET_WIDTH
RV_OFF = RK_OFF + RET_WIDTH
RG_OFF = RV_OFF + RET_WIDTH
PROJ_WIDTH = RG_OFF + RET_WIDTH

ROUTE_E1, ROUTE_E2, ROUTE_W1, ROUTE_W2 = 0, 1, 2, 3

SUM_ROWS = 16
LOGIT_BOUND_SLACK = 1.05
MAX_SINGLE_SWEEP_BOUND = 50.0

INT_MIN = -(2 ** 31)
NEG_BIG = -1e30

_NT = (((1,), (1,)), ((), ()))


def _dot(a, b):
    return jnp.dot(a, b, preferred_element_type=jnp.float32)


def _dot_nt(a, b):
    return lax.dot_general(a, b, _NT, preferred_element_type=jnp.float32)


def _params(*sem):
    return pltpu.CompilerParams(dimension_semantics=sem, vmem_limit_bytes=VMEM_LIMIT)


def _in_proj_kernel(x_ref, g_ref, wt_ref, o_ref, h_scr, *, row_chunk):
    @pl.when(pl.program_id(1) == 0)
    def _():
        def body(c, carry):
            rows = pl.ds(pl.multiple_of(c * row_chunk, row_chunk), row_chunk)
            x = x_ref[rows, :]
            ms = jnp.mean(x * x, axis=-1, keepdims=True)
            h_scr[rows, :] = ((x * lax.rsqrt(ms + EPS)) * g_ref[...]).astype(jnp.bfloat16)
            return carry
        lax.fori_loop(0, x_ref.shape[0] // row_chunk, body, 0)

    o_ref[...] = _dot_nt(h_scr[...], wt_ref[...].astype(jnp.bfloat16)).astype(o_ref.dtype)


def _in_proj(x2d, g, w_in_t, *, tm):
    n, d = x2d.shape
    tn = PROJ_TN
    assert w_in_t.shape == (IN_WIDTH, d) and W_RET % 8 == 0
    attn_tiles = RQ_OFF // tn

    def window(i, j):
        start8 = jnp.where(j < attn_tiles, j * (tn // 8), W_RET // 8 + (j - attn_tiles) * (tn // 8))
        return 8 * start8, 0

    return pl.pallas_call(
        functools.partial(_in_proj_kernel, row_chunk=min(tm, 128)),
        out_shape=jax.ShapeDtypeStruct((n, PROJ_WIDTH), jnp.bfloat16),
        grid=(n // tm, PROJ_WIDTH // tn),
        in_specs=[
            pl.BlockSpec((tm, d), lambda i, j: (i, 0)),
            pl.BlockSpec((1, d), lambda i, j: (0, 0)),
            pl.BlockSpec((pl.Element(tn), pl.Element(d)), window),
        ],
        out_specs=pl.BlockSpec((tm, tn), lambda i, j: (i, j)),
        scratch_shapes=[pltpu.VMEM((tm, d), jnp.bfloat16)],
        compiler_params=_params("arbitrary", "arbitrary"),
        name="in_proj",
    )(x2d, g, w_in_t)


def _ordered_float(v):
    bits = v ^ ((v >> 31) & jnp.int32(0x7FFFFFFF))
    return pltpu.bitcast(bits, jnp.float32)


def _attn_kernel(aq_ref, iqa_ref, iqb_ref, iw_ref, ak_ref, av_ref, ik_ref, qg_ref, kg_ref, lnw_ref,
                 lnb_ref, o_ref,
                 kn_scr, ikn_scr, vt_scr, key_scr, wt_scr, qn_scr, acc_scr, s_scr, kmax_scr,
                 *, tk, topk, idx_w_scale):
    i = pl.program_id(1)
    seq = ak_ref.shape[1]
    tq = aq_ref.shape[1]
    chunk_shift = CHUNK.bit_length() - 1

    @pl.when(i == 0)
    def _():
        def body(c, carry):
            rows = pl.ds(pl.multiple_of(c * tk, tk), tk)
            for g in range(KV_HEADS):
                cols = slice(g * HEAD_DIM, (g + 1) * HEAD_DIM)
                k = ak_ref[0, rows, cols].astype(jnp.float32)
                ms = jnp.mean(k * k, axis=-1, keepdims=True)
                kn = (k * lax.rsqrt(ms + EPS)) * kg_ref[...]
                kn_scr[rows, cols] = kn.astype(jnp.bfloat16)
                ksq = jnp.max(jnp.sum(kn * kn, axis=-1, keepdims=True), axis=0, keepdims=True)
                prev = jnp.where(c == 0, 0.0, kmax_scr[g])
                kmax_scr[g] = jnp.maximum(prev, jnp.broadcast_to(ksq, kmax_scr.shape[1:]))
                v = av_ref[0, rows, cols].astype(jnp.float32)
                vt_scr[g, c, :HEAD_DIM, :] = v.T.astype(jnp.bfloat16)
                vt_scr[g, c, HEAD_DIM:, :] = jnp.ones((SUM_ROWS, tk), jnp.bfloat16)
            ki = ik_ref[0, rows, :IDX_DIM].astype(jnp.float32)
            mu = jnp.mean(ki, axis=-1, keepdims=True)
            var = jnp.mean(jnp.square(ki - mu), axis=-1, keepdims=True)
            y = ((ki - mu) * lax.rsqrt(var + EPS) * lnw_ref[...] + lnb_ref[...]).astype(jnp.bfloat16)
            zeros = jnp.zeros_like(y)
            ikn_scr[0, rows, :] = jnp.concatenate([y, zeros], axis=1)
            ikn_scr[1, rows, :] = jnp.concatenate([zeros, y], axis=1)
            return carry
        lax.fori_loop(0, seq // tk, body, 0)

    t0 = i * tq
    n_kt = (t0 + tq) // tk
    scale = (HEAD_DIM ** -0.5) * math.log2(math.e)
    for h in range(ATTN_HEADS):
        g, r = divmod(h, HEADS_PER_KV)
        q = aq_ref[0, :, h * HEAD_DIM:(h + 1) * HEAD_DIM].astype(jnp.float32)
        ms = jnp.mean(q * q, axis=-1, keepdims=True)
        qn_scr[g, r * tq:(r + 1) * tq, :] = (
            (q * lax.rsqrt(ms + EPS)) * qg_ref[...] * scale).astype(jnp.bfloat16)
    ones_rows = jnp.ones((8, HEAD_DIM), jnp.bfloat16)
    bound = []
    for g in range(KV_HEADS):
        qf = qn_scr[g].astype(jnp.float32)
        qsq = _dot_nt(ones_rows, (qf * qf).astype(jnp.bfloat16))[0:1, :]
        kmax = jnp.concatenate([kmax_scr[g, 0:1, :]] * (HEADS_PER_KV * tq // LANES), axis=1)
        bound.append(LOGIT_BOUND_SLACK * jnp.sqrt(qsq * kmax))
    wt_scr[...] = iw_ref[0].astype(jnp.float32).T * idx_w_scale
    w_row = IW_OFF % LANES

    q_chunk = (t0 + lax.broadcasted_iota(jnp.int32, (tk, tq), 1)) >> chunk_shift

    def score_body(kt, carry):
        rows = pl.ds(pl.multiple_of(kt * tk, tk), tk)
        ik_first, ik_second = ikn_scr[0, rows, :], ikn_scr[1, rows, :]
        acc = jnp.zeros((tk, tq), jnp.float32)
        pairs_per_ref = iqa_ref.shape[2] // LANES
        for pair in range(IDX_HEADS // 2):
            src = iqa_ref if pair < pairs_per_ref else iqb_ref
            lane0 = (pair % pairs_per_ref) * LANES
            q_pair = src[0, :, lane0:lane0 + LANES]
            for sub, ik_t in enumerate((ik_first, ik_second)):
                h = 2 * pair + sub
                d = _dot_nt(ik_t, q_pair)
                acc = acc + jnp.maximum(d, 0.0) * wt_scr[w_row + h:w_row + h + 1, :]
        k_chunk = (kt * tk + lax.broadcasted_iota(jnp.int32, (tk, tq), 0)) >> chunk_shift
        key_scr[rows, :] = jnp.where(k_chunk <= q_chunk, acc, -jnp.inf)
        return carry
    lax.fori_loop(0, n_kt, score_body, 0)

    def bit_body(it, lo):
        cand = lo + lax.shift_left(jnp.int32(1), 31 - it)
        cand_f = _ordered_float(cand)

        def count_body(kt, part):
            rows = pl.ds(pl.multiple_of(kt * tk, tk), tk)
            hit = jnp.where(key_scr[rows, :] >= cand_f, 1.0, 0.0).reshape(tk // 8, 8, tq)
            while hit.shape[0] > 1:
                half = hit.shape[0] // 2
                hit = hit[:half] + hit[half:]
            return part + hit[0]
        part = lax.fori_loop(0, n_kt, count_body, jnp.zeros((8, tq), jnp.float32))
        cnt = jnp.sum(part, axis=0, keepdims=True)
        return jnp.where(cnt >= float(topk), cand, lo)
    lo = lax.fori_loop(0, 32, bit_body, jnp.full((1, tq), INT_MIN, jnp.int32))
    thr = jnp.where(lo == INT_MIN, jnp.finfo(jnp.float32).min, _ordered_float(lo))

    acc_scr[...] = jnp.zeros(acc_scr.shape, jnp.float32)

    def masked_logits(kt):
        rows = pl.ds(pl.multiple_of(kt * tk, tk), tk)
        bias = jnp.where(key_scr[rows, :] >= thr, 0.0, NEG_BIG)
        bias = jnp.concatenate([bias] * HEADS_PER_KV, axis=1)
        return [_dot_nt(kn_scr[rows, g * HEAD_DIM:(g + 1) * HEAD_DIM], qn_scr[g]) + bias
                for g in range(KV_HEADS)]

    bound_max = jnp.max(jnp.maximum(bound[0], bound[1]))
    single_sweep = bound_max <= MAX_SINGLE_SWEEP_BOUND

    @pl.when(single_sweep)
    def _():
        def body(kt, carry):
            for g, s in enumerate(masked_logits(kt)):
                acc_scr[g] += _dot(vt_scr[g, kt], jnp.exp2(s - bound[g]).astype(jnp.bfloat16))
            return carry
        lax.fori_loop(0, n_kt, body, 0)

    @pl.when(jnp.logical_not(single_sweep))
    def _():
        def logit_body(kt, m):
            rows = pl.ds(pl.multiple_of(kt * tk, tk), tk)
            new_m = []
            for g, s in enumerate(masked_logits(kt)):
                s_scr[g, rows, :] = s
                new_m.append(jnp.maximum(m[g], jnp.max(s, axis=0, keepdims=True)))
            return tuple(new_m)
        m0 = jnp.full((1, HEADS_PER_KV * tq), NEG_BIG, jnp.float32)
        m = lax.fori_loop(0, n_kt, logit_body, (m0,) * KV_HEADS)

        def pv_body(kt, carry):
            rows = pl.ds(pl.multiple_of(kt * tk, tk), tk)
            for g in range(KV_HEADS):
                p = jnp.exp2(s_scr[g, rows, :] - m[g]).astype(jnp.bfloat16)
                acc_scr[g] += _dot(vt_scr[g, kt], p)
            return carry
        lax.fori_loop(0, n_kt, pv_body, 0)

    for h in range(ATTN_HEADS):
        g, r = divmod(h, HEADS_PER_KV)
        cols = slice(r * tq, (r + 1) * tq)
        o = acc_scr[g, :HEAD_DIM, cols] / acc_scr[g, HEAD_DIM:HEAD_DIM + 1, cols]
        o_ref[0, :, h * HEAD_DIM:(h + 1) * HEAD_DIM] = o.T.astype(o_ref.dtype)


def _dsa_attention(p3, q_g, k_g, ln_w, ln_b, *, tq, tk):
    b, seq, _ = p3.shape
    topk = min(TOPK_MAX, seq // 4)
    idx_w_scale = (IDX_HEADS ** -0.5) * (IDX_DIM ** -0.5)
    assert seq % tq == 0 and tq % tk == 0 and tk % CHUNK == 0

    def col(off, width):
        assert off % width == 0 or width == LANES
        return off // width

    half_iq = IDX_WIDTH // 2
    return pl.pallas_call(
        functools.partial(_attn_kernel, tk=tk, topk=topk, idx_w_scale=idx_w_scale),
        out_shape=jax.ShapeDtypeStruct((b, seq, ATTN_WIDTH), jnp.bfloat16),
        grid=(b, seq // tq),
        in_specs=[
            pl.BlockSpec((1, tq, ATTN_WIDTH), lambda bi, i: (bi, i, col(AQ_OFF, ATTN_WIDTH))),
            pl.BlockSpec((1, tq, half_iq), lambda bi, i: (bi, i, col(IQ_OFF, half_iq))),
            pl.BlockSpec((1, tq, half_iq), lambda bi, i: (bi, i, col(IQ_OFF, half_iq) + 1)),
            pl.BlockSpec((1, tq, LANES), lambda bi, i: (bi, i, col(IW_OFF, LANES))),
            pl.BlockSpec((1, seq, KV_WIDTH), lambda bi, i: (bi, 0, col(AK_OFF, KV_WIDTH))),
            pl.BlockSpec((1, seq, KV_WIDTH), lambda bi, i: (bi, 0, col(AV_OFF, KV_WIDTH))),
            pl.BlockSpec((1, seq, LANES), lambda bi, i: (bi, 0, col(IK_OFF, LANES))),
            pl.BlockSpec((1, HEAD_DIM), lambda bi, i: (0, 0)),
            pl.BlockSpec((1, HEAD_DIM), lambda bi, i: (0, 0)),
            pl.BlockSpec((1, IDX_DIM), lambda bi, i: (0, 0)),
            pl.BlockSpec((1, IDX_DIM), lambda bi, i: (0, 0)),
        ],
        out_specs=pl.BlockSpec((1, tq, ATTN_WIDTH), lambda bi, i: (bi, i, 0)),
        scratch_shapes=[
            pltpu.VMEM((seq, KV_WIDTH), jnp.bfloat16),
            pltpu.VMEM((2, seq, 2 * IDX_DIM), jnp.bfloat16),
            pltpu.VMEM((KV_HEADS, seq // tk, HEAD_DIM + SUM_ROWS, tk), jnp.bfloat16),
            pltpu.VMEM((seq, tq), jnp.float32),
            pltpu.VMEM((LANES, tq), jnp.float32),
            pltpu.VMEM((KV_HEADS, HEADS_PER_KV * tq, HEAD_DIM), jnp.bfloat16),
            pltpu.VMEM((KV_HEADS, HEAD_DIM + SUM_ROWS, HEADS_PER_KV * tq), jnp.float32),
            pltpu.VMEM((KV_HEADS, seq, HEADS_PER_KV * tq), jnp.float32),
            pltpu.VMEM((KV_HEADS, 8, LANES), jnp.float32),
        ],
        compiler_params=_params("arbitrary", "arbitrary"),
        name="dsa_attn",
    )(p3, p3, p3, p3, p3, p3, p3, q_g, k_g, ln_w, ln_b)


RET_HEADS_PER_STEP = 2


def _ret_kernel(lg_ref, rq_ref, rk_ref, rv_ref, rg_ref, cos_ref, sin_ref, g_ref, o_ref, *, rc):
    seq = rq_ref.shape[1]
    n = lax.broadcasted_iota(jnp.int32, (rc, RET_DV), 0).astype(jnp.float32)
    rel = (lax.broadcasted_iota(jnp.int32, (rc, rc), 0)
           - lax.broadcasted_iota(jnp.int32, (rc, rc), 1)).astype(jnp.float32)

    def rot(x, rows):
        return x * cos_ref[rows, :] + pltpu.roll(x, RET_DK // 2, 1) * sin_ref[rows, :]

    heads = []
    for hh in range(RET_HEADS_PER_STEP):
        lg = lg_ref[pl.program_id(1) * RET_HEADS_PER_STEP + hh]
        heads.append(dict(
            cols=slice(hh * RET_DK, (hh + 1) * RET_DK),
            cross_decay=jnp.exp(lg * (n + 1.0)),
            state_decay=jnp.exp(lg * (rc - 1.0 - n)),
            chunk_decay=jnp.exp(lg * jnp.full((RET_DK, RET_DV), float(rc), jnp.float32)),
            intra=jnp.where(rel >= 0, jnp.exp(lg * jnp.maximum(rel, 0.0)), 0.0),
            state=jnp.zeros((RET_DK, RET_DV), jnp.float32),
            gain=g_ref[hh],
        ))

    for c in range(seq // rc):
        rows = slice(c * rc, (c + 1) * rc)
        for hd in heads:
            cols = hd["cols"]
            q = rot(rq_ref[0, rows, cols].astype(jnp.float32), rows)
            k = rot(rk_ref[0, rows, cols].astype(jnp.float32), rows) * (RET_DK ** -0.5)
            v = rv_ref[0, rows, cols]
            qb = q.astype(jnp.bfloat16)
            inner = _dot_nt(qb, k.astype(jnp.bfloat16)) * hd["intra"]
            o = (_dot(inner.astype(jnp.bfloat16), v)
                 + _dot(qb, hd["state"].astype(jnp.bfloat16)) * hd["cross_decay"])
            kd_t = (k * hd["state_decay"]).T.astype(jnp.bfloat16)
            hd["state"] = hd["state"] * hd["chunk_decay"] + _dot(kd_t, v)
            ms = jnp.mean(o * o, axis=-1, keepdims=True)
            y = (o * lax.rsqrt(ms + EPS)) * hd["gain"]
            gate = rg_ref[0, rows, cols].astype(jnp.float32)
            o_ref[0, rows, cols] = ((gate * (1.0 / (1.0 + jnp.exp(-gate)))) * y).astype(o_ref.dtype)


def _retention(p3, log_gamma, cos2, sin2, ret_g, *, rc):
    b, seq, _ = p3.shape
    hps = RET_HEADS_PER_STEP
    width = hps * RET_DK
    assert seq % rc == 0 and RET_HEADS % hps == 0

    def head_spec(off):
        assert off % width == 0
        return pl.BlockSpec((1, seq, width), lambda bi, h: (bi, 0, off // width + h))

    return pl.pallas_call(
        functools.partial(_ret_kernel, rc=rc),
        out_shape=jax.ShapeDtypeStruct((b, seq, RET_WIDTH), jnp.bfloat16),
        grid=(b, RET_HEADS // hps),
        in_specs=[
            pl.BlockSpec(memory_space=pltpu.SMEM),
            head_spec(RQ_OFF), head_spec(RK_OFF), head_spec(RV_OFF), head_spec(RG_OFF),
            pl.BlockSpec((seq, RET_DK), lambda bi, h: (0, 0)),
            pl.BlockSpec((seq, RET_DK), lambda bi, h: (0, 0)),
            pl.BlockSpec((hps, 1, RET_DV), lambda bi, h: (h, 0, 0)),
        ],
        out_specs=pl.BlockSpec((1, seq, width), lambda bi, h: (bi, 0, h)),
        compiler_params=_params("arbitrary", "arbitrary"),
        name="retention",
    )(log_gamma, p3, p3, p3, p3, cos2, sin2, ret_g)


def _routing(logits):
    lane = lax.broadcasted_iota(jnp.int32, logits.shape, 1).astype(jnp.float32)
    big = float(LANES)
    neg = -jnp.inf

    def first_argmax(v, vmax):
        return jnp.min(jnp.where(v == vmax, lane, big), axis=-1, keepdims=True)

    g_mask = (lane >= N_EXPERTS) & (lane < N_EXPERTS + N_GROUPS)
    gl = jnp.where(g_mask, logits, neg)
    g_max = jnp.max(gl, axis=-1, keepdims=True)
    g_sel = first_argmax(gl, g_max) - N_EXPERTS
    g_gate = 1.0 / jnp.sum(jnp.where(g_mask, jnp.exp(gl - g_max), 0.0), axis=-1, keepdims=True)

    e_lo = g_sel * EXPERTS_PER_GROUP
    el = jnp.where((lane >= e_lo) & (lane < e_lo + EXPERTS_PER_GROUP), logits, neg)
    v1 = jnp.max(el, axis=-1, keepdims=True)
    i1 = first_argmax(el, v1)
    el2 = jnp.where(lane == i1, neg, el)
    v2 = jnp.max(el2, axis=-1, keepdims=True)
    i2 = first_argmax(el2, v2)
    e2 = jnp.exp(v2 - v1)
    denom = 1.0 + e2
    w1 = (1.0 / denom) * g_gate
    w2 = (e2 / denom) * g_gate
    route = jnp.where(lane == ROUTE_E1, i1, 0.0) + jnp.where(lane == ROUTE_E2, i2, 0.0)
    return route + jnp.where(lane == ROUTE_W1, w1, 0.0) + jnp.where(lane == ROUTE_W2, w2, 0.0)


def _norm2(x1, g):
    ms = jnp.mean(x1 * x1, axis=-1, keepdims=True)
    return (x1 * lax.rsqrt(ms + EPS)) * g


OUT_SUBTILES = 2


def _out_proj_kernel(a_ref, r_ref, x_ref, wa_ref, wr_ref, g_ref, rcat_ref,
                     x1_ref, h2_ref, route_ref):
    sub = x_ref.shape[0] // OUT_SUBTILES
    for s in range(OUT_SUBTILES):
        rows = slice(s * sub, (s + 1) * sub)
        mixed = _dot(a_ref[rows, :], wa_ref[...]) + _dot(r_ref[rows, :], wr_ref[...])
        x1 = x_ref[rows, :] + mixed
        x1_ref[rows, :] = x1
        h2 = _norm2(x1, g_ref[...])
        h2_ref[rows, :] = h2
        hi = h2.astype(jnp.bfloat16)
        lo = (h2 - hi.astype(jnp.float32)).astype(jnp.bfloat16)
        both = _dot(hi, rcat_ref[...])
        logits = both[:, :LANES] + (both[:, LANES:] + _dot(lo, rcat_ref[:, :LANES]))
        route_ref[rows, :] = _routing(logits)


def _out_proj(attn2d, ret2d, x2d, w_out_bf, g2, r_cat, *, tm):
    n, d = x2d.shape
    return pl.pallas_call(
        _out_proj_kernel,
        out_shape=(
            jax.ShapeDtypeStruct((n, d), jnp.float32),
            jax.ShapeDtypeStruct((n, d), jnp.float32),
            jax.ShapeDtypeStruct((n, LANES), jnp.float32),
        ),
        grid=(n // tm,),
        in_specs=[
            pl.BlockSpec((tm, ATTN_WIDTH), lambda i: (i, 0)),
            pl.BlockSpec((tm, RET_WIDTH), lambda i: (i, 0)),
            pl.BlockSpec((tm, d), lambda i: (i, 0)),
            pl.BlockSpec((ATTN_WIDTH, d), lambda i: (0, 0)),
            pl.BlockSpec((RET_WIDTH, d), lambda i: (ATTN_WIDTH // RET_WIDTH, 0)),
            pl.BlockSpec((1, d), lambda i: (0, 0)),
            pl.BlockSpec((d, 2 * LANES), lambda i: (0, 0)),
        ],
        out_specs=(
            pl.BlockSpec((tm, d), lambda i: (i, 0)),
            pl.BlockSpec((tm, d), lambda i: (i, 0)),
            pl.BlockSpec((tm, LANES), lambda i: (i, 0)),
        ),
        compiler_params=_params("arbitrary"),
        name="out_proj",
    )(attn2d, ret2d, x2d, w_out_bf, w_out_bf, g2, r_cat)


def _plan_kernel(route_ref, pos_ref, cnt_ref, rank_scr, *, tm, blk):
    n = route_ref.shape[0]
    lane = lax.broadcasted_iota(jnp.int32, (blk, LANES), 1).astype(jnp.float32)
    before = (lax.broadcasted_iota(jnp.int32, (blk, blk), 1)
              < lax.broadcasted_iota(jnp.int32, (blk, blk), 0)).astype(jnp.bfloat16)

    def one_hot(rows):
        r = route_ref[rows, :]
        e1 = r[:, ROUTE_E1:ROUTE_E1 + 1]
        e2 = r[:, ROUTE_E2:ROUTE_E2 + 1]
        return lane == e1, lane == e2

    def rank_body(b, run):
        rows = pl.ds(pl.multiple_of(b * blk, blk), blk)
        m1, m2 = one_hot(rows)
        sel = jnp.where(m1 | m2, 1.0, 0.0)
        rank_scr[rows, :] = _dot(before, sel.astype(jnp.bfloat16)) + run
        return run + jnp.sum(sel, axis=0, keepdims=True)
    cnt = lax.fori_loop(0, n // blk, rank_body, jnp.zeros((1, LANES), jnp.float32), unroll=2)
    cnt_ref[...] = jnp.broadcast_to(cnt, cnt_ref.shape)

    tiles = jnp.floor((cnt + (tm - 1.0)) * (1.0 / tm))
    below = (lax.broadcasted_iota(jnp.int32, (LANES, LANES), 0)
             < lax.broadcasted_iota(jnp.int32, (LANES, LANES), 1)).astype(jnp.bfloat16)
    start = _dot(jnp.broadcast_to(tiles, (8, LANES)).astype(jnp.bfloat16), below)[0:1, :] * float(tm)

    def pos_body(b, carry):
        rows = pl.ds(pl.multiple_of(b * blk, blk), blk)
        m1, m2 = one_hot(rows)
        dest = rank_scr[rows, :] + start
        p1 = jnp.sum(jnp.where(m1, dest, 0.0), axis=-1, keepdims=True)
        p2 = jnp.sum(jnp.where(m2, dest, 0.0), axis=-1, keepdims=True)
        pos_ref[rows, :] = (jnp.where(lane == 0.0, p1, 0.0) + jnp.where(lane == 1.0, p2, 0.0)).astype(jnp.int32)
        return carry
    lax.fori_loop(0, n // blk, pos_body, 0, unroll=2)


def _moe_plan(route, *, tm, blk):
    n = route.shape[0]
    return pl.pallas_call(
        functools.partial(_plan_kernel, tm=tm, blk=blk),
        out_shape=(jax.ShapeDtypeStruct((n, LANES), jnp.int32),
                   jax.ShapeDtypeStruct((8, LANES), jnp.float32)),
        scratch_shapes=[pltpu.VMEM((n, LANES), jnp.float32)],
        compiler_params=pltpu.CompilerParams(vmem_limit_bytes=VMEM_LIMIT),
        name="moe_plan",
    )(route)


def _row_copy(src, src_row, dst, dst_row, sem):
    return pltpu.make_async_copy(src.at[pl.ds(src_row, 1), :], dst.at[pl.ds(dst_row, 1), :], sem)


def _invert_kernel(pos_ref, sid_ref):
    def body(j, carry):
        sid_ref[pos_ref[j]] = j
        return carry
    lax.fori_loop(0, pos_ref.shape[0], body, 0, unroll=8)


def _moe_invert(pos, *, rows):
    return pl.pallas_call(
        _invert_kernel,
        out_shape=jax.ShapeDtypeStruct((rows,), jnp.int32),
        in_specs=[pl.BlockSpec(memory_space=pltpu.SMEM)],
        out_specs=pl.BlockSpec(memory_space=pltpu.SMEM),
        name="moe_invert",
    )(pos)


ROW_GROUP = 8
ROW_BLOCK = 32
NO_NEXT, NOT_FIRST = -1, -2


def _ffn_kernel(te_ref, tv_ref, nu_ref, seg_ref, nexte_ref, sid_ref, tok_ref,
                h2_ref, w1_ref, w3_ref, w2_ref, y2_ref,
                xbuf, ybuf, w1buf, w3buf, w2buf, gsem, ssem, wsem):
    t = pl.program_id(0)
    n_used = nu_ref[0]
    tm = xbuf.shape[1]

    def weight_copies(e, wslot):
        return [pltpu.make_async_copy(src.at[e], dst.at[wslot], wsem.at[wslot])
                for src, dst in ((w1_ref, w1buf), (w3_ref, w3buf), (w2_ref, w2buf))]

    def gather(slot):
        return tok_ref, lambda r, tok: _row_copy(h2_ref, tok, xbuf.at[slot], r, gsem.at[slot])

    def scatter(slot):
        return sid_ref, lambda r, sid: _row_copy(ybuf.at[slot], r, y2_ref, sid, ssem.at[slot])

    def start_all(tile, table, copy):
        valid = tv_ref[tile]
        for blk in range(tm // ROW_BLOCK):
            @pl.when(valid >= (blk + 1) * ROW_BLOCK)
            def _():
                for r in range(blk * ROW_BLOCK, (blk + 1) * ROW_BLOCK):
                    copy(r, table[tile * tm + r]).start()
        done = (valid // ROW_BLOCK) * ROW_BLOCK

        def body(c, carry):
            for u in range(ROW_GROUP):
                r = done + c * ROW_GROUP + u

                @pl.when(r < valid)
                def _():
                    copy(r, table[tile * tm + r]).start()
            return carry
        lax.fori_loop(0, (valid - done + (ROW_GROUP - 1)) // ROW_GROUP, body, 0)

    def wait_all(tile, copy, block_copy):
        valid = tv_ref[tile]
        for blk in range(tm // ROW_BLOCK):
            @pl.when(valid >= (blk + 1) * ROW_BLOCK)
            def _():
                block_copy.wait()

        def body(r, carry):
            copy(0, 0).wait()
            return carry
        lax.fori_loop(0, valid % ROW_BLOCK, body, 0)

    def wait_gather(tile, slot):
        wait_all(tile, gather(slot)[1],
                 pltpu.make_async_copy(h2_ref.at[pl.ds(0, ROW_BLOCK), :],
                                       xbuf.at[slot, pl.ds(0, ROW_BLOCK), :], gsem.at[slot]))

    def wait_scatter(tile, slot):
        wait_all(tile, scatter(slot)[1],
                 pltpu.make_async_copy(ybuf.at[slot, pl.ds(0, ROW_BLOCK), :],
                                       y2_ref.at[pl.ds(0, ROW_BLOCK), :], ssem.at[slot]))

    @pl.when(t == 0)
    def _():
        for cp in weight_copies(te_ref[0], 0):
            cp.start()
        start_all(0, *gather(0))

    @pl.when(t < n_used)
    def _():
        slot = t % 2
        wslot = seg_ref[t] % 2

        @pl.when(t + 1 < n_used)
        def _():
            start_all(t + 1, *gather(1 - slot))

        @pl.when(nexte_ref[t] != NOT_FIRST)
        def _():
            for cp in weight_copies(te_ref[t], wslot):
                cp.wait()

            @pl.when(nexte_ref[t] >= 0)
            def _():
                for cp in weight_copies(nexte_ref[t], 1 - wslot):
                    cp.start()

        wait_gather(t, slot)

        @pl.when(t >= 2)
        def _():
            wait_scatter(t - 2, slot)

        row = lax.broadcasted_iota(jnp.int32, xbuf.shape[1:], 0)
        x = jnp.where(row < tv_ref[t], xbuf[slot], 0.0).astype(jnp.bfloat16)
        a = _dot(x, w1buf[wslot].astype(jnp.bfloat16))
        b = _dot(x, w3buf[wslot].astype(jnp.bfloat16))
        act = (a * (1.0 / (1.0 + jnp.exp(-a)))) * b
        ybuf[slot] = _dot(act.astype(jnp.bfloat16), w2buf[wslot].astype(jnp.bfloat16))
        start_all(t, *scatter(slot))

    @pl.when(t == pl.num_programs(0) - 1)
    def _():
        @pl.when(n_used >= 2)
        def _():
            wait_scatter(n_used - 2, n_used % 2)
        wait_scatter(n_used - 1, (n_used - 1) % 2)


def _moe_ffn(tile_expert, tile_valid, n_used, tile_seg, tile_next, row_sid, row_tok, h2, w1, w3, w2,
             *, tm):
    n, d = h2.shape
    _, _, ff = w1.shape
    max_tiles = tile_expert.shape[0]
    any_spec = pl.BlockSpec(memory_space=pl.ANY)
    return pl.pallas_call(
        _ffn_kernel,
        out_shape=jax.ShapeDtypeStruct((2 * n, d), jnp.float32),
        grid_spec=pltpu.PrefetchScalarGridSpec(
            num_scalar_prefetch=7,
            grid=(max_tiles,),
            in_specs=[any_spec, any_spec, any_spec, any_spec],
            out_specs=any_spec,
            scratch_shapes=[
                pltpu.VMEM((2, tm, d), jnp.float32),
                pltpu.VMEM((2, tm, d), jnp.float32),
                pltpu.VMEM((2, d, ff), jnp.float32),
                pltpu.VMEM((2, d, ff), jnp.float32),
                pltpu.VMEM((2, ff, d), jnp.float32),
                pltpu.SemaphoreType.DMA((2,)),
                pltpu.SemaphoreType.DMA((2,)),
                pltpu.SemaphoreType.DMA((2,)),
            ],
        ),
        compiler_params=_params("arbitrary"),
        name="moe_ffn",
    )(tile_expert, tile_valid, n_used, tile_seg, tile_next, row_sid, row_tok, h2, w1, w3, w2)


def _combine_kernel(x1_ref, route_ref, ya_ref, yb_ref, o_ref):
    route = route_ref[...]
    w1 = route[:, ROUTE_W1:ROUTE_W1 + 1]
    w2 = route[:, ROUTE_W2:ROUTE_W2 + 1]
    o_ref[...] = x1_ref[...] + (w1 * ya_ref[...] + w2 * yb_ref[...])


def _moe_combine(x1, route, y2, *, tm):
    n, d = x1.shape
    return pl.pallas_call(
        _combine_kernel,
        out_shape=jax.ShapeDtypeStruct((n, d), jnp.float32),
        grid=(n // tm,),
        in_specs=[pl.BlockSpec((tm, d), lambda i: (i, 0)),
                  pl.BlockSpec((tm, LANES), lambda i: (i, 0)),
                  pl.BlockSpec((tm, d), lambda i: (i, 0)),
                  pl.BlockSpec((tm, d), lambda i: (i + n // tm, 0))],
        out_specs=pl.BlockSpec((tm, d), lambda i: (i, 0)),
        compiler_params=_params("arbitrary"),
        name="moe_combine",
    )(x1, route, y2, y2)


def _moe(x1, h2, route, w1, w3, w2, *, tm, gather_tm):
    n, d = x1.shape
    max_tiles = (2 * n) // tm + N_EXPERTS
    pos2d, cnt = _moe_plan(route, tm=tm, blk=gather_tm)
    pos = pos2d[:, :2].T.reshape(2 * n)
    counts = cnt[0, :N_EXPERTS].astype(jnp.int32)
    tiles = (counts + (tm - 1)) // tm
    ends = jnp.cumsum(tiles)
    t_idx = jnp.arange(max_tiles, dtype=jnp.int32)
    tile_expert = jnp.sum((ends[None, :] <= t_idx[:, None]).astype(jnp.int32), axis=1)
    tile_expert = jnp.minimum(tile_expert, N_EXPERTS - 1)
    first_tile = (ends - tiles)[tile_expert]
    tile_valid = jnp.clip(counts[tile_expert] - (t_idx - first_tile) * tm, 0, tm).astype(jnp.int32)
    n_used = ends[-1:].astype(jnp.int32)
    used = t_idx < n_used[0]
    is_first = used & (t_idx == first_tile)
    tile_seg = (jnp.cumsum(is_first.astype(jnp.int32)) - 1).astype(jnp.int32)
    next_start = first_tile + tiles[tile_expert]
    next_e = jnp.where(next_start < n_used[0], tile_expert[jnp.minimum(next_start, max_tiles - 1)], NO_NEXT)
    tile_next = jnp.where(is_first, next_e, NOT_FIRST).astype(jnp.int32)

    row_sid = _moe_invert(pos, rows=max_tiles * tm)
    row_tok = jnp.where(row_sid >= n, row_sid - n, row_sid)
    y2 = _moe_ffn(tile_expert, tile_valid, n_used, tile_seg, tile_next, row_sid, row_tok, h2, w1, w3, w2,
                  tm=tm)
    return _moe_combine(x1, route, y2, tm=gather_tm)


def _pack_router(w_group, w_router):
    d = w_group.shape[0]
    experts = jnp.transpose(w_router, (1, 0, 2)).reshape(d, N_EXPERTS)
    wr = jnp.concatenate(
        [experts, w_group, jnp.zeros((d, LANES - N_EXPERTS - N_GROUPS), w_group.dtype)], axis=1)
    hi = wr.astype(jnp.bfloat16)
    lo = (wr - hi.astype(jnp.float32)).astype(jnp.bfloat16)
    return jnp.concatenate([hi, lo], axis=1)


def _rotation_tables(seq):
    half = RET_DK // 2
    pos = jnp.arange(seq, dtype=jnp.float32)
    inv = 1.0 / (ROT_BASE ** jnp.linspace(0.0, 1.0, half, dtype=jnp.float32))
    ang = pos[:, None] * inv[None, :]
    c, s = jnp.cos(ang), jnp.sin(ang)
    return jnp.concatenate([c, c], axis=-1), jnp.concatenate([-s, s], axis=-1)


def _tiles(n, seq):
    def fit(total, want):
        t = min(total, want)
        while total % t:
            t //= 2
        return t
    return dict(
        proj_tm=fit(n, 1024),
        attn_tq=fit(seq, 256), attn_tk=fit(seq, 256),
        ret_rc=fit(seq, 256),
        out_tm=fit(n, 512),
        moe_tm=fit(n, 256), moe_gather_tm=fit(n, 256),
    )


def kernel(x, norm1_g, w_in, q_norm_g, k_norm_g, idx_k_ln_w, idx_k_ln_b, ret_norm_g,
           w_out, norm2_g, w_group, w_router, w1, w3, w2):
    b, seq, d = x.shape
    n = b * seq
    depth = w_in.shape[0]
    t = _tiles(n, seq)
    cos2, sin2 = _rotation_tables(seq)
    log_gamma = jnp.log1p(-jnp.exp2(-5.0 - jnp.arange(RET_HEADS, dtype=jnp.float32)))

    x2d = x.reshape(n, d)
    for l in range(depth):
        proj = _in_proj(x2d, norm1_g[l][None, :], w_in[l].T, tm=t["proj_tm"])
        p3 = proj.reshape(b, seq, proj.shape[1])
        attn = _dsa_attention(p3, q_norm_g[l][None, :], k_norm_g[l][None, :],
                              idx_k_ln_w[l][None, :], idx_k_ln_b[l][None, :],
                              tq=t["attn_tq"], tk=t["attn_tk"])
        ret = _retention(p3, log_gamma, cos2, sin2, ret_norm_g[l].reshape(RET_HEADS, 1, RET_DV),
                         rc=t["ret_rc"])
        r_cat = _pack_router(w_group[l], w_router[l])
        g2 = norm2_g[l][None, :]
        x1, h2, route = _out_proj(attn.reshape(n, ATTN_WIDTH), ret.reshape(n, RET_WIDTH), x2d,
                                  w_out[l].astype(jnp.bfloat16), g2, r_cat, tm=t["out_tm"])
        x2d = _moe(x1, h2, route, w1[l], w3[l], w2[l], tm=t["moe_tm"], gather_tm=t["moe_gather_tm"])
    return x2d.reshape(b, seq, d)
```

```python
import functools
import math

import jax
import jax.numpy as jnp
from jax import lax
from jax.experimental import pallas as pl
from jax.experimental.pallas import tpu as pltpu

CHUNK = 64
ATTN_HEADS = 8
HEAD_DIM = 128
KV_HEADS = 2
HEADS_PER_KV = ATTN_HEADS // KV_HEADS
IDX_HEADS = 16
IDX_DIM = 64
TOPK_MAX = 256
RET_HEADS = 8
RET_DK = 128
RET_DV = 128
ROT_BASE = 10000.0
N_GROUPS = 4
EXPERTS_PER_GROUP = 8
N_EXPERTS = N_GROUPS * EXPERTS_PER_GROUP
EPS = 1e-6

ATTN_WIDTH = ATTN_HEADS * HEAD_DIM
KV_WIDTH = KV_HEADS * HEAD_DIM
IDX_WIDTH = IDX_HEADS * IDX_DIM
RET_WIDTH = RET_HEADS * RET_DK

LANES = 128
VMEM_LIMIT = 56 * 1024 * 1024

AQ_OFF = 0
AK_OFF = AQ_OFF + ATTN_WIDTH
AV_OFF = AK_OFF + KV_WIDTH
IQ_OFF = AV_OFF + KV_WIDTH
IK_OFF = IQ_OFF + IDX_WIDTH
IW_OFF = IK_OFF + IDX_DIM
W_RET = IW_OFF + IDX_HEADS
IN_WIDTH = W_RET + 4 * RET_WIDTH
assert IW_OFF // LANES == IK_OFF // LANES
PROJ_TN = 1024
RQ_OFF = -(-W_RET // PROJ_TN) * PROJ_TN
RK_OFF = RQ_OFF + RET_WIDTH
RV_OFF = RK_OFF + RET_WIDTH
RG_OFF = RV_OFF + RET_WIDTH
PROJ_WIDTH = RG_OFF + RET_WIDTH

ROUTE_E1, ROUTE_E2, ROUTE_W1, ROUTE_W2 = 0, 1, 2, 3

SUM_ROWS = 16
LOGIT_BOUND_SLACK = 1.05
MAX_SINGLE_SWEEP_BOUND = 50.0

INT_MIN = -(2 ** 31)
NEG_BIG = -1e30

_NT = (((1,), (1,)), ((), ()))


def _dot(a, b):
    return jnp.dot(a, b, preferred_element_type=jnp.float32)


def _dot_nt(a, b):
    return lax.dot_general(a, b, _NT, preferred_element_type=jnp.float32)


def _params(*sem):
    return pltpu.CompilerParams(dimension_semantics=sem, vmem_limit_bytes=VMEM_LIMIT)


def _in_proj_kernel(x_ref, g_ref, wt_ref, o_ref, h_scr, *, row_chunk):
    @pl.when(pl.program_id(1) == 0)
    def _():
        def body(c, carry):
            rows = pl.ds(pl.multiple_of(c * row_chunk, row_chunk), row_chunk)
            x = x_ref[rows, :]
            ms = jnp.mean(x * x, axis=-1, keepdims=True)
            h_scr[rows, :] = ((x * lax.rsqrt(ms + EPS)) * g_ref[...]).astype(jnp.bfloat16)
            return carry
        lax.fori_loop(0, x_ref.shape[0] // row_chunk, body, 0)

    o_ref[...] = _dot_nt(h_scr[...], wt_ref[...].astype(jnp.bfloat16)).astype(o_ref.dtype)


def _in_proj(x2d, g, w_in_t, *, tm):
    n, d = x2d.shape
    tn = PROJ_TN
    assert w_in_t.shape == (IN_WIDTH, d) and W_RET % 8 == 0
    attn_tiles = RQ_OFF // tn

    def window(i, j):
        start8 = jnp.where(j < attn_tiles, j * (tn // 8), W_RET // 8 + (j - attn_tiles) * (tn // 8))
        return 8 * start8, 0

    return pl.pallas_call(
        functools.partial(_in_proj_kernel, row_chunk=min(tm, 128)),
        out_shape=jax.ShapeDtypeStruct((n, PROJ_WIDTH), jnp.bfloat16),
        grid=(n // tm, PROJ_WIDTH // tn),
        in_specs=[
            pl.BlockSpec((tm, d), lambda i, j: (i, 0)),
            pl.BlockSpec((1, d), lambda i, j: (0, 0)),
            pl.BlockSpec((pl.Element(tn), pl.Element(d)), window),
        ],
        out_specs=pl.BlockSpec((tm, tn), lambda i, j: (i, j)),
        scratch_shapes=[pltpu.VMEM((tm, d), jnp.bfloat16)],
        compiler_params=_params("arbitrary", "arbitrary"),
        name="in_proj",
    )(x2d, g, w_in_t)


def _ordered_float(v):
    bits = v ^ ((v >> 31) & jnp.int32(0x7FFFFFFF))
    return pltpu.bitcast(bits, jnp.float32)


def _attn_kernel(aq_ref, iqa_ref, iqb_ref, iw_ref, ak_ref, av_ref, ik_ref, qg_ref, kg_ref, lnw_ref,
                 lnb_ref, o_ref,
                 kn_scr, ikn_scr, vt_scr, key_scr, wt_scr, qn_scr, acc_scr, s_scr, kmax_scr,
                 *, tk, topk, idx_w_scale):
    i = pl.program_id(1)
    seq = ak_ref.shape[1]
    tq = aq_ref.shape[1]
    chunk_shift = CHUNK.bit_length() - 1

    @pl.when(i == 0)
    def _():
        def body(c, carry):
            rows = pl.ds(pl.multiple_of(c * tk, tk), tk)
            for g in range(KV_HEADS):
                cols = slice(g * HEAD_DIM, (g + 1) * HEAD_DIM)
                k = ak_ref[0, rows, cols].astype(jnp.float32)
                ms = jnp.mean(k * k, axis=-1, keepdims=True)
                kn = (k * lax.rsqrt(ms + EPS)) * kg_ref[...]
                kn_scr[rows, cols] = kn.astype(jnp.bfloat16)
                ksq = jnp.max(jnp.sum(kn * kn, axis=-1, keepdims=True), axis=0, keepdims=True)
                prev = jnp.where(c == 0, 0.0, kmax_scr[g])
                kmax_scr[g] = jnp.maximum(prev, jnp.broadcast_to(ksq, kmax_scr.shape[1:]))
                v = av_ref[0, rows, cols].astype(jnp.float32)
                vt_scr[g, c, :HEAD_DIM, :] = v.T.astype(jnp.bfloat16)
                vt_scr[g, c, HEAD_DIM:, :] = jnp.ones((SUM_ROWS, tk), jnp.bfloat16)
            ki = ik_ref[0, rows, :IDX_DIM].astype(jnp.float32)
            mu = jnp.mean(ki, axis=-1, keepdims=True)
            var = jnp.mean(jnp.square(ki - mu), axis=-1, keepdims=True)
            y = ((ki - mu) * lax.rsqrt(var + EPS) * lnw_ref[...] + lnb_ref[...]).astype(jnp.bfloat16)
            zeros = jnp.zeros_like(y)
            ikn_scr[0, rows, :] = jnp.concatenate([y, zeros], axis=1)
            ikn_scr[1, rows, :] = jnp.concatenate([zeros, y], axis=1)
            return carry
        lax.fori_loop(0, seq // tk, body, 0)

    t0 = i * tq
    n_kt = (t0 + tq) // tk
    scale = (HEAD_DIM ** -0.5) * math.log2(math.e)
    for h in range(ATTN_HEADS):
        g, r = divmod(h, HEADS_PER_KV)
        q = aq_ref[0, :, h * HEAD_DIM:(h + 1) * HEAD_DIM].astype(jnp.float32)
        ms = jnp.mean(q * q, axis=-1, keepdims=True)
        qn_scr[g, r * tq:(r + 1) * tq, :] = (
            (q * lax.rsqrt(ms + EPS)) * qg_ref[...] * scale).astype(jnp.bfloat16)
    ones_rows = jnp.ones((8, HEAD_DIM), jnp.bfloat16)
    bound = []
    for g in range(KV_HEADS):
        qf = qn_scr[g].astype(jnp.float32)
        qsq = _dot_nt(ones_rows, (qf * qf).astype(jnp.bfloat16))[0:1, :]
        kmax = jnp.concatenate([kmax_scr[g, 0:1, :]] * (HEADS_PER_KV * tq // LANES), axis=1)
        bound.append(LOGIT_BOUND_SLACK * jnp.sqrt(qsq * kmax))
    wt_scr[...] = iw_ref[0].astype(jnp.float32).T * idx_w_scale
    w_row = IW_OFF % LANES

    q_chunk = (t0 + lax.broadcasted_iota(jnp.int32, (tk, tq), 1)) >> chunk_shift

    def score_body(kt, carry):
        rows = pl.ds(pl.multiple_of(kt * tk, tk), tk)
        ik_first, ik_second = ikn_scr[0, rows, :], ikn_scr[1, rows, :]
        acc = jnp.zeros((tk, tq), jnp.float32)
        pairs_per_ref = iqa_ref.shape[2] // LANES
        for pair in range(IDX_HEADS // 2):
            src = iqa_ref if pair < pairs_per_ref else iqb_ref
            lane0 = (pair % pairs_per_ref) * LANES
            q_pair = src[0, :, lane0:lane0 + LANES]
            for sub, ik_t in enumerate((ik_first, ik_second)):
                h = 2 * pair + sub
                d = _dot_nt(ik_t, q_pair)
                acc = acc + jnp.maximum(d, 0.0) * wt_scr[w_row + h:w_row + h + 1, :]
        k_chunk = (kt * tk + lax.broadcasted_iota(jnp.int32, (tk, tq), 0)) >> chunk_shift
        key_scr[rows, :] = jnp.where(k_chunk <= q_chunk, acc, -jnp.inf)
        return carry
    lax.fori_loop(0, n_kt, score_body, 0)

    def bit_body(it, lo):
        cand = lo + lax.shift_left(jnp.int32(1), 31 - it)
        cand_f = _ordered_float(cand)

        def count_body(kt, part):
            rows = pl.ds(pl.multiple_of(kt * tk, tk), tk)
            hit = jnp.where(key_scr[rows, :] >= cand_f, 1.0, 0.0).reshape(tk // 8, 8, tq)
            while hit.shape[0] > 1:
                half = hit.shape[0] // 2
                hit = hit[:half] + hit[half:]
            return part + hit[0]
        part = lax.fori_loop(0, n_kt, count_body, jnp.zeros((8, tq), jnp.float32))
        cnt = jnp.sum(part, axis=0, keepdims=True)
        return jnp.where(cnt >= float(topk), cand, lo)
    lo = lax.fori_loop(0, 32, bit_body, jnp.full((1, tq), INT_MIN, jnp.int32))
    thr = jnp.where(lo == INT_MIN, jnp.finfo(jnp.float32).min, _ordered_float(lo))

    acc_scr[...] = jnp.zeros(acc_scr.shape, jnp.float32)

    def masked_logits(kt):
        rows = pl.ds(pl.multiple_of(kt * tk, tk), tk)
        bias = jnp.where(key_scr[rows, :] >= thr, 0.0, NEG_BIG)
        bias = jnp.concatenate([bias] * HEADS_PER_KV, axis=1)
        return [_dot_nt(kn_scr[rows, g * HEAD_DIM:(g + 1) * HEAD_DIM], qn_scr[g]) + bias
                for g in range(KV_HEADS)]

    bound_max = jnp.max(jnp.maximum(bound[0], bound[1]))
    single_sweep = bound_max <= MAX_SINGLE_SWEEP_BOUND

    @pl.when(single_sweep)
    def _():
        def body(kt, carry):
            for g, s in enumerate(masked_logits(kt)):
                acc_scr[g] += _dot(vt_scr[g, kt], jnp.exp2(s - bound[g]).astype(jnp.bfloat16))
            return carry
        lax.fori_loop(0, n_kt, body, 0)

    @pl.when(jnp.logical_not(single_sweep))
    def _():
        def logit_body(kt, m):
            rows = pl.ds(pl.multiple_of(kt * tk, tk), tk)
            new_m = []
            for g, s in enumerate(masked_logits(kt)):
                s_scr[g, rows, :] = s
                new_m.append(jnp.maximum(m[g], jnp.max(s, axis=0, keepdims=True)))
            return tuple(new_m)
        m0 = jnp.full((1, HEADS_PER_KV * tq), NEG_BIG, jnp.float32)
        m = lax.fori_loop(0, n_kt, logit_body, (m0,) * KV_HEADS)

        def pv_body(kt, carry):
            rows = pl.ds(pl.multiple_of(kt * tk, tk), tk)
            for g in range(KV_HEADS):
                p = jnp.exp2(s_scr[g, rows, :] - m[g]).astype(jnp.bfloat16)
                acc_scr[g] += _dot(vt_scr[g, kt], p)
            return carry
        lax.fori_loop(0, n_kt, pv_body, 0)

    for h in range(ATTN_HEADS):
        g, r = divmod(h, HEADS_PER_KV)
        cols = slice(r * tq, (r + 1) * tq)
        o = acc_scr[g, :HEAD_DIM, cols] / acc_scr[g, HEAD_DIM:HEAD_DIM + 1, cols]
        o_ref[0, :, h * HEAD_DIM:(h + 1) * HEAD_DIM] = o.T.astype(o_ref.dtype)


def _dsa_attention(p3, q_g, k_g, ln_w, ln_b, *, tq, tk):
    b, seq, _ = p3.shape
    topk = min(TOPK_MAX, seq // 4)
    idx_w_scale = (IDX_HEADS ** -0.5) * (IDX_DIM ** -0.5)
    assert seq % tq == 0 and tq % tk == 0 and tk % CHUNK == 0

    def col(off, width):
        assert off % width == 0 or width == LANES
        return off // width

    half_iq = IDX_WIDTH // 2
    return pl.pallas_call(
        functools.partial(_attn_kernel, tk=tk, topk=topk, idx_w_scale=idx_w_scale),
        out_shape=jax.ShapeDtypeStruct((b, seq, ATTN_WIDTH), jnp.bfloat16),
        grid=(b, seq // tq),
        in_specs=[
            pl.BlockSpec((1, tq, ATTN_WIDTH), lambda bi, i: (bi, i, col(AQ_OFF, ATTN_WIDTH))),
            pl.BlockSpec((1, tq, half_iq), lambda bi, i: (bi, i, col(IQ_OFF, half_iq))),
            pl.BlockSpec((1, tq, half_iq), lambda bi, i: (bi, i, col(IQ_OFF, half_iq) + 1)),
            pl.BlockSpec((1, tq, LANES), lambda bi, i: (bi, i, col(IW_OFF, LANES))),
            pl.BlockSpec((1, seq, KV_WIDTH), lambda bi, i: (bi, 0, col(AK_OFF, KV_WIDTH))),
            pl.BlockSpec((1, seq, KV_WIDTH), lambda bi, i: (bi, 0, col(AV_OFF, KV_WIDTH))),
            pl.BlockSpec((1, seq, LANES), lambda bi, i: (bi, 0, col(IK_OFF, LANES))),
            pl.BlockSpec((1, HEAD_DIM), lambda bi, i: (0, 0)),
            pl.BlockSpec((1, HEAD_DIM), lambda bi, i: (0, 0)),
            pl.BlockSpec((1, IDX_DIM), lambda bi, i: (0, 0)),
            pl.BlockSpec((1, IDX_DIM), lambda bi, i: (0, 0)),
        ],
        out_specs=pl.BlockSpec((1, tq, ATTN_WIDTH), lambda bi, i: (bi, i, 0)),
        scratch_shapes=[
            pltpu.VMEM((seq, KV_WIDTH), jnp.bfloat16),
            pltpu.VMEM((2, seq, 2 * IDX_DIM), jnp.bfloat16),
            pltpu.VMEM((KV_HEADS, seq // tk, HEAD_DIM + SUM_ROWS, tk), jnp.bfloat16),
            pltpu.VMEM((seq, tq), jnp.float32),
            pltpu.VMEM((LANES, tq), jnp.float32),
            pltpu.VMEM((KV_HEADS, HEADS_PER_KV * tq, HEAD_DIM), jnp.bfloat16),
            pltpu.VMEM((KV_HEADS, HEAD_DIM + SUM_ROWS, HEADS_PER_KV * tq), jnp.float32),
            pltpu.VMEM((KV_HEADS, seq, HEADS_PER_KV * tq), jnp.float32),
            pltpu.VMEM((KV_HEADS, 8, LANES), jnp.float32),
        ],
        compiler_params=_params("arbitrary", "arbitrary"),
        name="dsa_attn",
    )(p3, p3, p3, p3, p3, p3, p3, q_g, k_g, ln_w, ln_b)


RET_HEADS_PER_STEP = 2


def _ret_kernel(lg_ref, rq_ref, rk_ref, rv_ref, rg_ref, cos_ref, sin_ref, g_ref, o_ref, *, rc):
    seq = rq_ref.shape[1]
    n = lax.broadcasted_iota(jnp.int32, (rc, RET_DV), 0).astype(jnp.float32)
    rel = (lax.broadcasted_iota(jnp.int32, (rc, rc), 0)
           - lax.broadcasted_iota(jnp.int32, (rc, rc), 1)).astype(jnp.float32)

    def rot(x, rows):
        return x * cos_ref[rows, :] + pltpu.roll(x, RET_DK // 2, 1) * sin_ref[rows, :]

    heads = []
    for hh in range(RET_HEADS_PER_STEP):
        lg = lg_ref[pl.program_id(1) * RET_HEADS_PER_STEP + hh]
        heads.append(dict(
            cols=slice(hh * RET_DK, (hh + 1) * RET_DK),
            cross_decay=jnp.exp(lg * (n + 1.0)),
            state_decay=jnp.exp(lg * (rc - 1.0 - n)),
            chunk_decay=jnp.exp(lg * jnp.full((RET_DK, RET_DV), float(rc), jnp.float32)),
            intra=jnp.where(rel >= 0, jnp.exp(lg * jnp.maximum(rel, 0.0)), 0.0),
            state=jnp.zeros((RET_DK, RET_DV), jnp.float32),
            gain=g_ref[hh],
        ))

    for c in range(seq // rc):
        rows = slice(c * rc, (c + 1) * rc)
        for hd in heads:
            cols = hd["cols"]
            q = rot(rq_ref[0, rows, cols].astype(jnp.float32), rows)
            k = rot(rk_ref[0, rows, cols].astype(jnp.float32), rows) * (RET_DK ** -0.5)
            v = rv_ref[0, rows, cols]
            qb = q.astype(jnp.bfloat16)
            inner = _dot_nt(qb, k.astype(jnp.bfloat16)) * hd["intra"]
            o = (_dot(inner.astype(jnp.bfloat16), v)
                 + _dot(qb, hd["state"].astype(jnp.bfloat16)) * hd["cross_decay"])
            kd_t = (k * hd["state_decay"]).T.astype(jnp.bfloat16)
            hd["state"] = hd["state"] * hd["chunk_decay"] + _dot(kd_t, v)
            ms = jnp.mean(o * o, axis=-1, keepdims=True)
            y = (o * lax.rsqrt(ms + EPS)) * hd["gain"]
            gate = rg_ref[0, rows, cols].astype(jnp.float32)
            o_ref[0, rows, cols] = ((gate * (1.0 / (1.0 + jnp.exp(-gate)))) * y).astype(o_ref.dtype)


def _retention(p3, log_gamma, cos2, sin2, ret_g, *, rc):
    b, seq, _ = p3.shape
    hps = RET_HEADS_PER_STEP
    width = hps * RET_DK
    assert seq % rc == 0 and RET_HEADS % hps == 0

    def head_spec(off):
        assert off % width == 0
        return pl.BlockSpec((1, seq, width), lambda bi, h: (bi, 0, off // width + h))

    return pl.pallas_call(
        functools.partial(_ret_kernel, rc=rc),
        out_shape=jax.ShapeDtypeStruct((b, seq, RET_WIDTH), jnp.bfloat16),
        grid=(b, RET_HEADS // hps),
        in_specs=[
            pl.BlockSpec(memory_space=pltpu.SMEM),
            head_spec(RQ_OFF), head_spec(RK_OFF), head_spec(RV_OFF), head_spec(RG_OFF),
            pl.BlockSpec((seq, RET_DK), lambda bi, h: (0, 0)),
            pl.BlockSpec((seq, RET_DK), lambda bi, h: (0, 0)),
            pl.BlockSpec((hps, 1, RET_DV), lambda bi, h: (h, 0, 0)),
        ],
        out_specs=pl.BlockSpec((1, seq, width), lambda bi, h: (bi, 0, h)),
        compiler_params=_params("arbitrary", "arbitrary"),
        name="retention",
    )(log_gamma, p3, p3, p3, p3, cos2, sin2, ret_g)


def _routing(logits):
    lane = lax.broadcasted_iota(jnp.int32, logits.shape, 1).astype(jnp.float32)
    big = float(LANES)
    neg = -jnp.inf

    def first_argmax(v, vmax):
        return jnp.min(jnp.where(v == vmax, lane, big), axis=-1, keepdims=True)

    g_mask = (lane >= N_EXPERTS) & (lane < N_EXPERTS + N_GROUPS)
    gl = jnp.where(g_mask, logits, neg)
    g_max = jnp.max(gl, axis=-1, keepdims=True)
    g_sel = first_argmax(gl, g_max) - N_EXPERTS
    g_gate = 1.0 / jnp.sum(jnp.where(g_mask, jnp.exp(gl - g_max), 0.0), axis=-1, keepdims=True)

    e_lo = g_sel * EXPERTS_PER_GROUP
    el = jnp.where((lane >= e_lo) & (lane < e_lo + EXPERTS_PER_GROUP), logits, neg)
    v1 = jnp.max(el, axis=-1, keepdims=True)
    i1 = first_argmax(el, v1)
    el2 = jnp.where(lane == i1, neg, el)
    v2 = jnp.max(el2, axis=-1, keepdims=True)
    i2 = first_argmax(el2, v2)
    e2 = jnp.exp(v2 - v1)
    denom = 1.0 + e2
    w1 = (1.0 / denom) * g_gate
    w2 = (e2 / denom) * g_gate
    route = jnp.where(lane == ROUTE_E1, i1, 0.0) + jnp.where(lane == ROUTE_E2, i2, 0.0)
    return route + jnp.where(lane == ROUTE_W1, w1, 0.0) + jnp.where(lane == ROUTE_W2, w2, 0.0)


def _norm2(x1, g):
    ms = jnp.mean(x1 * x1, axis=-1, keepdims=True)
    return (x1 * lax.rsqrt(ms + EPS)) * g


OUT_SUBTILES = 2


def _out_proj_kernel(a_ref, r_ref, x_ref, wa_ref, wr_ref, g_ref, rcat_ref,
                     x1_ref, h2_ref, route_ref):
    sub = x_ref.shape[0] // OUT_SUBTILES
    for s in range(OUT_SUBTILES):
        rows = slice(s * sub, (s + 1) * sub)
        mixed = _dot(a_ref[rows, :], wa_ref[...]) + _dot(r_ref[rows, :], wr_ref[...])
        x1 = x_ref[rows, :] + mixed
        x1_ref[rows, :] = x1
        h2 = _norm2(x1, g_ref[...])
        h2_ref[rows, :] = h2
        hi = h2.astype(jnp.bfloat16)
        lo = (h2 - hi.astype(jnp.float32)).astype(jnp.bfloat16)
        both = _dot(hi, rcat_ref[...])
        logits = both[:, :LANES] + (both[:, LANES:] + _dot(lo, rcat_ref[:, :LANES]))
        route_ref[rows, :] = _routing(logits)


def _out_proj(attn2d, ret2d, x2d, w_out_bf, g2, r_cat, *, tm):
    n, d = x2d.shape
    return pl.pallas_call(
        _out_proj_kernel,
        out_shape=(
            jax.ShapeDtypeStruct((n, d), jnp.float32),
            jax.ShapeDtypeStruct((n, d), jnp.float32),
            jax.ShapeDtypeStruct((n, LANES), jnp.float32),
        ),
        grid=(n // tm,),
        in_specs=[
            pl.BlockSpec((tm, ATTN_WIDTH), lambda i: (i, 0)),
            pl.BlockSpec((tm, RET_WIDTH), lambda i: (i, 0)),
            pl.BlockSpec((tm, d), lambda i: (i, 0)),
            pl.BlockSpec((ATTN_WIDTH, d), lambda i: (0, 0)),
            pl.BlockSpec((RET_WIDTH, d), lambda i: (ATTN_WIDTH // RET_WIDTH, 0)),
            pl.BlockSpec((1, d), lambda i: (0, 0)),
            pl.BlockSpec((d, 2 * LANES), lambda i: (0, 0)),
        ],
        out_specs=(
            pl.BlockSpec((tm, d), lambda i: (i, 0)),
            pl.BlockSpec((tm, d), lambda i: (i, 0)),
            pl.BlockSpec((tm, LANES), lambda i: (i, 0)),
        ),
        compiler_params=_params("arbitrary"),
        name="out_proj",
    )(attn2d, ret2d, x2d, w_out_bf, w_out_bf, g2, r_cat)


def _plan_kernel(route_ref, pos_ref, cnt_ref, rank_scr, *, tm, blk):
    n = route_ref.shape[0]
    lane = lax.broadcasted_iota(jnp.int32, (blk, LANES), 1).astype(jnp.float32)
    before = (lax.broadcasted_iota(jnp.int32, (blk, blk), 1)
              < lax.broadcasted_iota(jnp.int32, (blk, blk), 0)).astype(jnp.bfloat16)

    def one_hot(rows):
        r = route_ref[rows, :]
        e1 = r[:, ROUTE_E1:ROUTE_E1 + 1]
        e2 = r[:, ROUTE_E2:ROUTE_E2 + 1]
        return lane == e1, lane == e2

    def rank_body(b, run):
        rows = pl.ds(pl.multiple_of(b * blk, blk), blk)
        m1, m2 = one_hot(rows)
        sel = jnp.where(m1 | m2, 1.0, 0.0)
        rank_scr[rows, :] = _dot(before, sel.astype(jnp.bfloat16)) + run
        return run + jnp.sum(sel, axis=0, keepdims=True)
    cnt = lax.fori_loop(0, n // blk, rank_body, jnp.zeros((1, LANES), jnp.float32), unroll=2)
    cnt_ref[...] = jnp.broadcast_to(cnt, cnt_ref.shape)

    tiles = jnp.floor((cnt + (tm - 1.0)) * (1.0 / tm))
    below = (lax.broadcasted_iota(jnp.int32, (LANES, LANES), 0)
             < lax.broadcasted_iota(jnp.int32, (LANES, LANES), 1)).astype(jnp.bfloat16)
    start = _dot(jnp.broadcast_to(tiles, (8, LANES)).astype(jnp.bfloat16), below)[0:1, :] * float(tm)

    def pos_body(b, carry):
        rows = pl.ds(pl.multiple_of(b * blk, blk), blk)
        m1, m2 = one_hot(rows)
        dest = rank_scr[rows, :] + start
        p1 = jnp.sum(jnp.where(m1, dest, 0.0), axis=-1, keepdims=True)
        p2 = jnp.sum(jnp.where(m2, dest, 0.0), axis=-1, keepdims=True)
        pos_ref[rows, :] = (jnp.where(lane == 0.0, p1, 0.0) + jnp.where(lane == 1.0, p2, 0.0)).astype(jnp.int32)
        return carry
    lax.fori_loop(0, n // blk, pos_body, 0, unroll=2)


def _moe_plan(route, *, tm, blk):
    n = route.shape[0]
    return pl.pallas_call(
        functools.partial(_plan_kernel, tm=tm, blk=blk),
        out_shape=(jax.ShapeDtypeStruct((n, LANES), jnp.int32),
                   jax.ShapeDtypeStruct((8, LANES), jnp.float32)),
        scratch_shapes=[pltpu.VMEM((n, LANES), jnp.float32)],
        compiler_params=pltpu.CompilerParams(vmem_limit_bytes=VMEM_LIMIT),
        name="moe_plan",
    )(route)


def _row_copy(src, src_row, dst, dst_row, sem):
    return pltpu.make_async_copy(src.at[pl.ds(src_row, 1), :], dst.at[pl.ds(dst_row, 1), :], sem)


def _invert_kernel(pos_ref, sid_ref):
    def body(j, carry):
        sid_ref[pos_ref[j]] = j
        return carry
    lax.fori_loop(0, pos_ref.shape[0], body, 0, unroll=8)


def _moe_invert(pos, *, rows):
    return pl.pallas_call(
        _invert_kernel,
        out_shape=jax.ShapeDtypeStruct((rows,), jnp.int32),
        in_specs=[pl.BlockSpec(memory_space=pltpu.SMEM)],
        out_specs=pl.BlockSpec(memory_space=pltpu.SMEM),
        name="moe_invert",
    )(pos)


ROW_GROUP = 8
ROW_BLOCK = 32
FFN_ROW_STEP = 64
NO_NEXT, NOT_FIRST = -1, -2


def _ffn_kernel(te_ref, tv_ref, nu_ref, seg_ref, nexte_ref, sid_ref, tok_ref,
                h2_ref, w1_ref, w3_ref, w2_ref, y2_ref,
                xbuf, ybuf, w1buf, w3buf, w2buf, gsem, ssem, wsem):
    t = pl.program_id(0)
    n_used = nu_ref[0]
    tm = xbuf.shape[1]

    def weight_copies(e, wslot):
        return [pltpu.make_async_copy(src.at[e], dst.at[wslot], wsem.at[wslot])
                for src, dst in ((w1_ref, w1buf), (w3_ref, w3buf), (w2_ref, w2buf))]

    def gather(slot):
        return tok_ref, lambda r, tok: _row_copy(h2_ref, tok, xbuf.at[slot], r, gsem.at[slot])

    def scatter(slot):
        return sid_ref, lambda r, sid: _row_copy(ybuf.at[slot], r, y2_ref, sid, ssem.at[slot])

    def start_all(tile, table, copy):
        valid = tv_ref[tile]
        for blk in range(tm // ROW_BLOCK):
            @pl.when(valid >= (blk + 1) * ROW_BLOCK)
            def _():
                for r in range(blk * ROW_BLOCK, (blk + 1) * ROW_BLOCK):
                    copy(r, table[tile * tm + r]).start()
        done = (valid // ROW_BLOCK) * ROW_BLOCK

        def body(c, carry):
            for u in range(ROW_GROUP):
                r = done + c * ROW_GROUP + u

                @pl.when(r < valid)
                def _():
                    copy(r, table[tile * tm + r]).start()
            return carry
        lax.fori_loop(0, (valid - done + (ROW_GROUP - 1)) // ROW_GROUP, body, 0)

    def wait_all(tile, copy, block_copy):
        valid = tv_ref[tile]
        for blk in range(tm // ROW_BLOCK):
            @pl.when(valid >= (blk + 1) * ROW_BLOCK)
            def _():
                block_copy.wait()

        def body(r, carry):
            copy(0, 0).wait()
            return carry
        lax.fori_loop(0, valid % ROW_BLOCK, body, 0)

    def wait_gather(tile, slot):
        wait_all(tile, gather(slot)[1],
                 pltpu.make_async_copy(h2_ref.at[pl.ds(0, ROW_BLOCK), :],
                                       xbuf.at[slot, pl.ds(0, ROW_BLOCK), :], gsem.at[slot]))

    def wait_scatter(tile, slot):
        wait_all(tile, scatter(slot)[1],
                 pltpu.make_async_copy(ybuf.at[slot, pl.ds(0, ROW_BLOCK), :],
                                       y2_ref.at[pl.ds(0, ROW_BLOCK), :], ssem.at[slot]))

    @pl.when(t == 0)
    def _():
        for cp in weight_copies(te_ref[0], 0):
            cp.start()
        start_all(0, *gather(0))

    @pl.when(t < n_used)
    def _():
        slot = t % 2
        wslot = seg_ref[t] % 2

        @pl.when(t + 1 < n_used)
        def _():
            start_all(t + 1, *gather(1 - slot))

        @pl.when(nexte_ref[t] != NOT_FIRST)
        def _():
            for cp in weight_copies(te_ref[t], wslot):
                cp.wait()

            @pl.when(nexte_ref[t] >= 0)
            def _():
                for cp in weight_copies(nexte_ref[t], 1 - wslot):
                    cp.start()

        wait_gather(t, slot)

        @pl.when(t >= 2)
        def _():
            wait_scatter(t - 2, slot)

        valid = tv_ref[t]
        for live in range(FFN_ROW_STEP, tm + 1, FFN_ROW_STEP):
            @pl.when((valid > live - FFN_ROW_STEP) & (valid <= live))
            def _():
                row = lax.broadcasted_iota(jnp.int32, (live, xbuf.shape[2]), 0)
                x = jnp.where(row < valid, xbuf[slot, :live, :], 0.0).astype(jnp.bfloat16)
                a = _dot(x, w1buf[wslot].astype(jnp.bfloat16))
                b = _dot(x, w3buf[wslot].astype(jnp.bfloat16))
                act = (a * (1.0 / (1.0 + jnp.exp(-a)))) * b
                ybuf[slot, :live, :] = _dot(act.astype(jnp.bfloat16), w2buf[wslot].astype(jnp.bfloat16))
        start_all(t, *scatter(slot))

    @pl.when(t == pl.num_programs(0) - 1)
    def _():
        @pl.when(n_used >= 2)
        def _():
            wait_scatter(n_used - 2, n_used % 2)
        wait_scatter(n_used - 1, (n_used - 1) % 2)


def _moe_ffn(tile_expert, tile_valid, n_used, tile_seg, tile_next, row_sid, row_tok, h2, w1, w3, w2,
             *, tm):
    n, d = h2.shape
    _, _, ff = w1.shape
    max_tiles = tile_expert.shape[0]
    any_spec = pl.BlockSpec(memory_space=pl.ANY)
    return pl.pallas_call(
        _ffn_kernel,
        out_shape=jax.ShapeDtypeStruct((2 * n, d), jnp.float32),
        grid_spec=pltpu.PrefetchScalarGridSpec(
            num_scalar_prefetch=7,
            grid=(max_tiles,),
            in_specs=[any_spec, any_spec, any_spec, any_spec],
            out_specs=any_spec,
            scratch_shapes=[
                pltpu.VMEM((2, tm, d), jnp.float32),
                pltpu.VMEM((2, tm, d), jnp.float32),
                pltpu.VMEM((2, d, ff), jnp.float32),
                pltpu.VMEM((2, d, ff), jnp.float32),
                pltpu.VMEM((2, ff, d), jnp.float32),
                pltpu.SemaphoreType.DMA((2,)),
                pltpu.SemaphoreType.DMA((2,)),
                pltpu.SemaphoreType.DMA((2,)),
            ],
        ),
        compiler_params=_params("arbitrary"),
        name="moe_ffn",
    )(tile_expert, tile_valid, n_used, tile_seg, tile_next, row_sid, row_tok, h2, w1, w3, w2)


def _combine_kernel(x1_ref, route_ref, ya_ref, yb_ref, o_ref):
    route = route_ref[...]
    w1 = route[:, ROUTE_W1:ROUTE_W1 + 1]
    w2 = route[:, ROUTE_W2:ROUTE_W2 + 1]
    o_ref[...] = x1_ref[...] + (w1 * ya_ref[...] + w2 * yb_ref[...])


def _moe_combine(x1, route, y2, *, tm):
    n, d = x1.shape
    return pl.pallas_call(
        _combine_kernel,
        out_shape=jax.ShapeDtypeStruct((n, d), jnp.float32),
        grid=(n // tm,),
        in_specs=[pl.BlockSpec((tm, d), lambda i: (i, 0)),
                  pl.BlockSpec((tm, LANES), lambda i: (i, 0)),
                  pl.BlockSpec((tm, d), lambda i: (i, 0)),
                  pl.BlockSpec((tm, d), lambda i: (i + n // tm, 0))],
        out_specs=pl.BlockSpec((tm, d), lambda i: (i, 0)),
        compiler_params=_params("arbitrary"),
        name="moe_combine",
    )(x1, route, y2, y2)


def _moe(x1, h2, route, w1, w3, w2, *, tm, gather_tm):
    n, d = x1.shape
    max_tiles = (2 * n) // tm + N_EXPERTS
    pos2d, cnt = _moe_plan(route, tm=tm, blk=gather_tm)
    pos = pos2d[:, :2].T.reshape(2 * n)
    counts = cnt[0, :N_EXPERTS].astype(jnp.int32)
    tiles = (counts + (tm - 1)) // tm
    ends = jnp.cumsum(tiles)
    t_idx = jnp.arange(max_tiles, dtype=jnp.int32)
    tile_expert = jnp.sum((ends[None, :] <= t_idx[:, None]).astype(jnp.int32), axis=1)
    tile_expert = jnp.minimum(tile_expert, N_EXPERTS - 1)
    first_tile = (ends - tiles)[tile_expert]
    tile_valid = jnp.clip(counts[tile_expert] - (t_idx - first_tile) * tm, 0, tm).astype(jnp.int32)
    n_used = ends[-1:].astype(jnp.int32)
    used = t_idx < n_used[0]
    is_first = used & (t_idx == first_tile)
    tile_seg = (jnp.cumsum(is_first.astype(jnp.int32)) - 1).astype(jnp.int32)
    next_start = first_tile + tiles[tile_expert]
    next_e = jnp.where(next_start < n_used[0], tile_expert[jnp.minimum(next_start, max_tiles - 1)], NO_NEXT)
    tile_next = jnp.where(is_first, next_e, NOT_FIRST).astype(jnp.int32)

    row_sid = _moe_invert(pos, rows=max_tiles * tm)
    row_tok = jnp.where(row_sid >= n, row_sid - n, row_sid)
    y2 = _moe_ffn(tile_expert, tile_valid, n_used, tile_seg, tile_next, row_sid, row_tok, h2, w1, w3, w2,
                  tm=tm)
    return _moe_combine(x1, route, y2, tm=gather_tm)


def _pack_router(w_group, w_router):
    d = w_group.shape[0]
    experts = jnp.transpose(w_router, (1, 0, 2)).reshape(d, N_EXPERTS)
    wr = jnp.concatenate(
        [experts, w_group, jnp.zeros((d, LANES - N_EXPERTS - N_GROUPS), w_group.dtype)], axis=1)
    hi = wr.astype(jnp.bfloat16)
    lo = (wr - hi.astype(jnp.float32)).astype(jnp.bfloat16)
    return jnp.concatenate([hi, lo], axis=1)


def _rotation_tables(seq):
    half = RET_DK // 2
    pos = jnp.arange(seq, dtype=jnp.float32)
    inv = 1.0 / (ROT_BASE ** jnp.linspace(0.0, 1.0, half, dtype=jnp.float32))
    ang = pos[:, None] * inv[None, :]
    c, s = jnp.cos(ang), jnp.sin(ang)
    return jnp.concatenate([c, c], axis=-1), jnp.concatenate([-s, s], axis=-1)


def _tiles(n, seq):
    def fit(total, want):
        t = min(total, want)
        while total % t:
            t //= 2
        return t
    return dict(
        proj_tm=fit(n, 1024),
        attn_tq=fit(seq, 256), attn_tk=fit(seq, 256),
        ret_rc=fit(seq, 256),
        out_tm=fit(n, 512),
        moe_tm=fit(n, 256), moe_gather_tm=fit(n, 256),
    )


def kernel(x, norm1_g, w_in, q_norm_g, k_norm_g, idx_k_ln_w, idx_k_ln_b, ret_norm_g,
           w_out, norm2_g, w_group, w_router, w1, w3, w2):
    b, seq, d = x.shape
    n = b * seq
    depth = w_in.shape[0]
    t = _tiles(n, seq)
    cos2, sin2 = _rotation_tables(seq)
    log_gamma = jnp.log1p(-jnp.exp2(-5.0 - jnp.arange(RET_HEADS, dtype=jnp.float32)))

    x2d = x.reshape(n, d)
    for l in range(depth):
        proj = _in_proj(x2d, norm1_g[l][None, :], w_in[l].T, tm=t["proj_tm"])
        p3 = proj.reshape(b, seq, proj.shape[1])
        attn = _dsa_attention(p3, q_norm_g[l][None, :], k_norm_g[l][None, :],
                              idx_k_ln_w[l][None, :], idx_k_ln_b[l][None, :],
                              tq=t["attn_tq"], tk=t["attn_tk"])
        ret = _retention(p3, log_gamma, cos2, sin2, ret_norm_g[l].reshape(RET_HEADS, 1, RET_DV),
                         rc=t["ret_rc"])
        r_cat = _pack_router(w_group[l], w_router[l])
        g2 = norm2_g[l][None, :]
        x1, h2, route = _out_proj(attn.reshape(n, ATTN_WIDTH), ret.reshape(n, RET_WIDTH), x2d,
                                  w_out[l].astype(jnp.bfloat16), g2, r_cat, tm=t["out_tm"])
        x2d = _moe(x1, h2, route, w1[l], w3[l], w2[l], tm=t["moe_tm"], gather_tm=t["moe_gather_tm"])
    return x2d.reshape(b, seq, d)
```

```python
import functools
import math

import jax
import jax.numpy as jnp
from jax import lax
from jax.experimental import pallas as pl
from jax.experimental.pallas import tpu as pltpu

CHUNK = 64
ATTN_HEADS = 8
HEAD_DIM = 128
KV_HEADS = 2
HEADS_PER_KV = ATTN_HEADS // KV_HEADS
IDX_HEADS = 16
IDX_DIM = 64
TOPK_MAX = 256
RET_HEADS = 8
RET_DK = 128
RET_DV = 128
ROT_BASE = 10000.0
N_GROUPS = 4
EXPERTS_PER_GROUP = 8
N_EXPERTS = N_GROUPS * EXPERTS_PER_GROUP
EPS = 1e-6

ATTN_WIDTH = ATTN_HEADS * HEAD_DIM
KV_WIDTH = KV_HEADS * HEAD_DIM
IDX_WIDTH = IDX_HEADS * IDX_DIM
RET_WIDTH = RET_HEADS * RET_DK

LANES = 128
SUBLANES = 8
VMEM_LIMIT = 56 * 1024 * 1024

AQ_OFF = 0
AK_OFF = AQ_OFF + ATTN_WIDTH
AV_OFF = AK_OFF + KV_WIDTH
IQ_OFF = AV_OFF + KV_WIDTH
IK_OFF = IQ_OFF + IDX_WIDTH
IW_OFF = IK_OFF + IDX_DIM
W_RET = IW_OFF + IDX_HEADS
IN_WIDTH = W_RET + 4 * RET_WIDTH
assert IW_OFF // LANES == IK_OFF // LANES
PROJ_TN = 1024
RQ_OFF = -(-W_RET // PROJ_TN) * PROJ_TN
RK_OFF = RQ_OFF + RET_WIDTH
RV_OFF = RK_OFF + RET_WIDTH
RG_OFF = RV_OFF + RET_WIDTH
PROJ_WIDTH = RG_OFF + RET_WIDTH

ROUTE_E1, ROUTE_E2, ROUTE_W1, ROUTE_W2 = 0, 1, 2, 3

SUM_ROWS = 16
LOGIT_BOUND_SLACK = 1.05
MAX_SINGLE_SWEEP_BOUND = 50.0

INT_MIN = -(2 ** 31)
NEG_BIG = -1e30

_NT = (((1,), (1,)), ((), ()))


def _dot(a, b):
    return jnp.dot(a, b, preferred_element_type=jnp.float32)


def _dot_nt(a, b):
    return lax.dot_general(a, b, _NT, preferred_element_type=jnp.float32)


def _params(*sem):
    return pltpu.CompilerParams(dimension_semantics=sem, vmem_limit_bytes=VMEM_LIMIT)


def _in_proj_kernel(x_ref, g_ref, wt_ref, o_ref, h_scr, *, row_chunk):
    w = wt_ref[...].astype(jnp.bfloat16)
    first = pl.program_id(1) == 0

    @pl.when(first)
    def _():
        for c in range(x_ref.shape[0] // row_chunk):
            rows = slice(c * row_chunk, (c + 1) * row_chunk)
            x = x_ref[rows, :]
            ms = jnp.mean(x * x, axis=-1, keepdims=True)
            h = ((x * lax.rsqrt(ms + EPS)) * g_ref[...]).astype(jnp.bfloat16)
            h_scr[rows, :] = h
            o_ref[rows, :] = _dot_nt(h, w).astype(o_ref.dtype)

    @pl.when(jnp.logical_not(first))
    def _():
        o_ref[...] = _dot_nt(h_scr[...], w).astype(o_ref.dtype)


def _in_proj(x2d, g, w_in_t, *, tm):
    n, d = x2d.shape
    tn = PROJ_TN
    assert w_in_t.shape == (IN_WIDTH, d) and W_RET % SUBLANES == 0 and tn % SUBLANES == 0
    attn_tiles = RQ_OFF // tn

    def window(i, j):
        step = tn // SUBLANES
        start = jnp.where(j < attn_tiles, j * step, W_RET // SUBLANES + (j - attn_tiles) * step)
        return SUBLANES * start, 0

    return pl.pallas_call(
        functools.partial(_in_proj_kernel, row_chunk=min(tm, 256)),
        out_shape=jax.ShapeDtypeStruct((n, PROJ_WIDTH), jnp.bfloat16),
        grid=(n // tm, PROJ_WIDTH // tn),
        in_specs=[
            pl.BlockSpec((tm, d), lambda i, j: (i, 0)),
            pl.BlockSpec((1, d), lambda i, j: (0, 0)),
            pl.BlockSpec((pl.Element(tn), pl.Element(d)), window),
        ],
        out_specs=pl.BlockSpec((tm, tn), lambda i, j: (i, j)),
        scratch_shapes=[pltpu.VMEM((tm, d), jnp.bfloat16)],
        compiler_params=_params("arbitrary", "arbitrary"),
        name="in_proj",
    )(x2d, g, w_in_t)


def _ordered_float(v):
    bits = v ^ ((v >> 31) & jnp.int32(0x7FFFFFFF))
    return pltpu.bitcast(bits, jnp.float32)


def _attn_kernel(aq_ref, iqa_ref, iqb_ref, iw_ref, ak_ref, av_ref, ik_ref, qg_ref, kg_ref, lnw_ref,
                 lnb_ref, o_ref,
                 kn_scr, ikn_scr, vt_scr, key_scr, wt_scr, qn_scr, acc_scr, s_scr, kmax_scr,
                 *, tk, topk, idx_w_scale):
    i = pl.program_id(1)
    seq = ak_ref.shape[1]
    tq = aq_ref.shape[1]
    chunk_shift = CHUNK.bit_length() - 1

    @pl.when(i == 0)
    def _():
        def body(c, carry):
            rows = pl.ds(pl.multiple_of(c * tk, tk), tk)
            for g in range(KV_HEADS):
                cols = slice(g * HEAD_DIM, (g + 1) * HEAD_DIM)
                k = ak_ref[0, rows, cols].astype(jnp.float32)
                ms = jnp.mean(k * k, axis=-1, keepdims=True)
                kn = (k * lax.rsqrt(ms + EPS)) * kg_ref[...]
                kn_scr[rows, cols] = kn.astype(jnp.bfloat16)
                ksq = jnp.max(jnp.sum(kn * kn, axis=-1, keepdims=True), axis=0, keepdims=True)
                prev = jnp.where(c == 0, 0.0, kmax_scr[g])
                kmax_scr[g] = jnp.maximum(prev, jnp.broadcast_to(ksq, kmax_scr.shape[1:]))
                v = av_ref[0, rows, cols].astype(jnp.float32)
                vt_scr[g, c, :HEAD_DIM, :] = v.T.astype(jnp.bfloat16)
                vt_scr[g, c, HEAD_DIM:, :] = jnp.ones((SUM_ROWS, tk), jnp.bfloat16)
            ki = ik_ref[0, rows, :IDX_DIM].astype(jnp.float32)
            mu = jnp.mean(ki, axis=-1, keepdims=True)
            var = jnp.mean(jnp.square(ki - mu), axis=-1, keepdims=True)
            y = ((ki - mu) * lax.rsqrt(var + EPS) * lnw_ref[...] + lnb_ref[...]).astype(jnp.bfloat16)
            zeros = jnp.zeros_like(y)
            ikn_scr[0, rows, :] = jnp.concatenate([y, zeros], axis=1)
            ikn_scr[1, rows, :] = jnp.concatenate([zeros, y], axis=1)
            return carry
        lax.fori_loop(0, seq // tk, body, 0)

    t0 = i * tq
    n_kt = (t0 + tq) // tk
    scale = (HEAD_DIM ** -0.5) * math.log2(math.e)
    for h in range(ATTN_HEADS):
        g, r = divmod(h, HEADS_PER_KV)
        q = aq_ref[0, :, h * HEAD_DIM:(h + 1) * HEAD_DIM].astype(jnp.float32)
        ms = jnp.mean(q * q, axis=-1, keepdims=True)
        qn_scr[g, r * tq:(r + 1) * tq, :] = (
            (q * lax.rsqrt(ms + EPS)) * qg_ref[...] * scale).astype(jnp.bfloat16)
    ones_rows = jnp.ones((8, HEAD_DIM), jnp.bfloat16)
    bound = []
    for g in range(KV_HEADS):
        qf = qn_scr[g].astype(jnp.float32)
        qsq = _dot_nt(ones_rows, (qf * qf).astype(jnp.bfloat16))[0:1, :]
        kmax = jnp.concatenate([kmax_scr[g, 0:1, :]] * (HEADS_PER_KV * tq // LANES), axis=1)
        bound.append(LOGIT_BOUND_SLACK * jnp.sqrt(qsq * kmax))
    wt_scr[...] = iw_ref[0].astype(jnp.float32).T * idx_w_scale
    w_row = IW_OFF % LANES

    q_chunk = (t0 + lax.broadcasted_iota(jnp.int32, (tk, tq), 1)) >> chunk_shift

    def score_body(kt, carry):
        rows = pl.ds(pl.multiple_of(kt * tk, tk), tk)
        ik_first, ik_second = ikn_scr[0, rows, :], ikn_scr[1, rows, :]
        acc = jnp.zeros((tk, tq), jnp.float32)
        pairs_per_ref = iqa_ref.shape[2] // LANES
        for pair in range(IDX_HEADS // 2):
            src = iqa_ref if pair < pairs_per_ref else iqb_ref
            lane0 = (pair % pairs_per_ref) * LANES
            q_pair = src[0, :, lane0:lane0 + LANES]
            for sub, ik_t in enumerate((ik_first, ik_second)):
                h = 2 * pair + sub
                d = _dot_nt(ik_t, q_pair)
                acc = acc + jnp.maximum(d, 0.0) * wt_scr[w_row + h:w_row + h + 1, :]
        k_chunk = (kt * tk + lax.broadcasted_iota(jnp.int32, (tk, tq), 0)) >> chunk_shift
        key_scr[rows, :] = jnp.where(k_chunk <= q_chunk, acc, -jnp.inf)
        return carry
    lax.fori_loop(0, n_kt, score_body, 0)

    def bit_body(it, lo):
        cand = lo + lax.shift_left(jnp.int32(1), 31 - it)
        cand_f = _ordered_float(cand)

        def count_body(kt, part):
            rows = pl.ds(pl.multiple_of(kt * tk, tk), tk)
            hit = jnp.where(key_scr[rows, :] >= cand_f, 1.0, 0.0).reshape(tk // 8, 8, tq)
            while hit.shape[0] > 1:
                half = hit.shape[0] // 2
                hit = hit[:half] + hit[half:]
            return part + hit[0]
        part = lax.fori_loop(0, n_kt, count_body, jnp.zeros((8, tq), jnp.float32))
        cnt = jnp.sum(part, axis=0, keepdims=True)
        return jnp.where(cnt >= float(topk), cand, lo)
    lo = lax.fori_loop(0, 32, bit_body, jnp.full((1, tq), INT_MIN, jnp.int32))
    thr = jnp.where(lo == INT_MIN, jnp.finfo(jnp.float32).min, _ordered_float(lo))

    acc_scr[...] = jnp.zeros(acc_scr.shape, jnp.float32)

    def masked_logits(kt):
        rows = pl.ds(pl.multiple_of(kt * tk, tk), tk)
        bias = jnp.where(key_scr[rows, :] >= thr, 0.0, NEG_BIG)
        bias = jnp.concatenate([bias] * HEADS_PER_KV, axis=1)
        return [_dot_nt(kn_scr[rows, g * HEAD_DIM:(g + 1) * HEAD_DIM], qn_scr[g]) + bias
                for g in range(KV_HEADS)]

    bound_max = jnp.max(jnp.maximum(bound[0], bound[1]))
    single_sweep = bound_max <= MAX_SINGLE_SWEEP_BOUND

    @pl.when(single_sweep)
    def _():
        def body(kt, carry):
            for g, s in enumerate(masked_logits(kt)):
                acc_scr[g] += _dot(vt_scr[g, kt], jnp.exp2(s - bound[g]).astype(jnp.bfloat16))
            return carry
        lax.fori_loop(0, n_kt, body, 0)

    @pl.when(jnp.logical_not(single_sweep))
    def _():
        def logit_body(kt, m):
            rows = pl.ds(pl.multiple_of(kt * tk, tk), tk)
            new_m = []
            for g, s in enumerate(masked_logits(kt)):
                s_scr[g, rows, :] = s
                new_m.append(jnp.maximum(m[g], jnp.max(s, axis=0, keepdims=True)))
            return tuple(new_m)
        m0 = jnp.full((1, HEADS_PER_KV * tq), NEG_BIG, jnp.float32)
        m = lax.fori_loop(0, n_kt, logit_body, (m0,) * KV_HEADS)

        def pv_body(kt, carry):
            rows = pl.ds(pl.multiple_of(kt * tk, tk), tk)
            for g in range(KV_HEADS):
                p = jnp.exp2(s_scr[g, rows, :] - m[g]).astype(jnp.bfloat16)
                acc_scr[g] += _dot(vt_scr[g, kt], p)
            return carry
        lax.fori_loop(0, n_kt, pv_body, 0)

    for h in range(ATTN_HEADS):
        g, r = divmod(h, HEADS_PER_KV)
        cols = slice(r * tq, (r + 1) * tq)
        o = acc_scr[g, :HEAD_DIM, cols] / acc_scr[g, HEAD_DIM:HEAD_DIM + 1, cols]
        o_ref[0, :, h * HEAD_DIM:(h + 1) * HEAD_DIM] = o.T.astype(o_ref.dtype)


def _dsa_attention(p3, q_g, k_g, ln_w, ln_b, *, tq, tk):
    b, seq, _ = p3.shape
    topk = min(TOPK_MAX, seq // 4)
    idx_w_scale = (IDX_HEADS ** -0.5) * (IDX_DIM ** -0.5)
    assert seq % tq == 0 and tq % tk == 0 and tk % CHUNK == 0

    def col(off, width):
        assert off % width == 0 or width == LANES
        return off // width

    half_iq = IDX_WIDTH // 2
    return pl.pallas_call(
        functools.partial(_attn_kernel, tk=tk, topk=topk, idx_w_scale=idx_w_scale),
        out_shape=jax.ShapeDtypeStruct((b, seq, ATTN_WIDTH), jnp.bfloat16),
        grid=(b, seq // tq),
        in_specs=[
            pl.BlockSpec((1, tq, ATTN_WIDTH), lambda bi, i: (bi, i, col(AQ_OFF, ATTN_WIDTH))),
            pl.BlockSpec((1, tq, half_iq), lambda bi, i: (bi, i, col(IQ_OFF, half_iq))),
            pl.BlockSpec((1, tq, half_iq), lambda bi, i: (bi, i, col(IQ_OFF, half_iq) + 1)),
            pl.BlockSpec((1, tq, LANES), lambda bi, i: (bi, i, col(IW_OFF, LANES))),
            pl.BlockSpec((1, seq, KV_WIDTH), lambda bi, i: (bi, 0, col(AK_OFF, KV_WIDTH))),
            pl.BlockSpec((1, seq, KV_WIDTH), lambda bi, i: (bi, 0, col(AV_OFF, KV_WIDTH))),
            pl.BlockSpec((1, seq, LANES), lambda bi, i: (bi, 0, col(IK_OFF, LANES))),
            pl.BlockSpec((1, HEAD_DIM), lambda bi, i: (0, 0)),
            pl.BlockSpec((1, HEAD_DIM), lambda bi, i: (0, 0)),
            pl.BlockSpec((1, IDX_DIM), lambda bi, i: (0, 0)),
            pl.BlockSpec((1, IDX_DIM), lambda bi, i: (0, 0)),
        ],
        out_specs=pl.BlockSpec((1, tq, ATTN_WIDTH), lambda bi, i: (bi, i, 0)),
        scratch_shapes=[
            pltpu.VMEM((seq, KV_WIDTH), jnp.bfloat16),
            pltpu.VMEM((2, seq, 2 * IDX_DIM), jnp.bfloat16),
            pltpu.VMEM((KV_HEADS, seq // tk, HEAD_DIM + SUM_ROWS, tk), jnp.bfloat16),
            pltpu.VMEM((seq, tq), jnp.float32),
            pltpu.VMEM((LANES, tq), jnp.float32),
            pltpu.VMEM((KV_HEADS, HEADS_PER_KV * tq, HEAD_DIM), jnp.bfloat16),
            pltpu.VMEM((KV_HEADS, HEAD_DIM + SUM_ROWS, HEADS_PER_KV * tq), jnp.float32),
            pltpu.VMEM((KV_HEADS, seq, HEADS_PER_KV * tq), jnp.float32),
            pltpu.VMEM((KV_HEADS, 8, LANES), jnp.float32),
        ],
        compiler_params=_params("arbitrary", "arbitrary"),
        name="dsa_attn",
    )(p3, p3, p3, p3, p3, p3, p3, q_g, k_g, ln_w, ln_b)


RET_HEADS_PER_STEP = 2


def _ret_kernel(lg_ref, rq_ref, rk_ref, rv_ref, rg_ref, cos_ref, sin_ref, g_ref, o_ref, *, rc):
    seq = rq_ref.shape[1]
    n = lax.broadcasted_iota(jnp.int32, (rc, RET_DV), 0).astype(jnp.float32)
    rel = (lax.broadcasted_iota(jnp.int32, (rc, rc), 0)
           - lax.broadcasted_iota(jnp.int32, (rc, rc), 1)).astype(jnp.float32)

    def rot(x, rows):
        return x * cos_ref[rows, :] + pltpu.roll(x, RET_DK // 2, 1) * sin_ref[rows, :]

    heads = []
    for hh in range(RET_HEADS_PER_STEP):
        lg = lg_ref[pl.program_id(1) * RET_HEADS_PER_STEP + hh]
        heads.append(dict(
            cols=slice(hh * RET_DK, (hh + 1) * RET_DK),
            cross_decay=jnp.exp(lg * (n + 1.0)),
            state_decay=jnp.exp(lg * (rc - 1.0 - n)),
            chunk_decay=jnp.exp(lg * jnp.full((RET_DK, RET_DV), float(rc), jnp.float32)),
            intra=jnp.where(rel >= 0, jnp.exp(lg * jnp.maximum(rel, 0.0)), 0.0),
            state=jnp.zeros((RET_DK, RET_DV), jnp.float32),
            gain=g_ref[hh],
        ))

    for c in range(seq // rc):
        rows = slice(c * rc, (c + 1) * rc)
        for hd in heads:
            cols = hd["cols"]
            q = rot(rq_ref[0, rows, cols].astype(jnp.float32), rows)
            k = rot(rk_ref[0, rows, cols].astype(jnp.float32), rows) * (RET_DK ** -0.5)
            v = rv_ref[0, rows, cols]
            qb = q.astype(jnp.bfloat16)
            inner = _dot_nt(qb, k.astype(jnp.bfloat16)) * hd["intra"]
            o = (_dot(inner.astype(jnp.bfloat16), v)
                 + _dot(qb, hd["state"].astype(jnp.bfloat16)) * hd["cross_decay"])
            kd_t = (k * hd["state_decay"]).T.astype(jnp.bfloat16)
            hd["state"] = hd["state"] * hd["chunk_decay"] + _dot(kd_t, v)
            ms = jnp.mean(o * o, axis=-1, keepdims=True)
            y = (o * lax.rsqrt(ms + EPS)) * hd["gain"]
            gate = rg_ref[0, rows, cols].astype(jnp.float32)
            o_ref[0, rows, cols] = ((gate * (1.0 / (1.0 + jnp.exp(-gate)))) * y).astype(o_ref.dtype)


def _retention(p3, log_gamma, cos2, sin2, ret_g, *, rc):
    b, seq, _ = p3.shape
    hps = RET_HEADS_PER_STEP
    width = hps * RET_DK
    assert seq % rc == 0 and RET_HEADS % hps == 0

    def head_spec(off):
        assert off % width == 0
        return pl.BlockSpec((1, seq, width), lambda bi, h: (bi, 0, off // width + h))

    return pl.pallas_call(
        functools.partial(_ret_kernel, rc=rc),
        out_shape=jax.ShapeDtypeStruct((b, seq, RET_WIDTH), jnp.bfloat16),
        grid=(b, RET_HEADS // hps),
        in_specs=[
            pl.BlockSpec(memory_space=pltpu.SMEM),
            head_spec(RQ_OFF), head_spec(RK_OFF), head_spec(RV_OFF), head_spec(RG_OFF),
            pl.BlockSpec((seq, RET_DK), lambda bi, h: (0, 0)),
            pl.BlockSpec((seq, RET_DK), lambda bi, h: (0, 0)),
            pl.BlockSpec((hps, 1, RET_DV), lambda bi, h: (h, 0, 0)),
        ],
        out_specs=pl.BlockSpec((1, seq, width), lambda bi, h: (bi, 0, h)),
        compiler_params=_params("arbitrary", "arbitrary"),
        name="retention",
    )(log_gamma, p3, p3, p3, p3, cos2, sin2, ret_g)


def _routing(logits):
    lane = lax.broadcasted_iota(jnp.int32, logits.shape, 1).astype(jnp.float32)
    big = float(LANES)
    neg = -jnp.inf

    def first_argmax(v, vmax):
        return jnp.min(jnp.where(v == vmax, lane, big), axis=-1, keepdims=True)

    g_mask = (lane >= N_EXPERTS) & (lane < N_EXPERTS + N_GROUPS)
    gl = jnp.where(g_mask, logits, neg)
    g_max = jnp.max(gl, axis=-1, keepdims=True)
    g_sel = first_argmax(gl, g_max) - N_EXPERTS
    g_gate = 1.0 / jnp.sum(jnp.where(g_mask, jnp.exp(gl - g_max), 0.0), axis=-1, keepdims=True)

    e_lo = g_sel * EXPERTS_PER_GROUP
    el = jnp.where((lane >= e_lo) & (lane < e_lo + EXPERTS_PER_GROUP), logits, neg)
    v1 = jnp.max(el, axis=-1, keepdims=True)
    i1 = first_argmax(el, v1)
    el2 = jnp.where(lane == i1, neg, el)
    v2 = jnp.max(el2, axis=-1, keepdims=True)
    i2 = first_argmax(el2, v2)
    e2 = jnp.exp(v2 - v1)
    denom = 1.0 + e2
    w1 = (1.0 / denom) * g_gate
    w2 = (e2 / denom) * g_gate
    route = jnp.where(lane == ROUTE_E1, i1, 0.0) + jnp.where(lane == ROUTE_E2, i2, 0.0)
    return route + jnp.where(lane == ROUTE_W1, w1, 0.0) + jnp.where(lane == ROUTE_W2, w2, 0.0)


def _norm2(x1, g):
    ms = jnp.mean(x1 * x1, axis=-1, keepdims=True)
    return (x1 * lax.rsqrt(ms + EPS)) * g


OUT_SUBTILES = 2


def _out_proj_kernel(a_ref, r_ref, x_ref, wa_ref, wr_ref, g_ref, rcat_ref,
                     x1_ref, h2_ref, route_ref):
    sub = x_ref.shape[0] // OUT_SUBTILES
    for s in range(OUT_SUBTILES):
        rows = slice(s * sub, (s + 1) * sub)
        mixed = _dot(a_ref[rows, :], wa_ref[...]) + _dot(r_ref[rows, :], wr_ref[...])
        x1 = x_ref[rows, :] + mixed
        x1_ref[rows, :] = x1
        h2 = _norm2(x1, g_ref[...])
        h2_ref[rows, :] = h2
        hi = h2.astype(jnp.bfloat16)
        lo = (h2 - hi.astype(jnp.float32)).astype(jnp.bfloat16)
        both = _dot(hi, rcat_ref[...])
        logits = both[:, :LANES] + (both[:, LANES:] + _dot(lo, rcat_ref[:, :LANES]))
        route_ref[rows, :] = _routing(logits)


def _out_proj(attn2d, ret2d, x2d, w_out_bf, g2, r_cat, *, tm):
    n, d = x2d.shape
    return pl.pallas_call(
        _out_proj_kernel,
        out_shape=(
            jax.ShapeDtypeStruct((n, d), jnp.float32),
            jax.ShapeDtypeStruct((n, d), jnp.float32),
            jax.ShapeDtypeStruct((n, LANES), jnp.float32),
        ),
        grid=(n // tm,),
        in_specs=[
            pl.BlockSpec((tm, ATTN_WIDTH), lambda i: (i, 0)),
            pl.BlockSpec((tm, RET_WIDTH), lambda i: (i, 0)),
            pl.BlockSpec((tm, d), lambda i: (i, 0)),
            pl.BlockSpec((ATTN_WIDTH, d), lambda i: (0, 0)),
            pl.BlockSpec((RET_WIDTH, d), lambda i: (ATTN_WIDTH // RET_WIDTH, 0)),
            pl.BlockSpec((1, d), lambda i: (0, 0)),
            pl.BlockSpec((d, 2 * LANES), lambda i: (0, 0)),
        ],
        out_specs=(
            pl.BlockSpec((tm, d), lambda i: (i, 0)),
            pl.BlockSpec((tm, d), lambda i: (i, 0)),
            pl.BlockSpec((tm, LANES), lambda i: (i, 0)),
        ),
        compiler_params=_params("arbitrary"),
        name="out_proj",
    )(attn2d, ret2d, x2d, w_out_bf, w_out_bf, g2, r_cat)


def _plan_kernel(route_ref, pos_ref, cnt_ref, rank_scr, *, tm, blk):
    n = route_ref.shape[0]
    lane = lax.broadcasted_iota(jnp.int32, (blk, LANES), 1).astype(jnp.float32)
    before = (lax.broadcasted_iota(jnp.int32, (blk, blk), 1)
              < lax.broadcasted_iota(jnp.int32, (blk, blk), 0)).astype(jnp.bfloat16)

    def one_hot(rows):
        r = route_ref[rows, :]
        e1 = r[:, ROUTE_E1:ROUTE_E1 + 1]
        e2 = r[:, ROUTE_E2:ROUTE_E2 + 1]
        return lane == e1, lane == e2

    def rank_body(b, run):
        rows = pl.ds(pl.multiple_of(b * blk, blk), blk)
        m1, m2 = one_hot(rows)
        sel = jnp.where(m1 | m2, 1.0, 0.0)
        rank_scr[rows, :] = _dot(before, sel.astype(jnp.bfloat16)) + run
        return run + jnp.sum(sel, axis=0, keepdims=True)
    cnt = lax.fori_loop(0, n // blk, rank_body, jnp.zeros((1, LANES), jnp.float32), unroll=2)
    cnt_ref[...] = jnp.broadcast_to(cnt, cnt_ref.shape)

    tiles = jnp.floor((cnt + (tm - 1.0)) * (1.0 / tm))
    below = (lax.broadcasted_iota(jnp.int32, (LANES, LANES), 0)
             < lax.broadcasted_iota(jnp.int32, (LANES, LANES), 1)).astype(jnp.bfloat16)
    start = _dot(jnp.broadcast_to(tiles, (8, LANES)).astype(jnp.bfloat16), below)[0:1, :] * float(tm)

    def pos_body(b, carry):
        rows = pl.ds(pl.multiple_of(b * blk, blk), blk)
        m1, m2 = one_hot(rows)
        dest = rank_scr[rows, :] + start
        p1 = jnp.sum(jnp.where(m1, dest, 0.0), axis=-1, keepdims=True)
        p2 = jnp.sum(jnp.where(m2, dest, 0.0), axis=-1, keepdims=True)
        pos_ref[rows, :] = (jnp.where(lane == 0.0, p1, 0.0) + jnp.where(lane == 1.0, p2, 0.0)).astype(jnp.int32)
        return carry
    lax.fori_loop(0, n // blk, pos_body, 0, unroll=2)


def _moe_plan(route, *, tm, blk):
    n = route.shape[0]
    return pl.pallas_call(
        functools.partial(_plan_kernel, tm=tm, blk=blk),
        out_shape=(jax.ShapeDtypeStruct((n, LANES), jnp.int32),
                   jax.ShapeDtypeStruct((8, LANES), jnp.float32)),
        scratch_shapes=[pltpu.VMEM((n, LANES), jnp.float32)],
        compiler_params=pltpu.CompilerParams(vmem_limit_bytes=VMEM_LIMIT),
        name="moe_plan",
    )(route)


def _row_copy(src, src_row, dst, dst_row, sem):
    return pltpu.make_async_copy(src.at[pl.ds(src_row, 1), :], dst.at[pl.ds(dst_row, 1), :], sem)


def _invert_kernel(pos_ref, sid_ref):
    def body(j, carry):
        sid_ref[pos_ref[j]] = j
        return carry
    lax.fori_loop(0, pos_ref.shape[0], body, 0, unroll=8)


def _moe_invert(pos, *, rows):
    return pl.pallas_call(
        _invert_kernel,
        out_shape=jax.ShapeDtypeStruct((rows,), jnp.int32),
        in_specs=[pl.BlockSpec(memory_space=pltpu.SMEM)],
        out_specs=pl.BlockSpec(memory_space=pltpu.SMEM),
        name="moe_invert",
    )(pos)


ROW_GROUP = 8
ROW_BLOCK = 32
FFN_ROW_STEP = 64
NO_NEXT, NOT_FIRST = -1, -2


def _ffn_kernel(te_ref, tv_ref, nu_ref, seg_ref, nexte_ref, sid_ref, tok_ref,
                h2_ref, w1_ref, w3_ref, w2_ref, y2_ref,
                xbuf, ybuf, w1buf, w3buf, w2buf, gsem, ssem, wsem):
    t = pl.program_id(0)
    n_used = nu_ref[0]
    tm = xbuf.shape[1]

    def weight_copies(e, wslot):
        return [pltpu.make_async_copy(src.at[e], dst.at[wslot], wsem.at[wslot])
                for src, dst in ((w1_ref, w1buf), (w3_ref, w3buf), (w2_ref, w2buf))]

    def gather(slot):
        return tok_ref, lambda r, tok: _row_copy(h2_ref, tok, xbuf.at[slot], r, gsem.at[slot])

    def scatter(slot):
        return sid_ref, lambda r, sid: _row_copy(ybuf.at[slot], r, y2_ref, sid, ssem.at[slot])

    def start_all(tile, table, copy):
        valid = tv_ref[tile]
        for blk in range(tm // ROW_BLOCK):
            @pl.when(valid >= (blk + 1) * ROW_BLOCK)
            def _():
                for r in range(blk * ROW_BLOCK, (blk + 1) * ROW_BLOCK):
                    copy(r, table[tile * tm + r]).start()
        done = (valid // ROW_BLOCK) * ROW_BLOCK

        def body(c, carry):
            for u in range(ROW_GROUP):
                r = done + c * ROW_GROUP + u

                @pl.when(r < valid)
                def _():
                    copy(r, table[tile * tm + r]).start()
            return carry
        lax.fori_loop(0, (valid - done + (ROW_GROUP - 1)) // ROW_GROUP, body, 0)

    def wait_all(tile, copy, block_copy):
        valid = tv_ref[tile]
        for blk in range(tm // ROW_BLOCK):
            @pl.when(valid >= (blk + 1) * ROW_BLOCK)
            def _():
                block_copy.wait()

        def body(r, carry):
            copy(0, 0).wait()
            return carry
        lax.fori_loop(0, valid % ROW_BLOCK, body, 0)

    def wait_gather(tile, slot):
        wait_all(tile, gather(slot)[1],
                 pltpu.make_async_copy(h2_ref.at[pl.ds(0, ROW_BLOCK), :],
                                       xbuf.at[slot, pl.ds(0, ROW_BLOCK), :], gsem.at[slot]))

    def wait_scatter(tile, slot):
        wait_all(tile, scatter(slot)[1],
                 pltpu.make_async_copy(ybuf.at[slot, pl.ds(0, ROW_BLOCK), :],
                                       y2_ref.at[pl.ds(0, ROW_BLOCK), :], ssem.at[slot]))

    @pl.when(t == 0)
    def _():
        for cp in weight_copies(te_ref[0], 0):
            cp.start()
        start_all(0, *gather(0))

    @pl.when(t < n_used)
    def _():
        slot = t % 2
        wslot = seg_ref[t] % 2

        @pl.when(t + 1 < n_used)
        def _():
            start_all(t + 1, *gather(1 - slot))

        @pl.when(nexte_ref[t] != NOT_FIRST)
        def _():
            for cp in weight_copies(te_ref[t], wslot):
                cp.wait()

            @pl.when(nexte_ref[t] >= 0)
            def _():
                for cp in weight_copies(nexte_ref[t], 1 - wslot):
                    cp.start()

        wait_gather(t, slot)

        @pl.when(t >= 2)
        def _():
            wait_scatter(t - 2, slot)

        valid = tv_ref[t]
        for live in range(FFN_ROW_STEP, tm + 1, FFN_ROW_STEP):
            @pl.when((valid > live - FFN_ROW_STEP) & (valid <= live))
            def _():
                row = lax.broadcasted_iota(jnp.int32, (live, xbuf.shape[2]), 0)
                x = jnp.where(row < valid, xbuf[slot, :live, :], 0.0).astype(jnp.bfloat16)
                a = _dot(x, w1buf[wslot].astype(jnp.bfloat16))
                b = _dot(x, w3buf[wslot].astype(jnp.bfloat16))
                act = (a * (1.0 / (1.0 + jnp.exp(-a)))) * b
                ybuf[slot, :live, :] = _dot(act.astype(jnp.bfloat16), w2buf[wslot].astype(jnp.bfloat16))
        start_all(t, *scatter(slot))

    @pl.when(t == pl.num_programs(0) - 1)
    def _():
        @pl.when(n_used >= 2)
        def _():
            wait_scatter(n_used - 2, n_used % 2)
        wait_scatter(n_used - 1, (n_used - 1) % 2)


def _moe_ffn(tile_expert, tile_valid, n_used, tile_seg, tile_next, row_sid, row_tok, h2, w1, w3, w2,
             *, tm):
    n, d = h2.shape
    _, _, ff = w1.shape
    max_tiles = tile_expert.shape[0]
    any_spec = pl.BlockSpec(memory_space=pl.ANY)
    return pl.pallas_call(
        _ffn_kernel,
        out_shape=jax.ShapeDtypeStruct((2 * n, d), jnp.float32),
        grid_spec=pltpu.PrefetchScalarGridSpec(
            num_scalar_prefetch=7,
            grid=(max_tiles,),
            in_specs=[any_spec, any_spec, any_spec, any_spec],
            out_specs=any_spec,
            scratch_shapes=[
                pltpu.VMEM((2, tm, d), jnp.float32),
                pltpu.VMEM((2, tm, d), jnp.float32),
                pltpu.VMEM((2, d, ff), jnp.float32),
                pltpu.VMEM((2, d, ff), jnp.float32),
                pltpu.VMEM((2, ff, d), jnp.float32),
                pltpu.SemaphoreType.DMA((2,)),
                pltpu.SemaphoreType.DMA((2,)),
                pltpu.SemaphoreType.DMA((2,)),
            ],
        ),
        compiler_params=_params("arbitrary"),
        name="moe_ffn",
    )(tile_expert, tile_valid, n_used, tile_seg, tile_next, row_sid, row_tok, h2, w1, w3, w2)


def _combine_kernel(x1_ref, route_ref, ya_ref, yb_ref, o_ref):
    route = route_ref[...]
    w1 = route[:, ROUTE_W1:ROUTE_W1 + 1]
    w2 = route[:, ROUTE_W2:ROUTE_W2 + 1]
    o_ref[...] = x1_ref[...] + (w1 * ya_ref[...] + w2 * yb_ref[...])


def _moe_combine(x1, route, y2, *, tm):
    n, d = x1.shape
    return pl.pallas_call(
        _combine_kernel,
        out_shape=jax.ShapeDtypeStruct((n, d), jnp.float32),
        grid=(n // tm,),
        in_specs=[pl.BlockSpec((tm, d), lambda i: (i, 0)),
                  pl.BlockSpec((tm, LANES), lambda i: (i, 0)),
                  pl.BlockSpec((tm, d), lambda i: (i, 0)),
                  pl.BlockSpec((tm, d), lambda i: (i + n // tm, 0))],
        out_specs=pl.BlockSpec((tm, d), lambda i: (i, 0)),
        compiler_params=_params("arbitrary"),
        name="moe_combine",
    )(x1, route, y2, y2)


def _moe(x1, h2, route, w1, w3, w2, *, tm, gather_tm):
    n, d = x1.shape
    max_tiles = (2 * n) // tm + N_EXPERTS
    pos2d, cnt = _moe_plan(route, tm=tm, blk=gather_tm)
    pos = pos2d[:, :2].T.reshape(2 * n)
    counts = cnt[0, :N_EXPERTS].astype(jnp.int32)
    tiles = (counts + (tm - 1)) // tm
    ends = jnp.cumsum(tiles)
    t_idx = jnp.arange(max_tiles, dtype=jnp.int32)
    tile_expert = jnp.sum((ends[None, :] <= t_idx[:, None]).astype(jnp.int32), axis=1)
    tile_expert = jnp.minimum(tile_expert, N_EXPERTS - 1)
    first_tile = (ends - tiles)[tile_expert]
    tile_valid = jnp.clip(counts[tile_expert] - (t_idx - first_tile) * tm, 0, tm).astype(jnp.int32)
    n_used = ends[-1:].astype(jnp.int32)
    used = t_idx < n_used[0]
    is_first = used & (t_idx == first_tile)
    tile_seg = (jnp.cumsum(is_first.astype(jnp.int32)) - 1).astype(jnp.int32)
    next_start = first_tile + tiles[tile_expert]
    next_e = jnp.where(next_start < n_used[0], tile_expert[jnp.minimum(next_start, max_tiles - 1)], NO_NEXT)
    tile_next = jnp.where(is_first, next_e, NOT_FIRST).astype(jnp.int32)

    row_sid = _moe_invert(pos, rows=max_tiles * tm)
    row_tok = jnp.where(row_sid >= n, row_sid - n, row_sid)
    y2 = _moe_ffn(tile_expert, tile_valid, n_used, tile_seg, tile_next, row_sid, row_tok, h2, w1, w3, w2,
                  tm=tm)
    return _moe_combine(x1, route, y2, tm=gather_tm)


def _pack_router(w_group, w_router):
    d = w_group.shape[0]
    experts = jnp.transpose(w_router, (1, 0, 2)).reshape(d, N_EXPERTS)
    wr = jnp.concatenate(
        [experts, w_group, jnp.zeros((d, LANES - N_EXPERTS - N_GROUPS), w_group.dtype)], axis=1)
    hi = wr.astype(jnp.bfloat16)
    lo = (wr - hi.astype(jnp.float32)).astype(jnp.bfloat16)
    return jnp.concatenate([hi, lo], axis=1)


def _rotation_tables(seq):
    half = RET_DK // 2
    pos = jnp.arange(seq, dtype=jnp.float32)
    inv = 1.0 / (ROT_BASE ** jnp.linspace(0.0, 1.0, half, dtype=jnp.float32))
    ang = pos[:, None] * inv[None, :]
    c, s = jnp.cos(ang), jnp.sin(ang)
    return jnp.concatenate([c, c], axis=-1), jnp.concatenate([-s, s], axis=-1)


def _tiles(n, seq):
    def fit(total, want):
        t = min(total, want)
        while total % t:
            t //= 2
        return t
    return dict(
        proj_tm=fit(n, 1024),
        attn_tq=fit(seq, 256), attn_tk=fit(seq, 256),
        ret_rc=fit(seq, 256),
        out_tm=fit(n, 512),
        moe_tm=fit(n, 256), moe_gather_tm=fit(n, 256),
    )


def kernel(x, norm1_g, w_in, q_norm_g, k_norm_g, idx_k_ln_w, idx_k_ln_b, ret_norm_g,
           w_out, norm2_g, w_group, w_router, w1, w3, w2):
    b, seq, d = x.shape
    n = b * seq
    depth = w_in.shape[0]
    t = _tiles(n, seq)
    cos2, sin2 = _rotation_tables(seq)
    log_gamma = jnp.log1p(-jnp.exp2(-5.0 - jnp.arange(RET_HEADS, dtype=jnp.float32)))

    x2d = x.reshape(n, d)
    for l in range(depth):
        proj = _in_proj(x2d, norm1_g[l][None, :], w_in[l].T, tm=t["proj_tm"])
        p3 = proj.reshape(b, seq, proj.shape[1])
        attn = _dsa_attention(p3, q_norm_g[l][None, :], k_norm_g[l][None, :],
                              idx_k_ln_w[l][None, :], idx_k_ln_b[l][None, :],
                              tq=t["attn_tq"], tk=t["attn_tk"])
        ret = _retention(p3, log_gamma, cos2, sin2, ret_norm_g[l].reshape(RET_HEADS, 1, RET_DV),
                         rc=t["ret_rc"])
        r_cat = _pack_router(w_group[l], w_router[l])
        g2 = norm2_g[l][None, :]
        x1, h2, route = _out_proj(attn.reshape(n, ATTN_WIDTH), ret.reshape(n, RET_WIDTH), x2d,
                                  w_out[l].astype(jnp.bfloat16), g2, r_cat, tm=t["out_tm"])
        x2d = _moe(x1, h2, route, w1[l], w3[l], w2[l], tm=t["moe_tm"], gather_tm=t["moe_gather_tm"])
    return x2d.reshape(b, seq, d)
```

```python
import functools
import math

import jax
import jax.numpy as jnp
from jax import lax
from jax.experimental import pallas as pl
from jax.experimental.pallas import tpu as pltpu

CHUNK = 64
ATTN_HEADS = 8
HEAD_DIM = 128
KV_HEADS = 2
HEADS_PER_KV = ATTN_HEADS // KV_HEADS
IDX_HEADS = 16
IDX_DIM = 64
TOPK_MAX = 256
RET_HEADS = 8
RET_DK = 128
RET_DV = 128
ROT_BASE = 10000.0
N_GROUPS = 4
EXPERTS_PER_GROUP = 8
N_EXPERTS = N_GROUPS * EXPERTS_PER_GROUP
EPS = 1e-6

ATTN_WIDTH = ATTN_HEADS * HEAD_DIM
KV_WIDTH = KV_HEADS * HEAD_DIM
IDX_WIDTH = IDX_HEADS * IDX_DIM
RET_WIDTH = RET_HEADS * RET_DK

LANES = 128
SUBLANES = 8
VMEM_LIMIT = 56 * 1024 * 1024

AQ_OFF = 0
AK_OFF = AQ_OFF + ATTN_WIDTH
AV_OFF = AK_OFF + KV_WIDTH
IQ_OFF = AV_OFF + KV_WIDTH
IK_OFF = IQ_OFF + IDX_WIDTH
IW_OFF = IK_OFF + IDX_DIM
W_RET = IW_OFF + IDX_HEADS
IN_WIDTH = W_RET + 4 * RET_WIDTH
assert IW_OFF // LANES == IK_OFF // LANES
PROJ_TN = 1024
RQ_OFF = -(-W_RET // PROJ_TN) * PROJ_TN
RK_OFF = RQ_OFF + RET_WIDTH
RV_OFF = RK_OFF + RET_WIDTH
RG_OFF = RV_OFF + RET_WIDTH
PROJ_WIDTH = RG_OFF + RET_WIDTH

ROUTE_E1, ROUTE_E2, ROUTE_W1, ROUTE_W2 = 0, 1, 2, 3

SUM_ROWS = 16
LOGIT_BOUND_SLACK = 1.05
MAX_SINGLE_SWEEP_BOUND = 50.0

INT_MIN = -(2 ** 31)
NEG_BIG = -1e30

_NT = (((1,), (1,)), ((), ()))


def _dot(a, b):
    return jnp.dot(a, b, preferred_element_type=jnp.float32)


def _dot_nt(a, b):
    return lax.dot_general(a, b, _NT, preferred_element_type=jnp.float32)


def _params(*sem):
    return pltpu.CompilerParams(dimension_semantics=sem, vmem_limit_bytes=VMEM_LIMIT)


def _in_proj_kernel(x_ref, g_ref, wt_ref, o_ref, h_scr, *, row_chunk):
    first = pl.program_id(1) == 0

    @pl.when(first)
    def _():
        w = wt_ref[...].astype(jnp.bfloat16)
        for c in range(x_ref.shape[0] // row_chunk):
            rows = slice(c * row_chunk, (c + 1) * row_chunk)
            x = x_ref[rows, :]
            ms = jnp.mean(x * x, axis=-1, keepdims=True)
            h = ((x * lax.rsqrt(ms + EPS)) * g_ref[...]).astype(jnp.bfloat16)
            h_scr[rows, :] = h
            o_ref[rows, :] = _dot_nt(h, w).astype(o_ref.dtype)

    @pl.when(jnp.logical_not(first))
    def _():
        o_ref[...] = _dot_nt(h_scr[...], wt_ref[...].astype(jnp.bfloat16)).astype(o_ref.dtype)


def _in_proj(x2d, g, w_in_t, *, tm):
    n, d = x2d.shape
    tn = PROJ_TN
    assert w_in_t.shape == (IN_WIDTH, d) and W_RET % SUBLANES == 0 and tn % SUBLANES == 0
    attn_tiles = RQ_OFF // tn

    def window(i, j):
        step = tn // SUBLANES
        start = jnp.where(j < attn_tiles, j * step, W_RET // SUBLANES + (j - attn_tiles) * step)
        return SUBLANES * start, 0

    return pl.pallas_call(
        functools.partial(_in_proj_kernel, row_chunk=min(tm, 256)),
        out_shape=jax.ShapeDtypeStruct((n, PROJ_WIDTH), jnp.bfloat16),
        grid=(n // tm, PROJ_WIDTH // tn),
        in_specs=[
            pl.BlockSpec((tm, d), lambda i, j: (i, 0)),
            pl.BlockSpec((1, d), lambda i, j: (0, 0)),
            pl.BlockSpec((pl.Element(tn), pl.Element(d)), window),
        ],
        out_specs=pl.BlockSpec((tm, tn), lambda i, j: (i, j)),
        scratch_shapes=[pltpu.VMEM((tm, d), jnp.bfloat16)],
        compiler_params=_params("arbitrary", "arbitrary"),
        name="in_proj",
    )(x2d, g, w_in_t)


def _ordered_float(v):
    bits = v ^ ((v >> 31) & jnp.int32(0x7FFFFFFF))
    return pltpu.bitcast(bits, jnp.float32)


def _attn_kernel(aq_ref, iqa_ref, iqb_ref, iw_ref, ak_ref, av_ref, ik_ref, qg_ref, kg_ref, lnw_ref,
                 lnb_ref, o_ref,
                 kn_scr, ikn_scr, vt_scr, key_scr, wt_scr, qn_scr, acc_scr, s_scr, kmax_scr,
                 *, tk, topk, idx_w_scale):
    i = pl.program_id(1)
    seq = ak_ref.shape[1]
    tq = aq_ref.shape[1]
    chunk_shift = CHUNK.bit_length() - 1

    @pl.when(i == 0)
    def _():
        def body(c, carry):
            rows = pl.ds(pl.multiple_of(c * tk, tk), tk)
            for g in range(KV_HEADS):
                cols = slice(g * HEAD_DIM, (g + 1) * HEAD_DIM)
                k = ak_ref[0, rows, cols].astype(jnp.float32)
                ms = jnp.mean(k * k, axis=-1, keepdims=True)
                kn = (k * lax.rsqrt(ms + EPS)) * kg_ref[...]
                kn_scr[rows, cols] = kn.astype(jnp.bfloat16)
                ksq = jnp.max(jnp.sum(kn * kn, axis=-1, keepdims=True), axis=0, keepdims=True)
                prev = jnp.where(c == 0, 0.0, kmax_scr[g])
                kmax_scr[g] = jnp.maximum(prev, jnp.broadcast_to(ksq, kmax_scr.shape[1:]))
                v = av_ref[0, rows, cols].astype(jnp.float32)
                vt_scr[g, c, :HEAD_DIM, :] = v.T.astype(jnp.bfloat16)
                vt_scr[g, c, HEAD_DIM:, :] = jnp.ones((SUM_ROWS, tk), jnp.bfloat16)
            ki = ik_ref[0, rows, :IDX_DIM].astype(jnp.float32)
            mu = jnp.mean(ki, axis=-1, keepdims=True)
            var = jnp.mean(jnp.square(ki - mu), axis=-1, keepdims=True)
            y = ((ki - mu) * lax.rsqrt(var + EPS) * lnw_ref[...] + lnb_ref[...]).astype(jnp.bfloat16)
            zeros = jnp.zeros_like(y)
            ikn_scr[0, rows, :] = jnp.concatenate([y, zeros], axis=1)
            ikn_scr[1, rows, :] = jnp.concatenate([zeros, y], axis=1)
            return carry
        lax.fori_loop(0, seq // tk, body, 0)

    t0 = i * tq
    n_kt = (t0 + tq) // tk
    scale = (HEAD_DIM ** -0.5) * math.log2(math.e)
    for h in range(ATTN_HEADS):
        g, r = divmod(h, HEADS_PER_KV)
        q = aq_ref[0, :, h * HEAD_DIM:(h + 1) * HEAD_DIM].astype(jnp.float32)
        ms = jnp.mean(q * q, axis=-1, keepdims=True)
        qn_scr[g, r * tq:(r + 1) * tq, :] = (
            (q * lax.rsqrt(ms + EPS)) * qg_ref[...] * scale).astype(jnp.bfloat16)
    ones_rows = jnp.ones((8, HEAD_DIM), jnp.bfloat16)
    bound = []
    for g in range(KV_HEADS):
        qf = qn_scr[g].astype(jnp.float32)
        qsq = _dot_nt(ones_rows, (qf * qf).astype(jnp.bfloat16))[0:1, :]
        kmax = jnp.concatenate([kmax_scr[g, 0:1, :]] * (HEADS_PER_KV * tq // LANES), axis=1)
        bound.append(LOGIT_BOUND_SLACK * jnp.sqrt(qsq * kmax))
    wt_scr[...] = iw_ref[0].astype(jnp.float32).T * idx_w_scale
    w_row = IW_OFF % LANES

    q_chunk = (t0 + lax.broadcasted_iota(jnp.int32, (tk, tq), 1)) >> chunk_shift

    def score_body(kt, carry):
        rows = pl.ds(pl.multiple_of(kt * tk, tk), tk)
        ik_first, ik_second = ikn_scr[0, rows, :], ikn_scr[1, rows, :]
        acc = jnp.zeros((tk, tq), jnp.float32)
        pairs_per_ref = iqa_ref.shape[2] // LANES
        for pair in range(IDX_HEADS // 2):
            src = iqa_ref if pair < pairs_per_ref else iqb_ref
            lane0 = (pair % pairs_per_ref) * LANES
            q_pair = src[0, :, lane0:lane0 + LANES]
            for sub, ik_t in enumerate((ik_first, ik_second)):
                h = 2 * pair + sub
                d = _dot_nt(ik_t, q_pair)
                acc = acc + jnp.maximum(d, 0.0) * wt_scr[w_row + h:w_row + h + 1, :]
        k_chunk = (kt * tk + lax.broadcasted_iota(jnp.int32, (tk, tq), 0)) >> chunk_shift
        key_scr[rows, :] = jnp.where(k_chunk <= q_chunk, acc, -jnp.inf)
        return carry
    lax.fori_loop(0, n_kt, score_body, 0)

    def bit_body(it, lo):
        cand = lo + lax.shift_left(jnp.int32(1), 31 - it)
        cand_f = _ordered_float(cand)

        def count_body(kt, part):
            rows = pl.ds(pl.multiple_of(kt * tk, tk), tk)
            hit = jnp.where(key_scr[rows, :] >= cand_f, 1.0, 0.0).reshape(tk // 8, 8, tq)
            while hit.shape[0] > 1:
                half = hit.shape[0] // 2
                hit = hit[:half] + hit[half:]
            return part + hit[0]
        part = lax.fori_loop(0, n_kt, count_body, jnp.zeros((8, tq), jnp.float32))
        cnt = jnp.sum(part, axis=0, keepdims=True)
        return jnp.where(cnt >= float(topk), cand, lo)
    lo = lax.fori_loop(0, 32, bit_body, jnp.full((1, tq), INT_MIN, jnp.int32))
    thr = jnp.where(lo == INT_MIN, jnp.finfo(jnp.float32).min, _ordered_float(lo))

    acc_scr[...] = jnp.zeros(acc_scr.shape, jnp.float32)

    def masked_logits(kt):
        rows = pl.ds(pl.multiple_of(kt * tk, tk), tk)
        bias = jnp.where(key_scr[rows, :] >= thr, 0.0, NEG_BIG)
        bias = jnp.concatenate([bias] * HEADS_PER_KV, axis=1)
        return [_dot_nt(kn_scr[rows, g * HEAD_DIM:(g + 1) * HEAD_DIM], qn_scr[g]) + bias
                for g in range(KV_HEADS)]

    bound_max = jnp.max(jnp.maximum(bound[0], bound[1]))
    single_sweep = bound_max <= MAX_SINGLE_SWEEP_BOUND

    @pl.when(single_sweep)
    def _():
        def body(kt, carry):
            for g, s in enumerate(masked_logits(kt)):
                acc_scr[g] += _dot(vt_scr[g, kt], jnp.exp2(s - bound[g]).astype(jnp.bfloat16))
            return carry
        lax.fori_loop(0, n_kt, body, 0)

    @pl.when(jnp.logical_not(single_sweep))
    def _():
        def logit_body(kt, m):
            rows = pl.ds(pl.multiple_of(kt * tk, tk), tk)
            new_m = []
            for g, s in enumerate(masked_logits(kt)):
                s_scr[g, rows, :] = s
                new_m.append(jnp.maximum(m[g], jnp.max(s, axis=0, keepdims=True)))
            return tuple(new_m)
        m0 = jnp.full((1, HEADS_PER_KV * tq), NEG_BIG, jnp.float32)
        m = lax.fori_loop(0, n_kt, logit_body, (m0,) * KV_HEADS)

        def pv_body(kt, carry):
            rows = pl.ds(pl.multiple_of(kt * tk, tk), tk)
            for g in range(KV_HEADS):
                p = jnp.exp2(s_scr[g, rows, :] - m[g]).astype(jnp.bfloat16)
                acc_scr[g] += _dot(vt_scr[g, kt], p)
            return carry
        lax.fori_loop(0, n_kt, pv_body, 0)

    for h in range(ATTN_HEADS):
        g, r = divmod(h, HEADS_PER_KV)
        cols = slice(r * tq, (r + 1) * tq)
        o = acc_scr[g, :HEAD_DIM, cols] / acc_scr[g, HEAD_DIM:HEAD_DIM + 1, cols]
        o_ref[0, :, h * HEAD_DIM:(h + 1) * HEAD_DIM] = o.T.astype(o_ref.dtype)


def _dsa_attention(p3, q_g, k_g, ln_w, ln_b, *, tq, tk):
    b, seq, _ = p3.shape
    topk = min(TOPK_MAX, seq // 4)
    idx_w_scale = (IDX_HEADS ** -0.5) * (IDX_DIM ** -0.5)
    assert seq % tq == 0 and tq % tk == 0 and tk % CHUNK == 0

    def col(off, width):
        assert off % width == 0 or width == LANES
        return off // width

    half_iq = IDX_WIDTH // 2
    return pl.pallas_call(
        functools.partial(_attn_kernel, tk=tk, topk=topk, idx_w_scale=idx_w_scale),
        out_shape=jax.ShapeDtypeStruct((b, seq, ATTN_WIDTH), jnp.bfloat16),
        grid=(b, seq // tq),
        in_specs=[
            pl.BlockSpec((1, tq, ATTN_WIDTH), lambda bi, i: (bi, i, col(AQ_OFF, ATTN_WIDTH))),
            pl.BlockSpec((1, tq, half_iq), lambda bi, i: (bi, i, col(IQ_OFF, half_iq))),
            pl.BlockSpec((1, tq, half_iq), lambda bi, i: (bi, i, col(IQ_OFF, half_iq) + 1)),
            pl.BlockSpec((1, tq, LANES), lambda bi, i: (bi, i, col(IW_OFF, LANES))),
            pl.BlockSpec((1, seq, KV_WIDTH), lambda bi, i: (bi, 0, col(AK_OFF, KV_WIDTH))),
            pl.BlockSpec((1, seq, KV_WIDTH), lambda bi, i: (bi, 0, col(AV_OFF, KV_WIDTH))),
            pl.BlockSpec((1, seq, LANES), lambda bi, i: (bi, 0, col(IK_OFF, LANES))),
            pl.BlockSpec((1, HEAD_DIM), lambda bi, i: (0, 0)),
            pl.BlockSpec((1, HEAD_DIM), lambda bi, i: (0, 0)),
            pl.BlockSpec((1, IDX_DIM), lambda bi, i: (0, 0)),
            pl.BlockSpec((1, IDX_DIM), lambda bi, i: (0, 0)),
        ],
        out_specs=pl.BlockSpec((1, tq, ATTN_WIDTH), lambda bi, i: (bi, i, 0)),
        scratch_shapes=[
            pltpu.VMEM((seq, KV_WIDTH), jnp.bfloat16),
            pltpu.VMEM((2, seq, 2 * IDX_DIM), jnp.bfloat16),
            pltpu.VMEM((KV_HEADS, seq // tk, HEAD_DIM + SUM_ROWS, tk), jnp.bfloat16),
            pltpu.VMEM((seq, tq), jnp.float32),
            pltpu.VMEM((LANES, tq), jnp.float32),
            pltpu.VMEM((KV_HEADS, HEADS_PER_KV * tq, HEAD_DIM), jnp.bfloat16),
            pltpu.VMEM((KV_HEADS, HEAD_DIM + SUM_ROWS, HEADS_PER_KV * tq), jnp.float32),
            pltpu.VMEM((KV_HEADS, seq, HEADS_PER_KV * tq), jnp.float32),
            pltpu.VMEM((KV_HEADS, 8, LANES), jnp.float32),
        ],
        compiler_params=_params("arbitrary", "arbitrary"),
        name="dsa_attn",
    )(p3, p3, p3, p3, p3, p3, p3, q_g, k_g, ln_w, ln_b)


RET_HEADS_PER_STEP = 2


def _ret_kernel(lg_ref, rq_ref, rk_ref, rv_ref, rg_ref, cos_ref, sin_ref, g_ref, o_ref, *, rc):
    seq = rq_ref.shape[1]
    n = lax.broadcasted_iota(jnp.int32, (rc, RET_DV), 0).astype(jnp.float32)
    rel = (lax.broadcasted_iota(jnp.int32, (rc, rc), 0)
           - lax.broadcasted_iota(jnp.int32, (rc, rc), 1)).astype(jnp.float32)

    def rot(x, rows):
        return x * cos_ref[rows, :] + pltpu.roll(x, RET_DK // 2, 1) * sin_ref[rows, :]

    heads = []
    for hh in range(RET_HEADS_PER_STEP):
        lg = lg_ref[pl.program_id(1) * RET_HEADS_PER_STEP + hh]
        heads.append(dict(
            cols=slice(hh * RET_DK, (hh + 1) * RET_DK),
            cross_decay=jnp.exp(lg * (n + 1.0)),
            state_decay=jnp.exp(lg * (rc - 1.0 - n)),
            chunk_decay=jnp.exp(lg * jnp.full((RET_DK, RET_DV), float(rc), jnp.float32)),
            intra=jnp.where(rel >= 0, jnp.exp(lg * jnp.maximum(rel, 0.0)), 0.0),
            state=jnp.zeros((RET_DK, RET_DV), jnp.float32),
            gain=g_ref[hh],
        ))

    for c in range(seq // rc):
        rows = slice(c * rc, (c + 1) * rc)
        for hd in heads:
            cols = hd["cols"]
            q = rot(rq_ref[0, rows, cols].astype(jnp.float32), rows)
            k = rot(rk_ref[0, rows, cols].astype(jnp.float32), rows) * (RET_DK ** -0.5)
            v = rv_ref[0, rows, cols]
            qb = q.astype(jnp.bfloat16)
            inner = _dot_nt(qb, k.astype(jnp.bfloat16)) * hd["intra"]
            o = (_dot(inner.astype(jnp.bfloat16), v)
                 + _dot(qb, hd["state"].astype(jnp.bfloat16)) * hd["cross_decay"])
            kd_t = (k * hd["state_decay"]).T.astype(jnp.bfloat16)
            hd["state"] = hd["state"] * hd["chunk_decay"] + _dot(kd_t, v)
            ms = jnp.mean(o * o, axis=-1, keepdims=True)
            y = (o * lax.rsqrt(ms + EPS)) * hd["gain"]
            gate = rg_ref[0, rows, cols].astype(jnp.float32)
            o_ref[0, rows, cols] = ((gate * (1.0 / (1.0 + jnp.exp(-gate)))) * y).astype(o_ref.dtype)


def _retention(p3, log_gamma, cos2, sin2, ret_g, *, rc):
    b, seq, _ = p3.shape
    hps = RET_HEADS_PER_STEP
    width = hps * RET_DK
    assert seq % rc == 0 and RET_HEADS % hps == 0

    def head_spec(off):
        assert off % width == 0
        return pl.BlockSpec((1, seq, width), lambda bi, h: (bi, 0, off // width + h))

    return pl.pallas_call(
        functools.partial(_ret_kernel, rc=rc),
        out_shape=jax.ShapeDtypeStruct((b, seq, RET_WIDTH), jnp.bfloat16),
        grid=(b, RET_HEADS // hps),
        in_specs=[
            pl.BlockSpec(memory_space=pltpu.SMEM),
            head_spec(RQ_OFF), head_spec(RK_OFF), head_spec(RV_OFF), head_spec(RG_OFF),
            pl.BlockSpec((seq, RET_DK), lambda bi, h: (0, 0)),
            pl.BlockSpec((seq, RET_DK), lambda bi, h: (0, 0)),
            pl.BlockSpec((hps, 1, RET_DV), lambda bi, h: (h, 0, 0)),
        ],
        out_specs=pl.BlockSpec((1, seq, width), lambda bi, h: (bi, 0, h)),
        compiler_params=_params("arbitrary", "arbitrary"),
        name="retention",
    )(log_gamma, p3, p3, p3, p3, cos2, sin2, ret_g)


def _routing(logits):
    lane = lax.broadcasted_iota(jnp.int32, logits.shape, 1).astype(jnp.float32)
    big = float(LANES)
    neg = -jnp.inf

    def first_argmax(v, vmax):
        return jnp.min(jnp.where(v == vmax, lane, big), axis=-1, keepdims=True)

    g_mask = (lane >= N_EXPERTS) & (lane < N_EXPERTS + N_GROUPS)
    gl = jnp.where(g_mask, logits, neg)
    g_max = jnp.max(gl, axis=-1, keepdims=True)
    g_sel = first_argmax(gl, g_max) - N_EXPERTS
    g_gate = 1.0 / jnp.sum(jnp.where(g_mask, jnp.exp(gl - g_max), 0.0), axis=-1, keepdims=True)

    e_lo = g_sel * EXPERTS_PER_GROUP
    el = jnp.where((lane >= e_lo) & (lane < e_lo + EXPERTS_PER_GROUP), logits, neg)
    v1 = jnp.max(el, axis=-1, keepdims=True)
    i1 = first_argmax(el, v1)
    el2 = jnp.where(lane == i1, neg, el)
    v2 = jnp.max(el2, axis=-1, keepdims=True)
    i2 = first_argmax(el2, v2)
    e2 = jnp.exp(v2 - v1)
    denom = 1.0 + e2
    w1 = (1.0 / denom) * g_gate
    w2 = (e2 / denom) * g_gate
    route = jnp.where(lane == ROUTE_E1, i1, 0.0) + jnp.where(lane == ROUTE_E2, i2, 0.0)
    return route + jnp.where(lane == ROUTE_W1, w1, 0.0) + jnp.where(lane == ROUTE_W2, w2, 0.0)


def _norm2(x1, g):
    ms = jnp.mean(x1 * x1, axis=-1, keepdims=True)
    return (x1 * lax.rsqrt(ms + EPS)) * g


OUT_SUBTILES = 2


def _out_proj_kernel(a_ref, r_ref, x_ref, wa_ref, wr_ref, g_ref, rcat_ref,
                     x1_ref, h2_ref, route_ref):
    sub = x_ref.shape[0] // OUT_SUBTILES
    for s in range(OUT_SUBTILES):
        rows = slice(s * sub, (s + 1) * sub)
        mixed = _dot(a_ref[rows, :], wa_ref[...]) + _dot(r_ref[rows, :], wr_ref[...])
        x1 = x_ref[rows, :] + mixed
        x1_ref[rows, :] = x1
        h2 = _norm2(x1, g_ref[...])
        h2_ref[rows, :] = h2
        hi = h2.astype(jnp.bfloat16)
        lo = (h2 - hi.astype(jnp.float32)).astype(jnp.bfloat16)
        both = _dot(hi, rcat_ref[...])
        logits = both[:, :LANES] + (both[:, LANES:] + _dot(lo, rcat_ref[:, :LANES]))
        route_ref[rows, :] = _routing(logits)


def _out_proj(attn2d, ret2d, x2d, w_out_bf, g2, r_cat, *, tm):
    n, d = x2d.shape
    return pl.pallas_call(
        _out_proj_kernel,
        out_shape=(
            jax.ShapeDtypeStruct((n, d), jnp.float32),
            jax.ShapeDtypeStruct((n, d), jnp.float32),
            jax.ShapeDtypeStruct((n, LANES), jnp.float32),
        ),
        grid=(n // tm,),
        in_specs=[
            pl.BlockSpec((tm, ATTN_WIDTH), lambda i: (i, 0)),
            pl.BlockSpec((tm, RET_WIDTH), lambda i: (i, 0)),
            pl.BlockSpec((tm, d), lambda i: (i, 0)),
            pl.BlockSpec((ATTN_WIDTH, d), lambda i: (0, 0)),
            pl.BlockSpec((RET_WIDTH, d), lambda i: (ATTN_WIDTH // RET_WIDTH, 0)),
            pl.BlockSpec((1, d), lambda i: (0, 0)),
            pl.BlockSpec((d, 2 * LANES), lambda i: (0, 0)),
        ],
        out_specs=(
            pl.BlockSpec((tm, d), lambda i: (i, 0)),
            pl.BlockSpec((tm, d), lambda i: (i, 0)),
            pl.BlockSpec((tm, LANES), lambda i: (i, 0)),
        ),
        compiler_params=_params("arbitrary"),
        name="out_proj",
    )(attn2d, ret2d, x2d, w_out_bf, w_out_bf, g2, r_cat)


def _plan_kernel(route_ref, pos_ref, cnt_ref, rank_scr, *, tm, blk):
    n = route_ref.shape[0]
    lane = lax.broadcasted_iota(jnp.int32, (blk, LANES), 1).astype(jnp.float32)
    before = (lax.broadcasted_iota(jnp.int32, (blk, blk), 1)
              < lax.broadcasted_iota(jnp.int32, (blk, blk), 0)).astype(jnp.bfloat16)

    def one_hot(rows):
        r = route_ref[rows, :]
        e1 = r[:, ROUTE_E1:ROUTE_E1 + 1]
        e2 = r[:, ROUTE_E2:ROUTE_E2 + 1]
        return lane == e1, lane == e2

    def rank_body(b, run):
        rows = pl.ds(pl.multiple_of(b * blk, blk), blk)
        m1, m2 = one_hot(rows)
        sel = jnp.where(m1 | m2, 1.0, 0.0)
        rank_scr[rows, :] = _dot(before, sel.astype(jnp.bfloat16)) + run
        return run + jnp.sum(sel, axis=0, keepdims=True)
    cnt = lax.fori_loop(0, n // blk, rank_body, jnp.zeros((1, LANES), jnp.float32), unroll=2)
    cnt_ref[...] = jnp.broadcast_to(cnt, cnt_ref.shape)

    tiles = jnp.floor((cnt + (tm - 1.0)) * (1.0 / tm))
    below = (lax.broadcasted_iota(jnp.int32, (LANES, LANES), 0)
             < lax.broadcasted_iota(jnp.int32, (LANES, LANES), 1)).astype(jnp.bfloat16)
    start = _dot(jnp.broadcast_to(tiles, (8, LANES)).astype(jnp.bfloat16), below)[0:1, :] * float(tm)

    def pos_body(b, carry):
        rows = pl.ds(pl.multiple_of(b * blk, blk), blk)
        m1, m2 = one_hot(rows)
        dest = rank_scr[rows, :] + start
        p1 = jnp.sum(jnp.where(m1, dest, 0.0), axis=-1, keepdims=True)
        p2 = jnp.sum(jnp.where(m2, dest, 0.0), axis=-1, keepdims=True)
        pos_ref[rows, :] = (jnp.where(lane == 0.0, p1, 0.0) + jnp.where(lane == 1.0, p2, 0.0)).astype(jnp.int32)
        return carry
    lax.fori_loop(0, n // blk, pos_body, 0, unroll=2)


def _moe_plan(route, *, tm, blk):
    n = route.shape[0]
    return pl.pallas_call(
        functools.partial(_plan_kernel, tm=tm, blk=blk),
        out_shape=(jax.ShapeDtypeStruct((n, LANES), jnp.int32),
                   jax.ShapeDtypeStruct((8, LANES), jnp.float32)),
        scratch_shapes=[pltpu.VMEM((n, LANES), jnp.float32)],
        compiler_params=pltpu.CompilerParams(vmem_limit_bytes=VMEM_LIMIT),
        name="moe_plan",
    )(route)


def _row_copy(src, src_row, dst, dst_row, sem):
    return pltpu.make_async_copy(src.at[pl.ds(src_row, 1), :], dst.at[pl.ds(dst_row, 1), :], sem)


def _invert_kernel(pos_ref, sid_ref):
    def body(j, carry):
        sid_ref[pos_ref[j]] = j
        return carry
    lax.fori_loop(0, pos_ref.shape[0], body, 0, unroll=8)


def _moe_invert(pos, *, rows):
    return pl.pallas_call(
        _invert_kernel,
        out_shape=jax.ShapeDtypeStruct((rows,), jnp.int32),
        in_specs=[pl.BlockSpec(memory_space=pltpu.SMEM)],
        out_specs=pl.BlockSpec(memory_space=pltpu.SMEM),
        name="moe_invert",
    )(pos)


ROW_GROUP = 8
ROW_BLOCK = 32
FFN_ROW_STEP = 64
NO_NEXT, NOT_FIRST = -1, -2


def _ffn_kernel(te_ref, tv_ref, nu_ref, seg_ref, nexte_ref, sid_ref, tok_ref,
                h2_ref, w1_ref, w3_ref, w2_ref, y2_ref,
                xbuf, ybuf, w1buf, w3buf, w2buf, gsem, ssem, wsem):
    t = pl.program_id(0)
    n_used = nu_ref[0]
    tm = xbuf.shape[1]

    def weight_copies(e, wslot):
        return [pltpu.make_async_copy(src.at[e], dst.at[wslot], wsem.at[wslot])
                for src, dst in ((w1_ref, w1buf), (w3_ref, w3buf), (w2_ref, w2buf))]

    def gather(slot):
        return tok_ref, lambda r, tok: _row_copy(h2_ref, tok, xbuf.at[slot], r, gsem.at[slot])

    def scatter(slot):
        return sid_ref, lambda r, sid: _row_copy(ybuf.at[slot], r, y2_ref, sid, ssem.at[slot])

    def start_all(tile, table, copy):
        valid = tv_ref[tile]
        for blk in range(tm // ROW_BLOCK):
            @pl.when(valid >= (blk + 1) * ROW_BLOCK)
            def _():
                for r in range(blk * ROW_BLOCK, (blk + 1) * ROW_BLOCK):
                    copy(r, table[tile * tm + r]).start()
        done = (valid // ROW_BLOCK) * ROW_BLOCK

        def body(c, carry):
            for u in range(ROW_GROUP):
                r = done + c * ROW_GROUP + u

                @pl.when(r < valid)
                def _():
                    copy(r, table[tile * tm + r]).start()
            return carry
        lax.fori_loop(0, (valid - done + (ROW_GROUP - 1)) // ROW_GROUP, body, 0)

    def wait_all(tile, copy, block_copy):
        valid = tv_ref[tile]
        for blk in range(tm // ROW_BLOCK):
            @pl.when(valid >= (blk + 1) * ROW_BLOCK)
            def _():
                block_copy.wait()

        def body(r, carry):
            copy(0, 0).wait()
            return carry
        lax.fori_loop(0, valid % ROW_BLOCK, body, 0)

    def wait_gather(tile, slot):
        wait_all(tile, gather(slot)[1],
                 pltpu.make_async_copy(h2_ref.at[pl.ds(0, ROW_BLOCK), :],
                                       xbuf.at[slot, pl.ds(0, ROW_BLOCK), :], gsem.at[slot]))

    def wait_scatter(tile, slot):
        wait_all(tile, scatter(slot)[1],
                 pltpu.make_async_copy(ybuf.at[slot, pl.ds(0, ROW_BLOCK), :],
                                       y2_ref.at[pl.ds(0, ROW_BLOCK), :], ssem.at[slot]))

    @pl.when(t == 0)
    def _():
        for cp in weight_copies(te_ref[0], 0):
            cp.start()
        start_all(0, *gather(0))

    @pl.when(t < n_used)
    def _():
        slot = t % 2
        wslot = seg_ref[t] % 2

        @pl.when(t + 1 < n_used)
        def _():
            start_all(t + 1, *gather(1 - slot))

        @pl.when(nexte_ref[t] != NOT_FIRST)
        def _():
            for cp in weight_copies(te_ref[t], wslot):
                cp.wait()

            @pl.when(nexte_ref[t] >= 0)
            def _():
                for cp in weight_copies(nexte_ref[t], 1 - wslot):
                    cp.start()

        wait_gather(t, slot)

        @pl.when(t >= 2)
        def _():
            wait_scatter(t - 2, slot)

        valid = tv_ref[t]
        for live in range(FFN_ROW_STEP, tm + 1, FFN_ROW_STEP):
            @pl.when((valid > live - FFN_ROW_STEP) & (valid <= live))
            def _():
                row = lax.broadcasted_iota(jnp.int32, (live, xbuf.shape[2]), 0)
                x = jnp.where(row < valid, xbuf[slot, :live, :], 0.0).astype(jnp.bfloat16)
                a = _dot(x, w1buf[wslot].astype(jnp.bfloat16))
                b = _dot(x, w3buf[wslot].astype(jnp.bfloat16))
                act = (a * (1.0 / (1.0 + jnp.exp(-a)))) * b
                ybuf[slot, :live, :] = _dot(act.astype(jnp.bfloat16), w2buf[wslot].astype(jnp.bfloat16))
        start_all(t, *scatter(slot))

    @pl.when(t == pl.num_programs(0) - 1)
    def _():
        @pl.when(n_used >= 2)
        def _():
            wait_scatter(n_used - 2, n_used % 2)
        wait_scatter(n_used - 1, (n_used - 1) % 2)


def _moe_ffn(tile_expert, tile_valid, n_used, tile_seg, tile_next, row_sid, row_tok, h2, w1, w3, w2,
             *, tm):
    n, d = h2.shape
    _, _, ff = w1.shape
    max_tiles = tile_expert.shape[0]
    any_spec = pl.BlockSpec(memory_space=pl.ANY)
    return pl.pallas_call(
        _ffn_kernel,
        out_shape=jax.ShapeDtypeStruct((2 * n, d), jnp.float32),
        grid_spec=pltpu.PrefetchScalarGridSpec(
            num_scalar_prefetch=7,
            grid=(max_tiles,),
            in_specs=[any_spec, any_spec, any_spec, any_spec],
            out_specs=any_spec,
            scratch_shapes=[
                pltpu.VMEM((2, tm, d), jnp.float32),
                pltpu.VMEM((2, tm, d), jnp.float32),
                pltpu.VMEM((2, d, ff), jnp.float32),
                pltpu.VMEM((2, d, ff), jnp.float32),
                pltpu.VMEM((2, ff, d), jnp.float32),
                pltpu.SemaphoreType.DMA((2,)),
                pltpu.SemaphoreType.DMA((2,)),
                pltpu.SemaphoreType.DMA((2,)),
            ],
        ),
        compiler_params=_params("arbitrary"),
        name="moe_ffn",
    )(tile_expert, tile_valid, n_used, tile_seg, tile_next, row_sid, row_tok, h2, w1, w3, w2)


def _combine_kernel(x1_ref, route_ref, ya_ref, yb_ref, o_ref):
    route = route_ref[...]
    w1 = route[:, ROUTE_W1:ROUTE_W1 + 1]
    w2 = route[:, ROUTE_W2:ROUTE_W2 + 1]
    o_ref[...] = x1_ref[...] + (w1 * ya_ref[...] + w2 * yb_ref[...])


def _moe_combine(x1, route, y2, *, tm):
    n, d = x1.shape
    return pl.pallas_call(
        _combine_kernel,
        out_shape=jax.ShapeDtypeStruct((n, d), jnp.float32),
        grid=(n // tm,),
        in_specs=[pl.BlockSpec((tm, d), lambda i: (i, 0)),
                  pl.BlockSpec((tm, LANES), lambda i: (i, 0)),
                  pl.BlockSpec((tm, d), lambda i: (i, 0)),
                  pl.BlockSpec((tm, d), lambda i: (i + n // tm, 0))],
        out_specs=pl.BlockSpec((tm, d), lambda i: (i, 0)),
        compiler_params=_params("arbitrary"),
        name="moe_combine",
    )(x1, route, y2, y2)


def _moe(x1, h2, route, w1, w3, w2, *, tm, gather_tm):
    n, d = x1.shape
    max_tiles = (2 * n) // tm + N_EXPERTS
    pos2d, cnt = _moe_plan(route, tm=tm, blk=gather_tm)
    pos = pos2d[:, :2].T.reshape(2 * n)
    counts = cnt[0, :N_EXPERTS].astype(jnp.int32)
    tiles = (counts + (tm - 1)) // tm
    ends = jnp.cumsum(tiles)
    t_idx = jnp.arange(max_tiles, dtype=jnp.int32)
    tile_expert = jnp.sum((ends[None, :] <= t_idx[:, None]).astype(jnp.int32), axis=1)
    tile_expert = jnp.minimum(tile_expert, N_EXPERTS - 1)
    first_tile = (ends - tiles)[tile_expert]
    tile_valid = jnp.clip(counts[tile_expert] - (t_idx - first_tile) * tm, 0, tm).astype(jnp.int32)
    n_used = ends[-1:].astype(jnp.int32)
    used = t_idx < n_used[0]
    is_first = used & (t_idx == first_tile)
    tile_seg = (jnp.cumsum(is_first.astype(jnp.int32)) - 1).astype(jnp.int32)
    next_start = first_tile + tiles[tile_expert]
    next_e = jnp.where(next_start < n_used[0], tile_expert[jnp.minimum(next_start, max_tiles - 1)], NO_NEXT)
    tile_next = jnp.where(is_first, next_e, NOT_FIRST).astype(jnp.int32)

    row_sid = _moe_invert(pos, rows=max_tiles * tm)
    row_tok = jnp.where(row_sid >= n, row_sid - n, row_sid)
    y2 = _moe_ffn(tile_expert, tile_valid, n_used, tile_seg, tile_next, row_sid, row_tok, h2, w1, w3, w2,
                  tm=tm)
    return _moe_combine(x1, route, y2, tm=gather_tm)


def _pack_router(w_group, w_router):
    d = w_group.shape[0]
    experts = jnp.transpose(w_router, (1, 0, 2)).reshape(d, N_EXPERTS)
    wr = jnp.concatenate(
        [experts, w_group, jnp.zeros((d, LANES - N_EXPERTS - N_GROUPS), w_group.dtype)], axis=1)
    hi = wr.astype(jnp.bfloat16)
    lo = (wr - hi.astype(jnp.float32)).astype(jnp.bfloat16)
    return jnp.concatenate([hi, lo], axis=1)


def _rotation_tables(seq):
    half = RET_DK // 2
    pos = jnp.arange(seq, dtype=jnp.float32)
    inv = 1.0 / (ROT_BASE ** jnp.linspace(0.0, 1.0, half, dtype=jnp.float32))
    ang = pos[:, None] * inv[None, :]
    c, s = jnp.cos(ang), jnp.sin(ang)
    return jnp.concatenate([c, c], axis=-1), jnp.concatenate([-s, s], axis=-1)


def _tiles(n, seq):
    def fit(total, want):
        t = min(total, want)
        while total % t:
            t //= 2
        return t
    return dict(
        proj_tm=fit(n, 1024),
        attn_tq=fit(seq, 256), attn_tk=fit(seq, 256),
        ret_rc=fit(seq, 256),
        out_tm=fit(n, 512),
        moe_tm=fit(n, 256), moe_gather_tm=fit(n, 256),
    )


def kernel(x, norm1_g, w_in, q_norm_g, k_norm_g, idx_k_ln_w, idx_k_ln_b, ret_norm_g,
           w_out, norm2_g, w_group, w_router, w1, w3, w2):
    b, seq, d = x.shape
    n = b * seq
    depth = w_in.shape[0]
    t = _tiles(n, seq)
    cos2, sin2 = _rotation_tables(seq)
    log_gamma = jnp.log1p(-jnp.exp2(-5.0 - jnp.arange(RET_HEADS, dtype=jnp.float32)))

    x2d = x.reshape(n, d)
    for l in range(depth):
        proj = _in_proj(x2d, norm1_g[l][None, :], w_in[l].T, tm=t["proj_tm"])
        p3 = proj.reshape(b, seq, proj.shape[1])
        attn = _dsa_attention(p3, q_norm_g[l][None, :], k_norm_g[l][None, :],
                              idx_k_ln_w[l][None, :], idx_k_ln_b[l][None, :],
                              tq=t["attn_tq"], tk=t["attn_tk"])
        ret = _retention(p3, log_gamma, cos2, sin2, ret_norm_g[l].reshape(RET_HEADS, 1, RET_DV),
                         rc=t["ret_rc"])
        r_cat = _pack_router(w_group[l], w_router[l])
        g2 = norm2_g[l][None, :]
        x1, h2, route = _out_proj(attn.reshape(n, ATTN_WIDTH), ret.reshape(n, RET_WIDTH), x2d,
                                  w_out[l].astype(jnp.bfloat16), g2, r_cat, tm=t["out_tm"])
        x2d = _moe(x1, h2, route, w1[l], w3[l], w2[l], tm=t["moe_tm"], gather_tm=t["moe_gather_tm"])
    return x2d.reshape(b, seq, d)
```

```python
import functools
import math

import jax
import jax.numpy as jnp
from jax import lax
from jax.experimental import pallas as pl
from jax.experimental.pallas import tpu as pltpu

CHUNK = 64
ATTN_HEADS = 8
HEAD_DIM = 128
KV_HEADS = 2
HEADS_PER_KV = ATTN_HEADS // KV_HEADS
IDX_HEADS = 16
IDX_DIM = 64
TOPK_MAX = 256
RET_HEADS = 8
RET_DK = 128
RET_DV = 128
ROT_BASE = 10000.0
N_GROUPS = 4
EXPERTS_PER_GROUP = 8
N_EXPERTS = N_GROUPS * EXPERTS_PER_GROUP
EPS = 1e-6

ATTN_WIDTH = ATTN_HEADS * HEAD_DIM
KV_WIDTH = KV_HEADS * HEAD_DIM
IDX_WIDTH = IDX_HEADS * IDX_DIM
RET_WIDTH = RET_HEADS * RET_DK

LANES = 128
SUBLANES = 8
VMEM_LIMIT = 56 * 1024 * 1024

AQ_OFF = 0
AK_OFF = AQ_OFF + ATTN_WIDTH
AV_OFF = AK_OFF + KV_WIDTH
IQ_OFF = AV_OFF + KV_WIDTH
IK_OFF = IQ_OFF + IDX_WIDTH
IW_OFF = IK_OFF + IDX_DIM
W_RET = IW_OFF + IDX_HEADS
IN_WIDTH = W_RET + 4 * RET_WIDTH
assert IW_OFF // LANES == IK_OFF // LANES
PROJ_TN = 1024
RQ_OFF = -(-W_RET // PROJ_TN) * PROJ_TN
RK_OFF = RQ_OFF + RET_WIDTH
RV_OFF = RK_OFF + RET_WIDTH
RG_OFF = RV_OFF + RET_WIDTH
PROJ_WIDTH = RG_OFF + RET_WIDTH

ROUTE_E1, ROUTE_E2, ROUTE_W1, ROUTE_W2 = 0, 1, 2, 3

SUM_ROWS = 16
LOGIT_BOUND_SLACK = 1.05
MAX_SINGLE_SWEEP_BOUND = 50.0

INT_MIN = -(2 ** 31)
NEG_BIG = -1e30

_NT = (((1,), (1,)), ((), ()))


def _dot(a, b):
    return jnp.dot(a, b, preferred_element_type=jnp.float32)


def _dot_nt(a, b):
    return lax.dot_general(a, b, _NT, preferred_element_type=jnp.float32)


def _params(*sem):
    return pltpu.CompilerParams(dimension_semantics=sem, vmem_limit_bytes=VMEM_LIMIT)


def _in_proj_kernel(x_ref, g_ref, wt_ref, o_ref, h_scr, *, row_chunk):
    first = pl.program_id(1) == 0

    @pl.when(first)
    def _():
        w = wt_ref[...].astype(jnp.bfloat16)
        for c in range(x_ref.shape[0] // row_chunk):
            rows = slice(c * row_chunk, (c + 1) * row_chunk)
            x = x_ref[rows, :]
            ms = jnp.mean(x * x, axis=-1, keepdims=True)
            h = ((x * lax.rsqrt(ms + EPS)) * g_ref[...]).astype(jnp.bfloat16)
            h_scr[rows, :] = h
            o_ref[rows, :] = _dot_nt(h, w).astype(o_ref.dtype)

    @pl.when(jnp.logical_not(first))
    def _():
        o_ref[...] = _dot_nt(h_scr[...], wt_ref[...].astype(jnp.bfloat16)).astype(o_ref.dtype)


def _in_proj(x2d, g, w_in_t, *, tm):
    n, d = x2d.shape
    tn = PROJ_TN
    assert w_in_t.shape == (IN_WIDTH, d) and W_RET % SUBLANES == 0 and tn % SUBLANES == 0
    attn_tiles = RQ_OFF // tn

    def window(i, j):
        step = tn // SUBLANES
        start = jnp.where(j < attn_tiles, j * step, W_RET // SUBLANES + (j - attn_tiles) * step)
        return SUBLANES * start, 0

    return pl.pallas_call(
        functools.partial(_in_proj_kernel, row_chunk=min(tm, 256)),
        out_shape=jax.ShapeDtypeStruct((n, PROJ_WIDTH), jnp.bfloat16),
        grid=(n // tm, PROJ_WIDTH // tn),
        in_specs=[
            pl.BlockSpec((tm, d), lambda i, j: (i, 0)),
            pl.BlockSpec((1, d), lambda i, j: (0, 0)),
            pl.BlockSpec((pl.Element(tn), pl.Element(d)), window),
        ],
        out_specs=pl.BlockSpec((tm, tn), lambda i, j: (i, j)),
        scratch_shapes=[pltpu.VMEM((tm, d), jnp.bfloat16)],
        compiler_params=_params("arbitrary", "arbitrary"),
        name="in_proj",
    )(x2d, g, w_in_t)


def _ordered_float(v):
    bits = v ^ ((v >> 31) & jnp.int32(0x7FFFFFFF))
    return pltpu.bitcast(bits, jnp.float32)


def _ordered_bfloat(v16):
    bits16 = v16 ^ ((v16 >> 15) & jnp.int32(0x7FFF))
    return pltpu.bitcast(bits16 << 16, jnp.float32).astype(jnp.bfloat16)


def _attn_kernel(aq_ref, iqa_ref, iqb_ref, iw_ref, ak_ref, av_ref, ik_ref, qg_ref, kg_ref, lnw_ref,
                 lnb_ref, o_ref,
                 kn_scr, ikn_scr, vt_scr, key_scr, kb_scr, wt_scr, qn_scr, acc_scr, s_scr, kmax_scr,
                 *, tk, topk, idx_w_scale):
    i = pl.program_id(1)
    seq = ak_ref.shape[1]
    tq = aq_ref.shape[1]
    chunk_shift = CHUNK.bit_length() - 1

    @pl.when(i == 0)
    def _():
        def body(c, carry):
            rows = pl.ds(pl.multiple_of(c * tk, tk), tk)
            for g in range(KV_HEADS):
                cols = slice(g * HEAD_DIM, (g + 1) * HEAD_DIM)
                k = ak_ref[0, rows, cols].astype(jnp.float32)
                ms = jnp.mean(k * k, axis=-1, keepdims=True)
                kn = (k * lax.rsqrt(ms + EPS)) * kg_ref[...]
                kn_scr[rows, cols] = kn.astype(jnp.bfloat16)
                ksq = jnp.max(jnp.sum(kn * kn, axis=-1, keepdims=True), axis=0, keepdims=True)
                prev = jnp.where(c == 0, 0.0, kmax_scr[g])
                kmax_scr[g] = jnp.maximum(prev, jnp.broadcast_to(ksq, kmax_scr.shape[1:]))
                v = av_ref[0, rows, cols].astype(jnp.float32)
                vt_scr[g, c, :HEAD_DIM, :] = v.T.astype(jnp.bfloat16)
                vt_scr[g, c, HEAD_DIM:, :] = jnp.ones((SUM_ROWS, tk), jnp.bfloat16)
            ki = ik_ref[0, rows, :IDX_DIM].astype(jnp.float32)
            mu = jnp.mean(ki, axis=-1, keepdims=True)
            var = jnp.mean(jnp.square(ki - mu), axis=-1, keepdims=True)
            y = ((ki - mu) * lax.rsqrt(var + EPS) * lnw_ref[...] + lnb_ref[...]).astype(jnp.bfloat16)
            zeros = jnp.zeros_like(y)
            ikn_scr[0, rows, :] = jnp.concatenate([y, zeros], axis=1)
            ikn_scr[1, rows, :] = jnp.concatenate([zeros, y], axis=1)
            return carry
        lax.fori_loop(0, seq // tk, body, 0)

    t0 = i * tq
    n_kt = (t0 + tq) // tk
    scale = (HEAD_DIM ** -0.5) * math.log2(math.e)
    for h in range(ATTN_HEADS):
        g, r = divmod(h, HEADS_PER_KV)
        q = aq_ref[0, :, h * HEAD_DIM:(h + 1) * HEAD_DIM].astype(jnp.float32)
        ms = jnp.mean(q * q, axis=-1, keepdims=True)
        qn_scr[g, r * tq:(r + 1) * tq, :] = (
            (q * lax.rsqrt(ms + EPS)) * qg_ref[...] * scale).astype(jnp.bfloat16)
    ones_rows = jnp.ones((8, HEAD_DIM), jnp.bfloat16)
    bound = []
    for g in range(KV_HEADS):
        qf = qn_scr[g].astype(jnp.float32)
        qsq = _dot_nt(ones_rows, (qf * qf).astype(jnp.bfloat16))[0:1, :]
        kmax = jnp.concatenate([kmax_scr[g, 0:1, :]] * (HEADS_PER_KV * tq // LANES), axis=1)
        bound.append(LOGIT_BOUND_SLACK * jnp.sqrt(qsq * kmax))
    wt_scr[...] = iw_ref[0].astype(jnp.float32).T * idx_w_scale
    w_row = IW_OFF % LANES

    q_chunk = (t0 + lax.broadcasted_iota(jnp.int32, (tk, tq), 1)) >> chunk_shift

    def score_body(kt, carry):
        rows = pl.ds(pl.multiple_of(kt * tk, tk), tk)
        ik_first, ik_second = ikn_scr[0, rows, :], ikn_scr[1, rows, :]
        acc = jnp.zeros((tk, tq), jnp.float32)
        pairs_per_ref = iqa_ref.shape[2] // LANES
        for pair in range(IDX_HEADS // 2):
            src = iqa_ref if pair < pairs_per_ref else iqb_ref
            lane0 = (pair % pairs_per_ref) * LANES
            q_pair = src[0, :, lane0:lane0 + LANES]
            for sub, ik_t in enumerate((ik_first, ik_second)):
                h = 2 * pair + sub
                d = _dot_nt(ik_t, q_pair)
                acc = acc + jnp.maximum(d, 0.0) * wt_scr[w_row + h:w_row + h + 1, :]
        k_chunk = (kt * tk + lax.broadcasted_iota(jnp.int32, (tk, tq), 0)) >> chunk_shift
        score = jnp.where(k_chunk <= q_chunk, acc, -jnp.inf)
        key_scr[rows, :] = score
        kb_scr[rows, :] = score.astype(jnp.bfloat16)
        return carry
    lax.fori_loop(0, n_kt, score_body, 0)

    def tree_sum(hit):
        while hit.shape[0] > 1:
            half = hit.shape[0] // 2
            hit = hit[:half] + hit[half:]
        return hit[0]

    packed = 16

    def coarse_body(it, lo16):
        cand16 = lo16 + lax.shift_left(jnp.int32(1), 15 - it)
        cand_b = _ordered_bfloat(cand16)

        def count_body(kt, part):
            rows = pl.ds(pl.multiple_of(kt * tk, tk), tk)
            hit = jnp.where(kb_scr[rows, :] >= cand_b, jnp.bfloat16(1), jnp.bfloat16(0))
            return part + tree_sum(hit.reshape(tk // packed, packed, tq))
        part = lax.fori_loop(0, n_kt, count_body, jnp.zeros((packed, tq), jnp.bfloat16))
        cnt = jnp.sum(part.astype(jnp.float32), axis=0, keepdims=True)
        return jnp.where(cnt >= float(topk), cand16, lo16)
    lo16 = lax.fori_loop(0, 16, coarse_body, jnp.full((1, tq), -(2 ** 15), jnp.int32))

    fine_bits = 18
    lo0 = jnp.maximum(lo16 - 1, -(2 ** 15)) << 16

    def fine_body(it, lo):
        cand = lo + lax.shift_left(jnp.int32(1), fine_bits - 1 - it)
        cand_f = _ordered_float(cand)

        def count_body(kt, part):
            rows = pl.ds(pl.multiple_of(kt * tk, tk), tk)
            hit = jnp.where(key_scr[rows, :] >= cand_f, 1.0, 0.0)
            return part + tree_sum(hit.reshape(tk // SUBLANES, SUBLANES, tq))
        part = lax.fori_loop(0, n_kt, count_body, jnp.zeros((SUBLANES, tq), jnp.float32))
        cnt = jnp.sum(part, axis=0, keepdims=True)
        return jnp.where(cnt >= float(topk), cand, lo)
    lo = lax.fori_loop(0, fine_bits, fine_body, lo0)
    thr = jnp.where(lo16 == -(2 ** 15), jnp.finfo(jnp.float32).min, _ordered_float(lo))

    acc_scr[...] = jnp.zeros(acc_scr.shape, jnp.float32)

    def masked_logits(kt):
        rows = pl.ds(pl.multiple_of(kt * tk, tk), tk)
        bias = jnp.where(key_scr[rows, :] >= thr, 0.0, NEG_BIG)
        bias = jnp.concatenate([bias] * HEADS_PER_KV, axis=1)
        return [_dot_nt(kn_scr[rows, g * HEAD_DIM:(g + 1) * HEAD_DIM], qn_scr[g]) + bias
                for g in range(KV_HEADS)]

    bound_max = jnp.max(jnp.maximum(bound[0], bound[1]))
    single_sweep = bound_max <= MAX_SINGLE_SWEEP_BOUND

    @pl.when(single_sweep)
    def _():
        def body(kt, carry):
            for g, s in enumerate(masked_logits(kt)):
                acc_scr[g] += _dot(vt_scr[g, kt], jnp.exp2(s - bound[g]).astype(jnp.bfloat16))
            return carry
        lax.fori_loop(0, n_kt, body, 0)

    @pl.when(jnp.logical_not(single_sweep))
    def _():
        def logit_body(kt, m):
            rows = pl.ds(pl.multiple_of(kt * tk, tk), tk)
            new_m = []
            for g, s in enumerate(masked_logits(kt)):
                s_scr[g, rows, :] = s
                new_m.append(jnp.maximum(m[g], jnp.max(s, axis=0, keepdims=True)))
            return tuple(new_m)
        m0 = jnp.full((1, HEADS_PER_KV * tq), NEG_BIG, jnp.float32)
        m = lax.fori_loop(0, n_kt, logit_body, (m0,) * KV_HEADS)

        def pv_body(kt, carry):
            rows = pl.ds(pl.multiple_of(kt * tk, tk), tk)
            for g in range(KV_HEADS):
                p = jnp.exp2(s_scr[g, rows, :] - m[g]).astype(jnp.bfloat16)
                acc_scr[g] += _dot(vt_scr[g, kt], p)
            return carry
        lax.fori_loop(0, n_kt, pv_body, 0)

    for h in range(ATTN_HEADS):
        g, r = divmod(h, HEADS_PER_KV)
        cols = slice(r * tq, (r + 1) * tq)
        o = acc_scr[g, :HEAD_DIM, cols] / acc_scr[g, HEAD_DIM:HEAD_DIM + 1, cols]
        o_ref[0, :, h * HEAD_DIM:(h + 1) * HEAD_DIM] = o.T.astype(o_ref.dtype)


def _dsa_attention(p3, q_g, k_g, ln_w, ln_b, *, tq, tk):
    b, seq, _ = p3.shape
    topk = min(TOPK_MAX, seq // 4)
    idx_w_scale = (IDX_HEADS ** -0.5) * (IDX_DIM ** -0.5)
    assert seq % tq == 0 and tq % tk == 0 and tk % CHUNK == 0

    def col(off, width):
        assert off % width == 0 or width == LANES
        return off // width

    half_iq = IDX_WIDTH // 2
    return pl.pallas_call(
        functools.partial(_attn_kernel, tk=tk, topk=topk, idx_w_scale=idx_w_scale),
        out_shape=jax.ShapeDtypeStruct((b, seq, ATTN_WIDTH), jnp.bfloat16),
        grid=(b, seq // tq),
        in_specs=[
            pl.BlockSpec((1, tq, ATTN_WIDTH), lambda bi, i: (bi, i, col(AQ_OFF, ATTN_WIDTH))),
            pl.BlockSpec((1, tq, half_iq), lambda bi, i: (bi, i, col(IQ_OFF, half_iq))),
            pl.BlockSpec((1, tq, half_iq), lambda bi, i: (bi, i, col(IQ_OFF, half_iq) + 1)),
            pl.BlockSpec((1, tq, LANES), lambda bi, i: (bi, i, col(IW_OFF, LANES))),
            pl.BlockSpec((1, seq, KV_WIDTH), lambda bi, i: (bi, 0, col(AK_OFF, KV_WIDTH))),
            pl.BlockSpec((1, seq, KV_WIDTH), lambda bi, i: (bi, 0, col(AV_OFF, KV_WIDTH))),
            pl.BlockSpec((1, seq, LANES), lambda bi, i: (bi, 0, col(IK_OFF, LANES))),
            pl.BlockSpec((1, HEAD_DIM), lambda bi, i: (0, 0)),
            pl.BlockSpec((1, HEAD_DIM), lambda bi, i: (0, 0)),
            pl.BlockSpec((1, IDX_DIM), lambda bi, i: (0, 0)),
            pl.BlockSpec((1, IDX_DIM), lambda bi, i: (0, 0)),
        ],
        out_specs=pl.BlockSpec((1, tq, ATTN_WIDTH), lambda bi, i: (bi, i, 0)),
        scratch_shapes=[
            pltpu.VMEM((seq, KV_WIDTH), jnp.bfloat16),
            pltpu.VMEM((2, seq, 2 * IDX_DIM), jnp.bfloat16),
            pltpu.VMEM((KV_HEADS, seq // tk, HEAD_DIM + SUM_ROWS, tk), jnp.bfloat16),
            pltpu.VMEM((seq, tq), jnp.float32),
            pltpu.VMEM((seq, tq), jnp.bfloat16),
            pltpu.VMEM((LANES, tq), jnp.float32),
            pltpu.VMEM((KV_HEADS, HEADS_PER_KV * tq, HEAD_DIM), jnp.bfloat16),
            pltpu.VMEM((KV_HEADS, HEAD_DIM + SUM_ROWS, HEADS_PER_KV * tq), jnp.float32),
            pltpu.VMEM((KV_HEADS, seq, HEADS_PER_KV * tq), jnp.float32),
            pltpu.VMEM((KV_HEADS, 8, LANES), jnp.float32),
        ],
        compiler_params=_params("arbitrary", "arbitrary"),
        name="dsa_attn",
    )(p3, p3, p3, p3, p3, p3, p3, q_g, k_g, ln_w, ln_b)


RET_HEADS_PER_STEP = 2


def _ret_kernel(lg_ref, rq_ref, rk_ref, rv_ref, rg_ref, cos_ref, sin_ref, g_ref, o_ref, *, rc):
    seq = rq_ref.shape[1]
    n = lax.broadcasted_iota(jnp.int32, (rc, RET_DV), 0).astype(jnp.float32)
    rel = (lax.broadcasted_iota(jnp.int32, (rc, rc), 0)
           - lax.broadcasted_iota(jnp.int32, (rc, rc), 1)).astype(jnp.float32)

    def rot(x, rows):
        return x * cos_ref[rows, :] + pltpu.roll(x, RET_DK // 2, 1) * sin_ref[rows, :]

    heads = []
    for hh in range(RET_HEADS_PER_STEP):
        lg = lg_ref[pl.program_id(1) * RET_HEADS_PER_STEP + hh]
        heads.append(dict(
            cols=slice(hh * RET_DK, (hh + 1) * RET_DK),
            cross_decay=jnp.exp(lg * (n + 1.0)),
            state_decay=jnp.exp(lg * (rc - 1.0 - n)),
            chunk_decay=jnp.exp(lg * jnp.full((RET_DK, RET_DV), float(rc), jnp.float32)),
            intra=jnp.where(rel >= 0, jnp.exp(lg * jnp.maximum(rel, 0.0)), 0.0),
            state=jnp.zeros((RET_DK, RET_DV), jnp.float32),
            gain=g_ref[hh],
        ))

    for c in range(seq // rc):
        rows = slice(c * rc, (c + 1) * rc)
        for hd in heads:
            cols = hd["cols"]
            q = rot(rq_ref[0, rows, cols].astype(jnp.float32), rows)
            k = rot(rk_ref[0, rows, cols].astype(jnp.float32), rows) * (RET_DK ** -0.5)
            v = rv_ref[0, rows, cols]
            qb = q.astype(jnp.bfloat16)
            inner = _dot_nt(qb, k.astype(jnp.bfloat16)) * hd["intra"]
            o = (_dot(inner.astype(jnp.bfloat16), v)
                 + _dot(qb, hd["state"].astype(jnp.bfloat16)) * hd["cross_decay"])
            kd_t = (k * hd["state_decay"]).T.astype(jnp.bfloat16)
            hd["state"] = hd["state"] * hd["chunk_decay"] + _dot(kd_t, v)
            ms = jnp.mean(o * o, axis=-1, keepdims=True)
            y = (o * lax.rsqrt(ms + EPS)) * hd["gain"]
            gate = rg_ref[0, rows, cols].astype(jnp.float32)
            o_ref[0, rows, cols] = ((gate * (1.0 / (1.0 + jnp.exp(-gate)))) * y).astype(o_ref.dtype)


def _retention(p3, log_gamma, cos2, sin2, ret_g, *, rc):
    b, seq, _ = p3.shape
    hps = RET_HEADS_PER_STEP
    width = hps * RET_DK
    assert seq % rc == 0 and RET_HEADS % hps == 0

    def head_spec(off):
        assert off % width == 0
        return pl.BlockSpec((1, seq, width), lambda bi, h: (bi, 0, off // width + h))

    return pl.pallas_call(
        functools.partial(_ret_kernel, rc=rc),
        out_shape=jax.ShapeDtypeStruct((b, seq, RET_WIDTH), jnp.bfloat16),
        grid=(b, RET_HEADS // hps),
        in_specs=[
            pl.BlockSpec(memory_space=pltpu.SMEM),
            head_spec(RQ_OFF), head_spec(RK_OFF), head_spec(RV_OFF), head_spec(RG_OFF),
            pl.BlockSpec((seq, RET_DK), lambda bi, h: (0, 0)),
            pl.BlockSpec((seq, RET_DK), lambda bi, h: (0, 0)),
            pl.BlockSpec((hps, 1, RET_DV), lambda bi, h: (h, 0, 0)),
        ],
        out_specs=pl.BlockSpec((1, seq, width), lambda bi, h: (bi, 0, h)),
        compiler_params=_params("arbitrary", "arbitrary"),
        name="retention",
    )(log_gamma, p3, p3, p3, p3, cos2, sin2, ret_g)


def _routing(logits):
    lane = lax.broadcasted_iota(jnp.int32, logits.shape, 1).astype(jnp.float32)
    big = float(LANES)
    neg = -jnp.inf

    def first_argmax(v, vmax):
        return jnp.min(jnp.where(v == vmax, lane, big), axis=-1, keepdims=True)

    g_mask = (lane >= N_EXPERTS) & (lane < N_EXPERTS + N_GROUPS)
    gl = jnp.where(g_mask, logits, neg)
    g_max = jnp.max(gl, axis=-1, keepdims=True)
    g_sel = first_argmax(gl, g_max) - N_EXPERTS
    g_gate = 1.0 / jnp.sum(jnp.where(g_mask, jnp.exp(gl - g_max), 0.0), axis=-1, keepdims=True)

    e_lo = g_sel * EXPERTS_PER_GROUP
    el = jnp.where((lane >= e_lo) & (lane < e_lo + EXPERTS_PER_GROUP), logits, neg)
    v1 = jnp.max(el, axis=-1, keepdims=True)
    i1 = first_argmax(el, v1)
    el2 = jnp.where(lane == i1, neg, el)
    v2 = jnp.max(el2, axis=-1, keepdims=True)
    i2 = first_argmax(el2, v2)
    e2 = jnp.exp(v2 - v1)
    denom = 1.0 + e2
    w1 = (1.0 / denom) * g_gate
    w2 = (e2 / denom) * g_gate
    route = jnp.where(lane == ROUTE_E1, i1, 0.0) + jnp.where(lane == ROUTE_E2, i2, 0.0)
    return route + jnp.where(lane == ROUTE_W1, w1, 0.0) + jnp.where(lane == ROUTE_W2, w2, 0.0)


def _norm2(x1, g):
    ms = jnp.mean(x1 * x1, axis=-1, keepdims=True)
    return (x1 * lax.rsqrt(ms + EPS)) * g


OUT_SUBTILES = 2


def _out_proj_kernel(a_ref, r_ref, x_ref, wa_ref, wr_ref, g_ref, rcat_ref,
                     x1_ref, h2_ref, route_ref):
    sub = x_ref.shape[0] // OUT_SUBTILES
    for s in range(OUT_SUBTILES):
        rows = slice(s * sub, (s + 1) * sub)
        mixed = _dot(a_ref[rows, :], wa_ref[...]) + _dot(r_ref[rows, :], wr_ref[...])
        x1 = x_ref[rows, :] + mixed
        x1_ref[rows, :] = x1
        h2 = _norm2(x1, g_ref[...])
        h2_ref[rows, :] = h2
        hi = h2.astype(jnp.bfloat16)
        lo = (h2 - hi.astype(jnp.float32)).astype(jnp.bfloat16)
        both = _dot(hi, rcat_ref[...])
        logits = both[:, :LANES] + (both[:, LANES:] + _dot(lo, rcat_ref[:, :LANES]))
        route_ref[rows, :] = _routing(logits)


def _out_proj(attn2d, ret2d, x2d, w_out_bf, g2, r_cat, *, tm):
    n, d = x2d.shape
    return pl.pallas_call(
        _out_proj_kernel,
        out_shape=(
            jax.ShapeDtypeStruct((n, d), jnp.float32),
            jax.ShapeDtypeStruct((n, d), jnp.float32),
            jax.ShapeDtypeStruct((n, LANES), jnp.float32),
        ),
        grid=(n // tm,),
        in_specs=[
            pl.BlockSpec((tm, ATTN_WIDTH), lambda i: (i, 0)),
            pl.BlockSpec((tm, RET_WIDTH), lambda i: (i, 0)),
            pl.BlockSpec((tm, d), lambda i: (i, 0)),
            pl.BlockSpec((ATTN_WIDTH, d), lambda i: (0, 0)),
            pl.BlockSpec((RET_WIDTH, d), lambda i: (ATTN_WIDTH // RET_WIDTH, 0)),
            pl.BlockSpec((1, d), lambda i: (0, 0)),
            pl.BlockSpec((d, 2 * LANES), lambda i: (0, 0)),
        ],
        out_specs=(
            pl.BlockSpec((tm, d), lambda i: (i, 0)),
            pl.BlockSpec((tm, d), lambda i: (i, 0)),
            pl.BlockSpec((tm, LANES), lambda i: (i, 0)),
        ),
        compiler_params=_params("arbitrary"),
        name="out_proj",
    )(attn2d, ret2d, x2d, w_out_bf, w_out_bf, g2, r_cat)


def _plan_kernel(route_ref, pos_ref, cnt_ref, rank_scr, *, tm, blk):
    n = route_ref.shape[0]
    lane = lax.broadcasted_iota(jnp.int32, (blk, LANES), 1).astype(jnp.float32)
    before = (lax.broadcasted_iota(jnp.int32, (blk, blk), 1)
              < lax.broadcasted_iota(jnp.int32, (blk, blk), 0)).astype(jnp.bfloat16)

    def one_hot(rows):
        r = route_ref[rows, :]
        e1 = r[:, ROUTE_E1:ROUTE_E1 + 1]
        e2 = r[:, ROUTE_E2:ROUTE_E2 + 1]
        return lane == e1, lane == e2

    def rank_body(b, run):
        rows = pl.ds(pl.multiple_of(b * blk, blk), blk)
        m1, m2 = one_hot(rows)
        sel = jnp.where(m1 | m2, 1.0, 0.0)
        rank_scr[rows, :] = _dot(before, sel.astype(jnp.bfloat16)) + run
        return run + jnp.sum(sel, axis=0, keepdims=True)
    cnt = lax.fori_loop(0, n // blk, rank_body, jnp.zeros((1, LANES), jnp.float32), unroll=2)
    cnt_ref[...] = jnp.broadcast_to(cnt, cnt_ref.shape)

    tiles = jnp.floor((cnt + (tm - 1.0)) * (1.0 / tm))
    below = (lax.broadcasted_iota(jnp.int32, (LANES, LANES), 0)
             < lax.broadcasted_iota(jnp.int32, (LANES, LANES), 1)).astype(jnp.bfloat16)
    start = _dot(jnp.broadcast_to(tiles, (8, LANES)).astype(jnp.bfloat16), below)[0:1, :] * float(tm)

    def pos_body(b, carry):
        rows = pl.ds(pl.multiple_of(b * blk, blk), blk)
        m1, m2 = one_hot(rows)
        dest = rank_scr[rows, :] + start
        p1 = jnp.sum(jnp.where(m1, dest, 0.0), axis=-1, keepdims=True)
        p2 = jnp.sum(jnp.where(m2, dest, 0.0), axis=-1, keepdims=True)
        pos_ref[rows, :] = (jnp.where(lane == 0.0, p1, 0.0) + jnp.where(lane == 1.0, p2, 0.0)).astype(jnp.int32)
        return carry
    lax.fori_loop(0, n // blk, pos_body, 0, unroll=2)


def _moe_plan(route, *, tm, blk):
    n = route.shape[0]
    return pl.pallas_call(
        functools.partial(_plan_kernel, tm=tm, blk=blk),
        out_shape=(jax.ShapeDtypeStruct((n, LANES), jnp.int32),
                   jax.ShapeDtypeStruct((8, LANES), jnp.float32)),
        scratch_shapes=[pltpu.VMEM((n, LANES), jnp.float32)],
        compiler_params=pltpu.CompilerParams(vmem_limit_bytes=VMEM_LIMIT),
        name="moe_plan",
    )(route)


def _row_copy(src, src_row, dst, dst_row, sem):
    return pltpu.make_async_copy(src.at[pl.ds(src_row, 1), :], dst.at[pl.ds(dst_row, 1), :], sem)


def _invert_kernel(pos_ref, sid_ref):
    def body(j, carry):
        sid_ref[pos_ref[j]] = j
        return carry
    lax.fori_loop(0, pos_ref.shape[0], body, 0, unroll=8)


def _moe_invert(pos, *, rows):
    return pl.pallas_call(
        _invert_kernel,
        out_shape=jax.ShapeDtypeStruct((rows,), jnp.int32),
        in_specs=[pl.BlockSpec(memory_space=pltpu.SMEM)],
        out_specs=pl.BlockSpec(memory_space=pltpu.SMEM),
        name="moe_invert",
    )(pos)


ROW_GROUP = 8
ROW_BLOCK = 32
FFN_ROW_STEP = 64
NO_NEXT, NOT_FIRST = -1, -2


def _ffn_kernel(te_ref, tv_ref, nu_ref, seg_ref, nexte_ref, sid_ref, tok_ref,
                h2_ref, w1_ref, w3_ref, w2_ref, y2_ref,
                xbuf, ybuf, w1buf, w3buf, w2buf, gsem, ssem, wsem):
    t = pl.program_id(0)
    n_used = nu_ref[0]
    tm = xbuf.shape[1]

    def weight_copies(e, wslot):
        return [pltpu.make_async_copy(src.at[e], dst.at[wslot], wsem.at[wslot])
                for src, dst in ((w1_ref, w1buf), (w3_ref, w3buf), (w2_ref, w2buf))]

    def gather(slot):
        return tok_ref, lambda r, tok: _row_copy(h2_ref, tok, xbuf.at[slot], r, gsem.at[slot])

    def scatter(slot):
        return sid_ref, lambda r, sid: _row_copy(ybuf.at[slot], r, y2_ref, sid, ssem.at[slot])

    def start_all(tile, table, copy):
        valid = tv_ref[tile]
        for blk in range(tm // ROW_BLOCK):
            @pl.when(valid >= (blk + 1) * ROW_BLOCK)
            def _():
                for r in range(blk * ROW_BLOCK, (blk + 1) * ROW_BLOCK):
                    copy(r, table[tile * tm + r]).start()
        done = (valid // ROW_BLOCK) * ROW_BLOCK

        def body(c, carry):
            for u in range(ROW_GROUP):
                r = done + c * ROW_GROUP + u

                @pl.when(r < valid)
                def _():
                    copy(r, table[tile * tm + r]).start()
            return carry
        lax.fori_loop(0, (valid - done + (ROW_GROUP - 1)) // ROW_GROUP, body, 0)

    def wait_all(tile, copy, block_copy):
        valid = tv_ref[tile]
        for blk in range(tm // ROW_BLOCK):
            @pl.when(valid >= (blk + 1) * ROW_BLOCK)
            def _():
                block_copy.wait()

        def body(r, carry):
            copy(0, 0).wait()
            return carry
        lax.fori_loop(0, valid % ROW_BLOCK, body, 0)

    def wait_gather(tile, slot):
        wait_all(tile, gather(slot)[1],
                 pltpu.make_async_copy(h2_ref.at[pl.ds(0, ROW_BLOCK), :],
                                       xbuf.at[slot, pl.ds(0, ROW_BLOCK), :], gsem.at[slot]))

    def wait_scatter(tile, slot):
        wait_all(tile, scatter(slot)[1],
                 pltpu.make_async_copy(ybuf.at[slot, pl.ds(0, ROW_BLOCK), :],
                                       y2_ref.at[pl.ds(0, ROW_BLOCK), :], ssem.at[slot]))

    @pl.when(t == 0)
    def _():
        for cp in weight_copies(te_ref[0], 0):
            cp.start()
        start_all(0, *gather(0))

    @pl.when(t < n_used)
    def _():
        slot = t % 2
        wslot = seg_ref[t] % 2

        @pl.when(t + 1 < n_used)
        def _():
            start_all(t + 1, *gather(1 - slot))

        @pl.when(nexte_ref[t] != NOT_FIRST)
        def _():
            for cp in weight_copies(te_ref[t], wslot):
                cp.wait()

            @pl.when(nexte_ref[t] >= 0)
            def _():
                for cp in weight_copies(nexte_ref[t], 1 - wslot):
                    cp.start()

        wait_gather(t, slot)

        @pl.when(t >= 2)
        def _():
            wait_scatter(t - 2, slot)

        valid = tv_ref[t]
        for live in range(FFN_ROW_STEP, tm + 1, FFN_ROW_STEP):
            @pl.when((valid > live - FFN_ROW_STEP) & (valid <= live))
            def _():
                row = lax.broadcasted_iota(jnp.int32, (live, xbuf.shape[2]), 0)
                x = jnp.where(row < valid, xbuf[slot, :live, :], 0.0).astype(jnp.bfloat16)
                a = _dot(x, w1buf[wslot].astype(jnp.bfloat16))
                b = _dot(x, w3buf[wslot].astype(jnp.bfloat16))
                act = (a * (1.0 / (1.0 + jnp.exp(-a)))) * b
                ybuf[slot, :live, :] = _dot(act.astype(jnp.bfloat16), w2buf[wslot].astype(jnp.bfloat16))
        start_all(t, *scatter(slot))

    @pl.when(t == pl.num_programs(0) - 1)
    def _():
        @pl.when(n_used >= 2)
        def _():
            wait_scatter(n_used - 2, n_used % 2)
        wait_scatter(n_used - 1, (n_used - 1) % 2)


def _moe_ffn(tile_expert, tile_valid, n_used, tile_seg, tile_next, row_sid, row_tok, h2, w1, w3, w2,
             *, tm):
    n, d = h2.shape
    _, _, ff = w1.shape
    max_tiles = tile_expert.shape[0]
    any_spec = pl.BlockSpec(memory_space=pl.ANY)
    return pl.pallas_call(
        _ffn_kernel,
        out_shape=jax.ShapeDtypeStruct((2 * n, d), jnp.float32),
        grid_spec=pltpu.PrefetchScalarGridSpec(
            num_scalar_prefetch=7,
            grid=(max_tiles,),
            in_specs=[any_spec, any_spec, any_spec, any_spec],
            out_specs=any_spec,
            scratch_shapes=[
                pltpu.VMEM((2, tm, d), jnp.float32),
                pltpu.VMEM((2, tm, d), jnp.float32),
                pltpu.VMEM((2, d, ff), jnp.float32),
                pltpu.VMEM((2, d, ff), jnp.float32),
                pltpu.VMEM((2, ff, d), jnp.float32),
                pltpu.SemaphoreType.DMA((2,)),
                pltpu.SemaphoreType.DMA((2,)),
                pltpu.SemaphoreType.DMA((2,)),
            ],
        ),
        compiler_params=_params("arbitrary"),
        name="moe_ffn",
    )(tile_expert, tile_valid, n_used, tile_seg, tile_next, row_sid, row_tok, h2, w1, w3, w2)


def _combine_kernel(x1_ref, route_ref, ya_ref, yb_ref, o_ref):
    route = route_ref[...]
    w1 = route[:, ROUTE_W1:ROUTE_W1 + 1]
    w2 = route[:, ROUTE_W2:ROUTE_W2 + 1]
    o_ref[...] = x1_ref[...] + (w1 * ya_ref[...] + w2 * yb_ref[...])


def _moe_combine(x1, route, y2, *, tm):
    n, d = x1.shape
    return pl.pallas_call(
        _combine_kernel,
        out_shape=jax.ShapeDtypeStruct((n, d), jnp.float32),
        grid=(n // tm,),
        in_specs=[pl.BlockSpec((tm, d), lambda i: (i, 0)),
                  pl.BlockSpec((tm, LANES), lambda i: (i, 0)),
                  pl.BlockSpec((tm, d), lambda i: (i, 0)),
                  pl.BlockSpec((tm, d), lambda i: (i + n // tm, 0))],
        out_specs=pl.BlockSpec((tm, d), lambda i: (i, 0)),
        compiler_params=_params("arbitrary"),
        name="moe_combine",
    )(x1, route, y2, y2)


def _moe(x1, h2, route, w1, w3, w2, *, tm, gather_tm):
    n, d = x1.shape
    max_tiles = (2 * n) // tm + N_EXPERTS
    pos2d, cnt = _moe_plan(route, tm=tm, blk=gather_tm)
    pos = pos2d[:, :2].T.reshape(2 * n)
    counts = cnt[0, :N_EXPERTS].astype(jnp.int32)
    tiles = (counts + (tm - 1)) // tm
    ends = jnp.cumsum(tiles)
    t_idx = jnp.arange(max_tiles, dtype=jnp.int32)
    tile_expert = jnp.sum((ends[None, :] <= t_idx[:, None]).astype(jnp.int32), axis=1)
    tile_expert = jnp.minimum(tile_expert, N_EXPERTS - 1)
    first_tile = (ends - tiles)[tile_expert]
    tile_valid = jnp.clip(counts[tile_expert] - (t_idx - first_tile) * tm, 0, tm).astype(jnp.int32)
    n_used = ends[-1:].astype(jnp.int32)
    used = t_idx < n_used[0]
    is_first = used & (t_idx == first_tile)
    tile_seg = (jnp.cumsum(is_first.astype(jnp.int32)) - 1).astype(jnp.int32)
    next_start = first_tile + tiles[tile_expert]
    next_e = jnp.where(next_start < n_used[0], tile_expert[jnp.minimum(next_start, max_tiles - 1)], NO_NEXT)
    tile_next = jnp.where(is_first, next_e, NOT_FIRST).astype(jnp.int32)

    row_sid = _moe_invert(pos, rows=max_tiles * tm)
    row_tok = jnp.where(row_sid >= n, row_sid - n, row_sid)
    y2 = _moe_ffn(tile_expert, tile_valid, n_used, tile_seg, tile_next, row_sid, row_tok, h2, w1, w3, w2,
                  tm=tm)
    return _moe_combine(x1, route, y2, tm=gather_tm)


def _pack_router(w_group, w_router):
    d = w_group.shape[0]
    experts = jnp.transpose(w_router, (1, 0, 2)).reshape(d, N_EXPERTS)
    wr = jnp.concatenate(
        [experts, w_group, jnp.zeros((d, LANES - N_EXPERTS - N_GROUPS), w_group.dtype)], axis=1)
    hi = wr.astype(jnp.bfloat16)
    lo = (wr - hi.astype(jnp.float32)).astype(jnp.bfloat16)
    return jnp.concatenate([hi, lo], axis=1)


def _rotation_tables(seq):
    half = RET_DK // 2
    pos = jnp.arange(seq, dtype=jnp.float32)
    inv = 1.0 / (ROT_BASE ** jnp.linspace(0.0, 1.0, half, dtype=jnp.float32))
    ang = pos[:, None] * inv[None, :]
    c, s = jnp.cos(ang), jnp.sin(ang)
    return jnp.concatenate([c, c], axis=-1), jnp.concatenate([-s, s], axis=-1)


def _tiles(n, seq):
    def fit(total, want):
        t = min(total, want)
        while total % t:
            t //= 2
        return t
    return dict(
        proj_tm=fit(n, 1024),
        attn_tq=fit(seq, 256), attn_tk=fit(seq, 256),
        ret_rc=fit(seq, 256),
        out_tm=fit(n, 512),
        moe_tm=fit(n, 256), moe_gather_tm=fit(n, 256),
    )


def kernel(x, norm1_g, w_in, q_norm_g, k_norm_g, idx_k_ln_w, idx_k_ln_b, ret_norm_g,
           w_out, norm2_g, w_group, w_router, w1, w3, w2):
    b, seq, d = x.shape
    n = b * seq
    depth = w_in.shape[0]
    t = _tiles(n, seq)
    cos2, sin2 = _rotation_tables(seq)
    log_gamma = jnp.log1p(-jnp.exp2(-5.0 - jnp.arange(RET_HEADS, dtype=jnp.float32)))

    x2d = x.reshape(n, d)
    for l in range(depth):
        proj = _in_proj(x2d, norm1_g[l][None, :], w_in[l].T, tm=t["proj_tm"])
        p3 = proj.reshape(b, seq, proj.shape[1])
        attn = _dsa_attention(p3, q_norm_g[l][None, :], k_norm_g[l][None, :],
                              idx_k_ln_w[l][None, :], idx_k_ln_b[l][None, :],
                              tq=t["attn_tq"], tk=t["attn_tk"])
        ret = _retention(p3, log_gamma, cos2, sin2, ret_norm_g[l].reshape(RET_HEADS, 1, RET_DV),
                         rc=t["ret_rc"])
        r_cat = _pack_router(w_group[l], w_router[l])
        g2 = norm2_g[l][None, :]
        x1, h2, route = _out_proj(attn.reshape(n, ATTN_WIDTH), ret.reshape(n, RET_WIDTH), x2d,
                                  w_out[l].astype(jnp.bfloat16), g2, r_cat, tm=t["out_tm"])
        x2d = _moe(x1, h2, route, w1[l], w3[l], w2[l], tm=t["moe_tm"], gather_tm=t["moe_gather_tm"])
    return x2d.reshape(b, seq, d)
```

```python
import functools
import math

import jax
import jax.numpy as jnp
from jax import lax
from jax.experimental import pallas as pl
from jax.experimental.pallas import tpu as pltpu

CHUNK = 64
ATTN_HEADS = 8
HEAD_DIM = 128
KV_HEADS = 2
HEADS_PER_KV = ATTN_HEADS // KV_HEADS
IDX_HEADS = 16
IDX_DIM = 64
TOPK_MAX = 256
RET_HEADS = 8
RET_DK = 128
RET_DV = 128
ROT_BASE = 10000.0
N_GROUPS = 4
EXPERTS_PER_GROUP = 8
N_EXPERTS = N_GROUPS * EXPERTS_PER_GROUP
EPS = 1e-6

ATTN_WIDTH = ATTN_HEADS * HEAD_DIM
KV_WIDTH = KV_HEADS * HEAD_DIM
IDX_WIDTH = IDX_HEADS * IDX_DIM
RET_WIDTH = RET_HEADS * RET_DK

LANES = 128
SUBLANES = 8
VMEM_LIMIT = 56 * 1024 * 1024

AQ_OFF = 0
AK_OFF = AQ_OFF + ATTN_WIDTH
AV_OFF = AK_OFF + KV_WIDTH
IQ_OFF = AV_OFF + KV_WIDTH
IK_OFF = IQ_OFF + IDX_WIDTH
IW_OFF = IK_OFF + IDX_DIM
W_RET = IW_OFF + IDX_HEADS
IN_WIDTH = W_RET + 4 * RET_WIDTH
assert IW_OFF // LANES == IK_OFF // LANES
PROJ_TN = 1024
RQ_OFF = -(-W_RET // PROJ_TN) * PROJ_TN
RK_OFF = RQ_OFF + RET_WIDTH
RV_OFF = RK_OFF + RET_WIDTH
RG_OFF = RV_OFF + RET_WIDTH
PROJ_WIDTH = RG_OFF + RET_WIDTH

ROUTE_E1, ROUTE_E2, ROUTE_W1, ROUTE_W2 = 0, 1, 2, 3

SUM_ROWS = 16
LOGIT_BOUND_SLACK = 1.05
MAX_SINGLE_SWEEP_BOUND = 50.0

INT_MIN = -(2 ** 31)
NEG_BIG = -1e30

_NT = (((1,), (1,)), ((), ()))


def _dot(a, b):
    return jnp.dot(a, b, preferred_element_type=jnp.float32)


def _dot_nt(a, b):
    return lax.dot_general(a, b, _NT, preferred_element_type=jnp.float32)


def _params(*sem):
    return pltpu.CompilerParams(dimension_semantics=sem, vmem_limit_bytes=VMEM_LIMIT)


def _in_proj_kernel(x_ref, g_ref, wt_ref, o_ref, h_scr, *, row_chunk):
    first = pl.program_id(1) == 0

    @pl.when(first)
    def _():
        w = wt_ref[...].astype(jnp.bfloat16)
        for c in range(x_ref.shape[0] // row_chunk):
            rows = slice(c * row_chunk, (c + 1) * row_chunk)
            x = x_ref[rows, :]
            ms = jnp.mean(x * x, axis=-1, keepdims=True)
            h = ((x * lax.rsqrt(ms + EPS)) * g_ref[...]).astype(jnp.bfloat16)
            h_scr[rows, :] = h
            o_ref[rows, :] = _dot_nt(h, w).astype(o_ref.dtype)

    @pl.when(jnp.logical_not(first))
    def _():
        o_ref[...] = _dot_nt(h_scr[...], wt_ref[...].astype(jnp.bfloat16)).astype(o_ref.dtype)


def _in_proj(x2d, g, w_in_t, *, tm):
    n, d = x2d.shape
    tn = PROJ_TN
    assert w_in_t.shape == (IN_WIDTH, d) and W_RET % SUBLANES == 0 and tn % SUBLANES == 0
    attn_tiles = RQ_OFF // tn

    def window(i, j):
        step = tn // SUBLANES
        start = jnp.where(j < attn_tiles, j * step, W_RET // SUBLANES + (j - attn_tiles) * step)
        return SUBLANES * start, 0

    return pl.pallas_call(
        functools.partial(_in_proj_kernel, row_chunk=min(tm, 256)),
        out_shape=jax.ShapeDtypeStruct((n, PROJ_WIDTH), jnp.bfloat16),
        grid=(n // tm, PROJ_WIDTH // tn),
        in_specs=[
            pl.BlockSpec((tm, d), lambda i, j: (i, 0)),
            pl.BlockSpec((1, d), lambda i, j: (0, 0)),
            pl.BlockSpec((pl.Element(tn), pl.Element(d)), window),
        ],
        out_specs=pl.BlockSpec((tm, tn), lambda i, j: (i, j)),
        scratch_shapes=[pltpu.VMEM((tm, d), jnp.bfloat16)],
        compiler_params=_params("arbitrary", "arbitrary"),
        name="in_proj",
    )(x2d, g, w_in_t)


def _ordered_float(v):
    bits = v ^ ((v >> 31) & jnp.int32(0x7FFFFFFF))
    return pltpu.bitcast(bits, jnp.float32)


def _ordered_bfloat(v16):
    bits16 = v16 ^ ((v16 >> 15) & jnp.int32(0x7FFF))
    return pltpu.bitcast(bits16 << 16, jnp.float32).astype(jnp.bfloat16)


def _attn_kernel(aq_ref, iqa_ref, iqb_ref, iw_ref, ak_ref, av_ref, ik_ref, qg_ref, kg_ref, lnw_ref,
                 lnb_ref, o_ref,
                 kn_scr, ikn_scr, vt_scr, key_scr, kb_scr, wt_scr, qn_scr, acc_scr, s_scr, kmax_scr,
                 *, tk, topk, idx_w_scale):
    i = pl.program_id(1)
    seq = ak_ref.shape[1]
    tq = aq_ref.shape[1]
    chunk_shift = CHUNK.bit_length() - 1

    @pl.when(i == 0)
    def _():
        def body(c, carry):
            rows = pl.ds(pl.multiple_of(c * tk, tk), tk)
            for g in range(KV_HEADS):
                cols = slice(g * HEAD_DIM, (g + 1) * HEAD_DIM)
                k = ak_ref[0, rows, cols].astype(jnp.float32)
                ms = jnp.mean(k * k, axis=-1, keepdims=True)
                kn = (k * lax.rsqrt(ms + EPS)) * kg_ref[...]
                kn_scr[rows, cols] = kn.astype(jnp.bfloat16)
                ksq = jnp.max(jnp.sum(kn * kn, axis=-1, keepdims=True), axis=0, keepdims=True)
                prev = jnp.where(c == 0, 0.0, kmax_scr[g])
                kmax_scr[g] = jnp.maximum(prev, jnp.broadcast_to(ksq, kmax_scr.shape[1:]))
                v = av_ref[0, rows, cols].astype(jnp.float32)
                vt_scr[g, c, :HEAD_DIM, :] = v.T.astype(jnp.bfloat16)
                vt_scr[g, c, HEAD_DIM:, :] = jnp.ones((SUM_ROWS, tk), jnp.bfloat16)
            ki = ik_ref[0, rows, :IDX_DIM].astype(jnp.float32)
            mu = jnp.mean(ki, axis=-1, keepdims=True)
            var = jnp.mean(jnp.square(ki - mu), axis=-1, keepdims=True)
            y = ((ki - mu) * lax.rsqrt(var + EPS) * lnw_ref[...] + lnb_ref[...]).astype(jnp.bfloat16)
            zeros = jnp.zeros_like(y)
            ikn_scr[0, rows, :] = jnp.concatenate([y, zeros], axis=1)
            ikn_scr[1, rows, :] = jnp.concatenate([zeros, y], axis=1)
            return carry
        lax.fori_loop(0, seq // tk, body, 0)

    t0 = i * tq
    n_kt = (t0 + tq) // tk
    scale = (HEAD_DIM ** -0.5) * math.log2(math.e)
    for h in range(ATTN_HEADS):
        g, r = divmod(h, HEADS_PER_KV)
        q = aq_ref[0, :, h * HEAD_DIM:(h + 1) * HEAD_DIM].astype(jnp.float32)
        ms = jnp.mean(q * q, axis=-1, keepdims=True)
        qn_scr[g, r * tq:(r + 1) * tq, :] = (
            (q * lax.rsqrt(ms + EPS)) * qg_ref[...] * scale).astype(jnp.bfloat16)
    ones_rows = jnp.ones((8, HEAD_DIM), jnp.bfloat16)
    bound = []
    for g in range(KV_HEADS):
        qf = qn_scr[g].astype(jnp.float32)
        qsq = _dot_nt(ones_rows, (qf * qf).astype(jnp.bfloat16))[0:1, :]
        kmax = jnp.concatenate([kmax_scr[g, 0:1, :]] * (HEADS_PER_KV * tq // LANES), axis=1)
        bound.append(LOGIT_BOUND_SLACK * jnp.sqrt(qsq * kmax))
    wt_scr[...] = iw_ref[0].astype(jnp.float32).T * idx_w_scale
    w_row = IW_OFF % LANES

    q_chunk = (t0 + lax.broadcasted_iota(jnp.int32, (tk, tq), 1)) >> chunk_shift

    def score_body(kt, carry):
        rows = pl.ds(pl.multiple_of(kt * tk, tk), tk)
        ik_first, ik_second = ikn_scr[0, rows, :], ikn_scr[1, rows, :]
        acc = jnp.zeros((tk, tq), jnp.float32)
        pairs_per_ref = iqa_ref.shape[2] // LANES
        for pair in range(IDX_HEADS // 2):
            src = iqa_ref if pair < pairs_per_ref else iqb_ref
            lane0 = (pair % pairs_per_ref) * LANES
            q_pair = src[0, :, lane0:lane0 + LANES]
            for sub, ik_t in enumerate((ik_first, ik_second)):
                h = 2 * pair + sub
                d = _dot_nt(ik_t, q_pair)
                acc = acc + jnp.maximum(d, 0.0) * wt_scr[w_row + h:w_row + h + 1, :]
        k_chunk = (kt * tk + lax.broadcasted_iota(jnp.int32, (tk, tq), 0)) >> chunk_shift
        score = jnp.where(k_chunk <= q_chunk, acc, -jnp.inf)
        key_scr[rows, :] = score
        kb_scr[rows, :] = score.astype(jnp.bfloat16)
        return carry
    lax.fori_loop(0, n_kt, score_body, 0)

    def tree_sum(hit):
        while hit.shape[0] > 1:
            half = hit.shape[0] // 2
            hit = hit[:half] + hit[half:]
        return hit[0]

    packed = 16

    def coarse_body(it, lo16):
        cand16 = lo16 + lax.shift_left(jnp.int32(1), 15 - it)
        cand_b = _ordered_bfloat(cand16)

        def count_body(kt, part):
            rows = pl.ds(pl.multiple_of(kt * tk, tk), tk)
            hit = jnp.where(kb_scr[rows, :] >= cand_b, jnp.bfloat16(1), jnp.bfloat16(0))
            return part + tree_sum(hit.reshape(tk // packed, packed, tq))
        part = lax.fori_loop(0, n_kt, count_body, jnp.zeros((packed, tq), jnp.bfloat16))
        cnt = jnp.sum(part.astype(jnp.float32), axis=0, keepdims=True)
        return jnp.where(cnt >= float(topk), cand16, lo16)
    lo16 = lax.fori_loop(0, 16, coarse_body, jnp.full((1, tq), -(2 ** 15), jnp.int32))

    fine_bits = 18
    lo0 = jnp.maximum(lo16 - 1, -(2 ** 15)) << 16

    def fine_body(it, lo):
        cand = lo + lax.shift_left(jnp.int32(1), fine_bits - 1 - it)
        cand_f = _ordered_float(cand)

        def count_body(kt, part):
            rows = pl.ds(pl.multiple_of(kt * tk, tk), tk)
            hit = jnp.where(key_scr[rows, :] >= cand_f, 1.0, 0.0)
            return part + tree_sum(hit.reshape(tk // SUBLANES, SUBLANES, tq))
        part = lax.fori_loop(0, n_kt, count_body, jnp.zeros((SUBLANES, tq), jnp.float32))
        cnt = jnp.sum(part, axis=0, keepdims=True)
        return jnp.where(cnt >= float(topk), cand, lo)
    lo = lax.fori_loop(0, fine_bits, fine_body, lo0)
    thr = jnp.where(lo16 == -(2 ** 15), jnp.finfo(jnp.float32).min, _ordered_float(lo))

    acc_scr[...] = jnp.zeros(acc_scr.shape, jnp.float32)

    def masked_logits(kt):
        rows = pl.ds(pl.multiple_of(kt * tk, tk), tk)
        bias = jnp.where(key_scr[rows, :] >= thr, 0.0, NEG_BIG)
        bias = jnp.concatenate([bias] * HEADS_PER_KV, axis=1)
        return [_dot_nt(kn_scr[rows, g * HEAD_DIM:(g + 1) * HEAD_DIM], qn_scr[g]) + bias
                for g in range(KV_HEADS)]

    bound_max = jnp.max(jnp.maximum(bound[0], bound[1]))
    single_sweep = bound_max <= MAX_SINGLE_SWEEP_BOUND

    @pl.when(single_sweep)
    def _():
        def body(kt, carry):
            for g, s in enumerate(masked_logits(kt)):
                acc_scr[g] += _dot(vt_scr[g, kt], jnp.exp2(s - bound[g]).astype(jnp.bfloat16))
            return carry
        lax.fori_loop(0, n_kt, body, 0)

    @pl.when(jnp.logical_not(single_sweep))
    def _():
        def logit_body(kt, m):
            rows = pl.ds(pl.multiple_of(kt * tk, tk), tk)
            new_m = []
            for g, s in enumerate(masked_logits(kt)):
                s_scr[g, rows, :] = s
                new_m.append(jnp.maximum(m[g], jnp.max(s, axis=0, keepdims=True)))
            return tuple(new_m)
        m0 = jnp.full((1, HEADS_PER_KV * tq), NEG_BIG, jnp.float32)
        m = lax.fori_loop(0, n_kt, logit_body, (m0,) * KV_HEADS)

        def pv_body(kt, carry):
            rows = pl.ds(pl.multiple_of(kt * tk, tk), tk)
            for g in range(KV_HEADS):
                p = jnp.exp2(s_scr[g, rows, :] - m[g]).astype(jnp.bfloat16)
                acc_scr[g] += _dot(vt_scr[g, kt], p)
            return carry
        lax.fori_loop(0, n_kt, pv_body, 0)

    for h in range(ATTN_HEADS):
        g, r = divmod(h, HEADS_PER_KV)
        cols = slice(r * tq, (r + 1) * tq)
        o = acc_scr[g, :HEAD_DIM, cols] / acc_scr[g, HEAD_DIM:HEAD_DIM + 1, cols]
        o_ref[0, :, h * HEAD_DIM:(h + 1) * HEAD_DIM] = o.T.astype(o_ref.dtype)


def _dsa_attention(p3, q_g, k_g, ln_w, ln_b, *, tq, tk):
    b, seq, _ = p3.shape
    topk = min(TOPK_MAX, seq // 4)
    idx_w_scale = (IDX_HEADS ** -0.5) * (IDX_DIM ** -0.5)
    assert seq % tq == 0 and tq % tk == 0 and tk % CHUNK == 0

    def col(off, width):
        assert off % width == 0 or width == LANES
        return off // width

    half_iq = IDX_WIDTH // 2
    return pl.pallas_call(
        functools.partial(_attn_kernel, tk=tk, topk=topk, idx_w_scale=idx_w_scale),
        out_shape=jax.ShapeDtypeStruct((b, seq, ATTN_WIDTH), jnp.bfloat16),
        grid=(b, seq // tq),
        in_specs=[
            pl.BlockSpec((1, tq, ATTN_WIDTH), lambda bi, i: (bi, i, col(AQ_OFF, ATTN_WIDTH))),
            pl.BlockSpec((1, tq, half_iq), lambda bi, i: (bi, i, col(IQ_OFF, half_iq))),
            pl.BlockSpec((1, tq, half_iq), lambda bi, i: (bi, i, col(IQ_OFF, half_iq) + 1)),
            pl.BlockSpec((1, tq, LANES), lambda bi, i: (bi, i, col(IW_OFF, LANES))),
            pl.BlockSpec((1, seq, KV_WIDTH), lambda bi, i: (bi, 0, col(AK_OFF, KV_WIDTH))),
            pl.BlockSpec((1, seq, KV_WIDTH), lambda bi, i: (bi, 0, col(AV_OFF, KV_WIDTH))),
            pl.BlockSpec((1, seq, LANES), lambda bi, i: (bi, 0, col(IK_OFF, LANES))),
            pl.BlockSpec((1, HEAD_DIM), lambda bi, i: (0, 0)),
            pl.BlockSpec((1, HEAD_DIM), lambda bi, i: (0, 0)),
            pl.BlockSpec((1, IDX_DIM), lambda bi, i: (0, 0)),
            pl.BlockSpec((1, IDX_DIM), lambda bi, i: (0, 0)),
        ],
        out_specs=pl.BlockSpec((1, tq, ATTN_WIDTH), lambda bi, i: (bi, i, 0)),
        scratch_shapes=[
            pltpu.VMEM((seq, KV_WIDTH), jnp.bfloat16),
            pltpu.VMEM((2, seq, 2 * IDX_DIM), jnp.bfloat16),
            pltpu.VMEM((KV_HEADS, seq // tk, HEAD_DIM + SUM_ROWS, tk), jnp.bfloat16),
            pltpu.VMEM((seq, tq), jnp.float32),
            pltpu.VMEM((seq, tq), jnp.bfloat16),
            pltpu.VMEM((LANES, tq), jnp.float32),
            pltpu.VMEM((KV_HEADS, HEADS_PER_KV * tq, HEAD_DIM), jnp.bfloat16),
            pltpu.VMEM((KV_HEADS, HEAD_DIM + SUM_ROWS, HEADS_PER_KV * tq), jnp.float32),
            pltpu.VMEM((KV_HEADS, seq, HEADS_PER_KV * tq), jnp.float32),
            pltpu.VMEM((KV_HEADS, 8, LANES), jnp.float32),
        ],
        compiler_params=_params("arbitrary", "arbitrary"),
        name="dsa_attn",
    )(p3, p3, p3, p3, p3, p3, p3, q_g, k_g, ln_w, ln_b)


RET_HEADS_PER_STEP = 2


def _ret_kernel(lg_ref, rq_ref, rk_ref, rv_ref, rg_ref, cos_ref, sin_ref, g_ref, o_ref, *, rc):
    seq = rq_ref.shape[1]
    n = lax.broadcasted_iota(jnp.int32, (rc, RET_DV), 0).astype(jnp.float32)
    rel = (lax.broadcasted_iota(jnp.int32, (rc, rc), 0)
           - lax.broadcasted_iota(jnp.int32, (rc, rc), 1)).astype(jnp.float32)

    def rot(x, rows):
        return x * cos_ref[rows, :] + pltpu.roll(x, RET_DK // 2, 1) * sin_ref[rows, :]

    heads = []
    for hh in range(RET_HEADS_PER_STEP):
        lg = lg_ref[pl.program_id(1) * RET_HEADS_PER_STEP + hh]
        heads.append(dict(
            cols=slice(hh * RET_DK, (hh + 1) * RET_DK),
            cross_decay=jnp.exp(lg * (n + 1.0)),
            state_decay=jnp.exp(lg * (rc - 1.0 - n)),
            chunk_decay=jnp.exp(lg * jnp.full((RET_DK, RET_DV), float(rc), jnp.float32)),
            intra=jnp.where(rel >= 0, jnp.exp(lg * jnp.maximum(rel, 0.0)), 0.0),
            state=jnp.zeros((RET_DK, RET_DV), jnp.float32),
            gain=g_ref[hh],
        ))

    for c in range(seq // rc):
        rows = slice(c * rc, (c + 1) * rc)
        for hd in heads:
            cols = hd["cols"]
            q = rot(rq_ref[0, rows, cols].astype(jnp.float32), rows)
            k = rot(rk_ref[0, rows, cols].astype(jnp.float32), rows) * (RET_DK ** -0.5)
            v = rv_ref[0, rows, cols]
            qb = q.astype(jnp.bfloat16)
            inner = _dot_nt(qb, k.astype(jnp.bfloat16)) * hd["intra"]
            o = (_dot(inner.astype(jnp.bfloat16), v)
                 + _dot(qb, hd["state"].astype(jnp.bfloat16)) * hd["cross_decay"])
            kd_t = (k * hd["state_decay"]).T.astype(jnp.bfloat16)
            hd["state"] = hd["state"] * hd["chunk_decay"] + _dot(kd_t, v)
            ms = jnp.mean(o * o, axis=-1, keepdims=True)
            y = (o * lax.rsqrt(ms + EPS)) * hd["gain"]
            gate = rg_ref[0, rows, cols].astype(jnp.float32)
            o_ref[0, rows, cols] = ((gate * (1.0 / (1.0 + jnp.exp(-gate)))) * y).astype(o_ref.dtype)


def _retention(p3, log_gamma, cos2, sin2, ret_g, *, rc):
    b, seq, _ = p3.shape
    hps = RET_HEADS_PER_STEP
    width = hps * RET_DK
    assert seq % rc == 0 and RET_HEADS % hps == 0

    def head_spec(off):
        assert off % width == 0
        return pl.BlockSpec((1, seq, width), lambda bi, h: (bi, 0, off // width + h))

    return pl.pallas_call(
        functools.partial(_ret_kernel, rc=rc),
        out_shape=jax.ShapeDtypeStruct((b, seq, RET_WIDTH), jnp.bfloat16),
        grid=(b, RET_HEADS // hps),
        in_specs=[
            pl.BlockSpec(memory_space=pltpu.SMEM),
            head_spec(RQ_OFF), head_spec(RK_OFF), head_spec(RV_OFF), head_spec(RG_OFF),
            pl.BlockSpec((seq, RET_DK), lambda bi, h: (0, 0)),
            pl.BlockSpec((seq, RET_DK), lambda bi, h: (0, 0)),
            pl.BlockSpec((hps, 1, RET_DV), lambda bi, h: (h, 0, 0)),
        ],
        out_specs=pl.BlockSpec((1, seq, width), lambda bi, h: (bi, 0, h)),
        compiler_params=_params("arbitrary", "arbitrary"),
        name="retention",
    )(log_gamma, p3, p3, p3, p3, cos2, sin2, ret_g)


def _routing(logits):
    lane = lax.broadcasted_iota(jnp.int32, logits.shape, 1).astype(jnp.float32)
    big = float(LANES)
    neg = -jnp.inf

    def first_argmax(v, vmax):
        return jnp.min(jnp.where(v == vmax, lane, big), axis=-1, keepdims=True)

    g_mask = (lane >= N_EXPERTS) & (lane < N_EXPERTS + N_GROUPS)
    gl = jnp.where(g_mask, logits, neg)
    g_max = jnp.max(gl, axis=-1, keepdims=True)
    g_sel = first_argmax(gl, g_max) - N_EXPERTS
    g_gate = 1.0 / jnp.sum(jnp.where(g_mask, jnp.exp(gl - g_max), 0.0), axis=-1, keepdims=True)

    e_lo = g_sel * EXPERTS_PER_GROUP
    el = jnp.where((lane >= e_lo) & (lane < e_lo + EXPERTS_PER_GROUP), logits, neg)
    v1 = jnp.max(el, axis=-1, keepdims=True)
    i1 = first_argmax(el, v1)
    el2 = jnp.where(lane == i1, neg, el)
    v2 = jnp.max(el2, axis=-1, keepdims=True)
    i2 = first_argmax(el2, v2)
    e2 = jnp.exp(v2 - v1)
    denom = 1.0 + e2
    w1 = (1.0 / denom) * g_gate
    w2 = (e2 / denom) * g_gate
    route = jnp.where(lane == ROUTE_E1, i1, 0.0) + jnp.where(lane == ROUTE_E2, i2, 0.0)
    return route + jnp.where(lane == ROUTE_W1, w1, 0.0) + jnp.where(lane == ROUTE_W2, w2, 0.0)


def _norm2(x1, g):
    ms = jnp.mean(x1 * x1, axis=-1, keepdims=True)
    return (x1 * lax.rsqrt(ms + EPS)) * g


OUT_SUBTILES = 2


def _out_proj_kernel(a_ref, r_ref, x_ref, wa_ref, wr_ref, g_ref, rcat_ref,
                     x1_ref, h2_ref, route_ref):
    sub = x_ref.shape[0] // OUT_SUBTILES
    for s in range(OUT_SUBTILES):
        rows = slice(s * sub, (s + 1) * sub)
        mixed = _dot(a_ref[rows, :], wa_ref[...]) + _dot(r_ref[rows, :], wr_ref[...])
        x1 = x_ref[rows, :] + mixed
        x1_ref[rows, :] = x1
        h2 = _norm2(x1, g_ref[...])
        h2_ref[rows, :] = h2
        hi = h2.astype(jnp.bfloat16)
        lo = (h2 - hi.astype(jnp.float32)).astype(jnp.bfloat16)
        both = _dot(hi, rcat_ref[...])
        logits = both[:, :LANES] + (both[:, LANES:] + _dot(lo, rcat_ref[:, :LANES]))
        route_ref[rows, :] = _routing(logits)


def _out_proj(attn2d, ret2d, x2d, w_out_bf, g2, r_cat, *, tm):
    n, d = x2d.shape
    return pl.pallas_call(
        _out_proj_kernel,
        out_shape=(
            jax.ShapeDtypeStruct((n, d), jnp.float32),
            jax.ShapeDtypeStruct((n, d), jnp.float32),
            jax.ShapeDtypeStruct((n, LANES), jnp.float32),
        ),
        grid=(n // tm,),
        in_specs=[
            pl.BlockSpec((tm, ATTN_WIDTH), lambda i: (i, 0)),
            pl.BlockSpec((tm, RET_WIDTH), lambda i: (i, 0)),
            pl.BlockSpec((tm, d), lambda i: (i, 0)),
            pl.BlockSpec((ATTN_WIDTH, d), lambda i: (0, 0)),
            pl.BlockSpec((RET_WIDTH, d), lambda i: (ATTN_WIDTH // RET_WIDTH, 0)),
            pl.BlockSpec((1, d), lambda i: (0, 0)),
            pl.BlockSpec((d, 2 * LANES), lambda i: (0, 0)),
        ],
        out_specs=(
            pl.BlockSpec((tm, d), lambda i: (i, 0)),
            pl.BlockSpec((tm, d), lambda i: (i, 0)),
            pl.BlockSpec((tm, LANES), lambda i: (i, 0)),
        ),
        compiler_params=_params("arbitrary"),
        name="out_proj",
    )(attn2d, ret2d, x2d, w_out_bf, w_out_bf, g2, r_cat)


def _plan_kernel(route_ref, pos_ref, cnt_ref, rank_scr, *, tm, blk):
    n = route_ref.shape[0]
    lane = lax.broadcasted_iota(jnp.int32, (blk, LANES), 1).astype(jnp.float32)
    before = (lax.broadcasted_iota(jnp.int32, (blk, blk), 1)
              < lax.broadcasted_iota(jnp.int32, (blk, blk), 0)).astype(jnp.bfloat16)

    def one_hot(rows):
        r = route_ref[rows, :]
        e1 = r[:, ROUTE_E1:ROUTE_E1 + 1]
        e2 = r[:, ROUTE_E2:ROUTE_E2 + 1]
        return lane == e1, lane == e2

    def rank_body(b, run):
        rows = pl.ds(pl.multiple_of(b * blk, blk), blk)
        m1, m2 = one_hot(rows)
        sel = jnp.where(m1 | m2, 1.0, 0.0)
        rank_scr[rows, :] = _dot(before, sel.astype(jnp.bfloat16)) + run
        return run + jnp.sum(sel, axis=0, keepdims=True)
    cnt = lax.fori_loop(0, n // blk, rank_body, jnp.zeros((1, LANES), jnp.float32), unroll=2)
    cnt_ref[...] = jnp.broadcast_to(cnt, cnt_ref.shape)

    tiles = jnp.floor((cnt + (tm - 1.0)) * (1.0 / tm))
    below = (lax.broadcasted_iota(jnp.int32, (LANES, LANES), 0)
             < lax.broadcasted_iota(jnp.int32, (LANES, LANES), 1)).astype(jnp.bfloat16)
    start = _dot(jnp.broadcast_to(tiles, (8, LANES)).astype(jnp.bfloat16), below)[0:1, :] * float(tm)

    def pos_body(b, carry):
        rows = pl.ds(pl.multiple_of(b * blk, blk), blk)
        m1, m2 = one_hot(rows)
        dest = rank_scr[rows, :] + start
        p1 = jnp.sum(jnp.where(m1, dest, 0.0), axis=-1, keepdims=True)
        p2 = jnp.sum(jnp.where(m2, dest, 0.0), axis=-1, keepdims=True)
        pos_ref[rows, :] = (jnp.where(lane == 0.0, p1, 0.0) + jnp.where(lane == 1.0, p2, 0.0)).astype(jnp.int32)
        return carry
    lax.fori_loop(0, n // blk, pos_body, 0, unroll=2)


def _moe_plan(route, *, tm, blk):
    n = route.shape[0]
    return pl.pallas_call(
        functools.partial(_plan_kernel, tm=tm, blk=blk),
        out_shape=(jax.ShapeDtypeStruct((n, LANES), jnp.int32),
                   jax.ShapeDtypeStruct((8, LANES), jnp.float32)),
        scratch_shapes=[pltpu.VMEM((n, LANES), jnp.float32)],
        compiler_params=pltpu.CompilerParams(vmem_limit_bytes=VMEM_LIMIT),
        name="moe_plan",
    )(route)


def _row_copy(src, src_row, dst, dst_row, sem):
    return pltpu.make_async_copy(src.at[pl.ds(src_row, 1), :], dst.at[pl.ds(dst_row, 1), :], sem)


def _invert_kernel(pos_ref, sid_ref):
    def body(j, carry):
        sid_ref[pos_ref[j]] = j
        return carry
    lax.fori_loop(0, pos_ref.shape[0], body, 0, unroll=8)


def _moe_invert(pos, *, rows):
    return pl.pallas_call(
        _invert_kernel,
        out_shape=jax.ShapeDtypeStruct((rows,), jnp.int32),
        in_specs=[pl.BlockSpec(memory_space=pltpu.SMEM)],
        out_specs=pl.BlockSpec(memory_space=pltpu.SMEM),
        name="moe_invert",
    )(pos)


ROW_GROUP = 8
ROW_BLOCK = 32
FFN_ROW_STEP = 128
NO_NEXT, NOT_FIRST = -1, -2


def _ffn_kernel(te_ref, tv_ref, nu_ref, seg_ref, nexte_ref, sid_ref, tok_ref,
                h2_ref, w1_ref, w3_ref, w2_ref, y2_ref,
                xbuf, ybuf, w1buf, w3buf, w2buf, gsem, ssem, wsem):
    t = pl.program_id(0)
    n_used = nu_ref[0]
    tm = xbuf.shape[1]

    def weight_copies(e, wslot):
        return [pltpu.make_async_copy(src.at[e], dst.at[wslot], wsem.at[wslot])
                for src, dst in ((w1_ref, w1buf), (w3_ref, w3buf), (w2_ref, w2buf))]

    def gather(slot):
        return tok_ref, lambda r, tok: _row_copy(h2_ref, tok, xbuf.at[slot], r, gsem.at[slot])

    def scatter(slot):
        return sid_ref, lambda r, sid: _row_copy(ybuf.at[slot], r, y2_ref, sid, ssem.at[slot])

    def start_all(tile, table, copy):
        valid = tv_ref[tile]
        for blk in range(tm // ROW_BLOCK):
            @pl.when(valid >= (blk + 1) * ROW_BLOCK)
            def _():
                for r in range(blk * ROW_BLOCK, (blk + 1) * ROW_BLOCK):
                    copy(r, table[tile * tm + r]).start()
        done = (valid // ROW_BLOCK) * ROW_BLOCK

        def body(c, carry):
            for u in range(ROW_GROUP):
                r = done + c * ROW_GROUP + u

                @pl.when(r < valid)
                def _():
                    copy(r, table[tile * tm + r]).start()
            return carry
        lax.fori_loop(0, (valid - done + (ROW_GROUP - 1)) // ROW_GROUP, body, 0)

    def wait_all(tile, copy, block_copy):
        valid = tv_ref[tile]
        for blk in range(tm // ROW_BLOCK):
            @pl.when(valid >= (blk + 1) * ROW_BLOCK)
            def _():
                block_copy.wait()

        def body(r, carry):
            copy(0, 0).wait()
            return carry
        lax.fori_loop(0, valid % ROW_BLOCK, body, 0)

    def wait_gather(tile, slot):
        wait_all(tile, gather(slot)[1],
                 pltpu.make_async_copy(h2_ref.at[pl.ds(0, ROW_BLOCK), :],
                                       xbuf.at[slot, pl.ds(0, ROW_BLOCK), :], gsem.at[slot]))

    def wait_scatter(tile, slot):
        wait_all(tile, scatter(slot)[1],
                 pltpu.make_async_copy(ybuf.at[slot, pl.ds(0, ROW_BLOCK), :],
                                       y2_ref.at[pl.ds(0, ROW_BLOCK), :], ssem.at[slot]))

    @pl.when(t == 0)
    def _():
        for cp in weight_copies(te_ref[0], 0):
            cp.start()
        start_all(0, *gather(0))

    @pl.when(t < n_used)
    def _():
        slot = t % 2
        wslot = seg_ref[t] % 2

        @pl.when(t + 1 < n_used)
        def _():
            start_all(t + 1, *gather(1 - slot))

        @pl.when(nexte_ref[t] != NOT_FIRST)
        def _():
            for cp in weight_copies(te_ref[t], wslot):
                cp.wait()

            @pl.when(nexte_ref[t] >= 0)
            def _():
                for cp in weight_copies(nexte_ref[t], 1 - wslot):
                    cp.start()

        wait_gather(t, slot)

        @pl.when(t >= 2)
        def _():
            wait_scatter(t - 2, slot)

        valid = tv_ref[t]
        for live in range(FFN_ROW_STEP, tm + 1, FFN_ROW_STEP):
            @pl.when((valid > live - FFN_ROW_STEP) & (valid <= live))
            def _():
                row = lax.broadcasted_iota(jnp.int32, (live, xbuf.shape[2]), 0)
                x = jnp.where(row < valid, xbuf[slot, :live, :], 0.0).astype(jnp.bfloat16)
                a = _dot(x, w1buf[wslot].astype(jnp.bfloat16))
                b = _dot(x, w3buf[wslot].astype(jnp.bfloat16))
                act = (a * (1.0 / (1.0 + jnp.exp(-a)))) * b
                ybuf[slot, :live, :] = _dot(act.astype(jnp.bfloat16), w2buf[wslot].astype(jnp.bfloat16))
        start_all(t, *scatter(slot))

    @pl.when(t == pl.num_programs(0) - 1)
    def _():
        @pl.when(n_used >= 2)
        def _():
            wait_scatter(n_used - 2, n_used % 2)
        wait_scatter(n_used - 1, (n_used - 1) % 2)


def _moe_ffn(tile_expert, tile_valid, n_used, tile_seg, tile_next, row_sid, row_tok, h2, w1, w3, w2,
             *, tm):
    n, d = h2.shape
    _, _, ff = w1.shape
    max_tiles = tile_expert.shape[0]
    any_spec = pl.BlockSpec(memory_space=pl.ANY)
    return pl.pallas_call(
        _ffn_kernel,
        out_shape=jax.ShapeDtypeStruct((2 * n, d), jnp.float32),
        grid_spec=pltpu.PrefetchScalarGridSpec(
            num_scalar_prefetch=7,
            grid=(max_tiles,),
            in_specs=[any_spec, any_spec, any_spec, any_spec],
            out_specs=any_spec,
            scratch_shapes=[
                pltpu.VMEM((2, tm, d), jnp.float32),
                pltpu.VMEM((2, tm, d), jnp.float32),
                pltpu.VMEM((2, d, ff), jnp.float32),
                pltpu.VMEM((2, d, ff), jnp.float32),
                pltpu.VMEM((2, ff, d), jnp.float32),
                pltpu.SemaphoreType.DMA((2,)),
                pltpu.SemaphoreType.DMA((2,)),
                pltpu.SemaphoreType.DMA((2,)),
            ],
        ),
        compiler_params=_params("arbitrary"),
        name="moe_ffn",
    )(tile_expert, tile_valid, n_used, tile_seg, tile_next, row_sid, row_tok, h2, w1, w3, w2)


def _combine_kernel(x1_ref, route_ref, ya_ref, yb_ref, o_ref):
    route = route_ref[...]
    w1 = route[:, ROUTE_W1:ROUTE_W1 + 1]
    w2 = route[:, ROUTE_W2:ROUTE_W2 + 1]
    o_ref[...] = x1_ref[...] + (w1 * ya_ref[...] + w2 * yb_ref[...])


def _moe_combine(x1, route, y2, *, tm):
    n, d = x1.shape
    return pl.pallas_call(
        _combine_kernel,
        out_shape=jax.ShapeDtypeStruct((n, d), jnp.float32),
        grid=(n // tm,),
        in_specs=[pl.BlockSpec((tm, d), lambda i: (i, 0)),
                  pl.BlockSpec((tm, LANES), lambda i: (i, 0)),
                  pl.BlockSpec((tm, d), lambda i: (i, 0)),
                  pl.BlockSpec((tm, d), lambda i: (i + n // tm, 0))],
        out_specs=pl.BlockSpec((tm, d), lambda i: (i, 0)),
        compiler_params=_params("arbitrary"),
        name="moe_combine",
    )(x1, route, y2, y2)


def _moe(x1, h2, route, w1, w3, w2, *, tm, gather_tm):
    n, d = x1.shape
    max_tiles = (2 * n) // tm + N_EXPERTS
    pos2d, cnt = _moe_plan(route, tm=tm, blk=gather_tm)
    pos = pos2d[:, :2].T.reshape(2 * n)
    counts = cnt[0, :N_EXPERTS].astype(jnp.int32)
    tiles = (counts + (tm - 1)) // tm
    ends = jnp.cumsum(tiles)
    t_idx = jnp.arange(max_tiles, dtype=jnp.int32)
    tile_expert = jnp.sum((ends[None, :] <= t_idx[:, None]).astype(jnp.int32), axis=1)
    tile_expert = jnp.minimum(tile_expert, N_EXPERTS - 1)
    first_tile = (ends - tiles)[tile_expert]
    tile_valid = jnp.clip(counts[tile_expert] - (t_idx - first_tile) * tm, 0, tm).astype(jnp.int32)
    n_used = ends[-1:].astype(jnp.int32)
    used = t_idx < n_used[0]
    is_first = used & (t_idx == first_tile)
    tile_seg = (jnp.cumsum(is_first.astype(jnp.int32)) - 1).astype(jnp.int32)
    next_start = first_tile + tiles[tile_expert]
    next_e = jnp.where(next_start < n_used[0], tile_expert[jnp.minimum(next_start, max_tiles - 1)], NO_NEXT)
    tile_next = jnp.where(is_first, next_e, NOT_FIRST).astype(jnp.int32)

    row_sid = _moe_invert(pos, rows=max_tiles * tm)
    row_tok = jnp.where(row_sid >= n, row_sid - n, row_sid)
    y2 = _moe_ffn(tile_expert, tile_valid, n_used, tile_seg, tile_next, row_sid, row_tok, h2, w1, w3, w2,
                  tm=tm)
    return _moe_combine(x1, route, y2, tm=gather_tm)


def _pack_router(w_group, w_router):
    d = w_group.shape[0]
    experts = jnp.transpose(w_router, (1, 0, 2)).reshape(d, N_EXPERTS)
    wr = jnp.concatenate(
        [experts, w_group, jnp.zeros((d, LANES - N_EXPERTS - N_GROUPS), w_group.dtype)], axis=1)
    hi = wr.astype(jnp.bfloat16)
    lo = (wr - hi.astype(jnp.float32)).astype(jnp.bfloat16)
    return jnp.concatenate([hi, lo], axis=1)


def _rotation_tables(seq):
    half = RET_DK // 2
    pos = jnp.arange(seq, dtype=jnp.float32)
    inv = 1.0 / (ROT_BASE ** jnp.linspace(0.0, 1.0, half, dtype=jnp.float32))
    ang = pos[:, None] * inv[None, :]
    c, s = jnp.cos(ang), jnp.sin(ang)
    return jnp.concatenate([c, c], axis=-1), jnp.concatenate([-s, s], axis=-1)


def _tiles(n, seq):
    def fit(total, want):
        t = min(total, want)
        while total % t:
            t //= 2
        return t
    return dict(
        proj_tm=fit(n, 1024),
        attn_tq=fit(seq, 256), attn_tk=fit(seq, 256),
        ret_rc=fit(seq, 256),
        out_tm=fit(n, 512),
        moe_tm=fit(n, 512), moe_gather_tm=fit(n, 256),
    )


def kernel(x, norm1_g, w_in, q_norm_g, k_norm_g, idx_k_ln_w, idx_k_ln_b, ret_norm_g,
           w_out, norm2_g, w_group, w_router, w1, w3, w2):
    b, seq, d = x.shape
    n = b * seq
    depth = w_in.shape[0]
    t = _tiles(n, seq)
    cos2, sin2 = _rotation_tables(seq)
    log_gamma = jnp.log1p(-jnp.exp2(-5.0 - jnp.arange(RET_HEADS, dtype=jnp.float32)))

    x2d = x.reshape(n, d)
    for l in range(depth):
        proj = _in_proj(x2d, norm1_g[l][None, :], w_in[l].T, tm=t["proj_tm"])
        p3 = proj.reshape(b, seq, proj.shape[1])
        attn = _dsa_attention(p3, q_norm_g[l][None, :], k_norm_g[l][None, :],
                              idx_k_ln_w[l][None, :], idx_k_ln_b[l][None, :],
                              tq=t["attn_tq"], tk=t["attn_tk"])
        ret = _retention(p3, log_gamma, cos2, sin2, ret_norm_g[l].reshape(RET_HEADS, 1, RET_DV),
                         rc=t["ret_rc"])
        r_cat = _pack_router(w_group[l], w_router[l])
        g2 = norm2_g[l][None, :]
        x1, h2, route = _out_proj(attn.reshape(n, ATTN_WIDTH), ret.reshape(n, RET_WIDTH), x2d,
                                  w_out[l].astype(jnp.bfloat16), g2, r_cat, tm=t["out_tm"])
        x2d = _moe(x1, h2, route, w1[l], w3[l], w2[l], tm=t["moe_tm"], gather_tm=t["moe_gather_tm"])
    return x2d.reshape(b, seq, d)
```

```python
import functools
import math

import jax
import jax.numpy as jnp
from jax import lax
from jax.experimental import pallas as pl
from jax.experimental.pallas import tpu as pltpu

CHUNK = 64
ATTN_HEADS = 8
HEAD_DIM = 128
KV_HEADS = 2
HEADS_PER_KV = ATTN_HEADS // KV_HEADS
IDX_HEADS = 16
IDX_DIM = 64
TOPK_MAX = 256
RET_HEADS = 8
RET_DK = 128
RET_DV = 128
ROT_BASE = 10000.0
N_GROUPS = 4
EXPERTS_PER_GROUP = 8
N_EXPERTS = N_GROUPS * EXPERTS_PER_GROUP
EPS = 1e-6

ATTN_WIDTH = ATTN_HEADS * HEAD_DIM
KV_WIDTH = KV_HEADS * HEAD_DIM
IDX_WIDTH = IDX_HEADS * IDX_DIM
RET_WIDTH = RET_HEADS * RET_DK

LANES = 128
SUBLANES = 8
VMEM_LIMIT = 56 * 1024 * 1024

AQ_OFF = 0
AK_OFF = AQ_OFF + ATTN_WIDTH
AV_OFF = AK_OFF + KV_WIDTH
IQ_OFF = AV_OFF + KV_WIDTH
IK_OFF = IQ_OFF + IDX_WIDTH
IW_OFF = IK_OFF + IDX_DIM
W_RET = IW_OFF + IDX_HEADS
IN_WIDTH = W_RET + 4 * RET_WIDTH
assert IW_OFF // LANES == IK_OFF // LANES
PROJ_TN = 1024
RQ_OFF = -(-W_RET // PROJ_TN) * PROJ_TN
RK_OFF = RQ_OFF + RET_WIDTH
RV_OFF = RK_OFF + RET_WIDTH
RG_OFF = RV_OFF + RET_WIDTH
PROJ_WIDTH = RG_OFF + RET_WIDTH

ROUTE_E1, ROUTE_E2, ROUTE_W1, ROUTE_W2 = 0, 1, 2, 3

SUM_ROWS = 16
LOGIT_BOUND_SLACK = 1.05
MAX_SINGLE_SWEEP_BOUND = 50.0

INT_MIN = -(2 ** 31)
NEG_BIG = -1e30

_NT = (((1,), (1,)), ((), ()))


def _dot(a, b):
    return jnp.dot(a, b, preferred_element_type=jnp.float32)


def _dot_nt(a, b):
    return lax.dot_general(a, b, _NT, preferred_element_type=jnp.float32)


def _params(*sem):
    return pltpu.CompilerParams(dimension_semantics=sem, vmem_limit_bytes=VMEM_LIMIT)


def _in_proj_kernel(x_ref, g_ref, wt_ref, o_ref, h_scr, *, row_chunk):
    first = pl.program_id(1) == 0

    @pl.when(first)
    def _():
        w = wt_ref[...].astype(jnp.bfloat16)
        for c in range(x_ref.shape[0] // row_chunk):
            rows = slice(c * row_chunk, (c + 1) * row_chunk)
            x = x_ref[rows, :]
            ms = jnp.mean(x * x, axis=-1, keepdims=True)
            h = ((x * lax.rsqrt(ms + EPS)) * g_ref[...]).astype(jnp.bfloat16)
            h_scr[rows, :] = h
            o_ref[rows, :] = _dot_nt(h, w).astype(o_ref.dtype)

    @pl.when(jnp.logical_not(first))
    def _():
        o_ref[...] = _dot_nt(h_scr[...], wt_ref[...].astype(jnp.bfloat16)).astype(o_ref.dtype)


def _in_proj(x2d, g, w_in_t, *, tm):
    n, d = x2d.shape
    tn = PROJ_TN
    assert w_in_t.shape == (IN_WIDTH, d) and W_RET % SUBLANES == 0 and tn % SUBLANES == 0
    attn_tiles = RQ_OFF // tn

    def window(i, j):
        step = tn // SUBLANES
        start = jnp.where(j < attn_tiles, j * step, W_RET // SUBLANES + (j - attn_tiles) * step)
        return SUBLANES * start, 0

    return pl.pallas_call(
        functools.partial(_in_proj_kernel, row_chunk=min(tm, 256)),
        out_shape=jax.ShapeDtypeStruct((n, PROJ_WIDTH), jnp.bfloat16),
        grid=(n // tm, PROJ_WIDTH // tn),
        in_specs=[
            pl.BlockSpec((tm, d), lambda i, j: (i, 0)),
            pl.BlockSpec((1, d), lambda i, j: (0, 0)),
            pl.BlockSpec((pl.Element(tn), pl.Element(d)), window),
        ],
        out_specs=pl.BlockSpec((tm, tn), lambda i, j: (i, j)),
        scratch_shapes=[pltpu.VMEM((tm, d), jnp.bfloat16)],
        compiler_params=_params("arbitrary", "arbitrary"),
        name="in_proj",
    )(x2d, g, w_in_t)


def _ordered_float(v):
    bits = v ^ ((v >> 31) & jnp.int32(0x7FFFFFFF))
    return pltpu.bitcast(bits, jnp.float32)


def _ordered_bfloat(v16):
    bits16 = v16 ^ ((v16 >> 15) & jnp.int32(0x7FFF))
    return pltpu.bitcast(bits16 << 16, jnp.float32).astype(jnp.bfloat16)


def _attn_kernel(aq_ref, iqa_ref, iqb_ref, iw_ref, ak_ref, av_ref, ik_ref, qg_ref, kg_ref, lnw_ref,
                 lnb_ref, o_ref,
                 kn_scr, ikn_scr, vt_scr, key_scr, kb_scr, wt_scr, qn_scr, acc_scr, s_scr, kmax_scr,
                 *, tk, topk, idx_w_scale):
    i = pl.program_id(1)
    seq = ak_ref.shape[1]
    tq = aq_ref.shape[1]
    chunk_shift = CHUNK.bit_length() - 1

    @pl.when(i == 0)
    def _():
        def body(c, carry):
            rows = pl.ds(pl.multiple_of(c * tk, tk), tk)
            for g in range(KV_HEADS):
                cols = slice(g * HEAD_DIM, (g + 1) * HEAD_DIM)
                k = ak_ref[0, rows, cols].astype(jnp.float32)
                ms = jnp.mean(k * k, axis=-1, keepdims=True)
                kn = (k * lax.rsqrt(ms + EPS)) * kg_ref[...]
                kn_scr[rows, cols] = kn.astype(jnp.bfloat16)
                ksq = jnp.max(jnp.sum(kn * kn, axis=-1, keepdims=True), axis=0, keepdims=True)
                prev = jnp.where(c == 0, 0.0, kmax_scr[g])
                kmax_scr[g] = jnp.maximum(prev, jnp.broadcast_to(ksq, kmax_scr.shape[1:]))
                v = av_ref[0, rows, cols].astype(jnp.float32)
                vt_scr[g, c, :HEAD_DIM, :] = v.T.astype(jnp.bfloat16)
                vt_scr[g, c, HEAD_DIM:, :] = jnp.ones((SUM_ROWS, tk), jnp.bfloat16)
            ki = ik_ref[0, rows, :IDX_DIM].astype(jnp.float32)
            mu = jnp.mean(ki, axis=-1, keepdims=True)
            var = jnp.mean(jnp.square(ki - mu), axis=-1, keepdims=True)
            y = ((ki - mu) * lax.rsqrt(var + EPS) * lnw_ref[...] + lnb_ref[...]).astype(jnp.bfloat16)
            zeros = jnp.zeros_like(y)
            ikn_scr[0, rows, :] = jnp.concatenate([y, zeros], axis=1)
            ikn_scr[1, rows, :] = jnp.concatenate([zeros, y], axis=1)
            return carry
        lax.fori_loop(0, seq // tk, body, 0)

    t0 = i * tq
    n_kt = (t0 + tq) // tk
    scale = (HEAD_DIM ** -0.5) * math.log2(math.e)
    for h in range(ATTN_HEADS):
        g, r = divmod(h, HEADS_PER_KV)
        q = aq_ref[0, :, h * HEAD_DIM:(h + 1) * HEAD_DIM].astype(jnp.float32)
        ms = jnp.mean(q * q, axis=-1, keepdims=True)
        qn_scr[g, r * tq:(r + 1) * tq, :] = (
            (q * lax.rsqrt(ms + EPS)) * qg_ref[...] * scale).astype(jnp.bfloat16)
    ones_rows = jnp.ones((8, HEAD_DIM), jnp.bfloat16)
    bound = []
    for g in range(KV_HEADS):
        qf = qn_scr[g].astype(jnp.float32)
        qsq = _dot_nt(ones_rows, (qf * qf).astype(jnp.bfloat16))[0:1, :]
        kmax = jnp.concatenate([kmax_scr[g, 0:1, :]] * (HEADS_PER_KV * tq // LANES), axis=1)
        bound.append(LOGIT_BOUND_SLACK * jnp.sqrt(qsq * kmax))
    wt_scr[...] = iw_ref[0].astype(jnp.float32).T * idx_w_scale
    w_row = IW_OFF % LANES

    q_chunk = (t0 + lax.broadcasted_iota(jnp.int32, (tk, tq), 1)) >> chunk_shift

    def score_body(kt, carry):
        rows = pl.ds(pl.multiple_of(kt * tk, tk), tk)
        ik_first, ik_second = ikn_scr[0, rows, :], ikn_scr[1, rows, :]
        acc = jnp.zeros((tk, tq), jnp.float32)
        pairs_per_ref = iqa_ref.shape[2] // LANES
        for pair in range(IDX_HEADS // 2):
            src = iqa_ref if pair < pairs_per_ref else iqb_ref
            lane0 = (pair % pairs_per_ref) * LANES
            q_pair = src[0, :, lane0:lane0 + LANES]
            for sub, ik_t in enumerate((ik_first, ik_second)):
                h = 2 * pair + sub
                d = _dot_nt(ik_t, q_pair)
                acc = acc + jnp.maximum(d, 0.0) * wt_scr[w_row + h:w_row + h + 1, :]
        k_chunk = (kt * tk + lax.broadcasted_iota(jnp.int32, (tk, tq), 0)) >> chunk_shift
        score = jnp.where(k_chunk <= q_chunk, acc, -jnp.inf)
        key_scr[rows, :] = score
        kb_scr[rows, :] = score.astype(jnp.bfloat16)
        return carry
    lax.fori_loop(0, n_kt, score_body, 0)

    def tree_sum(hit):
        while hit.shape[0] > 1:
            half = hit.shape[0] // 2
            hit = hit[:half] + hit[half:]
        return hit[0]

    packed = 16

    def coarse_body(it, lo16):
        cand16 = lo16 + lax.shift_left(jnp.int32(1), 15 - it)
        cand_b = _ordered_bfloat(cand16)

        def count_body(kt, part):
            rows = pl.ds(pl.multiple_of(kt * tk, tk), tk)
            hit = jnp.where(kb_scr[rows, :] >= cand_b, jnp.bfloat16(1), jnp.bfloat16(0))
            return part + tree_sum(hit.reshape(tk // packed, packed, tq))
        part = lax.fori_loop(0, n_kt, count_body, jnp.zeros((packed, tq), jnp.bfloat16))
        cnt = jnp.sum(part.astype(jnp.float32), axis=0, keepdims=True)
        return jnp.where(cnt >= float(topk), cand16, lo16)
    lo16 = lax.fori_loop(0, 16, coarse_body, jnp.full((1, tq), -(2 ** 15), jnp.int32))

    fine_bits = 18
    lo0 = jnp.maximum(lo16 - 1, -(2 ** 15)) << 16

    def fine_body(it, lo):
        cand = lo + lax.shift_left(jnp.int32(1), fine_bits - 1 - it)
        cand_f = _ordered_float(cand)

        def count_body(kt, part):
            rows = pl.ds(pl.multiple_of(kt * tk, tk), tk)
            hit = jnp.where(key_scr[rows, :] >= cand_f, 1.0, 0.0)
            return part + tree_sum(hit.reshape(tk // SUBLANES, SUBLANES, tq))
        part = lax.fori_loop(0, n_kt, count_body, jnp.zeros((SUBLANES, tq), jnp.float32))
        cnt = jnp.sum(part, axis=0, keepdims=True)
        return jnp.where(cnt >= float(topk), cand, lo)
    lo = lax.fori_loop(0, fine_bits, fine_body, lo0)
    thr = jnp.where(lo16 == -(2 ** 15), jnp.finfo(jnp.float32).min, _ordered_float(lo))

    acc_scr[...] = jnp.zeros(acc_scr.shape, jnp.float32)

    def masked_logits(kt):
        rows = pl.ds(pl.multiple_of(kt * tk, tk), tk)
        bias = jnp.where(key_scr[rows, :] >= thr, 0.0, NEG_BIG)
        bias = jnp.concatenate([bias] * HEADS_PER_KV, axis=1)
        return [_dot_nt(kn_scr[rows, g * HEAD_DIM:(g + 1) * HEAD_DIM], qn_scr[g]) + bias
                for g in range(KV_HEADS)]

    bound_max = jnp.max(jnp.maximum(bound[0], bound[1]))
    single_sweep = bound_max <= MAX_SINGLE_SWEEP_BOUND

    @pl.when(single_sweep)
    def _():
        def body(kt, carry):
            for g, s in enumerate(masked_logits(kt)):
                acc_scr[g] += _dot(vt_scr[g, kt], jnp.exp2(s - bound[g]).astype(jnp.bfloat16))
            return carry
        lax.fori_loop(0, n_kt, body, 0)

    @pl.when(jnp.logical_not(single_sweep))
    def _():
        def logit_body(kt, m):
            rows = pl.ds(pl.multiple_of(kt * tk, tk), tk)
            new_m = []
            for g, s in enumerate(masked_logits(kt)):
                s_scr[g, rows, :] = s
                new_m.append(jnp.maximum(m[g], jnp.max(s, axis=0, keepdims=True)))
            return tuple(new_m)
        m0 = jnp.full((1, HEADS_PER_KV * tq), NEG_BIG, jnp.float32)
        m = lax.fori_loop(0, n_kt, logit_body, (m0,) * KV_HEADS)

        def pv_body(kt, carry):
            rows = pl.ds(pl.multiple_of(kt * tk, tk), tk)
            for g in range(KV_HEADS):
                p = jnp.exp2(s_scr[g, rows, :] - m[g]).astype(jnp.bfloat16)
                acc_scr[g] += _dot(vt_scr[g, kt], p)
            return carry
        lax.fori_loop(0, n_kt, pv_body, 0)

    for h in range(ATTN_HEADS):
        g, r = divmod(h, HEADS_PER_KV)
        cols = slice(r * tq, (r + 1) * tq)
        o = acc_scr[g, :HEAD_DIM, cols] / acc_scr[g, HEAD_DIM:HEAD_DIM + 1, cols]
        o_ref[0, :, h * HEAD_DIM:(h + 1) * HEAD_DIM] = o.T.astype(o_ref.dtype)


def _dsa_attention(p3, q_g, k_g, ln_w, ln_b, *, tq, tk):
    b, seq, _ = p3.shape
    topk = min(TOPK_MAX, seq // 4)
    idx_w_scale = (IDX_HEADS ** -0.5) * (IDX_DIM ** -0.5)
    assert seq % tq == 0 and tq % tk == 0 and tk % CHUNK == 0

    def col(off, width):
        assert off % width == 0 or width == LANES
        return off // width

    half_iq = IDX_WIDTH // 2
    return pl.pallas_call(
        functools.partial(_attn_kernel, tk=tk, topk=topk, idx_w_scale=idx_w_scale),
        out_shape=jax.ShapeDtypeStruct((b, seq, ATTN_WIDTH), jnp.bfloat16),
        grid=(b, seq // tq),
        in_specs=[
            pl.BlockSpec((1, tq, ATTN_WIDTH), lambda bi, i: (bi, i, col(AQ_OFF, ATTN_WIDTH))),
            pl.BlockSpec((1, tq, half_iq), lambda bi, i: (bi, i, col(IQ_OFF, half_iq))),
            pl.BlockSpec((1, tq, half_iq), lambda bi, i: (bi, i, col(IQ_OFF, half_iq) + 1)),
            pl.BlockSpec((1, tq, LANES), lambda bi, i: (bi, i, col(IW_OFF, LANES))),
            pl.BlockSpec((1, seq, KV_WIDTH), lambda bi, i: (bi, 0, col(AK_OFF, KV_WIDTH))),
            pl.BlockSpec((1, seq, KV_WIDTH), lambda bi, i: (bi, 0, col(AV_OFF, KV_WIDTH))),
            pl.BlockSpec((1, seq, LANES), lambda bi, i: (bi, 0, col(IK_OFF, LANES))),
            pl.BlockSpec((1, HEAD_DIM), lambda bi, i: (0, 0)),
            pl.BlockSpec((1, HEAD_DIM), lambda bi, i: (0, 0)),
            pl.BlockSpec((1, IDX_DIM), lambda bi, i: (0, 0)),
            pl.BlockSpec((1, IDX_DIM), lambda bi, i: (0, 0)),
        ],
        out_specs=pl.BlockSpec((1, tq, ATTN_WIDTH), lambda bi, i: (bi, i, 0)),
        scratch_shapes=[
            pltpu.VMEM((seq, KV_WIDTH), jnp.bfloat16),
            pltpu.VMEM((2, seq, 2 * IDX_DIM), jnp.bfloat16),
            pltpu.VMEM((KV_HEADS, seq // tk, HEAD_DIM + SUM_ROWS, tk), jnp.bfloat16),
            pltpu.VMEM((seq, tq), jnp.float32),
            pltpu.VMEM((seq, tq), jnp.bfloat16),
            pltpu.VMEM((LANES, tq), jnp.float32),
            pltpu.VMEM((KV_HEADS, HEADS_PER_KV * tq, HEAD_DIM), jnp.bfloat16),
            pltpu.VMEM((KV_HEADS, HEAD_DIM + SUM_ROWS, HEADS_PER_KV * tq), jnp.float32),
            pltpu.VMEM((KV_HEADS, seq, HEADS_PER_KV * tq), jnp.float32),
            pltpu.VMEM((KV_HEADS, 8, LANES), jnp.float32),
        ],
        compiler_params=_params("arbitrary", "arbitrary"),
        name="dsa_attn",
    )(p3, p3, p3, p3, p3, p3, p3, q_g, k_g, ln_w, ln_b)


RET_HEADS_PER_STEP = 2


def _ret_kernel(lg_ref, rq_ref, rk_ref, rv_ref, rg_ref, cos_ref, sin_ref, g_ref, o_ref, *, rc):
    seq = rq_ref.shape[1]
    n = lax.broadcasted_iota(jnp.int32, (rc, RET_DV), 0).astype(jnp.float32)
    rel = (lax.broadcasted_iota(jnp.int32, (rc, rc), 0)
           - lax.broadcasted_iota(jnp.int32, (rc, rc), 1)).astype(jnp.float32)

    def rot(x, rows):
        return x * cos_ref[rows, :] + pltpu.roll(x, RET_DK // 2, 1) * sin_ref[rows, :]

    heads = []
    for hh in range(RET_HEADS_PER_STEP):
        lg = lg_ref[pl.program_id(1) * RET_HEADS_PER_STEP + hh]
        heads.append(dict(
            cols=slice(hh * RET_DK, (hh + 1) * RET_DK),
            cross_decay=jnp.exp(lg * (n + 1.0)),
            state_decay=jnp.exp(lg * (rc - 1.0 - n)),
            chunk_decay=jnp.exp(lg * jnp.full((RET_DK, RET_DV), float(rc), jnp.float32)),
            intra=jnp.where(rel >= 0, jnp.exp(lg * jnp.maximum(rel, 0.0)), 0.0),
            state=jnp.zeros((RET_DK, RET_DV), jnp.float32),
            gain=g_ref[hh],
        ))

    for c in range(seq // rc):
        rows = slice(c * rc, (c + 1) * rc)
        for hd in heads:
            cols = hd["cols"]
            q = rot(rq_ref[0, rows, cols].astype(jnp.float32), rows)
            k = rot(rk_ref[0, rows, cols].astype(jnp.float32), rows) * (RET_DK ** -0.5)
            v = rv_ref[0, rows, cols]
            qb = q.astype(jnp.bfloat16)
            inner = _dot_nt(qb, k.astype(jnp.bfloat16)) * hd["intra"]
            o = (_dot(inner.astype(jnp.bfloat16), v)
                 + _dot(qb, hd["state"].astype(jnp.bfloat16)) * hd["cross_decay"])
            kd_t = (k * hd["state_decay"]).T.astype(jnp.bfloat16)
            hd["state"] = hd["state"] * hd["chunk_decay"] + _dot(kd_t, v)
            ms = jnp.mean(o * o, axis=-1, keepdims=True)
            y = (o * lax.rsqrt(ms + EPS)) * hd["gain"]
            gate = rg_ref[0, rows, cols].astype(jnp.float32)
            o_ref[0, rows, cols] = ((gate * (1.0 / (1.0 + jnp.exp(-gate)))) * y).astype(o_ref.dtype)


def _retention(p3, log_gamma, cos2, sin2, ret_g, *, rc):
    b, seq, _ = p3.shape
    hps = RET_HEADS_PER_STEP
    width = hps * RET_DK
    assert seq % rc == 0 and RET_HEADS % hps == 0

    def head_spec(off):
        assert off % width == 0
        return pl.BlockSpec((1, seq, width), lambda bi, h: (bi, 0, off // width + h))

    return pl.pallas_call(
        functools.partial(_ret_kernel, rc=rc),
        out_shape=jax.ShapeDtypeStruct((b, seq, RET_WIDTH), jnp.bfloat16),
        grid=(b, RET_HEADS // hps),
        in_specs=[
            pl.BlockSpec(memory_space=pltpu.SMEM),
            head_spec(RQ_OFF), head_spec(RK_OFF), head_spec(RV_OFF), head_spec(RG_OFF),
            pl.BlockSpec((seq, RET_DK), lambda bi, h: (0, 0)),
            pl.BlockSpec((seq, RET_DK), lambda bi, h: (0, 0)),
            pl.BlockSpec((hps, 1, RET_DV), lambda bi, h: (h, 0, 0)),
        ],
        out_specs=pl.BlockSpec((1, seq, width), lambda bi, h: (bi, 0, h)),
        compiler_params=_params("arbitrary", "arbitrary"),
        name="retention",
    )(log_gamma, p3, p3, p3, p3, cos2, sin2, ret_g)


def _routing(logits):
    lane = lax.broadcasted_iota(jnp.int32, logits.shape, 1).astype(jnp.float32)
    big = float(LANES)
    neg = -jnp.inf

    def first_argmax(v, vmax):
        return jnp.min(jnp.where(v == vmax, lane, big), axis=-1, keepdims=True)

    g_mask = (lane >= N_EXPERTS) & (lane < N_EXPERTS + N_GROUPS)
    gl = jnp.where(g_mask, logits, neg)
    g_max = jnp.max(gl, axis=-1, keepdims=True)
    g_sel = first_argmax(gl, g_max) - N_EXPERTS
    g_gate = 1.0 / jnp.sum(jnp.where(g_mask, jnp.exp(gl - g_max), 0.0), axis=-1, keepdims=True)

    e_lo = g_sel * EXPERTS_PER_GROUP
    el = jnp.where((lane >= e_lo) & (lane < e_lo + EXPERTS_PER_GROUP), logits, neg)
    v1 = jnp.max(el, axis=-1, keepdims=True)
    i1 = first_argmax(el, v1)
    el2 = jnp.where(lane == i1, neg, el)
    v2 = jnp.max(el2, axis=-1, keepdims=True)
    i2 = first_argmax(el2, v2)
    e2 = jnp.exp(v2 - v1)
    denom = 1.0 + e2
    w1 = (1.0 / denom) * g_gate
    w2 = (e2 / denom) * g_gate
    route = jnp.where(lane == ROUTE_E1, i1, 0.0) + jnp.where(lane == ROUTE_E2, i2, 0.0)
    return route + jnp.where(lane == ROUTE_W1, w1, 0.0) + jnp.where(lane == ROUTE_W2, w2, 0.0)


def _norm2(x1, g):
    ms = jnp.mean(x1 * x1, axis=-1, keepdims=True)
    return (x1 * lax.rsqrt(ms + EPS)) * g


OUT_SUBTILES = 2


def _out_proj_kernel(a_ref, r_ref, x_ref, wa_ref, wr_ref, g_ref, rcat_ref,
                     x1_ref, h2_ref, route_ref):
    sub = x_ref.shape[0] // OUT_SUBTILES
    for s in range(OUT_SUBTILES):
        rows = slice(s * sub, (s + 1) * sub)
        mixed = _dot(a_ref[rows, :], wa_ref[...]) + _dot(r_ref[rows, :], wr_ref[...])
        x1 = x_ref[rows, :] + mixed
        x1_ref[rows, :] = x1
        h2 = _norm2(x1, g_ref[...])
        h2_ref[rows, :] = h2
        hi = h2.astype(jnp.bfloat16)
        lo = (h2 - hi.astype(jnp.float32)).astype(jnp.bfloat16)
        both = _dot(hi, rcat_ref[...])
        logits = both[:, :LANES] + (both[:, LANES:] + _dot(lo, rcat_ref[:, :LANES]))
        route_ref[rows, :] = _routing(logits)


def _out_proj(attn2d, ret2d, x2d, w_out_bf, g2, r_cat, *, tm):
    n, d = x2d.shape
    return pl.pallas_call(
        _out_proj_kernel,
        out_shape=(
            jax.ShapeDtypeStruct((n, d), jnp.float32),
            jax.ShapeDtypeStruct((n, d), jnp.float32),
            jax.ShapeDtypeStruct((n, LANES), jnp.float32),
        ),
        grid=(n // tm,),
        in_specs=[
            pl.BlockSpec((tm, ATTN_WIDTH), lambda i: (i, 0)),
            pl.BlockSpec((tm, RET_WIDTH), lambda i: (i, 0)),
            pl.BlockSpec((tm, d), lambda i: (i, 0)),
            pl.BlockSpec((ATTN_WIDTH, d), lambda i: (0, 0)),
            pl.BlockSpec((RET_WIDTH, d), lambda i: (ATTN_WIDTH // RET_WIDTH, 0)),
            pl.BlockSpec((1, d), lambda i: (0, 0)),
            pl.BlockSpec((d, 2 * LANES), lambda i: (0, 0)),
        ],
        out_specs=(
            pl.BlockSpec((tm, d), lambda i: (i, 0)),
            pl.BlockSpec((tm, d), lambda i: (i, 0)),
            pl.BlockSpec((tm, LANES), lambda i: (i, 0)),
        ),
        compiler_params=_params("arbitrary"),
        name="out_proj",
    )(attn2d, ret2d, x2d, w_out_bf, w_out_bf, g2, r_cat)


def _plan_kernel(route_ref, pos_ref, cnt_ref, rank_scr, *, tm, blk):
    n = route_ref.shape[0]
    lane = lax.broadcasted_iota(jnp.int32, (blk, LANES), 1).astype(jnp.float32)
    before = (lax.broadcasted_iota(jnp.int32, (blk, blk), 1)
              < lax.broadcasted_iota(jnp.int32, (blk, blk), 0)).astype(jnp.bfloat16)

    def one_hot(rows):
        r = route_ref[rows, :]
        e1 = r[:, ROUTE_E1:ROUTE_E1 + 1]
        e2 = r[:, ROUTE_E2:ROUTE_E2 + 1]
        return lane == e1, lane == e2

    def rank_body(b, run):
        rows = pl.ds(pl.multiple_of(b * blk, blk), blk)
        m1, m2 = one_hot(rows)
        sel = jnp.where(m1 | m2, 1.0, 0.0)
        rank_scr[rows, :] = _dot(before, sel.astype(jnp.bfloat16)) + run
        return run + jnp.sum(sel, axis=0, keepdims=True)
    cnt = lax.fori_loop(0, n // blk, rank_body, jnp.zeros((1, LANES), jnp.float32), unroll=2)
    cnt_ref[...] = jnp.broadcast_to(cnt, cnt_ref.shape)

    tiles = jnp.floor((cnt + (tm - 1.0)) * (1.0 / tm))
    below = (lax.broadcasted_iota(jnp.int32, (LANES, LANES), 0)
             < lax.broadcasted_iota(jnp.int32, (LANES, LANES), 1)).astype(jnp.bfloat16)
    start = _dot(jnp.broadcast_to(tiles, (8, LANES)).astype(jnp.bfloat16), below)[0:1, :] * float(tm)

    def pos_body(b, carry):
        rows = pl.ds(pl.multiple_of(b * blk, blk), blk)
        m1, m2 = one_hot(rows)
        dest = rank_scr[rows, :] + start
        p1 = jnp.sum(jnp.where(m1, dest, 0.0), axis=-1, keepdims=True)
        p2 = jnp.sum(jnp.where(m2, dest, 0.0), axis=-1, keepdims=True)
        pos_ref[rows, :] = (jnp.where(lane == 0.0, p1, 0.0) + jnp.where(lane == 1.0, p2, 0.0)).astype(jnp.int32)
        return carry
    lax.fori_loop(0, n // blk, pos_body, 0, unroll=2)


def _moe_plan(route, *, tm, blk):
    n = route.shape[0]
    return pl.pallas_call(
        functools.partial(_plan_kernel, tm=tm, blk=blk),
        out_shape=(jax.ShapeDtypeStruct((n, LANES), jnp.int32),
                   jax.ShapeDtypeStruct((8, LANES), jnp.float32)),
        scratch_shapes=[pltpu.VMEM((n, LANES), jnp.float32)],
        compiler_params=pltpu.CompilerParams(vmem_limit_bytes=VMEM_LIMIT),
        name="moe_plan",
    )(route)


def _row_copy(src, src_row, dst, dst_row, sem):
    return pltpu.make_async_copy(src.at[pl.ds(src_row, 1), :], dst.at[pl.ds(dst_row, 1), :], sem)


def _invert_kernel(pos_ref, sid_ref):
    def body(j, carry):
        sid_ref[pos_ref[j]] = j
        return carry
    lax.fori_loop(0, pos_ref.shape[0], body, 0, unroll=8)


def _moe_invert(pos, *, rows):
    return pl.pallas_call(
        _invert_kernel,
        out_shape=jax.ShapeDtypeStruct((rows,), jnp.int32),
        in_specs=[pl.BlockSpec(memory_space=pltpu.SMEM)],
        out_specs=pl.BlockSpec(memory_space=pltpu.SMEM),
        name="moe_invert",
    )(pos)


ROW_GROUP = 8
ROW_BLOCK = 32
FFN_ROW_STEP = 64
NO_NEXT, NOT_FIRST = -1, -2


def _ffn_kernel(te_ref, tv_ref, nu_ref, seg_ref, nexte_ref, sid_ref, tok_ref,
                h2_ref, w1_ref, w3_ref, w2_ref, y2_ref,
                xbuf, ybuf, w1buf, w3buf, w2buf, gsem, ssem, wsem):
    t = pl.program_id(0)
    n_used = nu_ref[0]
    tm = xbuf.shape[1]

    def weight_copies(e, wslot):
        return [pltpu.make_async_copy(src.at[e], dst.at[wslot], wsem.at[wslot])
                for src, dst in ((w1_ref, w1buf), (w3_ref, w3buf), (w2_ref, w2buf))]

    def gather(slot):
        return tok_ref, lambda r, tok: _row_copy(h2_ref, tok, xbuf.at[slot], r, gsem.at[slot])

    def scatter(slot):
        return sid_ref, lambda r, sid: _row_copy(ybuf.at[slot], r, y2_ref, sid, ssem.at[slot])

    def start_all(tile, table, copy):
        valid = tv_ref[tile]
        for blk in range(tm // ROW_BLOCK):
            @pl.when(valid >= (blk + 1) * ROW_BLOCK)
            def _():
                for r in range(blk * ROW_BLOCK, (blk + 1) * ROW_BLOCK):
                    copy(r, table[tile * tm + r]).start()
        done = (valid // ROW_BLOCK) * ROW_BLOCK

        def body(c, carry):
            for u in range(ROW_GROUP):
                r = done + c * ROW_GROUP + u

                @pl.when(r < valid)
                def _():
                    copy(r, table[tile * tm + r]).start()
            return carry
        lax.fori_loop(0, (valid - done + (ROW_GROUP - 1)) // ROW_GROUP, body, 0)

    def wait_all(tile, copy, block_copy):
        valid = tv_ref[tile]
        for blk in range(tm // ROW_BLOCK):
            @pl.when(valid >= (blk + 1) * ROW_BLOCK)
            def _():
                block_copy.wait()

        def body(r, carry):
            copy(0, 0).wait()
            return carry
        lax.fori_loop(0, valid % ROW_BLOCK, body, 0)

    def wait_gather(tile, slot):
        wait_all(tile, gather(slot)[1],
                 pltpu.make_async_copy(h2_ref.at[pl.ds(0, ROW_BLOCK), :],
                                       xbuf.at[slot, pl.ds(0, ROW_BLOCK), :], gsem.at[slot]))

    def wait_scatter(tile, slot):
        wait_all(tile, scatter(slot)[1],
                 pltpu.make_async_copy(ybuf.at[slot, pl.ds(0, ROW_BLOCK), :],
                                       y2_ref.at[pl.ds(0, ROW_BLOCK), :], ssem.at[slot]))

    @pl.when(t == 0)
    def _():
        for cp in weight_copies(te_ref[0], 0):
            cp.start()
        start_all(0, *gather(0))

    @pl.when(t < n_used)
    def _():
        slot = t % 2
        wslot = seg_ref[t] % 2

        @pl.when(t + 1 < n_used)
        def _():
            start_all(t + 1, *gather(1 - slot))

        @pl.when(nexte_ref[t] != NOT_FIRST)
        def _():
            for cp in weight_copies(te_ref[t], wslot):
                cp.wait()

            @pl.when(nexte_ref[t] >= 0)
            def _():
                for cp in weight_copies(nexte_ref[t], 1 - wslot):
                    cp.start()

        wait_gather(t, slot)

        @pl.when(t >= 2)
        def _():
            wait_scatter(t - 2, slot)

        valid = tv_ref[t]
        for live in range(FFN_ROW_STEP, tm + 1, FFN_ROW_STEP):
            @pl.when((valid > live - FFN_ROW_STEP) & (valid <= live))
            def _():
                row = lax.broadcasted_iota(jnp.int32, (live, xbuf.shape[2]), 0)
                x = jnp.where(row < valid, xbuf[slot, :live, :], 0.0).astype(jnp.bfloat16)
                a = _dot(x, w1buf[wslot].astype(jnp.bfloat16))
                b = _dot(x, w3buf[wslot].astype(jnp.bfloat16))
                act = (a * (1.0 / (1.0 + jnp.exp(-a)))) * b
                ybuf[slot, :live, :] = _dot(act.astype(jnp.bfloat16), w2buf[wslot].astype(jnp.bfloat16))
        start_all(t, *scatter(slot))

    @pl.when(t == pl.num_programs(0) - 1)
    def _():
        @pl.when(n_used >= 2)
        def _():
            wait_scatter(n_used - 2, n_used % 2)
        wait_scatter(n_used - 1, (n_used - 1) % 2)


def _moe_ffn(tile_expert, tile_valid, n_used, tile_seg, tile_next, row_sid, row_tok, h2, w1, w3, w2,
             *, tm):
    n, d = h2.shape
    _, _, ff = w1.shape
    max_tiles = tile_expert.shape[0]
    any_spec = pl.BlockSpec(memory_space=pl.ANY)
    return pl.pallas_call(
        _ffn_kernel,
        out_shape=jax.ShapeDtypeStruct((2 * n, d), jnp.float32),
        grid_spec=pltpu.PrefetchScalarGridSpec(
            num_scalar_prefetch=7,
            grid=(max_tiles,),
            in_specs=[any_spec, any_spec, any_spec, any_spec],
            out_specs=any_spec,
            scratch_shapes=[
                pltpu.VMEM((2, tm, d), jnp.float32),
                pltpu.VMEM((2, tm, d), jnp.float32),
                pltpu.VMEM((2, d, ff), jnp.float32),
                pltpu.VMEM((2, d, ff), jnp.float32),
                pltpu.VMEM((2, ff, d), jnp.float32),
                pltpu.SemaphoreType.DMA((2,)),
                pltpu.SemaphoreType.DMA((2,)),
                pltpu.SemaphoreType.DMA((2,)),
            ],
        ),
        compiler_params=_params("arbitrary"),
        name="moe_ffn",
    )(tile_expert, tile_valid, n_used, tile_seg, tile_next, row_sid, row_tok, h2, w1, w3, w2)


def _combine_kernel(x1_ref, route_ref, ya_ref, yb_ref, o_ref):
    route = route_ref[...]
    w1 = route[:, ROUTE_W1:ROUTE_W1 + 1]
    w2 = route[:, ROUTE_W2:ROUTE_W2 + 1]
    o_ref[...] = x1_ref[...] + (w1 * ya_ref[...] + w2 * yb_ref[...])


def _moe_combine(x1, route, y2, *, tm):
    n, d = x1.shape
    return pl.pallas_call(
        _combine_kernel,
        out_shape=jax.ShapeDtypeStruct((n, d), jnp.float32),
        grid=(n // tm,),
        in_specs=[pl.BlockSpec((tm, d), lambda i: (i, 0)),
                  pl.BlockSpec((tm, LANES), lambda i: (i, 0)),
                  pl.BlockSpec((tm, d), lambda i: (i, 0)),
                  pl.BlockSpec((tm, d), lambda i: (i + n // tm, 0))],
        out_specs=pl.BlockSpec((tm, d), lambda i: (i, 0)),
        compiler_params=_params("arbitrary"),
        name="moe_combine",
    )(x1, route, y2, y2)


def _moe(x1, h2, route, w1, w3, w2, *, tm, gather_tm):
    n, d = x1.shape
    max_tiles = (2 * n) // tm + N_EXPERTS
    pos2d, cnt = _moe_plan(route, tm=tm, blk=gather_tm)
    pos = pos2d[:, :2].T.reshape(2 * n)
    counts = cnt[0, :N_EXPERTS].astype(jnp.int32)
    tiles = (counts + (tm - 1)) // tm
    ends = jnp.cumsum(tiles)
    t_idx = jnp.arange(max_tiles, dtype=jnp.int32)
    tile_expert = jnp.sum((ends[None, :] <= t_idx[:, None]).astype(jnp.int32), axis=1)
    tile_expert = jnp.minimum(tile_expert, N_EXPERTS - 1)
    first_tile = (ends - tiles)[tile_expert]
    tile_valid = jnp.clip(counts[tile_expert] - (t_idx - first_tile) * tm, 0, tm).astype(jnp.int32)
    n_used = ends[-1:].astype(jnp.int32)
    used = t_idx < n_used[0]
    is_first = used & (t_idx == first_tile)
    tile_seg = (jnp.cumsum(is_first.astype(jnp.int32)) - 1).astype(jnp.int32)
    next_start = first_tile + tiles[tile_expert]
    next_e = jnp.where(next_start < n_used[0], tile_expert[jnp.minimum(next_start, max_tiles - 1)], NO_NEXT)
    tile_next = jnp.where(is_first, next_e, NOT_FIRST).astype(jnp.int32)

    row_sid = _moe_invert(pos, rows=max_tiles * tm)
    row_tok = jnp.where(row_sid >= n, row_sid - n, row_sid)
    y2 = _moe_ffn(tile_expert, tile_valid, n_used, tile_seg, tile_next, row_sid, row_tok, h2, w1, w3, w2,
                  tm=tm)
    return _moe_combine(x1, route, y2, tm=gather_tm)


def _pack_router(w_group, w_router):
    d = w_group.shape[0]
    experts = jnp.transpose(w_router, (1, 0, 2)).reshape(d, N_EXPERTS)
    wr = jnp.concatenate(
        [experts, w_group, jnp.zeros((d, LANES - N_EXPERTS - N_GROUPS), w_group.dtype)], axis=1)
    hi = wr.astype(jnp.bfloat16)
    lo = (wr - hi.astype(jnp.float32)).astype(jnp.bfloat16)
    return jnp.concatenate([hi, lo], axis=1)


def _rotation_tables(seq):
    half = RET_DK // 2
    pos = jnp.arange(seq, dtype=jnp.float32)
    inv = 1.0 / (ROT_BASE ** jnp.linspace(0.0, 1.0, half, dtype=jnp.float32))
    ang = pos[:, None] * inv[None, :]
    c, s = jnp.cos(ang), jnp.sin(ang)
    return jnp.concatenate([c, c], axis=-1), jnp.concatenate([-s, s], axis=-1)


def _tiles(n, seq):
    def fit(total, want):
        t = min(total, want)
        while total % t:
            t //= 2
        return t
    return dict(
        proj_tm=fit(n, 1024),
        attn_tq=fit(seq, 256), attn_tk=fit(seq, 256),
        ret_rc=fit(seq, 256),
        out_tm=fit(n, 512),
        moe_tm=fit(n, 512), moe_gather_tm=fit(n, 256),
    )


def kernel(x, norm1_g, w_in, q_norm_g, k_norm_g, idx_k_ln_w, idx_k_ln_b, ret_norm_g,
           w_out, norm2_g, w_group, w_router, w1, w3, w2):
    b, seq, d = x.shape
    n = b * seq
    depth = w_in.shape[0]
    t = _tiles(n, seq)
    cos2, sin2 = _rotation_tables(seq)
    log_gamma = jnp.log1p(-jnp.exp2(-5.0 - jnp.arange(RET_HEADS, dtype=jnp.float32)))

    x2d = x.reshape(n, d)
    for l in range(depth):
        proj = _in_proj(x2d, norm1_g[l][None, :], w_in[l].T, tm=t["proj_tm"])
        p3 = proj.reshape(b, seq, proj.shape[1])
        attn = _dsa_attention(p3, q_norm_g[l][None, :], k_norm_g[l][None, :],
                              idx_k_ln_w[l][None, :], idx_k_ln_b[l][None, :],
                              tq=t["attn_tq"], tk=t["attn_tk"])
        ret = _retention(p3, log_gamma, cos2, sin2, ret_norm_g[l].reshape(RET_HEADS, 1, RET_DV),
                         rc=t["ret_rc"])
        r_cat = _pack_router(w_group[l], w_router[l])
        g2 = norm2_g[l][None, :]
        x1, h2, route = _out_proj(attn.reshape(n, ATTN_WIDTH), ret.reshape(n, RET_WIDTH), x2d,
                                  w_out[l].astype(jnp.bfloat16), g2, r_cat, tm=t["out_tm"])
        x2d = _moe(x1, h2, route, w1[l], w3[l], w2[l], tm=t["moe_tm"], gather_tm=t["moe_gather_tm"])
    return x2d.reshape(b, seq, d)
```

```python
import functools
import math

import jax
import jax.numpy as jnp
from jax import lax
from jax.experimental import pallas as pl
from jax.experimental.pallas import tpu as pltpu

CHUNK = 64
ATTN_HEADS = 8
HEAD_DIM = 128
KV_HEADS = 2
HEADS_PER_KV = ATTN_HEADS // KV_HEADS
IDX_HEADS = 16
IDX_DIM = 64
TOPK_MAX = 256
RET_HEADS = 8
RET_DK = 128
RET_DV = 128
ROT_BASE = 10000.0
N_GROUPS = 4
EXPERTS_PER_GROUP = 8
N_EXPERTS = N_GROUPS * EXPERTS_PER_GROUP
EPS = 1e-6

ATTN_WIDTH = ATTN_HEADS * HEAD_DIM
KV_WIDTH = KV_HEADS * HEAD_DIM
IDX_WIDTH = IDX_HEADS * IDX_DIM
RET_WIDTH = RET_HEADS * RET_DK

LANES = 128
SUBLANES = 8
VMEM_LIMIT = 56 * 1024 * 1024

AQ_OFF = 0
AK_OFF = AQ_OFF + ATTN_WIDTH
AV_OFF = AK_OFF + KV_WIDTH
IQ_OFF = AV_OFF + KV_WIDTH
IK_OFF = IQ_OFF + IDX_WIDTH
IW_OFF = IK_OFF + IDX_DIM
W_RET = IW_OFF + IDX_HEADS
IN_WIDTH = W_RET + 4 * RET_WIDTH
assert IW_OFF // LANES == IK_OFF // LANES
PROJ_TN = 1024
RQ_OFF = -(-W_RET // PROJ_TN) * PROJ_TN
RK_OFF = RQ_OFF + RET_WIDTH
RV_OFF = RK_OFF + RET_WIDTH
RG_OFF = RV_OFF + RET_WIDTH
PROJ_WIDTH = RG_OFF + RET_WIDTH

ROUTE_E1, ROUTE_E2, ROUTE_W1, ROUTE_W2 = 0, 1, 2, 3

SUM_ROWS = 16
LOGIT_BOUND_SLACK = 1.05
MAX_SINGLE_SWEEP_BOUND = 50.0

INT_MIN = -(2 ** 31)
NEG_BIG = -1e30

_NT = (((1,), (1,)), ((), ()))


def _dot(a, b):
    return jnp.dot(a, b, preferred_element_type=jnp.float32)


def _dot_nt(a, b):
    return lax.dot_general(a, b, _NT, preferred_element_type=jnp.float32)


def _params(*sem):
    return pltpu.CompilerParams(dimension_semantics=sem, vmem_limit_bytes=VMEM_LIMIT)


def _in_proj_kernel(x_ref, g_ref, wt_ref, o_ref, h_scr, *, row_chunk):
    first = pl.program_id(1) == 0

    @pl.when(first)
    def _():
        w = wt_ref[...].astype(jnp.bfloat16)
        for c in range(x_ref.shape[0] // row_chunk):
            rows = slice(c * row_chunk, (c + 1) * row_chunk)
            x = x_ref[rows, :]
            ms = jnp.mean(x * x, axis=-1, keepdims=True)
            h = ((x * lax.rsqrt(ms + EPS)) * g_ref[...]).astype(jnp.bfloat16)
            h_scr[rows, :] = h
            o_ref[rows, :] = _dot_nt(h, w).astype(o_ref.dtype)

    @pl.when(jnp.logical_not(first))
    def _():
        o_ref[...] = _dot_nt(h_scr[...], wt_ref[...].astype(jnp.bfloat16)).astype(o_ref.dtype)


def _in_proj(x2d, g, w_in_t, *, tm):
    n, d = x2d.shape
    tn = PROJ_TN
    assert w_in_t.shape == (IN_WIDTH, d) and W_RET % SUBLANES == 0 and tn % SUBLANES == 0
    attn_tiles = RQ_OFF // tn

    def window(i, j):
        step = tn // SUBLANES
        start = jnp.where(j < attn_tiles, j * step, W_RET // SUBLANES + (j - attn_tiles) * step)
        return SUBLANES * start, 0

    return pl.pallas_call(
        functools.partial(_in_proj_kernel, row_chunk=min(tm, 256)),
        out_shape=jax.ShapeDtypeStruct((n, PROJ_WIDTH), jnp.bfloat16),
        grid=(n // tm, PROJ_WIDTH // tn),
        in_specs=[
            pl.BlockSpec((tm, d), lambda i, j: (i, 0)),
            pl.BlockSpec((1, d), lambda i, j: (0, 0)),
            pl.BlockSpec((pl.Element(tn), pl.Element(d)), window),
        ],
        out_specs=pl.BlockSpec((tm, tn), lambda i, j: (i, j)),
        scratch_shapes=[pltpu.VMEM((tm, d), jnp.bfloat16)],
        compiler_params=_params("arbitrary", "arbitrary"),
        name="in_proj",
    )(x2d, g, w_in_t)


def _ordered_float(v):
    bits = v ^ ((v >> 31) & jnp.int32(0x7FFFFFFF))
    return pltpu.bitcast(bits, jnp.float32)


def _ordered_bfloat(v16):
    bits16 = v16 ^ ((v16 >> 15) & jnp.int32(0x7FFF))
    return pltpu.bitcast(bits16 << 16, jnp.float32).astype(jnp.bfloat16)


def _attn_kernel(aq_ref, iqa_ref, iqb_ref, iw_ref, ak_ref, av_ref, ik_ref, qg_ref, kg_ref, lnw_ref,
                 lnb_ref, o_ref,
                 kn_scr, ikn_scr, vt_scr, key_scr, kb_scr, wt_scr, qn_scr, acc_scr, s_scr, kmax_scr,
                 *, tk, topk, idx_w_scale):
    i = pl.program_id(1)
    seq = ak_ref.shape[1]
    tq = aq_ref.shape[1]
    chunk_shift = CHUNK.bit_length() - 1

    @pl.when(i == 0)
    def _():
        def body(c, carry):
            rows = pl.ds(pl.multiple_of(c * tk, tk), tk)
            for g in range(KV_HEADS):
                cols = slice(g * HEAD_DIM, (g + 1) * HEAD_DIM)
                k = ak_ref[0, rows, cols].astype(jnp.float32)
                ms = jnp.mean(k * k, axis=-1, keepdims=True)
                kn = (k * lax.rsqrt(ms + EPS)) * kg_ref[...]
                kn_scr[rows, cols] = kn.astype(jnp.bfloat16)
                ksq = jnp.max(jnp.sum(kn * kn, axis=-1, keepdims=True), axis=0, keepdims=True)
                prev = jnp.where(c == 0, 0.0, kmax_scr[g])
                kmax_scr[g] = jnp.maximum(prev, jnp.broadcast_to(ksq, kmax_scr.shape[1:]))
                v = av_ref[0, rows, cols].astype(jnp.float32)
                vt_scr[g, c, :HEAD_DIM, :] = v.T.astype(jnp.bfloat16)
                vt_scr[g, c, HEAD_DIM:, :] = jnp.ones((SUM_ROWS, tk), jnp.bfloat16)
            ki = ik_ref[0, rows, :IDX_DIM].astype(jnp.float32)
            mu = jnp.mean(ki, axis=-1, keepdims=True)
            var = jnp.mean(jnp.square(ki - mu), axis=-1, keepdims=True)
            y = ((ki - mu) * lax.rsqrt(var + EPS) * lnw_ref[...] + lnb_ref[...]).astype(jnp.bfloat16)
            zeros = jnp.zeros_like(y)
            ikn_scr[0, rows, :] = jnp.concatenate([y, zeros], axis=1)
            ikn_scr[1, rows, :] = jnp.concatenate([zeros, y], axis=1)
            return carry
        lax.fori_loop(0, seq // tk, body, 0)

    t0 = i * tq
    n_kt = (t0 + tq) // tk
    scale = (HEAD_DIM ** -0.5) * math.log2(math.e)
    for h in range(ATTN_HEADS):
        g, r = divmod(h, HEADS_PER_KV)
        q = aq_ref[0, :, h * HEAD_DIM:(h + 1) * HEAD_DIM].astype(jnp.float32)
        ms = jnp.mean(q * q, axis=-1, keepdims=True)
        qn_scr[g, r * tq:(r + 1) * tq, :] = (
            (q * lax.rsqrt(ms + EPS)) * qg_ref[...] * scale).astype(jnp.bfloat16)
    ones_rows = jnp.ones((8, HEAD_DIM), jnp.bfloat16)
    bound = []
    for g in range(KV_HEADS):
        qf = qn_scr[g].astype(jnp.float32)
        qsq = _dot_nt(ones_rows, (qf * qf).astype(jnp.bfloat16))[0:1, :]
        kmax = jnp.concatenate([kmax_scr[g, 0:1, :]] * (HEADS_PER_KV * tq // LANES), axis=1)
        bound.append(LOGIT_BOUND_SLACK * jnp.sqrt(qsq * kmax))
    wt_scr[...] = iw_ref[0].astype(jnp.float32).T * idx_w_scale
    w_row = IW_OFF % LANES

    q_chunk = (t0 + lax.broadcasted_iota(jnp.int32, (tk, tq), 1)) >> chunk_shift

    def score_body(kt, carry):
        rows = pl.ds(pl.multiple_of(kt * tk, tk), tk)
        ik_first, ik_second = ikn_scr[0, rows, :], ikn_scr[1, rows, :]
        acc = jnp.zeros((tk, tq), jnp.float32)
        pairs_per_ref = iqa_ref.shape[2] // LANES
        for pair in range(IDX_HEADS // 2):
            src = iqa_ref if pair < pairs_per_ref else iqb_ref
            lane0 = (pair % pairs_per_ref) * LANES
            q_pair = src[0, :, lane0:lane0 + LANES]
            for sub, ik_t in enumerate((ik_first, ik_second)):
                h = 2 * pair + sub
                d = _dot_nt(ik_t, q_pair)
                acc = acc + jnp.maximum(d, 0.0) * wt_scr[w_row + h:w_row + h + 1, :]
        k_chunk = (kt * tk + lax.broadcasted_iota(jnp.int32, (tk, tq), 0)) >> chunk_shift
        score = jnp.where(k_chunk <= q_chunk, acc, -jnp.inf)
        key_scr[rows, :] = score
        kb_scr[rows, :] = score.astype(jnp.bfloat16)
        return carry
    lax.fori_loop(0, n_kt, score_body, 0)

    def tree_sum(hit):
        while hit.shape[0] > 1:
            half = hit.shape[0] // 2
            hit = hit[:half] + hit[half:]
        return hit[0]

    packed = 16

    def coarse_body(it, lo16):
        cand16 = lo16 + lax.shift_left(jnp.int32(1), 15 - it)
        cand_b = _ordered_bfloat(cand16)

        def count_body(kt, part):
            rows = pl.ds(pl.multiple_of(kt * tk, tk), tk)
            hit = jnp.where(kb_scr[rows, :] >= cand_b, jnp.bfloat16(1), jnp.bfloat16(0))
            return part + tree_sum(hit.reshape(tk // packed, packed, tq))
        part = lax.fori_loop(0, n_kt, count_body, jnp.zeros((packed, tq), jnp.bfloat16))
        cnt = jnp.sum(part.astype(jnp.float32), axis=0, keepdims=True)
        return jnp.where(cnt >= float(topk), cand16, lo16)
    lo16 = lax.fori_loop(0, 16, coarse_body, jnp.full((1, tq), -(2 ** 15), jnp.int32))

    fine_bits = 18
    lo0 = jnp.maximum(lo16 - 1, -(2 ** 15)) << 16

    def fine_body(it, lo):
        cand = lo + lax.shift_left(jnp.int32(1), fine_bits - 1 - it)
        cand_f = _ordered_float(cand)

        def count_body(kt, part):
            rows = pl.ds(pl.multiple_of(kt * tk, tk), tk)
            hit = jnp.where(key_scr[rows, :] >= cand_f, 1.0, 0.0)
            return part + tree_sum(hit.reshape(tk // SUBLANES, SUBLANES, tq))
        part = lax.fori_loop(0, n_kt, count_body, jnp.zeros((SUBLANES, tq), jnp.float32))
        cnt = jnp.sum(part, axis=0, keepdims=True)
        return jnp.where(cnt >= float(topk), cand, lo)
    lo = lax.fori_loop(0, fine_bits, fine_body, lo0)
    thr = jnp.where(lo16 == -(2 ** 15), jnp.finfo(jnp.float32).min, _ordered_float(lo))

    acc_scr[...] = jnp.zeros(acc_scr.shape, jnp.float32)

    def masked_logits(kt):
        rows = pl.ds(pl.multiple_of(kt * tk, tk), tk)
        bias = jnp.where(key_scr[rows, :] >= thr, 0.0, NEG_BIG)
        bias = jnp.concatenate([bias] * HEADS_PER_KV, axis=1)
        return [_dot_nt(kn_scr[rows, g * HEAD_DIM:(g + 1) * HEAD_DIM], qn_scr[g]) + bias
                for g in range(KV_HEADS)]

    bound_max = jnp.max(jnp.maximum(bound[0], bound[1]))
    single_sweep = bound_max <= MAX_SINGLE_SWEEP_BOUND

    @pl.when(single_sweep)
    def _():
        def body(kt, carry):
            for g, s in enumerate(masked_logits(kt)):
                acc_scr[g] += _dot(vt_scr[g, kt], jnp.exp2(s - bound[g]).astype(jnp.bfloat16))
            return carry
        lax.fori_loop(0, n_kt, body, 0)

    @pl.when(jnp.logical_not(single_sweep))
    def _():
        def logit_body(kt, m):
            rows = pl.ds(pl.multiple_of(kt * tk, tk), tk)
            new_m = []
            for g, s in enumerate(masked_logits(kt)):
                s_scr[g, rows, :] = s
                new_m.append(jnp.maximum(m[g], jnp.max(s, axis=0, keepdims=True)))
            return tuple(new_m)
        m0 = jnp.full((1, HEADS_PER_KV * tq), NEG_BIG, jnp.float32)
        m = lax.fori_loop(0, n_kt, logit_body, (m0,) * KV_HEADS)

        def pv_body(kt, carry):
            rows = pl.ds(pl.multiple_of(kt * tk, tk), tk)
            for g in range(KV_HEADS):
                p = jnp.exp2(s_scr[g, rows, :] - m[g]).astype(jnp.bfloat16)
                acc_scr[g] += _dot(vt_scr[g, kt], p)
            return carry
        lax.fori_loop(0, n_kt, pv_body, 0)

    for h in range(ATTN_HEADS):
        g, r = divmod(h, HEADS_PER_KV)
        cols = slice(r * tq, (r + 1) * tq)
        o = acc_scr[g, :HEAD_DIM, cols] / acc_scr[g, HEAD_DIM:HEAD_DIM + 1, cols]
        o_ref[0, :, h * HEAD_DIM:(h + 1) * HEAD_DIM] = o.T.astype(o_ref.dtype)


def _dsa_attention(p3, q_g, k_g, ln_w, ln_b, *, tq, tk):
    b, seq, _ = p3.shape
    topk = min(TOPK_MAX, seq // 4)
    idx_w_scale = (IDX_HEADS ** -0.5) * (IDX_DIM ** -0.5)
    assert seq % tq == 0 and tq % tk == 0 and tk % CHUNK == 0

    def col(off, width):
        assert off % width == 0 or width == LANES
        return off // width

    half_iq = IDX_WIDTH // 2
    return pl.pallas_call(
        functools.partial(_attn_kernel, tk=tk, topk=topk, idx_w_scale=idx_w_scale),
        out_shape=jax.ShapeDtypeStruct((b, seq, ATTN_WIDTH), jnp.bfloat16),
        grid=(b, seq // tq),
        in_specs=[
            pl.BlockSpec((1, tq, ATTN_WIDTH), lambda bi, i: (bi, i, col(AQ_OFF, ATTN_WIDTH))),
            pl.BlockSpec((1, tq, half_iq), lambda bi, i: (bi, i, col(IQ_OFF, half_iq))),
            pl.BlockSpec((1, tq, half_iq), lambda bi, i: (bi, i, col(IQ_OFF, half_iq) + 1)),
            pl.BlockSpec((1, tq, LANES), lambda bi, i: (bi, i, col(IW_OFF, LANES))),
            pl.BlockSpec((1, seq, KV_WIDTH), lambda bi, i: (bi, 0, col(AK_OFF, KV_WIDTH))),
            pl.BlockSpec((1, seq, KV_WIDTH), lambda bi, i: (bi, 0, col(AV_OFF, KV_WIDTH))),
            pl.BlockSpec((1, seq, LANES), lambda bi, i: (bi, 0, col(IK_OFF, LANES))),
            pl.BlockSpec((1, HEAD_DIM), lambda bi, i: (0, 0)),
            pl.BlockSpec((1, HEAD_DIM), lambda bi, i: (0, 0)),
            pl.BlockSpec((1, IDX_DIM), lambda bi, i: (0, 0)),
            pl.BlockSpec((1, IDX_DIM), lambda bi, i: (0, 0)),
        ],
        out_specs=pl.BlockSpec((1, tq, ATTN_WIDTH), lambda bi, i: (bi, i, 0)),
        scratch_shapes=[
            pltpu.VMEM((seq, KV_WIDTH), jnp.bfloat16),
            pltpu.VMEM((2, seq, 2 * IDX_DIM), jnp.bfloat16),
            pltpu.VMEM((KV_HEADS, seq // tk, HEAD_DIM + SUM_ROWS, tk), jnp.bfloat16),
            pltpu.VMEM((seq, tq), jnp.float32),
            pltpu.VMEM((seq, tq), jnp.bfloat16),
            pltpu.VMEM((LANES, tq), jnp.float32),
            pltpu.VMEM((KV_HEADS, HEADS_PER_KV * tq, HEAD_DIM), jnp.bfloat16),
            pltpu.VMEM((KV_HEADS, HEAD_DIM + SUM_ROWS, HEADS_PER_KV * tq), jnp.float32),
            pltpu.VMEM((KV_HEADS, seq, HEADS_PER_KV * tq), jnp.float32),
            pltpu.VMEM((KV_HEADS, 8, LANES), jnp.float32),
        ],
        compiler_params=_params("arbitrary", "arbitrary"),
        name="dsa_attn",
    )(p3, p3, p3, p3, p3, p3, p3, q_g, k_g, ln_w, ln_b)


RET_HEADS_PER_STEP = 2


def _ret_kernel(lg_ref, rq_ref, rk_ref, rv_ref, rg_ref, cos_ref, sin_ref, g_ref, o_ref, *, rc):
    seq = rq_ref.shape[1]
    n = lax.broadcasted_iota(jnp.int32, (rc, RET_DV), 0).astype(jnp.float32)
    rel = (lax.broadcasted_iota(jnp.int32, (rc, rc), 0)
           - lax.broadcasted_iota(jnp.int32, (rc, rc), 1)).astype(jnp.float32)

    def rot(x, rows):
        return x * cos_ref[rows, :] + pltpu.roll(x, RET_DK // 2, 1) * sin_ref[rows, :]

    heads = []
    for hh in range(RET_HEADS_PER_STEP):
        lg = lg_ref[pl.program_id(1) * RET_HEADS_PER_STEP + hh]
        heads.append(dict(
            cols=slice(hh * RET_DK, (hh + 1) * RET_DK),
            cross_decay=jnp.exp(lg * (n + 1.0)),
            state_decay=jnp.exp(lg * (rc - 1.0 - n)),
            chunk_decay=jnp.exp(lg * jnp.full((RET_DK, RET_DV), float(rc), jnp.float32)),
            intra=jnp.where(rel >= 0, jnp.exp(lg * jnp.maximum(rel, 0.0)), 0.0),
            state=jnp.zeros((RET_DK, RET_DV), jnp.float32),
            gain=g_ref[hh],
        ))

    for c in range(seq // rc):
        rows = slice(c * rc, (c + 1) * rc)
        for hd in heads:
            cols = hd["cols"]
            q = rot(rq_ref[0, rows, cols].astype(jnp.float32), rows)
            k = rot(rk_ref[0, rows, cols].astype(jnp.float32), rows) * (RET_DK ** -0.5)
            v = rv_ref[0, rows, cols]
            qb = q.astype(jnp.bfloat16)
            inner = _dot_nt(qb, k.astype(jnp.bfloat16)) * hd["intra"]
            o = (_dot(inner.astype(jnp.bfloat16), v)
                 + _dot(qb, hd["state"].astype(jnp.bfloat16)) * hd["cross_decay"])
            kd_t = (k * hd["state_decay"]).T.astype(jnp.bfloat16)
            hd["state"] = hd["state"] * hd["chunk_decay"] + _dot(kd_t, v)
            ms = jnp.mean(o * o, axis=-1, keepdims=True)
            y = (o * lax.rsqrt(ms + EPS)) * hd["gain"]
            gate = rg_ref[0, rows, cols].astype(jnp.float32)
            o_ref[0, rows, cols] = ((gate * (1.0 / (1.0 + jnp.exp(-gate)))) * y).astype(o_ref.dtype)


def _retention(p3, log_gamma, cos2, sin2, ret_g, *, rc):
    b, seq, _ = p3.shape
    hps = RET_HEADS_PER_STEP
    width = hps * RET_DK
    assert seq % rc == 0 and RET_HEADS % hps == 0

    def head_spec(off):
        assert off % width == 0
        return pl.BlockSpec((1, seq, width), lambda bi, h: (bi, 0, off // width + h))

    return pl.pallas_call(
        functools.partial(_ret_kernel, rc=rc),
        out_shape=jax.ShapeDtypeStruct((b, seq, RET_WIDTH), jnp.bfloat16),
        grid=(b, RET_HEADS // hps),
        in_specs=[
            pl.BlockSpec(memory_space=pltpu.SMEM),
            head_spec(RQ_OFF), head_spec(RK_OFF), head_spec(RV_OFF), head_spec(RG_OFF),
            pl.BlockSpec((seq, RET_DK), lambda bi, h: (0, 0)),
            pl.BlockSpec((seq, RET_DK), lambda bi, h: (0, 0)),
            pl.BlockSpec((hps, 1, RET_DV), lambda bi, h: (h, 0, 0)),
        ],
        out_specs=pl.BlockSpec((1, seq, width), lambda bi, h: (bi, 0, h)),
        compiler_params=_params("arbitrary", "arbitrary"),
        name="retention",
    )(log_gamma, p3, p3, p3, p3, cos2, sin2, ret_g)


def _routing(logits):
    lane = lax.broadcasted_iota(jnp.int32, logits.shape, 1).astype(jnp.float32)
    big = float(LANES)
    neg = -jnp.inf

    def first_argmax(v, vmax):
        return jnp.min(jnp.where(v == vmax, lane, big), axis=-1, keepdims=True)

    g_mask = (lane >= N_EXPERTS) & (lane < N_EXPERTS + N_GROUPS)
    gl = jnp.where(g_mask, logits, neg)
    g_max = jnp.max(gl, axis=-1, keepdims=True)
    g_sel = first_argmax(gl, g_max) - N_EXPERTS
    g_gate = 1.0 / jnp.sum(jnp.where(g_mask, jnp.exp(gl - g_max), 0.0), axis=-1, keepdims=True)

    e_lo = g_sel * EXPERTS_PER_GROUP
    el = jnp.where((lane >= e_lo) & (lane < e_lo + EXPERTS_PER_GROUP), logits, neg)
    v1 = jnp.max(el, axis=-1, keepdims=True)
    i1 = first_argmax(el, v1)
    el2 = jnp.where(lane == i1, neg, el)
    v2 = jnp.max(el2, axis=-1, keepdims=True)
    i2 = first_argmax(el2, v2)
    e2 = jnp.exp(v2 - v1)
    denom = 1.0 + e2
    w1 = (1.0 / denom) * g_gate
    w2 = (e2 / denom) * g_gate
    route = jnp.where(lane == ROUTE_E1, i1, 0.0) + jnp.where(lane == ROUTE_E2, i2, 0.0)
    return route + jnp.where(lane == ROUTE_W1, w1, 0.0) + jnp.where(lane == ROUTE_W2, w2, 0.0)


def _norm2(x1, g):
    ms = jnp.mean(x1 * x1, axis=-1, keepdims=True)
    return (x1 * lax.rsqrt(ms + EPS)) * g


OUT_SUBTILES = 2


def _out_proj_kernel(a_ref, r_ref, x_ref, wa_ref, wr_ref, g_ref, rcat_ref,
                     x1_ref, h2_ref, route_ref):
    sub = x_ref.shape[0] // OUT_SUBTILES
    for s in range(OUT_SUBTILES):
        rows = slice(s * sub, (s + 1) * sub)
        mixed = _dot(a_ref[rows, :], wa_ref[...]) + _dot(r_ref[rows, :], wr_ref[...])
        x1 = x_ref[rows, :] + mixed
        x1_ref[rows, :] = x1
        h2 = _norm2(x1, g_ref[...])
        h2_ref[rows, :] = h2
        hi = h2.astype(jnp.bfloat16)
        lo = (h2 - hi.astype(jnp.float32)).astype(jnp.bfloat16)
        both = _dot(hi, rcat_ref[...])
        logits = both[:, :LANES] + (both[:, LANES:] + _dot(lo, rcat_ref[:, :LANES]))
        route_ref[rows, :] = _routing(logits)


def _out_proj(attn2d, ret2d, x2d, w_out_bf, g2, r_cat, *, tm):
    n, d = x2d.shape
    return pl.pallas_call(
        _out_proj_kernel,
        out_shape=(
            jax.ShapeDtypeStruct((n, d), jnp.float32),
            jax.ShapeDtypeStruct((n, d), jnp.float32),
            jax.ShapeDtypeStruct((n, LANES), jnp.float32),
        ),
        grid=(n // tm,),
        in_specs=[
            pl.BlockSpec((tm, ATTN_WIDTH), lambda i: (i, 0)),
            pl.BlockSpec((tm, RET_WIDTH), lambda i: (i, 0)),
            pl.BlockSpec((tm, d), lambda i: (i, 0)),
            pl.BlockSpec((ATTN_WIDTH, d), lambda i: (0, 0)),
            pl.BlockSpec((RET_WIDTH, d), lambda i: (ATTN_WIDTH // RET_WIDTH, 0)),
            pl.BlockSpec((1, d), lambda i: (0, 0)),
            pl.BlockSpec((d, 2 * LANES), lambda i: (0, 0)),
        ],
        out_specs=(
            pl.BlockSpec((tm, d), lambda i: (i, 0)),
            pl.BlockSpec((tm, d), lambda i: (i, 0)),
            pl.BlockSpec((tm, LANES), lambda i: (i, 0)),
        ),
        compiler_params=_params("arbitrary"),
        name="out_proj",
    )(attn2d, ret2d, x2d, w_out_bf, w_out_bf, g2, r_cat)


def _plan_kernel(route_ref, pos_ref, cnt_ref, rank_scr, *, tm, blk):
    n = route_ref.shape[0]
    lane = lax.broadcasted_iota(jnp.int32, (blk, LANES), 1).astype(jnp.float32)
    before = (lax.broadcasted_iota(jnp.int32, (blk, blk), 1)
              < lax.broadcasted_iota(jnp.int32, (blk, blk), 0)).astype(jnp.bfloat16)

    def one_hot(rows):
        r = route_ref[rows, :]
        e1 = r[:, ROUTE_E1:ROUTE_E1 + 1]
        e2 = r[:, ROUTE_E2:ROUTE_E2 + 1]
        return lane == e1, lane == e2

    def rank_body(b, run):
        rows = pl.ds(pl.multiple_of(b * blk, blk), blk)
        m1, m2 = one_hot(rows)
        sel = jnp.where(m1 | m2, 1.0, 0.0)
        rank_scr[rows, :] = _dot(before, sel.astype(jnp.bfloat16)) + run
        return run + jnp.sum(sel, axis=0, keepdims=True)
    cnt = lax.fori_loop(0, n // blk, rank_body, jnp.zeros((1, LANES), jnp.float32), unroll=2)
    cnt_ref[...] = jnp.broadcast_to(cnt, cnt_ref.shape)

    tiles = jnp.floor((cnt + (tm - 1.0)) * (1.0 / tm))
    below = (lax.broadcasted_iota(jnp.int32, (LANES, LANES), 0)
             < lax.broadcasted_iota(jnp.int32, (LANES, LANES), 1)).astype(jnp.bfloat16)
    start = _dot(jnp.broadcast_to(tiles, (8, LANES)).astype(jnp.bfloat16), below)[0:1, :] * float(tm)

    def pos_body(b, carry):
        rows = pl.ds(pl.multiple_of(b * blk, blk), blk)
        m1, m2 = one_hot(rows)
        dest = rank_scr[rows, :] + start
        p1 = jnp.sum(jnp.where(m1, dest, 0.0), axis=-1, keepdims=True)
        p2 = jnp.sum(jnp.where(m2, dest, 0.0), axis=-1, keepdims=True)
        pos_ref[rows, :] = (jnp.where(lane == 0.0, p1, 0.0) + jnp.where(lane == 1.0, p2, 0.0)).astype(jnp.int32)
        return carry
    lax.fori_loop(0, n // blk, pos_body, 0, unroll=2)


def _moe_plan(route, *, tm, blk):
    n = route.shape[0]
    return pl.pallas_call(
        functools.partial(_plan_kernel, tm=tm, blk=blk),
        out_shape=(jax.ShapeDtypeStruct((n, LANES), jnp.int32),
                   jax.ShapeDtypeStruct((8, LANES), jnp.float32)),
        scratch_shapes=[pltpu.VMEM((n, LANES), jnp.float32)],
        compiler_params=pltpu.CompilerParams(vmem_limit_bytes=VMEM_LIMIT),
        name="moe_plan",
    )(route)


def _row_copy(src, src_row, dst, dst_row, sem):
    return pltpu.make_async_copy(src.at[pl.ds(src_row, 1), :], dst.at[pl.ds(dst_row, 1), :], sem)


def _invert_kernel(pos_ref, sid_ref):
    def body(j, carry):
        sid_ref[pos_ref[j]] = j
        return carry
    lax.fori_loop(0, pos_ref.shape[0], body, 0, unroll=8)


def _moe_invert(pos, *, rows):
    return pl.pallas_call(
        _invert_kernel,
        out_shape=jax.ShapeDtypeStruct((rows,), jnp.int32),
        in_specs=[pl.BlockSpec(memory_space=pltpu.SMEM)],
        out_specs=pl.BlockSpec(memory_space=pltpu.SMEM),
        name="moe_invert",
    )(pos)


ROW_GROUP = 8
ROW_BLOCK = 32
FFN_ROW_STEP = 128
NO_NEXT, NOT_FIRST = -1, -2
WEIGHT_DMA_PRIORITY = 1


def _ffn_kernel(te_ref, tv_ref, nu_ref, seg_ref, nexte_ref, sid_ref, tok_ref,
                h2_ref, w1_ref, w3_ref, w2_ref, y2_ref,
                xbuf, ybuf, w1buf, w3buf, w2buf, gsem, ssem, wsem):
    t = pl.program_id(0)
    n_used = nu_ref[0]
    tm = xbuf.shape[1]

    def weight_copies(e, wslot):
        return [pltpu.make_async_copy(src.at[e], dst.at[wslot], wsem.at[wslot])
                for src, dst in ((w1_ref, w1buf), (w3_ref, w3buf), (w2_ref, w2buf))]

    def gather(slot):
        return tok_ref, lambda r, tok: _row_copy(h2_ref, tok, xbuf.at[slot], r, gsem.at[slot])

    def scatter(slot):
        return sid_ref, lambda r, sid: _row_copy(ybuf.at[slot], r, y2_ref, sid, ssem.at[slot])

    def start_all(tile, table, copy):
        valid = tv_ref[tile]
        for blk in range(tm // ROW_BLOCK):
            @pl.when(valid >= (blk + 1) * ROW_BLOCK)
            def _():
                for r in range(blk * ROW_BLOCK, (blk + 1) * ROW_BLOCK):
                    copy(r, table[tile * tm + r]).start()
        done = (valid // ROW_BLOCK) * ROW_BLOCK

        def body(c, carry):
            for u in range(ROW_GROUP):
                r = done + c * ROW_GROUP + u

                @pl.when(r < valid)
                def _():
                    copy(r, table[tile * tm + r]).start()
            return carry
        lax.fori_loop(0, (valid - done + (ROW_GROUP - 1)) // ROW_GROUP, body, 0)

    def wait_all(tile, copy, block_copy):
        valid = tv_ref[tile]
        for blk in range(tm // ROW_BLOCK):
            @pl.when(valid >= (blk + 1) * ROW_BLOCK)
            def _():
                block_copy.wait()

        def body(r, carry):
            copy(0, 0).wait()
            return carry
        lax.fori_loop(0, valid % ROW_BLOCK, body, 0)

    def wait_gather(tile, slot):
        wait_all(tile, gather(slot)[1],
                 pltpu.make_async_copy(h2_ref.at[pl.ds(0, ROW_BLOCK), :],
                                       xbuf.at[slot, pl.ds(0, ROW_BLOCK), :], gsem.at[slot]))

    def wait_scatter(tile, slot):
        wait_all(tile, scatter(slot)[1],
                 pltpu.make_async_copy(ybuf.at[slot, pl.ds(0, ROW_BLOCK), :],
                                       y2_ref.at[pl.ds(0, ROW_BLOCK), :], ssem.at[slot]))

    @pl.when(t == 0)
    def _():
        for cp in weight_copies(te_ref[0], 0):
            cp.start(priority=WEIGHT_DMA_PRIORITY)
        start_all(0, *gather(0))

    @pl.when(t < n_used)
    def _():
        slot = t % 2
        wslot = seg_ref[t] % 2

        @pl.when(t + 1 < n_used)
        def _():
            start_all(t + 1, *gather(1 - slot))

        @pl.when(nexte_ref[t] != NOT_FIRST)
        def _():
            for cp in weight_copies(te_ref[t], wslot):
                cp.wait()

            @pl.when(nexte_ref[t] >= 0)
            def _():
                for cp in weight_copies(nexte_ref[t], 1 - wslot):
                    cp.start(priority=WEIGHT_DMA_PRIORITY)

        wait_gather(t, slot)

        @pl.when(t >= 2)
        def _():
            wait_scatter(t - 2, slot)

        valid = tv_ref[t]
        for live in range(FFN_ROW_STEP, tm + 1, FFN_ROW_STEP):
            @pl.when((valid > live - FFN_ROW_STEP) & (valid <= live))
            def _():
                row = lax.broadcasted_iota(jnp.int32, (live, xbuf.shape[2]), 0)
                x = jnp.where(row < valid, xbuf[slot, :live, :], 0.0).astype(jnp.bfloat16)
                a = _dot(x, w1buf[wslot].astype(jnp.bfloat16))
                b = _dot(x, w3buf[wslot].astype(jnp.bfloat16))
                act = (a * (1.0 / (1.0 + jnp.exp(-a)))) * b
                ybuf[slot, :live, :] = _dot(act.astype(jnp.bfloat16), w2buf[wslot].astype(jnp.bfloat16))
        start_all(t, *scatter(slot))

    @pl.when(t == pl.num_programs(0) - 1)
    def _():
        @pl.when(n_used >= 2)
        def _():
            wait_scatter(n_used - 2, n_used % 2)
        wait_scatter(n_used - 1, (n_used - 1) % 2)


def _moe_ffn(tile_expert, tile_valid, n_used, tile_seg, tile_next, row_sid, row_tok, h2, w1, w3, w2,
             *, tm):
    n, d = h2.shape
    _, _, ff = w1.shape
    max_tiles = tile_expert.shape[0]
    any_spec = pl.BlockSpec(memory_space=pl.ANY)
    return pl.pallas_call(
        _ffn_kernel,
        out_shape=jax.ShapeDtypeStruct((2 * n, d), jnp.float32),
        grid_spec=pltpu.PrefetchScalarGridSpec(
            num_scalar_prefetch=7,
            grid=(max_tiles,),
            in_specs=[any_spec, any_spec, any_spec, any_spec],
            out_specs=any_spec,
            scratch_shapes=[
                pltpu.VMEM((2, tm, d), jnp.float32),
                pltpu.VMEM((2, tm, d), jnp.float32),
                pltpu.VMEM((2, d, ff), jnp.float32),
                pltpu.VMEM((2, d, ff), jnp.float32),
                pltpu.VMEM((2, ff, d), jnp.float32),
                pltpu.SemaphoreType.DMA((2,)),
                pltpu.SemaphoreType.DMA((2,)),
                pltpu.SemaphoreType.DMA((2,)),
            ],
        ),
        compiler_params=_params("arbitrary"),
        name="moe_ffn",
    )(tile_expert, tile_valid, n_used, tile_seg, tile_next, row_sid, row_tok, h2, w1, w3, w2)


def _combine_kernel(x1_ref, route_ref, ya_ref, yb_ref, o_ref):
    route = route_ref[...]
    w1 = route[:, ROUTE_W1:ROUTE_W1 + 1]
    w2 = route[:, ROUTE_W2:ROUTE_W2 + 1]
    o_ref[...] = x1_ref[...] + (w1 * ya_ref[...] + w2 * yb_ref[...])


def _moe_combine(x1, route, y2, *, tm):
    n, d = x1.shape
    return pl.pallas_call(
        _combine_kernel,
        out_shape=jax.ShapeDtypeStruct((n, d), jnp.float32),
        grid=(n // tm,),
        in_specs=[pl.BlockSpec((tm, d), lambda i: (i, 0)),
                  pl.BlockSpec((tm, LANES), lambda i: (i, 0)),
                  pl.BlockSpec((tm, d), lambda i: (i, 0)),
                  pl.BlockSpec((tm, d), lambda i: (i + n // tm, 0))],
        out_specs=pl.BlockSpec((tm, d), lambda i: (i, 0)),
        compiler_params=_params("arbitrary"),
        name="moe_combine",
    )(x1, route, y2, y2)


def _moe(x1, h2, route, w1, w3, w2, *, tm, gather_tm):
    n, d = x1.shape
    max_tiles = (2 * n) // tm + N_EXPERTS
    pos2d, cnt = _moe_plan(route, tm=tm, blk=gather_tm)
    pos = pos2d[:, :2].T.reshape(2 * n)
    counts = cnt[0, :N_EXPERTS].astype(jnp.int32)
    tiles = (counts + (tm - 1)) // tm
    ends = jnp.cumsum(tiles)
    t_idx = jnp.arange(max_tiles, dtype=jnp.int32)
    tile_expert = jnp.sum((ends[None, :] <= t_idx[:, None]).astype(jnp.int32), axis=1)
    tile_expert = jnp.minimum(tile_expert, N_EXPERTS - 1)
    first_tile = (ends - tiles)[tile_expert]
    tile_valid = jnp.clip(counts[tile_expert] - (t_idx - first_tile) * tm, 0, tm).astype(jnp.int32)
    n_used = ends[-1:].astype(jnp.int32)
    used = t_idx < n_used[0]
    is_first = used & (t_idx == first_tile)
    tile_seg = (jnp.cumsum(is_first.astype(jnp.int32)) - 1).astype(jnp.int32)
    next_start = first_tile + tiles[tile_expert]
    next_e = jnp.where(next_start < n_used[0], tile_expert[jnp.minimum(next_start, max_tiles - 1)], NO_NEXT)
    tile_next = jnp.where(is_first, next_e, NOT_FIRST).astype(jnp.int32)

    row_sid = _moe_invert(pos, rows=max_tiles * tm)
    row_tok = jnp.where(row_sid >= n, row_sid - n, row_sid)
    y2 = _moe_ffn(tile_expert, tile_valid, n_used, tile_seg, tile_next, row_sid, row_tok, h2, w1, w3, w2,
                  tm=tm)
    return _moe_combine(x1, route, y2, tm=gather_tm)


def _pack_router(w_group, w_router):
    d = w_group.shape[0]
    experts = jnp.transpose(w_router, (1, 0, 2)).reshape(d, N_EXPERTS)
    wr = jnp.concatenate(
        [experts, w_group, jnp.zeros((d, LANES - N_EXPERTS - N_GROUPS), w_group.dtype)], axis=1)
    hi = wr.astype(jnp.bfloat16)
    lo = (wr - hi.astype(jnp.float32)).astype(jnp.bfloat16)
    return jnp.concatenate([hi, lo], axis=1)


def _rotation_tables(seq):
    half = RET_DK // 2
    pos = jnp.arange(seq, dtype=jnp.float32)
    inv = 1.0 / (ROT_BASE ** jnp.linspace(0.0, 1.0, half, dtype=jnp.float32))
    ang = pos[:, None] * inv[None, :]
    c, s = jnp.cos(ang), jnp.sin(ang)
    return jnp.concatenate([c, c], axis=-1), jnp.concatenate([-s, s], axis=-1)


def _tiles(n, seq):
    def fit(total, want):
        t = min(total, want)
        while total % t:
            t //= 2
        return t
    return dict(
        proj_tm=fit(n, 1024),
        attn_tq=fit(seq, 256), attn_tk=fit(seq, 256),
        ret_rc=fit(seq, 256),
        out_tm=fit(n, 512),
        moe_tm=fit(n, 512), moe_gather_tm=fit(n, 256),
    )


def kernel(x, norm1_g, w_in, q_norm_g, k_norm_g, idx_k_ln_w, idx_k_ln_b, ret_norm_g,
           w_out, norm2_g, w_group, w_router, w1, w3, w2):
    b, seq, d = x.shape
    n = b * seq
    depth = w_in.shape[0]
    t = _tiles(n, seq)
    cos2, sin2 = _rotation_tables(seq)
    log_gamma = jnp.log1p(-jnp.exp2(-5.0 - jnp.arange(RET_HEADS, dtype=jnp.float32)))

    x2d = x.reshape(n, d)
    for l in range(depth):
        proj = _in_proj(x2d, norm1_g[l][None, :], w_in[l].T, tm=t["proj_tm"])
        p3 = proj.reshape(b, seq, proj.shape[1])
        attn = _dsa_attention(p3, q_norm_g[l][None, :], k_norm_g[l][None, :],
                              idx_k_ln_w[l][None, :], idx_k_ln_b[l][None, :],
                              tq=t["attn_tq"], tk=t["attn_tk"])
        ret = _retention(p3, log_gamma, cos2, sin2, ret_norm_g[l].reshape(RET_HEADS, 1, RET_DV),
                         rc=t["ret_rc"])
        r_cat = _pack_router(w_group[l], w_router[l])
        g2 = norm2_g[l][None, :]
        x1, h2, route = _out_proj(attn.reshape(n, ATTN_WIDTH), ret.reshape(n, RET_WIDTH), x2d,
                                  w_out[l].astype(jnp.bfloat16), g2, r_cat, tm=t["out_tm"])
        x2d = _moe(x1, h2, route, w1[l], w3[l], w2[l], tm=t["moe_tm"], gather_tm=t["moe_gather_tm"])
    return x2d.reshape(b, seq, d)
```

```python
import functools
import math

import jax
import jax.numpy as jnp
from jax import lax
from jax.experimental import pallas as pl
from jax.experimental.pallas import tpu as pltpu

CHUNK = 64
ATTN_HEADS = 8
HEAD_DIM = 128
KV_HEADS = 2
HEADS_PER_KV = ATTN_HEADS // KV_HEADS
IDX_HEADS = 16
IDX_DIM = 64
TOPK_MAX = 256
RET_HEADS = 8
RET_DK = 128
RET_DV = 128
ROT_BASE = 10000.0
N_GROUPS = 4
EXPERTS_PER_GROUP = 8
N_EXPERTS = N_GROUPS * EXPERTS_PER_GROUP
EPS = 1e-6

ATTN_WIDTH = ATTN_HEADS * HEAD_DIM
KV_WIDTH = KV_HEADS * HEAD_DIM
IDX_WIDTH = IDX_HEADS * IDX_DIM
RET_WIDTH = RET_HEADS * RET_DK

LANES = 128
SUBLANES = 8
VMEM_LIMIT = 56 * 1024 * 1024

AQ_OFF = 0
AK_OFF = AQ_OFF + ATTN_WIDTH
AV_OFF = AK_OFF + KV_WIDTH
IQ_OFF = AV_OFF + KV_WIDTH
IK_OFF = IQ_OFF + IDX_WIDTH
IW_OFF = IK_OFF + IDX_DIM
W_RET = IW_OFF + IDX_HEADS
IN_WIDTH = W_RET + 4 * RET_WIDTH
assert IW_OFF // LANES == IK_OFF // LANES
PROJ_TN = 1024
RQ_OFF = -(-W_RET // PROJ_TN) * PROJ_TN
RK_OFF = RQ_OFF + RET_WIDTH
RV_OFF = RK_OFF + RET_WIDTH
RG_OFF = RV_OFF + RET_WIDTH
PROJ_WIDTH = RG_OFF + RET_WIDTH

ROUTE_E1, ROUTE_E2, ROUTE_W1, ROUTE_W2 = 0, 1, 2, 3

SUM_ROWS = 16
LOGIT_BOUND_SLACK = 1.05
MAX_SINGLE_SWEEP_BOUND = 50.0

NEG_BIG = -1e30

_NT = (((1,), (1,)), ((), ()))


def _dot(a, b):
    return jnp.dot(a, b, preferred_element_type=jnp.float32)


def _dot_nt(a, b):
    return lax.dot_general(a, b, _NT, preferred_element_type=jnp.float32)


def _params(*sem):
    return pltpu.CompilerParams(dimension_semantics=sem, vmem_limit_bytes=VMEM_LIMIT)


def _in_proj_kernel(x_ref, g_ref, wt_ref, o_ref, h_scr, *, row_chunk):
    first = pl.program_id(1) == 0

    @pl.when(first)
    def _():
        w = wt_ref[...].astype(jnp.bfloat16)
        for c in range(x_ref.shape[0] // row_chunk):
            rows = slice(c * row_chunk, (c + 1) * row_chunk)
            x = x_ref[rows, :]
            ms = jnp.mean(x * x, axis=-1, keepdims=True)
            h = ((x * lax.rsqrt(ms + EPS)) * g_ref[...]).astype(jnp.bfloat16)
            h_scr[rows, :] = h
            o_ref[rows, :] = _dot_nt(h, w).astype(o_ref.dtype)

    @pl.when(jnp.logical_not(first))
    def _():
        o_ref[...] = _dot_nt(h_scr[...], wt_ref[...].astype(jnp.bfloat16)).astype(o_ref.dtype)


def _in_proj(x2d, g, w_in_t, *, tm):
    n, d = x2d.shape
    tn = PROJ_TN
    assert w_in_t.shape == (IN_WIDTH, d) and W_RET % SUBLANES == 0 and tn % SUBLANES == 0
    attn_tiles = RQ_OFF // tn

    def window(i, j):
        step = tn // SUBLANES
        start = jnp.where(j < attn_tiles, j * step, W_RET // SUBLANES + (j - attn_tiles) * step)
        return SUBLANES * start, 0

    return pl.pallas_call(
        functools.partial(_in_proj_kernel, row_chunk=min(tm, 256)),
        out_shape=jax.ShapeDtypeStruct((n, PROJ_WIDTH), jnp.bfloat16),
        grid=(n // tm, PROJ_WIDTH // tn),
        in_specs=[
            pl.BlockSpec((tm, d), lambda i, j: (i, 0)),
            pl.BlockSpec((1, d), lambda i, j: (0, 0)),
            pl.BlockSpec((pl.Element(tn), pl.Element(d)), window),
        ],
        out_specs=pl.BlockSpec((tm, tn), lambda i, j: (i, j)),
        scratch_shapes=[pltpu.VMEM((tm, d), jnp.bfloat16)],
        compiler_params=_params("arbitrary", "arbitrary"),
        name="in_proj",
    )(x2d, g, w_in_t)


def _ordered_float(v):
    bits = v ^ ((v >> 31) & jnp.int32(0x7FFFFFFF))
    return pltpu.bitcast(bits, jnp.float32)


def _ordered_bfloat(v16):
    bits16 = v16 ^ ((v16 >> 15) & jnp.int32(0x7FFF))
    return pltpu.bitcast(bits16 << 16, jnp.float32).astype(jnp.bfloat16)


def _attn_kernel(aq_ref, iqa_ref, iqb_ref, iw_ref, ak_ref, av_ref, ik_ref, qg_ref, kg_ref, lnw_ref,
                 lnb_ref, o_ref,
                 kn_scr, ikn_scr, vt_scr, key_scr, kb_scr, wt_scr, qn_scr, acc_scr, s_scr, kmax_scr,
                 *, tk, topk, idx_w_scale):
    i = pl.program_id(1)
    seq = ak_ref.shape[1]
    tq = aq_ref.shape[1]
    chunk_shift = CHUNK.bit_length() - 1

    @pl.when(i == 0)
    def _():
        def body(c, carry):
            rows = pl.ds(pl.multiple_of(c * tk, tk), tk)
            for g in range(KV_HEADS):
                cols = slice(g * HEAD_DIM, (g + 1) * HEAD_DIM)
                k = ak_ref[0, rows, cols].astype(jnp.float32)
                ms = jnp.mean(k * k, axis=-1, keepdims=True)
                kn = (k * lax.rsqrt(ms + EPS)) * kg_ref[...]
                kn_scr[rows, cols] = kn.astype(jnp.bfloat16)
                ksq = jnp.max(jnp.sum(kn * kn, axis=-1, keepdims=True), axis=0, keepdims=True)
                prev = jnp.where(c == 0, 0.0, kmax_scr[g])
                kmax_scr[g] = jnp.maximum(prev, jnp.broadcast_to(ksq, kmax_scr.shape[1:]))
                v = av_ref[0, rows, cols].astype(jnp.float32)
                vt_scr[g, c, :HEAD_DIM, :] = v.T.astype(jnp.bfloat16)
                vt_scr[g, c, HEAD_DIM:, :] = jnp.ones((SUM_ROWS, tk), jnp.bfloat16)
            ki = ik_ref[0, rows, :IDX_DIM].astype(jnp.float32)
            mu = jnp.mean(ki, axis=-1, keepdims=True)
            var = jnp.mean(jnp.square(ki - mu), axis=-1, keepdims=True)
            y = ((ki - mu) * lax.rsqrt(var + EPS) * lnw_ref[...] + lnb_ref[...]).astype(jnp.bfloat16)
            zeros = jnp.zeros_like(y)
            ikn_scr[0, rows, :] = jnp.concatenate([y, zeros], axis=1)
            ikn_scr[1, rows, :] = jnp.concatenate([zeros, y], axis=1)
            return carry
        lax.fori_loop(0, seq // tk, body, 0)

    t0 = i * tq
    n_kt = (t0 + tq) // tk
    scale = (HEAD_DIM ** -0.5) * math.log2(math.e)
    for h in range(ATTN_HEADS):
        g, r = divmod(h, HEADS_PER_KV)
        q = aq_ref[0, :, h * HEAD_DIM:(h + 1) * HEAD_DIM].astype(jnp.float32)
        ms = jnp.mean(q * q, axis=-1, keepdims=True)
        qn_scr[g, r * tq:(r + 1) * tq, :] = (
            (q * lax.rsqrt(ms + EPS)) * qg_ref[...] * scale).astype(jnp.bfloat16)
    ones_rows = jnp.ones((8, HEAD_DIM), jnp.bfloat16)
    bound = []
    for g in range(KV_HEADS):
        qf = qn_scr[g].astype(jnp.float32)
        qsq = _dot_nt(ones_rows, (qf * qf).astype(jnp.bfloat16))[0:1, :]
        kmax = jnp.concatenate([kmax_scr[g, 0:1, :]] * (HEADS_PER_KV * tq // LANES), axis=1)
        bound.append(LOGIT_BOUND_SLACK * jnp.sqrt(qsq * kmax))
    wt_scr[...] = iw_ref[0].astype(jnp.float32).T * idx_w_scale
    w_row = IW_OFF % LANES

    q_chunk = (t0 + lax.broadcasted_iota(jnp.int32, (tk, tq), 1)) >> chunk_shift

    def score_body(kt, carry):
        rows = pl.ds(pl.multiple_of(kt * tk, tk), tk)
        ik_first, ik_second = ikn_scr[0, rows, :], ikn_scr[1, rows, :]
        acc = jnp.zeros((tk, tq), jnp.float32)
        pairs_per_ref = iqa_ref.shape[2] // LANES
        for pair in range(IDX_HEADS // 2):
            src = iqa_ref if pair < pairs_per_ref else iqb_ref
            lane0 = (pair % pairs_per_ref) * LANES
            q_pair = src[0, :, lane0:lane0 + LANES]
            for sub, ik_t in enumerate((ik_first, ik_second)):
                h = 2 * pair + sub
                d = _dot_nt(ik_t, q_pair)
                acc = acc + jnp.maximum(d, 0.0) * wt_scr[w_row + h:w_row + h + 1, :]
        k_chunk = (kt * tk + lax.broadcasted_iota(jnp.int32, (tk, tq), 0)) >> chunk_shift
        score = jnp.where(k_chunk <= q_chunk, acc, -jnp.inf)
        key_scr[rows, :] = score
        kb_scr[rows, :] = score.astype(jnp.bfloat16)
        return carry
    lax.fori_loop(0, n_kt, score_body, 0)

    def tree_sum(hit):
        while hit.shape[0] > 1:
            half = hit.shape[0] // 2
            hit = hit[:half] + hit[half:]
        return hit[0]

    packed = 16
    searching = (t0 + tq > topk).astype(jnp.int32)

    def coarse_body(it, lo16):
        cand16 = lo16 + lax.shift_left(jnp.int32(1), 15 - it)
        cand_b = _ordered_bfloat(cand16)

        def count_body(kt, part):
            rows = pl.ds(pl.multiple_of(kt * tk, tk), tk)
            hit = jnp.where(kb_scr[rows, :] >= cand_b, jnp.bfloat16(1), jnp.bfloat16(0))
            return part + tree_sum(hit.reshape(tk // packed, packed, tq))
        part = lax.fori_loop(0, n_kt, count_body, jnp.zeros((packed, tq), jnp.bfloat16))
        cnt = jnp.sum(part.astype(jnp.float32), axis=0, keepdims=True)
        return jnp.where(cnt >= float(topk), cand16, lo16)
    lo16 = lax.fori_loop(0, 16 * searching, coarse_body, jnp.full((1, tq), -(2 ** 15), jnp.int32))

    fine_bits = 18
    lo0 = jnp.maximum(lo16 - 1, -(2 ** 15)) << 16

    def fine_body(it, lo):
        cand = lo + lax.shift_left(jnp.int32(1), fine_bits - 1 - it)
        cand_f = _ordered_float(cand)

        def count_body(kt, part):
            rows = pl.ds(pl.multiple_of(kt * tk, tk), tk)
            hit = jnp.where(key_scr[rows, :] >= cand_f, 1.0, 0.0)
            return part + tree_sum(hit.reshape(tk // SUBLANES, SUBLANES, tq))
        part = lax.fori_loop(0, n_kt, count_body, jnp.zeros((SUBLANES, tq), jnp.float32))
        cnt = jnp.sum(part, axis=0, keepdims=True)
        return jnp.where(cnt >= float(topk), cand, lo)
    lo = lax.fori_loop(0, fine_bits * searching, fine_body, lo0)
    thr = jnp.where(lo16 == -(2 ** 15), jnp.finfo(jnp.float32).min, _ordered_float(lo))

    acc_scr[...] = jnp.zeros(acc_scr.shape, jnp.float32)

    def masked_logits(kt):
        rows = pl.ds(pl.multiple_of(kt * tk, tk), tk)
        bias = jnp.where(key_scr[rows, :] >= thr, 0.0, NEG_BIG)
        bias = jnp.concatenate([bias] * HEADS_PER_KV, axis=1)
        return [_dot_nt(kn_scr[rows, g * HEAD_DIM:(g + 1) * HEAD_DIM], qn_scr[g]) + bias
                for g in range(KV_HEADS)]

    bound_max = jnp.max(jnp.maximum(bound[0], bound[1]))
    single_sweep = bound_max <= MAX_SINGLE_SWEEP_BOUND

    @pl.when(single_sweep)
    def _():
        def body(kt, carry):
            for g, s in enumerate(masked_logits(kt)):
                acc_scr[g] += _dot(vt_scr[g, kt], jnp.exp2(s - bound[g]).astype(jnp.bfloat16))
            return carry
        lax.fori_loop(0, n_kt, body, 0)

    @pl.when(jnp.logical_not(single_sweep))
    def _():
        def logit_body(kt, m):
            rows = pl.ds(pl.multiple_of(kt * tk, tk), tk)
            new_m = []
            for g, s in enumerate(masked_logits(kt)):
                s_scr[g, rows, :] = s
                new_m.append(jnp.maximum(m[g], jnp.max(s, axis=0, keepdims=True)))
            return tuple(new_m)
        m0 = jnp.full((1, HEADS_PER_KV * tq), NEG_BIG, jnp.float32)
        m = lax.fori_loop(0, n_kt, logit_body, (m0,) * KV_HEADS)

        def pv_body(kt, carry):
            rows = pl.ds(pl.multiple_of(kt * tk, tk), tk)
            for g in range(KV_HEADS):
                p = jnp.exp2(s_scr[g, rows, :] - m[g]).astype(jnp.bfloat16)
                acc_scr[g] += _dot(vt_scr[g, kt], p)
            return carry
        lax.fori_loop(0, n_kt, pv_body, 0)

    for h in range(ATTN_HEADS):
        g, r = divmod(h, HEADS_PER_KV)
        cols = slice(r * tq, (r + 1) * tq)
        o = acc_scr[g, :HEAD_DIM, cols] / acc_scr[g, HEAD_DIM:HEAD_DIM + 1, cols]
        o_ref[0, :, h * HEAD_DIM:(h + 1) * HEAD_DIM] = o.T.astype(o_ref.dtype)


def _dsa_attention(p3, q_g, k_g, ln_w, ln_b, *, tq, tk):
    b, seq, _ = p3.shape
    topk = min(TOPK_MAX, seq // 4)
    idx_w_scale = (IDX_HEADS ** -0.5) * (IDX_DIM ** -0.5)
    assert seq % tq == 0 and tq % tk == 0 and tk % CHUNK == 0

    def col(off, width):
        assert off % width == 0 or width == LANES
        return off // width

    half_iq = IDX_WIDTH // 2
    return pl.pallas_call(
        functools.partial(_attn_kernel, tk=tk, topk=topk, idx_w_scale=idx_w_scale),
        out_shape=jax.ShapeDtypeStruct((b, seq, ATTN_WIDTH), jnp.bfloat16),
        grid=(b, seq // tq),
        in_specs=[
            pl.BlockSpec((1, tq, ATTN_WIDTH), lambda bi, i: (bi, i, col(AQ_OFF, ATTN_WIDTH))),
            pl.BlockSpec((1, tq, half_iq), lambda bi, i: (bi, i, col(IQ_OFF, half_iq))),
            pl.BlockSpec((1, tq, half_iq), lambda bi, i: (bi, i, col(IQ_OFF, half_iq) + 1)),
            pl.BlockSpec((1, tq, LANES), lambda bi, i: (bi, i, col(IW_OFF, LANES))),
            pl.BlockSpec((1, seq, KV_WIDTH), lambda bi, i: (bi, 0, col(AK_OFF, KV_WIDTH))),
            pl.BlockSpec((1, seq, KV_WIDTH), lambda bi, i: (bi, 0, col(AV_OFF, KV_WIDTH))),
            pl.BlockSpec((1, seq, LANES), lambda bi, i: (bi, 0, col(IK_OFF, LANES))),
            pl.BlockSpec((1, HEAD_DIM), lambda bi, i: (0, 0)),
            pl.BlockSpec((1, HEAD_DIM), lambda bi, i: (0, 0)),
            pl.BlockSpec((1, IDX_DIM), lambda bi, i: (0, 0)),
            pl.BlockSpec((1, IDX_DIM), lambda bi, i: (0, 0)),
        ],
        out_specs=pl.BlockSpec((1, tq, ATTN_WIDTH), lambda bi, i: (bi, i, 0)),
        scratch_shapes=[
            pltpu.VMEM((seq, KV_WIDTH), jnp.bfloat16),
            pltpu.VMEM((2, seq, 2 * IDX_DIM), jnp.bfloat16),
            pltpu.VMEM((KV_HEADS, seq // tk, HEAD_DIM + SUM_ROWS, tk), jnp.bfloat16),
            pltpu.VMEM((seq, tq), jnp.float32),
            pltpu.VMEM((seq, tq), jnp.bfloat16),
            pltpu.VMEM((LANES, tq), jnp.float32),
            pltpu.VMEM((KV_HEADS, HEADS_PER_KV * tq, HEAD_DIM), jnp.bfloat16),
            pltpu.VMEM((KV_HEADS, HEAD_DIM + SUM_ROWS, HEADS_PER_KV * tq), jnp.float32),
            pltpu.VMEM((KV_HEADS, seq, HEADS_PER_KV * tq), jnp.float32),
            pltpu.VMEM((KV_HEADS, 8, LANES), jnp.float32),
        ],
        compiler_params=_params("arbitrary", "arbitrary"),
        name="dsa_attn",
    )(p3, p3, p3, p3, p3, p3, p3, q_g, k_g, ln_w, ln_b)


RET_HEADS_PER_STEP = 2


def _ret_kernel(lg_ref, rq_ref, rk_ref, rv_ref, rg_ref, cos_ref, sin_ref, g_ref, o_ref, *, rc):
    seq = rq_ref.shape[1]
    n = lax.broadcasted_iota(jnp.int32, (rc, RET_DV), 0).astype(jnp.float32)
    rel = (lax.broadcasted_iota(jnp.int32, (rc, rc), 0)
           - lax.broadcasted_iota(jnp.int32, (rc, rc), 1)).astype(jnp.float32)

    def rot(x, rows):
        return x * cos_ref[rows, :] + pltpu.roll(x, RET_DK // 2, 1) * sin_ref[rows, :]

    heads = []
    for hh in range(RET_HEADS_PER_STEP):
        lg = lg_ref[pl.program_id(1) * RET_HEADS_PER_STEP + hh]
        heads.append(dict(
            cols=slice(hh * RET_DK, (hh + 1) * RET_DK),
            cross_decay=jnp.exp(lg * (n + 1.0)),
            state_decay=jnp.exp(lg * (rc - 1.0 - n)),
            chunk_decay=jnp.exp(lg * jnp.full((RET_DK, RET_DV), float(rc), jnp.float32)),
            intra=jnp.where(rel >= 0, jnp.exp(lg * jnp.maximum(rel, 0.0)), 0.0),
            state=jnp.zeros((RET_DK, RET_DV), jnp.float32),
            gain=g_ref[hh],
        ))

    for c in range(seq // rc):
        rows = slice(c * rc, (c + 1) * rc)
        for hd in heads:
            cols = hd["cols"]
            q = rot(rq_ref[0, rows, cols].astype(jnp.float32), rows)
            k = rot(rk_ref[0, rows, cols].astype(jnp.float32), rows) * (RET_DK ** -0.5)
            v = rv_ref[0, rows, cols]
            qb = q.astype(jnp.bfloat16)
            inner = _dot_nt(qb, k.astype(jnp.bfloat16)) * hd["intra"]
            o = (_dot(inner.astype(jnp.bfloat16), v)
                 + _dot(qb, hd["state"].astype(jnp.bfloat16)) * hd["cross_decay"])
            kd_t = (k * hd["state_decay"]).T.astype(jnp.bfloat16)
            hd["state"] = hd["state"] * hd["chunk_decay"] + _dot(kd_t, v)
            ms = jnp.mean(o * o, axis=-1, keepdims=True)
            y = (o * lax.rsqrt(ms + EPS)) * hd["gain"]
            gate = rg_ref[0, rows, cols].astype(jnp.float32)
            o_ref[0, rows, cols] = ((gate * (1.0 / (1.0 + jnp.exp(-gate)))) * y).astype(o_ref.dtype)


def _retention(p3, log_gamma, cos2, sin2, ret_g, *, rc):
    b, seq, _ = p3.shape
    hps = RET_HEADS_PER_STEP
    width = hps * RET_DK
    assert seq % rc == 0 and RET_HEADS % hps == 0

    def head_spec(off):
        assert off % width == 0
        return pl.BlockSpec((1, seq, width), lambda bi, h: (bi, 0, off // width + h))

    return pl.pallas_call(
        functools.partial(_ret_kernel, rc=rc),
        out_shape=jax.ShapeDtypeStruct((b, seq, RET_WIDTH), jnp.bfloat16),
        grid=(b, RET_HEADS // hps),
        in_specs=[
            pl.BlockSpec(memory_space=pltpu.SMEM),
            head_spec(RQ_OFF), head_spec(RK_OFF), head_spec(RV_OFF), head_spec(RG_OFF),
            pl.BlockSpec((seq, RET_DK), lambda bi, h: (0, 0)),
            pl.BlockSpec((seq, RET_DK), lambda bi, h: (0, 0)),
            pl.BlockSpec((hps, 1, RET_DV), lambda bi, h: (h, 0, 0)),
        ],
        out_specs=pl.BlockSpec((1, seq, width), lambda bi, h: (bi, 0, h)),
        compiler_params=_params("arbitrary", "arbitrary"),
        name="retention",
    )(log_gamma, p3, p3, p3, p3, cos2, sin2, ret_g)


def _routing(logits):
    lane = lax.broadcasted_iota(jnp.int32, logits.shape, 1).astype(jnp.float32)
    big = float(LANES)
    neg = -jnp.inf

    def first_argmax(v, vmax):
        return jnp.min(jnp.where(v == vmax, lane, big), axis=-1, keepdims=True)

    g_mask = (lane >= N_EXPERTS) & (lane < N_EXPERTS + N_GROUPS)
    gl = jnp.where(g_mask, logits, neg)
    g_max = jnp.max(gl, axis=-1, keepdims=True)
    g_sel = first_argmax(gl, g_max) - N_EXPERTS
    g_gate = 1.0 / jnp.sum(jnp.where(g_mask, jnp.exp(gl - g_max), 0.0), axis=-1, keepdims=True)

    e_lo = g_sel * EXPERTS_PER_GROUP
    el = jnp.where((lane >= e_lo) & (lane < e_lo + EXPERTS_PER_GROUP), logits, neg)
    v1 = jnp.max(el, axis=-1, keepdims=True)
    i1 = first_argmax(el, v1)
    el2 = jnp.where(lane == i1, neg, el)
    v2 = jnp.max(el2, axis=-1, keepdims=True)
    i2 = first_argmax(el2, v2)
    e2 = jnp.exp(v2 - v1)
    denom = 1.0 + e2
    w1 = (1.0 / denom) * g_gate
    w2 = (e2 / denom) * g_gate
    route = jnp.where(lane == ROUTE_E1, i1, 0.0) + jnp.where(lane == ROUTE_E2, i2, 0.0)
    return route + jnp.where(lane == ROUTE_W1, w1, 0.0) + jnp.where(lane == ROUTE_W2, w2, 0.0)


def _norm2(x1, g):
    ms = jnp.mean(x1 * x1, axis=-1, keepdims=True)
    return (x1 * lax.rsqrt(ms + EPS)) * g


OUT_SUBTILES = 2


def _out_proj_kernel(a_ref, r_ref, x_ref, wa_ref, wr_ref, g_ref, rcat_ref,
                     x1_ref, h2_ref, route_ref):
    sub = x_ref.shape[0] // OUT_SUBTILES
    for s in range(OUT_SUBTILES):
        rows = slice(s * sub, (s + 1) * sub)
        mixed = _dot(a_ref[rows, :], wa_ref[...]) + _dot(r_ref[rows, :], wr_ref[...])
        x1 = x_ref[rows, :] + mixed
        x1_ref[rows, :] = x1
        h2 = _norm2(x1, g_ref[...])
        h2_ref[rows, :] = h2
        hi = h2.astype(jnp.bfloat16)
        lo = (h2 - hi.astype(jnp.float32)).astype(jnp.bfloat16)
        both = _dot(hi, rcat_ref[...])
        logits = both[:, :LANES] + (both[:, LANES:] + _dot(lo, rcat_ref[:, :LANES]))
        route_ref[rows, :] = _routing(logits)


def _out_proj(attn2d, ret2d, x2d, w_out_bf, g2, r_cat, *, tm):
    n, d = x2d.shape
    return pl.pallas_call(
        _out_proj_kernel,
        out_shape=(
            jax.ShapeDtypeStruct((n, d), jnp.float32),
            jax.ShapeDtypeStruct((n, d), jnp.float32),
            jax.ShapeDtypeStruct((n, LANES), jnp.float32),
        ),
        grid=(n // tm,),
        in_specs=[
            pl.BlockSpec((tm, ATTN_WIDTH), lambda i: (i, 0)),
            pl.BlockSpec((tm, RET_WIDTH), lambda i: (i, 0)),
            pl.BlockSpec((tm, d), lambda i: (i, 0)),
            pl.BlockSpec((ATTN_WIDTH, d), lambda i: (0, 0)),
            pl.BlockSpec((RET_WIDTH, d), lambda i: (ATTN_WIDTH // RET_WIDTH, 0)),
            pl.BlockSpec((1, d), lambda i: (0, 0)),
            pl.BlockSpec((d, 2 * LANES), lambda i: (0, 0)),
        ],
        out_specs=(
            pl.BlockSpec((tm, d), lambda i: (i, 0)),
            pl.BlockSpec((tm, d), lambda i: (i, 0)),
            pl.BlockSpec((tm, LANES), lambda i: (i, 0)),
        ),
        compiler_params=_params("arbitrary"),
        name="out_proj",
    )(attn2d, ret2d, x2d, w_out_bf, w_out_bf, g2, r_cat)


def _plan_kernel(route_ref, pos_ref, cnt_ref, rank_scr, *, tm, blk):
    n = route_ref.shape[0]
    lane = lax.broadcasted_iota(jnp.int32, (blk, LANES), 1).astype(jnp.float32)
    before = (lax.broadcasted_iota(jnp.int32, (blk, blk), 1)
              < lax.broadcasted_iota(jnp.int32, (blk, blk), 0)).astype(jnp.bfloat16)

    def one_hot(rows):
        r = route_ref[rows, :]
        e1 = r[:, ROUTE_E1:ROUTE_E1 + 1]
        e2 = r[:, ROUTE_E2:ROUTE_E2 + 1]
        return lane == e1, lane == e2

    def rank_body(b, run):
        rows = pl.ds(pl.multiple_of(b * blk, blk), blk)
        m1, m2 = one_hot(rows)
        sel = jnp.where(m1 | m2, 1.0, 0.0)
        rank_scr[rows, :] = _dot(before, sel.astype(jnp.bfloat16)) + run
        return run + jnp.sum(sel, axis=0, keepdims=True)
    cnt = lax.fori_loop(0, n // blk, rank_body, jnp.zeros((1, LANES), jnp.float32), unroll=2)
    cnt_ref[...] = jnp.broadcast_to(cnt, cnt_ref.shape)

    tiles = jnp.floor((cnt + (tm - 1.0)) * (1.0 / tm))
    below = (lax.broadcasted_iota(jnp.int32, (LANES, LANES), 0)
             < lax.broadcasted_iota(jnp.int32, (LANES, LANES), 1)).astype(jnp.bfloat16)
    start = _dot(jnp.broadcast_to(tiles, (8, LANES)).astype(jnp.bfloat16), below)[0:1, :] * float(tm)

    def pos_body(b, carry):
        rows = pl.ds(pl.multiple_of(b * blk, blk), blk)
        m1, m2 = one_hot(rows)
        dest = rank_scr[rows, :] + start
        p1 = jnp.sum(jnp.where(m1, dest, 0.0), axis=-1, keepdims=True)
        p2 = jnp.sum(jnp.where(m2, dest, 0.0), axis=-1, keepdims=True)
        pos_ref[rows, :] = (jnp.where(lane == 0.0, p1, 0.0) + jnp.where(lane == 1.0, p2, 0.0)).astype(jnp.int32)
        return carry
    lax.fori_loop(0, n // blk, pos_body, 0, unroll=2)


def _moe_plan(route, *, tm, blk):
    n = route.shape[0]
    return pl.pallas_call(
        functools.partial(_plan_kernel, tm=tm, blk=blk),
        out_shape=(jax.ShapeDtypeStruct((n, LANES), jnp.int32),
                   jax.ShapeDtypeStruct((8, LANES), jnp.float32)),
        scratch_shapes=[pltpu.VMEM((n, LANES), jnp.float32)],
        compiler_params=pltpu.CompilerParams(vmem_limit_bytes=VMEM_LIMIT),
        name="moe_plan",
    )(route)


def _row_copy(src, src_row, dst, dst_row, sem):
    return pltpu.make_async_copy(src.at[pl.ds(src_row, 1), :], dst.at[pl.ds(dst_row, 1), :], sem)


def _invert_kernel(pos_ref, sid_ref):
    def body(j, carry):
        sid_ref[pos_ref[j]] = j
        return carry
    lax.fori_loop(0, pos_ref.shape[0], body, 0, unroll=8)


def _moe_invert(pos, *, rows):
    return pl.pallas_call(
        _invert_kernel,
        out_shape=jax.ShapeDtypeStruct((rows,), jnp.int32),
        in_specs=[pl.BlockSpec(memory_space=pltpu.SMEM)],
        out_specs=pl.BlockSpec(memory_space=pltpu.SMEM),
        name="moe_invert",
    )(pos)


ROW_GROUP = 8
ROW_BLOCK = 32
FFN_ROW_STEP = 128
NO_NEXT, NOT_FIRST = -1, -2


def _ffn_kernel(te_ref, tv_ref, nu_ref, seg_ref, nexte_ref, sid_ref, tok_ref,
                h2_ref, w1_ref, w3_ref, w2_ref, y2_ref,
                xbuf, ybuf, w1buf, w3buf, w2buf, gsem, ssem, wsem):
    t = pl.program_id(0)
    n_used = nu_ref[0]
    tm = xbuf.shape[1]

    def weight_copies(e, wslot):
        return [pltpu.make_async_copy(src.at[e], dst.at[wslot], wsem.at[wslot])
                for src, dst in ((w1_ref, w1buf), (w3_ref, w3buf), (w2_ref, w2buf))]

    def gather(slot):
        return tok_ref, lambda r, tok: _row_copy(h2_ref, tok, xbuf.at[slot], r, gsem.at[slot])

    def scatter(slot):
        return sid_ref, lambda r, sid: _row_copy(ybuf.at[slot], r, y2_ref, sid, ssem.at[slot])

    def start_all(tile, table, copy):
        valid = tv_ref[tile]
        for blk in range(tm // ROW_BLOCK):
            @pl.when(valid >= (blk + 1) * ROW_BLOCK)
            def _():
                for r in range(blk * ROW_BLOCK, (blk + 1) * ROW_BLOCK):
                    copy(r, table[tile * tm + r]).start()
        done = (valid // ROW_BLOCK) * ROW_BLOCK

        def body(c, carry):
            for u in range(ROW_GROUP):
                r = done + c * ROW_GROUP + u

                @pl.when(r < valid)
                def _():
                    copy(r, table[tile * tm + r]).start()
            return carry
        lax.fori_loop(0, (valid - done + (ROW_GROUP - 1)) // ROW_GROUP, body, 0)

    def wait_all(tile, copy, block_copy):
        valid = tv_ref[tile]
        for blk in range(tm // ROW_BLOCK):
            @pl.when(valid >= (blk + 1) * ROW_BLOCK)
            def _():
                block_copy.wait()

        def body(r, carry):
            copy(0, 0).wait()
            return carry
        lax.fori_loop(0, valid % ROW_BLOCK, body, 0)

    def wait_gather(tile, slot):
        wait_all(tile, gather(slot)[1],
                 pltpu.make_async_copy(h2_ref.at[pl.ds(0, ROW_BLOCK), :],
                                       xbuf.at[slot, pl.ds(0, ROW_BLOCK), :], gsem.at[slot]))

    def wait_scatter(tile, slot):
        wait_all(tile, scatter(slot)[1],
                 pltpu.make_async_copy(ybuf.at[slot, pl.ds(0, ROW_BLOCK), :],
                                       y2_ref.at[pl.ds(0, ROW_BLOCK), :], ssem.at[slot]))

    @pl.when(t == 0)
    def _():
        for cp in weight_copies(te_ref[0], 0):
            cp.start()
        start_all(0, *gather(0))

    @pl.when(t < n_used)
    def _():
        slot = t % 2
        wslot = seg_ref[t] % 2

        @pl.when(t + 1 < n_used)
        def _():
            start_all(t + 1, *gather(1 - slot))

        @pl.when(nexte_ref[t] != NOT_FIRST)
        def _():
            for cp in weight_copies(te_ref[t], wslot):
                cp.wait()

            @pl.when(nexte_ref[t] >= 0)
            def _():
                for cp in weight_copies(nexte_ref[t], 1 - wslot):
                    cp.start()

        wait_gather(t, slot)

        @pl.when(t >= 2)
        def _():
            wait_scatter(t - 2, slot)

        valid = tv_ref[t]
        for live in range(FFN_ROW_STEP, tm + 1, FFN_ROW_STEP):
            @pl.when((valid > live - FFN_ROW_STEP) & (valid <= live))
            def _():
                row = lax.broadcasted_iota(jnp.int32, (live, xbuf.shape[2]), 0)
                x = jnp.where(row < valid, xbuf[slot, :live, :], 0.0).astype(jnp.bfloat16)
                a = _dot(x, w1buf[wslot].astype(jnp.bfloat16))
                b = _dot(x, w3buf[wslot].astype(jnp.bfloat16))
                act = (a * (1.0 / (1.0 + jnp.exp(-a)))) * b
                ybuf[slot, :live, :] = _dot(act.astype(jnp.bfloat16), w2buf[wslot].astype(jnp.bfloat16))
        start_all(t, *scatter(slot))

    @pl.when(t == pl.num_programs(0) - 1)
    def _():
        @pl.when(n_used >= 2)
        def _():
            wait_scatter(n_used - 2, n_used % 2)
        wait_scatter(n_used - 1, (n_used - 1) % 2)


def _moe_ffn(tile_expert, tile_valid, n_used, tile_seg, tile_next, row_sid, row_tok, h2, w1, w3, w2,
             *, tm):
    n, d = h2.shape
    _, _, ff = w1.shape
    max_tiles = tile_expert.shape[0]
    any_spec = pl.BlockSpec(memory_space=pl.ANY)
    return pl.pallas_call(
        _ffn_kernel,
        out_shape=jax.ShapeDtypeStruct((2 * n, d), jnp.float32),
        grid_spec=pltpu.PrefetchScalarGridSpec(
            num_scalar_prefetch=7,
            grid=(max_tiles,),
            in_specs=[any_spec, any_spec, any_spec, any_spec],
            out_specs=any_spec,
            scratch_shapes=[
                pltpu.VMEM((2, tm, d), jnp.float32),
                pltpu.VMEM((2, tm, d), jnp.float32),
                pltpu.VMEM((2, d, ff), jnp.float32),
                pltpu.VMEM((2, d, ff), jnp.float32),
                pltpu.VMEM((2, ff, d), jnp.float32),
                pltpu.SemaphoreType.DMA((2,)),
                pltpu.SemaphoreType.DMA((2,)),
                pltpu.SemaphoreType.DMA((2,)),
            ],
        ),
        compiler_params=_params("arbitrary"),
        name="moe_ffn",
    )(tile_expert, tile_valid, n_used, tile_seg, tile_next, row_sid, row_tok, h2, w1, w3, w2)


def _combine_kernel(x1_ref, route_ref, ya_ref, yb_ref, o_ref):
    route = route_ref[...]
    w1 = route[:, ROUTE_W1:ROUTE_W1 + 1]
    w2 = route[:, ROUTE_W2:ROUTE_W2 + 1]
    o_ref[...] = x1_ref[...] + (w1 * ya_ref[...] + w2 * yb_ref[...])


def _moe_combine(x1, route, y2, *, tm):
    n, d = x1.shape
    return pl.pallas_call(
        _combine_kernel,
        out_shape=jax.ShapeDtypeStruct((n, d), jnp.float32),
        grid=(n // tm,),
        in_specs=[pl.BlockSpec((tm, d), lambda i: (i, 0)),
                  pl.BlockSpec((tm, LANES), lambda i: (i, 0)),
                  pl.BlockSpec((tm, d), lambda i: (i, 0)),
                  pl.BlockSpec((tm, d), lambda i: (i + n // tm, 0))],
        out_specs=pl.BlockSpec((tm, d), lambda i: (i, 0)),
        compiler_params=_params("arbitrary"),
        name="moe_combine",
    )(x1, route, y2, y2)


def _moe(x1, h2, route, w1, w3, w2, *, tm, gather_tm):
    n, d = x1.shape
    max_tiles = (2 * n) // tm + N_EXPERTS
    pos2d, cnt = _moe_plan(route, tm=tm, blk=gather_tm)
    pos = pos2d[:, :2].T.reshape(2 * n)
    counts = cnt[0, :N_EXPERTS].astype(jnp.int32)
    tiles = (counts + (tm - 1)) // tm
    ends = jnp.cumsum(tiles)
    t_idx = jnp.arange(max_tiles, dtype=jnp.int32)
    tile_expert = jnp.sum((ends[None, :] <= t_idx[:, None]).astype(jnp.int32), axis=1)
    tile_expert = jnp.minimum(tile_expert, N_EXPERTS - 1)
    first_tile = (ends - tiles)[tile_expert]
    tile_valid = jnp.clip(counts[tile_expert] - (t_idx - first_tile) * tm, 0, tm).astype(jnp.int32)
    n_used = ends[-1:].astype(jnp.int32)
    used = t_idx < n_used[0]
    is_first = used & (t_idx == first_tile)
    tile_seg = (jnp.cumsum(is_first.astype(jnp.int32)) - 1).astype(jnp.int32)
    next_start = first_tile + tiles[tile_expert]
    next_e = jnp.where(next_start < n_used[0], tile_expert[jnp.minimum(next_start, max_tiles - 1)], NO_NEXT)
    tile_next = jnp.where(is_first, next_e, NOT_FIRST).astype(jnp.int32)

    row_sid = _moe_invert(pos, rows=max_tiles * tm)
    row_tok = jnp.where(row_sid >= n, row_sid - n, row_sid)
    y2 = _moe_ffn(tile_expert, tile_valid, n_used, tile_seg, tile_next, row_sid, row_tok, h2, w1, w3, w2,
                  tm=tm)
    return _moe_combine(x1, route, y2, tm=gather_tm)


def _pack_router(w_group, w_router):
    d = w_group.shape[0]
    experts = jnp.transpose(w_router, (1, 0, 2)).reshape(d, N_EXPERTS)
    wr = jnp.concatenate(
        [experts, w_group, jnp.zeros((d, LANES - N_EXPERTS - N_GROUPS), w_group.dtype)], axis=1)
    hi = wr.astype(jnp.bfloat16)
    lo = (wr - hi.astype(jnp.float32)).astype(jnp.bfloat16)
    return jnp.concatenate([hi, lo], axis=1)


def _rotation_tables(seq):
    half = RET_DK // 2
    pos = jnp.arange(seq, dtype=jnp.float32)
    inv = 1.0 / (ROT_BASE ** jnp.linspace(0.0, 1.0, half, dtype=jnp.float32))
    ang = pos[:, None] * inv[None, :]
    c, s = jnp.cos(ang), jnp.sin(ang)
    return jnp.concatenate([c, c], axis=-1), jnp.concatenate([-s, s], axis=-1)


def _tiles(n, seq):
    def fit(total, want):
        t = min(total, want)
        while total % t:
            t //= 2
        return t
    return dict(
        proj_tm=fit(n, 1024),
        attn_tq=fit(seq, 256), attn_tk=fit(seq, 256),
        ret_rc=fit(seq, 256),
        out_tm=fit(n, 512),
        moe_tm=fit(n, 512), moe_gather_tm=fit(n, 256),
    )


def kernel(x, norm1_g, w_in, q_norm_g, k_norm_g, idx_k_ln_w, idx_k_ln_b, ret_norm_g,
           w_out, norm2_g, w_group, w_router, w1, w3, w2):
    b, seq, d = x.shape
    n = b * seq
    depth = w_in.shape[0]
    t = _tiles(n, seq)
    cos2, sin2 = _rotation_tables(seq)
    log_gamma = jnp.log1p(-jnp.exp2(-5.0 - jnp.arange(RET_HEADS, dtype=jnp.float32)))

    x2d = x.reshape(n, d)
    for l in range(depth):
        proj = _in_proj(x2d, norm1_g[l][None, :], w_in[l].T, tm=t["proj_tm"])
        p3 = proj.reshape(b, seq, proj.shape[1])
        attn = _dsa_attention(p3, q_norm_g[l][None, :], k_norm_g[l][None, :],
                              idx_k_ln_w[l][None, :], idx_k_ln_b[l][None, :],
                              tq=t["attn_tq"], tk=t["attn_tk"])
        ret = _retention(p3, log_gamma, cos2, sin2, ret_norm_g[l].reshape(RET_HEADS, 1, RET_DV),
                         rc=t["ret_rc"])
        r_cat = _pack_router(w_group[l], w_router[l])
        g2 = norm2_g[l][None, :]
        x1, h2, route = _out_proj(attn.reshape(n, ATTN_WIDTH), ret.reshape(n, RET_WIDTH), x2d,
                                  w_out[l].astype(jnp.bfloat16), g2, r_cat, tm=t["out_tm"])
        x2d = _moe(x1, h2, route, w1[l], w3[l], w2[l], tm=t["moe_tm"], gather_tm=t["moe_gather_tm"])
    return x2d.reshape(b, seq, d)
```

```python
import functools
import math

import jax
import jax.numpy as jnp
from jax import lax
from jax.experimental import pallas as pl
from jax.experimental.pallas import tpu as pltpu

CHUNK = 64
ATTN_HEADS = 8
HEAD_DIM = 128
KV_HEADS = 2
HEADS_PER_KV = ATTN_HEADS // KV_HEADS
IDX_HEADS = 16
IDX_DIM = 64
TOPK_MAX = 256
RET_HEADS = 8
RET_DK = 128
RET_DV = 128
ROT_BASE = 10000.0
N_GROUPS = 4
EXPERTS_PER_GROUP = 8
N_EXPERTS = N_GROUPS * EXPERTS_PER_GROUP
EPS = 1e-6

ATTN_WIDTH = ATTN_HEADS * HEAD_DIM
KV_WIDTH = KV_HEADS * HEAD_DIM
IDX_WIDTH = IDX_HEADS * IDX_DIM
RET_WIDTH = RET_HEADS * RET_DK

LANES = 128
SUBLANES = 8
VMEM_LIMIT = 56 * 1024 * 1024

AQ_OFF = 0
AK_OFF = AQ_OFF + ATTN_WIDTH
AV_OFF = AK_OFF + KV_WIDTH
IQ_OFF = AV_OFF + KV_WIDTH
IK_OFF = IQ_OFF + IDX_WIDTH
IW_OFF = IK_OFF + IDX_DIM
W_RET = IW_OFF + IDX_HEADS
IN_WIDTH = W_RET + 4 * RET_WIDTH
assert IW_OFF // LANES == IK_OFF // LANES
PROJ_TN = 1024
RQ_OFF = -(-W_RET // PROJ_TN) * PROJ_TN
RK_OFF = RQ_OFF + RET_WIDTH
RV_OFF = RK_OFF + RET_WIDTH
RG_OFF = RV_OFF + RET_WIDTH
PROJ_WIDTH = RG_OFF + RET_WIDTH

ROUTE_E1, ROUTE_E2, ROUTE_W1, ROUTE_W2 = 0, 1, 2, 3

SUM_ROWS = 16
LOGIT_BOUND_SLACK = 1.05
MAX_SINGLE_SWEEP_BOUND = 50.0

NEG_BIG = -1e30

_NT = (((1,), (1,)), ((), ()))


def _dot(a, b):
    return jnp.dot(a, b, preferred_element_type=jnp.float32)


def _dot_nt(a, b):
    return lax.dot_general(a, b, _NT, preferred_element_type=jnp.float32)


def _params(*sem):
    return pltpu.CompilerParams(dimension_semantics=sem, vmem_limit_bytes=VMEM_LIMIT)


def _in_proj_kernel(x_ref, g_ref, wt_ref, o_ref, h_scr, *, row_chunk, partial_tile, partial_cols):
    j = pl.program_id(1)
    first = j == 0
    partial = j == partial_tile

    @pl.when(first)
    def _():
        w = wt_ref[...].astype(jnp.bfloat16)
        for c in range(x_ref.shape[0] // row_chunk):
            rows = slice(c * row_chunk, (c + 1) * row_chunk)
            x = x_ref[rows, :]
            ms = jnp.mean(x * x, axis=-1, keepdims=True)
            h = ((x * lax.rsqrt(ms + EPS)) * g_ref[...]).astype(jnp.bfloat16)
            h_scr[rows, :] = h
            o_ref[rows, :] = _dot_nt(h, w).astype(o_ref.dtype)

    @pl.when(partial)
    def _():
        w = wt_ref[:partial_cols, :].astype(jnp.bfloat16)
        o_ref[:, :partial_cols] = _dot_nt(h_scr[...], w).astype(o_ref.dtype)
        o_ref[:, partial_cols:] = jnp.zeros((o_ref.shape[0], o_ref.shape[1] - partial_cols), o_ref.dtype)

    @pl.when(jnp.logical_not(first | partial))
    def _():
        o_ref[...] = _dot_nt(h_scr[...], wt_ref[...].astype(jnp.bfloat16)).astype(o_ref.dtype)


def _in_proj(x2d, g, w_in_t, *, tm):
    n, d = x2d.shape
    tn = PROJ_TN
    assert w_in_t.shape == (IN_WIDTH, d) and W_RET % SUBLANES == 0 and tn % SUBLANES == 0
    attn_tiles = RQ_OFF // tn
    partial_tile = attn_tiles - 1
    partial_cols = -(-(W_RET - partial_tile * tn) // LANES) * LANES
    assert 0 < partial_tile and 0 < partial_cols < tn

    def window(i, j):
        step = tn // SUBLANES
        start = jnp.where(j < attn_tiles, j * step, W_RET // SUBLANES + (j - attn_tiles) * step)
        return SUBLANES * start, 0

    return pl.pallas_call(
        functools.partial(_in_proj_kernel, row_chunk=min(tm, 256), partial_tile=partial_tile,
                          partial_cols=partial_cols),
        out_shape=jax.ShapeDtypeStruct((n, PROJ_WIDTH), jnp.bfloat16),
        grid=(n // tm, PROJ_WIDTH // tn),
        in_specs=[
            pl.BlockSpec((tm, d), lambda i, j: (i, 0)),
            pl.BlockSpec((1, d), lambda i, j: (0, 0)),
            pl.BlockSpec((pl.Element(tn), pl.Element(d)), window),
        ],
        out_specs=pl.BlockSpec((tm, tn), lambda i, j: (i, j)),
        scratch_shapes=[pltpu.VMEM((tm, d), jnp.bfloat16)],
        compiler_params=_params("arbitrary", "arbitrary"),
        name="in_proj",
    )(x2d, g, w_in_t)


def _ordered_float(v):
    bits = v ^ ((v >> 31) & jnp.int32(0x7FFFFFFF))
    return pltpu.bitcast(bits, jnp.float32)


def _ordered_bfloat(v16):
    bits16 = v16 ^ ((v16 >> 15) & jnp.int32(0x7FFF))
    return pltpu.bitcast(bits16 << 16, jnp.float32).astype(jnp.bfloat16)


def _attn_kernel(aq_ref, iqa_ref, iqb_ref, iw_ref, ak_ref, av_ref, ik_ref, qg_ref, kg_ref, lnw_ref,
                 lnb_ref, o_ref,
                 kn_scr, ikn_scr, vt_scr, key_scr, kb_scr, wt_scr, qn_scr, acc_scr, s_scr, kmax_scr,
                 *, tk, topk, idx_w_scale):
    i = pl.program_id(1)
    seq = ak_ref.shape[1]
    tq = aq_ref.shape[1]
    chunk_shift = CHUNK.bit_length() - 1

    @pl.when(i == 0)
    def _():
        def body(c, carry):
            rows = pl.ds(pl.multiple_of(c * tk, tk), tk)
            for g in range(KV_HEADS):
                cols = slice(g * HEAD_DIM, (g + 1) * HEAD_DIM)
                k = ak_ref[0, rows, cols].astype(jnp.float32)
                ms = jnp.mean(k * k, axis=-1, keepdims=True)
                kn = (k * lax.rsqrt(ms + EPS)) * kg_ref[...]
                kn_scr[rows, cols] = kn.astype(jnp.bfloat16)
                ksq = jnp.max(jnp.sum(kn * kn, axis=-1, keepdims=True), axis=0, keepdims=True)
                prev = jnp.where(c == 0, 0.0, kmax_scr[g])
                kmax_scr[g] = jnp.maximum(prev, jnp.broadcast_to(ksq, kmax_scr.shape[1:]))
                v = av_ref[0, rows, cols].astype(jnp.float32)
                vt_scr[g, c, :HEAD_DIM, :] = v.T.astype(jnp.bfloat16)
                vt_scr[g, c, HEAD_DIM:, :] = jnp.ones((SUM_ROWS, tk), jnp.bfloat16)
            ki = ik_ref[0, rows, :IDX_DIM].astype(jnp.float32)
            mu = jnp.mean(ki, axis=-1, keepdims=True)
            var = jnp.mean(jnp.square(ki - mu), axis=-1, keepdims=True)
            y = ((ki - mu) * lax.rsqrt(var + EPS) * lnw_ref[...] + lnb_ref[...]).astype(jnp.bfloat16)
            zeros = jnp.zeros_like(y)
            ikn_scr[0, rows, :] = jnp.concatenate([y, zeros], axis=1)
            ikn_scr[1, rows, :] = jnp.concatenate([zeros, y], axis=1)
            return carry
        lax.fori_loop(0, seq // tk, body, 0)

    t0 = i * tq
    n_kt = (t0 + tq) // tk
    scale = (HEAD_DIM ** -0.5) * math.log2(math.e)
    for h in range(ATTN_HEADS):
        g, r = divmod(h, HEADS_PER_KV)
        q = aq_ref[0, :, h * HEAD_DIM:(h + 1) * HEAD_DIM].astype(jnp.float32)
        ms = jnp.mean(q * q, axis=-1, keepdims=True)
        qn_scr[g, r * tq:(r + 1) * tq, :] = (
            (q * lax.rsqrt(ms + EPS)) * qg_ref[...] * scale).astype(jnp.bfloat16)
    ones_rows = jnp.ones((8, HEAD_DIM), jnp.bfloat16)
    bound = []
    for g in range(KV_HEADS):
        qf = qn_scr[g].astype(jnp.float32)
        qsq = _dot_nt(ones_rows, (qf * qf).astype(jnp.bfloat16))[0:1, :]
        kmax = jnp.concatenate([kmax_scr[g, 0:1, :]] * (HEADS_PER_KV * tq // LANES), axis=1)
        bound.append(LOGIT_BOUND_SLACK * jnp.sqrt(qsq * kmax))
    wt_scr[...] = iw_ref[0].astype(jnp.float32).T * idx_w_scale
    w_row = IW_OFF % LANES

    q_chunk = (t0 + lax.broadcasted_iota(jnp.int32, (tk, tq), 1)) >> chunk_shift

    def score_body(kt, carry):
        rows = pl.ds(pl.multiple_of(kt * tk, tk), tk)
        ik_first, ik_second = ikn_scr[0, rows, :], ikn_scr[1, rows, :]
        acc = jnp.zeros((tk, tq), jnp.float32)
        pairs_per_ref = iqa_ref.shape[2] // LANES
        for pair in range(IDX_HEADS // 2):
            src = iqa_ref if pair < pairs_per_ref else iqb_ref
            lane0 = (pair % pairs_per_ref) * LANES
            q_pair = src[0, :, lane0:lane0 + LANES]
            for sub, ik_t in enumerate((ik_first, ik_second)):
                h = 2 * pair + sub
                d = _dot_nt(ik_t, q_pair)
                acc = acc + jnp.maximum(d, 0.0) * wt_scr[w_row + h:w_row + h + 1, :]
        k_chunk = (kt * tk + lax.broadcasted_iota(jnp.int32, (tk, tq), 0)) >> chunk_shift
        score = jnp.where(k_chunk <= q_chunk, acc, -jnp.inf)
        key_scr[rows, :] = score
        kb_scr[rows, :] = score.astype(jnp.bfloat16)
        return carry
    lax.fori_loop(0, n_kt, score_body, 0)

    def tree_sum(hit):
        while hit.shape[0] > 1:
            half = hit.shape[0] // 2
            hit = hit[:half] + hit[half:]
        return hit[0]

    packed = 16
    searching = (t0 + tq > topk).astype(jnp.int32)

    def coarse_body(it, lo16):
        cand16 = lo16 + lax.shift_left(jnp.int32(1), 15 - it)
        cand_b = _ordered_bfloat(cand16)

        def count_body(kt, part):
            rows = pl.ds(pl.multiple_of(kt * tk, tk), tk)
            hit = jnp.where(kb_scr[rows, :] >= cand_b, jnp.bfloat16(1), jnp.bfloat16(0))
            return part + tree_sum(hit.reshape(tk // packed, packed, tq))
        part = lax.fori_loop(0, n_kt, count_body, jnp.zeros((packed, tq), jnp.bfloat16))
        cnt = jnp.sum(part.astype(jnp.float32), axis=0, keepdims=True)
        return jnp.where(cnt >= float(topk), cand16, lo16)
    lo16 = lax.fori_loop(0, 16 * searching, coarse_body, jnp.full((1, tq), -(2 ** 15), jnp.int32))

    fine_bits = 18
    lo0 = jnp.maximum(lo16 - 1, -(2 ** 15)) << 16

    def fine_body(it, lo):
        cand = lo + lax.shift_left(jnp.int32(1), fine_bits - 1 - it)
        cand_f = _ordered_float(cand)

        def count_body(kt, part):
            rows = pl.ds(pl.multiple_of(kt * tk, tk), tk)
            hit = jnp.where(key_scr[rows, :] >= cand_f, 1.0, 0.0)
            return part + tree_sum(hit.reshape(tk // SUBLANES, SUBLANES, tq))
        part = lax.fori_loop(0, n_kt, count_body, jnp.zeros((SUBLANES, tq), jnp.float32))
        cnt = jnp.sum(part, axis=0, keepdims=True)
        return jnp.where(cnt >= float(topk), cand, lo)
    lo = lax.fori_loop(0, fine_bits * searching, fine_body, lo0)
    thr = jnp.where(lo16 == -(2 ** 15), jnp.finfo(jnp.float32).min, _ordered_float(lo))

    acc_scr[...] = jnp.zeros(acc_scr.shape, jnp.float32)

    def masked_logits(kt):
        rows = pl.ds(pl.multiple_of(kt * tk, tk), tk)
        bias = jnp.where(key_scr[rows, :] >= thr, 0.0, NEG_BIG)
        bias = jnp.concatenate([bias] * HEADS_PER_KV, axis=1)
        return [_dot_nt(kn_scr[rows, g * HEAD_DIM:(g + 1) * HEAD_DIM], qn_scr[g]) + bias
                for g in range(KV_HEADS)]

    bound_max = jnp.max(jnp.maximum(bound[0], bound[1]))
    single_sweep = bound_max <= MAX_SINGLE_SWEEP_BOUND

    @pl.when(single_sweep)
    def _():
        def body(kt, carry):
            for g, s in enumerate(masked_logits(kt)):
                acc_scr[g] += _dot(vt_scr[g, kt], jnp.exp2(s - bound[g]).astype(jnp.bfloat16))
            return carry
        lax.fori_loop(0, n_kt, body, 0)

    @pl.when(jnp.logical_not(single_sweep))
    def _():
        def logit_body(kt, m):
            rows = pl.ds(pl.multiple_of(kt * tk, tk), tk)
            new_m = []
            for g, s in enumerate(masked_logits(kt)):
                s_scr[g, rows, :] = s
                new_m.append(jnp.maximum(m[g], jnp.max(s, axis=0, keepdims=True)))
            return tuple(new_m)
        m0 = jnp.full((1, HEADS_PER_KV * tq), NEG_BIG, jnp.float32)
        m = lax.fori_loop(0, n_kt, logit_body, (m0,) * KV_HEADS)

        def pv_body(kt, carry):
            rows = pl.ds(pl.multiple_of(kt * tk, tk), tk)
            for g in range(KV_HEADS):
                p = jnp.exp2(s_scr[g, rows, :] - m[g]).astype(jnp.bfloat16)
                acc_scr[g] += _dot(vt_scr[g, kt], p)
            return carry
        lax.fori_loop(0, n_kt, pv_body, 0)

    for h in range(ATTN_HEADS):
        g, r = divmod(h, HEADS_PER_KV)
        cols = slice(r * tq, (r + 1) * tq)
        o = acc_scr[g, :HEAD_DIM, cols] / acc_scr[g, HEAD_DIM:HEAD_DIM + 1, cols]
        o_ref[0, :, h * HEAD_DIM:(h + 1) * HEAD_DIM] = o.T.astype(o_ref.dtype)


def _dsa_attention(p3, q_g, k_g, ln_w, ln_b, *, tq, tk):
    b, seq, _ = p3.shape
    topk = min(TOPK_MAX, seq // 4)
    idx_w_scale = (IDX_HEADS ** -0.5) * (IDX_DIM ** -0.5)
    assert seq % tq == 0 and tq % tk == 0 and tk % CHUNK == 0

    def col(off, width):
        assert off % width == 0 or width == LANES
        return off // width

    half_iq = IDX_WIDTH // 2
    return pl.pallas_call(
        functools.partial(_attn_kernel, tk=tk, topk=topk, idx_w_scale=idx_w_scale),
        out_shape=jax.ShapeDtypeStruct((b, seq, ATTN_WIDTH), jnp.bfloat16),
        grid=(b, seq // tq),
        in_specs=[
            pl.BlockSpec((1, tq, ATTN_WIDTH), lambda bi, i: (bi, i, col(AQ_OFF, ATTN_WIDTH))),
            pl.BlockSpec((1, tq, half_iq), lambda bi, i: (bi, i, col(IQ_OFF, half_iq))),
            pl.BlockSpec((1, tq, half_iq), lambda bi, i: (bi, i, col(IQ_OFF, half_iq) + 1)),
            pl.BlockSpec((1, tq, LANES), lambda bi, i: (bi, i, col(IW_OFF, LANES))),
            pl.BlockSpec((1, seq, KV_WIDTH), lambda bi, i: (bi, 0, col(AK_OFF, KV_WIDTH))),
            pl.BlockSpec((1, seq, KV_WIDTH), lambda bi, i: (bi, 0, col(AV_OFF, KV_WIDTH))),
            pl.BlockSpec((1, seq, LANES), lambda bi, i: (bi, 0, col(IK_OFF, LANES))),
            pl.BlockSpec((1, HEAD_DIM), lambda bi, i: (0, 0)),
            pl.BlockSpec((1, HEAD_DIM), lambda bi, i: (0, 0)),
            pl.BlockSpec((1, IDX_DIM), lambda bi, i: (0, 0)),
            pl.BlockSpec((1, IDX_DIM), lambda bi, i: (0, 0)),
        ],
        out_specs=pl.BlockSpec((1, tq, ATTN_WIDTH), lambda bi, i: (bi, i, 0)),
        scratch_shapes=[
            pltpu.VMEM((seq, KV_WIDTH), jnp.bfloat16),
            pltpu.VMEM((2, seq, 2 * IDX_DIM), jnp.bfloat16),
            pltpu.VMEM((KV_HEADS, seq // tk, HEAD_DIM + SUM_ROWS, tk), jnp.bfloat16),
            pltpu.VMEM((seq, tq), jnp.float32),
            pltpu.VMEM((seq, tq), jnp.bfloat16),
            pltpu.VMEM((LANES, tq), jnp.float32),
            pltpu.VMEM((KV_HEADS, HEADS_PER_KV * tq, HEAD_DIM), jnp.bfloat16),
            pltpu.VMEM((KV_HEADS, HEAD_DIM + SUM_ROWS, HEADS_PER_KV * tq), jnp.float32),
            pltpu.VMEM((KV_HEADS, seq, HEADS_PER_KV * tq), jnp.float32),
            pltpu.VMEM((KV_HEADS, 8, LANES), jnp.float32),
        ],
        compiler_params=_params("arbitrary", "arbitrary"),
        name="dsa_attn",
    )(p3, p3, p3, p3, p3, p3, p3, q_g, k_g, ln_w, ln_b)


RET_HEADS_PER_STEP = 2


def _ret_kernel(lg_ref, rq_ref, rk_ref, rv_ref, rg_ref, cos_ref, sin_ref, g_ref, o_ref, *, rc):
    seq = rq_ref.shape[1]
    n = lax.broadcasted_iota(jnp.int32, (rc, RET_DV), 0).astype(jnp.float32)
    rel = (lax.broadcasted_iota(jnp.int32, (rc, rc), 0)
           - lax.broadcasted_iota(jnp.int32, (rc, rc), 1)).astype(jnp.float32)

    def rot(x, rows):
        return x * cos_ref[rows, :] + pltpu.roll(x, RET_DK // 2, 1) * sin_ref[rows, :]

    heads = []
    for hh in range(RET_HEADS_PER_STEP):
        lg = lg_ref[pl.program_id(1) * RET_HEADS_PER_STEP + hh]
        heads.append(dict(
            cols=slice(hh * RET_DK, (hh + 1) * RET_DK),
            cross_decay=jnp.exp(lg * (n + 1.0)),
            state_decay=jnp.exp(lg * (rc - 1.0 - n)),
            chunk_decay=jnp.exp(lg * jnp.full((RET_DK, RET_DV), float(rc), jnp.float32)),
            intra=jnp.where(rel >= 0, jnp.exp(lg * jnp.maximum(rel, 0.0)), 0.0),
            state=jnp.zeros((RET_DK, RET_DV), jnp.float32),
            gain=g_ref[hh],
        ))

    for c in range(seq // rc):
        rows = slice(c * rc, (c + 1) * rc)
        for hd in heads:
            cols = hd["cols"]
            q = rot(rq_ref[0, rows, cols].astype(jnp.float32), rows)
            k = rot(rk_ref[0, rows, cols].astype(jnp.float32), rows) * (RET_DK ** -0.5)
            v = rv_ref[0, rows, cols]
            qb = q.astype(jnp.bfloat16)
            inner = _dot_nt(qb, k.astype(jnp.bfloat16)) * hd["intra"]
            o = (_dot(inner.astype(jnp.bfloat16), v)
                 + _dot(qb, hd["state"].astype(jnp.bfloat16)) * hd["cross_decay"])
            kd_t = (k * hd["state_decay"]).T.astype(jnp.bfloat16)
            hd["state"] = hd["state"] * hd["chunk_decay"] + _dot(kd_t, v)
            ms = jnp.mean(o * o, axis=-1, keepdims=True)
            y = (o * lax.rsqrt(ms + EPS)) * hd["gain"]
            gate = rg_ref[0, rows, cols].astype(jnp.float32)
            o_ref[0, rows, cols] = ((gate * (1.0 / (1.0 + jnp.exp(-gate)))) * y).astype(o_ref.dtype)


def _retention(p3, log_gamma, cos2, sin2, ret_g, *, rc):
    b, seq, _ = p3.shape
    hps = RET_HEADS_PER_STEP
    width = hps * RET_DK
    assert seq % rc == 0 and RET_HEADS % hps == 0

    def head_spec(off):
        assert off % width == 0
        return pl.BlockSpec((1, seq, width), lambda bi, h: (bi, 0, off // width + h))

    return pl.pallas_call(
        functools.partial(_ret_kernel, rc=rc),
        out_shape=jax.ShapeDtypeStruct((b, seq, RET_WIDTH), jnp.bfloat16),
        grid=(b, RET_HEADS // hps),
        in_specs=[
            pl.BlockSpec(memory_space=pltpu.SMEM),
            head_spec(RQ_OFF), head_spec(RK_OFF), head_spec(RV_OFF), head_spec(RG_OFF),
            pl.BlockSpec((seq, RET_DK), lambda bi, h: (0, 0)),
            pl.BlockSpec((seq, RET_DK), lambda bi, h: (0, 0)),
            pl.BlockSpec((hps, 1, RET_DV), lambda bi, h: (h, 0, 0)),
        ],
        out_specs=pl.BlockSpec((1, seq, width), lambda bi, h: (bi, 0, h)),
        compiler_params=_params("arbitrary", "arbitrary"),
        name="retention",
    )(log_gamma, p3, p3, p3, p3, cos2, sin2, ret_g)


def _routing(logits):
    lane = lax.broadcasted_iota(jnp.int32, logits.shape, 1).astype(jnp.float32)
    big = float(LANES)
    neg = -jnp.inf

    def first_argmax(v, vmax):
        return jnp.min(jnp.where(v == vmax, lane, big), axis=-1, keepdims=True)

    g_mask = (lane >= N_EXPERTS) & (lane < N_EXPERTS + N_GROUPS)
    gl = jnp.where(g_mask, logits, neg)
    g_max = jnp.max(gl, axis=-1, keepdims=True)
    g_sel = first_argmax(gl, g_max) - N_EXPERTS
    g_gate = 1.0 / jnp.sum(jnp.where(g_mask, jnp.exp(gl - g_max), 0.0), axis=-1, keepdims=True)

    e_lo = g_sel * EXPERTS_PER_GROUP
    el = jnp.where((lane >= e_lo) & (lane < e_lo + EXPERTS_PER_GROUP), logits, neg)
    v1 = jnp.max(el, axis=-1, keepdims=True)
    i1 = first_argmax(el, v1)
    el2 = jnp.where(lane == i1, neg, el)
    v2 = jnp.max(el2, axis=-1, keepdims=True)
    i2 = first_argmax(el2, v2)
    e2 = jnp.exp(v2 - v1)
    denom = 1.0 + e2
    w1 = (1.0 / denom) * g_gate
    w2 = (e2 / denom) * g_gate
    route = jnp.where(lane == ROUTE_E1, i1, 0.0) + jnp.where(lane == ROUTE_E2, i2, 0.0)
    return route + jnp.where(lane == ROUTE_W1, w1, 0.0) + jnp.where(lane == ROUTE_W2, w2, 0.0)


def _norm2(x1, g):
    ms = jnp.mean(x1 * x1, axis=-1, keepdims=True)
    return (x1 * lax.rsqrt(ms + EPS)) * g


OUT_SUBTILES = 2


def _out_proj_kernel(a_ref, r_ref, x_ref, wa_ref, wr_ref, g_ref, rcat_ref,
                     x1_ref, h2_ref, route_ref):
    sub = x_ref.shape[0] // OUT_SUBTILES
    for s in range(OUT_SUBTILES):
        rows = slice(s * sub, (s + 1) * sub)
        mixed = _dot(a_ref[rows, :], wa_ref[...]) + _dot(r_ref[rows, :], wr_ref[...])
        x1 = x_ref[rows, :] + mixed
        x1_ref[rows, :] = x1
        h2 = _norm2(x1, g_ref[...])
        h2_ref[rows, :] = h2
        hi = h2.astype(jnp.bfloat16)
        lo = (h2 - hi.astype(jnp.float32)).astype(jnp.bfloat16)
        both = _dot(hi, rcat_ref[...])
        logits = both[:, :LANES] + (both[:, LANES:] + _dot(lo, rcat_ref[:, :LANES]))
        route_ref[rows, :] = _routing(logits)


def _out_proj(attn2d, ret2d, x2d, w_out_bf, g2, r_cat, *, tm):
    n, d = x2d.shape
    return pl.pallas_call(
        _out_proj_kernel,
        out_shape=(
            jax.ShapeDtypeStruct((n, d), jnp.float32),
            jax.ShapeDtypeStruct((n, d), jnp.float32),
            jax.ShapeDtypeStruct((n, LANES), jnp.float32),
        ),
        grid=(n // tm,),
        in_specs=[
            pl.BlockSpec((tm, ATTN_WIDTH), lambda i: (i, 0)),
            pl.BlockSpec((tm, RET_WIDTH), lambda i: (i, 0)),
            pl.BlockSpec((tm, d), lambda i: (i, 0)),
            pl.BlockSpec((ATTN_WIDTH, d), lambda i: (0, 0)),
            pl.BlockSpec((RET_WIDTH, d), lambda i: (ATTN_WIDTH // RET_WIDTH, 0)),
            pl.BlockSpec((1, d), lambda i: (0, 0)),
            pl.BlockSpec((d, 2 * LANES), lambda i: (0, 0)),
        ],
        out_specs=(
            pl.BlockSpec((tm, d), lambda i: (i, 0)),
            pl.BlockSpec((tm, d), lambda i: (i, 0)),
            pl.BlockSpec((tm, LANES), lambda i: (i, 0)),
        ),
        compiler_params=_params("arbitrary"),
        name="out_proj",
    )(attn2d, ret2d, x2d, w_out_bf, w_out_bf, g2, r_cat)


def _plan_kernel(route_ref, pos_ref, cnt_ref, rank_scr, *, tm, blk):
    n = route_ref.shape[0]
    lane = lax.broadcasted_iota(jnp.int32, (blk, LANES), 1).astype(jnp.float32)
    before = (lax.broadcasted_iota(jnp.int32, (blk, blk), 1)
              < lax.broadcasted_iota(jnp.int32, (blk, blk), 0)).astype(jnp.bfloat16)

    def one_hot(rows):
        r = route_ref[rows, :]
        e1 = r[:, ROUTE_E1:ROUTE_E1 + 1]
        e2 = r[:, ROUTE_E2:ROUTE_E2 + 1]
        return lane == e1, lane == e2

    def rank_body(b, run):
        rows = pl.ds(pl.multiple_of(b * blk, blk), blk)
        m1, m2 = one_hot(rows)
        sel = jnp.where(m1 | m2, 1.0, 0.0)
        rank_scr[rows, :] = _dot(before, sel.astype(jnp.bfloat16)) + run
        return run + jnp.sum(sel, axis=0, keepdims=True)
    cnt = lax.fori_loop(0, n // blk, rank_body, jnp.zeros((1, LANES), jnp.float32), unroll=2)
    cnt_ref[...] = jnp.broadcast_to(cnt, cnt_ref.shape)

    tiles = jnp.floor((cnt + (tm - 1.0)) * (1.0 / tm))
    below = (lax.broadcasted_iota(jnp.int32, (LANES, LANES), 0)
             < lax.broadcasted_iota(jnp.int32, (LANES, LANES), 1)).astype(jnp.bfloat16)
    start = _dot(jnp.broadcast_to(tiles, (8, LANES)).astype(jnp.bfloat16), below)[0:1, :] * float(tm)

    def pos_body(b, carry):
        rows = pl.ds(pl.multiple_of(b * blk, blk), blk)
        m1, m2 = one_hot(rows)
        dest = rank_scr[rows, :] + start
        p1 = jnp.sum(jnp.where(m1, dest, 0.0), axis=-1, keepdims=True)
        p2 = jnp.sum(jnp.where(m2, dest, 0.0), axis=-1, keepdims=True)
        pos_ref[rows, :] = (jnp.where(lane == 0.0, p1, 0.0) + jnp.where(lane == 1.0, p2, 0.0)).astype(jnp.int32)
        return carry
    lax.fori_loop(0, n // blk, pos_body, 0, unroll=2)


def _moe_plan(route, *, tm, blk):
    n = route.shape[0]
    return pl.pallas_call(
        functools.partial(_plan_kernel, tm=tm, blk=blk),
        out_shape=(jax.ShapeDtypeStruct((n, LANES), jnp.int32),
                   jax.ShapeDtypeStruct((8, LANES), jnp.float32)),
        scratch_shapes=[pltpu.VMEM((n, LANES), jnp.float32)],
        compiler_params=pltpu.CompilerParams(vmem_limit_bytes=VMEM_LIMIT),
        name="moe_plan",
    )(route)


def _row_copy(src, src_row, dst, dst_row, sem):
    return pltpu.make_async_copy(src.at[pl.ds(src_row, 1), :], dst.at[pl.ds(dst_row, 1), :], sem)


def _invert_kernel(pos_ref, fill_ref, sid_ref, sem):
    fill = pltpu.make_async_copy(fill_ref, sid_ref, sem)
    fill.start()
    fill.wait()

    def body(j, carry):
        sid_ref[pos_ref[j]] = j
        return carry
    lax.fori_loop(0, pos_ref.shape[0], body, 0, unroll=8)


def _moe_invert(pos, *, rows):
    return pl.pallas_call(
        _invert_kernel,
        out_shape=jax.ShapeDtypeStruct((rows,), jnp.int32),
        in_specs=[pl.BlockSpec(memory_space=pltpu.SMEM), pl.BlockSpec(memory_space=pl.ANY)],
        out_specs=pl.BlockSpec(memory_space=pltpu.SMEM),
        scratch_shapes=[pltpu.SemaphoreType.DMA(())],
        name="moe_invert",
    )(pos, jnp.zeros((rows,), jnp.int32))


ROW_GROUP = 8
ROW_BLOCK = 32
FFN_ROW_STEP = 128
NO_NEXT, NOT_FIRST = -1, -2


def _ffn_kernel(te_ref, tv_ref, nu_ref, seg_ref, nexte_ref, sid_ref, tok_ref,
                h2_ref, w1_ref, w3_ref, w2_ref, y2_ref,
                xbuf, ybuf, w1buf, w3buf, w2buf, gsem, ssem, wsem):
    t = pl.program_id(0)
    n_used = nu_ref[0]
    tm = xbuf.shape[1]

    def weight_copies(e, wslot):
        return [pltpu.make_async_copy(src.at[e], dst.at[wslot], wsem.at[wslot])
                for src, dst in ((w1_ref, w1buf), (w3_ref, w3buf), (w2_ref, w2buf))]

    def gather(slot):
        return tok_ref, lambda r, tok: _row_copy(h2_ref, tok, xbuf.at[slot], r, gsem.at[slot])

    def scatter(slot):
        return sid_ref, lambda r, sid: _row_copy(ybuf.at[slot], r, y2_ref, sid, ssem.at[slot])

    def start_all(tile, table, copy, queues=1):
        valid = tv_ref[tile]
        for blk in range(tm // ROW_BLOCK):
            @pl.when(valid >= (blk + 1) * ROW_BLOCK)
            def _():
                for r in range(blk * ROW_BLOCK, (blk + 1) * ROW_BLOCK):
                    copy(r, table[tile * tm + r]).start(priority=r % queues)
        done = (valid // ROW_BLOCK) * ROW_BLOCK

        def body(c, carry):
            for u in range(ROW_GROUP):
                r = done + c * ROW_GROUP + u

                @pl.when(r < valid)
                def _():
                    copy(r, table[tile * tm + r]).start(priority=u % queues)
            return carry
        lax.fori_loop(0, (valid - done + (ROW_GROUP - 1)) // ROW_GROUP, body, 0)

    def wait_all(tile, copy, block_copy):
        valid = tv_ref[tile]
        for blk in range(tm // ROW_BLOCK):
            @pl.when(valid >= (blk + 1) * ROW_BLOCK)
            def _():
                block_copy.wait()

        def body(r, carry):
            copy(0, 0).wait()
            return carry
        lax.fori_loop(0, valid % ROW_BLOCK, body, 0)

    def wait_gather(tile, slot):
        wait_all(tile, gather(slot)[1],
                 pltpu.make_async_copy(h2_ref.at[pl.ds(0, ROW_BLOCK), :],
                                       xbuf.at[slot, pl.ds(0, ROW_BLOCK), :], gsem.at[slot]))

    def wait_scatter(tile, slot):
        wait_all(tile, scatter(slot)[1],
                 pltpu.make_async_copy(ybuf.at[slot, pl.ds(0, ROW_BLOCK), :],
                                       y2_ref.at[pl.ds(0, ROW_BLOCK), :], ssem.at[slot]))

    @pl.when(t == 0)
    def _():
        for cp in weight_copies(te_ref[0], 0):
            cp.start()
        start_all(0, *gather(0))

    @pl.when(t < n_used)
    def _():
        slot = t % 2
        wslot = seg_ref[t] % 2

        @pl.when(t + 1 < n_used)
        def _():
            start_all(t + 1, *gather(1 - slot))

        @pl.when(nexte_ref[t] != NOT_FIRST)
        def _():
            for cp in weight_copies(te_ref[t], wslot):
                cp.wait()

            @pl.when(nexte_ref[t] >= 0)
            def _():
                for cp in weight_copies(nexte_ref[t], 1 - wslot):
                    cp.start()

        wait_gather(t, slot)

        @pl.when(t >= 2)
        def _():
            wait_scatter(t - 2, slot)

        valid = tv_ref[t]
        for live in range(FFN_ROW_STEP, tm + 1, FFN_ROW_STEP):
            @pl.when((valid > live - FFN_ROW_STEP) & (valid <= live))
            def _():
                row = lax.broadcasted_iota(jnp.int32, (live, xbuf.shape[2]), 0)
                x = jnp.where(row < valid, xbuf[slot, :live, :], 0.0).astype(jnp.bfloat16)
                a = _dot(x, w1buf[wslot].astype(jnp.bfloat16))
                b = _dot(x, w3buf[wslot].astype(jnp.bfloat16))
                act = (a * (1.0 / (1.0 + jnp.exp(-a)))) * b
                ybuf[slot, :live, :] = _dot(act.astype(jnp.bfloat16), w2buf[wslot].astype(jnp.bfloat16))
        start_all(t, *scatter(slot), queues=2)

    @pl.when(t == pl.num_programs(0) - 1)
    def _():
        @pl.when(n_used >= 2)
        def _():
            wait_scatter(n_used - 2, n_used % 2)
        wait_scatter(n_used - 1, (n_used - 1) % 2)


def _moe_ffn(tile_expert, tile_valid, n_used, tile_seg, tile_next, row_sid, row_tok, h2, w1, w3, w2,
             *, tm):
    n, d = h2.shape
    _, _, ff = w1.shape
    max_tiles = tile_expert.shape[0]
    any_spec = pl.BlockSpec(memory_space=pl.ANY)
    return pl.pallas_call(
        _ffn_kernel,
        out_shape=jax.ShapeDtypeStruct((2 * n, d), jnp.float32),
        grid_spec=pltpu.PrefetchScalarGridSpec(
            num_scalar_prefetch=7,
            grid=(max_tiles,),
            in_specs=[any_spec, any_spec, any_spec, any_spec],
            out_specs=any_spec,
            scratch_shapes=[
                pltpu.VMEM((2, tm, d), jnp.float32),
                pltpu.VMEM((2, tm, d), jnp.float32),
                pltpu.VMEM((2, d, ff), jnp.float32),
                pltpu.VMEM((2, d, ff), jnp.float32),
                pltpu.VMEM((2, ff, d), jnp.float32),
                pltpu.SemaphoreType.DMA((2,)),
                pltpu.SemaphoreType.DMA((2,)),
                pltpu.SemaphoreType.DMA((2,)),
            ],
        ),
        compiler_params=_params("arbitrary"),
        name="moe_ffn",
    )(tile_expert, tile_valid, n_used, tile_seg, tile_next, row_sid, row_tok, h2, w1, w3, w2)


def _combine_kernel(x1_ref, route_ref, ya_ref, yb_ref, o_ref):
    route = route_ref[...]
    w1 = route[:, ROUTE_W1:ROUTE_W1 + 1]
    w2 = route[:, ROUTE_W2:ROUTE_W2 + 1]
    o_ref[...] = x1_ref[...] + (w1 * ya_ref[...] + w2 * yb_ref[...])


def _moe_combine(x1, route, y2, *, tm):
    n, d = x1.shape
    return pl.pallas_call(
        _combine_kernel,
        out_shape=jax.ShapeDtypeStruct((n, d), jnp.float32),
        grid=(n // tm,),
        in_specs=[pl.BlockSpec((tm, d), lambda i: (i, 0)),
                  pl.BlockSpec((tm, LANES), lambda i: (i, 0)),
                  pl.BlockSpec((tm, d), lambda i: (i, 0)),
                  pl.BlockSpec((tm, d), lambda i: (i + n // tm, 0))],
        out_specs=pl.BlockSpec((tm, d), lambda i: (i, 0)),
        compiler_params=_params("arbitrary"),
        name="moe_combine",
    )(x1, route, y2, y2)


def _moe(x1, h2, route, w1, w3, w2, *, tm, gather_tm):
    n, d = x1.shape
    max_tiles = (2 * n) // tm + N_EXPERTS
    pos2d, cnt = _moe_plan(route, tm=tm, blk=gather_tm)
    pos = pos2d[:, :2].T.reshape(2 * n)
    counts = cnt[0, :N_EXPERTS].astype(jnp.int32)
    tiles = (counts + (tm - 1)) // tm
    ends = jnp.cumsum(tiles)
    t_idx = jnp.arange(max_tiles, dtype=jnp.int32)
    tile_expert = jnp.sum((ends[None, :] <= t_idx[:, None]).astype(jnp.int32), axis=1)
    tile_expert = jnp.minimum(tile_expert, N_EXPERTS - 1)
    first_tile = (ends - tiles)[tile_expert]
    tile_valid = jnp.clip(counts[tile_expert] - (t_idx - first_tile) * tm, 0, tm).astype(jnp.int32)
    n_used = ends[-1:].astype(jnp.int32)
    used = t_idx < n_used[0]
    is_first = used & (t_idx == first_tile)
    tile_seg = (jnp.cumsum(is_first.astype(jnp.int32)) - 1).astype(jnp.int32)
    next_start = first_tile + tiles[tile_expert]
    next_e = jnp.where(next_start < n_used[0], tile_expert[jnp.minimum(next_start, max_tiles - 1)], NO_NEXT)
    tile_next = jnp.where(is_first, next_e, NOT_FIRST).astype(jnp.int32)

    row_sid = _moe_invert(pos, rows=max_tiles * tm)
    row_tok = jnp.where(row_sid >= n, row_sid - n, row_sid)
    y2 = _moe_ffn(tile_expert, tile_valid, n_used, tile_seg, tile_next, row_sid, row_tok, h2, w1, w3, w2,
                  tm=tm)
    return _moe_combine(x1, route, y2, tm=gather_tm)


def _pack_router(w_group, w_router):
    d = w_group.shape[0]
    experts = jnp.transpose(w_router, (1, 0, 2)).reshape(d, N_EXPERTS)
    wr = jnp.concatenate(
        [experts, w_group, jnp.zeros((d, LANES - N_EXPERTS - N_GROUPS), w_group.dtype)], axis=1)
    hi = wr.astype(jnp.bfloat16)
    lo = (wr - hi.astype(jnp.float32)).astype(jnp.bfloat16)
    return jnp.concatenate([hi, lo], axis=1)


def _rotation_tables(seq):
    half = RET_DK // 2
    pos = jnp.arange(seq, dtype=jnp.float32)
    inv = 1.0 / (ROT_BASE ** jnp.linspace(0.0, 1.0, half, dtype=jnp.float32))
    ang = pos[:, None] * inv[None, :]
    c, s = jnp.cos(ang), jnp.sin(ang)
    return jnp.concatenate([c, c], axis=-1), jnp.concatenate([-s, s], axis=-1)


def _tiles(n, seq):
    def fit(total, want):
        t = min(total, want)
        while total % t:
            t //= 2
        return t
    return dict(
        proj_tm=fit(n, 1024),
        attn_tq=fit(seq, 256), attn_tk=fit(seq, 256),
        ret_rc=fit(seq, 256),
        out_tm=fit(n, 512),
        moe_tm=fit(n, 512), moe_gather_tm=fit(n, 256),
    )


def kernel(x, norm1_g, w_in, q_norm_g, k_norm_g, idx_k_ln_w, idx_k_ln_b, ret_norm_g,
           w_out, norm2_g, w_group, w_router, w1, w3, w2):
    b, seq, d = x.shape
    n = b * seq
    depth = w_in.shape[0]
    t = _tiles(n, seq)
    cos2, sin2 = _rotation_tables(seq)
    log_gamma = jnp.log1p(-jnp.exp2(-5.0 - jnp.arange(RET_HEADS, dtype=jnp.float32)))

    x2d = x.reshape(n, d)
    for l in range(depth):
        proj = _in_proj(x2d, norm1_g[l][None, :], w_in[l].T, tm=t["proj_tm"])
        p3 = proj.reshape(b, seq, proj.shape[1])
        attn = _dsa_attention(p3, q_norm_g[l][None, :], k_norm_g[l][None, :],
                              idx_k_ln_w[l][None, :], idx_k_ln_b[l][None, :],
                              tq=t["attn_tq"], tk=t["attn_tk"])
        ret = _retention(p3, log_gamma, cos2, sin2, ret_norm_g[l].reshape(RET_HEADS, 1, RET_DV),
                         rc=t["ret_rc"])
        r_cat = _pack_router(w_group[l], w_router[l])
        g2 = norm2_g[l][None, :]
        x1, h2, route = _out_proj(attn.reshape(n, ATTN_WIDTH), ret.reshape(n, RET_WIDTH), x2d,
                                  w_out[l].astype(jnp.bfloat16), g2, r_cat, tm=t["out_tm"])
        x2d = _moe(x1, h2, route, w1[l], w3[l], w2[l], tm=t["moe_tm"], gather_tm=t["moe_gather_tm"])
    return x2d.reshape(b, seq, d)
```

```python
import functools
import math

import jax
import jax.numpy as jnp
from jax import lax
from jax.experimental import pallas as pl
from jax.experimental.pallas import tpu as pltpu

CHUNK = 64
ATTN_HEADS = 8
HEAD_DIM = 128
KV_HEADS = 2
HEADS_PER_KV = ATTN_HEADS // KV_HEADS
IDX_HEADS = 16
IDX_DIM = 64
TOPK_MAX = 256
RET_HEADS = 8
RET_DK = 128
RET_DV = 128
ROT_BASE = 10000.0
N_GROUPS = 4
EXPERTS_PER_GROUP = 8
N_EXPERTS = N_GROUPS * EXPERTS_PER_GROUP
EPS = 1e-6

ATTN_WIDTH = ATTN_HEADS * HEAD_DIM
KV_WIDTH = KV_HEADS * HEAD_DIM
IDX_WIDTH = IDX_HEADS * IDX_DIM
RET_WIDTH = RET_HEADS * RET_DK

LANES = 128
SUBLANES = 8
VMEM_LIMIT = 56 * 1024 * 1024

AQ_OFF = 0
AK_OFF = AQ_OFF + ATTN_WIDTH
AV_OFF = AK_OFF + KV_WIDTH
IQ_OFF = AV_OFF + KV_WIDTH
IK_OFF = IQ_OFF + IDX_WIDTH
IW_OFF = IK_OFF + IDX_DIM
W_RET = IW_OFF + IDX_HEADS
IN_WIDTH = W_RET + 4 * RET_WIDTH
assert IW_OFF // LANES == IK_OFF // LANES
PROJ_TN = 1024
RQ_OFF = -(-W_RET // PROJ_TN) * PROJ_TN
RK_OFF = RQ_OFF + RET_WIDTH
RV_OFF = RK_OFF + RET_WIDTH
RG_OFF = RV_OFF + RET_WIDTH
PROJ_WIDTH = RG_OFF + RET_WIDTH

ROUTE_E1, ROUTE_E2, ROUTE_W1, ROUTE_W2 = 0, 1, 2, 3

SUM_ROWS = 16
LOGIT_BOUND_SLACK = 1.05
MAX_SINGLE_SWEEP_BOUND = 50.0

NEG_BIG = -1e30

_NT = (((1,), (1,)), ((), ()))


def _dot(a, b):
    return jnp.dot(a, b, preferred_element_type=jnp.float32)


def _dot_nt(a, b):
    return lax.dot_general(a, b, _NT, preferred_element_type=jnp.float32)


def _params(*sem):
    return pltpu.CompilerParams(dimension_semantics=sem, vmem_limit_bytes=VMEM_LIMIT)


def _in_proj_kernel(x_ref, g_ref, wt_ref, o_ref, h_scr, *, row_chunk, partial_tile, partial_cols):
    j = pl.program_id(1)
    first = j == 0
    partial = j == partial_tile

    @pl.when(first)
    def _():
        w = wt_ref[...].astype(jnp.bfloat16)
        for c in range(x_ref.shape[0] // row_chunk):
            rows = slice(c * row_chunk, (c + 1) * row_chunk)
            x = x_ref[rows, :]
            ms = jnp.mean(x * x, axis=-1, keepdims=True)
            h = ((x * lax.rsqrt(ms + EPS)) * g_ref[...]).astype(jnp.bfloat16)
            h_scr[rows, :] = h
            o_ref[rows, :] = _dot_nt(h, w).astype(o_ref.dtype)

    @pl.when(partial)
    def _():
        w = wt_ref[:partial_cols, :].astype(jnp.bfloat16)
        o_ref[:, :partial_cols] = _dot_nt(h_scr[...], w).astype(o_ref.dtype)
        o_ref[:, partial_cols:] = jnp.zeros((o_ref.shape[0], o_ref.shape[1] - partial_cols), o_ref.dtype)

    @pl.when(jnp.logical_not(first | partial))
    def _():
        o_ref[...] = _dot_nt(h_scr[...], wt_ref[...].astype(jnp.bfloat16)).astype(o_ref.dtype)


def _in_proj(x2d, g, w_in_t, *, tm):
    n, d = x2d.shape
    tn = PROJ_TN
    assert w_in_t.shape == (IN_WIDTH, d) and W_RET % SUBLANES == 0 and tn % SUBLANES == 0
    attn_tiles = RQ_OFF // tn
    partial_tile = attn_tiles - 1
    partial_cols = -(-(W_RET - partial_tile * tn) // LANES) * LANES
    assert 0 < partial_tile and 0 < partial_cols < tn

    def window(i, j):
        step = tn // SUBLANES
        start = jnp.where(j < attn_tiles, j * step, W_RET // SUBLANES + (j - attn_tiles) * step)
        return SUBLANES * start, 0

    return pl.pallas_call(
        functools.partial(_in_proj_kernel, row_chunk=min(tm, 256), partial_tile=partial_tile,
                          partial_cols=partial_cols),
        out_shape=jax.ShapeDtypeStruct((n, PROJ_WIDTH), jnp.bfloat16),
        grid=(n // tm, PROJ_WIDTH // tn),
        in_specs=[
            pl.BlockSpec((tm, d), lambda i, j: (i, 0)),
            pl.BlockSpec((1, d), lambda i, j: (0, 0)),
            pl.BlockSpec((pl.Element(tn), pl.Element(d)), window),
        ],
        out_specs=pl.BlockSpec((tm, tn), lambda i, j: (i, j)),
        scratch_shapes=[pltpu.VMEM((tm, d), jnp.bfloat16)],
        compiler_params=_params("arbitrary", "arbitrary"),
        name="in_proj",
    )(x2d, g, w_in_t)


def _ordered_float(v):
    bits = v ^ ((v >> 31) & jnp.int32(0x7FFFFFFF))
    return pltpu.bitcast(bits, jnp.float32)


def _ordered_bfloat(v16):
    bits16 = v16 ^ ((v16 >> 15) & jnp.int32(0x7FFF))
    return pltpu.bitcast(bits16 << 16, jnp.float32).astype(jnp.bfloat16)


def _attn_kernel(aq_ref, iqa_ref, iqb_ref, iw_ref, ak_ref, av_ref, ik_ref, qg_ref, kg_ref, lnw_ref,
                 lnb_ref, o_ref,
                 kn_scr, ikn_scr, vt_scr, key_scr, kb_scr, wt_scr, qn_scr, acc_scr, s_scr, kmax_scr,
                 *, tk, topk, idx_w_scale):
    i = pl.program_id(1)
    seq = ak_ref.shape[1]
    tq = aq_ref.shape[1]
    chunk_shift = CHUNK.bit_length() - 1

    @pl.when(i == 0)
    def _():
        def body(c, carry):
            rows = pl.ds(pl.multiple_of(c * tk, tk), tk)
            for g in range(KV_HEADS):
                cols = slice(g * HEAD_DIM, (g + 1) * HEAD_DIM)
                k = ak_ref[0, rows, cols].astype(jnp.float32)
                ms = jnp.mean(k * k, axis=-1, keepdims=True)
                kn = (k * lax.rsqrt(ms + EPS)) * kg_ref[...]
                kn_scr[rows, cols] = kn.astype(jnp.bfloat16)
                ksq = jnp.max(jnp.sum(kn * kn, axis=-1, keepdims=True), axis=0, keepdims=True)
                prev = jnp.where(c == 0, 0.0, kmax_scr[g])
                kmax_scr[g] = jnp.maximum(prev, jnp.broadcast_to(ksq, kmax_scr.shape[1:]))
                v = av_ref[0, rows, cols].astype(jnp.float32)
                vt_scr[g, c, :HEAD_DIM, :] = v.T.astype(jnp.bfloat16)
                vt_scr[g, c, HEAD_DIM:, :] = jnp.ones((SUM_ROWS, tk), jnp.bfloat16)
            ki = ik_ref[0, rows, :IDX_DIM].astype(jnp.float32)
            mu = jnp.mean(ki, axis=-1, keepdims=True)
            var = jnp.mean(jnp.square(ki - mu), axis=-1, keepdims=True)
            y = ((ki - mu) * lax.rsqrt(var + EPS) * lnw_ref[...] + lnb_ref[...]).astype(jnp.bfloat16)
            zeros = jnp.zeros_like(y)
            ikn_scr[0, rows, :] = jnp.concatenate([y, zeros], axis=1)
            ikn_scr[1, rows, :] = jnp.concatenate([zeros, y], axis=1)
            return carry
        lax.fori_loop(0, seq // tk, body, 0)

    t0 = i * tq
    n_kt = (t0 + tq) // tk
    scale = (HEAD_DIM ** -0.5) * math.log2(math.e)
    for h in range(ATTN_HEADS):
        g, r = divmod(h, HEADS_PER_KV)
        q = aq_ref[0, :, h * HEAD_DIM:(h + 1) * HEAD_DIM].astype(jnp.float32)
        ms = jnp.mean(q * q, axis=-1, keepdims=True)
        qn_scr[g, r * tq:(r + 1) * tq, :] = (
            (q * lax.rsqrt(ms + EPS)) * qg_ref[...] * scale).astype(jnp.bfloat16)
    ones_rows = jnp.ones((8, HEAD_DIM), jnp.bfloat16)
    bound = []
    for g in range(KV_HEADS):
        qf = qn_scr[g].astype(jnp.float32)
        qsq = _dot_nt(ones_rows, (qf * qf).astype(jnp.bfloat16))[0:1, :]
        kmax = jnp.concatenate([kmax_scr[g, 0:1, :]] * (HEADS_PER_KV * tq // LANES), axis=1)
        bound.append(LOGIT_BOUND_SLACK * jnp.sqrt(qsq * kmax))
    wt_scr[...] = iw_ref[0].astype(jnp.float32).T * idx_w_scale
    w_row = IW_OFF % LANES

    q_chunk = (t0 + lax.broadcasted_iota(jnp.int32, (tk, tq), 1)) >> chunk_shift

    def score_body(kt, carry):
        rows = pl.ds(pl.multiple_of(kt * tk, tk), tk)
        ik_first, ik_second = ikn_scr[0, rows, :], ikn_scr[1, rows, :]
        acc = jnp.zeros((tk, tq), jnp.float32)
        pairs_per_ref = iqa_ref.shape[2] // LANES
        for pair in range(IDX_HEADS // 2):
            src = iqa_ref if pair < pairs_per_ref else iqb_ref
            lane0 = (pair % pairs_per_ref) * LANES
            q_pair = src[0, :, lane0:lane0 + LANES]
            for sub, ik_t in enumerate((ik_first, ik_second)):
                h = 2 * pair + sub
                d = _dot_nt(ik_t, q_pair)
                acc = acc + jnp.maximum(d, 0.0) * wt_scr[w_row + h:w_row + h + 1, :]
        k_chunk = (kt * tk + lax.broadcasted_iota(jnp.int32, (tk, tq), 0)) >> chunk_shift
        score = jnp.where(k_chunk <= q_chunk, acc, -jnp.inf)
        key_scr[rows, :] = score
        kb_scr[rows, :] = score.astype(jnp.bfloat16)
        return carry
    lax.fori_loop(0, n_kt, score_body, 0)

    def tree_sum(hit):
        while hit.shape[0] > 1:
            half = hit.shape[0] // 2
            hit = hit[:half] + hit[half:]
        return hit[0]

    packed = 16
    searching = (t0 + tq > topk).astype(jnp.int32)

    def coarse_body(it, lo16):
        cand16 = lo16 + lax.shift_left(jnp.int32(1), 15 - it)
        cand_b = _ordered_bfloat(cand16)

        def count_body(kt, part):
            rows = pl.ds(pl.multiple_of(kt * tk, tk), tk)
            hit = jnp.where(kb_scr[rows, :] >= cand_b, jnp.bfloat16(1), jnp.bfloat16(0))
            return part + tree_sum(hit.reshape(tk // packed, packed, tq))
        part = lax.fori_loop(0, n_kt, count_body, jnp.zeros((packed, tq), jnp.bfloat16))
        cnt = jnp.sum(part.astype(jnp.float32), axis=0, keepdims=True)
        return jnp.where(cnt >= float(topk), cand16, lo16)
    lo16 = lax.fori_loop(0, 16 * searching, coarse_body, jnp.full((1, tq), -(2 ** 15), jnp.int32))

    fine_bits = 18
    lo0 = jnp.maximum(lo16 - 1, -(2 ** 15)) << 16

    def fine_body(it, lo):
        cand = lo + lax.shift_left(jnp.int32(1), fine_bits - 1 - it)
        cand_f = _ordered_float(cand)

        def count_body(kt, part):
            rows = pl.ds(pl.multiple_of(kt * tk, tk), tk)
            hit = jnp.where(key_scr[rows, :] >= cand_f, 1.0, 0.0)
            return part + tree_sum(hit.reshape(tk // SUBLANES, SUBLANES, tq))
        part = lax.fori_loop(0, n_kt, count_body, jnp.zeros((SUBLANES, tq), jnp.float32))
        cnt = jnp.sum(part, axis=0, keepdims=True)
        return jnp.where(cnt >= float(topk), cand, lo)
    lo = lax.fori_loop(0, fine_bits * searching, fine_body, lo0)
    thr = jnp.where(lo16 == -(2 ** 15), jnp.finfo(jnp.float32).min, _ordered_float(lo))

    acc_scr[...] = jnp.zeros(acc_scr.shape, jnp.float32)

    def masked_logits(kt):
        rows = pl.ds(pl.multiple_of(kt * tk, tk), tk)
        bias = jnp.where(key_scr[rows, :] >= thr, 0.0, NEG_BIG)
        bias = jnp.concatenate([bias] * HEADS_PER_KV, axis=1)
        return [_dot_nt(kn_scr[rows, g * HEAD_DIM:(g + 1) * HEAD_DIM], qn_scr[g]) + bias
                for g in range(KV_HEADS)]

    bound_max = jnp.max(jnp.maximum(bound[0], bound[1]))
    single_sweep = bound_max <= MAX_SINGLE_SWEEP_BOUND

    @pl.when(single_sweep)
    def _():
        def body(kt, carry):
            for g, s in enumerate(masked_logits(kt)):
                acc_scr[g] += _dot(vt_scr[g, kt], jnp.exp2(s - bound[g]).astype(jnp.bfloat16))
            return carry
        lax.fori_loop(0, n_kt, body, 0)

    @pl.when(jnp.logical_not(single_sweep))
    def _():
        def logit_body(kt, m):
            rows = pl.ds(pl.multiple_of(kt * tk, tk), tk)
            new_m = []
            for g, s in enumerate(masked_logits(kt)):
                s_scr[g, rows, :] = s
                new_m.append(jnp.maximum(m[g], jnp.max(s, axis=0, keepdims=True)))
            return tuple(new_m)
        m0 = jnp.full((1, HEADS_PER_KV * tq), NEG_BIG, jnp.float32)
        m = lax.fori_loop(0, n_kt, logit_body, (m0,) * KV_HEADS)

        def pv_body(kt, carry):
            rows = pl.ds(pl.multiple_of(kt * tk, tk), tk)
            for g in range(KV_HEADS):
                p = jnp.exp2(s_scr[g, rows, :] - m[g]).astype(jnp.bfloat16)
                acc_scr[g] += _dot(vt_scr[g, kt], p)
            return carry
        lax.fori_loop(0, n_kt, pv_body, 0)

    for h in range(ATTN_HEADS):
        g, r = divmod(h, HEADS_PER_KV)
        cols = slice(r * tq, (r + 1) * tq)
        o = acc_scr[g, :HEAD_DIM, cols] / acc_scr[g, HEAD_DIM:HEAD_DIM + 1, cols]
        o_ref[0, :, h * HEAD_DIM:(h + 1) * HEAD_DIM] = o.T.astype(o_ref.dtype)


def _dsa_attention(p3, q_g, k_g, ln_w, ln_b, *, tq, tk):
    b, seq, _ = p3.shape
    topk = min(TOPK_MAX, seq // 4)
    idx_w_scale = (IDX_HEADS ** -0.5) * (IDX_DIM ** -0.5)
    assert seq % tq == 0 and tq % tk == 0 and tk % CHUNK == 0

    def col(off, width):
        assert off % width == 0 or width == LANES
        return off // width

    half_iq = IDX_WIDTH // 2
    return pl.pallas_call(
        functools.partial(_attn_kernel, tk=tk, topk=topk, idx_w_scale=idx_w_scale),
        out_shape=jax.ShapeDtypeStruct((b, seq, ATTN_WIDTH), jnp.bfloat16),
        grid=(b, seq // tq),
        in_specs=[
            pl.BlockSpec((1, tq, ATTN_WIDTH), lambda bi, i: (bi, i, col(AQ_OFF, ATTN_WIDTH))),
            pl.BlockSpec((1, tq, half_iq), lambda bi, i: (bi, i, col(IQ_OFF, half_iq))),
            pl.BlockSpec((1, tq, half_iq), lambda bi, i: (bi, i, col(IQ_OFF, half_iq) + 1)),
            pl.BlockSpec((1, tq, LANES), lambda bi, i: (bi, i, col(IW_OFF, LANES))),
            pl.BlockSpec((1, seq, KV_WIDTH), lambda bi, i: (bi, 0, col(AK_OFF, KV_WIDTH))),
            pl.BlockSpec((1, seq, KV_WIDTH), lambda bi, i: (bi, 0, col(AV_OFF, KV_WIDTH))),
            pl.BlockSpec((1, seq, LANES), lambda bi, i: (bi, 0, col(IK_OFF, LANES))),
            pl.BlockSpec((1, HEAD_DIM), lambda bi, i: (0, 0)),
            pl.BlockSpec((1, HEAD_DIM), lambda bi, i: (0, 0)),
            pl.BlockSpec((1, IDX_DIM), lambda bi, i: (0, 0)),
            pl.BlockSpec((1, IDX_DIM), lambda bi, i: (0, 0)),
        ],
        out_specs=pl.BlockSpec((1, tq, ATTN_WIDTH), lambda bi, i: (bi, i, 0)),
        scratch_shapes=[
            pltpu.VMEM((seq, KV_WIDTH), jnp.bfloat16),
            pltpu.VMEM((2, seq, 2 * IDX_DIM), jnp.bfloat16),
            pltpu.VMEM((KV_HEADS, seq // tk, HEAD_DIM + SUM_ROWS, tk), jnp.bfloat16),
            pltpu.VMEM((seq, tq), jnp.float32),
            pltpu.VMEM((seq, tq), jnp.bfloat16),
            pltpu.VMEM((LANES, tq), jnp.float32),
            pltpu.VMEM((KV_HEADS, HEADS_PER_KV * tq, HEAD_DIM), jnp.bfloat16),
            pltpu.VMEM((KV_HEADS, HEAD_DIM + SUM_ROWS, HEADS_PER_KV * tq), jnp.float32),
            pltpu.VMEM((KV_HEADS, seq, HEADS_PER_KV * tq), jnp.float32),
            pltpu.VMEM((KV_HEADS, 8, LANES), jnp.float32),
        ],
        compiler_params=_params("arbitrary", "arbitrary"),
        name="dsa_attn",
    )(p3, p3, p3, p3, p3, p3, p3, q_g, k_g, ln_w, ln_b)


RET_HEADS_PER_STEP = 2


def _ret_kernel(lg_ref, rq_ref, rk_ref, rv_ref, rg_ref, cos_ref, sin_ref, g_ref, o_ref, *, rc):
    seq = rq_ref.shape[1]
    n = lax.broadcasted_iota(jnp.int32, (rc, RET_DV), 0).astype(jnp.float32)
    rel = (lax.broadcasted_iota(jnp.int32, (rc, rc), 0)
           - lax.broadcasted_iota(jnp.int32, (rc, rc), 1)).astype(jnp.float32)

    def rot(x, rows):
        return x * cos_ref[rows, :] + pltpu.roll(x, RET_DK // 2, 1) * sin_ref[rows, :]

    heads = []
    for hh in range(RET_HEADS_PER_STEP):
        lg = lg_ref[pl.program_id(1) * RET_HEADS_PER_STEP + hh]
        heads.append(dict(
            cols=slice(hh * RET_DK, (hh + 1) * RET_DK),
            cross_decay=jnp.exp(lg * (n + 1.0)),
            state_decay=jnp.exp(lg * (rc - 1.0 - n)),
            chunk_decay=jnp.exp(lg * jnp.full((RET_DK, RET_DV), float(rc), jnp.float32)),
            intra=jnp.where(rel >= 0, jnp.exp(lg * jnp.maximum(rel, 0.0)), 0.0),
            state=jnp.zeros((RET_DK, RET_DV), jnp.float32),
            gain=g_ref[hh],
        ))

    for c in range(seq // rc):
        rows = slice(c * rc, (c + 1) * rc)
        for hd in heads:
            cols = hd["cols"]
            q = rot(rq_ref[0, rows, cols].astype(jnp.float32), rows)
            k = rot(rk_ref[0, rows, cols].astype(jnp.float32), rows) * (RET_DK ** -0.5)
            v = rv_ref[0, rows, cols]
            qb = q.astype(jnp.bfloat16)
            inner = _dot_nt(qb, k.astype(jnp.bfloat16)) * hd["intra"]
            o = (_dot(inner.astype(jnp.bfloat16), v)
                 + _dot(qb, hd["state"].astype(jnp.bfloat16)) * hd["cross_decay"])
            kd_t = (k * hd["state_decay"]).T.astype(jnp.bfloat16)
            hd["state"] = hd["state"] * hd["chunk_decay"] + _dot(kd_t, v)
            ms = jnp.mean(o * o, axis=-1, keepdims=True)
            y = (o * lax.rsqrt(ms + EPS)) * hd["gain"]
            gate = rg_ref[0, rows, cols].astype(jnp.float32)
            o_ref[0, rows, cols] = ((gate * (1.0 / (1.0 + jnp.exp(-gate)))) * y).astype(o_ref.dtype)


def _retention(p3, log_gamma, cos2, sin2, ret_g, *, rc):
    b, seq, _ = p3.shape
    hps = RET_HEADS_PER_STEP
    width = hps * RET_DK
    assert seq % rc == 0 and RET_HEADS % hps == 0

    def head_spec(off):
        assert off % width == 0
        return pl.BlockSpec((1, seq, width), lambda bi, h: (bi, 0, off // width + h))

    return pl.pallas_call(
        functools.partial(_ret_kernel, rc=rc),
        out_shape=jax.ShapeDtypeStruct((b, seq, RET_WIDTH), jnp.bfloat16),
        grid=(b, RET_HEADS // hps),
        in_specs=[
            pl.BlockSpec(memory_space=pltpu.SMEM),
            head_spec(RQ_OFF), head_spec(RK_OFF), head_spec(RV_OFF), head_spec(RG_OFF),
            pl.BlockSpec((seq, RET_DK), lambda bi, h: (0, 0)),
            pl.BlockSpec((seq, RET_DK), lambda bi, h: (0, 0)),
            pl.BlockSpec((hps, 1, RET_DV), lambda bi, h: (h, 0, 0)),
        ],
        out_specs=pl.BlockSpec((1, seq, width), lambda bi, h: (bi, 0, h)),
        compiler_params=_params("arbitrary", "arbitrary"),
        name="retention",
    )(log_gamma, p3, p3, p3, p3, cos2, sin2, ret_g)


def _routing(logits):
    lane = lax.broadcasted_iota(jnp.int32, logits.shape, 1).astype(jnp.float32)
    big = float(LANES)
    neg = -jnp.inf

    def first_argmax(v, vmax):
        return jnp.min(jnp.where(v == vmax, lane, big), axis=-1, keepdims=True)

    g_mask = (lane >= N_EXPERTS) & (lane < N_EXPERTS + N_GROUPS)
    gl = jnp.where(g_mask, logits, neg)
    g_max = jnp.max(gl, axis=-1, keepdims=True)
    g_sel = first_argmax(gl, g_max) - N_EXPERTS
    g_gate = 1.0 / jnp.sum(jnp.where(g_mask, jnp.exp(gl - g_max), 0.0), axis=-1, keepdims=True)

    e_lo = g_sel * EXPERTS_PER_GROUP
    el = jnp.where((lane >= e_lo) & (lane < e_lo + EXPERTS_PER_GROUP), logits, neg)
    v1 = jnp.max(el, axis=-1, keepdims=True)
    i1 = first_argmax(el, v1)
    el2 = jnp.where(lane == i1, neg, el)
    v2 = jnp.max(el2, axis=-1, keepdims=True)
    i2 = first_argmax(el2, v2)
    e2 = jnp.exp(v2 - v1)
    denom = 1.0 + e2
    w1 = (1.0 / denom) * g_gate
    w2 = (e2 / denom) * g_gate
    route = jnp.where(lane == ROUTE_E1, i1, 0.0) + jnp.where(lane == ROUTE_E2, i2, 0.0)
    return route + jnp.where(lane == ROUTE_W1, w1, 0.0) + jnp.where(lane == ROUTE_W2, w2, 0.0)


def _norm2(x1, g):
    ms = jnp.mean(x1 * x1, axis=-1, keepdims=True)
    return (x1 * lax.rsqrt(ms + EPS)) * g


OUT_SUBTILES = 2


def _out_proj_kernel(a_ref, r_ref, x_ref, wa_ref, wr_ref, g_ref, rcat_ref,
                     x1_ref, h2_ref, route_ref):
    sub = x_ref.shape[0] // OUT_SUBTILES
    for s in range(OUT_SUBTILES):
        rows = slice(s * sub, (s + 1) * sub)
        mixed = _dot(a_ref[rows, :], wa_ref[...]) + _dot(r_ref[rows, :], wr_ref[...])
        x1 = x_ref[rows, :] + mixed
        x1_ref[rows, :] = x1
        h2 = _norm2(x1, g_ref[...])
        h2_ref[rows, :] = h2
        hi = h2.astype(jnp.bfloat16)
        lo = (h2 - hi.astype(jnp.float32)).astype(jnp.bfloat16)
        both = _dot(hi, rcat_ref[...])
        logits = both[:, :LANES] + (both[:, LANES:] + _dot(lo, rcat_ref[:, :LANES]))
        route_ref[rows, :] = _routing(logits)


def _out_proj(attn2d, ret2d, x2d, w_out_bf, g2, r_cat, *, tm):
    n, d = x2d.shape
    return pl.pallas_call(
        _out_proj_kernel,
        out_shape=(
            jax.ShapeDtypeStruct((n, d), jnp.float32),
            jax.ShapeDtypeStruct((n, d), jnp.float32),
            jax.ShapeDtypeStruct((n, LANES), jnp.float32),
        ),
        grid=(n // tm,),
        in_specs=[
            pl.BlockSpec((tm, ATTN_WIDTH), lambda i: (i, 0)),
            pl.BlockSpec((tm, RET_WIDTH), lambda i: (i, 0)),
            pl.BlockSpec((tm, d), lambda i: (i, 0)),
            pl.BlockSpec((ATTN_WIDTH, d), lambda i: (0, 0)),
            pl.BlockSpec((RET_WIDTH, d), lambda i: (ATTN_WIDTH // RET_WIDTH, 0)),
            pl.BlockSpec((1, d), lambda i: (0, 0)),
            pl.BlockSpec((d, 2 * LANES), lambda i: (0, 0)),
        ],
        out_specs=(
            pl.BlockSpec((tm, d), lambda i: (i, 0)),
            pl.BlockSpec((tm, d), lambda i: (i, 0)),
            pl.BlockSpec((tm, LANES), lambda i: (i, 0)),
        ),
        compiler_params=_params("arbitrary"),
        name="out_proj",
    )(attn2d, ret2d, x2d, w_out_bf, w_out_bf, g2, r_cat)


def _plan_kernel(route_ref, pos_ref, cnt_ref, rank_scr, *, tm, blk):
    n = route_ref.shape[0]
    lane = lax.broadcasted_iota(jnp.int32, (blk, LANES), 1).astype(jnp.float32)
    before = (lax.broadcasted_iota(jnp.int32, (blk, blk), 1)
              < lax.broadcasted_iota(jnp.int32, (blk, blk), 0)).astype(jnp.bfloat16)

    def one_hot(rows):
        r = route_ref[rows, :]
        e1 = r[:, ROUTE_E1:ROUTE_E1 + 1]
        e2 = r[:, ROUTE_E2:ROUTE_E2 + 1]
        return lane == e1, lane == e2

    def rank_body(b, run):
        rows = pl.ds(pl.multiple_of(b * blk, blk), blk)
        m1, m2 = one_hot(rows)
        sel = jnp.where(m1 | m2, 1.0, 0.0)
        rank_scr[rows, :] = _dot(before, sel.astype(jnp.bfloat16)) + run
        return run + jnp.sum(sel, axis=0, keepdims=True)
    cnt = lax.fori_loop(0, n // blk, rank_body, jnp.zeros((1, LANES), jnp.float32), unroll=2)
    cnt_ref[...] = jnp.broadcast_to(cnt, cnt_ref.shape)

    tiles = jnp.floor((cnt + (tm - 1.0)) * (1.0 / tm))
    below = (lax.broadcasted_iota(jnp.int32, (LANES, LANES), 0)
             < lax.broadcasted_iota(jnp.int32, (LANES, LANES), 1)).astype(jnp.bfloat16)
    start = _dot(jnp.broadcast_to(tiles, (8, LANES)).astype(jnp.bfloat16), below)[0:1, :] * float(tm)

    def pos_body(b, carry):
        rows = pl.ds(pl.multiple_of(b * blk, blk), blk)
        m1, m2 = one_hot(rows)
        dest = rank_scr[rows, :] + start
        p1 = jnp.sum(jnp.where(m1, dest, 0.0), axis=-1, keepdims=True)
        p2 = jnp.sum(jnp.where(m2, dest, 0.0), axis=-1, keepdims=True)
        pos_ref[rows, :] = (jnp.where(lane == 0.0, p1, 0.0) + jnp.where(lane == 1.0, p2, 0.0)).astype(jnp.int32)
        return carry
    lax.fori_loop(0, n // blk, pos_body, 0, unroll=2)


def _moe_plan(route, *, tm, blk):
    n = route.shape[0]
    return pl.pallas_call(
        functools.partial(_plan_kernel, tm=tm, blk=blk),
        out_shape=(jax.ShapeDtypeStruct((n, LANES), jnp.int32),
                   jax.ShapeDtypeStruct((8, LANES), jnp.float32)),
        scratch_shapes=[pltpu.VMEM((n, LANES), jnp.float32)],
        compiler_params=pltpu.CompilerParams(vmem_limit_bytes=VMEM_LIMIT),
        name="moe_plan",
    )(route)


def _row_copy(src, src_row, dst, dst_row, sem):
    return pltpu.make_async_copy(src.at[pl.ds(src_row, 1), :], dst.at[pl.ds(dst_row, 1), :], sem)


def _invert_kernel(pos_ref, fill_ref, sid_ref, sem):
    fill = pltpu.make_async_copy(fill_ref, sid_ref, sem)
    fill.start()
    fill.wait()

    def body(j, carry):
        sid_ref[pos_ref[j]] = j
        return carry
    lax.fori_loop(0, pos_ref.shape[0], body, 0, unroll=8)


def _moe_invert(pos, *, rows):
    return pl.pallas_call(
        _invert_kernel,
        out_shape=jax.ShapeDtypeStruct((rows,), jnp.int32),
        in_specs=[pl.BlockSpec(memory_space=pltpu.SMEM), pl.BlockSpec(memory_space=pl.ANY)],
        out_specs=pl.BlockSpec(memory_space=pltpu.SMEM),
        scratch_shapes=[pltpu.SemaphoreType.DMA(())],
        name="moe_invert",
    )(pos, jnp.zeros((rows,), jnp.int32))


ROW_GROUP = 8
ROW_BLOCK = 32
FFN_ROW_STEP = 128
NO_NEXT, NOT_FIRST = -1, -2


def _ffn_kernel(te_ref, tv_ref, nu_ref, seg_ref, nexte_ref, sid_ref, tok_ref,
                h2_ref, w1_ref, w3_ref, w2_ref, y2_ref,
                xbuf, ybuf, w1buf, w3buf, w2buf, gsem, ssem, wsem):
    t = pl.program_id(0)
    n_used = nu_ref[0]
    tm = xbuf.shape[1]

    def weight_copies(e, wslot):
        return [pltpu.make_async_copy(src.at[e], dst.at[wslot], wsem.at[wslot])
                for src, dst in ((w1_ref, w1buf), (w3_ref, w3buf), (w2_ref, w2buf))]

    def gather(slot):
        return tok_ref, lambda r, tok: _row_copy(h2_ref, tok, xbuf.at[slot], r, gsem.at[slot])

    def scatter(slot):
        return sid_ref, lambda r, sid: _row_copy(ybuf.at[slot], r, y2_ref, sid, ssem.at[slot])

    def start_all(tile, table, copy):
        valid = tv_ref[tile]
        for blk in range(tm // ROW_BLOCK):
            @pl.when(valid >= (blk + 1) * ROW_BLOCK)
            def _():
                for r in range(blk * ROW_BLOCK, (blk + 1) * ROW_BLOCK):
                    copy(r, table[tile * tm + r]).start()
        done = (valid // ROW_BLOCK) * ROW_BLOCK

        def body(c, carry):
            for u in range(ROW_GROUP):
                r = done + c * ROW_GROUP + u

                @pl.when(r < valid)
                def _():
                    copy(r, table[tile * tm + r]).start()
            return carry
        lax.fori_loop(0, (valid - done + (ROW_GROUP - 1)) // ROW_GROUP, body, 0)

    def wait_all(tile, copy, block_copy):
        valid = tv_ref[tile]
        for blk in range(tm // ROW_BLOCK):
            @pl.when(valid >= (blk + 1) * ROW_BLOCK)
            def _():
                block_copy.wait()

        def body(r, carry):
            copy(0, 0).wait()
            return carry
        lax.fori_loop(0, valid % ROW_BLOCK, body, 0)

    def wait_gather(tile, slot):
        wait_all(tile, gather(slot)[1],
                 pltpu.make_async_copy(h2_ref.at[pl.ds(0, ROW_BLOCK), :],
                                       xbuf.at[slot, pl.ds(0, ROW_BLOCK), :], gsem.at[slot]))

    def wait_scatter(tile, slot):
        wait_all(tile, scatter(slot)[1],
                 pltpu.make_async_copy(ybuf.at[slot, pl.ds(0, ROW_BLOCK), :],
                                       y2_ref.at[pl.ds(0, ROW_BLOCK), :], ssem.at[slot]))

    @pl.when(t == 0)
    def _():
        for cp in weight_copies(te_ref[0], 0):
            cp.start()
        start_all(0, *gather(0))

    @pl.when(t < n_used)
    def _():
        slot = t % 2
        wslot = seg_ref[t] % 2

        @pl.when(t + 1 < n_used)
        def _():
            start_all(t + 1, *gather(1 - slot))

        @pl.when(nexte_ref[t] != NOT_FIRST)
        def _():
            for cp in weight_copies(te_ref[t], wslot):
                cp.wait()

            @pl.when(nexte_ref[t] >= 0)
            def _():
                for cp in weight_copies(nexte_ref[t], 1 - wslot):
                    cp.start()

        wait_gather(t, slot)

        @pl.when(t >= 2)
        def _():
            wait_scatter(t - 2, slot)

        valid = tv_ref[t]
        for live in range(FFN_ROW_STEP, tm + 1, FFN_ROW_STEP):
            @pl.when((valid > live - FFN_ROW_STEP) & (valid <= live))
            def _():
                row = lax.broadcasted_iota(jnp.int32, (live, xbuf.shape[2]), 0)
                x = jnp.where(row < valid, xbuf[slot, :live, :], 0.0).astype(jnp.bfloat16)
                a = _dot(x, w1buf[wslot].astype(jnp.bfloat16))
                b = _dot(x, w3buf[wslot].astype(jnp.bfloat16))
                act = (a * (1.0 / (1.0 + jnp.exp(-a)))) * b
                ybuf[slot, :live, :] = _dot(act.astype(jnp.bfloat16), w2buf[wslot].astype(jnp.bfloat16))
        start_all(t, *scatter(slot))

    @pl.when(t == pl.num_programs(0) - 1)
    def _():
        @pl.when(n_used >= 2)
        def _():
            wait_scatter(n_used - 2, n_used % 2)
        wait_scatter(n_used - 1, (n_used - 1) % 2)


def _moe_ffn(tile_expert, tile_valid, n_used, tile_seg, tile_next, row_sid, row_tok, h2, w1, w3, w2,
             *, tm):
    n, d = h2.shape
    _, _, ff = w1.shape
    max_tiles = tile_expert.shape[0]
    any_spec = pl.BlockSpec(memory_space=pl.ANY)
    return pl.pallas_call(
        _ffn_kernel,
        out_shape=jax.ShapeDtypeStruct((2 * n, d), jnp.float32),
        grid_spec=pltpu.PrefetchScalarGridSpec(
            num_scalar_prefetch=7,
            grid=(max_tiles,),
            in_specs=[any_spec, any_spec, any_spec, any_spec],
            out_specs=any_spec,
            scratch_shapes=[
                pltpu.VMEM((2, tm, d), jnp.float32),
                pltpu.VMEM((2, tm, d), jnp.float32),
                pltpu.VMEM((2, d, ff), jnp.float32),
                pltpu.VMEM((2, d, ff), jnp.float32),
                pltpu.VMEM((2, ff, d), jnp.float32),
                pltpu.SemaphoreType.DMA((2,)),
                pltpu.SemaphoreType.DMA((2,)),
                pltpu.SemaphoreType.DMA((2,)),
            ],
        ),
        compiler_params=_params("arbitrary"),
        name="moe_ffn",
    )(tile_expert, tile_valid, n_used, tile_seg, tile_next, row_sid, row_tok, h2, w1, w3, w2)


def _combine_kernel(x1_ref, route_ref, ya_ref, yb_ref, o_ref):
    route = route_ref[...]
    w1 = route[:, ROUTE_W1:ROUTE_W1 + 1]
    w2 = route[:, ROUTE_W2:ROUTE_W2 + 1]
    o_ref[...] = x1_ref[...] + (w1 * ya_ref[...] + w2 * yb_ref[...])


def _moe_combine(x1, route, y2, *, tm):
    n, d = x1.shape
    return pl.pallas_call(
        _combine_kernel,
        out_shape=jax.ShapeDtypeStruct((n, d), jnp.float32),
        grid=(n // tm,),
        in_specs=[pl.BlockSpec((tm, d), lambda i: (i, 0)),
                  pl.BlockSpec((tm, LANES), lambda i: (i, 0)),
                  pl.BlockSpec((tm, d), lambda i: (i, 0)),
                  pl.BlockSpec((tm, d), lambda i: (i + n // tm, 0))],
        out_specs=pl.BlockSpec((tm, d), lambda i: (i, 0)),
        compiler_params=_params("arbitrary"),
        name="moe_combine",
    )(x1, route, y2, y2)


def _moe(x1, h2, route, w1, w3, w2, *, tm, gather_tm):
    n, d = x1.shape
    max_tiles = (2 * n) // tm + N_EXPERTS
    pos2d, cnt = _moe_plan(route, tm=tm, blk=gather_tm)
    pos = pos2d[:, :2].T.reshape(2 * n)
    counts = cnt[0, :N_EXPERTS].astype(jnp.int32)
    tiles = (counts + (tm - 1)) // tm
    ends = jnp.cumsum(tiles)
    t_idx = jnp.arange(max_tiles, dtype=jnp.int32)
    tile_expert = jnp.sum((ends[None, :] <= t_idx[:, None]).astype(jnp.int32), axis=1)
    tile_expert = jnp.minimum(tile_expert, N_EXPERTS - 1)
    first_tile = (ends - tiles)[tile_expert]
    tile_valid = jnp.clip(counts[tile_expert] - (t_idx - first_tile) * tm, 0, tm).astype(jnp.int32)
    n_used = ends[-1:].astype(jnp.int32)
    used = t_idx < n_used[0]
    is_first = used & (t_idx == first_tile)
    tile_seg = (jnp.cumsum(is_first.astype(jnp.int32)) - 1).astype(jnp.int32)
    next_start = first_tile + tiles[tile_expert]
    next_e = jnp.where(next_start < n_used[0], tile_expert[jnp.minimum(next_start, max_tiles - 1)], NO_NEXT)
    tile_next = jnp.where(is_first, next_e, NOT_FIRST).astype(jnp.int32)

    row_sid = _moe_invert(pos, rows=max_tiles * tm)
    row_tok = jnp.where(row_sid >= n, row_sid - n, row_sid)
    y2 = _moe_ffn(tile_expert, tile_valid, n_used, tile_seg, tile_next, row_sid, row_tok, h2, w1, w3, w2,
                  tm=tm)
    return _moe_combine(x1, route, y2, tm=tm)


def _pack_router(w_group, w_router):
    d = w_group.shape[0]
    experts = jnp.transpose(w_router, (1, 0, 2)).reshape(d, N_EXPERTS)
    wr = jnp.concatenate(
        [experts, w_group, jnp.zeros((d, LANES - N_EXPERTS - N_GROUPS), w_group.dtype)], axis=1)
    hi = wr.astype(jnp.bfloat16)
    lo = (wr - hi.astype(jnp.float32)).astype(jnp.bfloat16)
    return jnp.concatenate([hi, lo], axis=1)


def _rotation_tables(seq):
    half = RET_DK // 2
    pos = jnp.arange(seq, dtype=jnp.float32)
    inv = 1.0 / (ROT_BASE ** jnp.linspace(0.0, 1.0, half, dtype=jnp.float32))
    ang = pos[:, None] * inv[None, :]
    c, s = jnp.cos(ang), jnp.sin(ang)
    return jnp.concatenate([c, c], axis=-1), jnp.concatenate([-s, s], axis=-1)


def _tiles(n, seq):
    def fit(total, want):
        t = min(total, want)
        while total % t:
            t //= 2
        return t
    return dict(
        proj_tm=fit(n, 1024),
        attn_tq=fit(seq, 256), attn_tk=fit(seq, 256),
        ret_rc=fit(seq, 256),
        out_tm=fit(n, 512),
        moe_tm=fit(n, 512), moe_gather_tm=fit(n, 256),
    )


def kernel(x, norm1_g, w_in, q_norm_g, k_norm_g, idx_k_ln_w, idx_k_ln_b, ret_norm_g,
           w_out, norm2_g, w_group, w_router, w1, w3, w2):
    b, seq, d = x.shape
    n = b * seq
    depth = w_in.shape[0]
    t = _tiles(n, seq)
    cos2, sin2 = _rotation_tables(seq)
    log_gamma = jnp.log1p(-jnp.exp2(-5.0 - jnp.arange(RET_HEADS, dtype=jnp.float32)))

    x2d = x.reshape(n, d)
    for l in range(depth):
        proj = _in_proj(x2d, norm1_g[l][None, :], w_in[l].T, tm=t["proj_tm"])
        p3 = proj.reshape(b, seq, proj.shape[1])
        attn = _dsa_attention(p3, q_norm_g[l][None, :], k_norm_g[l][None, :],
                              idx_k_ln_w[l][None, :], idx_k_ln_b[l][None, :],
                              tq=t["attn_tq"], tk=t["attn_tk"])
        ret = _retention(p3, log_gamma, cos2, sin2, ret_norm_g[l].reshape(RET_HEADS, 1, RET_DV),
                         rc=t["ret_rc"])
        r_cat = _pack_router(w_group[l], w_router[l])
        g2 = norm2_g[l][None, :]
        x1, h2, route = _out_proj(attn.reshape(n, ATTN_WIDTH), ret.reshape(n, RET_WIDTH), x2d,
                                  w_out[l].astype(jnp.bfloat16), g2, r_cat, tm=t["out_tm"])
        x2d = _moe(x1, h2, route, w1[l], w3[l], w2[l], tm=t["moe_tm"], gather_tm=t["moe_gather_tm"])
    return x2d.reshape(b, seq, d)
```
